```python
import math
import jax, jax.numpy as jnp
from jax import lax
import numpy as np

D_MODEL = 1024
BATCH = 8
SEQ = 4096
DEPTH = 2
DEC_BATCH = 8
DEC_SEQ = 16
PAST_LEN = 1024

CHUNK = 64
QBLK = 128
EPS = 1e-6
NEG = -1e30
H_A = 4
DH_A = 64
DV_A = 2 * DH_A
H_B = 8
DH_B = 64
BAND_CHUNKS = 8
REL_CLIP = 128
H_C = 4
DH_C = 128
D_C = H_C * DH_C
CONV_W = 4
N_MEM = 256
H_M = 4
DH_M = D_MODEL // H_M
D_FF = -(-8 * D_MODEL // (3 * 256)) * 256
IN_SIZES = (H_A * DH_A,) * 4 + (H_A * DV_A,) + (H_B * DH_B,) * 3 + (2 * D_C, D_C, D_C, 2 * H_C, 3 * D_MODEL)
N_IN = sum(IN_SIZES)

kernel_name = 'hybrid_streaming_encoder_step'


def rmsnorm(x, g):
    xf = x.astype(jnp.float32)
    y = xf * lax.rsqrt(jnp.mean(xf * xf, axis=-1, keepdims=True) + EPS)
    return (y * g.astype(jnp.float32)).astype(x.dtype)


def alibi_slopes():
    return 2.0 ** (-8.0 * jnp.arange(1, H_A + 1, dtype=jnp.float32) / H_A)


def diff_core(q1, q2, k1, k2, v, qpos, kpos, lam):
    scale = DH_A ** -0.5
    dist = jnp.abs(qpos[:, None] - kpos[None, :]).astype(jnp.float32)
    bias = -alibi_slopes()[:, None, None] * dist[None]
    allowed = (kpos[None, :] // CHUNK) <= (qpos[:, None] // CHUNK)

    def probs(q, k):
        s = jnp.einsum('bqhd,bkhd->bhqk', q, k).astype(jnp.float32) * scale + bias
        return jax.nn.softmax(jnp.where(allowed, s, NEG), axis=-1)

    p = probs(q1, k1) - lam * probs(q2, k2)
    return jnp.einsum('bhqk,bkhe->bqhe', p.astype(v.dtype), v)


def diff_prompt(q1, q2, k1, k2, v, lam):
    B, T = q1.shape[:2]
    nb = T // QBLK
    kpos = jnp.arange(T)
    blocks = lambda a: a.reshape((B, nb, QBLK) + a.shape[2:]).swapaxes(0, 1)

    def one(args):
        i, qa, qb = args
        qpos = i * QBLK + jnp.arange(QBLK)
        return diff_core(qa, qb, k1, k2, v, qpos, kpos, lam)

    o = lax.map(one, (jnp.arange(nb), blocks(q1), blocks(q2)))
    return o.swapaxes(0, 1).reshape((B, T) + o.shape[3:])


def band_core(q, k, v, qpos, kpos, table):
    rel = jnp.clip(qpos[:, None] - kpos[None, :], -REL_CLIP, REL_CLIP) + REL_CLIP
    bias = table[:, rel].astype(jnp.float32)
    qc = qpos[:, None] // CHUNK
    kc = kpos[None, :] // CHUNK
    allowed = (kpos[None, :] >= 0) & (kc <= qc) & (kc >= qc - BAND_CHUNKS)
    s = jnp.einsum('bqhd,bkhd->bhqk', q, k).astype(jnp.float32) * DH_B ** -0.5 + bias
    p = jax.nn.softmax(jnp.where(allowed, s, NEG), axis=-1)
    return jnp.einsum('bhqk,bkhd->bqhd', p.astype(v.dtype), v)


def band_prompt(q, k, v, table):
    B, T, H, d = q.shape
    nc = T // CHUNK
    pad = BAND_CHUNKS * CHUNK
    kp = jnp.pad(k, ((0, 0), (pad, 0), (0, 0), (0, 0)))
    vp = jnp.pad(v, ((0, 0), (pad, 0), (0, 0), (0, 0)))
    qr = q.reshape(B, nc, CHUNK, H, d).swapaxes(0, 1)

    def one(args):
        c, qc = args
        kb = lax.dynamic_slice_in_dim(kp, c * CHUNK, pad + CHUNK, axis=1)
        vb = lax.dynamic_slice_in_dim(vp, c * CHUNK, pad + CHUNK, axis=1)
        qpos = c * CHUNK + jnp.arange(CHUNK)
        kpos = c * CHUNK - pad + jnp.arange(pad + CHUNK)
        return band_core(qc, kb, vb, qpos, kpos, table)

    o = lax.map(one, (jnp.arange(nc), qr))
    return o.swapaxes(0, 1).reshape(B, T, H, d)


def mlstm_chunk(carry, inp):
    Cs, ns, ms = carry
    q, k, v, li, lf = inp
    L = q.shape[1]
    b = jnp.cumsum(lf, axis=1)
    causal = jnp.tril(jnp.ones((L, L), bool))[None, :, :, None]
    dmat = jnp.where(causal, b[:, :, None, :] - b[:, None, :, :] + li[:, None, :, :], NEG)
    inter = b + ms[:, None, :]
    m = jnp.maximum(inter, dmat.max(axis=2))
    w_intra = jnp.exp(dmat - m[:, :, None, :])
    w_inter = jnp.exp(inter - m)
    a = w_intra * jnp.einsum('bthd,bshd->btsh', q, k)
    num = jnp.einsum('btsh,bshe->bthe', a, v) + w_inter[..., None] * jnp.einsum('bhed,bthd->bthe', Cs, q)
    den = a.sum(axis=2) + w_inter * jnp.einsum('bhd,bthd->bth', ns, q)
    h = num / jnp.maximum(jnp.abs(den), jnp.exp(-m))[..., None]
    b_last = b[:, -1]
    g = b_last[:, None, :] - b + li
    m_new = jnp.maximum(b_last + ms, g.max(axis=1))
    ws = jnp.exp(g - m_new[:, None, :])
    decay = jnp.exp(b_last + ms - m_new)
    C_new = decay[..., None, None] * Cs + jnp.einsum('bsh,bshe,bshd->bhed', ws, v, k)
    n_new = decay[..., None] * ns + jnp.einsum('bsh,bshd->bhd', ws, k)
    return (C_new, n_new, m_new), h


def mlstm_branch(c_qk, c_v, c_o, c_if, conv_buf, state, p, prompt):
    B, T, _ = c_qk.shape
    f32 = jnp.float32
    xp = jnp.concatenate([conv_buf.astype(c_qk.dtype), c_qk], axis=1)
    u = p['conv_b'] + sum(xp[:, j:j + T] * p['conv_w'][j] for j in range(CONV_W))
    u = jax.nn.silu(u).astype(f32)
    q, k = jnp.split(u, 2, axis=-1)
    q = q.reshape(B, T, H_C, DH_C)
    k = k.reshape(B, T, H_C, DH_C) * DH_C ** -0.5
    v = c_v.astype(f32).reshape(B, T, H_C, DH_C)
    li = (c_if[..., :H_C] + p['b_i']).astype(f32)
    lf = jax.nn.log_sigmoid((c_if[..., H_C:] + p['b_f']).astype(f32))
    state = tuple(s.astype(f32) for s in state)
    if prompt:
        nc = T // CHUNK
        blk = lambda a: a.reshape((B, nc, CHUNK) + a.shape[2:]).swapaxes(0, 1)
        state, hs = lax.scan(mlstm_chunk, state, (blk(q), blk(k), blk(v), blk(li), blk(lf)))
        hs = hs.swapaxes(0, 1).reshape(B, T, H_C, DH_C)
    else:
        state, hs = mlstm_chunk(state, (q, k, v, li, lf))
    hs = rmsnorm(hs, p['c_head_g']) * jax.nn.sigmoid(c_o.astype(f32)).reshape(B, T, H_C, DH_C)
    return hs.reshape(B, T, D_C).astype(c_qk.dtype), state, xp[:, -(CONV_W - 1):]


def cross_attn(h, mem_k, mem_v, p):
    B, T, _ = h.shape
    q = (h @ p['w_mq']).reshape(B, T, H_M, DH_M)
    s = jnp.einsum('bthd,bmhd->bhtm', q, mem_k).astype(jnp.float32) * DH_M ** -0.5
    a = jax.nn.softmax(s, axis=-1).astype(mem_v.dtype)
    o = jnp.einsum('bhtm,bmhd->bthd', a, mem_v).reshape(B, T, D_MODEL)
    return o @ p['w_mo']


def trunk_layer(x, p, l, mem_k, mem_v, cache):
    B, T, _ = x.shape
    f32 = jnp.float32
    h = rmsnorm(x, p['g_mix'])
    idx = np.cumsum(IN_SIZES)[:-1].tolist()
    (a_q1, a_q2, a_k1, a_k2, a_v, b_q, b_k, b_v, c_qk, c_v, c_o, c_if, g) = jnp.split(h @ p['w_in'], idx, axis=-1)
    a_q1, a_q2, a_k1, a_k2, a_v = (t.reshape(B, T, H_A, -1) for t in (a_q1, a_q2, a_k1, a_k2, a_v))
    b_q, b_k, b_v = (t.reshape(B, T, H_B, DH_B) for t in (b_q, b_k, b_v))
    lam_init = 0.8 - 0.6 * math.exp(-0.3 * l)
    lam = (jnp.exp(jnp.sum(p['lq1'] * p['lk1'])) - jnp.exp(jnp.sum(p['lq2'] * p['lk2'])) + lam_init).astype(f32)
    a_k = jnp.concatenate([a_k1, a_k2], axis=-1)
    if cache is None:
        oa = diff_prompt(a_q1, a_q2, a_k1, a_k2, a_v, lam)
        ob = band_prompt(b_q, b_k, b_v, p['rel'])
        conv_buf = jnp.zeros((B, CONV_W - 1, 2 * D_C), x.dtype)
        st0 = (jnp.zeros((B, H_C, DH_C, DH_C), f32), jnp.zeros((B, H_C, DH_C), f32), jnp.zeros((B, H_C), f32))
        keep = min(BAND_CHUNKS * CHUNK, T)
        new_b = (b_k[:, T - keep:], b_v[:, T - keep:])
    else:
        past = cache['a_k'].shape[1]
        qpos = past + jnp.arange(T)
        ka = jnp.concatenate([cache['a_k'].astype(a_k.dtype), a_k], axis=1)
        va = jnp.concatenate([cache['a_v'].astype(a_v.dtype), a_v], axis=1)
        oa = diff_core(a_q1, a_q2, ka[..., :DH_A], ka[..., DH_A:], va, qpos, jnp.arange(past + T), lam)
        nband = cache['b_k'].shape[1]
        kb = jnp.concatenate([cache['b_k'].astype(b_k.dtype), b_k], axis=1)
        vb = jnp.concatenate([cache['b_v'].astype(b_v.dtype), b_v], axis=1)
        ob = band_core(b_q, kb, vb, qpos, jnp.arange(past - nband, past + T), p['rel'])
        conv_buf = cache['conv']
        st0 = (cache['C'], cache['n'], cache['m'])
        new_b = (b_k, b_v)
    oc, (C, n, m), conv_new = mlstm_branch(c_qk, c_v, c_o, c_if, conv_buf, st0, p, cache is None)
    oa = rmsnorm(oa, p['a_head_g']) * (1.0 - lam_init)
    g_a, g_b, g_c = jnp.split(jax.nn.sigmoid(g), 3, axis=-1)
    mixed = (g_a * (oa.reshape(B, T, -1) @ p['w_up_a'])
             + g_b * (ob.reshape(B, T, -1) @ p['w_up_b'])
             + g_c * (oc @ p['w_up_c']))
    x = x + mixed @ p['w_o']
    x = x + cross_attn(rmsnorm(x, p['g_cross']), mem_k, mem_v, p)
    hf = rmsnorm(x, p['g_ffn'])
    x = x + (jax.nn.silu(hf @ p['w_ff_g']) * (hf @ p['w_ff_u'])) @ p['w_ff_d']
    return x, (a_k, a_v) + new_b + (C, n, m, conv_new)


def setup_inputs(seed: int = 0) -> dict:
    key = jax.random.key(seed)
    ks = list(jax.random.split(key, 64))
    f32 = jnp.float32

    def nrm(shape, scale=1.0):
        return scale * jax.random.normal(ks.pop(), shape, f32)

    def proj(fan_in, fan_out):
        return nrm((DEPTH, fan_in, fan_out), fan_in ** -0.5)

    def gain(*shape):
        return 1.0 + nrm(shape, 0.02)

    band_past = min(BAND_CHUNKS * CHUNK, PAST_LEN)
    D = D_MODEL
    return {
        'x_prompt': nrm((BATCH, SEQ, D)),
        'x_sample': nrm((DEC_BATCH, DEC_SEQ, D)),
        'cache_a_k': nrm((DEPTH, DEC_BATCH, PAST_LEN, H_A, 2 * DH_A)),
        'cache_a_v': nrm((DEPTH, DEC_BATCH, PAST_LEN, H_A, DV_A)),
        'cache_b_k': nrm((DEPTH, DEC_BATCH, band_past, H_B, DH_B)),
        'cache_b_v': nrm((DEPTH, DEC_BATCH, band_past, H_B, DH_B)),
        'state_c_C': nrm((DEPTH, DEC_BATCH, H_C, DH_C, DH_C), 0.5),
        'state_c_n': nrm((DEPTH, DEC_BATCH, H_C, DH_C), 0.5),
        'state_c_m': nrm((DEPTH, DEC_BATCH, H_C)),
        'state_c_conv': nrm((DEPTH, DEC_BATCH, CONV_W - 1, 2 * D_C)),
        'cache_mem_k': nrm((DEPTH, DEC_BATCH, N_MEM, H_M, DH_M)),
        'cache_mem_v': nrm((DEPTH, DEC_BATCH, N_MEM, H_M, DH_M)),
        'mem_prompt': nrm((BATCH, N_MEM, D)),
        'g_mix': gain(DEPTH, D),
        'w_in': proj(D, N_IN),
        'a_lq1': nrm((DEPTH, DH_A), 0.1),
        'a_lk1': nrm((DEPTH, DH_A), 0.1),
        'a_lq2': nrm((DEPTH, DH_A), 0.1),
        'a_lk2': nrm((DEPTH, DH_A), 0.1),
        'a_head_g': gain(DEPTH, DV_A),
        'b_rel': nrm((DEPTH, H_B, 2 * REL_CLIP + 1), 0.5),
        'c_conv_w': nrm((DEPTH, CONV_W, 2 * D_C), CONV_W ** -0.5),
        'c_conv_b': nrm((DEPTH, 2 * D_C), 0.02),
        'c_b_i': nrm((DEPTH, H_C), 0.1),
        'c_b_f': jnp.linspace(3.0, 6.0, H_C, dtype=f32) + nrm((DEPTH, H_C), 0.1),
        'c_head_g': gain(DEPTH, DH_C),
        'w_up_a': proj(H_A * DV_A, D),
        'w_up_b': proj(H_B * DH_B, D),
        'w_up_c': proj(D_C, D),
        'w_o': proj(D, D),
        'g_cross': gain(DEPTH, D),
        'w_mq': proj(D, D),
        'w_mk': proj(D, D),
        'w_mv': proj(D, D),
        'w_mo': proj(D, D),
        'g_ffn': gain(DEPTH, D),
        'w_ff_g': proj(D, D_FF),
        'w_ff_u': proj(D, D_FF),
        'w_ff_d': proj(D_FF, D),
        'g_final': 1.0 + nrm((D,), 0.02),
    }


def reference(x_prompt, x_sample, cache_a_k, cache_a_v, cache_b_k, cache_b_v, state_c_C, state_c_n, state_c_m,
              state_c_conv, cache_mem_k, cache_mem_v, mem_prompt, g_mix, w_in, a_lq1, a_lk1, a_lq2, a_lk2,
              a_head_g, b_rel, c_conv_w, c_conv_b, c_b_i, c_b_f, c_head_g, w_up_a, w_up_b, w_up_c, w_o,
              g_cross, w_mq, w_mk, w_mv, w_mo, g_ffn, w_ff_g, w_ff_u, w_ff_d, g_final):
    xp, xs = x_prompt, x_sample
    Bp = x_prompt.shape[0]
    new_p = [[] for _ in range(10)]
    new_s = [[] for _ in range(8)]
    for l in range(DEPTH):
        p = dict(g_mix=g_mix[l], w_in=w_in[l], lq1=a_lq1[l], lk1=a_lk1[l], lq2=a_lq2[l], lk2=a_lk2[l],
                 a_head_g=a_head_g[l], rel=b_rel[l], conv_w=c_conv_w[l], conv_b=c_conv_b[l], b_i=c_b_i[l],
                 b_f=c_b_f[l], c_head_g=c_head_g[l], w_up_a=w_up_a[l], w_up_b=w_up_b[l], w_up_c=w_up_c[l],
                 w_o=w_o[l], g_cross=g_cross[l], w_mq=w_mq[l], w_mo=w_mo[l], g_ffn=g_ffn[l],
                 w_ff_g=w_ff_g[l], w_ff_u=w_ff_u[l], w_ff_d=w_ff_d[l])
        mk = (mem_prompt @ w_mk[l]).reshape(Bp, N_MEM, H_M, DH_M)
        mv = (mem_prompt @ w_mv[l]).reshape(Bp, N_MEM, H_M, DH_M)
        xp, st_p = trunk_layer(xp, p, l, mk, mv, None)
        cache = dict(a_k=cache_a_k[l], a_v=cache_a_v[l], b_k=cache_b_k[l], b_v=cache_b_v[l],
                     C=state_c_C[l], n=state_c_n[l], m=state_c_m[l], conv=state_c_conv[l])
        xs, st_s = trunk_layer(xs, p, l, cache_mem_k[l], cache_mem_v[l], cache)
        for lst, a in zip(new_p, st_p + (mk, mv)):
            lst.append(a)
        for lst, a in zip(new_s, st_s):
            lst.append(a)
    y_prompt = rmsnorm(xp, g_final)
    y_sample = rmsnorm(xs, g_final)
    (p_a_k, p_a_v, p_b_k, p_b_v, p_c_C, p_c_n, p_c_m, p_c_conv, p_mem_k, p_mem_v) = [jnp.stack(a, 0) for a in new_p]
    (s_a_k, s_a_v, s_b_k, s_b_v, s_c_C, s_c_n, s_c_m, s_c_conv) = [jnp.stack(a, 0) for a in new_s]
    return (y_prompt, y_sample, p_a_k, p_a_v, p_b_k, p_b_v, p_c_C, p_c_n, p_c_m, p_c_conv, p_mem_k, p_mem_v,
            s_a_k, s_a_v, s_b_k, s_b_v, s_c_C, s_c_n, s_c_m, s_c_conv)
```

```python
import functools
import math

import jax
import jax.numpy as jnp
from jax import lax
from jax.experimental import pallas as pl
from jax.experimental.pallas import tpu as pltpu

F32 = jnp.float32
BF16 = jnp.bfloat16

D_MODEL = 1024
DEPTH = 2
CHUNK = 64
EPS = 1e-6
NEG = -1e30
H_A = 4
DH_A = 64
DV_A = 2 * DH_A
H_B = 8
DH_B = 64
BAND_CHUNKS = 8
BAND = BAND_CHUNKS * CHUNK
REL_CLIP = 128
H_C = 4
DH_C = 128
D_C = H_C * DH_C
CONV_W = 4
H_M = 4
DH_M = D_MODEL // H_M
D_FF = -(-8 * D_MODEL // (3 * 256)) * 256

LANES = 128
VMEM_LIMIT = 48 * 1024 * 1024

OFF_G = 0
OFF_CQK = 3 * D_MODEL
OFF_AQ = OFF_CQK + 2 * D_C
OFF_AK = OFF_AQ + 512
OFF_AV = OFF_AK + 512
OFF_BQ = OFF_AV + 512
OFF_BK = OFF_BQ + 512
OFF_BV = OFF_BK + 512
OFF_CV = OFF_BV + 512
OFF_CO = OFF_CV + 512
OFF_CIF = OFF_CO + 512
PROJ_TN = 768
N_PROJ = -(-(OFF_CIF + LANES) // PROJ_TN) * PROJ_TN

ALIBI_SLOPES = tuple(2.0 ** (-8.0 * (i + 1) / H_A) for i in range(H_A))

NT_DIMS = (((1,), (1,)), ((), ()))


def _cparams(sem):
    return pltpu.CompilerParams(dimension_semantics=sem, vmem_limit_bytes=VMEM_LIMIT)


def _rms(x, g):
    ms = jnp.mean(x * x, axis=-1, keepdims=True)
    return x * lax.rsqrt(ms + EPS) * g


def _nt(a, b):
    return lax.dot_general(a, b, NT_DIMS, preferred_element_type=F32)


def _proj_kernel(x_ref, g_ref, w_ref, o_ref, h_scr, *, norm):
    @pl.when(pl.program_id(1) == 0)
    def _():
        x = x_ref[...]
        if norm:
            x = _rms(x, g_ref[...])
        h_scr[...] = x.astype(BF16)

    o_ref[...] = jnp.dot(h_scr[...], w_ref[...], preferred_element_type=F32)


def _proj(x, g, w, *, norm, tn):
    m, k = x.shape
    n = w.shape[1]
    tm = min(m, 1024)
    return pl.pallas_call(
        functools.partial(_proj_kernel, norm=norm),
        out_shape=jax.ShapeDtypeStruct((m, n), F32),
        grid=(m // tm, n // tn),
        in_specs=[
            pl.BlockSpec((tm, k), lambda i, j: (i, 0)),
            pl.BlockSpec((1, k), lambda i, j: (0, 0)),
            pl.BlockSpec((k, tn), lambda i, j: (0, j)),
        ],
        out_specs=pl.BlockSpec((tm, tn), lambda i, j: (i, j)),
        scratch_shapes=[pltpu.VMEM((tm, k), BF16)],
        compiler_params=_cparams(("parallel", "arbitrary")),
        name="proj",
    )(x, g, w)


def _online_update(s, v, acc, m, l):
    m_old = m[...]
    m_new = jnp.maximum(m_old, jnp.max(s, axis=-1, keepdims=True))
    alpha = jnp.exp(m_old - m_new)
    p = jnp.exp(s - m_new)
    l[...] = alpha * l[...] + jnp.sum(p, axis=-1, keepdims=True)
    acc[...] = alpha * acc[...] + jnp.dot(p.astype(BF16), v, preferred_element_type=F32)
    m[...] = m_new


def _split_halves(q):
    lane = lax.broadcasted_iota(jnp.int32, q.shape, 1)
    lo = jnp.where(lane < DH_A, q, 0.0).astype(BF16)
    hi = jnp.where(lane >= DH_A, q, 0.0).astype(BF16)
    return lo, hi


def _lambda(lqk, lam_init):
    e1 = jnp.exp(jnp.sum(lqk[0:1, :] * lqk[1:2, :], axis=-1, keepdims=True))
    e2 = jnp.exp(jnp.sum(lqk[2:3, :] * lqk[3:4, :], axis=-1, keepdims=True))
    return e1 - e2 + lam_init


def _diff_prompt_kernel(slopes_ref, lqk_ref, hg_ref, q_ref, k_ref, v_ref, o_ref,
                        kb, vb, acc1, acc2, m1, l1, m2, l2, *, tq, lam_init):
    h = pl.program_id(1)
    qi = pl.program_id(2)
    slope = slopes_ref[h]

    @pl.when(qi == 0)
    def _():
        kb[...] = k_ref[0].astype(BF16)
        vb[...] = v_ref[0].astype(BF16)

    qa, qb = _split_halves(q_ref[0])
    for m, l, acc in ((m1, l1, acc1), (m2, l2, acc2)):
        m[...] = jnp.full(m.shape, NEG, F32)
        l[...] = jnp.zeros(l.shape, F32)
        acc[...] = jnp.zeros(acc.shape, F32)

    def full_tile(ki, carry):
        k0 = pl.multiple_of(ki * tq, tq)
        k = kb[pl.ds(k0, tq), :]
        v = vb[pl.ds(k0, tq), :]
        kpos = k0 + lax.broadcasted_iota(jnp.int32, (1, tq), 1)
        bias = slope * kpos.astype(F32)
        _online_update(_nt(qa, k) + bias, v, acc1, m1, l1)
        _online_update(_nt(qb, k) + bias, v, acc2, m2, l2)
        return carry

    lax.fori_loop(0, qi, full_tile, 0)

    q0 = pl.multiple_of(qi * tq, tq)
    k = kb[pl.ds(q0, tq), :]
    v = vb[pl.ds(q0, tq), :]
    r = lax.broadcasted_iota(jnp.int32, (tq, tq), 0)
    c = lax.broadcasted_iota(jnp.int32, (tq, tq), 1)
    allowed = (c // CHUNK) <= (r // CHUNK)
    bias = slope * (q0 + r - jnp.abs(r - c)).astype(F32)
    _online_update(jnp.where(allowed, _nt(qa, k) + bias, NEG), v, acc1, m1, l1)
    _online_update(jnp.where(allowed, _nt(qb, k) + bias, NEG), v, acc2, m2, l2)

    lam = _lambda(lqk_ref[...], lam_init)
    o = acc1[...] / l1[...] - lam * (acc2[...] / l2[...])
    o_ref[0] = (_rms(o, hg_ref[...]) * (1.0 - lam_init)).astype(o_ref.dtype)


def _diff_prompt(p3, lqk, head_g, lam_init):
    b, t, _ = p3.shape
    tq = min(t, 256)
    cq, ck, cv = OFF_AQ // LANES, OFF_AK // LANES, OFF_AV // LANES
    slopes = jnp.asarray(ALIBI_SLOPES, F32)
    return pl.pallas_call(
        functools.partial(_diff_prompt_kernel, tq=tq, lam_init=lam_init),
        out_shape=jax.ShapeDtypeStruct((b, t, H_A * DV_A), BF16),
        grid=(b, H_A, t // tq),
        in_specs=[
            pl.BlockSpec(memory_space=pltpu.SMEM),
            pl.BlockSpec((4, DH_A), lambda bi, h, qi: (0, 0)),
            pl.BlockSpec((1, DV_A), lambda bi, h, qi: (0, 0)),
            pl.BlockSpec((1, tq, LANES), lambda bi, h, qi: (bi, qi, cq + h)),
            pl.BlockSpec((1, t, LANES), lambda bi, h, qi: (bi, 0, ck + h)),
            pl.BlockSpec((1, t, LANES), lambda bi, h, qi: (bi, 0, cv + h)),
        ],
        out_specs=pl.BlockSpec((1, tq, LANES), lambda bi, h, qi: (bi, qi, h)),
        scratch_shapes=[
            pltpu.VMEM((t, LANES), BF16), pltpu.VMEM((t, LANES), BF16),
            pltpu.VMEM((tq, LANES), F32), pltpu.VMEM((tq, LANES), F32),
            pltpu.VMEM((tq, 1), F32), pltpu.VMEM((tq, 1), F32),
            pltpu.VMEM((tq, 1), F32), pltpu.VMEM((tq, 1), F32),
        ],
        compiler_params=_cparams(("parallel", "parallel", "arbitrary")),
        name="diff_prompt",
    )(slopes, lqk, head_g, p3, p3, p3)


def _pieces_attention(qm, pieces):
    ss = [_nt(qm, k) + bias for k, _, bias in pieces]
    m = functools.reduce(jnp.maximum, [jnp.max(s, axis=-1, keepdims=True) for s in ss])
    es = [jnp.exp(s - m) for s in ss]
    l = functools.reduce(jnp.add, [jnp.sum(e, axis=-1, keepdims=True) for e in es])
    o = functools.reduce(jnp.add, [jnp.dot(e.astype(BF16), v, preferred_element_type=F32)
                                   for e, (_, v, _) in zip(es, pieces)])
    return o / l


def _diff_sample_kernel(lqk_ref, hg_ref, q_ref, kc_ref, vc_ref, kn_ref, vn_ref, o_ref, *, lam_init):
    t = q_ref.shape[1]
    past = kc_ref.shape[1]
    lam = _lambda(lqk_ref[...], lam_init)
    rc = lax.broadcasted_iota(jnp.int32, (t, past), 0)
    cc = lax.broadcasted_iota(jnp.int32, (t, past), 1)
    rn = lax.broadcasted_iota(jnp.int32, (t, t), 0)
    cn = lax.broadcasted_iota(jnp.int32, (t, t), 1)
    dist_c = jnp.abs(past + rc - cc).astype(F32)
    dist_n = jnp.abs(rn - cn).astype(F32)
    ok_c = (cc // CHUNK) <= ((past + rc) // CHUNK)
    ok_n = ((past + cn) // CHUNK) <= ((past + rn) // CHUNK)
    for h in range(H_A):
        cols = slice(h * LANES, (h + 1) * LANES)
        bias_c = jnp.where(ok_c, -ALIBI_SLOPES[h] * dist_c, NEG)
        bias_n = jnp.where(ok_n, -ALIBI_SLOPES[h] * dist_n, NEG)
        pieces = [(kc_ref[0, :, cols].astype(BF16), vc_ref[0, :, cols].astype(BF16), bias_c),
                  (kn_ref[0, :, cols].astype(BF16), vn_ref[0, :, cols].astype(BF16), bias_n)]
        qa, qb = _split_halves(q_ref[0, :, cols])
        o = _pieces_attention(qa, pieces) - lam * _pieces_attention(qb, pieces)
        o_ref[0, :, cols] = (_rms(o, hg_ref[...]) * (1.0 - lam_init)).astype(o_ref.dtype)


def _diff_sample(p3, kc, vc, lqk, head_g, lam_init):
    b, t, _ = p3.shape
    past = kc.shape[1]
    w = H_A * DV_A
    blk = lambda off: pl.BlockSpec((1, t, w), lambda bi: (bi, 0, off // w))
    cache = pl.BlockSpec((1, past, w), lambda bi: (bi, 0, 0))
    return pl.pallas_call(
        functools.partial(_diff_sample_kernel, lam_init=lam_init),
        out_shape=jax.ShapeDtypeStruct((b, t, w), BF16),
        grid=(b,),
        in_specs=[
            pl.BlockSpec((4, DH_A), lambda bi: (0, 0)),
            pl.BlockSpec((1, DV_A), lambda bi: (0, 0)),
            blk(OFF_AQ), cache, cache, blk(OFF_AK), blk(OFF_AV),
        ],
        out_specs=pl.BlockSpec((1, t, w), lambda bi: (bi, 0, 0)),
        compiler_params=_cparams(("parallel",)),
        name="diff_sample",
    )(lqk, head_g, p3, kc, vc, p3, p3)


QPAIR = 2 * CHUNK
WIN = BAND + QPAIR


def _relbias_kernel(tab_ref, o_ref):
    h = pl.program_id(0)
    r = lax.broadcasted_iota(jnp.int32, (QPAIR, WIN), 0)
    j = lax.broadcasted_iota(jnp.int32, (QPAIR, WIN), 1)
    rel = jnp.clip(r - j + BAND, -REL_CLIP, REL_CLIP) + REL_CLIP
    qc = r // CHUNK
    kc = j // CHUNK - BAND_CHUNKS
    allowed = (kc <= qc) & (kc >= qc - BAND_CHUNKS)

    def body(t, acc):
        return jnp.where(rel == t, tab_ref[h, t], acc)

    acc = lax.fori_loop(0, 2 * REL_CLIP + 1, body, jnp.zeros((QPAIR, WIN), F32))
    o_ref[0] = jnp.where(allowed, acc, NEG)


def _relbias(table):
    return pl.pallas_call(
        _relbias_kernel,
        out_shape=jax.ShapeDtypeStruct((H_B, QPAIR, WIN), F32),
        grid=(H_B,),
        in_specs=[pl.BlockSpec(memory_space=pltpu.SMEM)],
        out_specs=pl.BlockSpec((1, QPAIR, WIN), lambda h: (h, 0, 0)),
        compiler_params=_cparams(("arbitrary",)),
        name="relbias",
    )(table)


BAND_TQ = 512


def _band_prompt_kernel(bias_ref, q_ref, kp_ref, kc_ref, vp_ref, vc_ref, o_ref, kcat, vcat):
    i = pl.program_id(1)
    kcat[0:BAND_TQ, :] = kp_ref[0].astype(BF16)
    kcat[BAND_TQ:, :] = kc_ref[0].astype(BF16)
    vcat[0:BAND_TQ, :] = vp_ref[0].astype(BF16)
    vcat[BAND_TQ:, :] = vc_ref[0].astype(BF16)
    lane = lax.broadcasted_iota(jnp.int32, (QPAIR, LANES), 1)

    def pair_block(qp, carry):
        r0 = pl.multiple_of(qp * QPAIR, QPAIR)
        q = q_ref[0, pl.ds(r0, QPAIR), :]
        kpos = (i - 1) * BAND_TQ + r0 + lax.broadcasted_iota(jnp.int32, (1, WIN), 1)
        before_start = jnp.where(kpos >= 0, 0.0, NEG)
        outs = []
        for hp in range(H_B // 2):
            cols = slice(hp * LANES, (hp + 1) * LANES)
            kw = kcat[pl.ds(r0, WIN), cols]
            vw = vcat[pl.ds(r0, WIN), cols]
            halves = []
            for sub, qm in enumerate(_split_halves(q[:, cols])):
                s = _nt(qm, kw) + bias_ref[2 * hp + sub] + before_start
                m = jnp.max(s, axis=-1, keepdims=True)
                e = jnp.exp(s - m)
                l = jnp.sum(e, axis=-1, keepdims=True)
                halves.append(jnp.dot(e.astype(BF16), vw, preferred_element_type=F32) / l)
            outs.append(jnp.where(lane < DH_B, halves[0], halves[1]))
        o_ref[0, pl.ds(r0, QPAIR), :] = jnp.concatenate(outs, axis=1).astype(o_ref.dtype)
        return carry

    lax.fori_loop(0, BAND_TQ // QPAIR, pair_block, 0)


def _band_prompt(p3, bias):
    b, t, _ = p3.shape
    assert t % BAND_TQ == 0
    w = H_B * DH_B
    cq, ck, cv = OFF_BQ // w, OFF_BK // w, OFF_BV // w
    prev = lambda col: pl.BlockSpec((1, BAND_TQ, w), lambda bi, i: (bi, jnp.maximum(i - 1, 0), col))
    cur = lambda col: pl.BlockSpec((1, BAND_TQ, w), lambda bi, i: (bi, i, col))
    return pl.pallas_call(
        _band_prompt_kernel,
        out_shape=jax.ShapeDtypeStruct((b, t, w), BF16),
        grid=(b, t // BAND_TQ),
        in_specs=[
            pl.BlockSpec((H_B, QPAIR, WIN), lambda bi, i: (0, 0, 0)),
            cur(cq), prev(ck), cur(ck), prev(cv), cur(cv),
        ],
        out_specs=pl.BlockSpec((1, BAND_TQ, w), lambda bi, i: (bi, i, 0)),
        scratch_shapes=[pltpu.VMEM((2 * BAND_TQ, w), BF16), pltpu.VMEM((2 * BAND_TQ, w), BF16)],
        compiler_params=_cparams(("parallel", "arbitrary")),
        name="band_prompt",
    )(bias, p3, p3, p3, p3, p3)


def _band_sample_kernel(bc_ref, bn_ref, q_ref, kc_ref, vc_ref, kn_ref, vn_ref, o_ref, *, past):
    t = q_ref.shape[1]
    nband = kc_ref.shape[1]
    lane = lax.broadcasted_iota(jnp.int32, (t, LANES), 1)
    qpos_c = past + lax.broadcasted_iota(jnp.int32, (t, nband), 0)
    kpos_c = past - nband + lax.broadcasted_iota(jnp.int32, (t, nband), 1)
    qpos_n = past + lax.broadcasted_iota(jnp.int32, (t, t), 0)
    kpos_n = past + lax.broadcasted_iota(jnp.int32, (t, t), 1)

    def allowed(qpos, kpos):
        qc, kc = qpos // CHUNK, kpos // CHUNK
        return (kpos >= 0) & (kc <= qc) & (kc >= qc - BAND_CHUNKS)

    ok_c = allowed(qpos_c, kpos_c)
    ok_n = allowed(qpos_n, kpos_n)
    for hp in range(H_B // 2):
        cols = slice(hp * LANES, (hp + 1) * LANES)
        halves = []
        for sub, qm in enumerate(_split_halves(q_ref[0, :, cols])):
            hd = 2 * hp + sub
            pieces = [(kc_ref[0, :, cols].astype(BF16), vc_ref[0, :, cols].astype(BF16),
                       jnp.where(ok_c, bc_ref[hd], NEG)),
                      (kn_ref[0, :, cols].astype(BF16), vn_ref[0, :, cols].astype(BF16),
                       jnp.where(ok_n, bn_ref[hd], NEG))]
            halves.append(_pieces_attention(qm, pieces))
        o_ref[0, :, cols] = jnp.where(lane < DH_B, halves[0], halves[1]).astype(o_ref.dtype)


def _band_sample(p3, kc, vc, bias, past):
    b, t, _ = p3.shape
    nband = kc.shape[1]
    assert nband == BAND and t <= CHUNK
    w = H_B * DH_B
    bias_c = bias[:, :t, :nband]
    bias_n = bias[:, :t, nband:nband + t]
    blk = lambda off: pl.BlockSpec((1, t, w), lambda bi: (bi, 0, off // w))
    cache = pl.BlockSpec((1, nband, w), lambda bi: (bi, 0, 0))
    return pl.pallas_call(
        functools.partial(_band_sample_kernel, past=past),
        out_shape=jax.ShapeDtypeStruct((b, t, w), BF16),
        grid=(b,),
        in_specs=[
            pl.BlockSpec((H_B, t, nband), lambda bi: (0, 0, 0)),
            pl.BlockSpec((H_B, t, t), lambda bi: (0, 0, 0)),
            blk(OFF_BQ), cache, cache, blk(OFF_BK), blk(OFF_BV),
        ],
        out_specs=pl.BlockSpec((1, t, w), lambda bi: (bi, 0, 0)),
        compiler_params=_cparams(("parallel",)),
        name="band_sample",
    )(bias_c, bias_n, p3, kc, vc, p3, p3)


CONV_PAD = 8


def _mlstm_kernel(cqk_ref, cv_ref, co_ref, cif_ref, conv0_ref, cw_ref, cb_ref, gb_ref, hg_ref,
                  c0_ref, n0_ref, m0_ref, hs_ref, c_ref, n_ref, m_ref, xext, *, L):
    @pl.when(pl.program_id(1) == 0)
    def _():
        xext[0:CONV_PAD, :] = conv0_ref[0]
        c_ref[...] = c0_ref[...]
        n_ref[...] = n0_ref[...]
        m_ref[...] = m0_ref[...]

    xext[CONV_PAD:CONV_PAD + L, :] = cqk_ref[0]
    base = CONV_PAD - (CONV_W - 1)
    u = 0.0
    for j in range(CONV_W):
        u = u + xext[base + j:base + j + L, :] * cw_ref[j:j + 1, :]
    u = cb_ref[...] + u
    u = u * jax.nn.sigmoid(u)
    xext[0:CONV_PAD, :] = xext[L:L + CONV_PAD, :]

    z = cif_ref[0] + gb_ref[...]
    lane = lax.broadcasted_iota(jnp.int32, (L, LANES), 1)
    lf = jnp.minimum(z, 0.0) - jnp.log1p(jnp.exp(-jnp.abs(z)))
    tr = lax.broadcasted_iota(jnp.int32, (L, L), 0)
    tc = lax.broadcasted_iota(jnp.int32, (L, L), 1)
    causal = tr >= tc
    b_all = jnp.dot(causal.astype(F32), lf, preferred_element_type=F32,
                    precision=lax.Precision.HIGHEST)
    sr = lax.broadcasted_iota(jnp.int32, (8, LANES), 0)
    sc = lax.broadcasted_iota(jnp.int32, (8, LANES), 1)
    sel = jnp.where(sr < H_C, jnp.where(sc == sr, 1.0, jnp.where(sc == sr + H_C, -1.0, 0.0)), 0.0)
    rt = lax.dot_general(sel, jnp.where(lane < H_C, z, b_all), NT_DIMS,
                         preferred_element_type=F32, precision=lax.Precision.HIGHEST)
    er = lax.broadcasted_iota(jnp.int32, (DH_C, DH_C), 0)
    ec = lax.broadcasted_iota(jnp.int32, (DH_C, DH_C), 1)
    eye = jnp.where(er == ec, 1.0, 0.0).astype(BF16)

    for h in range(H_C):
        cols = slice(h * DH_C, (h + 1) * DH_C)
        q = u[:, h * DH_C:(h + 1) * DH_C]
        k = u[:, D_C + h * DH_C:D_C + (h + 1) * DH_C] * (DH_C ** -0.5)
        v = cv_ref[0, :, cols]
        qb, kb_, vb_ = q.astype(BF16), k.astype(BF16), v.astype(BF16)
        b_col = b_all[:, H_C + h:H_C + h + 1]
        li_col = z[:, h:h + 1]
        m_prev = m_ref[0, h:h + 1, 0:1]
        cs = c_ref[0, h]
        ns = n_ref[0, h:h + 1, :]

        dmat = jnp.where(causal, b_col + rt[h:h + 1, :], NEG)
        inter = b_col + m_prev
        m_t = jnp.maximum(inter, jnp.max(dmat, axis=-1, keepdims=True))
        w_intra = jnp.exp(dmat - m_t)
        w_inter = jnp.exp(inter - m_t)
        a = w_intra * _nt(qb, kb_)
        num = (jnp.dot(a.astype(BF16), vb_, preferred_element_type=F32)
               + w_inter * _nt(qb, cs.astype(BF16)))
        den = jnp.sum(a, axis=-1, keepdims=True) + w_inter * jnp.sum(q * ns, axis=-1, keepdims=True)
        hh = num / jnp.maximum(jnp.abs(den), jnp.exp(-m_t))

        b_last = b_col[L - 1:L, :]
        g = b_last - b_col + li_col
        m_new = jnp.maximum(b_last + m_prev, jnp.max(g, axis=0, keepdims=True))
        ws = jnp.exp(g - m_new)
        decay = jnp.exp(b_last + m_prev - m_new)
        vwt = _nt(eye, (ws * v).astype(BF16)).astype(BF16)
        c_ref[0, h] = decay * cs + jnp.dot(vwt, kb_, preferred_element_type=F32)
        n_ref[0, h:h + 1, :] = decay * ns + jnp.sum(ws * k, axis=0, keepdims=True)
        m_ref[0, h:h + 1, :] = jnp.broadcast_to(m_new, (1, LANES))

        gate = jax.nn.sigmoid(co_ref[0, :, cols])
        hs_ref[0, :, cols] = (_rms(hh, hg_ref[...]) * gate).astype(hs_ref.dtype)


def _mlstm(p3, conv0, conv_w, conv_b, gate_b, head_g, c0, n0, m0, L):
    b, t, _ = p3.shape
    nc = t // L
    blk = lambda off, w: pl.BlockSpec((1, L, w), lambda bi, c: (bi, c, off // w))
    const = lambda shape: pl.BlockSpec(shape, lambda bi, c: (0,) * len(shape))
    per_b = lambda shape: pl.BlockSpec((1,) + shape, lambda bi, c: (bi,) + (0,) * len(shape))
    return pl.pallas_call(
        functools.partial(_mlstm_kernel, L=L),
        out_shape=(jax.ShapeDtypeStruct((b, t, D_C), BF16),
                   jax.ShapeDtypeStruct((b, H_C, DH_C, DH_C), F32),
                   jax.ShapeDtypeStruct((b, H_C, DH_C), F32),
                   jax.ShapeDtypeStruct((b, H_C, LANES), F32)),
        grid=(b, nc),
        in_specs=[
            blk(OFF_CQK, 2 * D_C), blk(OFF_CV, D_C), blk(OFF_CO, D_C), blk(OFF_CIF, LANES),
            per_b((CONV_PAD, 2 * D_C)), const((CONV_W, 2 * D_C)), const((1, 2 * D_C)),
            const((1, LANES)), const((1, DH_C)),
            per_b((H_C, DH_C, DH_C)), per_b((H_C, DH_C)), per_b((H_C, LANES)),
        ],
        out_specs=(pl.BlockSpec((1, L, D_C), lambda bi, c: (bi, c, 0)),
                   per_b((H_C, DH_C, DH_C)), per_b((H_C, DH_C)), per_b((H_C, LANES))),
        scratch_shapes=[pltpu.VMEM((CONV_PAD + L, 2 * D_C), F32)],
        compiler_params=_cparams(("parallel", "arbitrary")),
        name="mlstm",
    )(p3, p3, p3, p3, conv0, conv_w, conv_b, gate_b, head_g, c0, n0, m0)


def _mix_kernel(oa_ref, ob_ref, oc_ref, ga_ref, gb_ref, gc_ref, x_ref, wa_ref, wb_ref, wc_ref, wo_ref, o_ref):
    mixed = (jax.nn.sigmoid(ga_ref[...]) * jnp.dot(oa_ref[...], wa_ref[...], preferred_element_type=F32)
             + jax.nn.sigmoid(gb_ref[...]) * jnp.dot(ob_ref[...], wb_ref[...], preferred_element_type=F32)
             + jax.nn.sigmoid(gc_ref[...]) * jnp.dot(oc_ref[...], wc_ref[...], preferred_element_type=F32))
    o_ref[...] = x_ref[...] + jnp.dot(mixed.astype(BF16), wo_ref[...], preferred_element_type=F32)


def _mix(oa, ob, oc, p2, x, wa, wb, wc, wo):
    m, d = x.shape
    tm = min(m, 512)
    row = lambda w, col=0: pl.BlockSpec((tm, w), lambda i: (i, col))
    full = lambda a: pl.BlockSpec(a.shape, lambda i: (0, 0))
    g0 = OFF_G // d
    return pl.pallas_call(
        _mix_kernel,
        out_shape=jax.ShapeDtypeStruct((m, d), F32),
        grid=(m // tm,),
        in_specs=[row(oa.shape[1]), row(ob.shape[1]), row(oc.shape[1]),
                  row(d, g0), row(d, g0 + 1), row(d, g0 + 2), row(d),
                  full(wa), full(wb), full(wc), full(wo)],
        out_specs=row(d),
        compiler_params=_cparams(("parallel",)),
        name="mix",
    )(oa, ob, oc, p2, p2, p2, x, wa, wb, wc, wo)


def _cross_kernel(x_ref, g_ref, wq_ref, mk_ref, mv_ref, wo_ref, o_ref):
    x = x_ref[0]
    h = _rms(x, g_ref[...]).astype(BF16)
    q = (jnp.dot(h, wq_ref[...], preferred_element_type=F32) * (DH_M ** -0.5)).astype(BF16)
    outs = []
    for hd in range(H_M):
        cols = slice(hd * DH_M, (hd + 1) * DH_M)
        s = _nt(q[:, cols], mk_ref[0, :, cols].astype(BF16))
        m = jnp.max(s, axis=-1, keepdims=True)
        e = jnp.exp(s - m)
        l = jnp.sum(e, axis=-1, keepdims=True)
        o = jnp.dot(e.astype(BF16), mv_ref[0, :, cols].astype(BF16), preferred_element_type=F32) / l
        outs.append(o.astype(BF16))
    o = jnp.concatenate(outs, axis=1)
    o_ref[0] = x + jnp.dot(o, wo_ref[...], preferred_element_type=F32)


def _cross(x3, g, wq, mk, mv, wo):
    b, t, d = x3.shape
    tm = min(t, 512)
    nm = mk.shape[1]
    full = lambda a: pl.BlockSpec(a.shape, lambda bi, i: (0, 0))
    return pl.pallas_call(
        _cross_kernel,
        out_shape=jax.ShapeDtypeStruct((b, t, d), F32),
        grid=(b, t // tm),
        in_specs=[pl.BlockSpec((1, tm, d), lambda bi, i: (bi, i, 0)), full(g), full(wq),
                  pl.BlockSpec((1, nm, d), lambda bi, i: (bi, 0, 0)),
                  pl.BlockSpec((1, nm, d), lambda bi, i: (bi, 0, 0)), full(wo)],
        out_specs=pl.BlockSpec((1, tm, d), lambda bi, i: (bi, i, 0)),
        compiler_params=_cparams(("parallel", "parallel")),
        name="cross",
    )(x3, g, wq, mk, mv, wo)


def _ffn_kernel(x_ref, g_ref, wg_ref, wu_ref, wd_ref, gf_ref, o_ref, h_scr, acc, *, final_norm):
    j = pl.program_id(1)

    @pl.when(j == 0)
    def _():
        h_scr[...] = _rms(x_ref[...], g_ref[...]).astype(BF16)
        acc[...] = jnp.zeros(acc.shape, F32)

    h = h_scr[...]
    gate = jnp.dot(h, wg_ref[...], preferred_element_type=F32)
    up = jnp.dot(h, wu_ref[...], preferred_element_type=F32)
    a = (gate * jax.nn.sigmoid(gate) * up).astype(BF16)
    acc[...] += jnp.dot(a, wd_ref[...], preferred_element_type=F32)

    @pl.when(j == pl.num_programs(1) - 1)
    def _():
        y = x_ref[...] + acc[...]
        if final_norm:
            y = _rms(y, gf_ref[...])
        o_ref[...] = y


FFN_TF = 256


def _ffn(x, g, wg, wu, wd, g_final, final_norm):
    m, d = x.shape
    tm = min(m, 1024)
    return pl.pallas_call(
        functools.partial(_ffn_kernel, final_norm=final_norm),
        out_shape=jax.ShapeDtypeStruct((m, d), F32),
        grid=(m // tm, D_FF // FFN_TF),
        in_specs=[
            pl.BlockSpec((tm, d), lambda i, j: (i, 0)),
            pl.BlockSpec((1, d), lambda i, j: (0, 0)),
            pl.BlockSpec((d, FFN_TF), lambda i, j: (0, j)),
            pl.BlockSpec((d, FFN_TF), lambda i, j: (0, j)),
            pl.BlockSpec((FFN_TF, d), lambda i, j: (j, 0)),
            pl.BlockSpec((1, d), lambda i, j: (0, 0)),
        ],
        out_specs=pl.BlockSpec((tm, d), lambda i, j: (i, 0)),
        scratch_shapes=[pltpu.VMEM((tm, d), BF16), pltpu.VMEM((tm, d), F32)],
        compiler_params=_cparams(("parallel", "arbitrary")),
        name="ffn",
    )(x, g, wg, wu, wd, g_final)


def _interleave_heads(a, b, heads):
    k = a.shape[0]
    return jnp.concatenate([a.reshape(k, heads, -1), b.reshape(k, heads, -1)], axis=-1).reshape(k, -1)


def _prep_w_in(w):
    sizes = (H_A * DH_A,) * 4 + (H_A * DV_A,) + (H_B * DH_B,) * 3 + (2 * D_C, D_C, D_C, 2 * H_C, 3 * D_MODEL)
    offs = [0]
    for s in sizes:
        offs.append(offs[-1] + s)
    (a_q1, a_q2, a_k1, a_k2, a_v, b_q, b_k, b_v, c_qk, c_v, c_o, c_if, g) = (
        w[:, offs[i]:offs[i + 1]] for i in range(len(sizes)))
    scale_a = DH_A ** -0.5
    scale_b = DH_B ** -0.5
    pad_if = jnp.zeros((w.shape[0], LANES - 2 * H_C), w.dtype)
    tail = jnp.zeros((w.shape[0], N_PROJ - OFF_CIF - LANES), w.dtype)
    cols = [g, c_qk, _interleave_heads(a_q1, a_q2, H_A) * scale_a, _interleave_heads(a_k1, a_k2, H_A), a_v,
            b_q * scale_b, b_k, b_v, c_v, c_o, c_if, pad_if, tail]
    return jnp.concatenate(cols, axis=1).astype(BF16)


def _layer(x3, l, w, mem_k, mem_v, cache, final_norm):
    b, t, d = x3.shape
    x2 = x3.reshape(b * t, d)
    row = lambda a: a.reshape(1, -1)
    lam_init = 0.8 - 0.6 * math.exp(-0.3 * l)

    p2 = _proj(x2, row(w['g_mix']), w['w_in'], norm=True, tn=PROJ_TN)
    p3 = p2.reshape(b, t, N_PROJ)
    a_k = p3[:, :, OFF_AK:OFF_AK + 512].reshape(b, t, H_A, 2 * DH_A)
    a_v = p3[:, :, OFF_AV:OFF_AV + 512].reshape(b, t, H_A, DV_A)
    b_k = p3[:, :, OFF_BK:OFF_BK + 512].reshape(b, t, H_B, DH_B)
    b_v = p3[:, :, OFF_BV:OFF_BV + 512].reshape(b, t, H_B, DH_B)

    gate_b = jnp.concatenate([w['b_i'], w['b_f'], jnp.zeros((LANES - 2 * H_C,), F32)]).reshape(1, LANES)
    if cache is None:
        oa = _diff_prompt(p3, w['lqk'], row(w['a_head_g']), lam_init)
        ob = _band_prompt(p3, w['relbias'])
        conv0 = jnp.zeros((b, CONV_PAD, 2 * D_C), F32)
        c0 = jnp.zeros((b, H_C, DH_C, DH_C), F32)
        n0 = jnp.zeros((b, H_C, DH_C), F32)
        m0 = jnp.zeros((b, H_C, LANES), F32)
        keep = min(BAND, t)
        new_b = (b_k[:, t - keep:], b_v[:, t - keep:])
        L = CHUNK
    else:
        past = cache['a_k'].shape[1]
        oa = _diff_sample(p3, cache['a_k'].reshape(b, past, -1), cache['a_v'].reshape(b, past, -1),
                          w['lqk'], row(w['a_head_g']), lam_init)
        nband = cache['b_k'].shape[1]
        ob = _band_sample(p3, cache['b_k'].reshape(b, nband, -1), cache['b_v'].reshape(b, nband, -1),
                          w['relbias'], past)
        conv0 = jnp.pad(cache['conv'], ((0, 0), (CONV_PAD - (CONV_W - 1), 0), (0, 0)))
        c0, n0 = cache['C'], cache['n']
        m0 = jnp.broadcast_to(cache['m'][:, :, None], (b, H_C, LANES))
        new_b = (b_k, b_v)
        L = t
    oc, c_new, n_new, m_new = _mlstm(p3, conv0, w['conv_w'], row(w['conv_b']), gate_b, row(w['c_head_g']),
                                     c0, n0, m0, L)
    assert t >= CONV_W - 1
    conv_new = p3[:, t - (CONV_W - 1):, OFF_CQK:OFF_CQK + 2 * D_C]

    x2 = _mix(oa.reshape(b * t, -1), ob.reshape(b * t, -1), oc.reshape(b * t, -1), p2, x2,
              w['w_up_a'], w['w_up_b'], w['w_up_c'], w['w_o'])
    x3 = _cross(x2.reshape(b, t, d), row(w['g_cross']), w['w_mq'], mem_k, mem_v, w['w_mo'])
    x2 = _ffn(x3.reshape(b * t, d), row(w['g_ffn']), w['w_ff_g'], w['w_ff_u'], w['w_ff_d'],
              row(w['g_final']), final_norm)
    return x2.reshape(b, t, d), (a_k, a_v) + new_b + (c_new, n_new, m_new[:, :, 0], conv_new)


def kernel(x_prompt, x_sample, cache_a_k, cache_a_v, cache_b_k, cache_b_v, state_c_C, state_c_n, state_c_m, state_c_conv, cache_mem_k, cache_mem_v, mem_prompt, g_mix, w_in, a_lq1, a_lk1, a_lq2, a_lk2, a_head_g, b_rel, c_conv_w, c_conv_b, c_b_i, c_b_f, c_head_g, w_up_a, w_up_b, w_up_c, w_o, g_cross, w_mq, w_mk, w_mv, w_mo, g_ffn, w_ff_g, w_ff_u, w_ff_d, g_final):
    xp, xs = x_prompt, x_sample
    bp = x_prompt.shape[0]
    n_mem = mem_prompt.shape[1]
    mem2 = mem_prompt.reshape(bp * n_mem, D_MODEL)
    new_p = [[] for _ in range(10)]
    new_s = [[] for _ in range(8)]
    bf = lambda a: a.astype(BF16)
    for l in range(DEPTH):
        w = dict(g_mix=g_mix[l], w_in=_prep_w_in(w_in[l]),
                 lqk=jnp.stack([a_lq1[l], a_lk1[l], a_lq2[l], a_lk2[l]]),
                 a_head_g=a_head_g[l], relbias=_relbias(b_rel[l]), conv_w=c_conv_w[l], conv_b=c_conv_b[l],
                 b_i=c_b_i[l], b_f=c_b_f[l], c_head_g=c_head_g[l],
                 w_up_a=bf(w_up_a[l]), w_up_b=bf(w_up_b[l]), w_up_c=bf(w_up_c[l]), w_o=bf(w_o[l]),
                 g_cross=g_cross[l], w_mq=bf(w_mq[l]), w_mo=bf(w_mo[l]), g_ffn=g_ffn[l],
                 w_ff_g=bf(w_ff_g[l]), w_ff_u=bf(w_ff_u[l]), w_ff_d=bf(w_ff_d[l]), g_final=g_final)
        w_mkv = bf(jnp.concatenate([w_mk[l], w_mv[l]], axis=1))
        mkv = _proj(mem2, jnp.ones((1, D_MODEL), F32), w_mkv, norm=False, tn=512)
        mk2 = mkv[:, :D_MODEL].reshape(bp, n_mem, D_MODEL)
        mv2 = mkv[:, D_MODEL:].reshape(bp, n_mem, D_MODEL)
        last = l == DEPTH - 1
        xp, st_p = _layer(xp, l, w, mk2, mv2, None, last)
        cache = dict(a_k=cache_a_k[l], a_v=cache_a_v[l], b_k=cache_b_k[l], b_v=cache_b_v[l],
                     C=state_c_C[l], n=state_c_n[l], m=state_c_m[l], conv=state_c_conv[l])
        bs = xs.shape[0]
        xs, st_s = _layer(xs, l, w, cache_mem_k[l].reshape(bs, n_mem, D_MODEL),
                          cache_mem_v[l].reshape(bs, n_mem, D_MODEL), cache, last)
        mk4 = mk2.reshape(bp, n_mem, H_M, DH_M)
        mv4 = mv2.reshape(bp, n_mem, H_M, DH_M)
        for lst, a in zip(new_p, st_p + (mk4, mv4)):
            lst.append(a)
        for lst, a in zip(new_s, st_s):
            lst.append(a)
    outs_p = [jnp.stack(a, 0) for a in new_p]
    outs_s = [jnp.stack(a, 0) for a in new_s]
    return (xp, xs) + tuple(outs_p) + tuple(outs_s)
```

```python
import functools
import math

import jax
import jax.numpy as jnp
from jax import lax
from jax.experimental import pallas as pl
from jax.experimental.pallas import tpu as pltpu

F32 = jnp.float32
BF16 = jnp.bfloat16

D_MODEL = 1024
DEPTH = 2
CHUNK = 64
EPS = 1e-6
NEG = -1e30
H_A = 4
DH_A = 64
DV_A = 2 * DH_A
H_B = 8
DH_B = 64
BAND_CHUNKS = 8
BAND = BAND_CHUNKS * CHUNK
REL_CLIP = 128
H_C = 4
DH_C = 128
D_C = H_C * DH_C
CONV_W = 4
H_M = 4
DH_M = D_MODEL // H_M
D_FF = -(-8 * D_MODEL // (3 * 256)) * 256

LANES = 128
VMEM_LIMIT = 48 * 1024 * 1024

OFF_G = 0
OFF_CQK = 3 * D_MODEL
OFF_AQ = OFF_CQK + 2 * D_C
OFF_AK = OFF_AQ + 512
OFF_AV = OFF_AK + 512
OFF_BQ = OFF_AV + 512
OFF_BK = OFF_BQ + 512
OFF_BV = OFF_BK + 512
OFF_CV = OFF_BV + 512
OFF_CO = OFF_CV + 512
OFF_CIF = OFF_CO + 512
PROJ_TN = 768
N_PROJ = -(-(OFF_CIF + LANES) // PROJ_TN) * PROJ_TN

ALIBI_SLOPES = tuple(2.0 ** (-8.0 * (i + 1) / H_A) for i in range(H_A))

NT_DIMS = (((1,), (1,)), ((), ()))


def _cparams(sem):
    return pltpu.CompilerParams(dimension_semantics=sem, vmem_limit_bytes=VMEM_LIMIT)


def _rms(x, g):
    ms = jnp.mean(x * x, axis=-1, keepdims=True)
    return x * lax.rsqrt(ms + EPS) * g


def _nt(a, b):
    return lax.dot_general(a, b, NT_DIMS, preferred_element_type=F32)


def _proj_kernel(x_ref, g_ref, w_ref, o_ref, h_scr, *, norm):
    @pl.when(pl.program_id(1) == 0)
    def _():
        x = x_ref[...]
        if norm:
            x = _rms(x, g_ref[...])
        h_scr[...] = x.astype(BF16)

    o_ref[...] = jnp.dot(h_scr[...], w_ref[...], preferred_element_type=F32)


def _proj(x, g, w, *, norm, tn):
    m, k = x.shape
    n = w.shape[1]
    tm = min(m, 1024)
    return pl.pallas_call(
        functools.partial(_proj_kernel, norm=norm),
        out_shape=jax.ShapeDtypeStruct((m, n), F32),
        grid=(m // tm, n // tn),
        in_specs=[
            pl.BlockSpec((tm, k), lambda i, j: (i, 0)),
            pl.BlockSpec((1, k), lambda i, j: (0, 0)),
            pl.BlockSpec((k, tn), lambda i, j: (0, j)),
        ],
        out_specs=pl.BlockSpec((tm, tn), lambda i, j: (i, j)),
        scratch_shapes=[pltpu.VMEM((tm, k), BF16)],
        compiler_params=_cparams(("parallel", "arbitrary")),
        name="proj",
    )(x, g, w)


def _online_update(s, v, acc, m, l):
    slabs = [s[:, j * LANES:(j + 1) * LANES] for j in range(s.shape[1] // LANES)]
    m_old = m[...]
    m_new = jnp.maximum(m_old, jnp.max(functools.reduce(jnp.maximum, slabs), axis=-1, keepdims=True))
    alpha = jnp.exp(m_old - m_new)
    ps = [jnp.exp(c - m_new) for c in slabs]
    l[...] = alpha * l[...] + jnp.sum(functools.reduce(jnp.add, ps), axis=-1, keepdims=True)
    p = jnp.concatenate(ps, axis=1).astype(BF16)
    acc[...] = alpha * acc[...] + jnp.dot(p, v, preferred_element_type=F32)
    m[...] = m_new


def _split_halves(q):
    lane = lax.broadcasted_iota(jnp.int32, q.shape, 1)
    lo = jnp.where(lane < DH_A, q, 0.0).astype(BF16)
    hi = jnp.where(lane >= DH_A, q, 0.0).astype(BF16)
    return lo, hi


def _lambda(lqk, lam_init):
    e1 = jnp.exp(jnp.sum(lqk[0:1, :] * lqk[1:2, :], axis=-1, keepdims=True))
    e2 = jnp.exp(jnp.sum(lqk[2:3, :] * lqk[3:4, :], axis=-1, keepdims=True))
    return e1 - e2 + lam_init


def _diff_prompt_kernel(slopes_ref, lqk_ref, hg_ref, q_ref, k_ref, v_ref, o_ref,
                        kb, vb, acc1, acc2, m1, l1, m2, l2, *, tq, lam_init):
    h = pl.program_id(1)
    qi = pl.program_id(2)
    slope = slopes_ref[h]

    @pl.when(qi == 0)
    def _():
        kb[...] = k_ref[0].astype(BF16)
        vb[...] = v_ref[0].astype(BF16)

    qa, qb = _split_halves(q_ref[0])
    for m, l, acc in ((m1, l1, acc1), (m2, l2, acc2)):
        m[...] = jnp.full(m.shape, NEG, F32)
        l[...] = jnp.zeros(l.shape, F32)
        acc[...] = jnp.zeros(acc.shape, F32)

    def full_tile(ki, carry):
        k0 = pl.multiple_of(ki * tq, tq)
        k = kb[pl.ds(k0, tq), :]
        v = vb[pl.ds(k0, tq), :]
        kpos = k0 + lax.broadcasted_iota(jnp.int32, (1, tq), 1)
        bias = slope * kpos.astype(F32)
        _online_update(_nt(qa, k) + bias, v, acc1, m1, l1)
        _online_update(_nt(qb, k) + bias, v, acc2, m2, l2)
        return carry

    lax.fori_loop(0, qi, full_tile, 0)

    q0 = pl.multiple_of(qi * tq, tq)
    k = kb[pl.ds(q0, tq), :]
    v = vb[pl.ds(q0, tq), :]
    r = lax.broadcasted_iota(jnp.int32, (tq, tq), 0)
    c = lax.broadcasted_iota(jnp.int32, (tq, tq), 1)
    allowed = (c // CHUNK) <= (r // CHUNK)
    bias = slope * (q0 + r - jnp.abs(r - c)).astype(F32)
    _online_update(jnp.where(allowed, _nt(qa, k) + bias, NEG), v, acc1, m1, l1)
    _online_update(jnp.where(allowed, _nt(qb, k) + bias, NEG), v, acc2, m2, l2)

    lam = _lambda(lqk_ref[...], lam_init)
    o = acc1[...] / l1[...] - lam * (acc2[...] / l2[...])
    o_ref[0] = (_rms(o, hg_ref[...]) * (1.0 - lam_init)).astype(o_ref.dtype)


def _diff_prompt(p3, lqk, head_g, lam_init):
    b, t, _ = p3.shape
    tq = min(t, 512)
    cq, ck, cv = OFF_AQ // LANES, OFF_AK // LANES, OFF_AV // LANES
    slopes = jnp.asarray(ALIBI_SLOPES, F32)
    return pl.pallas_call(
        functools.partial(_diff_prompt_kernel, tq=tq, lam_init=lam_init),
        out_shape=jax.ShapeDtypeStruct((b, t, H_A * DV_A), BF16),
        grid=(b, H_A, t // tq),
        in_specs=[
            pl.BlockSpec(memory_space=pltpu.SMEM),
            pl.BlockSpec((4, DH_A), lambda bi, h, qi: (0, 0)),
            pl.BlockSpec((1, DV_A), lambda bi, h, qi: (0, 0)),
            pl.BlockSpec((1, tq, LANES), lambda bi, h, qi: (bi, qi, cq + h)),
            pl.BlockSpec((1, t, LANES), lambda bi, h, qi: (bi, 0, ck + h)),
            pl.BlockSpec((1, t, LANES), lambda bi, h, qi: (bi, 0, cv + h)),
        ],
        out_specs=pl.BlockSpec((1, tq, LANES), lambda bi, h, qi: (bi, qi, h)),
        scratch_shapes=[
            pltpu.VMEM((t, LANES), BF16), pltpu.VMEM((t, LANES), BF16),
            pltpu.VMEM((tq, LANES), F32), pltpu.VMEM((tq, LANES), F32),
            pltpu.VMEM((tq, LANES), F32), pltpu.VMEM((tq, LANES), F32),
            pltpu.VMEM((tq, LANES), F32), pltpu.VMEM((tq, LANES), F32),
        ],
        compiler_params=_cparams(("parallel", "parallel", "arbitrary")),
        name="diff_prompt",
    )(slopes, lqk, head_g, p3, p3, p3)


def _pieces_attention(qm, pieces):
    ss = [_nt(qm, k) + bias for k, _, bias in pieces]
    m = functools.reduce(jnp.maximum, [jnp.max(s, axis=-1, keepdims=True) for s in ss])
    es = [jnp.exp(s - m) for s in ss]
    l = functools.reduce(jnp.add, [jnp.sum(e, axis=-1, keepdims=True) for e in es])
    o = functools.reduce(jnp.add, [jnp.dot(e.astype(BF16), v, preferred_element_type=F32)
                                   for e, (_, v, _) in zip(es, pieces)])
    return o / l


def _diff_sample_kernel(lqk_ref, hg_ref, q_ref, kc_ref, vc_ref, kn_ref, vn_ref, o_ref, *, lam_init):
    t = q_ref.shape[1]
    past = kc_ref.shape[1]
    lam = _lambda(lqk_ref[...], lam_init)
    rc = lax.broadcasted_iota(jnp.int32, (t, past), 0)
    cc = lax.broadcasted_iota(jnp.int32, (t, past), 1)
    rn = lax.broadcasted_iota(jnp.int32, (t, t), 0)
    cn = lax.broadcasted_iota(jnp.int32, (t, t), 1)
    dist_c = jnp.abs(past + rc - cc).astype(F32)
    dist_n = jnp.abs(rn - cn).astype(F32)
    ok_c = (cc // CHUNK) <= ((past + rc) // CHUNK)
    ok_n = ((past + cn) // CHUNK) <= ((past + rn) // CHUNK)
    for h in range(H_A):
        cols = slice(h * LANES, (h + 1) * LANES)
        bias_c = jnp.where(ok_c, -ALIBI_SLOPES[h] * dist_c, NEG)
        bias_n = jnp.where(ok_n, -ALIBI_SLOPES[h] * dist_n, NEG)
        pieces = [(kc_ref[0, :, cols].astype(BF16), vc_ref[0, :, cols].astype(BF16), bias_c),
                  (kn_ref[0, :, cols].astype(BF16), vn_ref[0, :, cols].astype(BF16), bias_n)]
        qa, qb = _split_halves(q_ref[0, :, cols])
        o = _pieces_attention(qa, pieces) - lam * _pieces_attention(qb, pieces)
        o_ref[0, :, cols] = (_rms(o, hg_ref[...]) * (1.0 - lam_init)).astype(o_ref.dtype)


def _diff_sample(p3, kc, vc, lqk, head_g, lam_init):
    b, t, _ = p3.shape
    past = kc.shape[1]
    w = H_A * DV_A
    blk = lambda off: pl.BlockSpec((1, t, w), lambda bi: (bi, 0, off // w))
    cache = pl.BlockSpec((1, past, w), lambda bi: (bi, 0, 0))
    return pl.pallas_call(
        functools.partial(_diff_sample_kernel, lam_init=lam_init),
        out_shape=jax.ShapeDtypeStruct((b, t, w), BF16),
        grid=(b,),
        in_specs=[
            pl.BlockSpec((4, DH_A), lambda bi: (0, 0)),
            pl.BlockSpec((1, DV_A), lambda bi: (0, 0)),
            blk(OFF_AQ), cache, cache, blk(OFF_AK), blk(OFF_AV),
        ],
        out_specs=pl.BlockSpec((1, t, w), lambda bi: (bi, 0, 0)),
        compiler_params=_cparams(("parallel",)),
        name="diff_sample",
    )(lqk, head_g, p3, kc, vc, p3, p3)


QPAIR = 2 * CHUNK
WIN = BAND + QPAIR


def _relbias_kernel(tab_ref, o_ref):
    h = pl.program_id(0)
    r = lax.broadcasted_iota(jnp.int32, (QPAIR, WIN), 0)
    j = lax.broadcasted_iota(jnp.int32, (QPAIR, WIN), 1)
    rel = jnp.clip(r - j + BAND, -REL_CLIP, REL_CLIP) + REL_CLIP
    qc = r // CHUNK
    kc = j // CHUNK - BAND_CHUNKS
    allowed = (kc <= qc) & (kc >= qc - BAND_CHUNKS)

    def body(t, acc):
        return jnp.where(rel == t, tab_ref[h, t], acc)

    acc = lax.fori_loop(0, 2 * REL_CLIP + 1, body, jnp.zeros((QPAIR, WIN), F32))
    o_ref[0] = jnp.where(allowed, acc, NEG)


def _relbias(table):
    return pl.pallas_call(
        _relbias_kernel,
        out_shape=jax.ShapeDtypeStruct((H_B, QPAIR, WIN), F32),
        grid=(H_B,),
        in_specs=[pl.BlockSpec(memory_space=pltpu.SMEM)],
        out_specs=pl.BlockSpec((1, QPAIR, WIN), lambda h: (h, 0, 0)),
        compiler_params=_cparams(("arbitrary",)),
        name="relbias",
    )(table)


BAND_TQ = 512


def _band_prompt_kernel(bias_ref, q_ref, kp_ref, kc_ref, vp_ref, vc_ref, o_ref, kcat, vcat):
    i = pl.program_id(1)
    kcat[0:BAND_TQ, :] = kp_ref[0].astype(BF16)
    kcat[BAND_TQ:, :] = kc_ref[0].astype(BF16)
    vcat[0:BAND_TQ, :] = vp_ref[0].astype(BF16)
    vcat[BAND_TQ:, :] = vc_ref[0].astype(BF16)
    lane = lax.broadcasted_iota(jnp.int32, (QPAIR, LANES), 1)

    def pair_block(qp, carry, *, first_block):
        r0 = pl.multiple_of(qp * QPAIR, QPAIR)
        q = q_ref[0, pl.ds(r0, QPAIR), :]
        if first_block:
            kpos = r0 - BAND_TQ + lax.broadcasted_iota(jnp.int32, (1, WIN), 1)
            before_start = jnp.where(kpos >= 0, 0.0, NEG)
        scores = []
        for hp in range(H_B // 2):
            cols = slice(hp * LANES, (hp + 1) * LANES)
            qs = jnp.concatenate(_split_halves(q[:, cols]), axis=0)
            s = _nt(qs, kcat[pl.ds(r0, WIN), cols]) + bias_ref[hp]
            scores.append(s + before_start if first_block else s)
        probs = []
        for s in scores:
            e = jnp.exp(s - jnp.max(s, axis=-1, keepdims=True))
            probs.append((e.astype(BF16), jnp.sum(e, axis=-1, keepdims=True)))
        outs = []
        for hp, (e, l) in enumerate(probs):
            cols = slice(hp * LANES, (hp + 1) * LANES)
            o = jnp.dot(e, vcat[pl.ds(r0, WIN), cols], preferred_element_type=F32) / l
            outs.append(jnp.where(lane < DH_B, o[:QPAIR], o[QPAIR:]))
        o_ref[0, pl.ds(r0, QPAIR), :] = jnp.concatenate(outs, axis=1).astype(o_ref.dtype)
        return carry

    @pl.when(i == 0)
    def _():
        lax.fori_loop(0, BAND_TQ // QPAIR, functools.partial(pair_block, first_block=True), 0)

    @pl.when(i > 0)
    def _():
        lax.fori_loop(0, BAND_TQ // QPAIR, functools.partial(pair_block, first_block=False), 0)


def _band_prompt(p3, bias):
    b, t, _ = p3.shape
    assert t % BAND_TQ == 0
    w = H_B * DH_B
    cq, ck, cv = OFF_BQ // w, OFF_BK // w, OFF_BV // w
    prev = lambda col: pl.BlockSpec((1, BAND_TQ, w), lambda bi, i: (bi, jnp.maximum(i - 1, 0), col))
    cur = lambda col: pl.BlockSpec((1, BAND_TQ, w), lambda bi, i: (bi, i, col))
    return pl.pallas_call(
        _band_prompt_kernel,
        out_shape=jax.ShapeDtypeStruct((b, t, w), BF16),
        grid=(b, t // BAND_TQ),
        in_specs=[
            pl.BlockSpec((H_B // 2, 2 * QPAIR, WIN), lambda bi, i: (0, 0, 0)),
            cur(cq), prev(ck), cur(ck), prev(cv), cur(cv),
        ],
        out_specs=pl.BlockSpec((1, BAND_TQ, w), lambda bi, i: (bi, i, 0)),
        scratch_shapes=[pltpu.VMEM((2 * BAND_TQ, w), BF16), pltpu.VMEM((2 * BAND_TQ, w), BF16)],
        compiler_params=_cparams(("parallel", "arbitrary")),
        name="band_prompt",
    )(bias.reshape(H_B // 2, 2 * QPAIR, WIN), p3, p3, p3, p3, p3)


def _band_sample_kernel(bc_ref, bn_ref, q_ref, kc_ref, vc_ref, kn_ref, vn_ref, o_ref, *, past):
    t = q_ref.shape[1]
    nband = kc_ref.shape[1]
    lane = lax.broadcasted_iota(jnp.int32, (t, LANES), 1)
    qpos_c = past + lax.broadcasted_iota(jnp.int32, (t, nband), 0)
    kpos_c = past - nband + lax.broadcasted_iota(jnp.int32, (t, nband), 1)
    qpos_n = past + lax.broadcasted_iota(jnp.int32, (t, t), 0)
    kpos_n = past + lax.broadcasted_iota(jnp.int32, (t, t), 1)

    def allowed(qpos, kpos):
        qc, kc = qpos // CHUNK, kpos // CHUNK
        return (kpos >= 0) & (kc <= qc) & (kc >= qc - BAND_CHUNKS)

    ok_c = allowed(qpos_c, kpos_c)
    ok_n = allowed(qpos_n, kpos_n)
    for hp in range(H_B // 2):
        cols = slice(hp * LANES, (hp + 1) * LANES)
        halves = []
        for sub, qm in enumerate(_split_halves(q_ref[0, :, cols])):
            hd = 2 * hp + sub
            pieces = [(kc_ref[0, :, cols].astype(BF16), vc_ref[0, :, cols].astype(BF16),
                       jnp.where(ok_c, bc_ref[hd], NEG)),
                      (kn_ref[0, :, cols].astype(BF16), vn_ref[0, :, cols].astype(BF16),
                       jnp.where(ok_n, bn_ref[hd], NEG))]
            halves.append(_pieces_attention(qm, pieces))
        o_ref[0, :, cols] = jnp.where(lane < DH_B, halves[0], halves[1]).astype(o_ref.dtype)


def _band_sample(p3, kc, vc, bias, past):
    b, t, _ = p3.shape
    nband = kc.shape[1]
    assert nband == BAND and t <= CHUNK
    w = H_B * DH_B
    bias_c = bias[:, :t, :nband]
    bias_n = bias[:, :t, nband:nband + t]
    blk = lambda off: pl.BlockSpec((1, t, w), lambda bi: (bi, 0, off // w))
    cache = pl.BlockSpec((1, nband, w), lambda bi: (bi, 0, 0))
    return pl.pallas_call(
        functools.partial(_band_sample_kernel, past=past),
        out_shape=jax.ShapeDtypeStruct((b, t, w), BF16),
        grid=(b,),
        in_specs=[
            pl.BlockSpec((H_B, t, nband), lambda bi: (0, 0, 0)),
            pl.BlockSpec((H_B, t, t), lambda bi: (0, 0, 0)),
            blk(OFF_BQ), cache, cache, blk(OFF_BK), blk(OFF_BV),
        ],
        out_specs=pl.BlockSpec((1, t, w), lambda bi: (bi, 0, 0)),
        compiler_params=_cparams(("parallel",)),
        name="band_sample",
    )(bias_c, bias_n, p3, kc, vc, p3, p3)


CONV_PAD = 8
MLSTM_L = 256


def _mlstm_kernel(cqk_ref, cv_ref, co_ref, cif_ref, conv0_ref, cw_ref, cb_ref, gb_ref, hg_ref,
                  c0_ref, n0_ref, m0_ref, hs_ref, c_ref, n_ref, m_ref, xext, *, L):
    @pl.when(pl.program_id(1) == 0)
    def _():
        xext[0:CONV_PAD, :] = conv0_ref[0]
        c_ref[...] = c0_ref[...]
        n_ref[...] = n0_ref[...]
        m_ref[...] = m0_ref[...]

    xext[CONV_PAD:CONV_PAD + L, :] = cqk_ref[0]
    base = CONV_PAD - (CONV_W - 1)
    u = 0.0
    for j in range(CONV_W):
        u = u + xext[base + j:base + j + L, :] * cw_ref[j:j + 1, :]
    u = cb_ref[...] + u
    u = u * jax.nn.sigmoid(u)
    xext[0:CONV_PAD, :] = xext[L:L + CONV_PAD, :]

    z = cif_ref[0] + gb_ref[...]
    lane = lax.broadcasted_iota(jnp.int32, (L, LANES), 1)
    lf = jnp.minimum(z, 0.0) - jnp.log1p(jnp.exp(-jnp.abs(z)))
    tr = lax.broadcasted_iota(jnp.int32, (L, L), 0)
    tc = lax.broadcasted_iota(jnp.int32, (L, L), 1)
    causal = tr >= tc
    b_all = jnp.dot(causal.astype(F32), lf, preferred_element_type=F32,
                    precision=lax.Precision.HIGHEST)
    sr = lax.broadcasted_iota(jnp.int32, (8, LANES), 0)
    sc = lax.broadcasted_iota(jnp.int32, (8, LANES), 1)
    sel = jnp.where(sr < H_C, jnp.where(sc == sr, 1.0, jnp.where(sc == sr + H_C, -1.0, 0.0)), 0.0)
    rt = lax.dot_general(sel, jnp.where(lane < H_C, z, b_all), NT_DIMS,
                         preferred_element_type=F32, precision=lax.Precision.HIGHEST)
    er = lax.broadcasted_iota(jnp.int32, (DH_C, DH_C), 0)
    ec = lax.broadcasted_iota(jnp.int32, (DH_C, DH_C), 1)
    eye = jnp.where(er == ec, 1.0, 0.0).astype(BF16)

    for h in range(H_C):
        cols = slice(h * DH_C, (h + 1) * DH_C)
        q = u[:, h * DH_C:(h + 1) * DH_C]
        k = u[:, D_C + h * DH_C:D_C + (h + 1) * DH_C] * (DH_C ** -0.5)
        v = cv_ref[0, :, cols]
        qb, kb_, vb_ = q.astype(BF16), k.astype(BF16), v.astype(BF16)
        b_col = b_all[:, H_C + h:H_C + h + 1]
        li_col = z[:, h:h + 1]
        m_prev = m_ref[0, h:h + 1, 0:1]
        cs = c_ref[0, h]
        ns = n_ref[0, h:h + 1, :]

        dmat = jnp.where(causal, b_col + rt[h:h + 1, :], NEG)
        inter = b_col + m_prev
        m_t = jnp.maximum(inter, jnp.max(dmat, axis=-1, keepdims=True))
        w_intra = jnp.exp(dmat - m_t)
        w_inter = jnp.exp(inter - m_t)
        a = w_intra * _nt(qb, kb_)
        num = (jnp.dot(a.astype(BF16), vb_, preferred_element_type=F32)
               + w_inter * _nt(qb, cs.astype(BF16)))
        den = jnp.sum(a, axis=-1, keepdims=True) + w_inter * jnp.sum(q * ns, axis=-1, keepdims=True)
        hh = num / jnp.maximum(jnp.abs(den), jnp.exp(-m_t))

        b_last = b_col[L - 1:L, :]
        g = b_last - b_col + li_col
        m_new = jnp.maximum(b_last + m_prev, jnp.max(g, axis=0, keepdims=True))
        ws = jnp.exp(g - m_new)
        decay = jnp.exp(b_last + m_prev - m_new)
        vwt = _nt(eye, (ws * v).astype(BF16)).astype(BF16)
        c_ref[0, h] = decay * cs + jnp.dot(vwt, kb_, preferred_element_type=F32)
        n_ref[0, h:h + 1, :] = decay * ns + jnp.sum(ws * k, axis=0, keepdims=True)
        m_ref[0, h:h + 1, :] = jnp.broadcast_to(m_new, (1, LANES))

        gate = jax.nn.sigmoid(co_ref[0, :, cols])
        hs_ref[0, :, cols] = (_rms(hh, hg_ref[...]) * gate).astype(hs_ref.dtype)


def _mlstm(p3, conv0, conv_w, conv_b, gate_b, head_g, c0, n0, m0, L):
    b, t, _ = p3.shape
    nc = t // L
    blk = lambda off, w: pl.BlockSpec((1, L, w), lambda bi, c: (bi, c, off // w))
    const = lambda shape: pl.BlockSpec(shape, lambda bi, c: (0,) * len(shape))
    per_b = lambda shape: pl.BlockSpec((1,) + shape, lambda bi, c: (bi,) + (0,) * len(shape))
    return pl.pallas_call(
        functools.partial(_mlstm_kernel, L=L),
        out_shape=(jax.ShapeDtypeStruct((b, t, D_C), BF16),
                   jax.ShapeDtypeStruct((b, H_C, DH_C, DH_C), F32),
                   jax.ShapeDtypeStruct((b, H_C, DH_C), F32),
                   jax.ShapeDtypeStruct((b, H_C, LANES), F32)),
        grid=(b, nc),
        in_specs=[
            blk(OFF_CQK, 2 * D_C), blk(OFF_CV, D_C), blk(OFF_CO, D_C), blk(OFF_CIF, LANES),
            per_b((CONV_PAD, 2 * D_C)), const((CONV_W, 2 * D_C)), const((1, 2 * D_C)),
            const((1, LANES)), const((1, DH_C)),
            per_b((H_C, DH_C, DH_C)), per_b((H_C, DH_C)), per_b((H_C, LANES)),
        ],
        out_specs=(pl.BlockSpec((1, L, D_C), lambda bi, c: (bi, c, 0)),
                   per_b((H_C, DH_C, DH_C)), per_b((H_C, DH_C)), per_b((H_C, LANES))),
        scratch_shapes=[pltpu.VMEM((CONV_PAD + L, 2 * D_C), F32)],
        compiler_params=_cparams(("parallel", "arbitrary")),
        name="mlstm",
    )(p3, p3, p3, p3, conv0, conv_w, conv_b, gate_b, head_g, c0, n0, m0)


def _mix_kernel(oa_ref, ob_ref, oc_ref, ga_ref, gb_ref, gc_ref, x_ref, wa_ref, wb_ref, wc_ref, wo_ref, o_ref):
    mixed = (jax.nn.sigmoid(ga_ref[...]) * jnp.dot(oa_ref[...], wa_ref[...], preferred_element_type=F32)
             + jax.nn.sigmoid(gb_ref[...]) * jnp.dot(ob_ref[...], wb_ref[...], preferred_element_type=F32)
             + jax.nn.sigmoid(gc_ref[...]) * jnp.dot(oc_ref[...], wc_ref[...], preferred_element_type=F32))
    o_ref[...] = x_ref[...] + jnp.dot(mixed.astype(BF16), wo_ref[...], preferred_element_type=F32)


def _mix(oa, ob, oc, p2, x, wa, wb, wc, wo):
    m, d = x.shape
    tm = min(m, 512)
    row = lambda w, col=0: pl.BlockSpec((tm, w), lambda i: (i, col))
    full = lambda a: pl.BlockSpec(a.shape, lambda i: (0, 0))
    g0 = OFF_G // d
    return pl.pallas_call(
        _mix_kernel,
        out_shape=jax.ShapeDtypeStruct((m, d), F32),
        grid=(m // tm,),
        in_specs=[row(oa.shape[1]), row(ob.shape[1]), row(oc.shape[1]),
                  row(d, g0), row(d, g0 + 1), row(d, g0 + 2), row(d),
                  full(wa), full(wb), full(wc), full(wo)],
        out_specs=row(d),
        compiler_params=_cparams(("parallel",)),
        name="mix",
    )(oa, ob, oc, p2, p2, p2, x, wa, wb, wc, wo)


def _cross_kernel(x_ref, g_ref, wq_ref, mk_ref, mv_ref, wo_ref, o_ref):
    x = x_ref[0]
    h = _rms(x, g_ref[...]).astype(BF16)
    q = (jnp.dot(h, wq_ref[...], preferred_element_type=F32) * (DH_M ** -0.5)).astype(BF16)
    outs = []
    for hd in range(H_M):
        cols = slice(hd * DH_M, (hd + 1) * DH_M)
        s = _nt(q[:, cols], mk_ref[0, :, cols].astype(BF16))
        m = jnp.max(s, axis=-1, keepdims=True)
        e = jnp.exp(s - m)
        l = jnp.sum(e, axis=-1, keepdims=True)
        o = jnp.dot(e.astype(BF16), mv_ref[0, :, cols].astype(BF16), preferred_element_type=F32) / l
        outs.append(o.astype(BF16))
    o = jnp.concatenate(outs, axis=1)
    o_ref[0] = x + jnp.dot(o, wo_ref[...], preferred_element_type=F32)


def _cross(x3, g, wq, mk, mv, wo):
    b, t, d = x3.shape
    tm = min(t, 512)
    nm = mk.shape[1]
    full = lambda a: pl.BlockSpec(a.shape, lambda bi, i: (0, 0))
    return pl.pallas_call(
        _cross_kernel,
        out_shape=jax.ShapeDtypeStruct((b, t, d), F32),
        grid=(b, t // tm),
        in_specs=[pl.BlockSpec((1, tm, d), lambda bi, i: (bi, i, 0)), full(g), full(wq),
                  pl.BlockSpec((1, nm, d), lambda bi, i: (bi, 0, 0)),
                  pl.BlockSpec((1, nm, d), lambda bi, i: (bi, 0, 0)), full(wo)],
        out_specs=pl.BlockSpec((1, tm, d), lambda bi, i: (bi, i, 0)),
        compiler_params=_cparams(("parallel", "parallel")),
        name="cross",
    )(x3, g, wq, mk, mv, wo)


def _ffn_kernel(x_ref, g_ref, wg_ref, wu_ref, wd_ref, gf_ref, o_ref, h_scr, acc, *, final_norm):
    j = pl.program_id(1)

    @pl.when(j == 0)
    def _():
        h_scr[...] = _rms(x_ref[...], g_ref[...]).astype(BF16)
        acc[...] = jnp.zeros(acc.shape, F32)

    h = h_scr[...]
    gate = jnp.dot(h, wg_ref[...], preferred_element_type=F32)
    up = jnp.dot(h, wu_ref[...], preferred_element_type=F32)
    a = (gate * jax.nn.sigmoid(gate) * up).astype(BF16)
    acc[...] += jnp.dot(a, wd_ref[...], preferred_element_type=F32)

    @pl.when(j == pl.num_programs(1) - 1)
    def _():
        y = x_ref[...] + acc[...]
        if final_norm:
            y = _rms(y, gf_ref[...])
        o_ref[...] = y


FFN_TF = 256


def _ffn(x, g, wg, wu, wd, g_final, final_norm):
    m, d = x.shape
    tm = min(m, 1024)
    return pl.pallas_call(
        functools.partial(_ffn_kernel, final_norm=final_norm),
        out_shape=jax.ShapeDtypeStruct((m, d), F32),
        grid=(m // tm, D_FF // FFN_TF),
        in_specs=[
            pl.BlockSpec((tm, d), lambda i, j: (i, 0)),
            pl.BlockSpec((1, d), lambda i, j: (0, 0)),
            pl.BlockSpec((d, FFN_TF), lambda i, j: (0, j)),
            pl.BlockSpec((d, FFN_TF), lambda i, j: (0, j)),
            pl.BlockSpec((FFN_TF, d), lambda i, j: (j, 0)),
            pl.BlockSpec((1, d), lambda i, j: (0, 0)),
        ],
        out_specs=pl.BlockSpec((tm, d), lambda i, j: (i, 0)),
        scratch_shapes=[pltpu.VMEM((tm, d), BF16), pltpu.VMEM((tm, d), F32)],
        compiler_params=_cparams(("parallel", "arbitrary")),
        name="ffn",
    )(x, g, wg, wu, wd, g_final)


def _interleave_heads(a, b, heads):
    k = a.shape[0]
    return jnp.concatenate([a.reshape(k, heads, -1), b.reshape(k, heads, -1)], axis=-1).reshape(k, -1)


def _prep_w_in(w):
    sizes = (H_A * DH_A,) * 4 + (H_A * DV_A,) + (H_B * DH_B,) * 3 + (2 * D_C, D_C, D_C, 2 * H_C, 3 * D_MODEL)
    offs = [0]
    for s in sizes:
        offs.append(offs[-1] + s)
    (a_q1, a_q2, a_k1, a_k2, a_v, b_q, b_k, b_v, c_qk, c_v, c_o, c_if, g) = (
        w[:, offs[i]:offs[i + 1]] for i in range(len(sizes)))
    scale_a = DH_A ** -0.5
    scale_b = DH_B ** -0.5
    pad_if = jnp.zeros((w.shape[0], LANES - 2 * H_C), w.dtype)
    tail = jnp.zeros((w.shape[0], N_PROJ - OFF_CIF - LANES), w.dtype)
    cols = [g, c_qk, _interleave_heads(a_q1, a_q2, H_A) * scale_a, _interleave_heads(a_k1, a_k2, H_A), a_v,
            b_q * scale_b, b_k, b_v, c_v, c_o, c_if, pad_if, tail]
    return jnp.concatenate(cols, axis=1).astype(BF16)


def _layer(x3, l, w, mem_k, mem_v, cache, final_norm):
    b, t, d = x3.shape
    x2 = x3.reshape(b * t, d)
    row = lambda a: a.reshape(1, -1)
    lam_init = 0.8 - 0.6 * math.exp(-0.3 * l)

    p2 = _proj(x2, row(w['g_mix']), w['w_in'], norm=True, tn=PROJ_TN)
    p3 = p2.reshape(b, t, N_PROJ)
    a_k = p3[:, :, OFF_AK:OFF_AK + 512].reshape(b, t, H_A, 2 * DH_A)
    a_v = p3[:, :, OFF_AV:OFF_AV + 512].reshape(b, t, H_A, DV_A)
    b_k = p3[:, :, OFF_BK:OFF_BK + 512].reshape(b, t, H_B, DH_B)
    b_v = p3[:, :, OFF_BV:OFF_BV + 512].reshape(b, t, H_B, DH_B)

    gate_b = jnp.concatenate([w['b_i'], w['b_f'], jnp.zeros((LANES - 2 * H_C,), F32)]).reshape(1, LANES)
    if cache is None:
        oa = _diff_prompt(p3, w['lqk'], row(w['a_head_g']), lam_init)
        ob = _band_prompt(p3, w['relbias'])
        conv0 = jnp.zeros((b, CONV_PAD, 2 * D_C), F32)
        c0 = jnp.zeros((b, H_C, DH_C, DH_C), F32)
        n0 = jnp.zeros((b, H_C, DH_C), F32)
        m0 = jnp.zeros((b, H_C, LANES), F32)
        keep = min(BAND, t)
        new_b = (b_k[:, t - keep:], b_v[:, t - keep:])
        L = min(t, MLSTM_L)
    else:
        past = cache['a_k'].shape[1]
        oa = _diff_sample(p3, cache['a_k'].reshape(b, past, -1), cache['a_v'].reshape(b, past, -1),
                          w['lqk'], row(w['a_head_g']), lam_init)
        nband = cache['b_k'].shape[1]
        ob = _band_sample(p3, cache['b_k'].reshape(b, nband, -1), cache['b_v'].reshape(b, nband, -1),
                          w['relbias'], past)
        conv0 = jnp.pad(cache['conv'], ((0, 0), (CONV_PAD - (CONV_W - 1), 0), (0, 0)))
        c0, n0 = cache['C'], cache['n']
        m0 = jnp.broadcast_to(cache['m'][:, :, None], (b, H_C, LANES))
        new_b = (b_k, b_v)
        L = t
    oc, c_new, n_new, m_new = _mlstm(p3, conv0, w['conv_w'], row(w['conv_b']), gate_b, row(w['c_head_g']),
                                     c0, n0, m0, L)
    assert t >= CONV_W - 1
    conv_new = p3[:, t - (CONV_W - 1):, OFF_CQK:OFF_CQK + 2 * D_C]

    x2 = _mix(oa.reshape(b * t, -1), ob.reshape(b * t, -1), oc.reshape(b * t, -1), p2, x2,
              w['w_up_a'], w['w_up_b'], w['w_up_c'], w['w_o'])
    x3 = _cross(x2.reshape(b, t, d), row(w['g_cross']), w['w_mq'], mem_k, mem_v, w['w_mo'])
    x2 = _ffn(x3.reshape(b * t, d), row(w['g_ffn']), w['w_ff_g'], w['w_ff_u'], w['w_ff_d'],
              row(w['g_final']), final_norm)
    return x2.reshape(b, t, d), (a_k, a_v) + new_b + (c_new, n_new, m_new[:, :, 0], conv_new)


def kernel(x_prompt, x_sample, cache_a_k, cache_a_v, cache_b_k, cache_b_v, state_c_C, state_c_n, state_c_m, state_c_conv, cache_mem_k, cache_mem_v, mem_prompt, g_mix, w_in, a_lq1, a_lk1, a_lq2, a_lk2, a_head_g, b_rel, c_conv_w, c_conv_b, c_b_i, c_b_f, c_head_g, w_up_a, w_up_b, w_up_c, w_o, g_cross, w_mq, w_mk, w_mv, w_mo, g_ffn, w_ff_g, w_ff_u, w_ff_d, g_final):
    xp, xs = x_prompt, x_sample
    bp = x_prompt.shape[0]
    n_mem = mem_prompt.shape[1]
    mem2 = mem_prompt.reshape(bp * n_mem, D_MODEL)
    new_p = [[] for _ in range(10)]
    new_s = [[] for _ in range(8)]
    bf = lambda a: a.astype(BF16)
    for l in range(DEPTH):
        w = dict(g_mix=g_mix[l], w_in=_prep_w_in(w_in[l]),
                 lqk=jnp.stack([a_lq1[l], a_lk1[l], a_lq2[l], a_lk2[l]]),
                 a_head_g=a_head_g[l], relbias=_relbias(b_rel[l]), conv_w=c_conv_w[l], conv_b=c_conv_b[l],
                 b_i=c_b_i[l], b_f=c_b_f[l], c_head_g=c_head_g[l],
                 w_up_a=bf(w_up_a[l]), w_up_b=bf(w_up_b[l]), w_up_c=bf(w_up_c[l]), w_o=bf(w_o[l]),
                 g_cross=g_cross[l], w_mq=bf(w_mq[l]), w_mo=bf(w_mo[l]), g_ffn=g_ffn[l],
                 w_ff_g=bf(w_ff_g[l]), w_ff_u=bf(w_ff_u[l]), w_ff_d=bf(w_ff_d[l]), g_final=g_final)
        w_mkv = bf(jnp.concatenate([w_mk[l], w_mv[l]], axis=1))
        mkv = _proj(mem2, jnp.ones((1, D_MODEL), F32), w_mkv, norm=False, tn=512)
        mk2 = mkv[:, :D_MODEL].reshape(bp, n_mem, D_MODEL)
        mv2 = mkv[:, D_MODEL:].reshape(bp, n_mem, D_MODEL)
        last = l == DEPTH - 1
        xp, st_p = _layer(xp, l, w, mk2, mv2, None, last)
        cache = dict(a_k=cache_a_k[l], a_v=cache_a_v[l], b_k=cache_b_k[l], b_v=cache_b_v[l],
                     C=state_c_C[l], n=state_c_n[l], m=state_c_m[l], conv=state_c_conv[l])
        bs = xs.shape[0]
        xs, st_s = _layer(xs, l, w, cache_mem_k[l].reshape(bs, n_mem, D_MODEL),
                          cache_mem_v[l].reshape(bs, n_mem, D_MODEL), cache, last)
        mk4 = mk2.reshape(bp, n_mem, H_M, DH_M)
        mv4 = mv2.reshape(bp, n_mem, H_M, DH_M)
        for lst, a in zip(new_p, st_p + (mk4, mv4)):
            lst.append(a)
        for lst, a in zip(new_s, st_s):
            lst.append(a)
    outs_p = [jnp.stack(a, 0) for a in new_p]
    outs_s = [jnp.stack(a, 0) for a in new_s]
    return (xp, xs) + tuple(outs_p) + tuple(outs_s)
```

```python
import functools
import math

import jax
import jax.numpy as jnp
from jax import lax
from jax.experimental import pallas as pl
from jax.experimental.pallas import tpu as pltpu

F32 = jnp.float32
BF16 = jnp.bfloat16

D_MODEL = 1024
DEPTH = 2
CHUNK = 64
EPS = 1e-6
NEG = -1e30
H_A = 4
DH_A = 64
DV_A = 2 * DH_A
H_B = 8
DH_B = 64
BAND_CHUNKS = 8
BAND = BAND_CHUNKS * CHUNK
REL_CLIP = 128
H_C = 4
DH_C = 128
D_C = H_C * DH_C
CONV_W = 4
H_M = 4
DH_M = D_MODEL // H_M
D_FF = -(-8 * D_MODEL // (3 * 256)) * 256

LANES = 128
VMEM_LIMIT = 48 * 1024 * 1024

SEG = 1024
HALF = SEG // 2
GATE_TILES = 3
N_TILES_A = GATE_TILES + 1
N_TILES_B = 4
IN_SIZES = (H_A * DH_A,) * 4 + (H_A * DV_A,) + (H_B * DH_B,) * 3 + (2 * D_C, D_C, D_C, 2 * H_C, 3 * D_MODEL)

ALIBI_SLOPES = tuple(2.0 ** (-8.0 * (i + 1) / H_A) for i in range(H_A))

NT_DIMS = (((1,), (1,)), ((), ()))


def _cparams(sem):
    return pltpu.CompilerParams(dimension_semantics=sem, vmem_limit_bytes=VMEM_LIMIT)


def _rms(x, g):
    ms = jnp.mean(x * x, axis=-1, keepdims=True)
    return x * lax.rsqrt(ms + EPS) * g


def _nt(a, b):
    return lax.dot_general(a, b, NT_DIMS, preferred_element_type=F32)


def _prep_w_in_kernel(w_ref, o_ref, oif_ref):
    offs = [0]
    for size in IN_SIZES:
        offs.append(offs[-1] + size)
    seg = lambda i: w_ref[:, offs[i]:offs[i + 1]]
    a_q1, a_q2, a_k1, a_k2, a_v, b_q, b_k, b_v, c_qk, c_v, c_o = (seg(i) for i in range(11))
    g = seg(12)

    def put(col, val):
        o_ref[:, col:col + val.shape[1]] = val.astype(BF16)

    put(0, g)
    put(GATE_TILES * SEG, c_qk)
    for h in range(H_A):
        hs = slice(h * DH_A, (h + 1) * DH_A)
        base = h * 2 * DH_A
        put(4 * SEG + base, a_q1[:, hs] * (DH_A ** -0.5))
        put(4 * SEG + base + DH_A, a_q2[:, hs] * (DH_A ** -0.5))
        put(5 * SEG + base, a_k1[:, hs])
        put(5 * SEG + base + DH_A, a_k2[:, hs])
    put(4 * SEG + HALF, b_q * (DH_B ** -0.5))
    put(5 * SEG + HALF, a_v)
    put(6 * SEG, b_k)
    put(6 * SEG + HALF, b_v)
    put(7 * SEG, c_v)
    put(7 * SEG + HALF, c_o)
    tile = w_ref[:, offs[11]:offs[11] + LANES]
    lane = lax.broadcasted_iota(jnp.int32, tile.shape, 1)
    oif_ref[...] = jnp.where(lane < 2 * H_C, tile, 0.0).astype(BF16)


def _prep_w_in(w):
    k, n = w.shape
    tk = 256
    return pl.pallas_call(
        _prep_w_in_kernel,
        out_shape=(jax.ShapeDtypeStruct((k, (N_TILES_A + N_TILES_B) * SEG), BF16),
                   jax.ShapeDtypeStruct((k, LANES), BF16)),
        grid=(k // tk,),
        in_specs=[pl.BlockSpec((tk, n), lambda i: (i, 0))],
        out_specs=(pl.BlockSpec((tk, (N_TILES_A + N_TILES_B) * SEG), lambda i: (i, 0)),
                   pl.BlockSpec((tk, LANES), lambda i: (i, 0))),
        compiler_params=_cparams(("parallel",)),
        name="prep_w_in",
    )(w)


def _norm_to_scratch(x_ref, g_ref, h_scr):
    h_scr[...] = _rms(x_ref[...], g_ref[...]).astype(BF16)


def _in_proj_a_kernel(x_ref, g_ref, w_ref, wif_ref, og_ref, ocqk_ref, ocif_ref, h_scr):
    j = pl.program_id(1)

    @pl.when(j == 0)
    def _():
        _norm_to_scratch(x_ref, g_ref, h_scr)
        ocif_ref[...] = jnp.dot(h_scr[...], wif_ref[...], preferred_element_type=F32)

    acc = jnp.dot(h_scr[...], w_ref[...], preferred_element_type=F32)

    @pl.when(j < GATE_TILES)
    def _():
        og_ref[...] = jax.nn.sigmoid(acc).astype(og_ref.dtype)

    @pl.when(j == GATE_TILES)
    def _():
        ocqk_ref[...] = acc


def _in_proj_a(x, g, wp, wif):
    m, k = x.shape
    tm = min(m, 1024)
    return pl.pallas_call(
        _in_proj_a_kernel,
        out_shape=(jax.ShapeDtypeStruct((m, GATE_TILES * SEG), BF16),
                   jax.ShapeDtypeStruct((m, SEG), F32),
                   jax.ShapeDtypeStruct((m, LANES), F32)),
        grid=(m // tm, N_TILES_A),
        in_specs=[
            pl.BlockSpec((tm, k), lambda i, j: (i, 0)),
            pl.BlockSpec((1, k), lambda i, j: (0, 0)),
            pl.BlockSpec((k, SEG), lambda i, j: (0, j)),
            pl.BlockSpec((k, LANES), lambda i, j: (0, 0)),
        ],
        out_specs=(pl.BlockSpec((tm, SEG), lambda i, j: (i, jnp.minimum(j, GATE_TILES - 1))),
                   pl.BlockSpec((tm, SEG), lambda i, j: (i, 0)),
                   pl.BlockSpec((tm, LANES), lambda i, j: (i, 0))),
        scratch_shapes=[pltpu.VMEM((tm, k), BF16)],
        compiler_params=_cparams(("parallel", "arbitrary")),
        name="in_proj_a",
    )(x, g, wp, wif)


def _in_proj_b_kernel(*refs, aliased):
    x_ref, g_ref, w_ref = refs[:3]
    oq_ref, oak_ref, oav_ref, obkv_ref, ocvo_ref, h_scr = refs[5 if aliased else 3:]
    j = pl.program_id(1)

    @pl.when(j == 0)
    def _():
        _norm_to_scratch(x_ref, g_ref, h_scr)

    acc = jnp.dot(h_scr[...], w_ref[...], preferred_element_type=F32)

    @pl.when(j == 0)
    def _():
        oq_ref[...] = acc.astype(oq_ref.dtype)

    @pl.when(j == 1)
    def _():
        oak_ref[0] = acc[:, :HALF]
        oav_ref[0] = acc[:, HALF:]

    @pl.when(j == 2)
    def _():
        obkv_ref[...] = acc

    @pl.when(j == 3)
    def _():
        ocvo_ref[...] = acc


def _in_proj_b(x, g, wp, l, ak_all, av_all):
    m, k = x.shape
    tm = min(m, 1024)
    aliased = ak_all is not None
    stacked = jax.ShapeDtypeStruct((DEPTH, m, HALF), F32)
    in_specs = [
        pl.BlockSpec((tm, k), lambda i, j: (i, 0)),
        pl.BlockSpec((1, k), lambda i, j: (0, 0)),
        pl.BlockSpec((k, SEG), lambda i, j: (0, N_TILES_A + j)),
    ]
    args = [x, g, wp]
    if aliased:
        in_specs += [pl.BlockSpec(memory_space=pl.ANY)] * 2
        args += [ak_all, av_all]
    row = lambda w: pl.BlockSpec((tm, w), lambda i, j: (i, 0))
    layer_row = pl.BlockSpec((1, tm, HALF), lambda i, j: (l, i, 0))
    return pl.pallas_call(
        functools.partial(_in_proj_b_kernel, aliased=aliased),
        out_shape=(jax.ShapeDtypeStruct((m, SEG), BF16), stacked, stacked,
                   jax.ShapeDtypeStruct((m, SEG), F32), jax.ShapeDtypeStruct((m, SEG), F32)),
        grid=(m // tm, N_TILES_B),
        in_specs=in_specs,
        out_specs=(row(SEG), layer_row, layer_row, row(SEG), row(SEG)),
        scratch_shapes=[pltpu.VMEM((tm, k), BF16)],
        input_output_aliases={3: 1, 4: 2} if aliased else {},
        compiler_params=_cparams(("parallel", "arbitrary")),
        name="in_proj_b",
    )(*args)


def _mem_proj_kernel(x_ref, w_ref, o_ref):
    o_ref[...] = jnp.dot(x_ref[...].astype(BF16), w_ref[...], preferred_element_type=F32)


def _mem_proj(x, w):
    m, k = x.shape
    n = w.shape[1]
    tm = min(m, 1024)
    return pl.pallas_call(
        _mem_proj_kernel,
        out_shape=jax.ShapeDtypeStruct((m, n), F32),
        grid=(m // tm,),
        in_specs=[pl.BlockSpec((tm, k), lambda i: (i, 0)), pl.BlockSpec((k, n), lambda i: (0, 0))],
        out_specs=pl.BlockSpec((tm, n), lambda i: (i, 0)),
        compiler_params=_cparams(("parallel",)),
        name="mem_proj",
    )(x, w)


def _online_update(s, v, acc, m, l):
    slabs = [s[:, j * LANES:(j + 1) * LANES] for j in range(s.shape[1] // LANES)]
    m_old = m[...]
    m_new = jnp.maximum(m_old, jnp.max(functools.reduce(jnp.maximum, slabs), axis=-1, keepdims=True))
    alpha = jnp.exp(m_old - m_new)
    ps = [jnp.exp(c - m_new) for c in slabs]
    l[...] = alpha * l[...] + jnp.sum(functools.reduce(jnp.add, ps), axis=-1, keepdims=True)
    p = jnp.concatenate(ps, axis=1).astype(BF16)
    acc[...] = alpha * acc[...] + jnp.dot(p, v, preferred_element_type=F32)
    m[...] = m_new


def _split_halves(q):
    lane = lax.broadcasted_iota(jnp.int32, q.shape, 1)
    lo = jnp.where(lane < DH_A, q, 0.0).astype(BF16)
    hi = jnp.where(lane >= DH_A, q, 0.0).astype(BF16)
    return lo, hi


def _lambda(lqk, lam_init):
    e1 = jnp.exp(jnp.sum(lqk[0:1, :] * lqk[1:2, :], axis=-1, keepdims=True))
    e2 = jnp.exp(jnp.sum(lqk[2:3, :] * lqk[3:4, :], axis=-1, keepdims=True))
    return e1 - e2 + lam_init


def _diff_prompt_kernel(slopes_ref, lqk_ref, hg_ref, q_ref, k_ref, v_ref, o_ref,
                        kb, vb, acc1, acc2, m1, l1, m2, l2, *, tq, lam_init):
    h = pl.program_id(1)
    qi = pl.program_id(2)
    slope = slopes_ref[h]

    @pl.when(qi == 0)
    def _():
        kb[...] = k_ref[0].astype(BF16)
        vb[...] = v_ref[0].astype(BF16)

    qa, qb = _split_halves(q_ref[0])
    for m, l, acc in ((m1, l1, acc1), (m2, l2, acc2)):
        m[...] = jnp.full(m.shape, NEG, F32)
        l[...] = jnp.zeros(l.shape, F32)
        acc[...] = jnp.zeros(acc.shape, F32)

    def full_tile(ki, carry):
        k0 = pl.multiple_of(ki * tq, tq)
        k = kb[pl.ds(k0, tq), :]
        v = vb[pl.ds(k0, tq), :]
        kpos = k0 + lax.broadcasted_iota(jnp.int32, (1, tq), 1)
        bias = slope * kpos.astype(F32)
        _online_update(_nt(qa, k) + bias, v, acc1, m1, l1)
        _online_update(_nt(qb, k) + bias, v, acc2, m2, l2)
        return carry

    lax.fori_loop(0, qi, full_tile, 0)

    q0 = pl.multiple_of(qi * tq, tq)
    k = kb[pl.ds(q0, tq), :]
    v = vb[pl.ds(q0, tq), :]
    r = lax.broadcasted_iota(jnp.int32, (tq, tq), 0)
    c = lax.broadcasted_iota(jnp.int32, (tq, tq), 1)
    allowed = (c // CHUNK) <= (r // CHUNK)
    bias = slope * (q0 + r - jnp.abs(r - c)).astype(F32)
    _online_update(jnp.where(allowed, _nt(qa, k) + bias, NEG), v, acc1, m1, l1)
    _online_update(jnp.where(allowed, _nt(qb, k) + bias, NEG), v, acc2, m2, l2)

    lam = _lambda(lqk_ref[...], lam_init)
    o = acc1[...] / l1[...] - lam * (acc2[...] / l2[...])
    o_ref[0] = (_rms(o, hg_ref[...]) * (1.0 - lam_init)).astype(o_ref.dtype)


def _diff_prompt(q3, k3, v3, l, lqk, head_g, lam_init):
    b, t, _ = q3.shape
    tq = min(t, 512)
    slopes = jnp.asarray(ALIBI_SLOPES, F32)
    return pl.pallas_call(
        functools.partial(_diff_prompt_kernel, tq=tq, lam_init=lam_init),
        out_shape=jax.ShapeDtypeStruct((b, t, H_A * DV_A), BF16),
        grid=(b, H_A, t // tq),
        in_specs=[
            pl.BlockSpec(memory_space=pltpu.SMEM),
            pl.BlockSpec((4, DH_A), lambda bi, h, qi: (0, 0)),
            pl.BlockSpec((1, DV_A), lambda bi, h, qi: (0, 0)),
            pl.BlockSpec((1, tq, LANES), lambda bi, h, qi: (bi, qi, h)),
            pl.BlockSpec((1, t, LANES), lambda bi, h, qi: (l * b + bi, 0, h)),
            pl.BlockSpec((1, t, LANES), lambda bi, h, qi: (l * b + bi, 0, h)),
        ],
        out_specs=pl.BlockSpec((1, tq, LANES), lambda bi, h, qi: (bi, qi, h)),
        scratch_shapes=[
            pltpu.VMEM((t, LANES), BF16), pltpu.VMEM((t, LANES), BF16),
            pltpu.VMEM((tq, LANES), F32), pltpu.VMEM((tq, LANES), F32),
            pltpu.VMEM((tq, LANES), F32), pltpu.VMEM((tq, LANES), F32),
            pltpu.VMEM((tq, LANES), F32), pltpu.VMEM((tq, LANES), F32),
        ],
        compiler_params=_cparams(("parallel", "parallel", "arbitrary")),
        name="diff_prompt",
    )(slopes, lqk, head_g, q3, k3, v3)


def _pieces_attention(qm, pieces):
    ss = [_nt(qm, k) + bias for k, _, bias in pieces]
    m = functools.reduce(jnp.maximum, [jnp.max(s, axis=-1, keepdims=True) for s in ss])
    es = [jnp.exp(s - m) for s in ss]
    l = functools.reduce(jnp.add, [jnp.sum(e, axis=-1, keepdims=True) for e in es])
    o = functools.reduce(jnp.add, [jnp.dot(e.astype(BF16), v, preferred_element_type=F32)
                                   for e, (_, v, _) in zip(es, pieces)])
    return o / l


def _diff_sample_kernel(lqk_ref, hg_ref, q_ref, kc_ref, vc_ref, kn_ref, vn_ref, o_ref, *, lam_init):
    t = q_ref.shape[1]
    past = kc_ref.shape[1]
    lam = _lambda(lqk_ref[...], lam_init)
    rc = lax.broadcasted_iota(jnp.int32, (t, past), 0)
    cc = lax.broadcasted_iota(jnp.int32, (t, past), 1)
    rn = lax.broadcasted_iota(jnp.int32, (t, t), 0)
    cn = lax.broadcasted_iota(jnp.int32, (t, t), 1)
    dist_c = jnp.abs(past + rc - cc).astype(F32)
    dist_n = jnp.abs(rn - cn).astype(F32)
    ok_c = (cc // CHUNK) <= ((past + rc) // CHUNK)
    ok_n = ((past + cn) // CHUNK) <= ((past + rn) // CHUNK)
    for h in range(H_A):
        cols = slice(h * LANES, (h + 1) * LANES)
        bias_c = jnp.where(ok_c, -ALIBI_SLOPES[h] * dist_c, NEG)
        bias_n = jnp.where(ok_n, -ALIBI_SLOPES[h] * dist_n, NEG)
        pieces = [(kc_ref[0, :, cols].astype(BF16), vc_ref[0, :, cols].astype(BF16), bias_c),
                  (kn_ref[0, :, cols].astype(BF16), vn_ref[0, :, cols].astype(BF16), bias_n)]
        qa, qb = _split_halves(q_ref[0, :, cols])
        o = _pieces_attention(qa, pieces) - lam * _pieces_attention(qb, pieces)
        o_ref[0, :, cols] = (_rms(o, hg_ref[...]) * (1.0 - lam_init)).astype(o_ref.dtype)


def _diff_sample(q3, kn3, vn3, l, kc, vc, lqk, head_g, lam_init):
    b, t, _ = q3.shape
    past = kc.shape[1]
    w = H_A * DV_A
    new = pl.BlockSpec((1, t, w), lambda bi: (l * b + bi, 0, 0))
    cache = pl.BlockSpec((1, past, w), lambda bi: (bi, 0, 0))
    return pl.pallas_call(
        functools.partial(_diff_sample_kernel, lam_init=lam_init),
        out_shape=jax.ShapeDtypeStruct((b, t, w), BF16),
        grid=(b,),
        in_specs=[
            pl.BlockSpec((4, DH_A), lambda bi: (0, 0)),
            pl.BlockSpec((1, DV_A), lambda bi: (0, 0)),
            pl.BlockSpec((1, t, w), lambda bi: (bi, 0, 0)), cache, cache, new, new,
        ],
        out_specs=pl.BlockSpec((1, t, w), lambda bi: (bi, 0, 0)),
        compiler_params=_cparams(("parallel",)),
        name="diff_sample",
    )(lqk, head_g, q3, kc, vc, kn3, vn3)


QPAIR = 2 * CHUNK
WIN = BAND + QPAIR


def _relbias_kernel(tab_ref, o_ref):
    h = pl.program_id(0)
    r = lax.broadcasted_iota(jnp.int32, (QPAIR, WIN), 0)
    j = lax.broadcasted_iota(jnp.int32, (QPAIR, WIN), 1)
    rel = jnp.clip(r - j + BAND, -REL_CLIP, REL_CLIP) + REL_CLIP
    qc = r // CHUNK
    kc = j // CHUNK - BAND_CHUNKS
    allowed = (kc <= qc) & (kc >= qc - BAND_CHUNKS)

    def body(t, acc):
        return jnp.where(rel == t, tab_ref[h, t], acc)

    acc = lax.fori_loop(0, 2 * REL_CLIP + 1, body, jnp.zeros((QPAIR, WIN), F32))
    o_ref[0] = jnp.where(allowed, acc, NEG)


def _relbias(table):
    return pl.pallas_call(
        _relbias_kernel,
        out_shape=jax.ShapeDtypeStruct((H_B, QPAIR, WIN), F32),
        grid=(H_B,),
        in_specs=[pl.BlockSpec(memory_space=pltpu.SMEM)],
        out_specs=pl.BlockSpec((1, QPAIR, WIN), lambda h: (h, 0, 0)),
        compiler_params=_cparams(("arbitrary",)),
        name="relbias",
    )(table)


BAND_TQ = 512


def _band_prompt_kernel(bias_ref, q_ref, kp_ref, kc_ref, vp_ref, vc_ref, o_ref, kcat, vcat):
    i = pl.program_id(1)
    kcat[0:BAND_TQ, :] = kp_ref[0].astype(BF16)
    kcat[BAND_TQ:, :] = kc_ref[0].astype(BF16)
    vcat[0:BAND_TQ, :] = vp_ref[0].astype(BF16)
    vcat[BAND_TQ:, :] = vc_ref[0].astype(BF16)
    lane = lax.broadcasted_iota(jnp.int32, (QPAIR, LANES), 1)

    def pair_block(qp, carry, *, first_block):
        r0 = pl.multiple_of(qp * QPAIR, QPAIR)
        q = q_ref[0, pl.ds(r0, QPAIR), :]
        if first_block:
            kpos = r0 - BAND_TQ + lax.broadcasted_iota(jnp.int32, (1, WIN), 1)
            before_start = jnp.where(kpos >= 0, 0.0, NEG)
        scores = []
        for hp in range(H_B // 2):
            cols = slice(hp * LANES, (hp + 1) * LANES)
            qs = jnp.concatenate(_split_halves(q[:, cols]), axis=0)
            s = _nt(qs, kcat[pl.ds(r0, WIN), cols]) + bias_ref[hp]
            scores.append(s + before_start if first_block else s)
        probs = []
        for s in scores:
            e = jnp.exp(s - jnp.max(s, axis=-1, keepdims=True))
            probs.append((e.astype(BF16), jnp.sum(e, axis=-1, keepdims=True)))
        outs = []
        for hp, (e, l) in enumerate(probs):
            cols = slice(hp * LANES, (hp + 1) * LANES)
            o = jnp.dot(e, vcat[pl.ds(r0, WIN), cols], preferred_element_type=F32) / l
            outs.append(jnp.where(lane < DH_B, o[:QPAIR], o[QPAIR:]))
        o_ref[0, pl.ds(r0, QPAIR), :] = jnp.concatenate(outs, axis=1).astype(o_ref.dtype)
        return carry

    @pl.when(i == 0)
    def _():
        lax.fori_loop(0, BAND_TQ // QPAIR, functools.partial(pair_block, first_block=True), 0)

    @pl.when(i > 0)
    def _():
        lax.fori_loop(0, BAND_TQ // QPAIR, functools.partial(pair_block, first_block=False), 0)


def _band_prompt(q3, kv3, bias):
    b, t, _ = q3.shape
    assert t % BAND_TQ == 0
    w = H_B * DH_B
    cq, ck, cv = 1, 0, 1
    prev = lambda col: pl.BlockSpec((1, BAND_TQ, w), lambda bi, i: (bi, jnp.maximum(i - 1, 0), col))
    cur = lambda col: pl.BlockSpec((1, BAND_TQ, w), lambda bi, i: (bi, i, col))
    return pl.pallas_call(
        _band_prompt_kernel,
        out_shape=jax.ShapeDtypeStruct((b, t, w), BF16),
        grid=(b, t // BAND_TQ),
        in_specs=[
            pl.BlockSpec((H_B // 2, 2 * QPAIR, WIN), lambda bi, i: (0, 0, 0)),
            cur(cq), prev(ck), cur(ck), prev(cv), cur(cv),
        ],
        out_specs=pl.BlockSpec((1, BAND_TQ, w), lambda bi, i: (bi, i, 0)),
        scratch_shapes=[pltpu.VMEM((2 * BAND_TQ, w), BF16), pltpu.VMEM((2 * BAND_TQ, w), BF16)],
        compiler_params=_cparams(("parallel", "arbitrary")),
        name="band_prompt",
    )(bias.reshape(H_B // 2, 2 * QPAIR, WIN), q3, kv3, kv3, kv3, kv3)


def _band_sample_kernel(bc_ref, bn_ref, q_ref, kc_ref, vc_ref, kn_ref, vn_ref, o_ref, *, past):
    t = q_ref.shape[1]
    nband = kc_ref.shape[1]
    lane = lax.broadcasted_iota(jnp.int32, (t, LANES), 1)
    qpos_c = past + lax.broadcasted_iota(jnp.int32, (t, nband), 0)
    kpos_c = past - nband + lax.broadcasted_iota(jnp.int32, (t, nband), 1)
    qpos_n = past + lax.broadcasted_iota(jnp.int32, (t, t), 0)
    kpos_n = past + lax.broadcasted_iota(jnp.int32, (t, t), 1)

    def allowed(qpos, kpos):
        qc, kc = qpos // CHUNK, kpos // CHUNK
        return (kpos >= 0) & (kc <= qc) & (kc >= qc - BAND_CHUNKS)

    ok_c = allowed(qpos_c, kpos_c)
    ok_n = allowed(qpos_n, kpos_n)
    for hp in range(H_B // 2):
        cols = slice(hp * LANES, (hp + 1) * LANES)
        halves = []
        for sub, qm in enumerate(_split_halves(q_ref[0, :, cols])):
            hd = 2 * hp + sub
            pieces = [(kc_ref[0, :, cols].astype(BF16), vc_ref[0, :, cols].astype(BF16),
                       jnp.where(ok_c, bc_ref[hd], NEG)),
                      (kn_ref[0, :, cols].astype(BF16), vn_ref[0, :, cols].astype(BF16),
                       jnp.where(ok_n, bn_ref[hd], NEG))]
            halves.append(_pieces_attention(qm, pieces))
        o_ref[0, :, cols] = jnp.where(lane < DH_B, halves[0], halves[1]).astype(o_ref.dtype)


def _band_sample(q3, kv3, kc, vc, bias, past):
    b, t, _ = q3.shape
    nband = kc.shape[1]
    assert nband == BAND and t <= CHUNK
    w = H_B * DH_B
    bias_c = bias[:, :t, :nband]
    bias_n = bias[:, :t, nband:nband + t]
    blk = lambda col: pl.BlockSpec((1, t, w), lambda bi: (bi, 0, col))
    cache = pl.BlockSpec((1, nband, w), lambda bi: (bi, 0, 0))
    return pl.pallas_call(
        functools.partial(_band_sample_kernel, past=past),
        out_shape=jax.ShapeDtypeStruct((b, t, w), BF16),
        grid=(b,),
        in_specs=[
            pl.BlockSpec((H_B, t, nband), lambda bi: (0, 0, 0)),
            pl.BlockSpec((H_B, t, t), lambda bi: (0, 0, 0)),
            blk(1), cache, cache, blk(0), blk(1),
        ],
        out_specs=pl.BlockSpec((1, t, w), lambda bi: (bi, 0, 0)),
        compiler_params=_cparams(("parallel",)),
        name="band_sample",
    )(bias_c, bias_n, q3, kc, vc, kv3, kv3)


CONV_PAD = 8
MLSTM_L = 256


def _mlstm_kernel(cqk_ref, cv_ref, co_ref, cif_ref, conv0_ref, cw_ref, cb_ref, gb_ref, hg_ref,
                  c0_ref, n0_ref, m0_ref, hs_ref, c_ref, n_ref, m_ref, xext, *, L):
    @pl.when(pl.program_id(1) == 0)
    def _():
        xext[0:CONV_PAD, :] = conv0_ref[0]
        c_ref[...] = c0_ref[...]
        n_ref[...] = n0_ref[...]
        m_ref[...] = m0_ref[...]

    xext[CONV_PAD:CONV_PAD + L, :] = cqk_ref[0]
    base = CONV_PAD - (CONV_W - 1)
    u = 0.0
    for j in range(CONV_W):
        u = u + xext[base + j:base + j + L, :] * cw_ref[j:j + 1, :]
    u = cb_ref[...] + u
    u = u * jax.nn.sigmoid(u)
    xext[0:CONV_PAD, :] = xext[L:L + CONV_PAD, :]

    z = cif_ref[0] + gb_ref[...]
    lane = lax.broadcasted_iota(jnp.int32, (L, LANES), 1)
    lf = jnp.minimum(z, 0.0) - jnp.log1p(jnp.exp(-jnp.abs(z)))
    tr = lax.broadcasted_iota(jnp.int32, (L, L), 0)
    tc = lax.broadcasted_iota(jnp.int32, (L, L), 1)
    causal = tr >= tc
    b_all = jnp.dot(causal.astype(F32), lf, preferred_element_type=F32,
                    precision=lax.Precision.HIGHEST)
    sr = lax.broadcasted_iota(jnp.int32, (8, LANES), 0)
    sc = lax.broadcasted_iota(jnp.int32, (8, LANES), 1)
    sel = jnp.where(sr < H_C, jnp.where(sc == sr, 1.0, jnp.where(sc == sr + H_C, -1.0, 0.0)), 0.0)
    rt = lax.dot_general(sel, jnp.where(lane < H_C, z, b_all), NT_DIMS,
                         preferred_element_type=F32, precision=lax.Precision.HIGHEST)
    er = lax.broadcasted_iota(jnp.int32, (DH_C, DH_C), 0)
    ec = lax.broadcasted_iota(jnp.int32, (DH_C, DH_C), 1)
    eye = jnp.where(er == ec, 1.0, 0.0).astype(BF16)

    for h in range(H_C):
        cols = slice(h * DH_C, (h + 1) * DH_C)
        q = u[:, h * DH_C:(h + 1) * DH_C]
        k = u[:, D_C + h * DH_C:D_C + (h + 1) * DH_C] * (DH_C ** -0.5)
        v = cv_ref[0, :, cols]
        qb, kb_, vb_ = q.astype(BF16), k.astype(BF16), v.astype(BF16)
        b_col = b_all[:, H_C + h:H_C + h + 1]
        li_col = z[:, h:h + 1]
        m_prev = m_ref[0, h:h + 1, 0:1]
        cs = c_ref[0, h]
        ns = n_ref[0, h:h + 1, :]

        dmat = jnp.where(causal, b_col + rt[h:h + 1, :], NEG)
        inter = b_col + m_prev
        m_t = jnp.maximum(inter, jnp.max(dmat, axis=-1, keepdims=True))
        w_intra = jnp.exp(dmat - m_t)
        w_inter = jnp.exp(inter - m_t)
        a = w_intra * _nt(qb, kb_)
        num = (jnp.dot(a.astype(BF16), vb_, preferred_element_type=F32)
               + w_inter * _nt(qb, cs.astype(BF16)))
        den = jnp.sum(a, axis=-1, keepdims=True) + w_inter * jnp.sum(q * ns, axis=-1, keepdims=True)
        hh = num / jnp.maximum(jnp.abs(den), jnp.exp(-m_t))

        b_last = b_col[L - 1:L, :]
        g = b_last - b_col + li_col
        m_new = jnp.maximum(b_last + m_prev, jnp.max(g, axis=0, keepdims=True))
        ws = jnp.exp(g - m_new)
        decay = jnp.exp(b_last + m_prev - m_new)
        vwt = _nt(eye, (ws * v).astype(BF16)).astype(BF16)
        c_ref[0, h] = decay * cs + jnp.dot(vwt, kb_, preferred_element_type=F32)
        n_ref[0, h:h + 1, :] = decay * ns + jnp.sum(ws * k, axis=0, keepdims=True)
        m_ref[0, h:h + 1, :] = jnp.broadcast_to(m_new, (1, LANES))

        gate = jax.nn.sigmoid(co_ref[0, :, cols])
        hs_ref[0, :, cols] = (_rms(hh, hg_ref[...]) * gate).astype(hs_ref.dtype)


def _mlstm(cqk3, cvo3, cif3, conv0, conv_w, conv_b, gate_b, head_g, c0, n0, m0, L):
    b, t, _ = cqk3.shape
    nc = t // L
    blk = lambda col, w: pl.BlockSpec((1, L, w), lambda bi, c: (bi, c, col))
    const = lambda shape: pl.BlockSpec(shape, lambda bi, c: (0,) * len(shape))
    per_b = lambda shape: pl.BlockSpec((1,) + shape, lambda bi, c: (bi,) + (0,) * len(shape))
    return pl.pallas_call(
        functools.partial(_mlstm_kernel, L=L),
        out_shape=(jax.ShapeDtypeStruct((b, t, D_C), BF16),
                   jax.ShapeDtypeStruct((b, H_C, DH_C, DH_C), F32),
                   jax.ShapeDtypeStruct((b, H_C, DH_C), F32),
                   jax.ShapeDtypeStruct((b, H_C, LANES), F32)),
        grid=(b, nc),
        in_specs=[
            blk(0, 2 * D_C), blk(0, D_C), blk(1, D_C), blk(0, LANES),
            per_b((CONV_PAD, 2 * D_C)), const((CONV_W, 2 * D_C)), const((1, 2 * D_C)),
            const((1, LANES)), const((1, DH_C)),
            per_b((H_C, DH_C, DH_C)), per_b((H_C, DH_C)), per_b((H_C, LANES)),
        ],
        out_specs=(pl.BlockSpec((1, L, D_C), lambda bi, c: (bi, c, 0)),
                   per_b((H_C, DH_C, DH_C)), per_b((H_C, DH_C)), per_b((H_C, LANES))),
        scratch_shapes=[pltpu.VMEM((CONV_PAD + L, 2 * D_C), F32)],
        compiler_params=_cparams(("parallel", "arbitrary")),
        name="mlstm",
    )(cqk3, cvo3, cvo3, cif3, conv0, conv_w, conv_b, gate_b, head_g, c0, n0, m0)


def _mix_kernel(oa_ref, ob_ref, oc_ref, ga_ref, gb_ref, gc_ref, x_ref, wa_ref, wb_ref, wc_ref, wo_ref, o_ref):
    mixed = (ga_ref[...] * jnp.dot(oa_ref[...], wa_ref[...], preferred_element_type=F32)
             + gb_ref[...] * jnp.dot(ob_ref[...], wb_ref[...], preferred_element_type=F32)
             + gc_ref[...] * jnp.dot(oc_ref[...], wc_ref[...], preferred_element_type=F32))
    o_ref[...] = x_ref[...] + jnp.dot(mixed.astype(BF16), wo_ref[...], preferred_element_type=F32)


def _mix(oa, ob, oc, gates, x, wa, wb, wc, wo):
    m, d = x.shape
    tm = min(m, 512)
    row = lambda w, col=0: pl.BlockSpec((tm, w), lambda i: (i, col))
    full = lambda a: pl.BlockSpec(a.shape, lambda i: (0, 0))
    g0 = 0
    return pl.pallas_call(
        _mix_kernel,
        out_shape=jax.ShapeDtypeStruct((m, d), F32),
        grid=(m // tm,),
        in_specs=[row(oa.shape[1]), row(ob.shape[1]), row(oc.shape[1]),
                  row(d, g0), row(d, g0 + 1), row(d, g0 + 2), row(d),
                  full(wa), full(wb), full(wc), full(wo)],
        out_specs=row(d),
        compiler_params=_cparams(("parallel",)),
        name="mix",
    )(oa, ob, oc, gates, gates, gates, x, wa, wb, wc, wo)


def _cross_kernel(x_ref, g_ref, wq_ref, mk_ref, mv_ref, wo_ref, o_ref):
    x = x_ref[0]
    h = _rms(x, g_ref[...]).astype(BF16)
    q = (jnp.dot(h, wq_ref[...], preferred_element_type=F32) * (DH_M ** -0.5)).astype(BF16)
    outs = []
    for hd in range(H_M):
        cols = slice(hd * DH_M, (hd + 1) * DH_M)
        s = _nt(q[:, cols], mk_ref[0, :, cols].astype(BF16))
        m = jnp.max(s, axis=-1, keepdims=True)
        e = jnp.exp(s - m)
        l = jnp.sum(e, axis=-1, keepdims=True)
        o = jnp.dot(e.astype(BF16), mv_ref[0, :, cols].astype(BF16), preferred_element_type=F32) / l
        outs.append(o.astype(BF16))
    o = jnp.concatenate(outs, axis=1)
    o_ref[0] = x + jnp.dot(o, wo_ref[...], preferred_element_type=F32)


def _cross(x3, g, wq, mk, mv, wo):
    b, t, d = x3.shape
    tm = min(t, 512)
    nm = mk.shape[1]
    full = lambda a: pl.BlockSpec(a.shape, lambda bi, i: (0, 0))
    return pl.pallas_call(
        _cross_kernel,
        out_shape=jax.ShapeDtypeStruct((b, t, d), F32),
        grid=(b, t // tm),
        in_specs=[pl.BlockSpec((1, tm, d), lambda bi, i: (bi, i, 0)), full(g), full(wq),
                  pl.BlockSpec((1, nm, d), lambda bi, i: (bi, 0, 0)),
                  pl.BlockSpec((1, nm, d), lambda bi, i: (bi, 0, 0)), full(wo)],
        out_specs=pl.BlockSpec((1, tm, d), lambda bi, i: (bi, i, 0)),
        compiler_params=_cparams(("parallel", "parallel")),
        name="cross",
    )(x3, g, wq, mk, mv, wo)


def _ffn_kernel(x_ref, g_ref, wg_ref, wu_ref, wd_ref, gf_ref, o_ref, h_scr, acc, *, final_norm):
    j = pl.program_id(1)

    @pl.when(j == 0)
    def _():
        h_scr[...] = _rms(x_ref[...], g_ref[...]).astype(BF16)
        acc[...] = jnp.zeros(acc.shape, F32)

    h = h_scr[...]
    gate = jnp.dot(h, wg_ref[...], preferred_element_type=F32)
    up = jnp.dot(h, wu_ref[...], preferred_element_type=F32)
    a = (gate * jax.nn.sigmoid(gate) * up).astype(BF16)
    acc[...] += jnp.dot(a, wd_ref[...], preferred_element_type=F32)

    @pl.when(j == pl.num_programs(1) - 1)
    def _():
        y = x_ref[...] + acc[...]
        if final_norm:
            y = _rms(y, gf_ref[...])
        o_ref[...] = y


FFN_TF = 256


def _ffn(x, g, wg, wu, wd, g_final, final_norm):
    m, d = x.shape
    tm = min(m, 1024)
    return pl.pallas_call(
        functools.partial(_ffn_kernel, final_norm=final_norm),
        out_shape=jax.ShapeDtypeStruct((m, d), F32),
        grid=(m // tm, D_FF // FFN_TF),
        in_specs=[
            pl.BlockSpec((tm, d), lambda i, j: (i, 0)),
            pl.BlockSpec((1, d), lambda i, j: (0, 0)),
            pl.BlockSpec((d, FFN_TF), lambda i, j: (0, j)),
            pl.BlockSpec((d, FFN_TF), lambda i, j: (0, j)),
            pl.BlockSpec((FFN_TF, d), lambda i, j: (j, 0)),
            pl.BlockSpec((1, d), lambda i, j: (0, 0)),
        ],
        out_specs=pl.BlockSpec((tm, d), lambda i, j: (i, 0)),
        scratch_shapes=[pltpu.VMEM((tm, d), BF16), pltpu.VMEM((tm, d), F32)],
        compiler_params=_cparams(("parallel", "arbitrary")),
        name="ffn",
    )(x, g, wg, wu, wd, g_final)


def _layer(x3, l, w, mem_k, mem_v, cache, final_norm, akv_all):
    b, t, d = x3.shape
    x2 = x3.reshape(b * t, d)
    row = lambda a: a.reshape(1, -1)
    lam_init = 0.8 - 0.6 * math.exp(-0.3 * l)

    gates, cqk, cif = _in_proj_a(x2, row(w['g_mix']), w['w_in'], w['w_if'])
    q, ak_all, av_all, bkv, cvo = _in_proj_b(x2, row(w['g_mix']), w['w_in'], l, *akv_all)
    q3 = q.reshape(b, t, SEG)
    bkv3 = bkv.reshape(b, t, SEG)
    ak3 = ak_all.reshape(DEPTH * b, t, HALF)
    av3 = av_all.reshape(DEPTH * b, t, HALF)
    cqk3 = cqk.reshape(b, t, 2 * D_C)
    b_k = bkv3[:, :, :HALF].reshape(b, t, H_B, DH_B)
    b_v = bkv3[:, :, HALF:].reshape(b, t, H_B, DH_B)

    gate_b = jnp.concatenate([w['b_i'], w['b_f'], jnp.zeros((LANES - 2 * H_C,), F32)]).reshape(1, LANES)
    if cache is None:
        oa = _diff_prompt(q3, ak3, av3, l, w['lqk'], row(w['a_head_g']), lam_init)
        ob = _band_prompt(q3, bkv3, w['relbias'])
        conv0 = jnp.zeros((b, CONV_PAD, 2 * D_C), F32)
        c0 = jnp.zeros((b, H_C, DH_C, DH_C), F32)
        n0 = jnp.zeros((b, H_C, DH_C), F32)
        m0 = jnp.zeros((b, H_C, LANES), F32)
        keep = min(BAND, t)
        new_b = (b_k[:, t - keep:], b_v[:, t - keep:])
        L = min(t, MLSTM_L)
    else:
        past = cache['a_k'].shape[1]
        oa = _diff_sample(q3, ak3, av3, l, cache['a_k'].reshape(b, past, -1), cache['a_v'].reshape(b, past, -1),
                          w['lqk'], row(w['a_head_g']), lam_init)
        nband = cache['b_k'].shape[1]
        ob = _band_sample(q3, bkv3, cache['b_k'].reshape(b, nband, -1), cache['b_v'].reshape(b, nband, -1),
                          w['relbias'], past)
        conv0 = jnp.pad(cache['conv'], ((0, 0), (CONV_PAD - (CONV_W - 1), 0), (0, 0)))
        c0, n0 = cache['C'], cache['n']
        m0 = jnp.broadcast_to(cache['m'][:, :, None], (b, H_C, LANES))
        new_b = (b_k, b_v)
        L = t
    oc, c_new, n_new, m_new = _mlstm(cqk3, cvo.reshape(b, t, SEG), cif.reshape(b, t, LANES), conv0,
                                     w['conv_w'], row(w['conv_b']), gate_b, row(w['c_head_g']), c0, n0, m0, L)
    assert t >= CONV_W - 1
    conv_new = cqk3[:, t - (CONV_W - 1):]

    x2 = _mix(oa.reshape(b * t, -1), ob.reshape(b * t, -1), oc.reshape(b * t, -1), gates, x2,
              w['w_up_a'], w['w_up_b'], w['w_up_c'], w['w_o'])
    x3 = _cross(x2.reshape(b, t, d), row(w['g_cross']), w['w_mq'], mem_k, mem_v, w['w_mo'])
    x2 = _ffn(x3.reshape(b * t, d), row(w['g_ffn']), w['w_ff_g'], w['w_ff_u'], w['w_ff_d'],
              row(w['g_final']), final_norm)
    return x2.reshape(b, t, d), (ak_all, av_all), new_b + (c_new, n_new, m_new[:, :, 0], conv_new)


def kernel(x_prompt, x_sample, cache_a_k, cache_a_v, cache_b_k, cache_b_v, state_c_C, state_c_n, state_c_m, state_c_conv, cache_mem_k, cache_mem_v, mem_prompt, g_mix, w_in, a_lq1, a_lk1, a_lq2, a_lk2, a_head_g, b_rel, c_conv_w, c_conv_b, c_b_i, c_b_f, c_head_g, w_up_a, w_up_b, w_up_c, w_o, g_cross, w_mq, w_mk, w_mv, w_mo, g_ffn, w_ff_g, w_ff_u, w_ff_d, g_final):
    xp, xs = x_prompt, x_sample
    bp = x_prompt.shape[0]
    n_mem = mem_prompt.shape[1]
    mem2 = mem_prompt.reshape(bp * n_mem, D_MODEL)
    new_p = [[] for _ in range(8)]
    new_s = [[] for _ in range(6)]
    akv_p = akv_s = (None, None)
    bf = lambda a: a.astype(BF16)
    for l in range(DEPTH):
        wp, wif = _prep_w_in(w_in[l])
        w = dict(g_mix=g_mix[l], w_in=wp, w_if=wif,
                 lqk=jnp.stack([a_lq1[l], a_lk1[l], a_lq2[l], a_lk2[l]]),
                 a_head_g=a_head_g[l], relbias=_relbias(b_rel[l]), conv_w=c_conv_w[l], conv_b=c_conv_b[l],
                 b_i=c_b_i[l], b_f=c_b_f[l], c_head_g=c_head_g[l],
                 w_up_a=bf(w_up_a[l]), w_up_b=bf(w_up_b[l]), w_up_c=bf(w_up_c[l]), w_o=bf(w_o[l]),
                 g_cross=g_cross[l], w_mq=bf(w_mq[l]), w_mo=bf(w_mo[l]), g_ffn=g_ffn[l],
                 w_ff_g=bf(w_ff_g[l]), w_ff_u=bf(w_ff_u[l]), w_ff_d=bf(w_ff_d[l]), g_final=g_final)
        mk2 = _mem_proj(mem2, bf(w_mk[l])).reshape(bp, n_mem, D_MODEL)
        mv2 = _mem_proj(mem2, bf(w_mv[l])).reshape(bp, n_mem, D_MODEL)
        last = l == DEPTH - 1
        xp, akv_p, st_p = _layer(xp, l, w, mk2, mv2, None, last, akv_p)
        cache = dict(a_k=cache_a_k[l], a_v=cache_a_v[l], b_k=cache_b_k[l], b_v=cache_b_v[l],
                     C=state_c_C[l], n=state_c_n[l], m=state_c_m[l], conv=state_c_conv[l])
        bs = xs.shape[0]
        xs, akv_s, st_s = _layer(xs, l, w, cache_mem_k[l].reshape(bs, n_mem, D_MODEL),
                                 cache_mem_v[l].reshape(bs, n_mem, D_MODEL), cache, last, akv_s)
        mk4 = mk2.reshape(bp, n_mem, H_M, DH_M)
        mv4 = mv2.reshape(bp, n_mem, H_M, DH_M)
        for lst, a in zip(new_p, st_p + (mk4, mv4)):
            lst.append(a)
        for lst, a in zip(new_s, st_s):
            lst.append(a)
    heads = lambda a, b, t: a.reshape(DEPTH, b, t, H_A, DV_A)
    tp, ts = x_prompt.shape[1], xs.shape[1]
    outs_p = [heads(a, bp, tp) for a in akv_p] + [jnp.stack(a, 0) for a in new_p]
    outs_s = [heads(a, xs.shape[0], ts) for a in akv_s] + [jnp.stack(a, 0) for a in new_s]
    return (xp, xs) + tuple(outs_p) + tuple(outs_s)
```

```python
import functools
import math

import jax
import jax.numpy as jnp
from jax import lax
from jax.experimental import pallas as pl
from jax.experimental.pallas import tpu as pltpu

F32 = jnp.float32
BF16 = jnp.bfloat16

D_MODEL = 1024
DEPTH = 2
CHUNK = 64
EPS = 1e-6
NEG = -1e30
H_A = 4
DH_A = 64
DV_A = 2 * DH_A
H_B = 8
DH_B = 64
BAND_CHUNKS = 8
BAND = BAND_CHUNKS * CHUNK
REL_CLIP = 128
H_C = 4
DH_C = 128
D_C = H_C * DH_C
CONV_W = 4
H_M = 4
DH_M = D_MODEL // H_M
D_FF = -(-8 * D_MODEL // (3 * 256)) * 256

LANES = 128
VMEM_LIMIT = 48 * 1024 * 1024

SEG = 1024
HALF = SEG // 2
GATE_TILES = 3
N_TILES_A = GATE_TILES + 1
N_TILES_B = 4
IN_SIZES = (H_A * DH_A,) * 4 + (H_A * DV_A,) + (H_B * DH_B,) * 3 + (2 * D_C, D_C, D_C, 2 * H_C, 3 * D_MODEL)

ALIBI_SLOPES = tuple(2.0 ** (-8.0 * (i + 1) / H_A) for i in range(H_A))

NT_DIMS = (((1,), (1,)), ((), ()))


def _cparams(sem):
    return pltpu.CompilerParams(dimension_semantics=sem, vmem_limit_bytes=VMEM_LIMIT)


def _rms(x, g):
    ms = jnp.mean(x * x, axis=-1, keepdims=True)
    return x * lax.rsqrt(ms + EPS) * g


def _nt(a, b):
    return lax.dot_general(a, b, NT_DIMS, preferred_element_type=F32)


def _prep_w_in_kernel(w_ref, o_ref, oif_ref):
    offs = [0]
    for size in IN_SIZES:
        offs.append(offs[-1] + size)
    seg = lambda i: w_ref[:, offs[i]:offs[i + 1]]
    a_q1, a_q2, a_k1, a_k2, a_v, b_q, b_k, b_v, c_qk, c_v, c_o = (seg(i) for i in range(11))
    g = seg(12)

    def put(col, val):
        o_ref[:, col:col + val.shape[1]] = val.astype(BF16)

    put(0, g)
    put(GATE_TILES * SEG, c_qk)
    for h in range(H_A):
        hs = slice(h * DH_A, (h + 1) * DH_A)
        base = h * 2 * DH_A
        put(4 * SEG + base, a_q1[:, hs] * (DH_A ** -0.5))
        put(4 * SEG + base + DH_A, a_q2[:, hs] * (DH_A ** -0.5))
        put(5 * SEG + base, a_k1[:, hs])
        put(5 * SEG + base + DH_A, a_k2[:, hs])
    put(4 * SEG + HALF, b_q * (DH_B ** -0.5))
    put(5 * SEG + HALF, a_v)
    put(6 * SEG, b_k)
    put(6 * SEG + HALF, b_v)
    put(7 * SEG, c_v)
    put(7 * SEG + HALF, c_o)
    tile = w_ref[:, offs[11]:offs[11] + LANES]
    lane = lax.broadcasted_iota(jnp.int32, tile.shape, 1)
    oif_ref[...] = jnp.where(lane < 2 * H_C, tile, 0.0).astype(BF16)


def _prep_w_in(w):
    depth, k, n = w.shape
    tk = 256
    n_out = (N_TILES_A + N_TILES_B) * SEG
    return pl.pallas_call(
        _prep_w_in_kernel,
        out_shape=(jax.ShapeDtypeStruct((depth, k, n_out), BF16),
                   jax.ShapeDtypeStruct((depth, k, LANES), BF16)),
        grid=(depth, k // tk),
        in_specs=[pl.BlockSpec((None, tk, n), lambda l, i: (l, i, 0))],
        out_specs=(pl.BlockSpec((None, tk, n_out), lambda l, i: (l, i, 0)),
                   pl.BlockSpec((None, tk, LANES), lambda l, i: (l, i, 0))),
        compiler_params=_cparams(("parallel", "parallel")),
        name="prep_w_in",
    )(w)


def _norm_to_scratch(x_ref, g_ref, h_scr):
    h_scr[...] = _rms(x_ref[...], g_ref[...]).astype(BF16)


def _in_proj_a_kernel(x_ref, g_ref, w_ref, wif_ref, og_ref, ocqk_ref, ocif_ref, h_scr):
    j = pl.program_id(1)

    @pl.when(j == 0)
    def _():
        _norm_to_scratch(x_ref, g_ref, h_scr)
        ocif_ref[...] = jnp.dot(h_scr[...], wif_ref[...], preferred_element_type=F32)

    acc = jnp.dot(h_scr[...], w_ref[...], preferred_element_type=F32)

    @pl.when(j < GATE_TILES)
    def _():
        og_ref[...] = jax.nn.sigmoid(acc).astype(og_ref.dtype)

    @pl.when(j == GATE_TILES)
    def _():
        ocqk_ref[...] = acc


def _in_proj_a(x, g, l, wp, wif):
    m, k = x.shape
    tm = min(m, 1024)
    return pl.pallas_call(
        _in_proj_a_kernel,
        out_shape=(jax.ShapeDtypeStruct((m, GATE_TILES * SEG), BF16),
                   jax.ShapeDtypeStruct((m, SEG), F32),
                   jax.ShapeDtypeStruct((m, LANES), F32)),
        grid=(m // tm, N_TILES_A),
        in_specs=[
            pl.BlockSpec((tm, k), lambda i, j: (i, 0)),
            pl.BlockSpec((1, k), lambda i, j: (0, 0)),
            pl.BlockSpec((None, k, SEG), lambda i, j: (l, 0, j)),
            pl.BlockSpec((None, k, LANES), lambda i, j: (l, 0, 0)),
        ],
        out_specs=(pl.BlockSpec((tm, SEG), lambda i, j: (i, jnp.minimum(j, GATE_TILES - 1))),
                   pl.BlockSpec((tm, SEG), lambda i, j: (i, 0)),
                   pl.BlockSpec((tm, LANES), lambda i, j: (i, 0))),
        scratch_shapes=[pltpu.VMEM((tm, k), BF16)],
        compiler_params=_cparams(("parallel", "arbitrary")),
        name="in_proj_a",
    )(x, g, wp, wif)


def _in_proj_b_kernel(*refs, aliased):
    x_ref, g_ref, w_ref = refs[:3]
    oq_ref, oak_ref, oav_ref, obkv_ref, ocvo_ref, h_scr = refs[5 if aliased else 3:]
    j = pl.program_id(1)

    @pl.when(j == 0)
    def _():
        _norm_to_scratch(x_ref, g_ref, h_scr)

    acc = jnp.dot(h_scr[...], w_ref[...], preferred_element_type=F32)

    @pl.when(j == 0)
    def _():
        oq_ref[...] = acc.astype(oq_ref.dtype)

    @pl.when(j == 1)
    def _():
        oak_ref[0] = acc[:, :HALF]
        oav_ref[0] = acc[:, HALF:]

    @pl.when(j == 2)
    def _():
        obkv_ref[...] = acc

    @pl.when(j == 3)
    def _():
        ocvo_ref[...] = acc


def _in_proj_b(x, g, wp, l, ak_all, av_all):
    m, k = x.shape
    tm = min(m, 1024)
    aliased = ak_all is not None
    stacked = jax.ShapeDtypeStruct((DEPTH, m, HALF), F32)
    in_specs = [
        pl.BlockSpec((tm, k), lambda i, j: (i, 0)),
        pl.BlockSpec((1, k), lambda i, j: (0, 0)),
        pl.BlockSpec((None, k, SEG), lambda i, j: (l, 0, N_TILES_A + j)),
    ]
    args = [x, g, wp]
    if aliased:
        in_specs += [pl.BlockSpec(memory_space=pl.ANY)] * 2
        args += [ak_all, av_all]
    row = lambda w: pl.BlockSpec((tm, w), lambda i, j: (i, 0))
    layer_row = pl.BlockSpec((1, tm, HALF), lambda i, j: (l, i, 0))
    return pl.pallas_call(
        functools.partial(_in_proj_b_kernel, aliased=aliased),
        out_shape=(jax.ShapeDtypeStruct((m, SEG), BF16), stacked, stacked,
                   jax.ShapeDtypeStruct((m, SEG), F32), jax.ShapeDtypeStruct((m, SEG), F32)),
        grid=(m // tm, N_TILES_B),
        in_specs=in_specs,
        out_specs=(row(SEG), layer_row, layer_row, row(SEG), row(SEG)),
        scratch_shapes=[pltpu.VMEM((tm, k), BF16)],
        input_output_aliases={3: 1, 4: 2} if aliased else {},
        compiler_params=_cparams(("parallel", "arbitrary")),
        name="in_proj_b",
    )(*args)


def _mem_proj_kernel(x_ref, w_ref, o_ref):
    o_ref[...] = jnp.dot(x_ref[...].astype(BF16), w_ref[...], preferred_element_type=F32)


def _mem_proj(x, l, w):
    m, k = x.shape
    n = w.shape[2]
    tm = min(m, 1024)
    return pl.pallas_call(
        _mem_proj_kernel,
        out_shape=jax.ShapeDtypeStruct((m, n), F32),
        grid=(m // tm,),
        in_specs=[pl.BlockSpec((tm, k), lambda i: (i, 0)), pl.BlockSpec((None, k, n), lambda i: (l, 0, 0))],
        out_specs=pl.BlockSpec((tm, n), lambda i: (i, 0)),
        compiler_params=_cparams(("parallel",)),
        name="mem_proj",
    )(x, w)


def _online_update(s, v, acc, m, l):
    slabs = [s[:, j * LANES:(j + 1) * LANES] for j in range(s.shape[1] // LANES)]
    m_old = m[...]
    m_new = jnp.maximum(m_old, jnp.max(functools.reduce(jnp.maximum, slabs), axis=-1, keepdims=True))
    alpha = jnp.exp(m_old - m_new)
    ps = [jnp.exp(c - m_new) for c in slabs]
    l[...] = alpha * l[...] + jnp.sum(functools.reduce(jnp.add, ps), axis=-1, keepdims=True)
    p = jnp.concatenate(ps, axis=1).astype(BF16)
    acc[...] = alpha * acc[...] + jnp.dot(p, v, preferred_element_type=F32)
    m[...] = m_new


def _split_halves(q):
    lane = lax.broadcasted_iota(jnp.int32, q.shape, 1)
    lo = jnp.where(lane < DH_A, q, 0.0).astype(BF16)
    hi = jnp.where(lane >= DH_A, q, 0.0).astype(BF16)
    return lo, hi


def _lambda(lqk, lam_init):
    e1 = jnp.exp(jnp.sum(lqk[0:1, :] * lqk[1:2, :], axis=-1, keepdims=True))
    e2 = jnp.exp(jnp.sum(lqk[2:3, :] * lqk[3:4, :], axis=-1, keepdims=True))
    return e1 - e2 + lam_init


def _diff_prompt_kernel(slopes_ref, lqk_ref, hg_ref, q_ref, k_ref, v_ref, o_ref,
                        kb, vb, acc1, acc2, m1, l1, m2, l2, *, tq, lam_init):
    h = pl.program_id(1)
    qi = pl.program_id(2)
    slope = slopes_ref[h]

    @pl.when(qi == 0)
    def _():
        kb[...] = k_ref[0].astype(BF16)
        vb[...] = v_ref[0].astype(BF16)

    qa, qb = _split_halves(q_ref[0])
    for m, l, acc in ((m1, l1, acc1), (m2, l2, acc2)):
        m[...] = jnp.full(m.shape, NEG, F32)
        l[...] = jnp.zeros(l.shape, F32)
        acc[...] = jnp.zeros(acc.shape, F32)

    def full_tile(ki, carry):
        k0 = pl.multiple_of(ki * tq, tq)
        k = kb[pl.ds(k0, tq), :]
        v = vb[pl.ds(k0, tq), :]
        kpos = k0 + lax.broadcasted_iota(jnp.int32, (1, tq), 1)
        bias = slope * kpos.astype(F32)
        _online_update(_nt(qa, k) + bias, v, acc1, m1, l1)
        _online_update(_nt(qb, k) + bias, v, acc2, m2, l2)
        return carry

    lax.fori_loop(0, qi, full_tile, 0)

    q0 = pl.multiple_of(qi * tq, tq)
    k = kb[pl.ds(q0, tq), :]
    v = vb[pl.ds(q0, tq), :]
    r = lax.broadcasted_iota(jnp.int32, (tq, tq), 0)
    c = lax.broadcasted_iota(jnp.int32, (tq, tq), 1)
    allowed = (c // CHUNK) <= (r // CHUNK)
    bias = slope * (q0 + r - jnp.abs(r - c)).astype(F32)
    _online_update(jnp.where(allowed, _nt(qa, k) + bias, NEG), v, acc1, m1, l1)
    _online_update(jnp.where(allowed, _nt(qb, k) + bias, NEG), v, acc2, m2, l2)

    lam = _lambda(lqk_ref[...], lam_init)
    o = acc1[...] / l1[...] - lam * (acc2[...] / l2[...])
    o_ref[0] = (_rms(o, hg_ref[...]) * (1.0 - lam_init)).astype(o_ref.dtype)


def _diff_prompt(q3, k3, v3, l, lqk, head_g, lam_init):
    b, t, _ = q3.shape
    tq = min(t, 512)
    slopes = jnp.asarray(ALIBI_SLOPES, F32)
    return pl.pallas_call(
        functools.partial(_diff_prompt_kernel, tq=tq, lam_init=lam_init),
        out_shape=jax.ShapeDtypeStruct((b, t, H_A * DV_A), BF16),
        grid=(b, H_A, t // tq),
        in_specs=[
            pl.BlockSpec(memory_space=pltpu.SMEM),
            pl.BlockSpec((4, DH_A), lambda bi, h, qi: (0, 0)),
            pl.BlockSpec((1, DV_A), lambda bi, h, qi: (0, 0)),
            pl.BlockSpec((1, tq, LANES), lambda bi, h, qi: (bi, qi, h)),
            pl.BlockSpec((1, t, LANES), lambda bi, h, qi: (l * b + bi, 0, h)),
            pl.BlockSpec((1, t, LANES), lambda bi, h, qi: (l * b + bi, 0, h)),
        ],
        out_specs=pl.BlockSpec((1, tq, LANES), lambda bi, h, qi: (bi, qi, h)),
        scratch_shapes=[
            pltpu.VMEM((t, LANES), BF16), pltpu.VMEM((t, LANES), BF16),
            pltpu.VMEM((tq, LANES), F32), pltpu.VMEM((tq, LANES), F32),
            pltpu.VMEM((tq, LANES), F32), pltpu.VMEM((tq, LANES), F32),
            pltpu.VMEM((tq, LANES), F32), pltpu.VMEM((tq, LANES), F32),
        ],
        compiler_params=_cparams(("parallel", "parallel", "arbitrary")),
        name="diff_prompt",
    )(slopes, lqk, head_g, q3, k3, v3)


def _pieces_attention(qm, pieces):
    ss = [_nt(qm, k) + bias for k, _, bias in pieces]
    m = functools.reduce(jnp.maximum, [jnp.max(s, axis=-1, keepdims=True) for s in ss])
    es = [jnp.exp(s - m) for s in ss]
    l = functools.reduce(jnp.add, [jnp.sum(e, axis=-1, keepdims=True) for e in es])
    o = functools.reduce(jnp.add, [jnp.dot(e.astype(BF16), v, preferred_element_type=F32)
                                   for e, (_, v, _) in zip(es, pieces)])
    return o / l


def _diff_sample_kernel(lqk_ref, hg_ref, q_ref, kc_ref, vc_ref, kn_ref, vn_ref, o_ref, *, lam_init):
    t = q_ref.shape[1]
    past = kc_ref.shape[1]
    lam = _lambda(lqk_ref[...], lam_init)
    rc = lax.broadcasted_iota(jnp.int32, (t, past), 0)
    cc = lax.broadcasted_iota(jnp.int32, (t, past), 1)
    rn = lax.broadcasted_iota(jnp.int32, (t, t), 0)
    cn = lax.broadcasted_iota(jnp.int32, (t, t), 1)
    dist_c = jnp.abs(past + rc - cc).astype(F32)
    dist_n = jnp.abs(rn - cn).astype(F32)
    ok_c = (cc // CHUNK) <= ((past + rc) // CHUNK)
    ok_n = ((past + cn) // CHUNK) <= ((past + rn) // CHUNK)
    for h in range(H_A):
        cols = slice(h * LANES, (h + 1) * LANES)
        bias_c = jnp.where(ok_c, -ALIBI_SLOPES[h] * dist_c, NEG)
        bias_n = jnp.where(ok_n, -ALIBI_SLOPES[h] * dist_n, NEG)
        pieces = [(kc_ref[0, :, cols].astype(BF16), vc_ref[0, :, cols].astype(BF16), bias_c),
                  (kn_ref[0, :, cols].astype(BF16), vn_ref[0, :, cols].astype(BF16), bias_n)]
        qa, qb = _split_halves(q_ref[0, :, cols])
        o = _pieces_attention(qa, pieces) - lam * _pieces_attention(qb, pieces)
        o_ref[0, :, cols] = (_rms(o, hg_ref[...]) * (1.0 - lam_init)).astype(o_ref.dtype)


def _diff_sample(q3, kn3, vn3, l, kc, vc, lqk, head_g, lam_init):
    b, t, _ = q3.shape
    past = kc.shape[1]
    w = H_A * DV_A
    new = pl.BlockSpec((1, t, w), lambda bi: (l * b + bi, 0, 0))
    cache = pl.BlockSpec((1, past, w), lambda bi: (bi, 0, 0))
    return pl.pallas_call(
        functools.partial(_diff_sample_kernel, lam_init=lam_init),
        out_shape=jax.ShapeDtypeStruct((b, t, w), BF16),
        grid=(b,),
        in_specs=[
            pl.BlockSpec((4, DH_A), lambda bi: (0, 0)),
            pl.BlockSpec((1, DV_A), lambda bi: (0, 0)),
            pl.BlockSpec((1, t, w), lambda bi: (bi, 0, 0)), cache, cache, new, new,
        ],
        out_specs=pl.BlockSpec((1, t, w), lambda bi: (bi, 0, 0)),
        compiler_params=_cparams(("parallel",)),
        name="diff_sample",
    )(lqk, head_g, q3, kc, vc, kn3, vn3)


QPAIR = 2 * CHUNK
WIN = BAND + QPAIR


def _relbias_kernel(tab_ref, o_ref):
    h = pl.program_id(0)
    r = lax.broadcasted_iota(jnp.int32, (QPAIR, WIN), 0)
    j = lax.broadcasted_iota(jnp.int32, (QPAIR, WIN), 1)
    rel = jnp.clip(r - j + BAND, -REL_CLIP, REL_CLIP) + REL_CLIP
    qc = r // CHUNK
    kc = j // CHUNK - BAND_CHUNKS
    allowed = (kc <= qc) & (kc >= qc - BAND_CHUNKS)

    def body(t, acc):
        return jnp.where(rel == t, tab_ref[h, t], acc)

    acc = lax.fori_loop(0, 2 * REL_CLIP + 1, body, jnp.zeros((QPAIR, WIN), F32))
    o_ref[0] = jnp.where(allowed, acc, NEG)


def _relbias(table):
    return pl.pallas_call(
        _relbias_kernel,
        out_shape=jax.ShapeDtypeStruct((H_B, QPAIR, WIN), F32),
        grid=(H_B,),
        in_specs=[pl.BlockSpec(memory_space=pltpu.SMEM)],
        out_specs=pl.BlockSpec((1, QPAIR, WIN), lambda h: (h, 0, 0)),
        compiler_params=_cparams(("arbitrary",)),
        name="relbias",
    )(table)


BAND_TQ = 512


def _band_prompt_kernel(bias_ref, q_ref, kp_ref, kc_ref, vp_ref, vc_ref, o_ref, kcat, vcat):
    i = pl.program_id(1)
    kcat[0:BAND_TQ, :] = kp_ref[0].astype(BF16)
    kcat[BAND_TQ:, :] = kc_ref[0].astype(BF16)
    vcat[0:BAND_TQ, :] = vp_ref[0].astype(BF16)
    vcat[BAND_TQ:, :] = vc_ref[0].astype(BF16)
    lane = lax.broadcasted_iota(jnp.int32, (QPAIR, LANES), 1)

    def pair_block(qp, carry, *, first_block):
        r0 = pl.multiple_of(qp * QPAIR, QPAIR)
        q = q_ref[0, pl.ds(r0, QPAIR), :]
        if first_block:
            kpos = r0 - BAND_TQ + lax.broadcasted_iota(jnp.int32, (1, WIN), 1)
            before_start = jnp.where(kpos >= 0, 0.0, NEG)
        scores = []
        for hp in range(H_B // 2):
            cols = slice(hp * LANES, (hp + 1) * LANES)
            qs = jnp.concatenate(_split_halves(q[:, cols]), axis=0)
            s = _nt(qs, kcat[pl.ds(r0, WIN), cols]) + bias_ref[hp]
            scores.append(s + before_start if first_block else s)
        probs = []
        for s in scores:
            e = jnp.exp(s - jnp.max(s, axis=-1, keepdims=True))
            probs.append((e.astype(BF16), jnp.sum(e, axis=-1, keepdims=True)))
        outs = []
        for hp, (e, l) in enumerate(probs):
            cols = slice(hp * LANES, (hp + 1) * LANES)
            o = jnp.dot(e, vcat[pl.ds(r0, WIN), cols], preferred_element_type=F32) / l
            outs.append(jnp.where(lane < DH_B, o[:QPAIR], o[QPAIR:]))
        o_ref[0, pl.ds(r0, QPAIR), :] = jnp.concatenate(outs, axis=1).astype(o_ref.dtype)
        return carry

    @pl.when(i == 0)
    def _():
        lax.fori_loop(0, BAND_TQ // QPAIR, functools.partial(pair_block, first_block=True), 0)

    @pl.when(i > 0)
    def _():
        lax.fori_loop(0, BAND_TQ // QPAIR, functools.partial(pair_block, first_block=False), 0)


def _band_prompt(q3, kv3, bias):
    b, t, _ = q3.shape
    assert t % BAND_TQ == 0
    w = H_B * DH_B
    cq, ck, cv = 1, 0, 1
    prev = lambda col: pl.BlockSpec((1, BAND_TQ, w), lambda bi, i: (bi, jnp.maximum(i - 1, 0), col))
    cur = lambda col: pl.BlockSpec((1, BAND_TQ, w), lambda bi, i: (bi, i, col))
    return pl.pallas_call(
        _band_prompt_kernel,
        out_shape=jax.ShapeDtypeStruct((b, t, w), BF16),
        grid=(b, t // BAND_TQ),
        in_specs=[
            pl.BlockSpec((H_B // 2, 2 * QPAIR, WIN), lambda bi, i: (0, 0, 0)),
            cur(cq), prev(ck), cur(ck), prev(cv), cur(cv),
        ],
        out_specs=pl.BlockSpec((1, BAND_TQ, w), lambda bi, i: (bi, i, 0)),
        scratch_shapes=[pltpu.VMEM((2 * BAND_TQ, w), BF16), pltpu.VMEM((2 * BAND_TQ, w), BF16)],
        compiler_params=_cparams(("parallel", "arbitrary")),
        name="band_prompt",
    )(bias.reshape(H_B // 2, 2 * QPAIR, WIN), q3, kv3, kv3, kv3, kv3)


def _band_sample_kernel(bc_ref, bn_ref, q_ref, kc_ref, vc_ref, kn_ref, vn_ref, o_ref, *, past):
    t = q_ref.shape[1]
    nband = kc_ref.shape[1]
    lane = lax.broadcasted_iota(jnp.int32, (t, LANES), 1)
    qpos_c = past + lax.broadcasted_iota(jnp.int32, (t, nband), 0)
    kpos_c = past - nband + lax.broadcasted_iota(jnp.int32, (t, nband), 1)
    qpos_n = past + lax.broadcasted_iota(jnp.int32, (t, t), 0)
    kpos_n = past + lax.broadcasted_iota(jnp.int32, (t, t), 1)

    def allowed(qpos, kpos):
        qc, kc = qpos // CHUNK, kpos // CHUNK
        return (kpos >= 0) & (kc <= qc) & (kc >= qc - BAND_CHUNKS)

    ok_c = allowed(qpos_c, kpos_c)
    ok_n = allowed(qpos_n, kpos_n)
    for hp in range(H_B // 2):
        cols = slice(hp * LANES, (hp + 1) * LANES)
        halves = []
        for sub, qm in enumerate(_split_halves(q_ref[0, :, cols])):
            hd = 2 * hp + sub
            pieces = [(kc_ref[0, :, cols].astype(BF16), vc_ref[0, :, cols].astype(BF16),
                       jnp.where(ok_c, bc_ref[hd], NEG)),
                      (kn_ref[0, :, cols].astype(BF16), vn_ref[0, :, cols].astype(BF16),
                       jnp.where(ok_n, bn_ref[hd], NEG))]
            halves.append(_pieces_attention(qm, pieces))
        o_ref[0, :, cols] = jnp.where(lane < DH_B, halves[0], halves[1]).astype(o_ref.dtype)


def _band_sample(q3, kv3, kc, vc, bias, past):
    b, t, _ = q3.shape
    nband = kc.shape[1]
    assert nband == BAND and t <= CHUNK
    w = H_B * DH_B
    bias_c = bias[:, :t, :nband]
    bias_n = bias[:, :t, nband:nband + t]
    blk = lambda col: pl.BlockSpec((1, t, w), lambda bi: (bi, 0, col))
    cache = pl.BlockSpec((1, nband, w), lambda bi: (bi, 0, 0))
    return pl.pallas_call(
        functools.partial(_band_sample_kernel, past=past),
        out_shape=jax.ShapeDtypeStruct((b, t, w), BF16),
        grid=(b,),
        in_specs=[
            pl.BlockSpec((H_B, t, nband), lambda bi: (0, 0, 0)),
            pl.BlockSpec((H_B, t, t), lambda bi: (0, 0, 0)),
            blk(1), cache, cache, blk(0), blk(1),
        ],
        out_specs=pl.BlockSpec((1, t, w), lambda bi: (bi, 0, 0)),
        compiler_params=_cparams(("parallel",)),
        name="band_sample",
    )(bias_c, bias_n, q3, kc, vc, kv3, kv3)


CONV_PAD = 8
MLSTM_L = 256


def _mlstm_kernel(cqk_ref, cv_ref, co_ref, cif_ref, conv0_ref, cw_ref, cb_ref, gb_ref, hg_ref,
                  c0_ref, n0_ref, m0_ref, hs_ref, c_ref, n_ref, m_ref, xext, *, L):
    @pl.when(pl.program_id(1) == 0)
    def _():
        xext[0:CONV_PAD, :] = conv0_ref[0]
        c_ref[...] = c0_ref[...]
        n_ref[...] = n0_ref[...]
        m_ref[...] = m0_ref[...]

    xext[CONV_PAD:CONV_PAD + L, :] = cqk_ref[0]
    base = CONV_PAD - (CONV_W - 1)
    u = 0.0
    for j in range(CONV_W):
        u = u + xext[base + j:base + j + L, :] * cw_ref[j:j + 1, :]
    u = cb_ref[...] + u
    u = u * jax.nn.sigmoid(u)
    xext[0:CONV_PAD, :] = xext[L:L + CONV_PAD, :]

    z = cif_ref[0] + gb_ref[...]
    lane = lax.broadcasted_iota(jnp.int32, (L, LANES), 1)
    lf = jnp.minimum(z, 0.0) - jnp.log1p(jnp.exp(-jnp.abs(z)))
    tr = lax.broadcasted_iota(jnp.int32, (L, L), 0)
    tc = lax.broadcasted_iota(jnp.int32, (L, L), 1)
    causal = tr >= tc
    b_all = jnp.dot(causal.astype(F32), lf, preferred_element_type=F32,
                    precision=lax.Precision.HIGHEST)
    sr = lax.broadcasted_iota(jnp.int32, (8, LANES), 0)
    sc = lax.broadcasted_iota(jnp.int32, (8, LANES), 1)
    sel = jnp.where(sr < H_C, jnp.where(sc == sr, 1.0, jnp.where(sc == sr + H_C, -1.0, 0.0)), 0.0)
    rt = lax.dot_general(sel, jnp.where(lane < H_C, z, b_all), NT_DIMS,
                         preferred_element_type=F32, precision=lax.Precision.HIGHEST)
    er = lax.broadcasted_iota(jnp.int32, (DH_C, DH_C), 0)
    ec = lax.broadcasted_iota(jnp.int32, (DH_C, DH_C), 1)
    eye = jnp.where(er == ec, 1.0, 0.0).astype(BF16)

    for h in range(H_C):
        cols = slice(h * DH_C, (h + 1) * DH_C)
        q = u[:, h * DH_C:(h + 1) * DH_C]
        k = u[:, D_C + h * DH_C:D_C + (h + 1) * DH_C] * (DH_C ** -0.5)
        v = cv_ref[0, :, cols]
        qb, kb_, vb_ = q.astype(BF16), k.astype(BF16), v.astype(BF16)
        b_col = b_all[:, H_C + h:H_C + h + 1]
        li_col = z[:, h:h + 1]
        m_prev = m_ref[0, h:h + 1, 0:1]
        cs = c_ref[0, h]
        ns = n_ref[0, h:h + 1, :]

        dmat = jnp.where(causal, b_col + rt[h:h + 1, :], NEG)
        inter = b_col + m_prev
        m_t = jnp.maximum(inter, jnp.max(dmat, axis=-1, keepdims=True))
        w_intra = jnp.exp(dmat - m_t)
        w_inter = jnp.exp(inter - m_t)
        a = w_intra * _nt(qb, kb_)
        num = (jnp.dot(a.astype(BF16), vb_, preferred_element_type=F32)
               + w_inter * _nt(qb, cs.astype(BF16)))
        den = jnp.sum(a, axis=-1, keepdims=True) + w_inter * jnp.sum(q * ns, axis=-1, keepdims=True)
        hh = num / jnp.maximum(jnp.abs(den), jnp.exp(-m_t))

        b_last = b_col[L - 1:L, :]
        g = b_last - b_col + li_col
        m_new = jnp.maximum(b_last + m_prev, jnp.max(g, axis=0, keepdims=True))
        ws = jnp.exp(g - m_new)
        decay = jnp.exp(b_last + m_prev - m_new)
        vwt = _nt(eye, (ws * v).astype(BF16)).astype(BF16)
        c_ref[0, h] = decay * cs + jnp.dot(vwt, kb_, preferred_element_type=F32)
        n_ref[0, h:h + 1, :] = decay * ns + jnp.sum(ws * k, axis=0, keepdims=True)
        m_ref[0, h:h + 1, :] = jnp.broadcast_to(m_new, (1, LANES))

        gate = jax.nn.sigmoid(co_ref[0, :, cols])
        hs_ref[0, :, cols] = (_rms(hh, hg_ref[...]) * gate).astype(hs_ref.dtype)


def _mlstm(cqk3, cvo3, cif3, conv0, conv_w, conv_b, gate_b, head_g, c0, n0, m0, L):
    b, t, _ = cqk3.shape
    nc = t // L
    blk = lambda col, w: pl.BlockSpec((1, L, w), lambda bi, c: (bi, c, col))
    const = lambda shape: pl.BlockSpec(shape, lambda bi, c: (0,) * len(shape))
    per_b = lambda shape: pl.BlockSpec((1,) + shape, lambda bi, c: (bi,) + (0,) * len(shape))
    return pl.pallas_call(
        functools.partial(_mlstm_kernel, L=L),
        out_shape=(jax.ShapeDtypeStruct((b, t, D_C), BF16),
                   jax.ShapeDtypeStruct((b, H_C, DH_C, DH_C), F32),
                   jax.ShapeDtypeStruct((b, H_C, DH_C), F32),
                   jax.ShapeDtypeStruct((b, H_C, LANES), F32)),
        grid=(b, nc),
        in_specs=[
            blk(0, 2 * D_C), blk(0, D_C), blk(1, D_C), blk(0, LANES),
            per_b((CONV_PAD, 2 * D_C)), const((CONV_W, 2 * D_C)), const((1, 2 * D_C)),
            const((1, LANES)), const((1, DH_C)),
            per_b((H_C, DH_C, DH_C)), per_b((H_C, DH_C)), per_b((H_C, LANES)),
        ],
        out_specs=(pl.BlockSpec((1, L, D_C), lambda bi, c: (bi, c, 0)),
                   per_b((H_C, DH_C, DH_C)), per_b((H_C, DH_C)), per_b((H_C, LANES))),
        scratch_shapes=[pltpu.VMEM((CONV_PAD + L, 2 * D_C), F32)],
        compiler_params=_cparams(("parallel", "arbitrary")),
        name="mlstm",
    )(cqk3, cvo3, cvo3, cif3, conv0, conv_w, conv_b, gate_b, head_g, c0, n0, m0)


def _mix_kernel(oa_ref, ob_ref, oc_ref, ga_ref, gb_ref, gc_ref, x_ref, wa_ref, wb_ref, wc_ref, wo_ref, o_ref):
    mixed = (ga_ref[...] * jnp.dot(oa_ref[...], wa_ref[...], preferred_element_type=F32)
             + gb_ref[...] * jnp.dot(ob_ref[...], wb_ref[...], preferred_element_type=F32)
             + gc_ref[...] * jnp.dot(oc_ref[...], wc_ref[...], preferred_element_type=F32))
    o_ref[...] = x_ref[...] + jnp.dot(mixed.astype(BF16), wo_ref[...], preferred_element_type=F32)


def _mix(oa, ob, oc, gates, x, l, wa, wb, wc, wo):
    m, d = x.shape
    tm = min(m, 512)
    row = lambda w, col=0: pl.BlockSpec((tm, w), lambda i: (i, col))
    full = lambda a: pl.BlockSpec((None,) + a.shape[1:], lambda i: (l, 0, 0))
    g0 = 0
    return pl.pallas_call(
        _mix_kernel,
        out_shape=jax.ShapeDtypeStruct((m, d), F32),
        grid=(m // tm,),
        in_specs=[row(oa.shape[1]), row(ob.shape[1]), row(oc.shape[1]),
                  row(d, g0), row(d, g0 + 1), row(d, g0 + 2), row(d),
                  full(wa), full(wb), full(wc), full(wo)],
        out_specs=row(d),
        compiler_params=_cparams(("parallel",)),
        name="mix",
    )(oa, ob, oc, gates, gates, gates, x, wa, wb, wc, wo)


def _cross_kernel(x_ref, g_ref, wq_ref, mk_ref, mv_ref, wo_ref, o_ref):
    x = x_ref[0]
    h = _rms(x, g_ref[...]).astype(BF16)
    q = (jnp.dot(h, wq_ref[...], preferred_element_type=F32) * (DH_M ** -0.5)).astype(BF16)
    outs = []
    for hd in range(H_M):
        cols = slice(hd * DH_M, (hd + 1) * DH_M)
        s = _nt(q[:, cols], mk_ref[0, :, cols].astype(BF16))
        m = jnp.max(s, axis=-1, keepdims=True)
        e = jnp.exp(s - m)
        l = jnp.sum(e, axis=-1, keepdims=True)
        o = jnp.dot(e.astype(BF16), mv_ref[0, :, cols].astype(BF16), preferred_element_type=F32) / l
        outs.append(o.astype(BF16))
    o = jnp.concatenate(outs, axis=1)
    o_ref[0] = x + jnp.dot(o, wo_ref[...], preferred_element_type=F32)


def _cross(x3, g, l, wq, mk, mv, wo):
    b, t, d = x3.shape
    tm = min(t, 512)
    nm = mk.shape[1]
    full = lambda a: pl.BlockSpec((None,) + a.shape[1:], lambda bi, i: (l, 0, 0))
    return pl.pallas_call(
        _cross_kernel,
        out_shape=jax.ShapeDtypeStruct((b, t, d), F32),
        grid=(b, t // tm),
        in_specs=[pl.BlockSpec((1, tm, d), lambda bi, i: (bi, i, 0)),
                  pl.BlockSpec(g.shape, lambda bi, i: (0, 0)), full(wq),
                  pl.BlockSpec((1, nm, d), lambda bi, i: (bi, 0, 0)),
                  pl.BlockSpec((1, nm, d), lambda bi, i: (bi, 0, 0)), full(wo)],
        out_specs=pl.BlockSpec((1, tm, d), lambda bi, i: (bi, i, 0)),
        compiler_params=_cparams(("parallel", "parallel")),
        name="cross",
    )(x3, g, wq, mk, mv, wo)


def _ffn_kernel(x_ref, g_ref, wg_ref, wu_ref, wd_ref, gf_ref, o_ref, h_scr, acc, *, final_norm):
    j = pl.program_id(1)

    @pl.when(j == 0)
    def _():
        h_scr[...] = _rms(x_ref[...], g_ref[...]).astype(BF16)
        acc[...] = jnp.zeros(acc.shape, F32)

    h = h_scr[...]
    gate = jnp.dot(h, wg_ref[...], preferred_element_type=F32)
    up = jnp.dot(h, wu_ref[...], preferred_element_type=F32)
    a = (gate * jax.nn.sigmoid(gate) * up).astype(BF16)
    acc[...] += jnp.dot(a, wd_ref[...], preferred_element_type=F32)

    @pl.when(j == pl.num_programs(1) - 1)
    def _():
        y = x_ref[...] + acc[...]
        if final_norm:
            y = _rms(y, gf_ref[...])
        o_ref[...] = y


FFN_TF = 1408
FFN_TM = 512


def _ffn(x, g, l, wg, wu, wd, g_final, final_norm):
    m, d = x.shape
    tm = min(m, FFN_TM)
    return pl.pallas_call(
        functools.partial(_ffn_kernel, final_norm=final_norm),
        out_shape=jax.ShapeDtypeStruct((m, d), F32),
        grid=(m // tm, D_FF // FFN_TF),
        in_specs=[
            pl.BlockSpec((tm, d), lambda i, j: (i, 0)),
            pl.BlockSpec((1, d), lambda i, j: (0, 0)),
            pl.BlockSpec((None, d, FFN_TF), lambda i, j: (l, 0, j)),
            pl.BlockSpec((None, d, FFN_TF), lambda i, j: (l, 0, j)),
            pl.BlockSpec((None, FFN_TF, d), lambda i, j: (l, j, 0)),
            pl.BlockSpec((1, d), lambda i, j: (0, 0)),
        ],
        out_specs=pl.BlockSpec((tm, d), lambda i, j: (i, 0)),
        scratch_shapes=[pltpu.VMEM((tm, d), BF16), pltpu.VMEM((tm, d), F32)],
        compiler_params=_cparams(("parallel", "arbitrary")),
        name="ffn",
    )(x, g, wg, wu, wd, g_final)


def _layer(x3, l, w, mem_k, mem_v, cache, final_norm, akv_all):
    b, t, d = x3.shape
    x2 = x3.reshape(b * t, d)
    row = lambda a: a.reshape(1, -1)
    lam_init = 0.8 - 0.6 * math.exp(-0.3 * l)

    gates, cqk, cif = _in_proj_a(x2, row(w['g_mix']), l, w['w_in'], w['w_if'])
    q, ak_all, av_all, bkv, cvo = _in_proj_b(x2, row(w['g_mix']), w['w_in'], l, *akv_all)
    q3 = q.reshape(b, t, SEG)
    bkv3 = bkv.reshape(b, t, SEG)
    ak3 = ak_all.reshape(DEPTH * b, t, HALF)
    av3 = av_all.reshape(DEPTH * b, t, HALF)
    cqk3 = cqk.reshape(b, t, 2 * D_C)
    keep = t if cache is not None else min(BAND, t)
    b_k = bkv3[:, t - keep:, :HALF].reshape(b, keep, H_B, DH_B)
    b_v = bkv3[:, t - keep:, HALF:].reshape(b, keep, H_B, DH_B)

    gate_b = jnp.concatenate([w['b_i'], w['b_f'], jnp.zeros((LANES - 2 * H_C,), F32)]).reshape(1, LANES)
    if cache is None:
        oa = _diff_prompt(q3, ak3, av3, l, w['lqk'], row(w['a_head_g']), lam_init)
        ob = _band_prompt(q3, bkv3, w['relbias'])
        conv0 = jnp.zeros((b, CONV_PAD, 2 * D_C), F32)
        c0 = jnp.zeros((b, H_C, DH_C, DH_C), F32)
        n0 = jnp.zeros((b, H_C, DH_C), F32)
        m0 = jnp.zeros((b, H_C, LANES), F32)
        L = min(t, MLSTM_L)
    else:
        past = cache['a_k'].shape[1]
        oa = _diff_sample(q3, ak3, av3, l, cache['a_k'].reshape(b, past, -1), cache['a_v'].reshape(b, past, -1),
                          w['lqk'], row(w['a_head_g']), lam_init)
        nband = cache['b_k'].shape[1]
        ob = _band_sample(q3, bkv3, cache['b_k'].reshape(b, nband, -1), cache['b_v'].reshape(b, nband, -1),
                          w['relbias'], past)
        conv0 = jnp.pad(cache['conv'], ((0, 0), (CONV_PAD - (CONV_W - 1), 0), (0, 0)))
        c0, n0 = cache['C'], cache['n']
        m0 = jnp.broadcast_to(cache['m'][:, :, None], (b, H_C, LANES))
        L = t
    oc, c_new, n_new, m_new = _mlstm(cqk3, cvo.reshape(b, t, SEG), cif.reshape(b, t, LANES), conv0,
                                     w['conv_w'], row(w['conv_b']), gate_b, row(w['c_head_g']), c0, n0, m0, L)
    assert t >= CONV_W - 1
    conv_new = cqk3[:, t - (CONV_W - 1):]

    x2 = _mix(oa.reshape(b * t, -1), ob.reshape(b * t, -1), oc.reshape(b * t, -1), gates, x2, l,
              w['w_up_a'], w['w_up_b'], w['w_up_c'], w['w_o'])
    x3 = _cross(x2.reshape(b, t, d), row(w['g_cross']), l, w['w_mq'], mem_k, mem_v, w['w_mo'])
    x2 = _ffn(x3.reshape(b * t, d), row(w['g_ffn']), l, w['w_ff_g'], w['w_ff_u'], w['w_ff_d'],
              row(w['g_final']), final_norm)
    return x2.reshape(b, t, d), (ak_all, av_all), (b_k, b_v, c_new, n_new, m_new[:, :, 0], conv_new)


def kernel(x_prompt, x_sample, cache_a_k, cache_a_v, cache_b_k, cache_b_v, state_c_C, state_c_n, state_c_m, state_c_conv, cache_mem_k, cache_mem_v, mem_prompt, g_mix, w_in, a_lq1, a_lk1, a_lq2, a_lk2, a_head_g, b_rel, c_conv_w, c_conv_b, c_b_i, c_b_f, c_head_g, w_up_a, w_up_b, w_up_c, w_o, g_cross, w_mq, w_mk, w_mv, w_mo, g_ffn, w_ff_g, w_ff_u, w_ff_d, g_final):
    xp, xs = x_prompt, x_sample
    bp = x_prompt.shape[0]
    n_mem = mem_prompt.shape[1]
    mem2 = mem_prompt.reshape(bp * n_mem, D_MODEL)
    new_p = [[] for _ in range(8)]
    new_s = [[] for _ in range(6)]
    akv_p = akv_s = (None, None)
    bf = lambda a: a.astype(BF16)
    wp, wif = _prep_w_in(w_in)
    wts = dict(w_up_a=bf(w_up_a), w_up_b=bf(w_up_b), w_up_c=bf(w_up_c), w_o=bf(w_o), w_mq=bf(w_mq),
               w_mo=bf(w_mo), w_ff_g=bf(w_ff_g), w_ff_u=bf(w_ff_u), w_ff_d=bf(w_ff_d))
    w_mk_b, w_mv_b = bf(w_mk), bf(w_mv)
    for l in range(DEPTH):
        w = dict(g_mix=g_mix[l], w_in=wp, w_if=wif,
                 lqk=jnp.stack([a_lq1[l], a_lk1[l], a_lq2[l], a_lk2[l]]),
                 a_head_g=a_head_g[l], relbias=_relbias(b_rel[l]), conv_w=c_conv_w[l], conv_b=c_conv_b[l],
                 b_i=c_b_i[l], b_f=c_b_f[l], c_head_g=c_head_g[l],
                 g_cross=g_cross[l], g_ffn=g_ffn[l], g_final=g_final, **wts)
        mk2 = _mem_proj(mem2, l, w_mk_b).reshape(bp, n_mem, D_MODEL)
        mv2 = _mem_proj(mem2, l, w_mv_b).reshape(bp, n_mem, D_MODEL)
        last = l == DEPTH - 1
        xp, akv_p, st_p = _layer(xp, l, w, mk2, mv2, None, last, akv_p)
        cache = dict(a_k=cache_a_k[l], a_v=cache_a_v[l], b_k=cache_b_k[l], b_v=cache_b_v[l],
                     C=state_c_C[l], n=state_c_n[l], m=state_c_m[l], conv=state_c_conv[l])
        bs = xs.shape[0]
        xs, akv_s, st_s = _layer(xs, l, w, cache_mem_k[l].reshape(bs, n_mem, D_MODEL),
                                 cache_mem_v[l].reshape(bs, n_mem, D_MODEL), cache, last, akv_s)
        mk4 = mk2.reshape(bp, n_mem, H_M, DH_M)
        mv4 = mv2.reshape(bp, n_mem, H_M, DH_M)
        for lst, a in zip(new_p, st_p + (mk4, mv4)):
            lst.append(a)
        for lst, a in zip(new_s, st_s):
            lst.append(a)
    heads = lambda a, b, t: a.reshape(DEPTH, b, t, H_A, DV_A)
    tp, ts = x_prompt.shape[1], xs.shape[1]
    outs_p = [heads(a, bp, tp) for a in akv_p] + [jnp.stack(a, 0) for a in new_p]
    outs_s = [heads(a, xs.shape[0], ts) for a in akv_s] + [jnp.stack(a, 0) for a in new_s]
    return (xp, xs) + tuple(outs_p) + tuple(outs_s)
```

```python
import functools
import math

import jax
import jax.numpy as jnp
from jax import lax
from jax.experimental import pallas as pl
from jax.experimental.pallas import tpu as pltpu

F32 = jnp.float32
BF16 = jnp.bfloat16

D_MODEL = 1024
DEPTH = 2
CHUNK = 64
EPS = 1e-6
NEG = -1e30
H_A = 4
DH_A = 64
DV_A = 2 * DH_A
H_B = 8
DH_B = 64
BAND_CHUNKS = 8
BAND = BAND_CHUNKS * CHUNK
REL_CLIP = 128
H_C = 4
DH_C = 128
D_C = H_C * DH_C
CONV_W = 4
H_M = 4
DH_M = D_MODEL // H_M
D_FF = -(-8 * D_MODEL // (3 * 256)) * 256

LANES = 128
VMEM_LIMIT = 48 * 1024 * 1024

SEG = 1024
HALF = SEG // 2
GATE_TILES = 3
N_TILES_A = GATE_TILES + 1
N_TILES_B = 4
IN_SIZES = (H_A * DH_A,) * 4 + (H_A * DV_A,) + (H_B * DH_B,) * 3 + (2 * D_C, D_C, D_C, 2 * H_C, 3 * D_MODEL)

ALIBI_SLOPES = tuple(2.0 ** (-8.0 * (i + 1) / H_A) for i in range(H_A))

NT_DIMS = (((1,), (1,)), ((), ()))


def _cparams(sem):
    return pltpu.CompilerParams(dimension_semantics=sem, vmem_limit_bytes=VMEM_LIMIT)


def _rms(x, g):
    ms = jnp.mean(x * x, axis=-1, keepdims=True)
    return x * lax.rsqrt(ms + EPS) * g


def _nt(a, b):
    return lax.dot_general(a, b, NT_DIMS, preferred_element_type=F32)


def _prep_w_in_kernel(w_ref, o_ref, oif_ref):
    offs = [0]
    for size in IN_SIZES:
        offs.append(offs[-1] + size)
    seg = lambda i: w_ref[:, offs[i]:offs[i + 1]]
    a_q1, a_q2, a_k1, a_k2, a_v, b_q, b_k, b_v, c_qk, c_v, c_o = (seg(i) for i in range(11))
    g = seg(12)

    def put(col, val):
        o_ref[:, col:col + val.shape[1]] = val.astype(BF16)

    put(0, g)
    put(GATE_TILES * SEG, c_qk)
    for h in range(H_A):
        hs = slice(h * DH_A, (h + 1) * DH_A)
        base = h * 2 * DH_A
        put(4 * SEG + base, a_q1[:, hs] * (DH_A ** -0.5))
        put(4 * SEG + base + DH_A, a_q2[:, hs] * (DH_A ** -0.5))
        put(5 * SEG + base, a_k1[:, hs])
        put(5 * SEG + base + DH_A, a_k2[:, hs])
    put(4 * SEG + HALF, b_q * (DH_B ** -0.5))
    put(5 * SEG + HALF, a_v)
    put(6 * SEG, b_k)
    put(6 * SEG + HALF, b_v)
    put(7 * SEG, c_v)
    put(7 * SEG + HALF, c_o)
    tile = w_ref[:, offs[11]:offs[11] + LANES]
    lane = lax.broadcasted_iota(jnp.int32, tile.shape, 1)
    oif_ref[...] = jnp.where(lane < 2 * H_C, tile, 0.0).astype(BF16)


def _prep_w_in(w):
    depth, k, n = w.shape
    tk = 256
    n_out = (N_TILES_A + N_TILES_B) * SEG
    return pl.pallas_call(
        _prep_w_in_kernel,
        out_shape=(jax.ShapeDtypeStruct((depth, k, n_out), BF16),
                   jax.ShapeDtypeStruct((depth, k, LANES), BF16)),
        grid=(depth, k // tk),
        in_specs=[pl.BlockSpec((None, tk, n), lambda l, i: (l, i, 0))],
        out_specs=(pl.BlockSpec((None, tk, n_out), lambda l, i: (l, i, 0)),
                   pl.BlockSpec((None, tk, LANES), lambda l, i: (l, i, 0))),
        compiler_params=_cparams(("parallel", "parallel")),
        name="prep_w_in",
    )(w)


def _norm_to_scratch(x_ref, g_ref, h_scr):
    h_scr[...] = _rms(x_ref[...], g_ref[...]).astype(BF16)


def _in_proj_a_kernel(x_ref, g_ref, w_ref, wif_ref, og_ref, ocqk_ref, ocif_ref, h_scr):
    j = pl.program_id(1)

    @pl.when(j == 0)
    def _():
        _norm_to_scratch(x_ref, g_ref, h_scr)
        ocif_ref[...] = jnp.dot(h_scr[...], wif_ref[...], preferred_element_type=F32)

    tile = lambda: jnp.dot(h_scr[...], w_ref[...], preferred_element_type=F32)

    @pl.when(j < GATE_TILES)
    def _():
        og_ref[...] = jax.nn.sigmoid(tile()).astype(og_ref.dtype)

    @pl.when(j == GATE_TILES)
    def _():
        ocqk_ref[...] = tile()


def _in_proj_a(x, g, l, wp, wif):
    m, k = x.shape
    tm = min(m, 1024)
    return pl.pallas_call(
        _in_proj_a_kernel,
        out_shape=(jax.ShapeDtypeStruct((m, GATE_TILES * SEG), BF16),
                   jax.ShapeDtypeStruct((m, SEG), F32),
                   jax.ShapeDtypeStruct((m, LANES), F32)),
        grid=(m // tm, N_TILES_A),
        in_specs=[
            pl.BlockSpec((tm, k), lambda i, j: (i, 0)),
            pl.BlockSpec((1, k), lambda i, j: (0, 0)),
            pl.BlockSpec((None, k, SEG), lambda i, j: (l, 0, j)),
            pl.BlockSpec((None, k, LANES), lambda i, j: (l, 0, 0)),
        ],
        out_specs=(pl.BlockSpec((tm, SEG), lambda i, j: (i, jnp.minimum(j, GATE_TILES - 1))),
                   pl.BlockSpec((tm, SEG), lambda i, j: (i, 0)),
                   pl.BlockSpec((tm, LANES), lambda i, j: (i, 0))),
        scratch_shapes=[pltpu.VMEM((tm, k), BF16)],
        compiler_params=_cparams(("parallel", "arbitrary")),
        name="in_proj_a",
    )(x, g, wp, wif)


def _in_proj_b_kernel(*refs, aliased, heads5d):
    x_ref, g_ref, w_ref = refs[:3]
    oq_ref, oak_ref, oav_ref, oakv_ref, obkv_ref, ocvo_ref, h_scr = refs[5 if aliased else 3:]
    j = pl.program_id(1)

    @pl.when(j == 0)
    def _():
        _norm_to_scratch(x_ref, g_ref, h_scr)

    tile = lambda: jnp.dot(h_scr[...], w_ref[...], preferred_element_type=F32)

    @pl.when(j == 0)
    def _():
        oq_ref[...] = tile().astype(oq_ref.dtype)

    @pl.when(j == 1)
    def _():
        for out_ref, off in ((oak_ref, 0), (oav_ref, HALF)):
            a = jnp.dot(h_scr[...], w_ref[:, off:off + HALF], preferred_element_type=F32)
            oakv_ref[:, off:off + HALF] = a.astype(oakv_ref.dtype)
            if heads5d:
                for hd in range(H_A):
                    out_ref[0, 0, :, hd, :] = a[:, hd * DV_A:(hd + 1) * DV_A]
            else:
                out_ref[0] = a

    @pl.when(j == 2)
    def _():
        obkv_ref[...] = tile()

    @pl.when(j == 3)
    def _():
        ocvo_ref[...] = tile()


def _in_proj_b(x, g, wp, l, ak_all, av_all, bt):
    m, k = x.shape
    tm = min(m, 1024)
    aliased = ak_all is not None
    b, t = bt
    heads5d = t % tm == 0
    if heads5d:
        stacked = jax.ShapeDtypeStruct((DEPTH, b, t, H_A, DV_A), F32)
        layer_row = pl.BlockSpec((1, 1, tm, H_A, DV_A),
                                 lambda i, j: (l, i // (t // tm), i % (t // tm), 0, 0))
    else:
        stacked = jax.ShapeDtypeStruct((DEPTH, m, HALF), F32)
        layer_row = pl.BlockSpec((1, tm, HALF), lambda i, j: (l, i, 0))
    in_specs = [
        pl.BlockSpec((tm, k), lambda i, j: (i, 0)),
        pl.BlockSpec((1, k), lambda i, j: (0, 0)),
        pl.BlockSpec((None, k, SEG), lambda i, j: (l, 0, N_TILES_A + j)),
    ]
    args = [x, g, wp]
    if aliased:
        in_specs += [pl.BlockSpec(memory_space=pl.ANY)] * 2
        args += [ak_all, av_all]
    row = lambda w: pl.BlockSpec((tm, w), lambda i, j: (i, 0))
    return pl.pallas_call(
        functools.partial(_in_proj_b_kernel, aliased=aliased, heads5d=heads5d),
        out_shape=(jax.ShapeDtypeStruct((m, SEG), BF16), stacked, stacked, jax.ShapeDtypeStruct((m, SEG), BF16),
                   jax.ShapeDtypeStruct((m, SEG), F32), jax.ShapeDtypeStruct((m, SEG), F32)),
        grid=(m // tm, N_TILES_B),
        in_specs=in_specs,
        out_specs=(row(SEG), layer_row, layer_row, row(SEG), row(SEG), row(SEG)),
        scratch_shapes=[pltpu.VMEM((tm, k), BF16)],
        input_output_aliases={3: 1, 4: 2} if aliased else {},
        compiler_params=_cparams(("parallel", "arbitrary")),
        name="in_proj_b",
    )(*args)


def _mem_proj_kernel(x_ref, w_ref, o_ref):
    o_ref[...] = jnp.dot(x_ref[...].astype(BF16), w_ref[...], preferred_element_type=F32)


def _mem_proj(x, l, w):
    m, k = x.shape
    n = w.shape[2]
    tm = min(m, 1024)
    return pl.pallas_call(
        _mem_proj_kernel,
        out_shape=jax.ShapeDtypeStruct((m, n), F32),
        grid=(m // tm,),
        in_specs=[pl.BlockSpec((tm, k), lambda i: (i, 0)), pl.BlockSpec((None, k, n), lambda i: (l, 0, 0))],
        out_specs=pl.BlockSpec((tm, n), lambda i: (i, 0)),
        compiler_params=_cparams(("parallel",)),
        name="mem_proj",
    )(x, w)


def _online_update(s, v, acc, m, l):
    slabs = [s[:, j * LANES:(j + 1) * LANES] for j in range(s.shape[1] // LANES)]
    m_old = m[...]
    m_new = jnp.maximum(m_old, jnp.max(functools.reduce(jnp.maximum, slabs), axis=-1, keepdims=True))
    alpha = jnp.exp(m_old - m_new)
    ps = [jnp.exp(c - m_new) for c in slabs]
    l[...] = alpha * l[...] + jnp.sum(functools.reduce(jnp.add, ps), axis=-1, keepdims=True)
    p = jnp.concatenate(ps, axis=1).astype(BF16)
    acc[...] = alpha * acc[...] + jnp.dot(p, v, preferred_element_type=F32)
    m[...] = m_new


def _split_halves(q):
    lane = lax.broadcasted_iota(jnp.int32, q.shape, 1)
    lo = jnp.where(lane < DH_A, q, 0.0).astype(BF16)
    hi = jnp.where(lane >= DH_A, q, 0.0).astype(BF16)
    return lo, hi


def _lambda(lqk, lam_init):
    e1 = jnp.exp(jnp.sum(lqk[0:1, :] * lqk[1:2, :], axis=-1, keepdims=True))
    e2 = jnp.exp(jnp.sum(lqk[2:3, :] * lqk[3:4, :], axis=-1, keepdims=True))
    return e1 - e2 + lam_init


def _diff_prompt_kernel(slopes_ref, lqk_ref, hg_ref, q_ref, k_ref, v_ref, o_ref,
                        acc1, acc2, m1, l1, m2, l2, *, tq, lam_init):
    h = pl.program_id(1)
    qi = pl.program_id(2)
    slope = slopes_ref[h]
    kb, vb = k_ref.at[0], v_ref.at[0]

    qa, qb = _split_halves(q_ref[0])
    for m, l, acc in ((m1, l1, acc1), (m2, l2, acc2)):
        m[...] = jnp.full(m.shape, NEG, F32)
        l[...] = jnp.zeros(l.shape, F32)
        acc[...] = jnp.zeros(acc.shape, F32)

    def full_tile(ki, carry):
        k0 = pl.multiple_of(ki * tq, tq)
        k = kb[pl.ds(k0, tq), :]
        v = vb[pl.ds(k0, tq), :]
        kpos = k0 + lax.broadcasted_iota(jnp.int32, (1, tq), 1)
        bias = slope * kpos.astype(F32)
        _online_update(_nt(qa, k) + bias, v, acc1, m1, l1)
        _online_update(_nt(qb, k) + bias, v, acc2, m2, l2)
        return carry

    lax.fori_loop(0, qi, full_tile, 0)

    q0 = pl.multiple_of(qi * tq, tq)
    k = kb[pl.ds(q0, tq), :]
    v = vb[pl.ds(q0, tq), :]
    r = lax.broadcasted_iota(jnp.int32, (tq, tq), 0)
    c = lax.broadcasted_iota(jnp.int32, (tq, tq), 1)
    allowed = (c // CHUNK) <= (r // CHUNK)
    bias = slope * (q0 + r - jnp.abs(r - c)).astype(F32)
    _online_update(jnp.where(allowed, _nt(qa, k) + bias, NEG), v, acc1, m1, l1)
    _online_update(jnp.where(allowed, _nt(qb, k) + bias, NEG), v, acc2, m2, l2)

    lam = _lambda(lqk_ref[...], lam_init)
    o = acc1[...] / l1[...] - lam * (acc2[...] / l2[...])
    o_ref[0] = (_rms(o, hg_ref[...]) * (1.0 - lam_init)).astype(o_ref.dtype)


def _diff_prompt(q3, kv3, lqk, head_g, lam_init):
    b, t, _ = q3.shape
    tq = min(t, 512)
    slopes = jnp.asarray(ALIBI_SLOPES, F32)
    return pl.pallas_call(
        functools.partial(_diff_prompt_kernel, tq=tq, lam_init=lam_init),
        out_shape=jax.ShapeDtypeStruct((b, t, H_A * DV_A), BF16),
        grid=(b, H_A, t // tq),
        in_specs=[
            pl.BlockSpec(memory_space=pltpu.SMEM),
            pl.BlockSpec((4, DH_A), lambda bi, h, qi: (0, 0)),
            pl.BlockSpec((1, DV_A), lambda bi, h, qi: (0, 0)),
            pl.BlockSpec((1, tq, LANES), lambda bi, h, qi: (bi, qi, h)),
            pl.BlockSpec((1, t, LANES), lambda bi, h, qi: (bi, 0, h)),
            pl.BlockSpec((1, t, LANES), lambda bi, h, qi: (bi, 0, H_A + h)),
        ],
        out_specs=pl.BlockSpec((1, tq, LANES), lambda bi, h, qi: (bi, qi, h)),
        scratch_shapes=[
            pltpu.VMEM((tq, LANES), F32), pltpu.VMEM((tq, LANES), F32),
            pltpu.VMEM((tq, LANES), F32), pltpu.VMEM((tq, LANES), F32),
            pltpu.VMEM((tq, LANES), F32), pltpu.VMEM((tq, LANES), F32),
        ],
        compiler_params=_cparams(("parallel", "parallel", "arbitrary")),
        name="diff_prompt",
    )(slopes, lqk, head_g, q3, kv3, kv3)


def _pieces_attention(qm, pieces):
    ss = [_nt(qm, k) + bias for k, _, bias in pieces]
    m = functools.reduce(jnp.maximum, [jnp.max(s, axis=-1, keepdims=True) for s in ss])
    es = [jnp.exp(s - m) for s in ss]
    l = functools.reduce(jnp.add, [jnp.sum(e, axis=-1, keepdims=True) for e in es])
    o = functools.reduce(jnp.add, [jnp.dot(e.astype(BF16), v, preferred_element_type=F32)
                                   for e, (_, v, _) in zip(es, pieces)])
    return o / l


def _diff_sample_kernel(lqk_ref, hg_ref, q_ref, kc_ref, vc_ref, kn_ref, vn_ref, o_ref, *, lam_init):
    t = q_ref.shape[1]
    past = kc_ref.shape[1]
    lam = _lambda(lqk_ref[...], lam_init)
    rc = lax.broadcasted_iota(jnp.int32, (t, past), 0)
    cc = lax.broadcasted_iota(jnp.int32, (t, past), 1)
    rn = lax.broadcasted_iota(jnp.int32, (t, t), 0)
    cn = lax.broadcasted_iota(jnp.int32, (t, t), 1)
    dist_c = jnp.abs(past + rc - cc).astype(F32)
    dist_n = jnp.abs(rn - cn).astype(F32)
    ok_c = (cc // CHUNK) <= ((past + rc) // CHUNK)
    ok_n = ((past + cn) // CHUNK) <= ((past + rn) // CHUNK)
    for h in range(H_A):
        cols = slice(h * LANES, (h + 1) * LANES)
        bias_c = jnp.where(ok_c, -ALIBI_SLOPES[h] * dist_c, NEG)
        bias_n = jnp.where(ok_n, -ALIBI_SLOPES[h] * dist_n, NEG)
        pieces = [(kc_ref[0, :, cols].astype(BF16), vc_ref[0, :, cols].astype(BF16), bias_c),
                  (kn_ref[0, :, cols].astype(BF16), vn_ref[0, :, cols].astype(BF16), bias_n)]
        qa, qb = _split_halves(q_ref[0, :, cols])
        o = _pieces_attention(qa, pieces) - lam * _pieces_attention(qb, pieces)
        o_ref[0, :, cols] = (_rms(o, hg_ref[...]) * (1.0 - lam_init)).astype(o_ref.dtype)


def _diff_sample(q3, kn3, vn3, l, kc, vc, lqk, head_g, lam_init):
    b, t, _ = q3.shape
    past = kc.shape[1]
    w = H_A * DV_A
    new = pl.BlockSpec((1, t, w), lambda bi: (l * b + bi, 0, 0))
    cache = pl.BlockSpec((1, past, w), lambda bi: (bi, 0, 0))
    return pl.pallas_call(
        functools.partial(_diff_sample_kernel, lam_init=lam_init),
        out_shape=jax.ShapeDtypeStruct((b, t, w), BF16),
        grid=(b,),
        in_specs=[
            pl.BlockSpec((4, DH_A), lambda bi: (0, 0)),
            pl.BlockSpec((1, DV_A), lambda bi: (0, 0)),
            pl.BlockSpec((1, t, w), lambda bi: (bi, 0, 0)), cache, cache, new, new,
        ],
        out_specs=pl.BlockSpec((1, t, w), lambda bi: (bi, 0, 0)),
        compiler_params=_cparams(("parallel",)),
        name="diff_sample",
    )(lqk, head_g, q3, kc, vc, kn3, vn3)


QPAIR = 2 * CHUNK
WIN = BAND + QPAIR


def _relbias_kernel(tab_ref, o_ref):
    h = pl.program_id(0)
    r = lax.broadcasted_iota(jnp.int32, (QPAIR, WIN), 0)
    j = lax.broadcasted_iota(jnp.int32, (QPAIR, WIN), 1)
    rel = jnp.clip(r - j + BAND, -REL_CLIP, REL_CLIP) + REL_CLIP
    qc = r // CHUNK
    kc = j // CHUNK - BAND_CHUNKS
    allowed = (kc <= qc) & (kc >= qc - BAND_CHUNKS)

    def body(t, acc):
        return jnp.where(rel == t, tab_ref[h, t], acc)

    acc = lax.fori_loop(0, 2 * REL_CLIP + 1, body, jnp.zeros((QPAIR, WIN), F32))
    o_ref[0] = jnp.where(allowed, acc, NEG)


def _relbias(table):
    return pl.pallas_call(
        _relbias_kernel,
        out_shape=jax.ShapeDtypeStruct((H_B, QPAIR, WIN), F32),
        grid=(H_B,),
        in_specs=[pl.BlockSpec(memory_space=pltpu.SMEM)],
        out_specs=pl.BlockSpec((1, QPAIR, WIN), lambda h: (h, 0, 0)),
        compiler_params=_cparams(("arbitrary",)),
        name="relbias",
    )(table)


BAND_TQ = 512


def _band_prompt_kernel(bias_ref, q_ref, kp_ref, kc_ref, vp_ref, vc_ref, o_ref, kcat, vcat):
    i = pl.program_id(1)
    kcat[0:BAND_TQ, :] = kp_ref[0].astype(BF16)
    kcat[BAND_TQ:, :] = kc_ref[0].astype(BF16)
    vcat[0:BAND_TQ, :] = vp_ref[0].astype(BF16)
    vcat[BAND_TQ:, :] = vc_ref[0].astype(BF16)
    lane = lax.broadcasted_iota(jnp.int32, (QPAIR, LANES), 1)

    def pair_block(qp, carry, *, first_block):
        r0 = pl.multiple_of(qp * QPAIR, QPAIR)
        q = q_ref[0, pl.ds(r0, QPAIR), :]
        if first_block:
            kpos = r0 - BAND_TQ + lax.broadcasted_iota(jnp.int32, (1, WIN), 1)
            before_start = jnp.where(kpos >= 0, 0.0, NEG)
        scores = []
        for hp in range(H_B // 2):
            cols = slice(hp * LANES, (hp + 1) * LANES)
            qs = jnp.concatenate(_split_halves(q[:, cols]), axis=0)
            s = _nt(qs, kcat[pl.ds(r0, WIN), cols]) + bias_ref[hp]
            scores.append(s + before_start if first_block else s)
        probs = []
        for s in scores:
            e = jnp.exp(s - jnp.max(s, axis=-1, keepdims=True))
            probs.append((e.astype(BF16), jnp.sum(e, axis=-1, keepdims=True)))
        outs = []
        for hp, (e, l) in enumerate(probs):
            cols = slice(hp * LANES, (hp + 1) * LANES)
            o = jnp.dot(e, vcat[pl.ds(r0, WIN), cols], preferred_element_type=F32) / l
            outs.append(jnp.where(lane < DH_B, o[:QPAIR], o[QPAIR:]))
        o_ref[0, pl.ds(r0, QPAIR), :] = jnp.concatenate(outs, axis=1).astype(o_ref.dtype)
        return carry

    @pl.when(i == 0)
    def _():
        lax.fori_loop(0, BAND_TQ // QPAIR, functools.partial(pair_block, first_block=True), 0)

    @pl.when(i > 0)
    def _():
        lax.fori_loop(0, BAND_TQ // QPAIR, functools.partial(pair_block, first_block=False), 0)


def _band_prompt(q3, kv3, bias):
    b, t, _ = q3.shape
    assert t % BAND_TQ == 0
    w = H_B * DH_B
    cq, ck, cv = 1, 0, 1
    prev = lambda col: pl.BlockSpec((1, BAND_TQ, w), lambda bi, i: (bi, jnp.maximum(i - 1, 0), col))
    cur = lambda col: pl.BlockSpec((1, BAND_TQ, w), lambda bi, i: (bi, i, col))
    return pl.pallas_call(
        _band_prompt_kernel,
        out_shape=jax.ShapeDtypeStruct((b, t, w), BF16),
        grid=(b, t // BAND_TQ),
        in_specs=[
            pl.BlockSpec((H_B // 2, 2 * QPAIR, WIN), lambda bi, i: (0, 0, 0)),
            cur(cq), prev(ck), cur(ck), prev(cv), cur(cv),
        ],
        out_specs=pl.BlockSpec((1, BAND_TQ, w), lambda bi, i: (bi, i, 0)),
        scratch_shapes=[pltpu.VMEM((2 * BAND_TQ, w), BF16), pltpu.VMEM((2 * BAND_TQ, w), BF16)],
        compiler_params=_cparams(("parallel", "arbitrary")),
        name="band_prompt",
    )(bias.reshape(H_B // 2, 2 * QPAIR, WIN), q3, kv3, kv3, kv3, kv3)


def _band_sample_kernel(bc_ref, bn_ref, q_ref, kc_ref, vc_ref, kn_ref, vn_ref, o_ref, *, past):
    t = q_ref.shape[1]
    nband = kc_ref.shape[1]
    lane = lax.broadcasted_iota(jnp.int32, (t, LANES), 1)
    qpos_c = past + lax.broadcasted_iota(jnp.int32, (t, nband), 0)
    kpos_c = past - nband + lax.broadcasted_iota(jnp.int32, (t, nband), 1)
    qpos_n = past + lax.broadcasted_iota(jnp.int32, (t, t), 0)
    kpos_n = past + lax.broadcasted_iota(jnp.int32, (t, t), 1)

    def allowed(qpos, kpos):
        qc, kc = qpos // CHUNK, kpos // CHUNK
        return (kpos >= 0) & (kc <= qc) & (kc >= qc - BAND_CHUNKS)

    ok_c = allowed(qpos_c, kpos_c)
    ok_n = allowed(qpos_n, kpos_n)
    for hp in range(H_B // 2):
        cols = slice(hp * LANES, (hp + 1) * LANES)
        halves = []
        for sub, qm in enumerate(_split_halves(q_ref[0, :, cols])):
            hd = 2 * hp + sub
            pieces = [(kc_ref[0, :, cols].astype(BF16), vc_ref[0, :, cols].astype(BF16),
                       jnp.where(ok_c, bc_ref[hd], NEG)),
                      (kn_ref[0, :, cols].astype(BF16), vn_ref[0, :, cols].astype(BF16),
                       jnp.where(ok_n, bn_ref[hd], NEG))]
            halves.append(_pieces_attention(qm, pieces))
        o_ref[0, :, cols] = jnp.where(lane < DH_B, halves[0], halves[1]).astype(o_ref.dtype)


def _band_sample(q3, kv3, kc, vc, bias, past):
    b, t, _ = q3.shape
    nband = kc.shape[1]
    assert nband == BAND and t <= CHUNK
    w = H_B * DH_B
    bias_c = bias[:, :t, :nband]
    bias_n = bias[:, :t, nband:nband + t]
    blk = lambda col: pl.BlockSpec((1, t, w), lambda bi: (bi, 0, col))
    cache = pl.BlockSpec((1, nband, w), lambda bi: (bi, 0, 0))
    return pl.pallas_call(
        functools.partial(_band_sample_kernel, past=past),
        out_shape=jax.ShapeDtypeStruct((b, t, w), BF16),
        grid=(b,),
        in_specs=[
            pl.BlockSpec((H_B, t, nband), lambda bi: (0, 0, 0)),
            pl.BlockSpec((H_B, t, t), lambda bi: (0, 0, 0)),
            blk(1), cache, cache, blk(0), blk(1),
        ],
        out_specs=pl.BlockSpec((1, t, w), lambda bi: (bi, 0, 0)),
        compiler_params=_cparams(("parallel",)),
        name="band_sample",
    )(bias_c, bias_n, q3, kc, vc, kv3, kv3)


CONV_PAD = 8
MLSTM_L = 256


def _mlstm_kernel(cqk_ref, cv_ref, co_ref, cif_ref, conv0_ref, cw_ref, cb_ref, gb_ref, hg_ref,
                  c0_ref, n0_ref, m0_ref, hs_ref, c_ref, n_ref, m_ref, xext, *, L):
    @pl.when(pl.program_id(1) == 0)
    def _():
        xext[0:CONV_PAD, :] = conv0_ref[0]
        c_ref[...] = c0_ref[...]
        n_ref[...] = n0_ref[...]
        m_ref[...] = m0_ref[...]

    xext[CONV_PAD:CONV_PAD + L, :] = cqk_ref[0]
    base = CONV_PAD - (CONV_W - 1)
    u = 0.0
    for j in range(CONV_W):
        u = u + xext[base + j:base + j + L, :] * cw_ref[j:j + 1, :]
    u = cb_ref[...] + u
    u = u * jax.nn.sigmoid(u)
    xext[0:CONV_PAD, :] = xext[L:L + CONV_PAD, :]

    z = cif_ref[0] + gb_ref[...]
    lane = lax.broadcasted_iota(jnp.int32, (L, LANES), 1)
    lf = jnp.minimum(z, 0.0) - jnp.log1p(jnp.exp(-jnp.abs(z)))
    tr = lax.broadcasted_iota(jnp.int32, (L, L), 0)
    tc = lax.broadcasted_iota(jnp.int32, (L, L), 1)
    causal = tr >= tc
    b_all = jnp.dot(causal.astype(F32), lf, preferred_element_type=F32,
                    precision=lax.Precision.HIGHEST)
    sr = lax.broadcasted_iota(jnp.int32, (8, LANES), 0)
    sc = lax.broadcasted_iota(jnp.int32, (8, LANES), 1)
    sel = jnp.where(sr < H_C, jnp.where(sc == sr, 1.0, jnp.where(sc == sr + H_C, -1.0, 0.0)), 0.0)
    rt = lax.dot_general(sel, jnp.where(lane < H_C, z, b_all), NT_DIMS,
                         preferred_element_type=F32, precision=lax.Precision.HIGHEST)
    er = lax.broadcasted_iota(jnp.int32, (DH_C, DH_C), 0)
    ec = lax.broadcasted_iota(jnp.int32, (DH_C, DH_C), 1)
    eye = jnp.where(er == ec, 1.0, 0.0).astype(BF16)

    for h in range(H_C):
        cols = slice(h * DH_C, (h + 1) * DH_C)
        q = u[:, h * DH_C:(h + 1) * DH_C]
        k = u[:, D_C + h * DH_C:D_C + (h + 1) * DH_C] * (DH_C ** -0.5)
        v = cv_ref[0, :, cols]
        qb, kb_, vb_ = q.astype(BF16), k.astype(BF16), v.astype(BF16)
        b_col = b_all[:, H_C + h:H_C + h + 1]
        li_col = z[:, h:h + 1]
        m_prev = m_ref[0, h:h + 1, 0:1]
        cs = c_ref[0, h]
        ns = n_ref[0, h:h + 1, :]

        dmat = jnp.where(causal, b_col + rt[h:h + 1, :], NEG)
        inter = b_col + m_prev
        m_t = jnp.maximum(inter, jnp.max(dmat, axis=-1, keepdims=True))
        w_intra = jnp.exp(dmat - m_t)
        w_inter = jnp.exp(inter - m_t)
        a = w_intra * _nt(qb, kb_)
        num = (jnp.dot(a.astype(BF16), vb_, preferred_element_type=F32)
               + w_inter * _nt(qb, cs.astype(BF16)))
        den = jnp.sum(a, axis=-1, keepdims=True) + w_inter * jnp.sum(q * ns, axis=-1, keepdims=True)
        hh = num / jnp.maximum(jnp.abs(den), jnp.exp(-m_t))

        b_last = b_col[L - 1:L, :]
        g = b_last - b_col + li_col
        m_new = jnp.maximum(b_last + m_prev, jnp.max(g, axis=0, keepdims=True))
        ws = jnp.exp(g - m_new)
        decay = jnp.exp(b_last + m_prev - m_new)
        vwt = _nt(eye, (ws * v).astype(BF16)).astype(BF16)
        c_ref[0, h] = decay * cs + jnp.dot(vwt, kb_, preferred_element_type=F32)
        n_ref[0, h:h + 1, :] = decay * ns + jnp.sum(ws * k, axis=0, keepdims=True)
        m_ref[0, h:h + 1, :] = jnp.broadcast_to(m_new, (1, LANES))

        gate = jax.nn.sigmoid(co_ref[0, :, cols])
        hs_ref[0, :, cols] = (_rms(hh, hg_ref[...]) * gate).astype(hs_ref.dtype)


def _mlstm(cqk3, cvo3, cif3, conv0, conv_w, conv_b, gate_b, head_g, c0, n0, m0, L):
    b, t, _ = cqk3.shape
    nc = t // L
    blk = lambda col, w: pl.BlockSpec((1, L, w), lambda bi, c: (bi, c, col))
    const = lambda shape: pl.BlockSpec(shape, lambda bi, c: (0,) * len(shape))
    per_b = lambda shape: pl.BlockSpec((1,) + shape, lambda bi, c: (bi,) + (0,) * len(shape))
    return pl.pallas_call(
        functools.partial(_mlstm_kernel, L=L),
        out_shape=(jax.ShapeDtypeStruct((b, t, D_C), BF16),
                   jax.ShapeDtypeStruct((b, H_C, DH_C, DH_C), F32),
                   jax.ShapeDtypeStruct((b, H_C, DH_C), F32),
                   jax.ShapeDtypeStruct((b, H_C, LANES), F32)),
        grid=(b, nc),
        in_specs=[
            blk(0, 2 * D_C), blk(0, D_C), blk(1, D_C), blk(0, LANES),
            per_b((CONV_PAD, 2 * D_C)), const((CONV_W, 2 * D_C)), const((1, 2 * D_C)),
            const((1, LANES)), const((1, DH_C)),
            per_b((H_C, DH_C, DH_C)), per_b((H_C, DH_C)), per_b((H_C, LANES)),
        ],
        out_specs=(pl.BlockSpec((1, L, D_C), lambda bi, c: (bi, c, 0)),
                   per_b((H_C, DH_C, DH_C)), per_b((H_C, DH_C)), per_b((H_C, LANES))),
        scratch_shapes=[pltpu.VMEM((CONV_PAD + L, 2 * D_C), F32)],
        compiler_params=_cparams(("parallel", "arbitrary")),
        name="mlstm",
    )(cqk3, cvo3, cvo3, cif3, conv0, conv_w, conv_b, gate_b, head_g, c0, n0, m0)


def _mix_kernel(oa_ref, ob_ref, oc_ref, ga_ref, gb_ref, gc_ref, x_ref, wa_ref, wb_ref, wc_ref, wo_ref, o_ref):
    mixed = (ga_ref[...] * jnp.dot(oa_ref[...], wa_ref[...], preferred_element_type=F32)
             + gb_ref[...] * jnp.dot(ob_ref[...], wb_ref[...], preferred_element_type=F32)
             + gc_ref[...] * jnp.dot(oc_ref[...], wc_ref[...], preferred_element_type=F32))
    o_ref[...] = x_ref[...] + jnp.dot(mixed.astype(BF16), wo_ref[...], preferred_element_type=F32)


def _mix(oa, ob, oc, gates, x, l, wa, wb, wc, wo):
    m, d = x.shape
    tm = min(m, 512)
    row = lambda w, col=0: pl.BlockSpec((tm, w), lambda i: (i, col))
    full = lambda a: pl.BlockSpec((None,) + a.shape[1:], lambda i: (l, 0, 0))
    g0 = 0
    return pl.pallas_call(
        _mix_kernel,
        out_shape=jax.ShapeDtypeStruct((m, d), F32),
        grid=(m // tm,),
        in_specs=[row(oa.shape[1]), row(ob.shape[1]), row(oc.shape[1]),
                  row(d, g0), row(d, g0 + 1), row(d, g0 + 2), row(d),
                  full(wa), full(wb), full(wc), full(wo)],
        out_specs=row(d),
        compiler_params=_cparams(("parallel",)),
        name="mix",
    )(oa, ob, oc, gates, gates, gates, x, wa, wb, wc, wo)


def _cross_kernel(x_ref, g_ref, wq_ref, mk_ref, mv_ref, wo_ref, o_ref):
    x = x_ref[0]
    h = _rms(x, g_ref[...]).astype(BF16)
    q = (jnp.dot(h, wq_ref[...], preferred_element_type=F32) * (DH_M ** -0.5)).astype(BF16)
    outs = []
    for hd in range(H_M):
        cols = slice(hd * DH_M, (hd + 1) * DH_M)
        s = _nt(q[:, cols], mk_ref[0, :, cols].astype(BF16))
        m = jnp.max(s, axis=-1, keepdims=True)
        e = jnp.exp(s - m)
        l = jnp.sum(e, axis=-1, keepdims=True)
        o = jnp.dot(e.astype(BF16), mv_ref[0, :, cols].astype(BF16), preferred_element_type=F32) / l
        outs.append(o.astype(BF16))
    o = jnp.concatenate(outs, axis=1)
    o_ref[0] = x + jnp.dot(o, wo_ref[...], preferred_element_type=F32)


def _cross(x3, g, l, wq, mk, mv, wo):
    b, t, d = x3.shape
    tm = min(t, 512)
    nm = mk.shape[1]
    full = lambda a: pl.BlockSpec((None,) + a.shape[1:], lambda bi, i: (l, 0, 0))
    return pl.pallas_call(
        _cross_kernel,
        out_shape=jax.ShapeDtypeStruct((b, t, d), F32),
        grid=(b, t // tm),
        in_specs=[pl.BlockSpec((1, tm, d), lambda bi, i: (bi, i, 0)),
                  pl.BlockSpec(g.shape, lambda bi, i: (0, 0)), full(wq),
                  pl.BlockSpec((1, nm, d), lambda bi, i: (bi, 0, 0)),
                  pl.BlockSpec((1, nm, d), lambda bi, i: (bi, 0, 0)), full(wo)],
        out_specs=pl.BlockSpec((1, tm, d), lambda bi, i: (bi, i, 0)),
        compiler_params=_cparams(("parallel", "parallel")),
        name="cross",
    )(x3, g, wq, mk, mv, wo)


def _ffn_kernel(x_ref, g_ref, wg_ref, wu_ref, wd_ref, gf_ref, o_ref, h_scr, acc, *, final_norm):
    j = pl.program_id(1)

    @pl.when(j == 0)
    def _():
        h_scr[...] = _rms(x_ref[...], g_ref[...]).astype(BF16)
        acc[...] = jnp.zeros(acc.shape, F32)

    h = h_scr[...]
    gate = jnp.dot(h, wg_ref[...], preferred_element_type=F32)
    up = jnp.dot(h, wu_ref[...], preferred_element_type=F32)
    a = (gate * jax.nn.sigmoid(gate) * up).astype(BF16)
    acc[...] += jnp.dot(a, wd_ref[...], preferred_element_type=F32)

    @pl.when(j == pl.num_programs(1) - 1)
    def _():
        y = x_ref[...] + acc[...]
        if final_norm:
            y = _rms(y, gf_ref[...])
        o_ref[...] = y


FFN_TF = 1408
FFN_TM = 512


def _ffn(x, g, l, wg, wu, wd, g_final, final_norm):
    m, d = x.shape
    tm = min(m, FFN_TM)
    return pl.pallas_call(
        functools.partial(_ffn_kernel, final_norm=final_norm),
        out_shape=jax.ShapeDtypeStruct((m, d), F32),
        grid=(m // tm, D_FF // FFN_TF),
        in_specs=[
            pl.BlockSpec((tm, d), lambda i, j: (i, 0)),
            pl.BlockSpec((1, d), lambda i, j: (0, 0)),
            pl.BlockSpec((None, d, FFN_TF), lambda i, j: (l, 0, j)),
            pl.BlockSpec((None, d, FFN_TF), lambda i, j: (l, 0, j)),
            pl.BlockSpec((None, FFN_TF, d), lambda i, j: (l, j, 0)),
            pl.BlockSpec((1, d), lambda i, j: (0, 0)),
        ],
        out_specs=pl.BlockSpec((tm, d), lambda i, j: (i, 0)),
        scratch_shapes=[pltpu.VMEM((tm, d), BF16), pltpu.VMEM((tm, d), F32)],
        compiler_params=_cparams(("parallel", "arbitrary")),
        name="ffn",
    )(x, g, wg, wu, wd, g_final)


def _layer(x3, l, w, mem_k, mem_v, cache, final_norm, akv_all):
    b, t, d = x3.shape
    x2 = x3.reshape(b * t, d)
    row = lambda a: a.reshape(1, -1)
    lam_init = 0.8 - 0.6 * math.exp(-0.3 * l)

    gates, cqk, cif = _in_proj_a(x2, row(w['g_mix']), l, w['w_in'], w['w_if'])
    q, ak_all, av_all, akv, bkv, cvo = _in_proj_b(x2, row(w['g_mix']), w['w_in'], l, *akv_all, (b, t))
    q3 = q.reshape(b, t, SEG)
    bkv3 = bkv.reshape(b, t, SEG)
    cqk3 = cqk.reshape(b, t, 2 * D_C)
    keep = t if cache is not None else min(BAND, t)
    b_k = bkv3[:, t - keep:, :HALF].reshape(b, keep, H_B, DH_B)
    b_v = bkv3[:, t - keep:, HALF:].reshape(b, keep, H_B, DH_B)

    gate_b = jnp.concatenate([w['b_i'], w['b_f'], jnp.zeros((LANES - 2 * H_C,), F32)]).reshape(1, LANES)
    if cache is None:
        oa = _diff_prompt(q3, akv.reshape(b, t, SEG), w['lqk'], row(w['a_head_g']), lam_init)
        ob = _band_prompt(q3, bkv3, w['relbias'])
        conv0 = jnp.zeros((b, CONV_PAD, 2 * D_C), F32)
        c0 = jnp.zeros((b, H_C, DH_C, DH_C), F32)
        n0 = jnp.zeros((b, H_C, DH_C), F32)
        m0 = jnp.zeros((b, H_C, LANES), F32)
        L = min(t, MLSTM_L)
    else:
        past = cache['a_k'].shape[1]
        ak3 = ak_all.reshape(DEPTH * b, t, HALF)
        av3 = av_all.reshape(DEPTH * b, t, HALF)
        oa = _diff_sample(q3, ak3, av3, l, cache['a_k'].reshape(b, past, -1), cache['a_v'].reshape(b, past, -1),
                          w['lqk'], row(w['a_head_g']), lam_init)
        nband = cache['b_k'].shape[1]
        ob = _band_sample(q3, bkv3, cache['b_k'].reshape(b, nband, -1), cache['b_v'].reshape(b, nband, -1),
                          w['relbias'], past)
        conv0 = jnp.pad(cache['conv'], ((0, 0), (CONV_PAD - (CONV_W - 1), 0), (0, 0)))
        c0, n0 = cache['C'], cache['n']
        m0 = jnp.broadcast_to(cache['m'][:, :, None], (b, H_C, LANES))
        L = t
    oc, c_new, n_new, m_new = _mlstm(cqk3, cvo.reshape(b, t, SEG), cif.reshape(b, t, LANES), conv0,
                                     w['conv_w'], row(w['conv_b']), gate_b, row(w['c_head_g']), c0, n0, m0, L)
    assert t >= CONV_W - 1
    conv_new = cqk3[:, t - (CONV_W - 1):]

    x2 = _mix(oa.reshape(b * t, -1), ob.reshape(b * t, -1), oc.reshape(b * t, -1), gates, x2, l,
              w['w_up_a'], w['w_up_b'], w['w_up_c'], w['w_o'])
    x3 = _cross(x2.reshape(b, t, d), row(w['g_cross']), l, w['w_mq'], mem_k, mem_v, w['w_mo'])
    x2 = _ffn(x3.reshape(b * t, d), row(w['g_ffn']), l, w['w_ff_g'], w['w_ff_u'], w['w_ff_d'],
              row(w['g_final']), final_norm)
    return x2.reshape(b, t, d), (ak_all, av_all), (b_k, b_v, c_new, n_new, m_new[:, :, 0], conv_new)


def kernel(x_prompt, x_sample, cache_a_k, cache_a_v, cache_b_k, cache_b_v, state_c_C, state_c_n, state_c_m, state_c_conv, cache_mem_k, cache_mem_v, mem_prompt, g_mix, w_in, a_lq1, a_lk1, a_lq2, a_lk2, a_head_g, b_rel, c_conv_w, c_conv_b, c_b_i, c_b_f, c_head_g, w_up_a, w_up_b, w_up_c, w_o, g_cross, w_mq, w_mk, w_mv, w_mo, g_ffn, w_ff_g, w_ff_u, w_ff_d, g_final):
    xp, xs = x_prompt, x_sample
    bp = x_prompt.shape[0]
    n_mem = mem_prompt.shape[1]
    mem2 = mem_prompt.reshape(bp * n_mem, D_MODEL)
    new_p = [[] for _ in range(8)]
    new_s = [[] for _ in range(6)]
    akv_p = akv_s = (None, None)
    bf = lambda a: a.astype(BF16)
    wp, wif = _prep_w_in(w_in)
    wts = dict(w_up_a=bf(w_up_a), w_up_b=bf(w_up_b), w_up_c=bf(w_up_c), w_o=bf(w_o), w_mq=bf(w_mq),
               w_mo=bf(w_mo), w_ff_g=bf(w_ff_g), w_ff_u=bf(w_ff_u), w_ff_d=bf(w_ff_d))
    w_mk_b, w_mv_b = bf(w_mk), bf(w_mv)
    for l in range(DEPTH):
        w = dict(g_mix=g_mix[l], w_in=wp, w_if=wif,
                 lqk=jnp.stack([a_lq1[l], a_lk1[l], a_lq2[l], a_lk2[l]]),
                 a_head_g=a_head_g[l], relbias=_relbias(b_rel[l]), conv_w=c_conv_w[l], conv_b=c_conv_b[l],
                 b_i=c_b_i[l], b_f=c_b_f[l], c_head_g=c_head_g[l],
                 g_cross=g_cross[l], g_ffn=g_ffn[l], g_final=g_final, **wts)
        mk2 = _mem_proj(mem2, l, w_mk_b).reshape(bp, n_mem, D_MODEL)
        mv2 = _mem_proj(mem2, l, w_mv_b).reshape(bp, n_mem, D_MODEL)
        last = l == DEPTH - 1
        xp, akv_p, st_p = _layer(xp, l, w, mk2, mv2, None, last, akv_p)
        cache = dict(a_k=cache_a_k[l], a_v=cache_a_v[l], b_k=cache_b_k[l], b_v=cache_b_v[l],
                     C=state_c_C[l], n=state_c_n[l], m=state_c_m[l], conv=state_c_conv[l])
        bs = xs.shape[0]
        xs, akv_s, st_s = _layer(xs, l, w, cache_mem_k[l].reshape(bs, n_mem, D_MODEL),
                                 cache_mem_v[l].reshape(bs, n_mem, D_MODEL), cache, last, akv_s)
        mk4 = mk2.reshape(bp, n_mem, H_M, DH_M)
        mv4 = mv2.reshape(bp, n_mem, H_M, DH_M)
        for lst, a in zip(new_p, st_p + (mk4, mv4)):
            lst.append(a)
        for lst, a in zip(new_s, st_s):
            lst.append(a)
    heads = lambda a, b, t: a.reshape(DEPTH, b, t, H_A, DV_A)
    outs_p = [heads(a, bp, x_prompt.shape[1]) for a in akv_p] + [jnp.stack(a, 0) for a in new_p]
    outs_s = [heads(a, xs.shape[0], xs.shape[1]) for a in akv_s] + [jnp.stack(a, 0) for a in new_s]
    return (xp, xs) + tuple(outs_p) + tuple(outs_s)
```

```python
import functools
import math

import jax
import jax.numpy as jnp
from jax import lax
from jax.experimental import pallas as pl
from jax.experimental.pallas import tpu as pltpu

F32 = jnp.float32
BF16 = jnp.bfloat16

D_MODEL = 1024
DEPTH = 2
CHUNK = 64
EPS = 1e-6
NEG = -1e30
LOG2E = math.log2(math.e)
H_A = 4
DH_A = 64
DV_A = 2 * DH_A
H_B = 8
DH_B = 64
BAND_CHUNKS = 8
BAND = BAND_CHUNKS * CHUNK
REL_CLIP = 128
H_C = 4
DH_C = 128
D_C = H_C * DH_C
CONV_W = 4
H_M = 4
DH_M = D_MODEL // H_M
D_FF = -(-8 * D_MODEL // (3 * 256)) * 256

LANES = 128
VMEM_LIMIT = 48 * 1024 * 1024

SEG = 1024
HALF = SEG // 2
GATE_TILES = 3
N_TILES_A = GATE_TILES + 1
N_TILES_B = 4
IN_SIZES = (H_A * DH_A,) * 4 + (H_A * DV_A,) + (H_B * DH_B,) * 3 + (2 * D_C, D_C, D_C, 2 * H_C, 3 * D_MODEL)

ALIBI_SLOPES = tuple(2.0 ** (-8.0 * (i + 1) / H_A) for i in range(H_A))

NT_DIMS = (((1,), (1,)), ((), ()))


def _cparams(sem):
    return pltpu.CompilerParams(dimension_semantics=sem, vmem_limit_bytes=VMEM_LIMIT)


def _rms(x, g):
    ms = jnp.mean(x * x, axis=-1, keepdims=True)
    return x * lax.rsqrt(ms + EPS) * g


def _nt(a, b):
    return lax.dot_general(a, b, NT_DIMS, preferred_element_type=F32)


def _prep_w_in_kernel(w_ref, o_ref, oif_ref):
    offs = [0]
    for size in IN_SIZES:
        offs.append(offs[-1] + size)
    seg = lambda i: w_ref[:, offs[i]:offs[i + 1]]
    a_q1, a_q2, a_k1, a_k2, a_v, b_q, b_k, b_v, c_qk, c_v, c_o = (seg(i) for i in range(11))
    g = seg(12)

    def put(col, val):
        o_ref[:, col:col + val.shape[1]] = val.astype(BF16)

    put(0, g)
    put(GATE_TILES * SEG, c_qk)
    for h in range(H_A):
        hs = slice(h * DH_A, (h + 1) * DH_A)
        base = h * 2 * DH_A
        put(4 * SEG + base, a_q1[:, hs] * (DH_A ** -0.5))
        put(4 * SEG + base + DH_A, a_q2[:, hs] * (DH_A ** -0.5))
        put(5 * SEG + base, a_k1[:, hs])
        put(5 * SEG + base + DH_A, a_k2[:, hs])
    put(4 * SEG + HALF, b_q * (DH_B ** -0.5))
    put(5 * SEG + HALF, a_v)
    put(6 * SEG, b_k)
    put(6 * SEG + HALF, b_v)
    put(7 * SEG, c_v)
    put(7 * SEG + HALF, c_o)
    tile = w_ref[:, offs[11]:offs[11] + LANES]
    lane = lax.broadcasted_iota(jnp.int32, tile.shape, 1)
    oif_ref[...] = jnp.where(lane < 2 * H_C, tile, 0.0).astype(BF16)


def _prep_w_in(w):
    depth, k, n = w.shape
    tk = 256
    n_out = (N_TILES_A + N_TILES_B) * SEG
    return pl.pallas_call(
        _prep_w_in_kernel,
        out_shape=(jax.ShapeDtypeStruct((depth, k, n_out), BF16),
                   jax.ShapeDtypeStruct((depth, k, LANES), BF16)),
        grid=(depth, k // tk),
        in_specs=[pl.BlockSpec((None, tk, n), lambda l, i: (l, i, 0))],
        out_specs=(pl.BlockSpec((None, tk, n_out), lambda l, i: (l, i, 0)),
                   pl.BlockSpec((None, tk, LANES), lambda l, i: (l, i, 0))),
        compiler_params=_cparams(("parallel", "parallel")),
        name="prep_w_in",
    )(w)


def _norm_to_scratch(x_ref, g_ref, h_scr):
    h_scr[...] = _rms(x_ref[...], g_ref[...]).astype(BF16)


def _in_proj_a_kernel(x_ref, g_ref, w_ref, wif_ref, og_ref, ocqk_ref, ocif_ref, h_scr):
    j = pl.program_id(1)

    @pl.when(j == 0)
    def _():
        _norm_to_scratch(x_ref, g_ref, h_scr)
        ocif_ref[...] = jnp.dot(h_scr[...], wif_ref[...], preferred_element_type=F32)

    tile = lambda: jnp.dot(h_scr[...], w_ref[...], preferred_element_type=F32)

    @pl.when(j < GATE_TILES)
    def _():
        og_ref[...] = jax.nn.sigmoid(tile()).astype(og_ref.dtype)

    @pl.when(j == GATE_TILES)
    def _():
        ocqk_ref[...] = tile()


def _in_proj_a(x, g, l, wp, wif):
    m, k = x.shape
    tm = min(m, 1024)
    return pl.pallas_call(
        _in_proj_a_kernel,
        out_shape=(jax.ShapeDtypeStruct((m, GATE_TILES * SEG), BF16),
                   jax.ShapeDtypeStruct((m, SEG), F32),
                   jax.ShapeDtypeStruct((m, LANES), F32)),
        grid=(m // tm, N_TILES_A),
        in_specs=[
            pl.BlockSpec((tm, k), lambda i, j: (i, 0)),
            pl.BlockSpec((1, k), lambda i, j: (0, 0)),
            pl.BlockSpec((None, k, SEG), lambda i, j: (l, 0, j)),
            pl.BlockSpec((None, k, LANES), lambda i, j: (l, 0, 0)),
        ],
        out_specs=(pl.BlockSpec((tm, SEG), lambda i, j: (i, jnp.minimum(j, GATE_TILES - 1))),
                   pl.BlockSpec((tm, SEG), lambda i, j: (i, 0)),
                   pl.BlockSpec((tm, LANES), lambda i, j: (i, 0))),
        scratch_shapes=[pltpu.VMEM((tm, k), BF16)],
        compiler_params=_cparams(("parallel", "arbitrary")),
        name="in_proj_a",
    )(x, g, wp, wif)


def _in_proj_b_kernel(*refs, aliased, heads5d):
    x_ref, g_ref, w_ref = refs[:3]
    oq_ref, oak_ref, oav_ref, oakv_ref, obkv_ref, ocvo_ref, h_scr = refs[5 if aliased else 3:]
    j = pl.program_id(1)

    @pl.when(j == 0)
    def _():
        _norm_to_scratch(x_ref, g_ref, h_scr)

    tile = lambda: jnp.dot(h_scr[...], w_ref[...], preferred_element_type=F32)

    @pl.when(j == 0)
    def _():
        oq_ref[...] = (tile() * LOG2E).astype(oq_ref.dtype)

    @pl.when(j == 1)
    def _():
        for out_ref, off in ((oak_ref, 0), (oav_ref, HALF)):
            a = jnp.dot(h_scr[...], w_ref[:, off:off + HALF], preferred_element_type=F32)
            oakv_ref[:, off:off + HALF] = a.astype(oakv_ref.dtype)
            if heads5d:
                for hd in range(H_A):
                    out_ref[0, 0, :, hd, :] = a[:, hd * DV_A:(hd + 1) * DV_A]
            else:
                out_ref[0] = a

    @pl.when(j == 2)
    def _():
        obkv_ref[...] = tile()

    @pl.when(j == 3)
    def _():
        ocvo_ref[...] = tile()


def _in_proj_b(x, g, wp, l, ak_all, av_all, bt):
    m, k = x.shape
    tm = min(m, 1024)
    aliased = ak_all is not None
    b, t = bt
    heads5d = t % tm == 0
    if heads5d:
        stacked = jax.ShapeDtypeStruct((DEPTH, b, t, H_A, DV_A), F32)
        layer_row = pl.BlockSpec((1, 1, tm, H_A, DV_A),
                                 lambda i, j: (l, i // (t // tm), i % (t // tm), 0, 0))
    else:
        stacked = jax.ShapeDtypeStruct((DEPTH, m, HALF), F32)
        layer_row = pl.BlockSpec((1, tm, HALF), lambda i, j: (l, i, 0))
    in_specs = [
        pl.BlockSpec((tm, k), lambda i, j: (i, 0)),
        pl.BlockSpec((1, k), lambda i, j: (0, 0)),
        pl.BlockSpec((None, k, SEG), lambda i, j: (l, 0, N_TILES_A + j)),
    ]
    args = [x, g, wp]
    if aliased:
        in_specs += [pl.BlockSpec(memory_space=pl.ANY)] * 2
        args += [ak_all, av_all]
    row = lambda w: pl.BlockSpec((tm, w), lambda i, j: (i, 0))
    return pl.pallas_call(
        functools.partial(_in_proj_b_kernel, aliased=aliased, heads5d=heads5d),
        out_shape=(jax.ShapeDtypeStruct((m, SEG), BF16), stacked, stacked, jax.ShapeDtypeStruct((m, SEG), BF16),
                   jax.ShapeDtypeStruct((m, SEG), F32), jax.ShapeDtypeStruct((m, SEG), F32)),
        grid=(m // tm, N_TILES_B),
        in_specs=in_specs,
        out_specs=(row(SEG), layer_row, layer_row, row(SEG), row(SEG), row(SEG)),
        scratch_shapes=[pltpu.VMEM((tm, k), BF16)],
        input_output_aliases={3: 1, 4: 2} if aliased else {},
        compiler_params=_cparams(("parallel", "arbitrary")),
        name="in_proj_b",
    )(*args)


def _mem_proj_kernel(x_ref, w_ref, o_ref):
    o_ref[...] = jnp.dot(x_ref[...].astype(BF16), w_ref[...], preferred_element_type=F32)


def _mem_proj(x, l, w):
    m, k = x.shape
    n = w.shape[2]
    tm = min(m, 1024)
    return pl.pallas_call(
        _mem_proj_kernel,
        out_shape=jax.ShapeDtypeStruct((m, n), F32),
        grid=(m // tm,),
        in_specs=[pl.BlockSpec((tm, k), lambda i: (i, 0)), pl.BlockSpec((None, k, n), lambda i: (l, 0, 0))],
        out_specs=pl.BlockSpec((tm, n), lambda i: (i, 0)),
        compiler_params=_cparams(("parallel",)),
        name="mem_proj",
    )(x, w)


def _online_updates(scores, v, states):
    staged = []
    for s, (acc, m, l) in zip(scores, states):
        slabs = [s[:, j * LANES:(j + 1) * LANES] for j in range(s.shape[1] // LANES)]
        m_old = m[...]
        m_new = jnp.maximum(m_old, jnp.max(functools.reduce(jnp.maximum, slabs), axis=-1, keepdims=True))
        alpha = jnp.exp2(m_old - m_new)
        ps = [jnp.exp2(c - m_new) for c in slabs]
        l[...] = alpha * l[...] + jnp.sum(functools.reduce(jnp.add, ps), axis=-1, keepdims=True)
        m[...] = m_new
        staged.append((alpha, jnp.concatenate(ps, axis=1).astype(BF16)))
    for (alpha, p), (acc, m, l) in zip(staged, states):
        acc[...] = alpha * acc[...] + jnp.dot(p, v, preferred_element_type=F32)


def _split_halves(q):
    lane = lax.broadcasted_iota(jnp.int32, q.shape, 1)
    lo = jnp.where(lane < DH_A, q, 0.0).astype(BF16)
    hi = jnp.where(lane >= DH_A, q, 0.0).astype(BF16)
    return lo, hi


def _lambda(lqk, lam_init):
    e1 = jnp.exp(jnp.sum(lqk[0:1, :] * lqk[1:2, :], axis=-1, keepdims=True))
    e2 = jnp.exp(jnp.sum(lqk[2:3, :] * lqk[3:4, :], axis=-1, keepdims=True))
    return e1 - e2 + lam_init


def _diff_prompt_kernel(slopes_ref, lqk_ref, hg_ref, q_ref, k_ref, v_ref, o_ref,
                        acc1, acc2, m1, l1, m2, l2, *, tq, lam_init):
    h = pl.program_id(1)
    qi = pl.program_id(2)
    slope = slopes_ref[h] * LOG2E
    kb, vb = k_ref.at[0], v_ref.at[0]

    qa, qb = _split_halves(q_ref[0])
    states = ((acc1, m1, l1), (acc2, m2, l2))
    for acc, m, l in states:
        m[...] = jnp.full(m.shape, NEG, F32)
        l[...] = jnp.zeros(l.shape, F32)
        acc[...] = jnp.zeros(acc.shape, F32)

    def full_tile(ki, carry):
        k0 = pl.multiple_of(ki * tq, tq)
        k = kb[pl.ds(k0, tq), :]
        v = vb[pl.ds(k0, tq), :]
        kpos = k0 + lax.broadcasted_iota(jnp.int32, (1, tq), 1)
        bias = slope * kpos.astype(F32)
        _online_updates([_nt(qa, k) + bias, _nt(qb, k) + bias], v, states)
        return carry

    lax.fori_loop(0, qi, full_tile, 0)

    q0 = pl.multiple_of(qi * tq, tq)
    k = kb[pl.ds(q0, tq), :]
    v = vb[pl.ds(q0, tq), :]
    r = lax.broadcasted_iota(jnp.int32, (tq, tq), 0)
    c = lax.broadcasted_iota(jnp.int32, (tq, tq), 1)
    allowed = (c // CHUNK) <= (r // CHUNK)
    bias = slope * (q0 + r - jnp.abs(r - c)).astype(F32)
    _online_updates([jnp.where(allowed, _nt(qa, k) + bias, NEG),
                     jnp.where(allowed, _nt(qb, k) + bias, NEG)], v, states)

    lam = _lambda(lqk_ref[...], lam_init)
    o = acc1[...] / l1[...] - lam * (acc2[...] / l2[...])
    o_ref[0] = (_rms(o, hg_ref[...]) * (1.0 - lam_init)).astype(o_ref.dtype)


def _diff_prompt(q3, kv3, lqk, head_g, lam_init):
    b, t, _ = q3.shape
    tq = min(t, 512)
    slopes = jnp.asarray(ALIBI_SLOPES, F32)
    return pl.pallas_call(
        functools.partial(_diff_prompt_kernel, tq=tq, lam_init=lam_init),
        out_shape=jax.ShapeDtypeStruct((b, t, H_A * DV_A), BF16),
        grid=(b, H_A, t // tq),
        in_specs=[
            pl.BlockSpec(memory_space=pltpu.SMEM),
            pl.BlockSpec((4, DH_A), lambda bi, h, qi: (0, 0)),
            pl.BlockSpec((1, DV_A), lambda bi, h, qi: (0, 0)),
            pl.BlockSpec((1, tq, LANES), lambda bi, h, qi: (bi, qi, h)),
            pl.BlockSpec((1, t, LANES), lambda bi, h, qi: (bi, 0, h)),
            pl.BlockSpec((1, t, LANES), lambda bi, h, qi: (bi, 0, H_A + h)),
        ],
        out_specs=pl.BlockSpec((1, tq, LANES), lambda bi, h, qi: (bi, qi, h)),
        scratch_shapes=[
            pltpu.VMEM((tq, LANES), F32), pltpu.VMEM((tq, LANES), F32),
            pltpu.VMEM((tq, LANES), F32), pltpu.VMEM((tq, LANES), F32),
            pltpu.VMEM((tq, LANES), F32), pltpu.VMEM((tq, LANES), F32),
        ],
        compiler_params=_cparams(("parallel", "parallel", "arbitrary")),
        name="diff_prompt",
    )(slopes, lqk, head_g, q3, kv3, kv3)


def _pieces_attention(qm, pieces):
    ss = [_nt(qm, k) + bias for k, _, bias in pieces]
    m = functools.reduce(jnp.maximum, [jnp.max(s, axis=-1, keepdims=True) for s in ss])
    es = [jnp.exp2(s - m) for s in ss]
    l = functools.reduce(jnp.add, [jnp.sum(e, axis=-1, keepdims=True) for e in es])
    o = functools.reduce(jnp.add, [jnp.dot(e.astype(BF16), v, preferred_element_type=F32)
                                   for e, (_, v, _) in zip(es, pieces)])
    return o / l


def _diff_sample_kernel(lqk_ref, hg_ref, q_ref, kc_ref, vc_ref, kn_ref, vn_ref, o_ref, *, lam_init):
    t = q_ref.shape[1]
    past = kc_ref.shape[1]
    lam = _lambda(lqk_ref[...], lam_init)
    rc = lax.broadcasted_iota(jnp.int32, (t, past), 0)
    cc = lax.broadcasted_iota(jnp.int32, (t, past), 1)
    rn = lax.broadcasted_iota(jnp.int32, (t, t), 0)
    cn = lax.broadcasted_iota(jnp.int32, (t, t), 1)
    dist_c = jnp.abs(past + rc - cc).astype(F32)
    dist_n = jnp.abs(rn - cn).astype(F32)
    ok_c = (cc // CHUNK) <= ((past + rc) // CHUNK)
    ok_n = ((past + cn) // CHUNK) <= ((past + rn) // CHUNK)
    for h in range(H_A):
        cols = slice(h * LANES, (h + 1) * LANES)
        bias_c = jnp.where(ok_c, (-ALIBI_SLOPES[h] * LOG2E) * dist_c, NEG)
        bias_n = jnp.where(ok_n, (-ALIBI_SLOPES[h] * LOG2E) * dist_n, NEG)
        pieces = [(kc_ref[0, :, cols].astype(BF16), vc_ref[0, :, cols].astype(BF16), bias_c),
                  (kn_ref[0, :, cols].astype(BF16), vn_ref[0, :, cols].astype(BF16), bias_n)]
        qa, qb = _split_halves(q_ref[0, :, cols])
        o = _pieces_attention(qa, pieces) - lam * _pieces_attention(qb, pieces)
        o_ref[0, :, cols] = (_rms(o, hg_ref[...]) * (1.0 - lam_init)).astype(o_ref.dtype)


def _diff_sample(q3, kn3, vn3, l, kc, vc, lqk, head_g, lam_init):
    b, t, _ = q3.shape
    past = kc.shape[1]
    w = H_A * DV_A
    new = pl.BlockSpec((1, t, w), lambda bi: (l * b + bi, 0, 0))
    cache = pl.BlockSpec((1, past, w), lambda bi: (bi, 0, 0))
    return pl.pallas_call(
        functools.partial(_diff_sample_kernel, lam_init=lam_init),
        out_shape=jax.ShapeDtypeStruct((b, t, w), BF16),
        grid=(b,),
        in_specs=[
            pl.BlockSpec((4, DH_A), lambda bi: (0, 0)),
            pl.BlockSpec((1, DV_A), lambda bi: (0, 0)),
            pl.BlockSpec((1, t, w), lambda bi: (bi, 0, 0)), cache, cache, new, new,
        ],
        out_specs=pl.BlockSpec((1, t, w), lambda bi: (bi, 0, 0)),
        compiler_params=_cparams(("parallel",)),
        name="diff_sample",
    )(lqk, head_g, q3, kc, vc, kn3, vn3)


QPAIR = 2 * CHUNK
WIN = BAND + QPAIR


def _relbias_kernel(tab_ref, o_ref):
    h = pl.program_id(0)
    r = lax.broadcasted_iota(jnp.int32, (QPAIR, WIN), 0)
    j = lax.broadcasted_iota(jnp.int32, (QPAIR, WIN), 1)
    rel = jnp.clip(r - j + BAND, -REL_CLIP, REL_CLIP) + REL_CLIP
    qc = r // CHUNK
    kc = j // CHUNK - BAND_CHUNKS
    allowed = (kc <= qc) & (kc >= qc - BAND_CHUNKS)

    def body(t, acc):
        return jnp.where(rel == t, tab_ref[h, t], acc)

    acc = lax.fori_loop(0, 2 * REL_CLIP + 1, body, jnp.zeros((QPAIR, WIN), F32))
    o_ref[0] = jnp.where(allowed, acc * LOG2E, NEG)


def _relbias(table):
    return pl.pallas_call(
        _relbias_kernel,
        out_shape=jax.ShapeDtypeStruct((H_B, QPAIR, WIN), F32),
        grid=(H_B,),
        in_specs=[pl.BlockSpec(memory_space=pltpu.SMEM)],
        out_specs=pl.BlockSpec((1, QPAIR, WIN), lambda h: (h, 0, 0)),
        compiler_params=_cparams(("arbitrary",)),
        name="relbias",
    )(table)


BAND_TQ = 512


def _band_prompt_kernel(bias_ref, q_ref, kp_ref, kc_ref, vp_ref, vc_ref, o_ref, kcat, vcat):
    i = pl.program_id(1)
    kcat[0:BAND_TQ, :] = kp_ref[0].astype(BF16)
    kcat[BAND_TQ:, :] = kc_ref[0].astype(BF16)
    vcat[0:BAND_TQ, :] = vp_ref[0].astype(BF16)
    vcat[BAND_TQ:, :] = vc_ref[0].astype(BF16)
    lane = lax.broadcasted_iota(jnp.int32, (QPAIR, LANES), 1)

    def pair_block(qp, carry, *, first_block):
        r0 = pl.multiple_of(qp * QPAIR, QPAIR)
        q = q_ref[0, pl.ds(r0, QPAIR), :]
        if first_block:
            kpos = r0 - BAND_TQ + lax.broadcasted_iota(jnp.int32, (1, WIN), 1)
            before_start = jnp.where(kpos >= 0, 0.0, NEG)
        scores = []
        for hp in range(H_B // 2):
            cols = slice(hp * LANES, (hp + 1) * LANES)
            qs = jnp.concatenate(_split_halves(q[:, cols]), axis=0)
            s = _nt(qs, kcat[pl.ds(r0, WIN), cols]) + bias_ref[hp]
            scores.append(s + before_start if first_block else s)
        probs = []
        for s in scores:
            e = jnp.exp2(s - jnp.max(s, axis=-1, keepdims=True))
            probs.append((e.astype(BF16), jnp.sum(e, axis=-1, keepdims=True)))
        outs = []
        for hp, (e, l) in enumerate(probs):
            cols = slice(hp * LANES, (hp + 1) * LANES)
            o = jnp.dot(e, vcat[pl.ds(r0, WIN), cols], preferred_element_type=F32) / l
            outs.append(jnp.where(lane < DH_B, o[:QPAIR], o[QPAIR:]))
        o_ref[0, pl.ds(r0, QPAIR), :] = jnp.concatenate(outs, axis=1).astype(o_ref.dtype)
        return carry

    @pl.when(i == 0)
    def _():
        lax.fori_loop(0, BAND_TQ // QPAIR, functools.partial(pair_block, first_block=True), 0)

    @pl.when(i > 0)
    def _():
        lax.fori_loop(0, BAND_TQ // QPAIR, functools.partial(pair_block, first_block=False), 0)


def _band_prompt(q3, kv3, bias):
    b, t, _ = q3.shape
    assert t % BAND_TQ == 0
    w = H_B * DH_B
    cq, ck, cv = 1, 0, 1
    prev = lambda col: pl.BlockSpec((1, BAND_TQ, w), lambda bi, i: (bi, jnp.maximum(i - 1, 0), col))
    cur = lambda col: pl.BlockSpec((1, BAND_TQ, w), lambda bi, i: (bi, i, col))
    return pl.pallas_call(
        _band_prompt_kernel,
        out_shape=jax.ShapeDtypeStruct((b, t, w), BF16),
        grid=(b, t // BAND_TQ),
        in_specs=[
            pl.BlockSpec((H_B // 2, 2 * QPAIR, WIN), lambda bi, i: (0, 0, 0)),
            cur(cq), prev(ck), cur(ck), prev(cv), cur(cv),
        ],
        out_specs=pl.BlockSpec((1, BAND_TQ, w), lambda bi, i: (bi, i, 0)),
        scratch_shapes=[pltpu.VMEM((2 * BAND_TQ, w), BF16), pltpu.VMEM((2 * BAND_TQ, w), BF16)],
        compiler_params=_cparams(("parallel", "arbitrary")),
        name="band_prompt",
    )(bias.reshape(H_B // 2, 2 * QPAIR, WIN), q3, kv3, kv3, kv3, kv3)


def _band_sample_kernel(bc_ref, bn_ref, q_ref, kc_ref, vc_ref, kn_ref, vn_ref, o_ref, *, past):
    t = q_ref.shape[1]
    nband = kc_ref.shape[1]
    lane = lax.broadcasted_iota(jnp.int32, (t, LANES), 1)
    qpos_c = past + lax.broadcasted_iota(jnp.int32, (t, nband), 0)
    kpos_c = past - nband + lax.broadcasted_iota(jnp.int32, (t, nband), 1)
    qpos_n = past + lax.broadcasted_iota(jnp.int32, (t, t), 0)
    kpos_n = past + lax.broadcasted_iota(jnp.int32, (t, t), 1)

    def allowed(qpos, kpos):
        qc, kc = qpos // CHUNK, kpos // CHUNK
        return (kpos >= 0) & (kc <= qc) & (kc >= qc - BAND_CHUNKS)

    ok_c = allowed(qpos_c, kpos_c)
    ok_n = allowed(qpos_n, kpos_n)
    for hp in range(H_B // 2):
        cols = slice(hp * LANES, (hp + 1) * LANES)
        halves = []
        for sub, qm in enumerate(_split_halves(q_ref[0, :, cols])):
            hd = 2 * hp + sub
            pieces = [(kc_ref[0, :, cols].astype(BF16), vc_ref[0, :, cols].astype(BF16),
                       jnp.where(ok_c, bc_ref[hd], NEG)),
                      (kn_ref[0, :, cols].astype(BF16), vn_ref[0, :, cols].astype(BF16),
                       jnp.where(ok_n, bn_ref[hd], NEG))]
            halves.append(_pieces_attention(qm, pieces))
        o_ref[0, :, cols] = jnp.where(lane < DH_B, halves[0], halves[1]).astype(o_ref.dtype)


def _band_sample(q3, kv3, kc, vc, bias, past):
    b, t, _ = q3.shape
    nband = kc.shape[1]
    assert nband == BAND and t <= CHUNK
    w = H_B * DH_B
    bias_c = bias[:, :t, :nband]
    bias_n = bias[:, :t, nband:nband + t]
    blk = lambda col: pl.BlockSpec((1, t, w), lambda bi: (bi, 0, col))
    cache = pl.BlockSpec((1, nband, w), lambda bi: (bi, 0, 0))
    return pl.pallas_call(
        functools.partial(_band_sample_kernel, past=past),
        out_shape=jax.ShapeDtypeStruct((b, t, w), BF16),
        grid=(b,),
        in_specs=[
            pl.BlockSpec((H_B, t, nband), lambda bi: (0, 0, 0)),
            pl.BlockSpec((H_B, t, t), lambda bi: (0, 0, 0)),
            blk(1), cache, cache, blk(0), blk(1),
        ],
        out_specs=pl.BlockSpec((1, t, w), lambda bi: (bi, 0, 0)),
        compiler_params=_cparams(("parallel",)),
        name="band_sample",
    )(bias_c, bias_n, q3, kc, vc, kv3, kv3)


CONV_PAD = 8
MLSTM_L = 256


def _mlstm_kernel(cqk_ref, cv_ref, co_ref, cif_ref, conv0_ref, cw_ref, cb_ref, gb_ref, hg_ref,
                  c0_ref, n0_ref, m0_ref, hs_ref, c_ref, n_ref, m_ref, xext, *, L):
    @pl.when(pl.program_id(1) == 0)
    def _():
        xext[0:CONV_PAD, :] = conv0_ref[0]
        c_ref[...] = c0_ref[...]
        n_ref[...] = n0_ref[...]
        m_ref[...] = m0_ref[...]

    xext[CONV_PAD:CONV_PAD + L, :] = cqk_ref[0]
    base = CONV_PAD - (CONV_W - 1)
    u = 0.0
    for j in range(CONV_W):
        u = u + xext[base + j:base + j + L, :] * cw_ref[j:j + 1, :]
    u = cb_ref[...] + u
    u = u * jax.nn.sigmoid(u)
    xext[0:CONV_PAD, :] = xext[L:L + CONV_PAD, :]

    z = cif_ref[0] + gb_ref[...]
    lane = lax.broadcasted_iota(jnp.int32, (L, LANES), 1)
    lf = jnp.minimum(z, 0.0) - jnp.log1p(jnp.exp(-jnp.abs(z)))
    tr = lax.broadcasted_iota(jnp.int32, (L, L), 0)
    tc = lax.broadcasted_iota(jnp.int32, (L, L), 1)
    causal = tr >= tc
    b_all = jnp.dot(causal.astype(F32), lf, preferred_element_type=F32,
                    precision=lax.Precision.HIGHEST)
    sr = lax.broadcasted_iota(jnp.int32, (8, LANES), 0)
    sc = lax.broadcasted_iota(jnp.int32, (8, LANES), 1)
    sel = jnp.where(sr < H_C, jnp.where(sc == sr, 1.0, jnp.where(sc == sr + H_C, -1.0, 0.0)), 0.0)
    rt = lax.dot_general(sel, jnp.where(lane < H_C, z, b_all), NT_DIMS,
                         preferred_element_type=F32, precision=lax.Precision.HIGHEST)
    er = lax.broadcasted_iota(jnp.int32, (DH_C, DH_C), 0)
    ec = lax.broadcasted_iota(jnp.int32, (DH_C, DH_C), 1)
    eye = jnp.where(er == ec, 1.0, 0.0).astype(BF16)

    for h in range(H_C):
        cols = slice(h * DH_C, (h + 1) * DH_C)
        q = u[:, h * DH_C:(h + 1) * DH_C]
        k = u[:, D_C + h * DH_C:D_C + (h + 1) * DH_C] * (DH_C ** -0.5)
        v = cv_ref[0, :, cols]
        qb, kb_, vb_ = q.astype(BF16), k.astype(BF16), v.astype(BF16)
        b_col = b_all[:, H_C + h:H_C + h + 1]
        li_col = z[:, h:h + 1]
        m_prev = m_ref[0, h:h + 1, 0:1]
        cs = c_ref[0, h]
        ns = n_ref[0, h:h + 1, :]

        dmat = jnp.where(causal, b_col + rt[h:h + 1, :], NEG)
        inter = b_col + m_prev
        m_t = jnp.maximum(inter, jnp.max(dmat, axis=-1, keepdims=True))
        w_intra = jnp.exp(dmat - m_t)
        w_inter = jnp.exp(inter - m_t)
        a = w_intra * _nt(qb, kb_)
        num = (jnp.dot(a.astype(BF16), vb_, preferred_element_type=F32)
               + w_inter * _nt(qb, cs.astype(BF16)))
        den = jnp.sum(a, axis=-1, keepdims=True) + w_inter * jnp.sum(q * ns, axis=-1, keepdims=True)
        hh = num / jnp.maximum(jnp.abs(den), jnp.exp(-m_t))

        b_last = b_col[L - 1:L, :]
        g = b_last - b_col + li_col
        m_new = jnp.maximum(b_last + m_prev, jnp.max(g, axis=0, keepdims=True))
        ws = jnp.exp(g - m_new)
        decay = jnp.exp(b_last + m_prev - m_new)
        vwt = _nt(eye, (ws * v).astype(BF16)).astype(BF16)
        c_ref[0, h] = decay * cs + jnp.dot(vwt, kb_, preferred_element_type=F32)
        n_ref[0, h:h + 1, :] = decay * ns + jnp.sum(ws * k, axis=0, keepdims=True)
        m_ref[0, h:h + 1, :] = jnp.broadcast_to(m_new, (1, LANES))

        gate = jax.nn.sigmoid(co_ref[0, :, cols])
        hs_ref[0, :, cols] = (_rms(hh, hg_ref[...]) * gate).astype(hs_ref.dtype)


def _mlstm(cqk3, cvo3, cif3, conv0, conv_w, conv_b, gate_b, head_g, c0, n0, m0, L):
    b, t, _ = cqk3.shape
    nc = t // L
    blk = lambda col, w: pl.BlockSpec((1, L, w), lambda bi, c: (bi, c, col))
    const = lambda shape: pl.BlockSpec(shape, lambda bi, c: (0,) * len(shape))
    per_b = lambda shape: pl.BlockSpec((1,) + shape, lambda bi, c: (bi,) + (0,) * len(shape))
    return pl.pallas_call(
        functools.partial(_mlstm_kernel, L=L),
        out_shape=(jax.ShapeDtypeStruct((b, t, D_C), BF16),
                   jax.ShapeDtypeStruct((b, H_C, DH_C, DH_C), F32),
                   jax.ShapeDtypeStruct((b, H_C, DH_C), F32),
                   jax.ShapeDtypeStruct((b, H_C, LANES), F32)),
        grid=(b, nc),
        in_specs=[
            blk(0, 2 * D_C), blk(0, D_C), blk(1, D_C), blk(0, LANES),
            per_b((CONV_PAD, 2 * D_C)), const((CONV_W, 2 * D_C)), const((1, 2 * D_C)),
            const((1, LANES)), const((1, DH_C)),
            per_b((H_C, DH_C, DH_C)), per_b((H_C, DH_C)), per_b((H_C, LANES)),
        ],
        out_specs=(pl.BlockSpec((1, L, D_C), lambda bi, c: (bi, c, 0)),
                   per_b((H_C, DH_C, DH_C)), per_b((H_C, DH_C)), per_b((H_C, LANES))),
        scratch_shapes=[pltpu.VMEM((CONV_PAD + L, 2 * D_C), F32)],
        compiler_params=_cparams(("parallel", "arbitrary")),
        name="mlstm",
    )(cqk3, cvo3, cvo3, cif3, conv0, conv_w, conv_b, gate_b, head_g, c0, n0, m0)


def _mix_kernel(oa_ref, ob_ref, oc_ref, ga_ref, gb_ref, gc_ref, x_ref, wa_ref, wb_ref, wc_ref, wo_ref, o_ref):
    mixed = (ga_ref[...] * jnp.dot(oa_ref[...], wa_ref[...], preferred_element_type=F32)
             + gb_ref[...] * jnp.dot(ob_ref[...], wb_ref[...], preferred_element_type=F32)
             + gc_ref[...] * jnp.dot(oc_ref[...], wc_ref[...], preferred_element_type=F32))
    o_ref[...] = x_ref[...] + jnp.dot(mixed.astype(BF16), wo_ref[...], preferred_element_type=F32)


def _mix(oa, ob, oc, gates, x, l, wa, wb, wc, wo):
    m, d = x.shape
    tm = min(m, 512)
    row = lambda w, col=0: pl.BlockSpec((tm, w), lambda i: (i, col))
    full = lambda a: pl.BlockSpec((None,) + a.shape[1:], lambda i: (l, 0, 0))
    g0 = 0
    return pl.pallas_call(
        _mix_kernel,
        out_shape=jax.ShapeDtypeStruct((m, d), F32),
        grid=(m // tm,),
        in_specs=[row(oa.shape[1]), row(ob.shape[1]), row(oc.shape[1]),
                  row(d, g0), row(d, g0 + 1), row(d, g0 + 2), row(d),
                  full(wa), full(wb), full(wc), full(wo)],
        out_specs=row(d),
        compiler_params=_cparams(("parallel",)),
        name="mix",
    )(oa, ob, oc, gates, gates, gates, x, wa, wb, wc, wo)


def _cross_kernel(x_ref, g_ref, wq_ref, mk_ref, mv_ref, wo_ref, o_ref):
    x = x_ref[0]
    h = _rms(x, g_ref[...]).astype(BF16)
    q = (jnp.dot(h, wq_ref[...], preferred_element_type=F32) * (DH_M ** -0.5 * LOG2E)).astype(BF16)
    outs = []
    for hd in range(H_M):
        cols = slice(hd * DH_M, (hd + 1) * DH_M)
        s = _nt(q[:, cols], mk_ref[0, :, cols].astype(BF16))
        m = jnp.max(s, axis=-1, keepdims=True)
        e = jnp.exp2(s - m)
        l = jnp.sum(e, axis=-1, keepdims=True)
        o = jnp.dot(e.astype(BF16), mv_ref[0, :, cols].astype(BF16), preferred_element_type=F32) / l
        outs.append(o.astype(BF16))
    o = jnp.concatenate(outs, axis=1)
    o_ref[0] = x + jnp.dot(o, wo_ref[...], preferred_element_type=F32)


def _cross(x3, g, l, wq, mk, mv, wo):
    b, t, d = x3.shape
    tm = min(t, 512)
    nm = mk.shape[1]
    full = lambda a: pl.BlockSpec((None,) + a.shape[1:], lambda bi, i: (l, 0, 0))
    return pl.pallas_call(
        _cross_kernel,
        out_shape=jax.ShapeDtypeStruct((b, t, d), F32),
        grid=(b, t // tm),
        in_specs=[pl.BlockSpec((1, tm, d), lambda bi, i: (bi, i, 0)),
                  pl.BlockSpec(g.shape, lambda bi, i: (0, 0)), full(wq),
                  pl.BlockSpec((1, nm, d), lambda bi, i: (bi, 0, 0)),
                  pl.BlockSpec((1, nm, d), lambda bi, i: (bi, 0, 0)), full(wo)],
        out_specs=pl.BlockSpec((1, tm, d), lambda bi, i: (bi, i, 0)),
        compiler_params=_cparams(("parallel", "parallel")),
        name="cross",
    )(x3, g, wq, mk, mv, wo)


def _ffn_kernel(x_ref, g_ref, wg_ref, wu_ref, wd_ref, gf_ref, o_ref, h_scr, acc, *, final_norm):
    j = pl.program_id(1)

    @pl.when(j == 0)
    def _():
        h_scr[...] = _rms(x_ref[...], g_ref[...]).astype(BF16)
        acc[...] = jnp.zeros(acc.shape, F32)

    h = h_scr[...]
    gate = jnp.dot(h, wg_ref[...], preferred_element_type=F32)
    up = jnp.dot(h, wu_ref[...], preferred_element_type=F32)
    a = (gate * jax.nn.sigmoid(gate) * up).astype(BF16)
    acc[...] += jnp.dot(a, wd_ref[...], preferred_element_type=F32)

    @pl.when(j == pl.num_programs(1) - 1)
    def _():
        y = x_ref[...] + acc[...]
        if final_norm:
            y = _rms(y, gf_ref[...])
        o_ref[...] = y


FFN_TF = 1408
FFN_TM = 512


def _ffn(x, g, l, wg, wu, wd, g_final, final_norm):
    m, d = x.shape
    tm = min(m, FFN_TM)
    return pl.pallas_call(
        functools.partial(_ffn_kernel, final_norm=final_norm),
        out_shape=jax.ShapeDtypeStruct((m, d), F32),
        grid=(m // tm, D_FF // FFN_TF),
        in_specs=[
            pl.BlockSpec((tm, d), lambda i, j: (i, 0)),
            pl.BlockSpec((1, d), lambda i, j: (0, 0)),
            pl.BlockSpec((None, d, FFN_TF), lambda i, j: (l, 0, j)),
            pl.BlockSpec((None, d, FFN_TF), lambda i, j: (l, 0, j)),
            pl.BlockSpec((None, FFN_TF, d), lambda i, j: (l, j, 0)),
            pl.BlockSpec((1, d), lambda i, j: (0, 0)),
        ],
        out_specs=pl.BlockSpec((tm, d), lambda i, j: (i, 0)),
        scratch_shapes=[pltpu.VMEM((tm, d), BF16), pltpu.VMEM((tm, d), F32)],
        compiler_params=_cparams(("parallel", "arbitrary")),
        name="ffn",
    )(x, g, wg, wu, wd, g_final)


def _layer(x3, l, w, mem_k, mem_v, cache, final_norm, akv_all):
    b, t, d = x3.shape
    x2 = x3.reshape(b * t, d)
    row = lambda a: a.reshape(1, -1)
    lam_init = 0.8 - 0.6 * math.exp(-0.3 * l)

    gates, cqk, cif = _in_proj_a(x2, row(w['g_mix']), l, w['w_in'], w['w_if'])
    q, ak_all, av_all, akv, bkv, cvo = _in_proj_b(x2, row(w['g_mix']), w['w_in'], l, *akv_all, (b, t))
    q3 = q.reshape(b, t, SEG)
    bkv3 = bkv.reshape(b, t, SEG)
    cqk3 = cqk.reshape(b, t, 2 * D_C)
    keep = t if cache is not None else min(BAND, t)
    b_k = bkv3[:, t - keep:, :HALF].reshape(b, keep, H_B, DH_B)
    b_v = bkv3[:, t - keep:, HALF:].reshape(b, keep, H_B, DH_B)

    gate_b = jnp.concatenate([w['b_i'], w['b_f'], jnp.zeros((LANES - 2 * H_C,), F32)]).reshape(1, LANES)
    if cache is None:
        oa = _diff_prompt(q3, akv.reshape(b, t, SEG), w['lqk'], row(w['a_head_g']), lam_init)
        ob = _band_prompt(q3, bkv3, w['relbias'])
        conv0 = jnp.zeros((b, CONV_PAD, 2 * D_C), F32)
        c0 = jnp.zeros((b, H_C, DH_C, DH_C), F32)
        n0 = jnp.zeros((b, H_C, DH_C), F32)
        m0 = jnp.zeros((b, H_C, LANES), F32)
        L = min(t, MLSTM_L)
    else:
        past = cache['a_k'].shape[1]
        ak3 = ak_all.reshape(DEPTH * b, t, HALF)
        av3 = av_all.reshape(DEPTH * b, t, HALF)
        oa = _diff_sample(q3, ak3, av3, l, cache['a_k'].reshape(b, past, -1), cache['a_v'].reshape(b, past, -1),
                          w['lqk'], row(w['a_head_g']), lam_init)
        nband = cache['b_k'].shape[1]
        ob = _band_sample(q3, bkv3, cache['b_k'].reshape(b, nband, -1), cache['b_v'].reshape(b, nband, -1),
                          w['relbias'], past)
        conv0 = jnp.pad(cache['conv'], ((0, 0), (CONV_PAD - (CONV_W - 1), 0), (0, 0)))
        c0, n0 = cache['C'], cache['n']
        m0 = jnp.broadcast_to(cache['m'][:, :, None], (b, H_C, LANES))
        L = t
    oc, c_new, n_new, m_new = _mlstm(cqk3, cvo.reshape(b, t, SEG), cif.reshape(b, t, LANES), conv0,
                                     w['conv_w'], row(w['conv_b']), gate_b, row(w['c_head_g']), c0, n0, m0, L)
    assert t >= CONV_W - 1
    conv_new = cqk3[:, t - (CONV_W - 1):]

    x2 = _mix(oa.reshape(b * t, -1), ob.reshape(b * t, -1), oc.reshape(b * t, -1), gates, x2, l,
              w['w_up_a'], w['w_up_b'], w['w_up_c'], w['w_o'])
    x3 = _cross(x2.reshape(b, t, d), row(w['g_cross']), l, w['w_mq'], mem_k, mem_v, w['w_mo'])
    x2 = _ffn(x3.reshape(b * t, d), row(w['g_ffn']), l, w['w_ff_g'], w['w_ff_u'], w['w_ff_d'],
              row(w['g_final']), final_norm)
    return x2.reshape(b, t, d), (ak_all, av_all), (b_k, b_v, c_new, n_new, m_new[:, :, 0], conv_new)


def kernel(x_prompt, x_sample, cache_a_k, cache_a_v, cache_b_k, cache_b_v, state_c_C, state_c_n, state_c_m, state_c_conv, cache_mem_k, cache_mem_v, mem_prompt, g_mix, w_in, a_lq1, a_lk1, a_lq2, a_lk2, a_head_g, b_rel, c_conv_w, c_conv_b, c_b_i, c_b_f, c_head_g, w_up_a, w_up_b, w_up_c, w_o, g_cross, w_mq, w_mk, w_mv, w_mo, g_ffn, w_ff_g, w_ff_u, w_ff_d, g_final):
    xp, xs = x_prompt, x_sample
    bp = x_prompt.shape[0]
    n_mem = mem_prompt.shape[1]
    mem2 = mem_prompt.reshape(bp * n_mem, D_MODEL)
    new_p = [[] for _ in range(8)]
    new_s = [[] for _ in range(6)]
    akv_p = akv_s = (None, None)
    bf = lambda a: a.astype(BF16)
    wp, wif = _prep_w_in(w_in)
    wts = dict(w_up_a=bf(w_up_a), w_up_b=bf(w_up_b), w_up_c=bf(w_up_c), w_o=bf(w_o), w_mq=bf(w_mq),
               w_mo=bf(w_mo), w_ff_g=bf(w_ff_g), w_ff_u=bf(w_ff_u), w_ff_d=bf(w_ff_d))
    w_mk_b, w_mv_b = bf(w_mk), bf(w_mv)
    for l in range(DEPTH):
        w = dict(g_mix=g_mix[l], w_in=wp, w_if=wif,
                 lqk=jnp.stack([a_lq1[l], a_lk1[l], a_lq2[l], a_lk2[l]]),
                 a_head_g=a_head_g[l], relbias=_relbias(b_rel[l]), conv_w=c_conv_w[l], conv_b=c_conv_b[l],
                 b_i=c_b_i[l], b_f=c_b_f[l], c_head_g=c_head_g[l],
                 g_cross=g_cross[l], g_ffn=g_ffn[l], g_final=g_final, **wts)
        mk2 = _mem_proj(mem2, l, w_mk_b).reshape(bp, n_mem, D_MODEL)
        mv2 = _mem_proj(mem2, l, w_mv_b).reshape(bp, n_mem, D_MODEL)
        last = l == DEPTH - 1
        xp, akv_p, st_p = _layer(xp, l, w, mk2, mv2, None, last, akv_p)
        cache = dict(a_k=cache_a_k[l], a_v=cache_a_v[l], b_k=cache_b_k[l], b_v=cache_b_v[l],
                     C=state_c_C[l], n=state_c_n[l], m=state_c_m[l], conv=state_c_conv[l])
        bs = xs.shape[0]
        xs, akv_s, st_s = _layer(xs, l, w, cache_mem_k[l].reshape(bs, n_mem, D_MODEL),
                                 cache_mem_v[l].reshape(bs, n_mem, D_MODEL), cache, last, akv_s)
        mk4 = mk2.reshape(bp, n_mem, H_M, DH_M)
        mv4 = mv2.reshape(bp, n_mem, H_M, DH_M)
        for lst, a in zip(new_p, st_p + (mk4, mv4)):
            lst.append(a)
        for lst, a in zip(new_s, st_s):
            lst.append(a)
    heads = lambda a, b, t: a.reshape(DEPTH, b, t, H_A, DV_A)
    outs_p = [heads(a, bp, x_prompt.shape[1]) for a in akv_p] + [jnp.stack(a, 0) for a in new_p]
    outs_s = [heads(a, xs.shape[0], xs.shape[1]) for a in akv_s] + [jnp.stack(a, 0) for a in new_s]
    return (xp, xs) + tuple(outs_p) + tuple(outs_s)
```

```python
import functools
import math

import jax
import jax.numpy as jnp
from jax import lax
from jax.experimental import pallas as pl
from jax.experimental.pallas import tpu as pltpu

F32 = jnp.float32
BF16 = jnp.bfloat16

D_MODEL = 1024
DEPTH = 2
CHUNK = 64
EPS = 1e-6
NEG = -1e30
LOG2E = math.log2(math.e)
H_A = 4
DH_A = 64
DV_A = 2 * DH_A
H_B = 8
DH_B = 64
BAND_CHUNKS = 8
BAND = BAND_CHUNKS * CHUNK
REL_CLIP = 128
H_C = 4
DH_C = 128
D_C = H_C * DH_C
CONV_W = 4
H_M = 4
DH_M = D_MODEL // H_M
D_FF = -(-8 * D_MODEL // (3 * 256)) * 256

LANES = 128
VMEM_LIMIT = 48 * 1024 * 1024

SEG = 1024
HALF = SEG // 2
GATE_TILES = 3
N_TILES_A = GATE_TILES + 1
N_TILES_B = 4
IN_SIZES = (H_A * DH_A,) * 4 + (H_A * DV_A,) + (H_B * DH_B,) * 3 + (2 * D_C, D_C, D_C, 2 * H_C, 3 * D_MODEL)

ALIBI_SLOPES = tuple(2.0 ** (-8.0 * (i + 1) / H_A) for i in range(H_A))

NT_DIMS = (((1,), (1,)), ((), ()))


def _cparams(sem):
    return pltpu.CompilerParams(dimension_semantics=sem, vmem_limit_bytes=VMEM_LIMIT)


def _rms(x, g):
    ms = jnp.mean(x * x, axis=-1, keepdims=True)
    return x * lax.rsqrt(ms + EPS) * g


def _nt(a, b):
    return lax.dot_general(a, b, NT_DIMS, preferred_element_type=F32)


def _prep_w_in_kernel(w_ref, o_ref, oif_ref):
    offs = [0]
    for size in IN_SIZES:
        offs.append(offs[-1] + size)
    seg = lambda i: w_ref[:, offs[i]:offs[i + 1]]
    a_q1, a_q2, a_k1, a_k2, a_v, b_q, b_k, b_v, c_qk, c_v, c_o = (seg(i) for i in range(11))
    g = seg(12)

    def put(col, val):
        o_ref[:, col:col + val.shape[1]] = val.astype(BF16)

    put(0, g)
    put(GATE_TILES * SEG, c_qk)
    for h in range(H_A):
        hs = slice(h * DH_A, (h + 1) * DH_A)
        base = h * 2 * DH_A
        put(4 * SEG + base, a_q1[:, hs] * (DH_A ** -0.5))
        put(4 * SEG + base + DH_A, a_q2[:, hs] * (DH_A ** -0.5))
        put(5 * SEG + base, a_k1[:, hs])
        put(5 * SEG + base + DH_A, a_k2[:, hs])
    put(4 * SEG + HALF, b_q * (DH_B ** -0.5))
    put(5 * SEG + HALF, a_v)
    put(6 * SEG, b_k)
    put(6 * SEG + HALF, b_v)
    put(7 * SEG, c_v)
    put(7 * SEG + HALF, c_o)
    tile = w_ref[:, offs[11]:offs[11] + LANES]
    lane = lax.broadcasted_iota(jnp.int32, tile.shape, 1)
    oif_ref[...] = jnp.where(lane < 2 * H_C, tile, 0.0).astype(BF16)


def _prep_w_in(w):
    depth, k, n = w.shape
    tk = 256
    n_out = (N_TILES_A + N_TILES_B) * SEG
    return pl.pallas_call(
        _prep_w_in_kernel,
        out_shape=(jax.ShapeDtypeStruct((depth, k, n_out), BF16),
                   jax.ShapeDtypeStruct((depth, k, LANES), BF16)),
        grid=(depth, k // tk),
        in_specs=[pl.BlockSpec((None, tk, n), lambda l, i: (l, i, 0))],
        out_specs=(pl.BlockSpec((None, tk, n_out), lambda l, i: (l, i, 0)),
                   pl.BlockSpec((None, tk, LANES), lambda l, i: (l, i, 0))),
        compiler_params=_cparams(("parallel", "parallel")),
        name="prep_w_in",
    )(w)


def _norm_to_scratch(x_ref, g_ref, h_scr):
    h_scr[...] = _rms(x_ref[...], g_ref[...]).astype(BF16)


def _in_proj_a_kernel(x_ref, g_ref, w_ref, wif_ref, og_ref, ocqk_ref, ocif_ref, h_scr):
    j = pl.program_id(1)

    @pl.when(j == 0)
    def _():
        _norm_to_scratch(x_ref, g_ref, h_scr)
        ocif_ref[...] = jnp.dot(h_scr[...], wif_ref[...], preferred_element_type=F32)

    tile = lambda: jnp.dot(h_scr[...], w_ref[...], preferred_element_type=F32)

    @pl.when(j < GATE_TILES)
    def _():
        og_ref[...] = jax.nn.sigmoid(tile()).astype(og_ref.dtype)

    @pl.when(j == GATE_TILES)
    def _():
        ocqk_ref[...] = tile()


def _in_proj_a(x, g, l, wp, wif):
    m, k = x.shape
    tm = min(m, 1024)
    return pl.pallas_call(
        _in_proj_a_kernel,
        out_shape=(jax.ShapeDtypeStruct((m, GATE_TILES * SEG), BF16),
                   jax.ShapeDtypeStruct((m, SEG), F32),
                   jax.ShapeDtypeStruct((m, LANES), F32)),
        grid=(m // tm, N_TILES_A),
        in_specs=[
            pl.BlockSpec((tm, k), lambda i, j: (i, 0)),
            pl.BlockSpec((1, k), lambda i, j: (0, 0)),
            pl.BlockSpec((None, k, SEG), lambda i, j: (l, 0, j)),
            pl.BlockSpec((None, k, LANES), lambda i, j: (l, 0, 0)),
        ],
        out_specs=(pl.BlockSpec((tm, SEG), lambda i, j: (i, jnp.minimum(j, GATE_TILES - 1))),
                   pl.BlockSpec((tm, SEG), lambda i, j: (i, 0)),
                   pl.BlockSpec((tm, LANES), lambda i, j: (i, 0))),
        scratch_shapes=[pltpu.VMEM((tm, k), BF16)],
        compiler_params=_cparams(("parallel", "arbitrary")),
        name="in_proj_a",
    )(x, g, wp, wif)


def _in_proj_b_kernel(*refs, aliased, heads5d):
    x_ref, g_ref, w_ref = refs[:3]
    oq_ref, oak_ref, oav_ref, oakv_ref, obkv_ref, ocvo_ref, h_scr = refs[5 if aliased else 3:]
    j = pl.program_id(1)

    @pl.when(j == 0)
    def _():
        _norm_to_scratch(x_ref, g_ref, h_scr)

    tile = lambda: jnp.dot(h_scr[...], w_ref[...], preferred_element_type=F32)

    @pl.when(j == 0)
    def _():
        oq_ref[...] = (tile() * LOG2E).astype(oq_ref.dtype)

    @pl.when(j == 1)
    def _():
        for out_ref, off in ((oak_ref, 0), (oav_ref, HALF)):
            a = jnp.dot(h_scr[...], w_ref[:, off:off + HALF], preferred_element_type=F32)
            oakv_ref[:, off:off + HALF] = a.astype(oakv_ref.dtype)
            if heads5d:
                for hd in range(H_A):
                    out_ref[0, 0, :, hd, :] = a[:, hd * DV_A:(hd + 1) * DV_A]
            else:
                out_ref[0] = a

    @pl.when(j == 2)
    def _():
        obkv_ref[...] = tile()

    @pl.when(j == 3)
    def _():
        ocvo_ref[...] = tile()


def _in_proj_b(x, g, wp, l, ak_all, av_all, bt):
    m, k = x.shape
    tm = min(m, 1024)
    aliased = ak_all is not None
    b, t = bt
    heads5d = t % tm == 0
    if heads5d:
        stacked = jax.ShapeDtypeStruct((DEPTH, b, t, H_A, DV_A), F32)
        layer_row = pl.BlockSpec((1, 1, tm, H_A, DV_A),
                                 lambda i, j: (l, i // (t // tm), i % (t // tm), 0, 0))
    else:
        stacked = jax.ShapeDtypeStruct((DEPTH, m, HALF), F32)
        layer_row = pl.BlockSpec((1, tm, HALF), lambda i, j: (l, i, 0))
    in_specs = [
        pl.BlockSpec((tm, k), lambda i, j: (i, 0)),
        pl.BlockSpec((1, k), lambda i, j: (0, 0)),
        pl.BlockSpec((None, k, SEG), lambda i, j: (l, 0, N_TILES_A + j)),
    ]
    args = [x, g, wp]
    if aliased:
        in_specs += [pl.BlockSpec(memory_space=pl.ANY)] * 2
        args += [ak_all, av_all]
    row = lambda w: pl.BlockSpec((tm, w), lambda i, j: (i, 0))
    return pl.pallas_call(
        functools.partial(_in_proj_b_kernel, aliased=aliased, heads5d=heads5d),
        out_shape=(jax.ShapeDtypeStruct((m, SEG), BF16), stacked, stacked, jax.ShapeDtypeStruct((m, SEG), BF16),
                   jax.ShapeDtypeStruct((m, SEG), F32), jax.ShapeDtypeStruct((m, SEG), F32)),
        grid=(m // tm, N_TILES_B),
        in_specs=in_specs,
        out_specs=(row(SEG), layer_row, layer_row, row(SEG), row(SEG), row(SEG)),
        scratch_shapes=[pltpu.VMEM((tm, k), BF16)],
        input_output_aliases={3: 1, 4: 2} if aliased else {},
        compiler_params=_cparams(("parallel", "arbitrary")),
        name="in_proj_b",
    )(*args)


def _mem_proj_kernel(x_ref, w_ref, o_ref):
    o_ref[...] = jnp.dot(x_ref[...].astype(BF16), w_ref[...], preferred_element_type=F32)


def _mem_proj(x, l, w):
    m, k = x.shape
    n = w.shape[2]
    tm = min(m, 1024)
    return pl.pallas_call(
        _mem_proj_kernel,
        out_shape=jax.ShapeDtypeStruct((m, n), F32),
        grid=(m // tm,),
        in_specs=[pl.BlockSpec((tm, k), lambda i: (i, 0)), pl.BlockSpec((None, k, n), lambda i: (l, 0, 0))],
        out_specs=pl.BlockSpec((tm, n), lambda i: (i, 0)),
        compiler_params=_cparams(("parallel",)),
        name="mem_proj",
    )(x, w)


def _online_updates(scores, v, states):
    staged = []
    for s, (acc, m, l) in zip(scores, states):
        slabs = [s[:, j * LANES:(j + 1) * LANES] for j in range(s.shape[1] // LANES)]
        m_old = m[...]
        m_new = jnp.maximum(m_old, jnp.max(functools.reduce(jnp.maximum, slabs), axis=-1, keepdims=True))
        alpha = jnp.exp2(m_old - m_new)
        ps = [jnp.exp2(c - m_new) for c in slabs]
        l[...] = alpha * l[...] + jnp.sum(functools.reduce(jnp.add, ps), axis=-1, keepdims=True)
        m[...] = m_new
        staged.append((alpha, jnp.concatenate(ps, axis=1).astype(BF16)))
    for (alpha, p), (acc, m, l) in zip(staged, states):
        acc[...] = alpha * acc[...] + jnp.dot(p, v, preferred_element_type=F32)


def _split_halves(q):
    lane = lax.broadcasted_iota(jnp.int32, q.shape, 1)
    lo = jnp.where(lane < DH_A, q, 0.0).astype(BF16)
    hi = jnp.where(lane >= DH_A, q, 0.0).astype(BF16)
    return lo, hi


def _lambda(lqk, lam_init):
    e1 = jnp.exp(jnp.sum(lqk[0:1, :] * lqk[1:2, :], axis=-1, keepdims=True))
    e2 = jnp.exp(jnp.sum(lqk[2:3, :] * lqk[3:4, :], axis=-1, keepdims=True))
    return e1 - e2 + lam_init


def _diff_prompt_kernel(slopes_ref, lqk_ref, hg_ref, q_ref, k_ref, v_ref, o_ref,
                        acc1, acc2, m1, l1, m2, l2, *, tq, lam_init):
    h = pl.program_id(1)
    qi = pl.program_id(2)
    slope = slopes_ref[h] * LOG2E
    kb, vb = k_ref.at[0], v_ref.at[0]

    qa, qb = _split_halves(q_ref[0])
    states = ((acc1, m1, l1), (acc2, m2, l2))
    for acc, m, l in states:
        m[...] = jnp.full(m.shape, NEG, F32)
        l[...] = jnp.zeros(l.shape, F32)
        acc[...] = jnp.zeros(acc.shape, F32)

    def full_tile(ki, carry):
        k0 = pl.multiple_of(ki * tq, tq)
        k = kb[pl.ds(k0, tq), :]
        v = vb[pl.ds(k0, tq), :]
        kpos = k0 + lax.broadcasted_iota(jnp.int32, (1, tq), 1)
        bias = slope * kpos.astype(F32)
        _online_updates([_nt(qa, k) + bias, _nt(qb, k) + bias], v, states)
        return carry

    lax.fori_loop(0, qi, full_tile, 0)

    q0 = pl.multiple_of(qi * tq, tq)
    k = kb[pl.ds(q0, tq), :]
    v = vb[pl.ds(q0, tq), :]
    r = lax.broadcasted_iota(jnp.int32, (tq, tq), 0)
    c = lax.broadcasted_iota(jnp.int32, (tq, tq), 1)
    allowed = (c // CHUNK) <= (r // CHUNK)
    bias = slope * (q0 + r - jnp.abs(r - c)).astype(F32)
    _online_updates([jnp.where(allowed, _nt(qa, k) + bias, NEG),
                     jnp.where(allowed, _nt(qb, k) + bias, NEG)], v, states)

    lam = _lambda(lqk_ref[...], lam_init)
    o = acc1[...] / l1[...] - lam * (acc2[...] / l2[...])
    o_ref[0] = (_rms(o, hg_ref[...]) * (1.0 - lam_init)).astype(o_ref.dtype)


def _diff_prompt(q3, kv3, lqk, head_g, lam_init):
    b, t, _ = q3.shape
    tq = min(t, 512)
    slopes = jnp.asarray(ALIBI_SLOPES, F32)
    return pl.pallas_call(
        functools.partial(_diff_prompt_kernel, tq=tq, lam_init=lam_init),
        out_shape=jax.ShapeDtypeStruct((b, t, H_A * DV_A), BF16),
        grid=(b, H_A, t // tq),
        in_specs=[
            pl.BlockSpec(memory_space=pltpu.SMEM),
            pl.BlockSpec((4, DH_A), lambda bi, h, qi: (0, 0)),
            pl.BlockSpec((1, DV_A), lambda bi, h, qi: (0, 0)),
            pl.BlockSpec((1, tq, LANES), lambda bi, h, qi: (bi, qi, h)),
            pl.BlockSpec((1, t, LANES), lambda bi, h, qi: (bi, 0, h)),
            pl.BlockSpec((1, t, LANES), lambda bi, h, qi: (bi, 0, H_A + h)),
        ],
        out_specs=pl.BlockSpec((1, tq, LANES), lambda bi, h, qi: (bi, qi, h)),
        scratch_shapes=[
            pltpu.VMEM((tq, LANES), F32), pltpu.VMEM((tq, LANES), F32),
            pltpu.VMEM((tq, LANES), F32), pltpu.VMEM((tq, LANES), F32),
            pltpu.VMEM((tq, LANES), F32), pltpu.VMEM((tq, LANES), F32),
        ],
        compiler_params=_cparams(("parallel", "parallel", "arbitrary")),
        name="diff_prompt",
    )(slopes, lqk, head_g, q3, kv3, kv3)


def _pieces_attention(qm, pieces):
    ss = [_nt(qm, k) + bias for k, _, bias in pieces]
    m = functools.reduce(jnp.maximum, [jnp.max(s, axis=-1, keepdims=True) for s in ss])
    es = [jnp.exp2(s - m) for s in ss]
    l = functools.reduce(jnp.add, [jnp.sum(e, axis=-1, keepdims=True) for e in es])
    o = functools.reduce(jnp.add, [jnp.dot(e.astype(BF16), v, preferred_element_type=F32)
                                   for e, (_, v, _) in zip(es, pieces)])
    return o / l


def _diff_sample_kernel(lqk_ref, hg_ref, q_ref, kc_ref, vc_ref, kn_ref, vn_ref, o_ref, *, lam_init):
    t = q_ref.shape[1]
    past = kc_ref.shape[1]
    lam = _lambda(lqk_ref[...], lam_init)
    rc = lax.broadcasted_iota(jnp.int32, (t, past), 0)
    cc = lax.broadcasted_iota(jnp.int32, (t, past), 1)
    rn = lax.broadcasted_iota(jnp.int32, (t, t), 0)
    cn = lax.broadcasted_iota(jnp.int32, (t, t), 1)
    dist_c = jnp.abs(past + rc - cc).astype(F32)
    dist_n = jnp.abs(rn - cn).astype(F32)
    ok_c = (cc // CHUNK) <= ((past + rc) // CHUNK)
    ok_n = ((past + cn) // CHUNK) <= ((past + rn) // CHUNK)
    for h in range(H_A):
        cols = slice(h * LANES, (h + 1) * LANES)
        bias_c = jnp.where(ok_c, (-ALIBI_SLOPES[h] * LOG2E) * dist_c, NEG)
        bias_n = jnp.where(ok_n, (-ALIBI_SLOPES[h] * LOG2E) * dist_n, NEG)
        pieces = [(kc_ref[0, :, h, :].astype(BF16), vc_ref[0, :, h, :].astype(BF16), bias_c),
                  (kn_ref[0, :, cols].astype(BF16), vn_ref[0, :, cols].astype(BF16), bias_n)]
        qa, qb = _split_halves(q_ref[0, :, cols])
        o = _pieces_attention(qa, pieces) - lam * _pieces_attention(qb, pieces)
        o_ref[0, :, cols] = (_rms(o, hg_ref[...]) * (1.0 - lam_init)).astype(o_ref.dtype)


def _diff_sample(q3, kn3, vn3, l, kc, vc, lqk, head_g, lam_init):
    b, t, _ = q3.shape
    past = kc.shape[2]
    w = H_A * DV_A
    new = pl.BlockSpec((1, t, w), lambda bi: (l * b + bi, 0, 0))
    cache = pl.BlockSpec((None, 1, past, H_A, DV_A), lambda bi: (l, bi, 0, 0, 0))
    return pl.pallas_call(
        functools.partial(_diff_sample_kernel, lam_init=lam_init),
        out_shape=jax.ShapeDtypeStruct((b, t, w), BF16),
        grid=(b,),
        in_specs=[
            pl.BlockSpec((4, DH_A), lambda bi: (0, 0)),
            pl.BlockSpec((1, DV_A), lambda bi: (0, 0)),
            pl.BlockSpec((1, t, w), lambda bi: (bi, 0, 0)), cache, cache, new, new,
        ],
        out_specs=pl.BlockSpec((1, t, w), lambda bi: (bi, 0, 0)),
        compiler_params=_cparams(("parallel",)),
        name="diff_sample",
    )(lqk, head_g, q3, kc, vc, kn3, vn3)


QPAIR = 2 * CHUNK
WIN = BAND + QPAIR


def _relbias_kernel(tab_ref, o_ref):
    h = pl.program_id(0)
    r = lax.broadcasted_iota(jnp.int32, (QPAIR, WIN), 0)
    j = lax.broadcasted_iota(jnp.int32, (QPAIR, WIN), 1)
    rel = jnp.clip(r - j + BAND, -REL_CLIP, REL_CLIP) + REL_CLIP
    qc = r // CHUNK
    kc = j // CHUNK - BAND_CHUNKS
    allowed = (kc <= qc) & (kc >= qc - BAND_CHUNKS)

    far = BAND - REL_CLIP
    near_rel = rel[:, far:]

    near = jnp.zeros((QPAIR, WIN - far), F32)
    for t in range(2 * REL_CLIP + 1):
        near = jnp.where(near_rel == t, tab_ref[h, t], near)
    bias = jnp.concatenate([jnp.full((QPAIR, far), tab_ref[h, 2 * REL_CLIP], F32), near], axis=1)
    o_ref[0] = jnp.where(allowed, bias * LOG2E, NEG)


def _relbias(table):
    return pl.pallas_call(
        _relbias_kernel,
        out_shape=jax.ShapeDtypeStruct((H_B, QPAIR, WIN), F32),
        grid=(H_B,),
        in_specs=[pl.BlockSpec(memory_space=pltpu.SMEM)],
        out_specs=pl.BlockSpec((1, QPAIR, WIN), lambda h: (h, 0, 0)),
        compiler_params=_cparams(("arbitrary",)),
        name="relbias",
    )(table)


BAND_TQ = 512


def _band_prompt_kernel(bias_ref, q_ref, kp_ref, kc_ref, vp_ref, vc_ref, o_ref, kcat, vcat):
    i = pl.program_id(1)
    kcat[0:BAND_TQ, :] = kp_ref[0].astype(BF16)
    kcat[BAND_TQ:, :] = kc_ref[0].astype(BF16)
    vcat[0:BAND_TQ, :] = vp_ref[0].astype(BF16)
    vcat[BAND_TQ:, :] = vc_ref[0].astype(BF16)
    lane = lax.broadcasted_iota(jnp.int32, (QPAIR, LANES), 1)

    def pair_block(qp, carry, *, first_block):
        r0 = pl.multiple_of(qp * QPAIR, QPAIR)
        q = q_ref[0, pl.ds(r0, QPAIR), :]
        if first_block:
            kpos = r0 - BAND_TQ + lax.broadcasted_iota(jnp.int32, (1, WIN), 1)
            before_start = jnp.where(kpos >= 0, 0.0, NEG)
        scores = []
        for hp in range(H_B // 2):
            cols = slice(hp * LANES, (hp + 1) * LANES)
            qs = jnp.concatenate(_split_halves(q[:, cols]), axis=0)
            s = _nt(qs, kcat[pl.ds(r0, WIN), cols]) + bias_ref[hp]
            scores.append(s + before_start if first_block else s)
        probs = []
        for s in scores:
            e = jnp.exp2(s - jnp.max(s, axis=-1, keepdims=True))
            probs.append((e.astype(BF16), jnp.sum(e, axis=-1, keepdims=True)))
        outs = []
        for hp, (e, l) in enumerate(probs):
            cols = slice(hp * LANES, (hp + 1) * LANES)
            o = jnp.dot(e, vcat[pl.ds(r0, WIN), cols], preferred_element_type=F32) / l
            outs.append(jnp.where(lane < DH_B, o[:QPAIR], o[QPAIR:]))
        o_ref[0, pl.ds(r0, QPAIR), :] = jnp.concatenate(outs, axis=1).astype(o_ref.dtype)
        return carry

    @pl.when(i == 0)
    def _():
        lax.fori_loop(0, BAND_TQ // QPAIR, functools.partial(pair_block, first_block=True), 0)

    @pl.when(i > 0)
    def _():
        lax.fori_loop(0, BAND_TQ // QPAIR, functools.partial(pair_block, first_block=False), 0)


def _band_prompt(q3, kv3, bias):
    b, t, _ = q3.shape
    assert t % BAND_TQ == 0
    w = H_B * DH_B
    cq, ck, cv = 1, 0, 1
    prev = lambda col: pl.BlockSpec((1, BAND_TQ, w), lambda bi, i: (bi, jnp.maximum(i - 1, 0), col))
    cur = lambda col: pl.BlockSpec((1, BAND_TQ, w), lambda bi, i: (bi, i, col))
    return pl.pallas_call(
        _band_prompt_kernel,
        out_shape=jax.ShapeDtypeStruct((b, t, w), BF16),
        grid=(b, t // BAND_TQ),
        in_specs=[
            pl.BlockSpec((H_B // 2, 2 * QPAIR, WIN), lambda bi, i: (0, 0, 0)),
            cur(cq), prev(ck), cur(ck), prev(cv), cur(cv),
        ],
        out_specs=pl.BlockSpec((1, BAND_TQ, w), lambda bi, i: (bi, i, 0)),
        scratch_shapes=[pltpu.VMEM((2 * BAND_TQ, w), BF16), pltpu.VMEM((2 * BAND_TQ, w), BF16)],
        compiler_params=_cparams(("parallel", "arbitrary")),
        name="band_prompt",
    )(bias.reshape(H_B // 2, 2 * QPAIR, WIN), q3, kv3, kv3, kv3, kv3)


def _band_sample_kernel(bc_ref, bn_ref, q_ref, kc_ref, vc_ref, kn_ref, vn_ref, o_ref, *, past):
    t = q_ref.shape[1]
    nband = kc_ref.shape[1]
    lane = lax.broadcasted_iota(jnp.int32, (t, LANES), 1)
    qpos_c = past + lax.broadcasted_iota(jnp.int32, (t, nband), 0)
    kpos_c = past - nband + lax.broadcasted_iota(jnp.int32, (t, nband), 1)
    qpos_n = past + lax.broadcasted_iota(jnp.int32, (t, t), 0)
    kpos_n = past + lax.broadcasted_iota(jnp.int32, (t, t), 1)

    def allowed(qpos, kpos):
        qc, kc = qpos // CHUNK, kpos // CHUNK
        return (kpos >= 0) & (kc <= qc) & (kc >= qc - BAND_CHUNKS)

    ok_c = allowed(qpos_c, kpos_c)
    ok_n = allowed(qpos_n, kpos_n)
    for hp in range(H_B // 2):
        cols = slice(hp * LANES, (hp + 1) * LANES)
        halves = []
        for sub, qm in enumerate(_split_halves(q_ref[0, :, cols])):
            hd = 2 * hp + sub
            pieces = [(kc_ref[0, :, cols].astype(BF16), vc_ref[0, :, cols].astype(BF16),
                       jnp.where(ok_c, bc_ref[hd], NEG)),
                      (kn_ref[0, :, cols].astype(BF16), vn_ref[0, :, cols].astype(BF16),
                       jnp.where(ok_n, bn_ref[hd], NEG))]
            halves.append(_pieces_attention(qm, pieces))
        o_ref[0, :, cols] = jnp.where(lane < DH_B, halves[0], halves[1]).astype(o_ref.dtype)


def _band_sample(q3, kv3, kc, vc, bias, past):
    b, t, _ = q3.shape
    nband = kc.shape[1]
    assert nband == BAND and t <= CHUNK
    w = H_B * DH_B
    bias_c = bias[:, :t, :nband]
    bias_n = bias[:, :t, nband:nband + t]
    blk = lambda col: pl.BlockSpec((1, t, w), lambda bi: (bi, 0, col))
    cache = pl.BlockSpec((1, nband, w), lambda bi: (bi, 0, 0))
    return pl.pallas_call(
        functools.partial(_band_sample_kernel, past=past),
        out_shape=jax.ShapeDtypeStruct((b, t, w), BF16),
        grid=(b,),
        in_specs=[
            pl.BlockSpec((H_B, t, nband), lambda bi: (0, 0, 0)),
            pl.BlockSpec((H_B, t, t), lambda bi: (0, 0, 0)),
            blk(1), cache, cache, blk(0), blk(1),
        ],
        out_specs=pl.BlockSpec((1, t, w), lambda bi: (bi, 0, 0)),
        compiler_params=_cparams(("parallel",)),
        name="band_sample",
    )(bias_c, bias_n, q3, kc, vc, kv3, kv3)


CONV_PAD = 8
MLSTM_L = 256


def _mlstm_kernel(cqk_ref, cv_ref, co_ref, cif_ref, conv0_ref, cw_ref, cb_ref, gb_ref, hg_ref,
                  c0_ref, n0_ref, m0_ref, hs_ref, c_ref, n_ref, m_ref, xext, *, L):
    @pl.when(pl.program_id(1) == 0)
    def _():
        xext[0:CONV_PAD, :] = conv0_ref[0]
        c_ref[...] = c0_ref[...]
        n_ref[...] = n0_ref[...]
        m_ref[...] = m0_ref[...]

    xext[CONV_PAD:CONV_PAD + L, :] = cqk_ref[0]
    base = CONV_PAD - (CONV_W - 1)
    u = 0.0
    for j in range(CONV_W):
        u = u + xext[base + j:base + j + L, :] * cw_ref[j:j + 1, :]
    u = cb_ref[...] + u
    u = u * jax.nn.sigmoid(u)
    xext[0:CONV_PAD, :] = xext[L:L + CONV_PAD, :]

    z = cif_ref[0] + gb_ref[...]
    lane = lax.broadcasted_iota(jnp.int32, (L, LANES), 1)
    lf = jnp.minimum(z, 0.0) - jnp.log1p(jnp.exp(-jnp.abs(z)))
    tr = lax.broadcasted_iota(jnp.int32, (L, L), 0)
    tc = lax.broadcasted_iota(jnp.int32, (L, L), 1)
    causal = tr >= tc
    b_all = jnp.dot(causal.astype(F32), lf, preferred_element_type=F32,
                    precision=lax.Precision.HIGHEST)
    sr = lax.broadcasted_iota(jnp.int32, (8, LANES), 0)
    sc = lax.broadcasted_iota(jnp.int32, (8, LANES), 1)
    sel = jnp.where(sr < H_C, jnp.where(sc == sr, 1.0, jnp.where(sc == sr + H_C, -1.0, 0.0)), 0.0)
    rt = lax.dot_general(sel, jnp.where(lane < H_C, z, b_all), NT_DIMS,
                         preferred_element_type=F32, precision=lax.Precision.HIGHEST)
    er = lax.broadcasted_iota(jnp.int32, (DH_C, DH_C), 0)
    ec = lax.broadcasted_iota(jnp.int32, (DH_C, DH_C), 1)
    eye = jnp.where(er == ec, 1.0, 0.0).astype(BF16)

    for h in range(H_C):
        cols = slice(h * DH_C, (h + 1) * DH_C)
        q = u[:, h * DH_C:(h + 1) * DH_C]
        k = u[:, D_C + h * DH_C:D_C + (h + 1) * DH_C] * (DH_C ** -0.5)
        v = cv_ref[0, :, cols]
        qb, kb_, vb_ = q.astype(BF16), k.astype(BF16), v.astype(BF16)
        b_col = b_all[:, H_C + h:H_C + h + 1]
        li_col = z[:, h:h + 1]
        m_prev = m_ref[0, h:h + 1, 0:1]
        cs = c_ref[0, h]
        ns = n_ref[0, h:h + 1, :]

        dmat = jnp.where(causal, b_col + rt[h:h + 1, :], NEG)
        inter = b_col + m_prev
        m_t = jnp.maximum(inter, jnp.max(dmat, axis=-1, keepdims=True))
        w_intra = jnp.exp(dmat - m_t)
        w_inter = jnp.exp(inter - m_t)
        a = w_intra * _nt(qb, kb_)
        num = (jnp.dot(a.astype(BF16), vb_, preferred_element_type=F32)
               + w_inter * _nt(qb, cs.astype(BF16)))
        den = jnp.sum(a, axis=-1, keepdims=True) + w_inter * jnp.sum(q * ns, axis=-1, keepdims=True)
        hh = num / jnp.maximum(jnp.abs(den), jnp.exp(-m_t))

        b_last = b_col[L - 1:L, :]
        g = b_last - b_col + li_col
        m_new = jnp.maximum(b_last + m_prev, jnp.max(g, axis=0, keepdims=True))
        ws = jnp.exp(g - m_new)
        decay = jnp.exp(b_last + m_prev - m_new)
        vwt = _nt(eye, (ws * v).astype(BF16)).astype(BF16)
        c_ref[0, h] = decay * cs + jnp.dot(vwt, kb_, preferred_element_type=F32)
        n_ref[0, h:h + 1, :] = decay * ns + jnp.sum(ws * k, axis=0, keepdims=True)
        m_ref[0, h:h + 1, :] = jnp.broadcast_to(m_new, (1, LANES))

        gate = jax.nn.sigmoid(co_ref[0, :, cols])
        hs_ref[0, :, cols] = (_rms(hh, hg_ref[...]) * gate).astype(hs_ref.dtype)


def _mlstm(cqk3, cvo3, cif3, conv0, conv_w, conv_b, gate_b, head_g, c0, n0, m0, L):
    b, t, _ = cqk3.shape
    nc = t // L
    blk = lambda col, w: pl.BlockSpec((1, L, w), lambda bi, c: (bi, c, col))
    const = lambda shape: pl.BlockSpec(shape, lambda bi, c: (0,) * len(shape))
    per_b = lambda shape: pl.BlockSpec((1,) + shape, lambda bi, c: (bi,) + (0,) * len(shape))
    return pl.pallas_call(
        functools.partial(_mlstm_kernel, L=L),
        out_shape=(jax.ShapeDtypeStruct((b, t, D_C), BF16),
                   jax.ShapeDtypeStruct((b, H_C, DH_C, DH_C), F32),
                   jax.ShapeDtypeStruct((b, H_C, DH_C), F32),
                   jax.ShapeDtypeStruct((b, H_C, LANES), F32)),
        grid=(b, nc),
        in_specs=[
            blk(0, 2 * D_C), blk(0, D_C), blk(1, D_C), blk(0, LANES),
            per_b((CONV_PAD, 2 * D_C)), const((CONV_W, 2 * D_C)), const((1, 2 * D_C)),
            const((1, LANES)), const((1, DH_C)),
            per_b((H_C, DH_C, DH_C)), per_b((H_C, DH_C)), per_b((H_C, LANES)),
        ],
        out_specs=(pl.BlockSpec((1, L, D_C), lambda bi, c: (bi, c, 0)),
                   per_b((H_C, DH_C, DH_C)), per_b((H_C, DH_C)), per_b((H_C, LANES))),
        scratch_shapes=[pltpu.VMEM((CONV_PAD + L, 2 * D_C), F32)],
        compiler_params=_cparams(("parallel", "arbitrary")),
        name="mlstm",
    )(cqk3, cvo3, cvo3, cif3, conv0, conv_w, conv_b, gate_b, head_g, c0, n0, m0)


def _mix_kernel(oa_ref, ob_ref, oc_ref, ga_ref, gb_ref, gc_ref, x_ref, wa_ref, wb_ref, wc_ref, wo_ref, o_ref):
    mixed = (ga_ref[...] * jnp.dot(oa_ref[...], wa_ref[...], preferred_element_type=F32)
             + gb_ref[...] * jnp.dot(ob_ref[...], wb_ref[...], preferred_element_type=F32)
             + gc_ref[...] * jnp.dot(oc_ref[...], wc_ref[...], preferred_element_type=F32))
    o_ref[...] = x_ref[...] + jnp.dot(mixed.astype(BF16), wo_ref[...], preferred_element_type=F32)


def _mix(oa, ob, oc, gates, x, l, wa, wb, wc, wo):
    m, d = x.shape
    tm = min(m, 512)
    row = lambda w, col=0: pl.BlockSpec((tm, w), lambda i: (i, col))
    full = lambda a: pl.BlockSpec((None,) + a.shape[1:], lambda i: (l, 0, 0))
    g0 = 0
    return pl.pallas_call(
        _mix_kernel,
        out_shape=jax.ShapeDtypeStruct((m, d), F32),
        grid=(m // tm,),
        in_specs=[row(oa.shape[1]), row(ob.shape[1]), row(oc.shape[1]),
                  row(d, g0), row(d, g0 + 1), row(d, g0 + 2), row(d),
                  full(wa), full(wb), full(wc), full(wo)],
        out_specs=row(d),
        compiler_params=_cparams(("parallel",)),
        name="mix",
    )(oa, ob, oc, gates, gates, gates, x, wa, wb, wc, wo)


def _cross_kernel(x_ref, g_ref, wq_ref, mk_ref, mv_ref, wo_ref, o_ref):
    x = x_ref[0]
    h = _rms(x, g_ref[...]).astype(BF16)
    q = (jnp.dot(h, wq_ref[...], preferred_element_type=F32) * (DH_M ** -0.5 * LOG2E)).astype(BF16)
    outs = []
    for hd in range(H_M):
        cols = slice(hd * DH_M, (hd + 1) * DH_M)
        s = _nt(q[:, cols], mk_ref[0, :, cols].astype(BF16))
        m = jnp.max(s, axis=-1, keepdims=True)
        e = jnp.exp2(s - m)
        l = jnp.sum(e, axis=-1, keepdims=True)
        o = jnp.dot(e.astype(BF16), mv_ref[0, :, cols].astype(BF16), preferred_element_type=F32) / l
        outs.append(o.astype(BF16))
    o = jnp.concatenate(outs, axis=1)
    o_ref[0] = x + jnp.dot(o, wo_ref[...], preferred_element_type=F32)


def _cross(x3, g, l, wq, mk, mv, wo):
    b, t, d = x3.shape
    tm = min(t, 512)
    nm = mk.shape[1]
    full = lambda a: pl.BlockSpec((None,) + a.shape[1:], lambda bi, i: (l, 0, 0))
    return pl.pallas_call(
        _cross_kernel,
        out_shape=jax.ShapeDtypeStruct((b, t, d), F32),
        grid=(b, t // tm),
        in_specs=[pl.BlockSpec((1, tm, d), lambda bi, i: (bi, i, 0)),
                  pl.BlockSpec(g.shape, lambda bi, i: (0, 0)), full(wq),
                  pl.BlockSpec((1, nm, d), lambda bi, i: (bi, 0, 0)),
                  pl.BlockSpec((1, nm, d), lambda bi, i: (bi, 0, 0)), full(wo)],
        out_specs=pl.BlockSpec((1, tm, d), lambda bi, i: (bi, i, 0)),
        compiler_params=_cparams(("parallel", "parallel")),
        name="cross",
    )(x3, g, wq, mk, mv, wo)


def _ffn_kernel(x_ref, g_ref, wg_ref, wu_ref, wd_ref, gf_ref, o_ref, h_scr, acc, *, final_norm):
    j = pl.program_id(1)

    @pl.when(j == 0)
    def _():
        h_scr[...] = _rms(x_ref[...], g_ref[...]).astype(BF16)
        acc[...] = jnp.zeros(acc.shape, F32)

    h = h_scr[...]
    gate = jnp.dot(h, wg_ref[...], preferred_element_type=F32)
    up = jnp.dot(h, wu_ref[...], preferred_element_type=F32)
    a = (gate * jax.nn.sigmoid(gate) * up).astype(BF16)
    acc[...] += jnp.dot(a, wd_ref[...], preferred_element_type=F32)

    @pl.when(j == pl.num_programs(1) - 1)
    def _():
        y = x_ref[...] + acc[...]
        if final_norm:
            y = _rms(y, gf_ref[...])
        o_ref[...] = y


FFN_TF = 1408
FFN_TM = 512


def _ffn(x, g, l, wg, wu, wd, g_final, final_norm):
    m, d = x.shape
    tm = min(m, FFN_TM)
    return pl.pallas_call(
        functools.partial(_ffn_kernel, final_norm=final_norm),
        out_shape=jax.ShapeDtypeStruct((m, d), F32),
        grid=(m // tm, D_FF // FFN_TF),
        in_specs=[
            pl.BlockSpec((tm, d), lambda i, j: (i, 0)),
            pl.BlockSpec((1, d), lambda i, j: (0, 0)),
            pl.BlockSpec((None, d, FFN_TF), lambda i, j: (l, 0, j)),
            pl.BlockSpec((None, d, FFN_TF), lambda i, j: (l, 0, j)),
            pl.BlockSpec((None, FFN_TF, d), lambda i, j: (l, j, 0)),
            pl.BlockSpec((1, d), lambda i, j: (0, 0)),
        ],
        out_specs=pl.BlockSpec((tm, d), lambda i, j: (i, 0)),
        scratch_shapes=[pltpu.VMEM((tm, d), BF16), pltpu.VMEM((tm, d), F32)],
        compiler_params=_cparams(("parallel", "arbitrary")),
        name="ffn",
    )(x, g, wg, wu, wd, g_final)


def _layer(x3, l, w, mem_k, mem_v, cache, final_norm, akv_all):
    b, t, d = x3.shape
    x2 = x3.reshape(b * t, d)
    row = lambda a: a.reshape(1, -1)
    lam_init = 0.8 - 0.6 * math.exp(-0.3 * l)

    gates, cqk, cif = _in_proj_a(x2, row(w['g_mix']), l, w['w_in'], w['w_if'])
    q, ak_all, av_all, akv, bkv, cvo = _in_proj_b(x2, row(w['g_mix']), w['w_in'], l, *akv_all, (b, t))
    q3 = q.reshape(b, t, SEG)
    bkv3 = bkv.reshape(b, t, SEG)
    cqk3 = cqk.reshape(b, t, 2 * D_C)
    keep = t if cache is not None else min(BAND, t)
    b_k = bkv3[:, t - keep:, :HALF].reshape(b, keep, H_B, DH_B)
    b_v = bkv3[:, t - keep:, HALF:].reshape(b, keep, H_B, DH_B)

    gate_b = jnp.concatenate([w['b_i'], w['b_f'], jnp.zeros((LANES - 2 * H_C,), F32)]).reshape(1, LANES)
    if cache is None:
        oa = _diff_prompt(q3, akv.reshape(b, t, SEG), w['lqk'], row(w['a_head_g']), lam_init)
        ob = _band_prompt(q3, bkv3, w['relbias'])
        conv0 = jnp.zeros((b, CONV_PAD, 2 * D_C), F32)
        c0 = jnp.zeros((b, H_C, DH_C, DH_C), F32)
        n0 = jnp.zeros((b, H_C, DH_C), F32)
        m0 = jnp.zeros((b, H_C, LANES), F32)
        L = min(t, MLSTM_L)
    else:
        past = cache['a_k'].shape[2]
        ak3 = ak_all.reshape(DEPTH * b, t, HALF)
        av3 = av_all.reshape(DEPTH * b, t, HALF)
        oa = _diff_sample(q3, ak3, av3, l, cache['a_k'], cache['a_v'],
                          w['lqk'], row(w['a_head_g']), lam_init)
        nband = cache['b_k'].shape[1]
        ob = _band_sample(q3, bkv3, cache['b_k'].reshape(b, nband, -1), cache['b_v'].reshape(b, nband, -1),
                          w['relbias'], past)
        conv0 = jnp.pad(cache['conv'], ((0, 0), (CONV_PAD - (CONV_W - 1), 0), (0, 0)))
        c0, n0 = cache['C'], cache['n']
        m0 = jnp.broadcast_to(cache['m'][:, :, None], (b, H_C, LANES))
        L = t
    oc, c_new, n_new, m_new = _mlstm(cqk3, cvo.reshape(b, t, SEG), cif.reshape(b, t, LANES), conv0,
                                     w['conv_w'], row(w['conv_b']), gate_b, row(w['c_head_g']), c0, n0, m0, L)
    assert t >= CONV_W - 1
    conv_new = cqk3[:, t - (CONV_W - 1):]

    x2 = _mix(oa.reshape(b * t, -1), ob.reshape(b * t, -1), oc.reshape(b * t, -1), gates, x2, l,
              w['w_up_a'], w['w_up_b'], w['w_up_c'], w['w_o'])
    x3 = _cross(x2.reshape(b, t, d), row(w['g_cross']), l, w['w_mq'], mem_k, mem_v, w['w_mo'])
    x2 = _ffn(x3.reshape(b * t, d), row(w['g_ffn']), l, w['w_ff_g'], w['w_ff_u'], w['w_ff_d'],
              row(w['g_final']), final_norm)
    return x2.reshape(b, t, d), (ak_all, av_all), (b_k, b_v, c_new, n_new, m_new[:, :, 0], conv_new)


def kernel(x_prompt, x_sample, cache_a_k, cache_a_v, cache_b_k, cache_b_v, state_c_C, state_c_n, state_c_m, state_c_conv, cache_mem_k, cache_mem_v, mem_prompt, g_mix, w_in, a_lq1, a_lk1, a_lq2, a_lk2, a_head_g, b_rel, c_conv_w, c_conv_b, c_b_i, c_b_f, c_head_g, w_up_a, w_up_b, w_up_c, w_o, g_cross, w_mq, w_mk, w_mv, w_mo, g_ffn, w_ff_g, w_ff_u, w_ff_d, g_final):
    xp, xs = x_prompt, x_sample
    bp = x_prompt.shape[0]
    n_mem = mem_prompt.shape[1]
    mem2 = mem_prompt.reshape(bp * n_mem, D_MODEL)
    new_p = [[] for _ in range(8)]
    new_s = [[] for _ in range(6)]
    akv_p = akv_s = (None, None)
    bf = lambda a: a.astype(BF16)
    wp, wif = _prep_w_in(w_in)
    wts = dict(w_up_a=bf(w_up_a), w_up_b=bf(w_up_b), w_up_c=bf(w_up_c), w_o=bf(w_o), w_mq=bf(w_mq),
               w_mo=bf(w_mo), w_ff_g=bf(w_ff_g), w_ff_u=bf(w_ff_u), w_ff_d=bf(w_ff_d))
    w_mk_b, w_mv_b = bf(w_mk), bf(w_mv)
    for l in range(DEPTH):
        w = dict(g_mix=g_mix[l], w_in=wp, w_if=wif,
                 lqk=jnp.stack([a_lq1[l], a_lk1[l], a_lq2[l], a_lk2[l]]),
                 a_head_g=a_head_g[l], relbias=_relbias(b_rel[l]), conv_w=c_conv_w[l], conv_b=c_conv_b[l],
                 b_i=c_b_i[l], b_f=c_b_f[l], c_head_g=c_head_g[l],
                 g_cross=g_cross[l], g_ffn=g_ffn[l], g_final=g_final, **wts)
        mk2 = _mem_proj(mem2, l, w_mk_b).reshape(bp, n_mem, D_MODEL)
        mv2 = _mem_proj(mem2, l, w_mv_b).reshape(bp, n_mem, D_MODEL)
        last = l == DEPTH - 1
        xp, akv_p, st_p = _layer(xp, l, w, mk2, mv2, None, last, akv_p)
        cache = dict(a_k=cache_a_k, a_v=cache_a_v, b_k=cache_b_k[l], b_v=cache_b_v[l],
                     C=state_c_C[l], n=state_c_n[l], m=state_c_m[l], conv=state_c_conv[l])
        bs = xs.shape[0]
        xs, akv_s, st_s = _layer(xs, l, w, cache_mem_k[l].reshape(bs, n_mem, D_MODEL),
                                 cache_mem_v[l].reshape(bs, n_mem, D_MODEL), cache, last, akv_s)
        mk4 = mk2.reshape(bp, n_mem, H_M, DH_M)
        mv4 = mv2.reshape(bp, n_mem, H_M, DH_M)
        for lst, a in zip(new_p, st_p + (mk4, mv4)):
            lst.append(a)
        for lst, a in zip(new_s, st_s):
            lst.append(a)
    heads = lambda a, b, t: a.reshape(DEPTH, b, t, H_A, DV_A)
    outs_p = [heads(a, bp, x_prompt.shape[1]) for a in akv_p] + [jnp.stack(a, 0) for a in new_p]
    outs_s = [heads(a, xs.shape[0], xs.shape[1]) for a in akv_s] + [jnp.stack(a, 0) for a in new_s]
    return (xp, xs) + tuple(outs_p) + tuple(outs_s)
```

```python
import functools
import math

import jax
import jax.numpy as jnp
from jax import lax
from jax.experimental import pallas as pl
from jax.experimental.pallas import tpu as pltpu

F32 = jnp.float32
BF16 = jnp.bfloat16

D_MODEL = 1024
DEPTH = 2
CHUNK = 64
EPS = 1e-6
NEG = -1e30
LOG2E = math.log2(math.e)
H_A = 4
DH_A = 64
DV_A = 2 * DH_A
H_B = 8
DH_B = 64
BAND_CHUNKS = 8
BAND = BAND_CHUNKS * CHUNK
REL_CLIP = 128
H_C = 4
DH_C = 128
D_C = H_C * DH_C
CONV_W = 4
H_M = 4
DH_M = D_MODEL // H_M
D_FF = -(-8 * D_MODEL // (3 * 256)) * 256

LANES = 128
VMEM_LIMIT = 48 * 1024 * 1024

SEG = 1024
HALF = SEG // 2
GATE_TILES = 3
N_TILES_A = GATE_TILES + 1
N_TILES_B = 4
IN_SIZES = (H_A * DH_A,) * 4 + (H_A * DV_A,) + (H_B * DH_B,) * 3 + (2 * D_C, D_C, D_C, 2 * H_C, 3 * D_MODEL)

ALIBI_SLOPES = tuple(2.0 ** (-8.0 * (i + 1) / H_A) for i in range(H_A))

NT_DIMS = (((1,), (1,)), ((), ()))


def _cparams(sem):
    return pltpu.CompilerParams(dimension_semantics=sem, vmem_limit_bytes=VMEM_LIMIT)


def _rms(x, g):
    ms = jnp.mean(x * x, axis=-1, keepdims=True)
    return x * lax.rsqrt(ms + EPS) * g


def _nt(a, b):
    return lax.dot_general(a, b, NT_DIMS, preferred_element_type=F32)


def _prep_w_in_kernel(w_ref, o_ref, oif_ref):
    offs = [0]
    for size in IN_SIZES:
        offs.append(offs[-1] + size)
    seg = lambda i: w_ref[:, offs[i]:offs[i + 1]]
    a_q1, a_q2, a_k1, a_k2, a_v, b_q, b_k, b_v, c_qk, c_v, c_o = (seg(i) for i in range(11))
    g = seg(12)

    def put(col, val):
        o_ref[:, col:col + val.shape[1]] = val.astype(BF16)

    put(0, g)
    put(GATE_TILES * SEG, c_qk)
    for h in range(H_A):
        hs = slice(h * DH_A, (h + 1) * DH_A)
        base = h * 2 * DH_A
        put(4 * SEG + base, a_q1[:, hs] * (DH_A ** -0.5))
        put(4 * SEG + base + DH_A, a_q2[:, hs] * (DH_A ** -0.5))
        put(5 * SEG + base, a_k1[:, hs])
        put(5 * SEG + base + DH_A, a_k2[:, hs])
    put(4 * SEG + HALF, b_q * (DH_B ** -0.5))
    put(5 * SEG + HALF, a_v)
    put(6 * SEG, b_k)
    put(6 * SEG + HALF, b_v)
    put(7 * SEG, c_v)
    put(7 * SEG + HALF, c_o)
    tile = w_ref[:, offs[11]:offs[11] + LANES]
    lane = lax.broadcasted_iota(jnp.int32, tile.shape, 1)
    oif_ref[...] = jnp.where(lane < 2 * H_C, tile, 0.0).astype(BF16)


def _prep_w_in(w):
    depth, k, n = w.shape
    tk = 256
    n_out = (N_TILES_A + N_TILES_B) * SEG
    return pl.pallas_call(
        _prep_w_in_kernel,
        out_shape=(jax.ShapeDtypeStruct((depth, k, n_out), BF16),
                   jax.ShapeDtypeStruct((depth, k, LANES), BF16)),
        grid=(depth, k // tk),
        in_specs=[pl.BlockSpec((None, tk, n), lambda l, i: (l, i, 0))],
        out_specs=(pl.BlockSpec((None, tk, n_out), lambda l, i: (l, i, 0)),
                   pl.BlockSpec((None, tk, LANES), lambda l, i: (l, i, 0))),
        compiler_params=_cparams(("parallel", "parallel")),
        name="prep_w_in",
    )(w)


def _norm_to_scratch(x_ref, g_ref, h_scr):
    h_scr[...] = _rms(x_ref[...], g_ref[...]).astype(BF16)


def _in_proj_a_kernel(x_ref, g_ref, w_ref, wif_ref, og_ref, ocqk_ref, ocif_ref, h_scr):
    j = pl.program_id(1)

    @pl.when(j == 0)
    def _():
        _norm_to_scratch(x_ref, g_ref, h_scr)
        ocif_ref[...] = jnp.dot(h_scr[...], wif_ref[...], preferred_element_type=F32)

    tile = lambda: jnp.dot(h_scr[...], w_ref[...], preferred_element_type=F32)

    @pl.when(j < GATE_TILES)
    def _():
        og_ref[...] = jax.nn.sigmoid(tile()).astype(og_ref.dtype)

    @pl.when(j == GATE_TILES)
    def _():
        ocqk_ref[...] = tile()


def _in_proj_a(x, g, l, wp, wif):
    m, k = x.shape
    tm = min(m, 1024)
    return pl.pallas_call(
        _in_proj_a_kernel,
        out_shape=(jax.ShapeDtypeStruct((m, GATE_TILES * SEG), BF16),
                   jax.ShapeDtypeStruct((m, SEG), F32),
                   jax.ShapeDtypeStruct((m, LANES), F32)),
        grid=(m // tm, N_TILES_A),
        in_specs=[
            pl.BlockSpec((tm, k), lambda i, j: (i, 0)),
            pl.BlockSpec((1, k), lambda i, j: (0, 0)),
            pl.BlockSpec((None, k, SEG), lambda i, j: (l, 0, j)),
            pl.BlockSpec((None, k, LANES), lambda i, j: (l, 0, 0)),
        ],
        out_specs=(pl.BlockSpec((tm, SEG), lambda i, j: (i, jnp.minimum(j, GATE_TILES - 1))),
                   pl.BlockSpec((tm, SEG), lambda i, j: (i, 0)),
                   pl.BlockSpec((tm, LANES), lambda i, j: (i, 0))),
        scratch_shapes=[pltpu.VMEM((tm, k), BF16)],
        compiler_params=_cparams(("parallel", "arbitrary")),
        name="in_proj_a",
    )(x, g, wp, wif)


def _in_proj_b_kernel(*refs, aliased, heads5d):
    x_ref, g_ref, w_ref = refs[:3]
    oq_ref, oak_ref, oav_ref, oakv_ref, obkv_ref, ocvo_ref, h_scr = refs[5 if aliased else 3:]
    j = pl.program_id(1)

    @pl.when(j == 0)
    def _():
        _norm_to_scratch(x_ref, g_ref, h_scr)

    tile = lambda: jnp.dot(h_scr[...], w_ref[...], preferred_element_type=F32)

    @pl.when(j == 0)
    def _():
        oq_ref[...] = (tile() * LOG2E).astype(oq_ref.dtype)

    @pl.when(j == 1)
    def _():
        for out_ref, off in ((oak_ref, 0), (oav_ref, HALF)):
            a = jnp.dot(h_scr[...], w_ref[:, off:off + HALF], preferred_element_type=F32)
            oakv_ref[:, off:off + HALF] = a.astype(oakv_ref.dtype)
            if heads5d:
                for hd in range(H_A):
                    out_ref[0, 0, :, hd, :] = a[:, hd * DV_A:(hd + 1) * DV_A]
            else:
                out_ref[0] = a

    @pl.when(j == 2)
    def _():
        obkv_ref[...] = tile()

    @pl.when(j == 3)
    def _():
        ocvo_ref[...] = tile()


def _in_proj_b(x, g, wp, l, ak_all, av_all, bt):
    m, k = x.shape
    tm = min(m, 1024)
    aliased = ak_all is not None
    b, t = bt
    heads5d = t % tm == 0
    if heads5d:
        stacked = jax.ShapeDtypeStruct((DEPTH, b, t, H_A, DV_A), F32)
        layer_row = pl.BlockSpec((1, 1, tm, H_A, DV_A),
                                 lambda i, j: (l, i // (t // tm), i % (t // tm), 0, 0))
    else:
        stacked = jax.ShapeDtypeStruct((DEPTH, m, HALF), F32)
        layer_row = pl.BlockSpec((1, tm, HALF), lambda i, j: (l, i, 0))
    in_specs = [
        pl.BlockSpec((tm, k), lambda i, j: (i, 0)),
        pl.BlockSpec((1, k), lambda i, j: (0, 0)),
        pl.BlockSpec((None, k, SEG), lambda i, j: (l, 0, N_TILES_A + j)),
    ]
    args = [x, g, wp]
    if aliased:
        in_specs += [pl.BlockSpec(memory_space=pl.ANY)] * 2
        args += [ak_all, av_all]
    row = lambda w: pl.BlockSpec((tm, w), lambda i, j: (i, 0))
    return pl.pallas_call(
        functools.partial(_in_proj_b_kernel, aliased=aliased, heads5d=heads5d),
        out_shape=(jax.ShapeDtypeStruct((m, SEG), BF16), stacked, stacked, jax.ShapeDtypeStruct((m, SEG), BF16),
                   jax.ShapeDtypeStruct((m, SEG), F32), jax.ShapeDtypeStruct((m, SEG), F32)),
        grid=(m // tm, N_TILES_B),
        in_specs=in_specs,
        out_specs=(row(SEG), layer_row, layer_row, row(SEG), row(SEG), row(SEG)),
        scratch_shapes=[pltpu.VMEM((tm, k), BF16)],
        input_output_aliases={3: 1, 4: 2} if aliased else {},
        compiler_params=_cparams(("parallel", "arbitrary")),
        name="in_proj_b",
    )(*args)


def _mem_proj_kernel(x_ref, w_ref, o_ref):
    o_ref[...] = jnp.dot(x_ref[...].astype(BF16), w_ref[...], preferred_element_type=F32)


def _mem_proj(x, l, w):
    m, k = x.shape
    n = w.shape[2]
    tm = min(m, 1024)
    return pl.pallas_call(
        _mem_proj_kernel,
        out_shape=jax.ShapeDtypeStruct((m, n), F32),
        grid=(m // tm,),
        in_specs=[pl.BlockSpec((tm, k), lambda i: (i, 0)), pl.BlockSpec((None, k, n), lambda i: (l, 0, 0))],
        out_specs=pl.BlockSpec((tm, n), lambda i: (i, 0)),
        compiler_params=_cparams(("parallel",)),
        name="mem_proj",
    )(x, w)


def _online_updates(jobs):
    staged = []
    for s, _, (acc, m, l) in jobs:
        slabs = [s[:, j * LANES:(j + 1) * LANES] for j in range(s.shape[1] // LANES)]
        m_old = m[...]
        m_new = jnp.maximum(m_old, jnp.max(functools.reduce(jnp.maximum, slabs), axis=-1, keepdims=True))
        alpha = jnp.exp2(m_old - m_new)
        ps = [jnp.exp2(c - m_new) for c in slabs]
        l[...] = alpha * l[...] + jnp.sum(functools.reduce(jnp.add, ps), axis=-1, keepdims=True)
        m[...] = m_new
        staged.append((alpha, jnp.concatenate(ps, axis=1).astype(BF16)))
    for (alpha, p), (_, v, (acc, m, l)) in zip(staged, jobs):
        acc[...] = alpha * acc[...] + jnp.dot(p, v, preferred_element_type=F32)


def _split_halves(q):
    lane = lax.broadcasted_iota(jnp.int32, q.shape, 1)
    lo = jnp.where(lane < DH_A, q, 0.0).astype(BF16)
    hi = jnp.where(lane >= DH_A, q, 0.0).astype(BF16)
    return lo, hi


def _lambda(lqk, lam_init):
    e1 = jnp.exp(jnp.sum(lqk[0:1, :] * lqk[1:2, :], axis=-1, keepdims=True))
    e2 = jnp.exp(jnp.sum(lqk[2:3, :] * lqk[3:4, :], axis=-1, keepdims=True))
    return e1 - e2 + lam_init


def _diff_prompt_kernel(slopes_ref, lqk_ref, hg_ref, q_ref, k_ref, v_ref, o_ref,
                        acc1, acc2, m1, l1, m2, l2, *, tq, lam_init):
    h = pl.program_id(1)
    qi = pl.program_id(2)
    slope = slopes_ref[h] * LOG2E
    kb, vb = k_ref.at[0], v_ref.at[0]

    qa, qb = _split_halves(q_ref[0])
    states = ((acc1, m1, l1), (acc2, m2, l2))
    for acc, m, l in states:
        m[...] = jnp.full(m.shape, NEG, F32)
        l[...] = jnp.zeros(l.shape, F32)
        acc[...] = jnp.zeros(acc.shape, F32)

    def full_tiles(k_starts):
        jobs = []
        for k0 in k_starts:
            k = kb[pl.ds(k0, tq), :]
            v = vb[pl.ds(k0, tq), :]
            kpos = k0 + lax.broadcasted_iota(jnp.int32, (1, tq), 1)
            bias = slope * kpos.astype(F32)
            jobs += [(_nt(qa, k) + bias, v, states[0]), (_nt(qb, k) + bias, v, states[1])]
        _online_updates(jobs)

    def tile_pair(it, carry):
        k0 = pl.multiple_of(it * (2 * tq), 2 * tq)
        full_tiles([k0, pl.multiple_of(k0 + tq, tq)])
        return carry

    lax.fori_loop(0, qi // 2, tile_pair, 0)

    @pl.when(qi % 2 == 1)
    def _():
        full_tiles([pl.multiple_of((qi - 1) * tq, tq)])

    q0 = pl.multiple_of(qi * tq, tq)
    k = kb[pl.ds(q0, tq), :]
    v = vb[pl.ds(q0, tq), :]
    r = lax.broadcasted_iota(jnp.int32, (tq, tq), 0)
    c = lax.broadcasted_iota(jnp.int32, (tq, tq), 1)
    allowed = (c // CHUNK) <= (r // CHUNK)
    bias = slope * (q0 + r - jnp.abs(r - c)).astype(F32)
    _online_updates([(jnp.where(allowed, _nt(qa, k) + bias, NEG), v, states[0]),
                     (jnp.where(allowed, _nt(qb, k) + bias, NEG), v, states[1])])

    lam = _lambda(lqk_ref[...], lam_init)
    o = acc1[...] / l1[...] - lam * (acc2[...] / l2[...])
    o_ref[0] = (_rms(o, hg_ref[...]) * (1.0 - lam_init)).astype(o_ref.dtype)


def _diff_prompt(q3, kv3, lqk, head_g, lam_init):
    b, t, _ = q3.shape
    tq = min(t, 512)
    slopes = jnp.asarray(ALIBI_SLOPES, F32)
    return pl.pallas_call(
        functools.partial(_diff_prompt_kernel, tq=tq, lam_init=lam_init),
        out_shape=jax.ShapeDtypeStruct((b, t, H_A * DV_A), BF16),
        grid=(b, H_A, t // tq),
        in_specs=[
            pl.BlockSpec(memory_space=pltpu.SMEM),
            pl.BlockSpec((4, DH_A), lambda bi, h, qi: (0, 0)),
            pl.BlockSpec((1, DV_A), lambda bi, h, qi: (0, 0)),
            pl.BlockSpec((1, tq, LANES), lambda bi, h, qi: (bi, qi, h)),
            pl.BlockSpec((1, t, LANES), lambda bi, h, qi: (bi, 0, h)),
            pl.BlockSpec((1, t, LANES), lambda bi, h, qi: (bi, 0, H_A + h)),
        ],
        out_specs=pl.BlockSpec((1, tq, LANES), lambda bi, h, qi: (bi, qi, h)),
        scratch_shapes=[
            pltpu.VMEM((tq, LANES), F32), pltpu.VMEM((tq, LANES), F32),
            pltpu.VMEM((tq, LANES), F32), pltpu.VMEM((tq, LANES), F32),
            pltpu.VMEM((tq, LANES), F32), pltpu.VMEM((tq, LANES), F32),
        ],
        compiler_params=_cparams(("parallel", "parallel", "arbitrary")),
        name="diff_prompt",
    )(slopes, lqk, head_g, q3, kv3, kv3)


def _pieces_attention(qm, pieces):
    ss = [_nt(qm, k) + bias for k, _, bias in pieces]
    m = functools.reduce(jnp.maximum, [jnp.max(s, axis=-1, keepdims=True) for s in ss])
    es = [jnp.exp2(s - m) for s in ss]
    l = functools.reduce(jnp.add, [jnp.sum(e, axis=-1, keepdims=True) for e in es])
    o = functools.reduce(jnp.add, [jnp.dot(e.astype(BF16), v, preferred_element_type=F32)
                                   for e, (_, v, _) in zip(es, pieces)])
    return o / l


def _diff_sample_kernel(lqk_ref, hg_ref, q_ref, kc_ref, vc_ref, kn_ref, vn_ref, o_ref, *, lam_init):
    t = q_ref.shape[1]
    past = kc_ref.shape[1]
    lam = _lambda(lqk_ref[...], lam_init)
    rc = lax.broadcasted_iota(jnp.int32, (t, past), 0)
    cc = lax.broadcasted_iota(jnp.int32, (t, past), 1)
    rn = lax.broadcasted_iota(jnp.int32, (t, t), 0)
    cn = lax.broadcasted_iota(jnp.int32, (t, t), 1)
    dist_c = jnp.abs(past + rc - cc).astype(F32)
    dist_n = jnp.abs(rn - cn).astype(F32)
    ok_c = (cc // CHUNK) <= ((past + rc) // CHUNK)
    ok_n = ((past + cn) // CHUNK) <= ((past + rn) // CHUNK)
    for h in range(H_A):
        cols = slice(h * LANES, (h + 1) * LANES)
        bias_c = jnp.where(ok_c, (-ALIBI_SLOPES[h] * LOG2E) * dist_c, NEG)
        bias_n = jnp.where(ok_n, (-ALIBI_SLOPES[h] * LOG2E) * dist_n, NEG)
        pieces = [(kc_ref[0, :, h, :].astype(BF16), vc_ref[0, :, h, :].astype(BF16), bias_c),
                  (kn_ref[0, :, cols].astype(BF16), vn_ref[0, :, cols].astype(BF16), bias_n)]
        qa, qb = _split_halves(q_ref[0, :, cols])
        o = _pieces_attention(qa, pieces) - lam * _pieces_attention(qb, pieces)
        o_ref[0, :, cols] = (_rms(o, hg_ref[...]) * (1.0 - lam_init)).astype(o_ref.dtype)


def _diff_sample(q3, kn3, vn3, l, kc, vc, lqk, head_g, lam_init):
    b, t, _ = q3.shape
    past = kc.shape[2]
    w = H_A * DV_A
    new = pl.BlockSpec((1, t, w), lambda bi: (l * b + bi, 0, 0))
    cache = pl.BlockSpec((None, 1, past, H_A, DV_A), lambda bi: (l, bi, 0, 0, 0))
    return pl.pallas_call(
        functools.partial(_diff_sample_kernel, lam_init=lam_init),
        out_shape=jax.ShapeDtypeStruct((b, t, w), BF16),
        grid=(b,),
        in_specs=[
            pl.BlockSpec((4, DH_A), lambda bi: (0, 0)),
            pl.BlockSpec((1, DV_A), lambda bi: (0, 0)),
            pl.BlockSpec((1, t, w), lambda bi: (bi, 0, 0)), cache, cache, new, new,
        ],
        out_specs=pl.BlockSpec((1, t, w), lambda bi: (bi, 0, 0)),
        compiler_params=_cparams(("parallel",)),
        name="diff_sample",
    )(lqk, head_g, q3, kc, vc, kn3, vn3)


QPAIR = 2 * CHUNK
WIN = BAND + QPAIR


def _relbias_kernel(tab_ref, o_ref):
    h = pl.program_id(0)
    r = lax.broadcasted_iota(jnp.int32, (QPAIR, WIN), 0)
    j = lax.broadcasted_iota(jnp.int32, (QPAIR, WIN), 1)
    rel = jnp.clip(r - j + BAND, -REL_CLIP, REL_CLIP) + REL_CLIP
    qc = r // CHUNK
    kc = j // CHUNK - BAND_CHUNKS
    allowed = (kc <= qc) & (kc >= qc - BAND_CHUNKS)

    far = BAND - REL_CLIP
    near_rel = rel[:, far:]

    near = jnp.zeros((QPAIR, WIN - far), F32)
    for t in range(2 * REL_CLIP + 1):
        near = jnp.where(near_rel == t, tab_ref[h, t], near)
    bias = jnp.concatenate([jnp.full((QPAIR, far), tab_ref[h, 2 * REL_CLIP], F32), near], axis=1)
    o_ref[0] = jnp.where(allowed, bias * LOG2E, NEG)


def _relbias(table):
    return pl.pallas_call(
        _relbias_kernel,
        out_shape=jax.ShapeDtypeStruct((H_B, QPAIR, WIN), F32),
        grid=(H_B,),
        in_specs=[pl.BlockSpec(memory_space=pltpu.SMEM)],
        out_specs=pl.BlockSpec((1, QPAIR, WIN), lambda h: (h, 0, 0)),
        compiler_params=_cparams(("arbitrary",)),
        name="relbias",
    )(table)


BAND_TQ = 512
BAND_UNROLL = 2


def _band_prompt_kernel(bias_ref, q_ref, kp_ref, kc_ref, vp_ref, vc_ref, o_ref, kcat, vcat):
    i = pl.program_id(1)
    kcat[0:BAND_TQ, :] = kp_ref[0].astype(BF16)
    kcat[BAND_TQ:, :] = kc_ref[0].astype(BF16)
    vcat[0:BAND_TQ, :] = vp_ref[0].astype(BF16)
    vcat[BAND_TQ:, :] = vc_ref[0].astype(BF16)
    lane = lax.broadcasted_iota(jnp.int32, (QPAIR, LANES), 1)

    def pair_blocks(it, carry, *, first_block):
        starts = [pl.multiple_of((it * BAND_UNROLL + u) * QPAIR, QPAIR) for u in range(BAND_UNROLL)]
        scores = []
        for r0 in starts:
            q = q_ref[0, pl.ds(r0, QPAIR), :]
            if first_block:
                kpos = r0 - BAND_TQ + lax.broadcasted_iota(jnp.int32, (1, WIN), 1)
                before_start = jnp.where(kpos >= 0, 0.0, NEG)
            for hp in range(H_B // 2):
                cols = slice(hp * LANES, (hp + 1) * LANES)
                qs = jnp.concatenate(_split_halves(q[:, cols]), axis=0)
                s = _nt(qs, kcat[pl.ds(r0, WIN), cols]) + bias_ref[hp]
                scores.append(s + before_start if first_block else s)
        probs = []
        for s in scores:
            e = jnp.exp2(s - jnp.max(s, axis=-1, keepdims=True))
            probs.append((e.astype(BF16), jnp.sum(e, axis=-1, keepdims=True)))
        for u, r0 in enumerate(starts):
            outs = []
            for hp in range(H_B // 2):
                e, l = probs[u * (H_B // 2) + hp]
                cols = slice(hp * LANES, (hp + 1) * LANES)
                o = jnp.dot(e, vcat[pl.ds(r0, WIN), cols], preferred_element_type=F32) / l
                outs.append(jnp.where(lane < DH_B, o[:QPAIR], o[QPAIR:]))
            o_ref[0, pl.ds(r0, QPAIR), :] = jnp.concatenate(outs, axis=1).astype(o_ref.dtype)
        return carry

    trips = BAND_TQ // QPAIR // BAND_UNROLL

    @pl.when(i == 0)
    def _():
        lax.fori_loop(0, trips, functools.partial(pair_blocks, first_block=True), 0)

    @pl.when(i > 0)
    def _():
        lax.fori_loop(0, trips, functools.partial(pair_blocks, first_block=False), 0)


def _band_prompt(q3, kv3, bias):
    b, t, _ = q3.shape
    assert t % BAND_TQ == 0
    w = H_B * DH_B
    cq, ck, cv = 1, 0, 1
    prev = lambda col: pl.BlockSpec((1, BAND_TQ, w), lambda bi, i: (bi, jnp.maximum(i - 1, 0), col))
    cur = lambda col: pl.BlockSpec((1, BAND_TQ, w), lambda bi, i: (bi, i, col))
    return pl.pallas_call(
        _band_prompt_kernel,
        out_shape=jax.ShapeDtypeStruct((b, t, w), BF16),
        grid=(b, t // BAND_TQ),
        in_specs=[
            pl.BlockSpec((H_B // 2, 2 * QPAIR, WIN), lambda bi, i: (0, 0, 0)),
            cur(cq), prev(ck), cur(ck), prev(cv), cur(cv),
        ],
        out_specs=pl.BlockSpec((1, BAND_TQ, w), lambda bi, i: (bi, i, 0)),
        scratch_shapes=[pltpu.VMEM((2 * BAND_TQ, w), BF16), pltpu.VMEM((2 * BAND_TQ, w), BF16)],
        compiler_params=_cparams(("parallel", "arbitrary")),
        name="band_prompt",
    )(bias.reshape(H_B // 2, 2 * QPAIR, WIN), q3, kv3, kv3, kv3, kv3)


def _band_sample_kernel(bc_ref, bn_ref, q_ref, kc_ref, vc_ref, kn_ref, vn_ref, o_ref, *, past):
    t = q_ref.shape[1]
    nband = kc_ref.shape[1]
    lane = lax.broadcasted_iota(jnp.int32, (t, LANES), 1)
    qpos_c = past + lax.broadcasted_iota(jnp.int32, (t, nband), 0)
    kpos_c = past - nband + lax.broadcasted_iota(jnp.int32, (t, nband), 1)
    qpos_n = past + lax.broadcasted_iota(jnp.int32, (t, t), 0)
    kpos_n = past + lax.broadcasted_iota(jnp.int32, (t, t), 1)

    def allowed(qpos, kpos):
        qc, kc = qpos // CHUNK, kpos // CHUNK
        return (kpos >= 0) & (kc <= qc) & (kc >= qc - BAND_CHUNKS)

    ok_c = allowed(qpos_c, kpos_c)
    ok_n = allowed(qpos_n, kpos_n)
    for hp in range(H_B // 2):
        cols = slice(hp * LANES, (hp + 1) * LANES)
        halves = []
        for sub, qm in enumerate(_split_halves(q_ref[0, :, cols])):
            hd = 2 * hp + sub
            pieces = [(kc_ref[0, :, cols].astype(BF16), vc_ref[0, :, cols].astype(BF16),
                       jnp.where(ok_c, bc_ref[hd], NEG)),
                      (kn_ref[0, :, cols].astype(BF16), vn_ref[0, :, cols].astype(BF16),
                       jnp.where(ok_n, bn_ref[hd], NEG))]
            halves.append(_pieces_attention(qm, pieces))
        o_ref[0, :, cols] = jnp.where(lane < DH_B, halves[0], halves[1]).astype(o_ref.dtype)


def _band_sample(q3, kv3, kc, vc, bias, past):
    b, t, _ = q3.shape
    nband = kc.shape[1]
    assert nband == BAND and t <= CHUNK
    w = H_B * DH_B
    bias_c = bias[:, :t, :nband]
    bias_n = bias[:, :t, nband:nband + t]
    blk = lambda col: pl.BlockSpec((1, t, w), lambda bi: (bi, 0, col))
    cache = pl.BlockSpec((1, nband, w), lambda bi: (bi, 0, 0))
    return pl.pallas_call(
        functools.partial(_band_sample_kernel, past=past),
        out_shape=jax.ShapeDtypeStruct((b, t, w), BF16),
        grid=(b,),
        in_specs=[
            pl.BlockSpec((H_B, t, nband), lambda bi: (0, 0, 0)),
            pl.BlockSpec((H_B, t, t), lambda bi: (0, 0, 0)),
            blk(1), cache, cache, blk(0), blk(1),
        ],
        out_specs=pl.BlockSpec((1, t, w), lambda bi: (bi, 0, 0)),
        compiler_params=_cparams(("parallel",)),
        name="band_sample",
    )(bias_c, bias_n, q3, kc, vc, kv3, kv3)


CONV_PAD = 8
MLSTM_L = 256


def _mlstm_kernel(cqk_ref, cv_ref, co_ref, cif_ref, conv0_ref, cw_ref, cb_ref, gb_ref, hg_ref,
                  c0_ref, n0_ref, m0_ref, hs_ref, c_ref, n_ref, m_ref, xext, *, L):
    @pl.when(pl.program_id(1) == 0)
    def _():
        xext[0:CONV_PAD, :] = conv0_ref[0]
        c_ref[...] = c0_ref[...]
        n_ref[...] = n0_ref[...]
        m_ref[...] = m0_ref[...]

    xext[CONV_PAD:CONV_PAD + L, :] = cqk_ref[0]
    base = CONV_PAD - (CONV_W - 1)
    u = 0.0
    for j in range(CONV_W):
        u = u + xext[base + j:base + j + L, :] * cw_ref[j:j + 1, :]
    u = cb_ref[...] + u
    u = u * jax.nn.sigmoid(u)
    xext[0:CONV_PAD, :] = xext[L:L + CONV_PAD, :]

    z = cif_ref[0] + gb_ref[...]
    lane = lax.broadcasted_iota(jnp.int32, (L, LANES), 1)
    lf = jnp.minimum(z, 0.0) - jnp.log1p(jnp.exp(-jnp.abs(z)))
    tr = lax.broadcasted_iota(jnp.int32, (L, L), 0)
    tc = lax.broadcasted_iota(jnp.int32, (L, L), 1)
    causal = tr >= tc
    b_all = jnp.dot(causal.astype(F32), lf, preferred_element_type=F32,
                    precision=lax.Precision.HIGHEST)
    sr = lax.broadcasted_iota(jnp.int32, (8, LANES), 0)
    sc = lax.broadcasted_iota(jnp.int32, (8, LANES), 1)
    sel = jnp.where(sr < H_C, jnp.where(sc == sr, 1.0, jnp.where(sc == sr + H_C, -1.0, 0.0)), 0.0)
    rt = lax.dot_general(sel, jnp.where(lane < H_C, z, b_all), NT_DIMS,
                         preferred_element_type=F32, precision=lax.Precision.HIGHEST)
    er = lax.broadcasted_iota(jnp.int32, (DH_C, DH_C), 0)
    ec = lax.broadcasted_iota(jnp.int32, (DH_C, DH_C), 1)
    eye = jnp.where(er == ec, 1.0, 0.0).astype(BF16)

    for h in range(H_C):
        cols = slice(h * DH_C, (h + 1) * DH_C)
        q = u[:, h * DH_C:(h + 1) * DH_C]
        k = u[:, D_C + h * DH_C:D_C + (h + 1) * DH_C] * (DH_C ** -0.5)
        v = cv_ref[0, :, cols]
        qb, kb_, vb_ = q.astype(BF16), k.astype(BF16), v.astype(BF16)
        b_col = b_all[:, H_C + h:H_C + h + 1]
        li_col = z[:, h:h + 1]
        m_prev = m_ref[0, h:h + 1, 0:1]
        cs = c_ref[0, h]
        ns = n_ref[0, h:h + 1, :]

        dmat = jnp.where(causal, b_col + rt[h:h + 1, :], NEG)
        inter = b_col + m_prev
        m_t = jnp.maximum(inter, jnp.max(dmat, axis=-1, keepdims=True))
        w_intra = jnp.exp(dmat - m_t)
        w_inter = jnp.exp(inter - m_t)
        a = w_intra * _nt(qb, kb_)
        num = (jnp.dot(a.astype(BF16), vb_, preferred_element_type=F32)
               + w_inter * _nt(qb, cs.astype(BF16)))
        den = jnp.sum(a, axis=-1, keepdims=True) + w_inter * jnp.sum(q * ns, axis=-1, keepdims=True)
        hh = num / jnp.maximum(jnp.abs(den), jnp.exp(-m_t))

        b_last = b_col[L - 1:L, :]
        g = b_last - b_col + li_col
        m_new = jnp.maximum(b_last + m_prev, jnp.max(g, axis=0, keepdims=True))
        ws = jnp.exp(g - m_new)
        decay = jnp.exp(b_last + m_prev - m_new)
        vwt = _nt(eye, (ws * v).astype(BF16)).astype(BF16)
        c_ref[0, h] = decay * cs + jnp.dot(vwt, kb_, preferred_element_type=F32)
        n_ref[0, h:h + 1, :] = decay * ns + jnp.sum(ws * k, axis=0, keepdims=True)
        m_ref[0, h:h + 1, :] = jnp.broadcast_to(m_new, (1, LANES))

        gate = jax.nn.sigmoid(co_ref[0, :, cols])
        hs_ref[0, :, cols] = (_rms(hh, hg_ref[...]) * gate).astype(hs_ref.dtype)


def _mlstm(cqk3, cvo3, cif3, conv0, conv_w, conv_b, gate_b, head_g, c0, n0, m0, L):
    b, t, _ = cqk3.shape
    nc = t // L
    blk = lambda col, w: pl.BlockSpec((1, L, w), lambda bi, c: (bi, c, col))
    const = lambda shape: pl.BlockSpec(shape, lambda bi, c: (0,) * len(shape))
    per_b = lambda shape: pl.BlockSpec((1,) + shape, lambda bi, c: (bi,) + (0,) * len(shape))
    return pl.pallas_call(
        functools.partial(_mlstm_kernel, L=L),
        out_shape=(jax.ShapeDtypeStruct((b, t, D_C), BF16),
                   jax.ShapeDtypeStruct((b, H_C, DH_C, DH_C), F32),
                   jax.ShapeDtypeStruct((b, H_C, DH_C), F32),
                   jax.ShapeDtypeStruct((b, H_C, LANES), F32)),
        grid=(b, nc),
        in_specs=[
            blk(0, 2 * D_C), blk(0, D_C), blk(1, D_C), blk(0, LANES),
            per_b((CONV_PAD, 2 * D_C)), const((CONV_W, 2 * D_C)), const((1, 2 * D_C)),
            const((1, LANES)), const((1, DH_C)),
            per_b((H_C, DH_C, DH_C)), per_b((H_C, DH_C)), per_b((H_C, LANES)),
        ],
        out_specs=(pl.BlockSpec((1, L, D_C), lambda bi, c: (bi, c, 0)),
                   per_b((H_C, DH_C, DH_C)), per_b((H_C, DH_C)), per_b((H_C, LANES))),
        scratch_shapes=[pltpu.VMEM((CONV_PAD + L, 2 * D_C), F32)],
        compiler_params=_cparams(("parallel", "arbitrary")),
        name="mlstm",
    )(cqk3, cvo3, cvo3, cif3, conv0, conv_w, conv_b, gate_b, head_g, c0, n0, m0)


def _mix_kernel(oa_ref, ob_ref, oc_ref, ga_ref, gb_ref, gc_ref, x_ref, wa_ref, wb_ref, wc_ref, wo_ref, o_ref):
    mixed = (ga_ref[...] * jnp.dot(oa_ref[...], wa_ref[...], preferred_element_type=F32)
             + gb_ref[...] * jnp.dot(ob_ref[...], wb_ref[...], preferred_element_type=F32)
             + gc_ref[...] * jnp.dot(oc_ref[...], wc_ref[...], preferred_element_type=F32))
    o_ref[...] = x_ref[...] + jnp.dot(mixed.astype(BF16), wo_ref[...], preferred_element_type=F32)


def _mix(oa, ob, oc, gates, x, l, wa, wb, wc, wo):
    m, d = x.shape
    tm = min(m, 512)
    row = lambda w, col=0: pl.BlockSpec((tm, w), lambda i: (i, col))
    full = lambda a: pl.BlockSpec((None,) + a.shape[1:], lambda i: (l, 0, 0))
    g0 = 0
    return pl.pallas_call(
        _mix_kernel,
        out_shape=jax.ShapeDtypeStruct((m, d), F32),
        grid=(m // tm,),
        in_specs=[row(oa.shape[1]), row(ob.shape[1]), row(oc.shape[1]),
                  row(d, g0), row(d, g0 + 1), row(d, g0 + 2), row(d),
                  full(wa), full(wb), full(wc), full(wo)],
        out_specs=row(d),
        compiler_params=_cparams(("parallel",)),
        name="mix",
    )(oa, ob, oc, gates, gates, gates, x, wa, wb, wc, wo)


def _cross_kernel(x_ref, g_ref, wq_ref, mk_ref, mv_ref, wo_ref, o_ref):
    x = x_ref[0]
    h = _rms(x, g_ref[...]).astype(BF16)
    q = (jnp.dot(h, wq_ref[...], preferred_element_type=F32) * (DH_M ** -0.5 * LOG2E)).astype(BF16)
    outs = []
    for hd in range(H_M):
        cols = slice(hd * DH_M, (hd + 1) * DH_M)
        s = _nt(q[:, cols], mk_ref[0, :, cols].astype(BF16))
        m = jnp.max(s, axis=-1, keepdims=True)
        e = jnp.exp2(s - m)
        l = jnp.sum(e, axis=-1, keepdims=True)
        o = jnp.dot(e.astype(BF16), mv_ref[0, :, cols].astype(BF16), preferred_element_type=F32) / l
        outs.append(o.astype(BF16))
    o = jnp.concatenate(outs, axis=1)
    o_ref[0] = x + jnp.dot(o, wo_ref[...], preferred_element_type=F32)


def _cross(x3, g, l, wq, mk, mv, wo):
    b, t, d = x3.shape
    tm = min(t, 512)
    nm = mk.shape[1]
    full = lambda a: pl.BlockSpec((None,) + a.shape[1:], lambda bi, i: (l, 0, 0))
    return pl.pallas_call(
        _cross_kernel,
        out_shape=jax.ShapeDtypeStruct((b, t, d), F32),
        grid=(b, t // tm),
        in_specs=[pl.BlockSpec((1, tm, d), lambda bi, i: (bi, i, 0)),
                  pl.BlockSpec(g.shape, lambda bi, i: (0, 0)), full(wq),
                  pl.BlockSpec((1, nm, d), lambda bi, i: (bi, 0, 0)),
                  pl.BlockSpec((1, nm, d), lambda bi, i: (bi, 0, 0)), full(wo)],
        out_specs=pl.BlockSpec((1, tm, d), lambda bi, i: (bi, i, 0)),
        compiler_params=_cparams(("parallel", "parallel")),
        name="cross",
    )(x3, g, wq, mk, mv, wo)


def _ffn_kernel(x_ref, g_ref, wg_ref, wu_ref, wd_ref, gf_ref, o_ref, h_scr, acc, *, final_norm):
    j = pl.program_id(1)

    @pl.when(j == 0)
    def _():
        h_scr[...] = _rms(x_ref[...], g_ref[...]).astype(BF16)
        acc[...] = jnp.zeros(acc.shape, F32)

    h = h_scr[...]
    gate = jnp.dot(h, wg_ref[...], preferred_element_type=F32)
    up = jnp.dot(h, wu_ref[...], preferred_element_type=F32)
    a = (gate * jax.nn.sigmoid(gate) * up).astype(BF16)
    acc[...] += jnp.dot(a, wd_ref[...], preferred_element_type=F32)

    @pl.when(j == pl.num_programs(1) - 1)
    def _():
        y = x_ref[...] + acc[...]
        if final_norm:
            y = _rms(y, gf_ref[...])
        o_ref[...] = y


FFN_TF = 1408
FFN_TM = 512


def _ffn(x, g, l, wg, wu, wd, g_final, final_norm):
    m, d = x.shape
    tm = min(m, FFN_TM)
    return pl.pallas_call(
        functools.partial(_ffn_kernel, final_norm=final_norm),
        out_shape=jax.ShapeDtypeStruct((m, d), F32),
        grid=(m // tm, D_FF // FFN_TF),
        in_specs=[
            pl.BlockSpec((tm, d), lambda i, j: (i, 0)),
            pl.BlockSpec((1, d), lambda i, j: (0, 0)),
            pl.BlockSpec((None, d, FFN_TF), lambda i, j: (l, 0, j)),
            pl.BlockSpec((None, d, FFN_TF), lambda i, j: (l, 0, j)),
            pl.BlockSpec((None, FFN_TF, d), lambda i, j: (l, j, 0)),
            pl.BlockSpec((1, d), lambda i, j: (0, 0)),
        ],
        out_specs=pl.BlockSpec((tm, d), lambda i, j: (i, 0)),
        scratch_shapes=[pltpu.VMEM((tm, d), BF16), pltpu.VMEM((tm, d), F32)],
        compiler_params=_cparams(("parallel", "arbitrary")),
        name="ffn",
    )(x, g, wg, wu, wd, g_final)


def _layer(x3, l, w, mem_k, mem_v, cache, final_norm, akv_all):
    b, t, d = x3.shape
    x2 = x3.reshape(b * t, d)
    row = lambda a: a.reshape(1, -1)
    lam_init = 0.8 - 0.6 * math.exp(-0.3 * l)

    gates, cqk, cif = _in_proj_a(x2, row(w['g_mix']), l, w['w_in'], w['w_if'])
    q, ak_all, av_all, akv, bkv, cvo = _in_proj_b(x2, row(w['g_mix']), w['w_in'], l, *akv_all, (b, t))
    q3 = q.reshape(b, t, SEG)
    bkv3 = bkv.reshape(b, t, SEG)
    cqk3 = cqk.reshape(b, t, 2 * D_C)
    keep = t if cache is not None else min(BAND, t)
    b_k = bkv3[:, t - keep:, :HALF].reshape(b, keep, H_B, DH_B)
    b_v = bkv3[:, t - keep:, HALF:].reshape(b, keep, H_B, DH_B)

    gate_b = jnp.concatenate([w['b_i'], w['b_f'], jnp.zeros((LANES - 2 * H_C,), F32)]).reshape(1, LANES)
    if cache is None:
        oa = _diff_prompt(q3, akv.reshape(b, t, SEG), w['lqk'], row(w['a_head_g']), lam_init)
        ob = _band_prompt(q3, bkv3, w['relbias'])
        conv0 = jnp.zeros((b, CONV_PAD, 2 * D_C), F32)
        c0 = jnp.zeros((b, H_C, DH_C, DH_C), F32)
        n0 = jnp.zeros((b, H_C, DH_C), F32)
        m0 = jnp.zeros((b, H_C, LANES), F32)
        L = min(t, MLSTM_L)
    else:
        past = cache['a_k'].shape[2]
        ak3 = ak_all.reshape(DEPTH * b, t, HALF)
        av3 = av_all.reshape(DEPTH * b, t, HALF)
        oa = _diff_sample(q3, ak3, av3, l, cache['a_k'], cache['a_v'],
                          w['lqk'], row(w['a_head_g']), lam_init)
        nband = cache['b_k'].shape[1]
        ob = _band_sample(q3, bkv3, cache['b_k'].reshape(b, nband, -1), cache['b_v'].reshape(b, nband, -1),
                          w['relbias'], past)
        conv0 = jnp.pad(cache['conv'], ((0, 0), (CONV_PAD - (CONV_W - 1), 0), (0, 0)))
        c0, n0 = cache['C'], cache['n']
        m0 = jnp.broadcast_to(cache['m'][:, :, None], (b, H_C, LANES))
        L = t
    oc, c_new, n_new, m_new = _mlstm(cqk3, cvo.reshape(b, t, SEG), cif.reshape(b, t, LANES), conv0,
                                     w['conv_w'], row(w['conv_b']), gate_b, row(w['c_head_g']), c0, n0, m0, L)
    assert t >= CONV_W - 1
    conv_new = cqk3[:, t - (CONV_W - 1):]

    x2 = _mix(oa.reshape(b * t, -1), ob.reshape(b * t, -1), oc.reshape(b * t, -1), gates, x2, l,
              w['w_up_a'], w['w_up_b'], w['w_up_c'], w['w_o'])
    x3 = _cross(x2.reshape(b, t, d), row(w['g_cross']), l, w['w_mq'], mem_k, mem_v, w['w_mo'])
    x2 = _ffn(x3.reshape(b * t, d), row(w['g_ffn']), l, w['w_ff_g'], w['w_ff_u'], w['w_ff_d'],
              row(w['g_final']), final_norm)
    return x2.reshape(b, t, d), (ak_all, av_all), (b_k, b_v, c_new, n_new, m_new[:, :, 0], conv_new)


def kernel(x_prompt, x_sample, cache_a_k, cache_a_v, cache_b_k, cache_b_v, state_c_C, state_c_n, state_c_m, state_c_conv, cache_mem_k, cache_mem_v, mem_prompt, g_mix, w_in, a_lq1, a_lk1, a_lq2, a_lk2, a_head_g, b_rel, c_conv_w, c_conv_b, c_b_i, c_b_f, c_head_g, w_up_a, w_up_b, w_up_c, w_o, g_cross, w_mq, w_mk, w_mv, w_mo, g_ffn, w_ff_g, w_ff_u, w_ff_d, g_final):
    xp, xs = x_prompt, x_sample
    bp = x_prompt.shape[0]
    n_mem = mem_prompt.shape[1]
    mem2 = mem_prompt.reshape(bp * n_mem, D_MODEL)
    new_p = [[] for _ in range(8)]
    new_s = [[] for _ in range(6)]
    akv_p = akv_s = (None, None)
    bf = lambda a: a.astype(BF16)
    wp, wif = _prep_w_in(w_in)
    wts = dict(w_up_a=bf(w_up_a), w_up_b=bf(w_up_b), w_up_c=bf(w_up_c), w_o=bf(w_o), w_mq=bf(w_mq),
               w_mo=bf(w_mo), w_ff_g=bf(w_ff_g), w_ff_u=bf(w_ff_u), w_ff_d=bf(w_ff_d))
    w_mk_b, w_mv_b = bf(w_mk), bf(w_mv)
    for l in range(DEPTH):
        w = dict(g_mix=g_mix[l], w_in=wp, w_if=wif,
                 lqk=jnp.stack([a_lq1[l], a_lk1[l], a_lq2[l], a_lk2[l]]),
                 a_head_g=a_head_g[l], relbias=_relbias(b_rel[l]), conv_w=c_conv_w[l], conv_b=c_conv_b[l],
                 b_i=c_b_i[l], b_f=c_b_f[l], c_head_g=c_head_g[l],
                 g_cross=g_cross[l], g_ffn=g_ffn[l], g_final=g_final, **wts)
        mk2 = _mem_proj(mem2, l, w_mk_b).reshape(bp, n_mem, D_MODEL)
        mv2 = _mem_proj(mem2, l, w_mv_b).reshape(bp, n_mem, D_MODEL)
        last = l == DEPTH - 1
        xp, akv_p, st_p = _layer(xp, l, w, mk2, mv2, None, last, akv_p)
        cache = dict(a_k=cache_a_k, a_v=cache_a_v, b_k=cache_b_k[l], b_v=cache_b_v[l],
                     C=state_c_C[l], n=state_c_n[l], m=state_c_m[l], conv=state_c_conv[l])
        bs = xs.shape[0]
        xs, akv_s, st_s = _layer(xs, l, w, cache_mem_k[l].reshape(bs, n_mem, D_MODEL),
                                 cache_mem_v[l].reshape(bs, n_mem, D_MODEL), cache, last, akv_s)
        mk4 = mk2.reshape(bp, n_mem, H_M, DH_M)
        mv4 = mv2.reshape(bp, n_mem, H_M, DH_M)
        for lst, a in zip(new_p, st_p + (mk4, mv4)):
            lst.append(a)
        for lst, a in zip(new_s, st_s):
            lst.append(a)
    heads = lambda a, b, t: a.reshape(DEPTH, b, t, H_A, DV_A)
    outs_p = [heads(a, bp, x_prompt.shape[1]) for a in akv_p] + [jnp.stack(a, 0) for a in new_p]
    outs_s = [heads(a, xs.shape[0], xs.shape[1]) for a in akv_s] + [jnp.stack(a, 0) for a in new_s]
    return (xp, xs) + tuple(outs_p) + tuple(outs_s)
```

```python
import functools
import math

import jax
import jax.numpy as jnp
from jax import lax
from jax.experimental import pallas as pl
from jax.experimental.pallas import tpu as pltpu

F32 = jnp.float32
BF16 = jnp.bfloat16

D_MODEL = 1024
DEPTH = 2
CHUNK = 64
EPS = 1e-6
NEG = -1e30
LOG2E = math.log2(math.e)
H_A = 4
DH_A = 64
DV_A = 2 * DH_A
H_B = 8
DH_B = 64
BAND_CHUNKS = 8
BAND = BAND_CHUNKS * CHUNK
REL_CLIP = 128
H_C = 4
DH_C = 128
D_C = H_C * DH_C
CONV_W = 4
H_M = 4
DH_M = D_MODEL // H_M
D_FF = -(-8 * D_MODEL // (3 * 256)) * 256

LANES = 128
VMEM_LIMIT = 48 * 1024 * 1024

SEG = 1024
HALF = SEG // 2
GATE_TILES = 3
N_TILES_A = GATE_TILES + 1
N_TILES_B = 4
IN_SIZES = (H_A * DH_A,) * 4 + (H_A * DV_A,) + (H_B * DH_B,) * 3 + (2 * D_C, D_C, D_C, 2 * H_C, 3 * D_MODEL)

ALIBI_SLOPES = tuple(2.0 ** (-8.0 * (i + 1) / H_A) for i in range(H_A))

NT_DIMS = (((1,), (1,)), ((), ()))


def _cparams(sem):
    return pltpu.CompilerParams(dimension_semantics=sem, vmem_limit_bytes=VMEM_LIMIT)


def _rms(x, g):
    ms = jnp.mean(x * x, axis=-1, keepdims=True)
    return x * lax.rsqrt(ms + EPS) * g


def _nt(a, b):
    return lax.dot_general(a, b, NT_DIMS, preferred_element_type=F32)


def _prep_w_in_kernel(w_ref, o_ref, oif_ref):
    offs = [0]
    for size in IN_SIZES:
        offs.append(offs[-1] + size)
    seg = lambda i: w_ref[:, offs[i]:offs[i + 1]]
    a_q1, a_q2, a_k1, a_k2, a_v, b_q, b_k, b_v, c_qk, c_v, c_o = (seg(i) for i in range(11))
    g = seg(12)

    def put(col, val):
        o_ref[:, col:col + val.shape[1]] = val.astype(BF16)

    put(0, g)
    put(GATE_TILES * SEG, c_qk)
    for h in range(H_A):
        hs = slice(h * DH_A, (h + 1) * DH_A)
        base = h * 2 * DH_A
        put(4 * SEG + base, a_q1[:, hs] * (DH_A ** -0.5))
        put(4 * SEG + base + DH_A, a_q2[:, hs] * (DH_A ** -0.5))
        put(5 * SEG + base, a_k1[:, hs])
        put(5 * SEG + base + DH_A, a_k2[:, hs])
    put(4 * SEG + HALF, b_q * (DH_B ** -0.5))
    put(5 * SEG + HALF, a_v)
    put(6 * SEG, b_k)
    put(6 * SEG + HALF, b_v)
    put(7 * SEG, c_v)
    put(7 * SEG + HALF, c_o)
    tile = w_ref[:, offs[11]:offs[11] + LANES]
    lane = lax.broadcasted_iota(jnp.int32, tile.shape, 1)
    oif_ref[...] = jnp.where(lane < 2 * H_C, tile, 0.0).astype(BF16)


def _prep_w_in(w):
    depth, k, n = w.shape
    tk = 256
    n_out = (N_TILES_A + N_TILES_B) * SEG
    return pl.pallas_call(
        _prep_w_in_kernel,
        out_shape=(jax.ShapeDtypeStruct((depth, k, n_out), BF16),
                   jax.ShapeDtypeStruct((depth, k, LANES), BF16)),
        grid=(depth, k // tk),
        in_specs=[pl.BlockSpec((None, tk, n), lambda l, i: (l, i, 0))],
        out_specs=(pl.BlockSpec((None, tk, n_out), lambda l, i: (l, i, 0)),
                   pl.BlockSpec((None, tk, LANES), lambda l, i: (l, i, 0))),
        compiler_params=_cparams(("parallel", "parallel")),
        name="prep_w_in",
    )(w)


def _norm_to_scratch(x_ref, g_ref, h_scr):
    h_scr[...] = _rms(x_ref[...], g_ref[...]).astype(BF16)


def _in_proj_a_kernel(x_ref, g_ref, w_ref, wif_ref, og_ref, ocqk_ref, ocif_ref, h_scr):
    j = pl.program_id(1)

    @pl.when(j == 0)
    def _():
        _norm_to_scratch(x_ref, g_ref, h_scr)
        ocif_ref[...] = jnp.dot(h_scr[...], wif_ref[...], preferred_element_type=F32)

    tile = lambda: jnp.dot(h_scr[...], w_ref[...], preferred_element_type=F32)

    @pl.when(j < GATE_TILES)
    def _():
        og_ref[...] = jax.nn.sigmoid(tile()).astype(og_ref.dtype)

    @pl.when(j == GATE_TILES)
    def _():
        ocqk_ref[...] = tile()


def _in_proj_a(x, g, l, wp, wif):
    m, k = x.shape
    tm = min(m, 1024)
    return pl.pallas_call(
        _in_proj_a_kernel,
        out_shape=(jax.ShapeDtypeStruct((m, GATE_TILES * SEG), BF16),
                   jax.ShapeDtypeStruct((m, SEG), F32),
                   jax.ShapeDtypeStruct((m, LANES), F32)),
        grid=(m // tm, N_TILES_A),
        in_specs=[
            pl.BlockSpec((tm, k), lambda i, j: (i, 0)),
            pl.BlockSpec((1, k), lambda i, j: (0, 0)),
            pl.BlockSpec((None, k, SEG), lambda i, j: (l, 0, j)),
            pl.BlockSpec((None, k, LANES), lambda i, j: (l, 0, 0)),
        ],
        out_specs=(pl.BlockSpec((tm, SEG), lambda i, j: (i, jnp.minimum(j, GATE_TILES - 1))),
                   pl.BlockSpec((tm, SEG), lambda i, j: (i, 0)),
                   pl.BlockSpec((tm, LANES), lambda i, j: (i, 0))),
        scratch_shapes=[pltpu.VMEM((tm, k), BF16)],
        compiler_params=_cparams(("parallel", "arbitrary")),
        name="in_proj_a",
    )(x, g, wp, wif)


def _in_proj_b_kernel(*refs, aliased, heads5d):
    x_ref, g_ref, w_ref = refs[:3]
    oq_ref, oak_ref, oav_ref, oakv_ref, obkv_ref, ocvo_ref, h_scr = refs[5 if aliased else 3:]
    j = pl.program_id(1)

    @pl.when(j == 0)
    def _():
        _norm_to_scratch(x_ref, g_ref, h_scr)

    tile = lambda: jnp.dot(h_scr[...], w_ref[...], preferred_element_type=F32)

    @pl.when(j == 0)
    def _():
        oq_ref[...] = (tile() * LOG2E).astype(oq_ref.dtype)

    @pl.when(j == 1)
    def _():
        for out_ref, off in ((oak_ref, 0), (oav_ref, HALF)):
            a = jnp.dot(h_scr[...], w_ref[:, off:off + HALF], preferred_element_type=F32)
            oakv_ref[:, off:off + HALF] = a.astype(oakv_ref.dtype)
            if heads5d:
                for hd in range(H_A):
                    out_ref[0, 0, :, hd, :] = a[:, hd * DV_A:(hd + 1) * DV_A]
            else:
                out_ref[0] = a

    @pl.when(j == 2)
    def _():
        obkv_ref[...] = tile()

    @pl.when(j == 3)
    def _():
        ocvo_ref[...] = tile()


def _in_proj_b(x, g, wp, l, ak_all, av_all, bt):
    m, k = x.shape
    tm = min(m, 1024)
    aliased = ak_all is not None
    b, t = bt
    heads5d = t % tm == 0
    if heads5d:
        stacked = jax.ShapeDtypeStruct((DEPTH, b, t, H_A, DV_A), F32)
        layer_row = pl.BlockSpec((1, 1, tm, H_A, DV_A),
                                 lambda i, j: (l, i // (t // tm), i % (t // tm), 0, 0))
    else:
        stacked = jax.ShapeDtypeStruct((DEPTH, m, HALF), F32)
        layer_row = pl.BlockSpec((1, tm, HALF), lambda i, j: (l, i, 0))
    in_specs = [
        pl.BlockSpec((tm, k), lambda i, j: (i, 0)),
        pl.BlockSpec((1, k), lambda i, j: (0, 0)),
        pl.BlockSpec((None, k, SEG), lambda i, j: (l, 0, N_TILES_A + j)),
    ]
    args = [x, g, wp]
    if aliased:
        in_specs += [pl.BlockSpec(memory_space=pl.ANY)] * 2
        args += [ak_all, av_all]
    row = lambda w: pl.BlockSpec((tm, w), lambda i, j: (i, 0))
    return pl.pallas_call(
        functools.partial(_in_proj_b_kernel, aliased=aliased, heads5d=heads5d),
        out_shape=(jax.ShapeDtypeStruct((m, SEG), BF16), stacked, stacked, jax.ShapeDtypeStruct((m, SEG), BF16),
                   jax.ShapeDtypeStruct((m, SEG), F32), jax.ShapeDtypeStruct((m, SEG), F32)),
        grid=(m // tm, N_TILES_B),
        in_specs=in_specs,
        out_specs=(row(SEG), layer_row, layer_row, row(SEG), row(SEG), row(SEG)),
        scratch_shapes=[pltpu.VMEM((tm, k), BF16)],
        input_output_aliases={3: 1, 4: 2} if aliased else {},
        compiler_params=_cparams(("parallel", "arbitrary")),
        name="in_proj_b",
    )(*args)


IN_PROJ_TM = 256


def _in_proj_kernel(*refs, aliased, heads5d):
    x_ref, g_ref, w_ref, wif_ref = refs[:4]
    og_ref, ocqk_ref, ocif_ref, oq_ref, oak_ref, oav_ref, oakv_ref, obkv_ref, ocvo_ref = refs[6 if aliased else 4:]
    h = _rms(x_ref[...], g_ref[...]).astype(BF16)
    cols = lambda j, lo=0, hi=SEG: jnp.dot(h, w_ref[:, j * SEG + lo:j * SEG + hi], preferred_element_type=F32)

    ocif_ref[...] = jnp.dot(h, wif_ref[...], preferred_element_type=F32)
    for j in range(GATE_TILES):
        og_ref[:, j * SEG:(j + 1) * SEG] = jax.nn.sigmoid(cols(j)).astype(og_ref.dtype)
    ocqk_ref[...] = cols(GATE_TILES)
    oq_ref[...] = (cols(4) * LOG2E).astype(oq_ref.dtype)
    for out_ref, off in ((oak_ref, 0), (oav_ref, HALF)):
        a = cols(5, off, off + HALF)
        oakv_ref[:, off:off + HALF] = a.astype(oakv_ref.dtype)
        if heads5d:
            for hd in range(H_A):
                out_ref[0, 0, :, hd, :] = a[:, hd * DV_A:(hd + 1) * DV_A]
        else:
            out_ref[0] = a
    obkv_ref[...] = cols(6)
    ocvo_ref[...] = cols(7)


def _in_proj(x, g, l, wp, wif, ak_all, av_all, bt):
    m, k = x.shape
    tm = min(m, IN_PROJ_TM)
    aliased = ak_all is not None
    b, t = bt
    heads5d = t % tm == 0
    if heads5d:
        stacked = jax.ShapeDtypeStruct((DEPTH, b, t, H_A, DV_A), F32)
        layer_row = pl.BlockSpec((1, 1, tm, H_A, DV_A), lambda i: (l, i // (t // tm), i % (t // tm), 0, 0))
    else:
        stacked = jax.ShapeDtypeStruct((DEPTH, m, HALF), F32)
        layer_row = pl.BlockSpec((1, tm, HALF), lambda i: (l, i, 0))
    resident = lambda a: pl.BlockSpec((None,) + a.shape[1:], lambda i: (l, 0, 0), pipeline_mode=pl.Buffered(1))
    in_specs = [pl.BlockSpec((tm, k), lambda i: (i, 0)), pl.BlockSpec((1, k), lambda i: (0, 0)),
                resident(wp), resident(wif)]
    args = [x, g, wp, wif]
    if aliased:
        in_specs += [pl.BlockSpec(memory_space=pl.ANY)] * 2
        args += [ak_all, av_all]
    row = lambda w: pl.BlockSpec((tm, w), lambda i: (i, 0))
    sds = jax.ShapeDtypeStruct
    return pl.pallas_call(
        functools.partial(_in_proj_kernel, aliased=aliased, heads5d=heads5d),
        out_shape=(sds((m, GATE_TILES * SEG), BF16), sds((m, SEG), F32), sds((m, LANES), F32),
                   sds((m, SEG), BF16), stacked, stacked, sds((m, SEG), BF16),
                   sds((m, SEG), F32), sds((m, SEG), F32)),
        grid=(m // tm,),
        in_specs=in_specs,
        out_specs=(row(GATE_TILES * SEG), row(SEG), row(LANES), row(SEG), layer_row, layer_row, row(SEG),
                   row(SEG), row(SEG)),
        input_output_aliases={4: 4, 5: 5} if aliased else {},
        compiler_params=_cparams(("parallel",)),
        name="in_proj",
    )(*args)


def _mem_proj_kernel(x_ref, w_ref, o_ref):
    o_ref[...] = jnp.dot(x_ref[...].astype(BF16), w_ref[...], preferred_element_type=F32)


def _mem_proj(x, l, w):
    m, k = x.shape
    n = w.shape[2]
    tm = min(m, 1024)
    return pl.pallas_call(
        _mem_proj_kernel,
        out_shape=jax.ShapeDtypeStruct((m, n), F32),
        grid=(m // tm,),
        in_specs=[pl.BlockSpec((tm, k), lambda i: (i, 0)), pl.BlockSpec((None, k, n), lambda i: (l, 0, 0))],
        out_specs=pl.BlockSpec((tm, n), lambda i: (i, 0)),
        compiler_params=_cparams(("parallel",)),
        name="mem_proj",
    )(x, w)


def _online_updates(jobs):
    staged = []
    for s, _, (acc, m, l) in jobs:
        slabs = [s[:, j * LANES:(j + 1) * LANES] for j in range(s.shape[1] // LANES)]
        m_old = m[...]
        m_new = jnp.maximum(m_old, jnp.max(functools.reduce(jnp.maximum, slabs), axis=-1, keepdims=True))
        alpha = jnp.exp2(m_old - m_new)
        ps = [jnp.exp2(c - m_new) for c in slabs]
        l[...] = alpha * l[...] + jnp.sum(functools.reduce(jnp.add, ps), axis=-1, keepdims=True)
        m[...] = m_new
        staged.append((alpha, jnp.concatenate(ps, axis=1).astype(BF16)))
    for (alpha, p), (_, v, (acc, m, l)) in zip(staged, jobs):
        acc[...] = alpha * acc[...] + jnp.dot(p, v, preferred_element_type=F32)


def _split_halves(q):
    lane = lax.broadcasted_iota(jnp.int32, q.shape, 1)
    lo = jnp.where(lane < DH_A, q, 0.0).astype(BF16)
    hi = jnp.where(lane >= DH_A, q, 0.0).astype(BF16)
    return lo, hi


def _lambda(lqk, lam_init):
    e1 = jnp.exp(jnp.sum(lqk[0:1, :] * lqk[1:2, :], axis=-1, keepdims=True))
    e2 = jnp.exp(jnp.sum(lqk[2:3, :] * lqk[3:4, :], axis=-1, keepdims=True))
    return e1 - e2 + lam_init


def _diff_prompt_kernel(slopes_ref, lqk_ref, hg_ref, q_ref, k_ref, v_ref, o_ref,
                        acc1, acc2, m1, l1, m2, l2, *, tq, lam_init):
    h = pl.program_id(1)
    qi = pl.program_id(2)
    slope = slopes_ref[h] * LOG2E
    kb, vb = k_ref.at[0], v_ref.at[0]

    qa, qb = _split_halves(q_ref[0])
    states = ((acc1, m1, l1), (acc2, m2, l2))
    for acc, m, l in states:
        m[...] = jnp.full(m.shape, NEG, F32)
        l[...] = jnp.zeros(l.shape, F32)
        acc[...] = jnp.zeros(acc.shape, F32)

    def full_tiles(k_starts):
        jobs = []
        for k0 in k_starts:
            k = kb[pl.ds(k0, tq), :]
            v = vb[pl.ds(k0, tq), :]
            kpos = k0 + lax.broadcasted_iota(jnp.int32, (1, tq), 1)
            bias = slope * kpos.astype(F32)
            jobs += [(_nt(qa, k) + bias, v, states[0]), (_nt(qb, k) + bias, v, states[1])]
        _online_updates(jobs)

    def tile_pair(it, carry):
        k0 = pl.multiple_of(it * (2 * tq), 2 * tq)
        full_tiles([k0, pl.multiple_of(k0 + tq, tq)])
        return carry

    lax.fori_loop(0, qi // 2, tile_pair, 0)

    @pl.when(qi % 2 == 1)
    def _():
        full_tiles([pl.multiple_of((qi - 1) * tq, tq)])

    q0 = pl.multiple_of(qi * tq, tq)
    k = kb[pl.ds(q0, tq), :]
    v = vb[pl.ds(q0, tq), :]
    r = lax.broadcasted_iota(jnp.int32, (tq, tq), 0)
    c = lax.broadcasted_iota(jnp.int32, (tq, tq), 1)
    allowed = (c // CHUNK) <= (r // CHUNK)
    bias = slope * (q0 + r - jnp.abs(r - c)).astype(F32)
    _online_updates([(jnp.where(allowed, _nt(qa, k) + bias, NEG), v, states[0]),
                     (jnp.where(allowed, _nt(qb, k) + bias, NEG), v, states[1])])

    lam = _lambda(lqk_ref[...], lam_init)
    o = acc1[...] / l1[...] - lam * (acc2[...] / l2[...])
    o_ref[0] = (_rms(o, hg_ref[...]) * (1.0 - lam_init)).astype(o_ref.dtype)


def _diff_prompt(q3, kv3, lqk, head_g, lam_init):
    b, t, _ = q3.shape
    tq = min(t, 512)
    slopes = jnp.asarray(ALIBI_SLOPES, F32)
    return pl.pallas_call(
        functools.partial(_diff_prompt_kernel, tq=tq, lam_init=lam_init),
        out_shape=jax.ShapeDtypeStruct((b, t, H_A * DV_A), BF16),
        grid=(b, H_A, t // tq),
        in_specs=[
            pl.BlockSpec(memory_space=pltpu.SMEM),
            pl.BlockSpec((4, DH_A), lambda bi, h, qi: (0, 0)),
            pl.BlockSpec((1, DV_A), lambda bi, h, qi: (0, 0)),
            pl.BlockSpec((1, tq, LANES), lambda bi, h, qi: (bi, qi, h)),
            pl.BlockSpec((1, t, LANES), lambda bi, h, qi: (bi, 0, h)),
            pl.BlockSpec((1, t, LANES), lambda bi, h, qi: (bi, 0, H_A + h)),
        ],
        out_specs=pl.BlockSpec((1, tq, LANES), lambda bi, h, qi: (bi, qi, h)),
        scratch_shapes=[
            pltpu.VMEM((tq, LANES), F32), pltpu.VMEM((tq, LANES), F32),
            pltpu.VMEM((tq, LANES), F32), pltpu.VMEM((tq, LANES), F32),
            pltpu.VMEM((tq, LANES), F32), pltpu.VMEM((tq, LANES), F32),
        ],
        compiler_params=_cparams(("parallel", "parallel", "arbitrary")),
        name="diff_prompt",
    )(slopes, lqk, head_g, q3, kv3, kv3)


def _pieces_attention(qm, pieces):
    ss = [_nt(qm, k) + bias for k, _, bias in pieces]
    m = functools.reduce(jnp.maximum, [jnp.max(s, axis=-1, keepdims=True) for s in ss])
    es = [jnp.exp2(s - m) for s in ss]
    l = functools.reduce(jnp.add, [jnp.sum(e, axis=-1, keepdims=True) for e in es])
    o = functools.reduce(jnp.add, [jnp.dot(e.astype(BF16), v, preferred_element_type=F32)
                                   for e, (_, v, _) in zip(es, pieces)])
    return o / l


def _diff_sample_kernel(lqk_ref, hg_ref, q_ref, kc_ref, vc_ref, kn_ref, vn_ref, o_ref, *, lam_init):
    t = q_ref.shape[1]
    past = kc_ref.shape[1]
    lam = _lambda(lqk_ref[...], lam_init)
    rc = lax.broadcasted_iota(jnp.int32, (t, past), 0)
    cc = lax.broadcasted_iota(jnp.int32, (t, past), 1)
    rn = lax.broadcasted_iota(jnp.int32, (t, t), 0)
    cn = lax.broadcasted_iota(jnp.int32, (t, t), 1)
    dist_c = jnp.abs(past + rc - cc).astype(F32)
    dist_n = jnp.abs(rn - cn).astype(F32)
    ok_c = (cc // CHUNK) <= ((past + rc) // CHUNK)
    ok_n = ((past + cn) // CHUNK) <= ((past + rn) // CHUNK)
    for h in range(H_A):
        cols = slice(h * LANES, (h + 1) * LANES)
        bias_c = jnp.where(ok_c, (-ALIBI_SLOPES[h] * LOG2E) * dist_c, NEG)
        bias_n = jnp.where(ok_n, (-ALIBI_SLOPES[h] * LOG2E) * dist_n, NEG)
        pieces = [(kc_ref[0, :, h, :].astype(BF16), vc_ref[0, :, h, :].astype(BF16), bias_c),
                  (kn_ref[0, :, cols].astype(BF16), vn_ref[0, :, cols].astype(BF16), bias_n)]
        qa, qb = _split_halves(q_ref[0, :, cols])
        o = _pieces_attention(qa, pieces) - lam * _pieces_attention(qb, pieces)
        o_ref[0, :, cols] = (_rms(o, hg_ref[...]) * (1.0 - lam_init)).astype(o_ref.dtype)


def _diff_sample(q3, kn3, vn3, l, kc, vc, lqk, head_g, lam_init):
    b, t, _ = q3.shape
    past = kc.shape[2]
    w = H_A * DV_A
    new = pl.BlockSpec((1, t, w), lambda bi: (l * b + bi, 0, 0))
    cache = pl.BlockSpec((None, 1, past, H_A, DV_A), lambda bi: (l, bi, 0, 0, 0))
    return pl.pallas_call(
        functools.partial(_diff_sample_kernel, lam_init=lam_init),
        out_shape=jax.ShapeDtypeStruct((b, t, w), BF16),
        grid=(b,),
        in_specs=[
            pl.BlockSpec((4, DH_A), lambda bi: (0, 0)),
            pl.BlockSpec((1, DV_A), lambda bi: (0, 0)),
            pl.BlockSpec((1, t, w), lambda bi: (bi, 0, 0)), cache, cache, new, new,
        ],
        out_specs=pl.BlockSpec((1, t, w), lambda bi: (bi, 0, 0)),
        compiler_params=_cparams(("parallel",)),
        name="diff_sample",
    )(lqk, head_g, q3, kc, vc, kn3, vn3)


QPAIR = 2 * CHUNK
WIN = BAND + QPAIR


def _relbias_kernel(tab_ref, o_ref):
    h = pl.program_id(0)
    r = lax.broadcasted_iota(jnp.int32, (QPAIR, WIN), 0)
    j = lax.broadcasted_iota(jnp.int32, (QPAIR, WIN), 1)
    rel = jnp.clip(r - j + BAND, -REL_CLIP, REL_CLIP) + REL_CLIP
    qc = r // CHUNK
    kc = j // CHUNK - BAND_CHUNKS
    allowed = (kc <= qc) & (kc >= qc - BAND_CHUNKS)

    far = BAND - REL_CLIP
    near_rel = rel[:, far:]

    near = jnp.zeros((QPAIR, WIN - far), F32)
    for t in range(2 * REL_CLIP + 1):
        near = jnp.where(near_rel == t, tab_ref[h, t], near)
    bias = jnp.concatenate([jnp.full((QPAIR, far), tab_ref[h, 2 * REL_CLIP], F32), near], axis=1)
    o_ref[0] = jnp.where(allowed, bias * LOG2E, NEG)


def _relbias(table):
    return pl.pallas_call(
        _relbias_kernel,
        out_shape=jax.ShapeDtypeStruct((H_B, QPAIR, WIN), F32),
        grid=(H_B,),
        in_specs=[pl.BlockSpec(memory_space=pltpu.SMEM)],
        out_specs=pl.BlockSpec((1, QPAIR, WIN), lambda h: (h, 0, 0)),
        compiler_params=_cparams(("arbitrary",)),
        name="relbias",
    )(table)


BAND_TQ = 512
BAND_UNROLL = 1


def _band_prompt_kernel(bias_ref, q_ref, kp_ref, kc_ref, vp_ref, vc_ref, o_ref, kcat, vcat):
    i = pl.program_id(1)
    kcat[0:BAND_TQ, :] = kp_ref[0].astype(BF16)
    kcat[BAND_TQ:, :] = kc_ref[0].astype(BF16)
    vcat[0:BAND_TQ, :] = vp_ref[0].astype(BF16)
    vcat[BAND_TQ:, :] = vc_ref[0].astype(BF16)
    lane = lax.broadcasted_iota(jnp.int32, (QPAIR, LANES), 1)

    def pair_blocks(it, carry, *, first_block):
        starts = [pl.multiple_of((it * BAND_UNROLL + u) * QPAIR, QPAIR) for u in range(BAND_UNROLL)]
        scores = []
        for r0 in starts:
            q = q_ref[0, pl.ds(r0, QPAIR), :]
            if first_block:
                kpos = r0 - BAND_TQ + lax.broadcasted_iota(jnp.int32, (1, WIN), 1)
                before_start = jnp.where(kpos >= 0, 0.0, NEG)
            for hp in range(H_B // 2):
                cols = slice(hp * LANES, (hp + 1) * LANES)
                qs = jnp.concatenate(_split_halves(q[:, cols]), axis=0)
                s = _nt(qs, kcat[pl.ds(r0, WIN), cols]) + bias_ref[hp]
                scores.append(s + before_start if first_block else s)
        probs = []
        for s in scores:
            e = jnp.exp2(s - jnp.max(s, axis=-1, keepdims=True))
            probs.append((e.astype(BF16), jnp.sum(e, axis=-1, keepdims=True)))
        for u, r0 in enumerate(starts):
            outs = []
            for hp in range(H_B // 2):
                e, l = probs[u * (H_B // 2) + hp]
                cols = slice(hp * LANES, (hp + 1) * LANES)
                o = jnp.dot(e, vcat[pl.ds(r0, WIN), cols], preferred_element_type=F32) / l
                outs.append(jnp.where(lane < DH_B, o[:QPAIR], o[QPAIR:]))
            o_ref[0, pl.ds(r0, QPAIR), :] = jnp.concatenate(outs, axis=1).astype(o_ref.dtype)
        return carry

    trips = BAND_TQ // QPAIR // BAND_UNROLL

    @pl.when(i == 0)
    def _():
        lax.fori_loop(0, trips, functools.partial(pair_blocks, first_block=True), 0)

    @pl.when(i > 0)
    def _():
        lax.fori_loop(0, trips, functools.partial(pair_blocks, first_block=False), 0)


def _band_prompt(q3, kv3, bias):
    b, t, _ = q3.shape
    assert t % BAND_TQ == 0
    w = H_B * DH_B
    cq, ck, cv = 1, 0, 1
    prev = lambda col: pl.BlockSpec((1, BAND_TQ, w), lambda bi, i: (bi, jnp.maximum(i - 1, 0), col))
    cur = lambda col: pl.BlockSpec((1, BAND_TQ, w), lambda bi, i: (bi, i, col))
    return pl.pallas_call(
        _band_prompt_kernel,
        out_shape=jax.ShapeDtypeStruct((b, t, w), BF16),
        grid=(b, t // BAND_TQ),
        in_specs=[
            pl.BlockSpec((H_B // 2, 2 * QPAIR, WIN), lambda bi, i: (0, 0, 0)),
            cur(cq), prev(ck), cur(ck), prev(cv), cur(cv),
        ],
        out_specs=pl.BlockSpec((1, BAND_TQ, w), lambda bi, i: (bi, i, 0)),
        scratch_shapes=[pltpu.VMEM((2 * BAND_TQ, w), BF16), pltpu.VMEM((2 * BAND_TQ, w), BF16)],
        compiler_params=_cparams(("parallel", "arbitrary")),
        name="band_prompt",
    )(bias.reshape(H_B // 2, 2 * QPAIR, WIN), q3, kv3, kv3, kv3, kv3)


def _band_sample_kernel(bc_ref, bn_ref, q_ref, kc_ref, vc_ref, kn_ref, vn_ref, o_ref, *, past):
    t = q_ref.shape[1]
    nband = kc_ref.shape[1]
    lane = lax.broadcasted_iota(jnp.int32, (t, LANES), 1)
    qpos_c = past + lax.broadcasted_iota(jnp.int32, (t, nband), 0)
    kpos_c = past - nband + lax.broadcasted_iota(jnp.int32, (t, nband), 1)
    qpos_n = past + lax.broadcasted_iota(jnp.int32, (t, t), 0)
    kpos_n = past + lax.broadcasted_iota(jnp.int32, (t, t), 1)

    def allowed(qpos, kpos):
        qc, kc = qpos // CHUNK, kpos // CHUNK
        return (kpos >= 0) & (kc <= qc) & (kc >= qc - BAND_CHUNKS)

    ok_c = allowed(qpos_c, kpos_c)
    ok_n = allowed(qpos_n, kpos_n)
    for hp in range(H_B // 2):
        cols = slice(hp * LANES, (hp + 1) * LANES)
        halves = []
        for sub, qm in enumerate(_split_halves(q_ref[0, :, cols])):
            hd = 2 * hp + sub
            pieces = [(kc_ref[0, :, cols].astype(BF16), vc_ref[0, :, cols].astype(BF16),
                       jnp.where(ok_c, bc_ref[hd], NEG)),
                      (kn_ref[0, :, cols].astype(BF16), vn_ref[0, :, cols].astype(BF16),
                       jnp.where(ok_n, bn_ref[hd], NEG))]
            halves.append(_pieces_attention(qm, pieces))
        o_ref[0, :, cols] = jnp.where(lane < DH_B, halves[0], halves[1]).astype(o_ref.dtype)


def _band_sample(q3, kv3, kc, vc, bias, past):
    b, t, _ = q3.shape
    nband = kc.shape[1]
    assert nband == BAND and t <= CHUNK
    w = H_B * DH_B
    bias_c = bias[:, :t, :nband]
    bias_n = bias[:, :t, nband:nband + t]
    blk = lambda col: pl.BlockSpec((1, t, w), lambda bi: (bi, 0, col))
    cache = pl.BlockSpec((1, nband, w), lambda bi: (bi, 0, 0))
    return pl.pallas_call(
        functools.partial(_band_sample_kernel, past=past),
        out_shape=jax.ShapeDtypeStruct((b, t, w), BF16),
        grid=(b,),
        in_specs=[
            pl.BlockSpec((H_B, t, nband), lambda bi: (0, 0, 0)),
            pl.BlockSpec((H_B, t, t), lambda bi: (0, 0, 0)),
            blk(1), cache, cache, blk(0), blk(1),
        ],
        out_specs=pl.BlockSpec((1, t, w), lambda bi: (bi, 0, 0)),
        compiler_params=_cparams(("parallel",)),
        name="band_sample",
    )(bias_c, bias_n, q3, kc, vc, kv3, kv3)


CONV_PAD = 8
MLSTM_L = 256


def _mlstm_kernel(cqk_ref, cv_ref, co_ref, cif_ref, conv0_ref, cw_ref, cb_ref, gb_ref, hg_ref,
                  c0_ref, n0_ref, m0_ref, hs_ref, c_ref, n_ref, m_ref, xext, *, L):
    @pl.when(pl.program_id(1) == 0)
    def _():
        xext[0:CONV_PAD, :] = conv0_ref[0]
        c_ref[...] = c0_ref[...]
        n_ref[...] = n0_ref[...]
        m_ref[...] = m0_ref[...]

    xext[CONV_PAD:CONV_PAD + L, :] = cqk_ref[0]
    base = CONV_PAD - (CONV_W - 1)
    u = 0.0
    for j in range(CONV_W):
        u = u + xext[base + j:base + j + L, :] * cw_ref[j:j + 1, :]
    u = cb_ref[...] + u
    u = u * jax.nn.sigmoid(u)
    xext[0:CONV_PAD, :] = xext[L:L + CONV_PAD, :]

    z = cif_ref[0] + gb_ref[...]
    lane = lax.broadcasted_iota(jnp.int32, (L, LANES), 1)
    lf = jnp.minimum(z, 0.0) - jnp.log1p(jnp.exp(-jnp.abs(z)))
    tr = lax.broadcasted_iota(jnp.int32, (L, L), 0)
    tc = lax.broadcasted_iota(jnp.int32, (L, L), 1)
    causal = tr >= tc
    b_all = jnp.dot(causal.astype(F32), lf, preferred_element_type=F32,
                    precision=lax.Precision.HIGHEST)
    sr = lax.broadcasted_iota(jnp.int32, (8, LANES), 0)
    sc = lax.broadcasted_iota(jnp.int32, (8, LANES), 1)
    sel = jnp.where(sr < H_C, jnp.where(sc == sr, 1.0, jnp.where(sc == sr + H_C, -1.0, 0.0)), 0.0)
    rt = lax.dot_general(sel, jnp.where(lane < H_C, z, b_all), NT_DIMS,
                         preferred_element_type=F32, precision=lax.Precision.HIGHEST)
    er = lax.broadcasted_iota(jnp.int32, (DH_C, DH_C), 0)
    ec = lax.broadcasted_iota(jnp.int32, (DH_C, DH_C), 1)
    eye = jnp.where(er == ec, 1.0, 0.0).astype(BF16)

    for h in range(H_C):
        cols = slice(h * DH_C, (h + 1) * DH_C)
        q = u[:, h * DH_C:(h + 1) * DH_C]
        k = u[:, D_C + h * DH_C:D_C + (h + 1) * DH_C] * (DH_C ** -0.5)
        v = cv_ref[0, :, cols]
        qb, kb_, vb_ = q.astype(BF16), k.astype(BF16), v.astype(BF16)
        b_col = b_all[:, H_C + h:H_C + h + 1]
        li_col = z[:, h:h + 1]
        m_prev = m_ref[0, h:h + 1, 0:1]
        cs = c_ref[0, h]
        ns = n_ref[0, h:h + 1, :]

        dmat = jnp.where(causal, b_col + rt[h:h + 1, :], NEG)
        inter = b_col + m_prev
        m_t = jnp.maximum(inter, jnp.max(dmat, axis=-1, keepdims=True))
        w_intra = jnp.exp(dmat - m_t)
        w_inter = jnp.exp(inter - m_t)
        a = w_intra * _nt(qb, kb_)
        num = (jnp.dot(a.astype(BF16), vb_, preferred_element_type=F32)
               + w_inter * _nt(qb, cs.astype(BF16)))
        den = jnp.sum(a, axis=-1, keepdims=True) + w_inter * jnp.sum(q * ns, axis=-1, keepdims=True)
        hh = num / jnp.maximum(jnp.abs(den), jnp.exp(-m_t))

        b_last = b_col[L - 1:L, :]
        g = b_last - b_col + li_col
        m_new = jnp.maximum(b_last + m_prev, jnp.max(g, axis=0, keepdims=True))
        ws = jnp.exp(g - m_new)
        decay = jnp.exp(b_last + m_prev - m_new)
        vwt = _nt(eye, (ws * v).astype(BF16)).astype(BF16)
        c_ref[0, h] = decay * cs + jnp.dot(vwt, kb_, preferred_element_type=F32)
        n_ref[0, h:h + 1, :] = decay * ns + jnp.sum(ws * k, axis=0, keepdims=True)
        m_ref[0, h:h + 1, :] = jnp.broadcast_to(m_new, (1, LANES))

        gate = jax.nn.sigmoid(co_ref[0, :, cols])
        hs_ref[0, :, cols] = (_rms(hh, hg_ref[...]) * gate).astype(hs_ref.dtype)


def _mlstm(cqk3, cvo3, cif3, conv0, conv_w, conv_b, gate_b, head_g, c0, n0, m0, L):
    b, t, _ = cqk3.shape
    nc = t // L
    blk = lambda col, w: pl.BlockSpec((1, L, w), lambda bi, c: (bi, c, col))
    const = lambda shape: pl.BlockSpec(shape, lambda bi, c: (0,) * len(shape))
    per_b = lambda shape: pl.BlockSpec((1,) + shape, lambda bi, c: (bi,) + (0,) * len(shape))
    return pl.pallas_call(
        functools.partial(_mlstm_kernel, L=L),
        out_shape=(jax.ShapeDtypeStruct((b, t, D_C), BF16),
                   jax.ShapeDtypeStruct((b, H_C, DH_C, DH_C), F32),
                   jax.ShapeDtypeStruct((b, H_C, DH_C), F32),
                   jax.ShapeDtypeStruct((b, H_C, LANES), F32)),
        grid=(b, nc),
        in_specs=[
            blk(0, 2 * D_C), blk(0, D_C), blk(1, D_C), blk(0, LANES),
            per_b((CONV_PAD, 2 * D_C)), const((CONV_W, 2 * D_C)), const((1, 2 * D_C)),
            const((1, LANES)), const((1, DH_C)),
            per_b((H_C, DH_C, DH_C)), per_b((H_C, DH_C)), per_b((H_C, LANES)),
        ],
        out_specs=(pl.BlockSpec((1, L, D_C), lambda bi, c: (bi, c, 0)),
                   per_b((H_C, DH_C, DH_C)), per_b((H_C, DH_C)), per_b((H_C, LANES))),
        scratch_shapes=[pltpu.VMEM((CONV_PAD + L, 2 * D_C), F32)],
        compiler_params=_cparams(("parallel", "arbitrary")),
        name="mlstm",
    )(cqk3, cvo3, cvo3, cif3, conv0, conv_w, conv_b, gate_b, head_g, c0, n0, m0)


def _mix_kernel(oa_ref, ob_ref, oc_ref, ga_ref, gb_ref, gc_ref, x_ref, wa_ref, wb_ref, wc_ref, wo_ref, o_ref):
    mixed = (ga_ref[...] * jnp.dot(oa_ref[...], wa_ref[...], preferred_element_type=F32)
             + gb_ref[...] * jnp.dot(ob_ref[...], wb_ref[...], preferred_element_type=F32)
             + gc_ref[...] * jnp.dot(oc_ref[...], wc_ref[...], preferred_element_type=F32))
    o_ref[...] = x_ref[...] + jnp.dot(mixed.astype(BF16), wo_ref[...], preferred_element_type=F32)


def _mix(oa, ob, oc, gates, x, l, wa, wb, wc, wo):
    m, d = x.shape
    tm = min(m, 512)
    row = lambda w, col=0: pl.BlockSpec((tm, w), lambda i: (i, col))
    full = lambda a: pl.BlockSpec((None,) + a.shape[1:], lambda i: (l, 0, 0))
    g0 = 0
    return pl.pallas_call(
        _mix_kernel,
        out_shape=jax.ShapeDtypeStruct((m, d), F32),
        grid=(m // tm,),
        in_specs=[row(oa.shape[1]), row(ob.shape[1]), row(oc.shape[1]),
                  row(d, g0), row(d, g0 + 1), row(d, g0 + 2), row(d),
                  full(wa), full(wb), full(wc), full(wo)],
        out_specs=row(d),
        compiler_params=_cparams(("parallel",)),
        name="mix",
    )(oa, ob, oc, gates, gates, gates, x, wa, wb, wc, wo)


def _cross_kernel(x_ref, g_ref, wq_ref, mk_ref, mv_ref, wo_ref, o_ref):
    x = x_ref[0]
    h = _rms(x, g_ref[...]).astype(BF16)
    q = (jnp.dot(h, wq_ref[...], preferred_element_type=F32) * (DH_M ** -0.5 * LOG2E)).astype(BF16)
    outs = []
    for hd in range(H_M):
        cols = slice(hd * DH_M, (hd + 1) * DH_M)
        s = _nt(q[:, cols], mk_ref[0, :, cols].astype(BF16))
        m = jnp.max(s, axis=-1, keepdims=True)
        e = jnp.exp2(s - m)
        l = jnp.sum(e, axis=-1, keepdims=True)
        o = jnp.dot(e.astype(BF16), mv_ref[0, :, cols].astype(BF16), preferred_element_type=F32) / l
        outs.append(o.astype(BF16))
    o = jnp.concatenate(outs, axis=1)
    o_ref[0] = x + jnp.dot(o, wo_ref[...], preferred_element_type=F32)


def _cross(x3, g, l, wq, mk, mv, wo):
    b, t, d = x3.shape
    tm = min(t, 512)
    nm = mk.shape[1]
    full = lambda a: pl.BlockSpec((None,) + a.shape[1:], lambda bi, i: (l, 0, 0))
    return pl.pallas_call(
        _cross_kernel,
        out_shape=jax.ShapeDtypeStruct((b, t, d), F32),
        grid=(b, t // tm),
        in_specs=[pl.BlockSpec((1, tm, d), lambda bi, i: (bi, i, 0)),
                  pl.BlockSpec(g.shape, lambda bi, i: (0, 0)), full(wq),
                  pl.BlockSpec((1, nm, d), lambda bi, i: (bi, 0, 0)),
                  pl.BlockSpec((1, nm, d), lambda bi, i: (bi, 0, 0)), full(wo)],
        out_specs=pl.BlockSpec((1, tm, d), lambda bi, i: (bi, i, 0)),
        compiler_params=_cparams(("parallel", "parallel")),
        name="cross",
    )(x3, g, wq, mk, mv, wo)


FFN_TF = 1408
FFN_TM = 512


def _ffn_kernel(x_ref, g_ref, wg_ref, wu_ref, wd_ref, gf_ref, o_ref, *, final_norm):
    x = x_ref[...]
    h = _rms(x, g_ref[...]).astype(BF16)
    chunks = [slice(c, c + FFN_TF) for c in range(0, D_FF, FFN_TF)]
    pre = [(jnp.dot(h, wg_ref[:, c], preferred_element_type=F32),
            jnp.dot(h, wu_ref[:, c], preferred_element_type=F32)) for c in chunks]
    y = x
    for c, (gate, up) in zip(chunks, pre):
        a = (gate * jax.nn.sigmoid(gate) * up).astype(BF16)
        y = y + jnp.dot(a, wd_ref[c, :], preferred_element_type=F32)
    if final_norm:
        y = _rms(y, gf_ref[...])
    o_ref[...] = y


def _ffn(x, g, l, wg, wu, wd, g_final, final_norm):
    m, d = x.shape
    tm = min(m, FFN_TM)
    resident = lambda a: pl.BlockSpec((None,) + a.shape[1:], lambda i: (l, 0, 0),
                                      pipeline_mode=pl.Buffered(1))
    return pl.pallas_call(
        functools.partial(_ffn_kernel, final_norm=final_norm),
        out_shape=jax.ShapeDtypeStruct((m, d), F32),
        grid=(m // tm,),
        in_specs=[
            pl.BlockSpec((tm, d), lambda i: (i, 0)),
            pl.BlockSpec((1, d), lambda i: (0, 0)),
            resident(wg), resident(wu), resident(wd),
            pl.BlockSpec((1, d), lambda i: (0, 0)),
        ],
        out_specs=pl.BlockSpec((tm, d), lambda i: (i, 0)),
        compiler_params=_cparams(("parallel",)),
        name="ffn",
    )(x, g, wg, wu, wd, g_final)


def _layer(x3, l, w, mem_k, mem_v, cache, final_norm, akv_all):
    b, t, d = x3.shape
    x2 = x3.reshape(b * t, d)
    row = lambda a: a.reshape(1, -1)
    lam_init = 0.8 - 0.6 * math.exp(-0.3 * l)

    gates, cqk, cif, q, ak_all, av_all, akv, bkv, cvo = _in_proj(
        x2, row(w['g_mix']), l, w['w_in'], w['w_if'], *akv_all, (b, t))
    q3 = q.reshape(b, t, SEG)
    bkv3 = bkv.reshape(b, t, SEG)
    cqk3 = cqk.reshape(b, t, 2 * D_C)
    keep = t if cache is not None else min(BAND, t)
    b_k = bkv3[:, t - keep:, :HALF].reshape(b, keep, H_B, DH_B)
    b_v = bkv3[:, t - keep:, HALF:].reshape(b, keep, H_B, DH_B)

    gate_b = jnp.concatenate([w['b_i'], w['b_f'], jnp.zeros((LANES - 2 * H_C,), F32)]).reshape(1, LANES)
    if cache is None:
        oa = _diff_prompt(q3, akv.reshape(b, t, SEG), w['lqk'], row(w['a_head_g']), lam_init)
        ob = _band_prompt(q3, bkv3, w['relbias'])
        conv0 = jnp.zeros((b, CONV_PAD, 2 * D_C), F32)
        c0 = jnp.zeros((b, H_C, DH_C, DH_C), F32)
        n0 = jnp.zeros((b, H_C, DH_C), F32)
        m0 = jnp.zeros((b, H_C, LANES), F32)
        L = min(t, MLSTM_L)
    else:
        past = cache['a_k'].shape[2]
        ak3 = ak_all.reshape(DEPTH * b, t, HALF)
        av3 = av_all.reshape(DEPTH * b, t, HALF)
        oa = _diff_sample(q3, ak3, av3, l, cache['a_k'], cache['a_v'],
                          w['lqk'], row(w['a_head_g']), lam_init)
        nband = cache['b_k'].shape[1]
        ob = _band_sample(q3, bkv3, cache['b_k'].reshape(b, nband, -1), cache['b_v'].reshape(b, nband, -1),
                          w['relbias'], past)
        conv0 = jnp.pad(cache['conv'], ((0, 0), (CONV_PAD - (CONV_W - 1), 0), (0, 0)))
        c0, n0 = cache['C'], cache['n']
        m0 = jnp.broadcast_to(cache['m'][:, :, None], (b, H_C, LANES))
        L = t
    oc, c_new, n_new, m_new = _mlstm(cqk3, cvo.reshape(b, t, SEG), cif.reshape(b, t, LANES), conv0,
                                     w['conv_w'], row(w['conv_b']), gate_b, row(w['c_head_g']), c0, n0, m0, L)
    assert t >= CONV_W - 1
    conv_new = cqk3[:, t - (CONV_W - 1):]

    x2 = _mix(oa.reshape(b * t, -1), ob.reshape(b * t, -1), oc.reshape(b * t, -1), gates, x2, l,
              w['w_up_a'], w['w_up_b'], w['w_up_c'], w['w_o'])
    x3 = _cross(x2.reshape(b, t, d), row(w['g_cross']), l, w['w_mq'], mem_k, mem_v, w['w_mo'])
    x2 = _ffn(x3.reshape(b * t, d), row(w['g_ffn']), l, w['w_ff_g'], w['w_ff_u'], w['w_ff_d'],
              row(w['g_final']), final_norm)
    return x2.reshape(b, t, d), (ak_all, av_all), (b_k, b_v, c_new, n_new, m_new[:, :, 0], conv_new)


def kernel(x_prompt, x_sample, cache_a_k, cache_a_v, cache_b_k, cache_b_v, state_c_C, state_c_n, state_c_m, state_c_conv, cache_mem_k, cache_mem_v, mem_prompt, g_mix, w_in, a_lq1, a_lk1, a_lq2, a_lk2, a_head_g, b_rel, c_conv_w, c_conv_b, c_b_i, c_b_f, c_head_g, w_up_a, w_up_b, w_up_c, w_o, g_cross, w_mq, w_mk, w_mv, w_mo, g_ffn, w_ff_g, w_ff_u, w_ff_d, g_final):
    xp, xs = x_prompt, x_sample
    bp = x_prompt.shape[0]
    n_mem = mem_prompt.shape[1]
    mem2 = mem_prompt.reshape(bp * n_mem, D_MODEL)
    new_p = [[] for _ in range(8)]
    new_s = [[] for _ in range(6)]
    akv_p = akv_s = (None, None)
    bf = lambda a: a.astype(BF16)
    wp, wif = _prep_w_in(w_in)
    wts = dict(w_up_a=bf(w_up_a), w_up_b=bf(w_up_b), w_up_c=bf(w_up_c), w_o=bf(w_o), w_mq=bf(w_mq),
               w_mo=bf(w_mo), w_ff_g=bf(w_ff_g), w_ff_u=bf(w_ff_u), w_ff_d=bf(w_ff_d))
    w_mk_b, w_mv_b = bf(w_mk), bf(w_mv)
    for l in range(DEPTH):
        w = dict(g_mix=g_mix[l], w_in=wp, w_if=wif,
                 lqk=jnp.stack([a_lq1[l], a_lk1[l], a_lq2[l], a_lk2[l]]),
                 a_head_g=a_head_g[l], relbias=_relbias(b_rel[l]), conv_w=c_conv_w[l], conv_b=c_conv_b[l],
                 b_i=c_b_i[l], b_f=c_b_f[l], c_head_g=c_head_g[l],
                 g_cross=g_cross[l], g_ffn=g_ffn[l], g_final=g_final, **wts)
        mk2 = _mem_proj(mem2, l, w_mk_b).reshape(bp, n_mem, D_MODEL)
        mv2 = _mem_proj(mem2, l, w_mv_b).reshape(bp, n_mem, D_MODEL)
        last = l == DEPTH - 1
        xp, akv_p, st_p = _layer(xp, l, w, mk2, mv2, None, last, akv_p)
        cache = dict(a_k=cache_a_k, a_v=cache_a_v, b_k=cache_b_k[l], b_v=cache_b_v[l],
                     C=state_c_C[l], n=state_c_n[l], m=state_c_m[l], conv=state_c_conv[l])
        bs = xs.shape[0]
        xs, akv_s, st_s = _layer(xs, l, w, cache_mem_k[l].reshape(bs, n_mem, D_MODEL),
                                 cache_mem_v[l].reshape(bs, n_mem, D_MODEL), cache, last, akv_s)
        mk4 = mk2.reshape(bp, n_mem, H_M, DH_M)
        mv4 = mv2.reshape(bp, n_mem, H_M, DH_M)
        for lst, a in zip(new_p, st_p + (mk4, mv4)):
            lst.append(a)
        for lst, a in zip(new_s, st_s):
            lst.append(a)
    heads = lambda a, b, t: a.reshape(DEPTH, b, t, H_A, DV_A)
    outs_p = [heads(a, bp, x_prompt.shape[1]) for a in akv_p] + [jnp.stack(a, 0) for a in new_p]
    outs_s = [heads(a, xs.shape[0], xs.shape[1]) for a in akv_s] + [jnp.stack(a, 0) for a in new_s]
    return (xp, xs) + tuple(outs_p) + tuple(outs_s)
```

```python
import functools
import math

import jax
import jax.numpy as jnp
from jax import lax
from jax.experimental import pallas as pl
from jax.experimental.pallas import tpu as pltpu

F32 = jnp.float32
BF16 = jnp.bfloat16

D_MODEL = 1024
DEPTH = 2
CHUNK = 64
EPS = 1e-6
NEG = -1e30
LOG2E = math.log2(math.e)
H_A = 4
DH_A = 64
DV_A = 2 * DH_A
H_B = 8
DH_B = 64
BAND_CHUNKS = 8
BAND = BAND_CHUNKS * CHUNK
REL_CLIP = 128
H_C = 4
DH_C = 128
D_C = H_C * DH_C
CONV_W = 4
H_M = 4
DH_M = D_MODEL // H_M
D_FF = -(-8 * D_MODEL // (3 * 256)) * 256

LANES = 128
VMEM_LIMIT = 48 * 1024 * 1024

SEG = 1024
HALF = SEG // 2
GATE_TILES = 3
N_TILES = 8
IN_SIZES = (H_A * DH_A,) * 4 + (H_A * DV_A,) + (H_B * DH_B,) * 3 + (2 * D_C, D_C, D_C, 2 * H_C, 3 * D_MODEL)

ALIBI_SLOPES = tuple(2.0 ** (-8.0 * (i + 1) / H_A) for i in range(H_A))

NT_DIMS = (((1,), (1,)), ((), ()))


def _cparams(sem):
    return pltpu.CompilerParams(dimension_semantics=sem, vmem_limit_bytes=VMEM_LIMIT)


def _rms(x, g):
    ms = jnp.mean(x * x, axis=-1, keepdims=True)
    return x * lax.rsqrt(ms + EPS) * g


def _nt(a, b):
    return lax.dot_general(a, b, NT_DIMS, preferred_element_type=F32)


def _prep_w_in_kernel(w_ref, o_ref, oif_ref):
    offs = [0]
    for size in IN_SIZES:
        offs.append(offs[-1] + size)
    seg = lambda i: w_ref[:, offs[i]:offs[i + 1]]
    a_q1, a_q2, a_k1, a_k2, a_v, b_q, b_k, b_v, c_qk, c_v, c_o = (seg(i) for i in range(11))
    g = seg(12)

    def put(col, val):
        o_ref[:, col:col + val.shape[1]] = val.astype(BF16)

    put(0, g)
    put(GATE_TILES * SEG, c_qk)
    for h in range(H_A):
        hs = slice(h * DH_A, (h + 1) * DH_A)
        base = h * 2 * DH_A
        put(4 * SEG + base, a_q1[:, hs] * (DH_A ** -0.5))
        put(4 * SEG + base + DH_A, a_q2[:, hs] * (DH_A ** -0.5))
        put(5 * SEG + base, a_k1[:, hs])
        put(5 * SEG + base + DH_A, a_k2[:, hs])
    put(4 * SEG + HALF, b_q * (DH_B ** -0.5))
    put(5 * SEG + HALF, a_v)
    put(6 * SEG, b_k)
    put(6 * SEG + HALF, b_v)
    put(7 * SEG, c_v)
    put(7 * SEG + HALF, c_o)
    tile = w_ref[:, offs[11]:offs[11] + LANES]
    lane = lax.broadcasted_iota(jnp.int32, tile.shape, 1)
    oif_ref[...] = jnp.where(lane < 2 * H_C, tile, 0.0).astype(BF16)


def _prep_w_in(w):
    depth, k, n = w.shape
    tk = 256
    n_out = N_TILES * SEG
    return pl.pallas_call(
        _prep_w_in_kernel,
        out_shape=(jax.ShapeDtypeStruct((depth, k, n_out), BF16),
                   jax.ShapeDtypeStruct((depth, k, LANES), BF16)),
        grid=(depth, k // tk),
        in_specs=[pl.BlockSpec((None, tk, n), lambda l, i: (l, i, 0))],
        out_specs=(pl.BlockSpec((None, tk, n_out), lambda l, i: (l, i, 0)),
                   pl.BlockSpec((None, tk, LANES), lambda l, i: (l, i, 0))),
        compiler_params=_cparams(("parallel", "parallel")),
        name="prep_w_in",
    )(w)


IN_PROJ_TM = 256


def _in_proj_kernel(*refs, aliased, heads5d):
    x_ref, g_ref, w_ref, wif_ref = refs[:4]
    og_ref, ocqk_ref, ocif_ref, oq_ref, oak_ref, oav_ref, oakv_ref, obkv_ref, ocvo_ref = refs[6 if aliased else 4:]
    h = _rms(x_ref[...], g_ref[...]).astype(BF16)
    cols = lambda j, lo=0, hi=SEG: jnp.dot(h, w_ref[:, j * SEG + lo:j * SEG + hi], preferred_element_type=F32)

    ocif_ref[...] = jnp.dot(h, wif_ref[...], preferred_element_type=F32)
    for j in range(GATE_TILES):
        og_ref[:, j * SEG:(j + 1) * SEG] = jax.nn.sigmoid(cols(j)).astype(og_ref.dtype)
    ocqk_ref[...] = cols(GATE_TILES)
    oq_ref[...] = (cols(4) * LOG2E).astype(oq_ref.dtype)
    for out_ref, off in ((oak_ref, 0), (oav_ref, HALF)):
        a = cols(5, off, off + HALF)
        oakv_ref[:, off:off + HALF] = a.astype(oakv_ref.dtype)
        if heads5d:
            for hd in range(H_A):
                out_ref[0, 0, :, hd, :] = a[:, hd * DV_A:(hd + 1) * DV_A]
        else:
            out_ref[0] = a
    obkv_ref[...] = cols(6)
    ocvo_ref[...] = cols(7)


def _in_proj(x, g, l, wp, wif, ak_all, av_all, bt):
    m, k = x.shape
    tm = min(m, IN_PROJ_TM)
    aliased = ak_all is not None
    b, t = bt
    heads5d = t % tm == 0
    if heads5d:
        stacked = jax.ShapeDtypeStruct((DEPTH, b, t, H_A, DV_A), F32)
        layer_row = pl.BlockSpec((1, 1, tm, H_A, DV_A), lambda i: (l, i // (t // tm), i % (t // tm), 0, 0))
    else:
        stacked = jax.ShapeDtypeStruct((DEPTH, m, HALF), F32)
        layer_row = pl.BlockSpec((1, tm, HALF), lambda i: (l, i, 0))
    resident = lambda a: pl.BlockSpec((None,) + a.shape[1:], lambda i: (l, 0, 0), pipeline_mode=pl.Buffered(1))
    in_specs = [pl.BlockSpec((tm, k), lambda i: (i, 0)), pl.BlockSpec((1, k), lambda i: (0, 0)),
                resident(wp), resident(wif)]
    args = [x, g, wp, wif]
    if aliased:
        in_specs += [pl.BlockSpec(memory_space=pl.ANY)] * 2
        args += [ak_all, av_all]
    row = lambda w: pl.BlockSpec((tm, w), lambda i: (i, 0))
    sds = jax.ShapeDtypeStruct
    return pl.pallas_call(
        functools.partial(_in_proj_kernel, aliased=aliased, heads5d=heads5d),
        out_shape=(sds((m, GATE_TILES * SEG), BF16), sds((m, SEG), F32), sds((m, LANES), F32),
                   sds((m, SEG), BF16), stacked, stacked, sds((m, SEG), BF16),
                   sds((m, SEG), F32), sds((m, SEG), F32)),
        grid=(m // tm,),
        in_specs=in_specs,
        out_specs=(row(GATE_TILES * SEG), row(SEG), row(LANES), row(SEG), layer_row, layer_row, row(SEG),
                   row(SEG), row(SEG)),
        input_output_aliases={4: 4, 5: 5} if aliased else {},
        compiler_params=_cparams(("parallel",)),
        name="in_proj",
    )(*args)


def _mem_proj_kernel(x_ref, w_ref, o_ref):
    o_ref[...] = jnp.dot(x_ref[...].astype(BF16), w_ref[...], preferred_element_type=F32)


def _mem_proj(x, l, w):
    m, k = x.shape
    n = w.shape[2]
    tm = min(m, 1024)
    return pl.pallas_call(
        _mem_proj_kernel,
        out_shape=jax.ShapeDtypeStruct((m, n), F32),
        grid=(m // tm,),
        in_specs=[pl.BlockSpec((tm, k), lambda i: (i, 0)), pl.BlockSpec((None, k, n), lambda i: (l, 0, 0))],
        out_specs=pl.BlockSpec((tm, n), lambda i: (i, 0)),
        compiler_params=_cparams(("parallel",)),
        name="mem_proj",
    )(x, w)


def _online_updates(jobs):
    staged = []
    for s, _, (acc, m, l) in jobs:
        slabs = [s[:, j * LANES:(j + 1) * LANES] for j in range(s.shape[1] // LANES)]
        m_old = m[...]
        m_new = jnp.maximum(m_old, jnp.max(functools.reduce(jnp.maximum, slabs), axis=-1, keepdims=True))
        alpha = jnp.exp2(m_old - m_new)
        ps = [jnp.exp2(c - m_new) for c in slabs]
        l[...] = alpha * l[...] + jnp.sum(functools.reduce(jnp.add, ps), axis=-1, keepdims=True)
        m[...] = m_new
        staged.append((alpha, jnp.concatenate(ps, axis=1).astype(BF16)))
    for (alpha, p), (_, v, (acc, m, l)) in zip(staged, jobs):
        acc[...] = alpha * acc[...] + jnp.dot(p, v, preferred_element_type=F32)


def _split_halves(q):
    lane = lax.broadcasted_iota(jnp.int32, q.shape, 1)
    lo = jnp.where(lane < DH_A, q, 0.0).astype(BF16)
    hi = jnp.where(lane >= DH_A, q, 0.0).astype(BF16)
    return lo, hi


def _lambda(lqk, lam_init):
    e1 = jnp.exp(jnp.sum(lqk[0:1, :] * lqk[1:2, :], axis=-1, keepdims=True))
    e2 = jnp.exp(jnp.sum(lqk[2:3, :] * lqk[3:4, :], axis=-1, keepdims=True))
    return e1 - e2 + lam_init


def _diff_prompt_kernel(slopes_ref, lqk_ref, hg_ref, q_ref, k_ref, v_ref, o_ref,
                        diag_bias, acc1, acc2, m1, l1, m2, l2, *, tq, lam_init):
    h = pl.program_id(1)
    qi = pl.program_id(2)
    slope = slopes_ref[h] * LOG2E
    kb, vb = k_ref.at[0], v_ref.at[0]
    q0 = pl.multiple_of(qi * tq, tq)

    @pl.when(qi == 0)
    def _():
        r = lax.broadcasted_iota(jnp.int32, (tq, tq), 0)
        c = lax.broadcasted_iota(jnp.int32, (tq, tq), 1)
        bias = slope * (r - jnp.abs(r - c)).astype(F32)
        diag_bias[...] = jnp.where((c // CHUNK) <= (r // CHUNK), bias, NEG)

    qa, qb = _split_halves(q_ref[0])
    states = ((acc1, m1, l1), (acc2, m2, l2))
    for acc, m, l in states:
        m[...] = jnp.full(m.shape, NEG, F32)
        l[...] = jnp.zeros(l.shape, F32)
        acc[...] = jnp.zeros(acc.shape, F32)

    def full_tiles(k_starts):
        jobs = []
        for k0 in k_starts:
            k = kb[pl.ds(k0, tq), :]
            v = vb[pl.ds(k0, tq), :]
            kpos = (k0 - q0) + lax.broadcasted_iota(jnp.int32, (1, tq), 1)
            bias = slope * kpos.astype(F32)
            jobs += [(_nt(qa, k) + bias, v, states[0]), (_nt(qb, k) + bias, v, states[1])]
        _online_updates(jobs)

    def tile_pair(it, carry):
        k0 = pl.multiple_of(it * (2 * tq), 2 * tq)
        full_tiles([k0, pl.multiple_of(k0 + tq, tq)])
        return carry

    lax.fori_loop(0, qi // 2, tile_pair, 0)

    @pl.when(qi % 2 == 1)
    def _():
        full_tiles([pl.multiple_of((qi - 1) * tq, tq)])

    half = tq // 2
    jobs = []
    for rows, nk in ((slice(0, half), half), (slice(half, tq), tq)):
        k = kb[pl.ds(q0, nk), :]
        v = vb[pl.ds(q0, nk), :]
        bias = diag_bias[rows, 0:nk]
        jobs += [(_nt(qm[rows], k) + bias, v, tuple(ref.at[rows] for ref in st))
                 for qm, st in ((qa, states[0]), (qb, states[1]))]
    _online_updates(jobs)

    lam = _lambda(lqk_ref[...], lam_init)
    o = acc1[...] / l1[...] - lam * (acc2[...] / l2[...])
    o_ref[0] = (_rms(o, hg_ref[...]) * (1.0 - lam_init)).astype(o_ref.dtype)


def _diff_prompt(q3, kv3, lqk, head_g, lam_init):
    b, t, _ = q3.shape
    tq = min(t, 512)
    slopes = jnp.asarray(ALIBI_SLOPES, F32)
    return pl.pallas_call(
        functools.partial(_diff_prompt_kernel, tq=tq, lam_init=lam_init),
        out_shape=jax.ShapeDtypeStruct((b, t, H_A * DV_A), BF16),
        grid=(b, H_A, t // tq),
        in_specs=[
            pl.BlockSpec(memory_space=pltpu.SMEM),
            pl.BlockSpec((4, DH_A), lambda bi, h, qi: (0, 0)),
            pl.BlockSpec((1, DV_A), lambda bi, h, qi: (0, 0)),
            pl.BlockSpec((1, tq, LANES), lambda bi, h, qi: (bi, qi, h)),
            pl.BlockSpec((1, t, LANES), lambda bi, h, qi: (bi, 0, h)),
            pl.BlockSpec((1, t, LANES), lambda bi, h, qi: (bi, 0, H_A + h)),
        ],
        out_specs=pl.BlockSpec((1, tq, LANES), lambda bi, h, qi: (bi, qi, h)),
        scratch_shapes=[
            pltpu.VMEM((tq, tq), F32),
            pltpu.VMEM((tq, LANES), F32), pltpu.VMEM((tq, LANES), F32),
            pltpu.VMEM((tq, LANES), F32), pltpu.VMEM((tq, LANES), F32),
            pltpu.VMEM((tq, LANES), F32), pltpu.VMEM((tq, LANES), F32),
        ],
        compiler_params=_cparams(("parallel", "parallel", "arbitrary")),
        name="diff_prompt",
    )(slopes, lqk, head_g, q3, kv3, kv3)


def _pieces_attention(qm, pieces):
    ss = [_nt(qm, k) + bias for k, _, bias in pieces]
    m = functools.reduce(jnp.maximum, [jnp.max(s, axis=-1, keepdims=True) for s in ss])
    es = [jnp.exp2(s - m) for s in ss]
    l = functools.reduce(jnp.add, [jnp.sum(e, axis=-1, keepdims=True) for e in es])
    o = functools.reduce(jnp.add, [jnp.dot(e.astype(BF16), v, preferred_element_type=F32)
                                   for e, (_, v, _) in zip(es, pieces)])
    return o / l


def _diff_sample_kernel(lqk_ref, hg_ref, q_ref, kc_ref, vc_ref, kn_ref, vn_ref, o_ref, *, lam_init):
    t = q_ref.shape[1]
    past = kc_ref.shape[1]
    lam = _lambda(lqk_ref[...], lam_init)
    rc = lax.broadcasted_iota(jnp.int32, (t, past), 0)
    cc = lax.broadcasted_iota(jnp.int32, (t, past), 1)
    rn = lax.broadcasted_iota(jnp.int32, (t, t), 0)
    cn = lax.broadcasted_iota(jnp.int32, (t, t), 1)
    dist_c = jnp.abs(past + rc - cc).astype(F32)
    dist_n = jnp.abs(rn - cn).astype(F32)
    ok_c = (cc // CHUNK) <= ((past + rc) // CHUNK)
    ok_n = ((past + cn) // CHUNK) <= ((past + rn) // CHUNK)
    for h in range(H_A):
        cols = slice(h * LANES, (h + 1) * LANES)
        bias_c = jnp.where(ok_c, (-ALIBI_SLOPES[h] * LOG2E) * dist_c, NEG)
        bias_n = jnp.where(ok_n, (-ALIBI_SLOPES[h] * LOG2E) * dist_n, NEG)
        pieces = [(kc_ref[0, :, h, :].astype(BF16), vc_ref[0, :, h, :].astype(BF16), bias_c),
                  (kn_ref[0, :, cols].astype(BF16), vn_ref[0, :, cols].astype(BF16), bias_n)]
        qa, qb = _split_halves(q_ref[0, :, cols])
        o = _pieces_attention(qa, pieces) - lam * _pieces_attention(qb, pieces)
        o_ref[0, :, cols] = (_rms(o, hg_ref[...]) * (1.0 - lam_init)).astype(o_ref.dtype)


def _diff_sample(q3, kn3, vn3, l, kc, vc, lqk, head_g, lam_init):
    b, t, _ = q3.shape
    past = kc.shape[2]
    w = H_A * DV_A
    new = pl.BlockSpec((1, t, w), lambda bi: (l * b + bi, 0, 0))
    cache = pl.BlockSpec((None, 1, past, H_A, DV_A), lambda bi: (l, bi, 0, 0, 0))
    return pl.pallas_call(
        functools.partial(_diff_sample_kernel, lam_init=lam_init),
        out_shape=jax.ShapeDtypeStruct((b, t, w), BF16),
        grid=(b,),
        in_specs=[
            pl.BlockSpec((4, DH_A), lambda bi: (0, 0)),
            pl.BlockSpec((1, DV_A), lambda bi: (0, 0)),
            pl.BlockSpec((1, t, w), lambda bi: (bi, 0, 0)), cache, cache, new, new,
        ],
        out_specs=pl.BlockSpec((1, t, w), lambda bi: (bi, 0, 0)),
        compiler_params=_cparams(("parallel",)),
        name="diff_sample",
    )(lqk, head_g, q3, kc, vc, kn3, vn3)


QPAIR = 2 * CHUNK
WIN = BAND + QPAIR


def _relbias_kernel(tab_ref, o_ref):
    h = pl.program_id(0)
    r = lax.broadcasted_iota(jnp.int32, (QPAIR, WIN), 0)
    j = lax.broadcasted_iota(jnp.int32, (QPAIR, WIN), 1)
    rel = jnp.clip(r - j + BAND, -REL_CLIP, REL_CLIP) + REL_CLIP
    qc = r // CHUNK
    kc = j // CHUNK - BAND_CHUNKS
    allowed = (kc <= qc) & (kc >= qc - BAND_CHUNKS)

    far = BAND - REL_CLIP
    near_rel = rel[:, far:]

    near = jnp.zeros((QPAIR, WIN - far), F32)
    for t in range(2 * REL_CLIP + 1):
        near = jnp.where(near_rel == t, tab_ref[h, t], near)
    bias = jnp.concatenate([jnp.full((QPAIR, far), tab_ref[h, 2 * REL_CLIP], F32), near], axis=1)
    o_ref[0] = jnp.where(allowed, bias * LOG2E, NEG)


def _relbias(table):
    return pl.pallas_call(
        _relbias_kernel,
        out_shape=jax.ShapeDtypeStruct((H_B, QPAIR, WIN), F32),
        grid=(H_B,),
        in_specs=[pl.BlockSpec(memory_space=pltpu.SMEM)],
        out_specs=pl.BlockSpec((1, QPAIR, WIN), lambda h: (h, 0, 0)),
        compiler_params=_cparams(("arbitrary",)),
        name="relbias",
    )(table)


BAND_TQ = 512
BAND_UNROLL = 1


def _band_prompt_kernel(bias_ref, q_ref, kp_ref, kc_ref, vp_ref, vc_ref, o_ref, kcat, vcat):
    i = pl.program_id(1)
    kcat[0:BAND_TQ, :] = kp_ref[0].astype(BF16)
    kcat[BAND_TQ:, :] = kc_ref[0].astype(BF16)
    vcat[0:BAND_TQ, :] = vp_ref[0].astype(BF16)
    vcat[BAND_TQ:, :] = vc_ref[0].astype(BF16)
    lane = lax.broadcasted_iota(jnp.int32, (QPAIR, LANES), 1)

    def pair_blocks(it, carry, *, first_block):
        starts = [pl.multiple_of((it * BAND_UNROLL + u) * QPAIR, QPAIR) for u in range(BAND_UNROLL)]
        scores = []
        for r0 in starts:
            q = q_ref[0, pl.ds(r0, QPAIR), :]
            if first_block:
                kpos = r0 - BAND_TQ + lax.broadcasted_iota(jnp.int32, (1, WIN), 1)
                before_start = jnp.where(kpos >= 0, 0.0, NEG)
            for hp in range(H_B // 2):
                cols = slice(hp * LANES, (hp + 1) * LANES)
                qs = jnp.concatenate(_split_halves(q[:, cols]), axis=0)
                s = _nt(qs, kcat[pl.ds(r0, WIN), cols]) + bias_ref[hp]
                scores.append(s + before_start if first_block else s)
        probs = []
        for s in scores:
            e = jnp.exp2(s - jnp.max(s, axis=-1, keepdims=True))
            probs.append((e.astype(BF16), jnp.sum(e, axis=-1, keepdims=True)))
        for u, r0 in enumerate(starts):
            outs = []
            for hp in range(H_B // 2):
                e, l = probs[u * (H_B // 2) + hp]
                cols = slice(hp * LANES, (hp + 1) * LANES)
                o = jnp.dot(e, vcat[pl.ds(r0, WIN), cols], preferred_element_type=F32) / l
                outs.append(jnp.where(lane < DH_B, o[:QPAIR], o[QPAIR:]))
            o_ref[0, pl.ds(r0, QPAIR), :] = jnp.concatenate(outs, axis=1).astype(o_ref.dtype)
        return carry

    trips = BAND_TQ // QPAIR // BAND_UNROLL

    @pl.when(i == 0)
    def _():
        lax.fori_loop(0, trips, functools.partial(pair_blocks, first_block=True), 0)

    @pl.when(i > 0)
    def _():
        lax.fori_loop(0, trips, functools.partial(pair_blocks, first_block=False), 0)


def _band_prompt(q3, kv3, bias):
    b, t, _ = q3.shape
    assert t % BAND_TQ == 0
    w = H_B * DH_B
    cq, ck, cv = 1, 0, 1
    prev = lambda col: pl.BlockSpec((1, BAND_TQ, w), lambda bi, i: (bi, jnp.maximum(i - 1, 0), col))
    cur = lambda col: pl.BlockSpec((1, BAND_TQ, w), lambda bi, i: (bi, i, col))
    return pl.pallas_call(
        _band_prompt_kernel,
        out_shape=jax.ShapeDtypeStruct((b, t, w), BF16),
        grid=(b, t // BAND_TQ),
        in_specs=[
            pl.BlockSpec((H_B // 2, 2 * QPAIR, WIN), lambda bi, i: (0, 0, 0)),
            cur(cq), prev(ck), cur(ck), prev(cv), cur(cv),
        ],
        out_specs=pl.BlockSpec((1, BAND_TQ, w), lambda bi, i: (bi, i, 0)),
        scratch_shapes=[pltpu.VMEM((2 * BAND_TQ, w), BF16), pltpu.VMEM((2 * BAND_TQ, w), BF16)],
        compiler_params=_cparams(("parallel", "arbitrary")),
        name="band_prompt",
    )(bias.reshape(H_B // 2, 2 * QPAIR, WIN), q3, kv3, kv3, kv3, kv3)


def _band_sample_kernel(bc_ref, bn_ref, q_ref, kc_ref, vc_ref, kn_ref, vn_ref, o_ref, *, past):
    t = q_ref.shape[1]
    nband = kc_ref.shape[1]
    lane = lax.broadcasted_iota(jnp.int32, (t, LANES), 1)
    qpos_c = past + lax.broadcasted_iota(jnp.int32, (t, nband), 0)
    kpos_c = past - nband + lax.broadcasted_iota(jnp.int32, (t, nband), 1)
    qpos_n = past + lax.broadcasted_iota(jnp.int32, (t, t), 0)
    kpos_n = past + lax.broadcasted_iota(jnp.int32, (t, t), 1)

    def allowed(qpos, kpos):
        qc, kc = qpos // CHUNK, kpos // CHUNK
        return (kpos >= 0) & (kc <= qc) & (kc >= qc - BAND_CHUNKS)

    ok_c = allowed(qpos_c, kpos_c)
    ok_n = allowed(qpos_n, kpos_n)
    for hp in range(H_B // 2):
        cols = slice(hp * LANES, (hp + 1) * LANES)
        halves = []
        for sub, qm in enumerate(_split_halves(q_ref[0, :, cols])):
            hd = 2 * hp + sub
            pieces = [(kc_ref[0, :, cols].astype(BF16), vc_ref[0, :, cols].astype(BF16),
                       jnp.where(ok_c, bc_ref[hd], NEG)),
                      (kn_ref[0, :, cols].astype(BF16), vn_ref[0, :, cols].astype(BF16),
                       jnp.where(ok_n, bn_ref[hd], NEG))]
            halves.append(_pieces_attention(qm, pieces))
        o_ref[0, :, cols] = jnp.where(lane < DH_B, halves[0], halves[1]).astype(o_ref.dtype)


def _band_sample(q3, kv3, kc, vc, bias, past):
    b, t, _ = q3.shape
    nband = kc.shape[1]
    assert nband == BAND and t <= CHUNK
    w = H_B * DH_B
    bias_c = bias[:, :t, :nband]
    bias_n = bias[:, :t, nband:nband + t]
    blk = lambda col: pl.BlockSpec((1, t, w), lambda bi: (bi, 0, col))
    cache = pl.BlockSpec((1, nband, w), lambda bi: (bi, 0, 0))
    return pl.pallas_call(
        functools.partial(_band_sample_kernel, past=past),
        out_shape=jax.ShapeDtypeStruct((b, t, w), BF16),
        grid=(b,),
        in_specs=[
            pl.BlockSpec((H_B, t, nband), lambda bi: (0, 0, 0)),
            pl.BlockSpec((H_B, t, t), lambda bi: (0, 0, 0)),
            blk(1), cache, cache, blk(0), blk(1),
        ],
        out_specs=pl.BlockSpec((1, t, w), lambda bi: (bi, 0, 0)),
        compiler_params=_cparams(("parallel",)),
        name="band_sample",
    )(bias_c, bias_n, q3, kc, vc, kv3, kv3)


CONV_PAD = 8
MLSTM_L = 256


def _mlstm_kernel(cqk_ref, cv_ref, co_ref, cif_ref, conv0_ref, cw_ref, cb_ref, gb_ref, hg_ref,
                  c0_ref, n0_ref, m0_ref, hs_ref, c_ref, n_ref, m_ref, xext, *, L):
    @pl.when(pl.program_id(1) == 0)
    def _():
        xext[0:CONV_PAD, :] = conv0_ref[0]
        c_ref[...] = c0_ref[...]
        n_ref[...] = n0_ref[...]
        m_ref[...] = m0_ref[...]

    xext[CONV_PAD:CONV_PAD + L, :] = cqk_ref[0]
    base = CONV_PAD - (CONV_W - 1)
    u = 0.0
    for j in range(CONV_W):
        u = u + xext[base + j:base + j + L, :] * cw_ref[j:j + 1, :]
    u = cb_ref[...] + u
    u = u * jax.nn.sigmoid(u)
    xext[0:CONV_PAD, :] = xext[L:L + CONV_PAD, :]

    z = cif_ref[0] + gb_ref[...]
    lane = lax.broadcasted_iota(jnp.int32, (L, LANES), 1)
    lf = jnp.minimum(z, 0.0) - jnp.log1p(jnp.exp(-jnp.abs(z)))
    tr = lax.broadcasted_iota(jnp.int32, (L, L), 0)
    tc = lax.broadcasted_iota(jnp.int32, (L, L), 1)
    causal = tr >= tc
    b_all = jnp.dot(causal.astype(F32), lf, preferred_element_type=F32,
                    precision=lax.Precision.HIGHEST)
    sr = lax.broadcasted_iota(jnp.int32, (8, LANES), 0)
    sc = lax.broadcasted_iota(jnp.int32, (8, LANES), 1)
    sel = jnp.where(sr < H_C, jnp.where(sc == sr, 1.0, jnp.where(sc == sr + H_C, -1.0, 0.0)), 0.0)
    rt = lax.dot_general(sel, jnp.where(lane < H_C, z, b_all), NT_DIMS,
                         preferred_element_type=F32, precision=lax.Precision.HIGHEST)
    er = lax.broadcasted_iota(jnp.int32, (DH_C, DH_C), 0)
    ec = lax.broadcasted_iota(jnp.int32, (DH_C, DH_C), 1)
    eye = jnp.where(er == ec, 1.0, 0.0).astype(BF16)

    for h in range(H_C):
        cols = slice(h * DH_C, (h + 1) * DH_C)
        q = u[:, h * DH_C:(h + 1) * DH_C]
        k = u[:, D_C + h * DH_C:D_C + (h + 1) * DH_C] * (DH_C ** -0.5)
        v = cv_ref[0, :, cols]
        qb, kb_, vb_ = q.astype(BF16), k.astype(BF16), v.astype(BF16)
        b_col = b_all[:, H_C + h:H_C + h + 1]
        li_col = z[:, h:h + 1]
        m_prev = m_ref[0, h:h + 1, 0:1]
        cs = c_ref[0, h]
        ns = n_ref[0, h:h + 1, :]

        dmat = jnp.where(causal, b_col + rt[h:h + 1, :], NEG)
        inter = b_col + m_prev
        m_t = jnp.maximum(inter, jnp.max(dmat, axis=-1, keepdims=True))
        w_intra = jnp.exp(dmat - m_t)
        w_inter = jnp.exp(inter - m_t)
        a = w_intra * _nt(qb, kb_)
        num = (jnp.dot(a.astype(BF16), vb_, preferred_element_type=F32)
               + w_inter * _nt(qb, cs.astype(BF16)))
        den = jnp.sum(a, axis=-1, keepdims=True) + w_inter * jnp.sum(q * ns, axis=-1, keepdims=True)
        hh = num / jnp.maximum(jnp.abs(den), jnp.exp(-m_t))

        b_last = b_col[L - 1:L, :]
        g = b_last - b_col + li_col
        m_new = jnp.maximum(b_last + m_prev, jnp.max(g, axis=0, keepdims=True))
        ws = jnp.exp(g - m_new)
        decay = jnp.exp(b_last + m_prev - m_new)
        vwt = _nt(eye, (ws * v).astype(BF16)).astype(BF16)
        c_ref[0, h] = decay * cs + jnp.dot(vwt, kb_, preferred_element_type=F32)
        n_ref[0, h:h + 1, :] = decay * ns + jnp.sum(ws * k, axis=0, keepdims=True)
        m_ref[0, h:h + 1, :] = jnp.broadcast_to(m_new, (1, LANES))

        gate = jax.nn.sigmoid(co_ref[0, :, cols])
        hs_ref[0, :, cols] = (_rms(hh, hg_ref[...]) * gate).astype(hs_ref.dtype)


def _mlstm(cqk3, cvo3, cif3, conv0, conv_w, conv_b, gate_b, head_g, c0, n0, m0, L):
    b, t, _ = cqk3.shape
    nc = t // L
    blk = lambda col, w: pl.BlockSpec((1, L, w), lambda bi, c: (bi, c, col))
    const = lambda shape: pl.BlockSpec(shape, lambda bi, c: (0,) * len(shape))
    per_b = lambda shape: pl.BlockSpec((1,) + shape, lambda bi, c: (bi,) + (0,) * len(shape))
    return pl.pallas_call(
        functools.partial(_mlstm_kernel, L=L),
        out_shape=(jax.ShapeDtypeStruct((b, t, D_C), BF16),
                   jax.ShapeDtypeStruct((b, H_C, DH_C, DH_C), F32),
                   jax.ShapeDtypeStruct((b, H_C, DH_C), F32),
                   jax.ShapeDtypeStruct((b, H_C, LANES), F32)),
        grid=(b, nc),
        in_specs=[
            blk(0, 2 * D_C), blk(0, D_C), blk(1, D_C), blk(0, LANES),
            per_b((CONV_PAD, 2 * D_C)), const((CONV_W, 2 * D_C)), const((1, 2 * D_C)),
            const((1, LANES)), const((1, DH_C)),
            per_b((H_C, DH_C, DH_C)), per_b((H_C, DH_C)), per_b((H_C, LANES)),
        ],
        out_specs=(pl.BlockSpec((1, L, D_C), lambda bi, c: (bi, c, 0)),
                   per_b((H_C, DH_C, DH_C)), per_b((H_C, DH_C)), per_b((H_C, LANES))),
        scratch_shapes=[pltpu.VMEM((CONV_PAD + L, 2 * D_C), F32)],
        compiler_params=_cparams(("parallel", "arbitrary")),
        name="mlstm",
    )(cqk3, cvo3, cvo3, cif3, conv0, conv_w, conv_b, gate_b, head_g, c0, n0, m0)


def _mix_kernel(oa_ref, ob_ref, oc_ref, ga_ref, gb_ref, gc_ref, x_ref, wa_ref, wb_ref, wc_ref, wo_ref, o_ref):
    mixed = (ga_ref[...] * jnp.dot(oa_ref[...], wa_ref[...], preferred_element_type=F32)
             + gb_ref[...] * jnp.dot(ob_ref[...], wb_ref[...], preferred_element_type=F32)
             + gc_ref[...] * jnp.dot(oc_ref[...], wc_ref[...], preferred_element_type=F32))
    o_ref[...] = x_ref[...] + jnp.dot(mixed.astype(BF16), wo_ref[...], preferred_element_type=F32)


def _mix(oa, ob, oc, gates, x, l, wa, wb, wc, wo):
    m, d = x.shape
    tm = min(m, 512)
    row = lambda w, col=0: pl.BlockSpec((tm, w), lambda i: (i, col))
    full = lambda a: pl.BlockSpec((None,) + a.shape[1:], lambda i: (l, 0, 0))
    g0 = 0
    return pl.pallas_call(
        _mix_kernel,
        out_shape=jax.ShapeDtypeStruct((m, d), F32),
        grid=(m // tm,),
        in_specs=[row(oa.shape[1]), row(ob.shape[1]), row(oc.shape[1]),
                  row(d, g0), row(d, g0 + 1), row(d, g0 + 2), row(d),
                  full(wa), full(wb), full(wc), full(wo)],
        out_specs=row(d),
        compiler_params=_cparams(("parallel",)),
        name="mix",
    )(oa, ob, oc, gates, gates, gates, x, wa, wb, wc, wo)


def _cross_kernel(x_ref, g_ref, wq_ref, mk_ref, mv_ref, wo_ref, o_ref):
    x = x_ref[0]
    h = _rms(x, g_ref[...]).astype(BF16)
    q = (jnp.dot(h, wq_ref[...], preferred_element_type=F32) * (DH_M ** -0.5 * LOG2E)).astype(BF16)
    outs = []
    for hd in range(H_M):
        cols = slice(hd * DH_M, (hd + 1) * DH_M)
        s = _nt(q[:, cols], mk_ref[0, :, cols].astype(BF16))
        m = jnp.max(s, axis=-1, keepdims=True)
        e = jnp.exp2(s - m)
        l = jnp.sum(e, axis=-1, keepdims=True)
        o = jnp.dot(e.astype(BF16), mv_ref[0, :, cols].astype(BF16), preferred_element_type=F32) / l
        outs.append(o.astype(BF16))
    o = jnp.concatenate(outs, axis=1)
    o_ref[0] = x + jnp.dot(o, wo_ref[...], preferred_element_type=F32)


def _cross(x3, g, l, wq, mk, mv, wo):
    b, t, d = x3.shape
    tm = min(t, 512)
    nm = mk.shape[1]
    full = lambda a: pl.BlockSpec((None,) + a.shape[1:], lambda bi, i: (l, 0, 0))
    return pl.pallas_call(
        _cross_kernel,
        out_shape=jax.ShapeDtypeStruct((b, t, d), F32),
        grid=(b, t // tm),
        in_specs=[pl.BlockSpec((1, tm, d), lambda bi, i: (bi, i, 0)),
                  pl.BlockSpec(g.shape, lambda bi, i: (0, 0)), full(wq),
                  pl.BlockSpec((1, nm, d), lambda bi, i: (bi, 0, 0)),
                  pl.BlockSpec((1, nm, d), lambda bi, i: (bi, 0, 0)), full(wo)],
        out_specs=pl.BlockSpec((1, tm, d), lambda bi, i: (bi, i, 0)),
        compiler_params=_cparams(("parallel", "parallel")),
        name="cross",
    )(x3, g, wq, mk, mv, wo)


FFN_TF = 1408
FFN_TM = 512


def _ffn_kernel(x_ref, g_ref, wg_ref, wu_ref, wd_ref, gf_ref, o_ref, *, final_norm):
    x = x_ref[...]
    h = _rms(x, g_ref[...]).astype(BF16)
    chunks = [slice(c, c + FFN_TF) for c in range(0, D_FF, FFN_TF)]
    pre = [(jnp.dot(h, wg_ref[:, c], preferred_element_type=F32),
            jnp.dot(h, wu_ref[:, c], preferred_element_type=F32)) for c in chunks]
    y = x
    for c, (gate, up) in zip(chunks, pre):
        a = (gate * jax.nn.sigmoid(gate) * up).astype(BF16)
        y = y + jnp.dot(a, wd_ref[c, :], preferred_element_type=F32)
    if final_norm:
        y = _rms(y, gf_ref[...])
    o_ref[...] = y


def _ffn(x, g, l, wg, wu, wd, g_final, final_norm):
    m, d = x.shape
    tm = min(m, FFN_TM)
    resident = lambda a: pl.BlockSpec((None,) + a.shape[1:], lambda i: (l, 0, 0),
                                      pipeline_mode=pl.Buffered(1))
    return pl.pallas_call(
        functools.partial(_ffn_kernel, final_norm=final_norm),
        out_shape=jax.ShapeDtypeStruct((m, d), F32),
        grid=(m // tm,),
        in_specs=[
            pl.BlockSpec((tm, d), lambda i: (i, 0)),
            pl.BlockSpec((1, d), lambda i: (0, 0)),
            resident(wg), resident(wu), resident(wd),
            pl.BlockSpec((1, d), lambda i: (0, 0)),
        ],
        out_specs=pl.BlockSpec((tm, d), lambda i: (i, 0)),
        compiler_params=_cparams(("parallel",)),
        name="ffn",
    )(x, g, wg, wu, wd, g_final)


def _layer(x3, l, w, mem_k, mem_v, cache, final_norm, akv_all):
    b, t, d = x3.shape
    x2 = x3.reshape(b * t, d)
    row = lambda a: a.reshape(1, -1)
    lam_init = 0.8 - 0.6 * math.exp(-0.3 * l)

    gates, cqk, cif, q, ak_all, av_all, akv, bkv, cvo = _in_proj(
        x2, row(w['g_mix']), l, w['w_in'], w['w_if'], *akv_all, (b, t))
    q3 = q.reshape(b, t, SEG)
    bkv3 = bkv.reshape(b, t, SEG)
    cqk3 = cqk.reshape(b, t, 2 * D_C)
    keep = t if cache is not None else min(BAND, t)
    b_k = bkv3[:, t - keep:, :HALF].reshape(b, keep, H_B, DH_B)
    b_v = bkv3[:, t - keep:, HALF:].reshape(b, keep, H_B, DH_B)

    gate_b = jnp.concatenate([w['b_i'], w['b_f'], jnp.zeros((LANES - 2 * H_C,), F32)]).reshape(1, LANES)
    if cache is None:
        oa = _diff_prompt(q3, akv.reshape(b, t, SEG), w['lqk'], row(w['a_head_g']), lam_init)
        ob = _band_prompt(q3, bkv3, w['relbias'])
        conv0 = jnp.zeros((b, CONV_PAD, 2 * D_C), F32)
        c0 = jnp.zeros((b, H_C, DH_C, DH_C), F32)
        n0 = jnp.zeros((b, H_C, DH_C), F32)
        m0 = jnp.zeros((b, H_C, LANES), F32)
        L = min(t, MLSTM_L)
    else:
        past = cache['a_k'].shape[2]
        ak3 = ak_all.reshape(DEPTH * b, t, HALF)
        av3 = av_all.reshape(DEPTH * b, t, HALF)
        oa = _diff_sample(q3, ak3, av3, l, cache['a_k'], cache['a_v'],
                          w['lqk'], row(w['a_head_g']), lam_init)
        nband = cache['b_k'].shape[1]
        ob = _band_sample(q3, bkv3, cache['b_k'].reshape(b, nband, -1), cache['b_v'].reshape(b, nband, -1),
                          w['relbias'], past)
        conv0 = jnp.pad(cache['conv'], ((0, 0), (CONV_PAD - (CONV_W - 1), 0), (0, 0)))
        c0, n0 = cache['C'], cache['n']
        m0 = jnp.broadcast_to(cache['m'][:, :, None], (b, H_C, LANES))
        L = t
    oc, c_new, n_new, m_new = _mlstm(cqk3, cvo.reshape(b, t, SEG), cif.reshape(b, t, LANES), conv0,
                                     w['conv_w'], row(w['conv_b']), gate_b, row(w['c_head_g']), c0, n0, m0, L)
    assert t >= CONV_W - 1
    conv_new = cqk3[:, t - (CONV_W - 1):]

    x2 = _mix(oa.reshape(b * t, -1), ob.reshape(b * t, -1), oc.reshape(b * t, -1), gates, x2, l,
              w['w_up_a'], w['w_up_b'], w['w_up_c'], w['w_o'])
    x3 = _cross(x2.reshape(b, t, d), row(w['g_cross']), l, w['w_mq'], mem_k, mem_v, w['w_mo'])
    x2 = _ffn(x3.reshape(b * t, d), row(w['g_ffn']), l, w['w_ff_g'], w['w_ff_u'], w['w_ff_d'],
              row(w['g_final']), final_norm)
    return x2.reshape(b, t, d), (ak_all, av_all), (b_k, b_v, c_new, n_new, m_new[:, :, 0], conv_new)


def kernel(x_prompt, x_sample, cache_a_k, cache_a_v, cache_b_k, cache_b_v, state_c_C, state_c_n, state_c_m, state_c_conv, cache_mem_k, cache_mem_v, mem_prompt, g_mix, w_in, a_lq1, a_lk1, a_lq2, a_lk2, a_head_g, b_rel, c_conv_w, c_conv_b, c_b_i, c_b_f, c_head_g, w_up_a, w_up_b, w_up_c, w_o, g_cross, w_mq, w_mk, w_mv, w_mo, g_ffn, w_ff_g, w_ff_u, w_ff_d, g_final):
    xp, xs = x_prompt, x_sample
    bp = x_prompt.shape[0]
    n_mem = mem_prompt.shape[1]
    mem2 = mem_prompt.reshape(bp * n_mem, D_MODEL)
    new_p = [[] for _ in range(8)]
    new_s = [[] for _ in range(6)]
    akv_p = akv_s = (None, None)
    bf = lambda a: a.astype(BF16)
    wp, wif = _prep_w_in(w_in)
    wts = dict(w_up_a=bf(w_up_a), w_up_b=bf(w_up_b), w_up_c=bf(w_up_c), w_o=bf(w_o), w_mq=bf(w_mq),
               w_mo=bf(w_mo), w_ff_g=bf(w_ff_g), w_ff_u=bf(w_ff_u), w_ff_d=bf(w_ff_d))
    w_mk_b, w_mv_b = bf(w_mk), bf(w_mv)
    for l in range(DEPTH):
        w = dict(g_mix=g_mix[l], w_in=wp, w_if=wif,
                 lqk=jnp.stack([a_lq1[l], a_lk1[l], a_lq2[l], a_lk2[l]]),
                 a_head_g=a_head_g[l], relbias=_relbias(b_rel[l]), conv_w=c_conv_w[l], conv_b=c_conv_b[l],
                 b_i=c_b_i[l], b_f=c_b_f[l], c_head_g=c_head_g[l],
                 g_cross=g_cross[l], g_ffn=g_ffn[l], g_final=g_final, **wts)
        mk2 = _mem_proj(mem2, l, w_mk_b).reshape(bp, n_mem, D_MODEL)
        mv2 = _mem_proj(mem2, l, w_mv_b).reshape(bp, n_mem, D_MODEL)
        last = l == DEPTH - 1
        xp, akv_p, st_p = _layer(xp, l, w, mk2, mv2, None, last, akv_p)
        cache = dict(a_k=cache_a_k, a_v=cache_a_v, b_k=cache_b_k[l], b_v=cache_b_v[l],
                     C=state_c_C[l], n=state_c_n[l], m=state_c_m[l], conv=state_c_conv[l])
        bs = xs.shape[0]
        xs, akv_s, st_s = _layer(xs, l, w, cache_mem_k[l].reshape(bs, n_mem, D_MODEL),
                                 cache_mem_v[l].reshape(bs, n_mem, D_MODEL), cache, last, akv_s)
        mk4 = mk2.reshape(bp, n_mem, H_M, DH_M)
        mv4 = mv2.reshape(bp, n_mem, H_M, DH_M)
        for lst, a in zip(new_p, st_p + (mk4, mv4)):
            lst.append(a)
        for lst, a in zip(new_s, st_s):
            lst.append(a)
    heads = lambda a, b, t: a.reshape(DEPTH, b, t, H_A, DV_A)
    outs_p = [heads(a, bp, x_prompt.shape[1]) for a in akv_p] + [jnp.stack(a, 0) for a in new_p]
    outs_s = [heads(a, xs.shape[0], xs.shape[1]) for a in akv_s] + [jnp.stack(a, 0) for a in new_s]
    return (xp, xs) + tuple(outs_p) + tuple(outs_s)
```

```python
import functools
import math

import jax
import jax.numpy as jnp
from jax import lax
from jax.experimental import pallas as pl
from jax.experimental.pallas import tpu as pltpu

F32 = jnp.float32
BF16 = jnp.bfloat16

D_MODEL = 1024
DEPTH = 2
CHUNK = 64
EPS = 1e-6
NEG = -1e30
LOG2E = math.log2(math.e)
H_A = 4
DH_A = 64
DV_A = 2 * DH_A
H_B = 8
DH_B = 64
BAND_CHUNKS = 8
BAND = BAND_CHUNKS * CHUNK
REL_CLIP = 128
H_C = 4
DH_C = 128
D_C = H_C * DH_C
CONV_W = 4
H_M = 4
DH_M = D_MODEL // H_M
D_FF = -(-8 * D_MODEL // (3 * 256)) * 256

LANES = 128
VMEM_LIMIT = 48 * 1024 * 1024

SEG = 1024
HALF = SEG // 2
GATE_TILES = 3
N_TILES = 8
IN_SIZES = (H_A * DH_A,) * 4 + (H_A * DV_A,) + (H_B * DH_B,) * 3 + (2 * D_C, D_C, D_C, 2 * H_C, 3 * D_MODEL)

ALIBI_SLOPES = tuple(2.0 ** (-8.0 * (i + 1) / H_A) for i in range(H_A))

NT_DIMS = (((1,), (1,)), ((), ()))


def _cparams(sem):
    return pltpu.CompilerParams(dimension_semantics=sem, vmem_limit_bytes=VMEM_LIMIT)


def _rms(x, g):
    ms = jnp.mean(x * x, axis=-1, keepdims=True)
    return x * lax.rsqrt(ms + EPS) * g


def _nt(a, b):
    return lax.dot_general(a, b, NT_DIMS, preferred_element_type=F32)


def _prep_w_in_kernel(w_ref, o_ref, oif_ref):
    offs = [0]
    for size in IN_SIZES:
        offs.append(offs[-1] + size)
    seg = lambda i: w_ref[:, offs[i]:offs[i + 1]]
    a_q1, a_q2, a_k1, a_k2, a_v, b_q, b_k, b_v, c_qk, c_v, c_o = (seg(i) for i in range(11))
    g = seg(12)

    def put(col, val):
        o_ref[:, col:col + val.shape[1]] = val.astype(BF16)

    put(0, g)
    put(GATE_TILES * SEG, c_qk)
    for h in range(H_A):
        hs = slice(h * DH_A, (h + 1) * DH_A)
        base = h * 2 * DH_A
        put(4 * SEG + base, a_q1[:, hs] * (DH_A ** -0.5))
        put(4 * SEG + base + DH_A, a_q2[:, hs] * (DH_A ** -0.5))
        put(5 * SEG + base, a_k1[:, hs])
        put(5 * SEG + base + DH_A, a_k2[:, hs])
    put(4 * SEG + HALF, b_q * (DH_B ** -0.5))
    put(5 * SEG + HALF, a_v)
    put(6 * SEG, b_k)
    put(6 * SEG + HALF, b_v)
    put(7 * SEG, c_v)
    put(7 * SEG + HALF, c_o)
    tile = w_ref[:, offs[11]:offs[11] + LANES]
    lane = lax.broadcasted_iota(jnp.int32, tile.shape, 1)
    oif_ref[...] = jnp.where(lane < 2 * H_C, tile, 0.0).astype(BF16)


def _prep_w_in(w):
    depth, k, n = w.shape
    tk = 256
    n_out = N_TILES * SEG
    return pl.pallas_call(
        _prep_w_in_kernel,
        out_shape=(jax.ShapeDtypeStruct((depth, k, n_out), BF16),
                   jax.ShapeDtypeStruct((depth, k, LANES), BF16)),
        grid=(depth, k // tk),
        in_specs=[pl.BlockSpec((None, tk, n), lambda l, i: (l, i, 0))],
        out_specs=(pl.BlockSpec((None, tk, n_out), lambda l, i: (l, i, 0)),
                   pl.BlockSpec((None, tk, LANES), lambda l, i: (l, i, 0))),
        compiler_params=_cparams(("parallel", "parallel")),
        name="prep_w_in",
    )(w)


IN_PROJ_TM = 256


def _in_proj_kernel(*refs, aliased, heads5d):
    x_ref, g_ref, w_ref, wif_ref = refs[:4]
    og_ref, ocqk_ref, ocif_ref, oq_ref, oak_ref, oav_ref, oakv_ref, obkv_ref, ocvo_ref = refs[6 if aliased else 4:]
    h = _rms(x_ref[...], g_ref[...]).astype(BF16)
    cols = lambda j, lo=0, hi=SEG: jnp.dot(h, w_ref[:, j * SEG + lo:j * SEG + hi], preferred_element_type=F32)

    ocif_ref[...] = jnp.dot(h, wif_ref[...], preferred_element_type=F32)
    for j in range(GATE_TILES):
        og_ref[:, j * SEG:(j + 1) * SEG] = jax.nn.sigmoid(cols(j)).astype(og_ref.dtype)
    ocqk_ref[...] = cols(GATE_TILES)
    oq_ref[...] = (cols(4) * LOG2E).astype(oq_ref.dtype)
    for out_ref, off in ((oak_ref, 0), (oav_ref, HALF)):
        a = cols(5, off, off + HALF)
        oakv_ref[:, off:off + HALF] = a.astype(oakv_ref.dtype)
        if heads5d:
            out_ref[0, 0] = a.reshape(a.shape[0], H_A, DV_A)
        else:
            out_ref[0] = a
    obkv_ref[...] = cols(6)
    ocvo_ref[...] = cols(7)


def _in_proj(x, g, l, wp, wif, ak_all, av_all, bt):
    m, k = x.shape
    tm = min(m, IN_PROJ_TM)
    aliased = ak_all is not None
    b, t = bt
    heads5d = t % tm == 0
    if heads5d:
        stacked = jax.ShapeDtypeStruct((DEPTH, b, t, H_A, DV_A), F32)
        layer_row = pl.BlockSpec((1, 1, tm, H_A, DV_A), lambda i: (l, i // (t // tm), i % (t // tm), 0, 0))
    else:
        stacked = jax.ShapeDtypeStruct((DEPTH, m, HALF), F32)
        layer_row = pl.BlockSpec((1, tm, HALF), lambda i: (l, i, 0))
    resident = lambda a: pl.BlockSpec((None,) + a.shape[1:], lambda i: (l, 0, 0), pipeline_mode=pl.Buffered(1))
    in_specs = [pl.BlockSpec((tm, k), lambda i: (i, 0)), pl.BlockSpec((1, k), lambda i: (0, 0)),
                resident(wp), resident(wif)]
    args = [x, g, wp, wif]
    if aliased:
        in_specs += [pl.BlockSpec(memory_space=pl.ANY)] * 2
        args += [ak_all, av_all]
    row = lambda w: pl.BlockSpec((tm, w), lambda i: (i, 0))
    sds = jax.ShapeDtypeStruct
    return pl.pallas_call(
        functools.partial(_in_proj_kernel, aliased=aliased, heads5d=heads5d),
        out_shape=(sds((m, GATE_TILES * SEG), BF16), sds((m, SEG), F32), sds((m, LANES), F32),
                   sds((m, SEG), BF16), stacked, stacked, sds((m, SEG), BF16),
                   sds((m, SEG), F32), sds((m, SEG), F32)),
        grid=(m // tm,),
        in_specs=in_specs,
        out_specs=(row(GATE_TILES * SEG), row(SEG), row(LANES), row(SEG), layer_row, layer_row, row(SEG),
                   row(SEG), row(SEG)),
        input_output_aliases={4: 4, 5: 5} if aliased else {},
        compiler_params=_cparams(("parallel",)),
        name="in_proj",
    )(*args)


def _mem_proj_kernel(x_ref, w_ref, o_ref):
    o_ref[...] = jnp.dot(x_ref[...].astype(BF16), w_ref[...], preferred_element_type=F32)


def _mem_proj(x, l, w):
    m, k = x.shape
    n = w.shape[2]
    tm = min(m, 1024)
    return pl.pallas_call(
        _mem_proj_kernel,
        out_shape=jax.ShapeDtypeStruct((m, n), F32),
        grid=(m // tm,),
        in_specs=[pl.BlockSpec((tm, k), lambda i: (i, 0)), pl.BlockSpec((None, k, n), lambda i: (l, 0, 0))],
        out_specs=pl.BlockSpec((tm, n), lambda i: (i, 0)),
        compiler_params=_cparams(("parallel",)),
        name="mem_proj",
    )(x, w)


def _online_updates(jobs):
    staged = []
    for s, _, (acc, m, l) in jobs:
        slabs = [s[:, j * LANES:(j + 1) * LANES] for j in range(s.shape[1] // LANES)]
        m_old = m[...]
        m_new = jnp.maximum(m_old, jnp.max(functools.reduce(jnp.maximum, slabs), axis=-1, keepdims=True))
        alpha = jnp.exp2(m_old - m_new)
        ps = [jnp.exp2(c - m_new) for c in slabs]
        l[...] = alpha * l[...] + jnp.sum(functools.reduce(jnp.add, ps), axis=-1, keepdims=True)
        m[...] = m_new
        staged.append((alpha, jnp.concatenate(ps, axis=1).astype(BF16)))
    for (alpha, p), (_, v, (acc, m, l)) in zip(staged, jobs):
        acc[...] = alpha * acc[...] + jnp.dot(p, v, preferred_element_type=F32)


def _split_halves(q):
    lane = lax.broadcasted_iota(jnp.int32, q.shape, 1)
    lo = jnp.where(lane < DH_A, q, 0.0).astype(BF16)
    hi = jnp.where(lane >= DH_A, q, 0.0).astype(BF16)
    return lo, hi


def _lambda(lqk, lam_init):
    e1 = jnp.exp(jnp.sum(lqk[0:1, :] * lqk[1:2, :], axis=-1, keepdims=True))
    e2 = jnp.exp(jnp.sum(lqk[2:3, :] * lqk[3:4, :], axis=-1, keepdims=True))
    return e1 - e2 + lam_init


def _diff_prompt_kernel(slopes_ref, lqk_ref, hg_ref, q_ref, k_ref, v_ref, o_ref,
                        diag_bias, acc1, acc2, m1, l1, m2, l2, *, tq, lam_init):
    h = pl.program_id(1)
    qi = pl.program_id(2)
    slope = slopes_ref[h] * LOG2E
    kb, vb = k_ref.at[0], v_ref.at[0]
    q0 = pl.multiple_of(qi * tq, tq)

    @pl.when(qi == 0)
    def _():
        r = lax.broadcasted_iota(jnp.int32, (tq, tq), 0)
        c = lax.broadcasted_iota(jnp.int32, (tq, tq), 1)
        bias = slope * (r - jnp.abs(r - c)).astype(F32)
        diag_bias[...] = jnp.where((c // CHUNK) <= (r // CHUNK), bias, NEG)

    qa, qb = _split_halves(q_ref[0])
    states = ((acc1, m1, l1), (acc2, m2, l2))
    for acc, m, l in states:
        m[...] = jnp.full(m.shape, NEG, F32)
        l[...] = jnp.zeros(l.shape, F32)
        acc[...] = jnp.zeros(acc.shape, F32)

    def full_tiles(k_starts):
        jobs = []
        for k0 in k_starts:
            k = kb[pl.ds(k0, tq), :]
            v = vb[pl.ds(k0, tq), :]
            kpos = (k0 - q0) + lax.broadcasted_iota(jnp.int32, (1, tq), 1)
            bias = slope * kpos.astype(F32)
            jobs += [(_nt(qa, k) + bias, v, states[0]), (_nt(qb, k) + bias, v, states[1])]
        _online_updates(jobs)

    def tile_pair(it, carry):
        k0 = pl.multiple_of(it * (2 * tq), 2 * tq)
        full_tiles([k0, pl.multiple_of(k0 + tq, tq)])
        return carry

    lax.fori_loop(0, qi // 2, tile_pair, 0)

    @pl.when(qi % 2 == 1)
    def _():
        full_tiles([pl.multiple_of((qi - 1) * tq, tq)])

    half = tq // 2
    jobs = []
    for rows, nk in ((slice(0, half), half), (slice(half, tq), tq)):
        k = kb[pl.ds(q0, nk), :]
        v = vb[pl.ds(q0, nk), :]
        bias = diag_bias[rows, 0:nk]
        jobs += [(_nt(qm[rows], k) + bias, v, tuple(ref.at[rows] for ref in st))
                 for qm, st in ((qa, states[0]), (qb, states[1]))]
    _online_updates(jobs)

    lam = _lambda(lqk_ref[...], lam_init)
    o = acc1[...] / l1[...] - lam * (acc2[...] / l2[...])
    o_ref[0] = (_rms(o, hg_ref[...]) * (1.0 - lam_init)).astype(o_ref.dtype)


def _diff_prompt(q3, kv3, lqk, head_g, lam_init):
    b, t, _ = q3.shape
    tq = min(t, 512)
    slopes = jnp.asarray(ALIBI_SLOPES, F32)
    return pl.pallas_call(
        functools.partial(_diff_prompt_kernel, tq=tq, lam_init=lam_init),
        out_shape=jax.ShapeDtypeStruct((b, t, H_A * DV_A), BF16),
        grid=(b, H_A, t // tq),
        in_specs=[
            pl.BlockSpec(memory_space=pltpu.SMEM),
            pl.BlockSpec((4, DH_A), lambda bi, h, qi: (0, 0)),
            pl.BlockSpec((1, DV_A), lambda bi, h, qi: (0, 0)),
            pl.BlockSpec((1, tq, LANES), lambda bi, h, qi: (bi, qi, h)),
            pl.BlockSpec((1, t, LANES), lambda bi, h, qi: (bi, 0, h)),
            pl.BlockSpec((1, t, LANES), lambda bi, h, qi: (bi, 0, H_A + h)),
        ],
        out_specs=pl.BlockSpec((1, tq, LANES), lambda bi, h, qi: (bi, qi, h)),
        scratch_shapes=[
            pltpu.VMEM((tq, tq), F32),
            pltpu.VMEM((tq, LANES), F32), pltpu.VMEM((tq, LANES), F32),
            pltpu.VMEM((tq, LANES), F32), pltpu.VMEM((tq, LANES), F32),
            pltpu.VMEM((tq, LANES), F32), pltpu.VMEM((tq, LANES), F32),
        ],
        compiler_params=_cparams(("parallel", "parallel", "arbitrary")),
        name="diff_prompt",
    )(slopes, lqk, head_g, q3, kv3, kv3)


def _pieces_attention(qm, pieces):
    ss = [_nt(qm, k) + bias for k, _, bias in pieces]
    m = functools.reduce(jnp.maximum, [jnp.max(s, axis=-1, keepdims=True) for s in ss])
    es = [jnp.exp2(s - m) for s in ss]
    l = functools.reduce(jnp.add, [jnp.sum(e, axis=-1, keepdims=True) for e in es])
    o = functools.reduce(jnp.add, [jnp.dot(e.astype(BF16), v, preferred_element_type=F32)
                                   for e, (_, v, _) in zip(es, pieces)])
    return o / l


def _diff_sample_kernel(lqk_ref, hg_ref, q_ref, kc_ref, vc_ref, kn_ref, vn_ref, o_ref, *, lam_init):
    t = q_ref.shape[1]
    past = kc_ref.shape[1]
    lam = _lambda(lqk_ref[...], lam_init)
    rc = lax.broadcasted_iota(jnp.int32, (t, past), 0)
    cc = lax.broadcasted_iota(jnp.int32, (t, past), 1)
    rn = lax.broadcasted_iota(jnp.int32, (t, t), 0)
    cn = lax.broadcasted_iota(jnp.int32, (t, t), 1)
    dist_c = jnp.abs(past + rc - cc).astype(F32)
    dist_n = jnp.abs(rn - cn).astype(F32)
    ok_c = (cc // CHUNK) <= ((past + rc) // CHUNK)
    ok_n = ((past + cn) // CHUNK) <= ((past + rn) // CHUNK)
    kc = kc_ref[0].reshape(past, H_A * DV_A).astype(BF16)
    vc = vc_ref[0].reshape(past, H_A * DV_A).astype(BF16)
    for h in range(H_A):
        cols = slice(h * LANES, (h + 1) * LANES)
        bias_c = jnp.where(ok_c, (-ALIBI_SLOPES[h] * LOG2E) * dist_c, NEG)
        bias_n = jnp.where(ok_n, (-ALIBI_SLOPES[h] * LOG2E) * dist_n, NEG)
        pieces = [(kc[:, cols], vc[:, cols], bias_c),
                  (kn_ref[0, :, cols].astype(BF16), vn_ref[0, :, cols].astype(BF16), bias_n)]
        qa, qb = _split_halves(q_ref[0, :, cols])
        o = _pieces_attention(qa, pieces) - lam * _pieces_attention(qb, pieces)
        o_ref[0, :, cols] = (_rms(o, hg_ref[...]) * (1.0 - lam_init)).astype(o_ref.dtype)


def _diff_sample(q3, kn3, vn3, l, kc, vc, lqk, head_g, lam_init):
    b, t, _ = q3.shape
    past = kc.shape[2]
    w = H_A * DV_A
    new = pl.BlockSpec((1, t, w), lambda bi: (l * b + bi, 0, 0))
    cache = pl.BlockSpec((None, 1, past, H_A, DV_A), lambda bi: (l, bi, 0, 0, 0))
    return pl.pallas_call(
        functools.partial(_diff_sample_kernel, lam_init=lam_init),
        out_shape=jax.ShapeDtypeStruct((b, t, w), BF16),
        grid=(b,),
        in_specs=[
            pl.BlockSpec((4, DH_A), lambda bi: (0, 0)),
            pl.BlockSpec((1, DV_A), lambda bi: (0, 0)),
            pl.BlockSpec((1, t, w), lambda bi: (bi, 0, 0)), cache, cache, new, new,
        ],
        out_specs=pl.BlockSpec((1, t, w), lambda bi: (bi, 0, 0)),
        compiler_params=_cparams(("parallel",)),
        name="diff_sample",
    )(lqk, head_g, q3, kc, vc, kn3, vn3)


QPAIR = 2 * CHUNK
WIN = BAND + QPAIR


def _relbias_kernel(tab_ref, o_ref):
    h = pl.program_id(0)
    r = lax.broadcasted_iota(jnp.int32, (QPAIR, WIN), 0)
    j = lax.broadcasted_iota(jnp.int32, (QPAIR, WIN), 1)
    rel = jnp.clip(r - j + BAND, -REL_CLIP, REL_CLIP) + REL_CLIP
    qc = r // CHUNK
    kc = j // CHUNK - BAND_CHUNKS
    allowed = (kc <= qc) & (kc >= qc - BAND_CHUNKS)

    far = BAND - REL_CLIP
    near_rel = rel[:, far:]

    near = jnp.zeros((QPAIR, WIN - far), F32)
    for t in range(2 * REL_CLIP + 1):
        near = jnp.where(near_rel == t, tab_ref[h, t], near)
    bias = jnp.concatenate([jnp.full((QPAIR, far), tab_ref[h, 2 * REL_CLIP], F32), near], axis=1)
    o_ref[0] = jnp.where(allowed, bias * LOG2E, NEG)


def _relbias(table):
    return pl.pallas_call(
        _relbias_kernel,
        out_shape=jax.ShapeDtypeStruct((H_B, QPAIR, WIN), F32),
        grid=(H_B,),
        in_specs=[pl.BlockSpec(memory_space=pltpu.SMEM)],
        out_specs=pl.BlockSpec((1, QPAIR, WIN), lambda h: (h, 0, 0)),
        compiler_params=_cparams(("arbitrary",)),
        name="relbias",
    )(table)


BAND_TQ = 512
BAND_UNROLL = 1


def _band_prompt_kernel(bias_ref, q_ref, kp_ref, kc_ref, vp_ref, vc_ref, o_ref, kcat, vcat):
    i = pl.program_id(1)
    kcat[0:BAND_TQ, :] = kp_ref[0].astype(BF16)
    kcat[BAND_TQ:, :] = kc_ref[0].astype(BF16)
    vcat[0:BAND_TQ, :] = vp_ref[0].astype(BF16)
    vcat[BAND_TQ:, :] = vc_ref[0].astype(BF16)
    lane = lax.broadcasted_iota(jnp.int32, (QPAIR, LANES), 1)

    def pair_blocks(it, carry, *, first_block):
        starts = [pl.multiple_of((it * BAND_UNROLL + u) * QPAIR, QPAIR) for u in range(BAND_UNROLL)]
        scores = []
        for r0 in starts:
            q = q_ref[0, pl.ds(r0, QPAIR), :]
            if first_block:
                kpos = r0 - BAND_TQ + lax.broadcasted_iota(jnp.int32, (1, WIN), 1)
                before_start = jnp.where(kpos >= 0, 0.0, NEG)
            for hp in range(H_B // 2):
                cols = slice(hp * LANES, (hp + 1) * LANES)
                qs = jnp.concatenate(_split_halves(q[:, cols]), axis=0)
                s = _nt(qs, kcat[pl.ds(r0, WIN), cols]) + bias_ref[hp]
                scores.append(s + before_start if first_block else s)
        probs = []
        for s in scores:
            e = jnp.exp2(s - jnp.max(s, axis=-1, keepdims=True))
            probs.append((e.astype(BF16), jnp.sum(e, axis=-1, keepdims=True)))
        for u, r0 in enumerate(starts):
            outs = []
            for hp in range(H_B // 2):
                e, l = probs[u * (H_B // 2) + hp]
                cols = slice(hp * LANES, (hp + 1) * LANES)
                o = jnp.dot(e, vcat[pl.ds(r0, WIN), cols], preferred_element_type=F32) / l
                outs.append(jnp.where(lane < DH_B, o[:QPAIR], o[QPAIR:]))
            o_ref[0, pl.ds(r0, QPAIR), :] = jnp.concatenate(outs, axis=1).astype(o_ref.dtype)
        return carry

    trips = BAND_TQ // QPAIR // BAND_UNROLL

    @pl.when(i == 0)
    def _():
        lax.fori_loop(0, trips, functools.partial(pair_blocks, first_block=True), 0)

    @pl.when(i > 0)
    def _():
        lax.fori_loop(0, trips, functools.partial(pair_blocks, first_block=False), 0)


def _band_prompt(q3, kv3, bias):
    b, t, _ = q3.shape
    assert t % BAND_TQ == 0
    w = H_B * DH_B
    cq, ck, cv = 1, 0, 1
    prev = lambda col: pl.BlockSpec((1, BAND_TQ, w), lambda bi, i: (bi, jnp.maximum(i - 1, 0), col))
    cur = lambda col: pl.BlockSpec((1, BAND_TQ, w), lambda bi, i: (bi, i, col))
    return pl.pallas_call(
        _band_prompt_kernel,
        out_shape=jax.ShapeDtypeStruct((b, t, w), BF16),
        grid=(b, t // BAND_TQ),
        in_specs=[
            pl.BlockSpec((H_B // 2, 2 * QPAIR, WIN), lambda bi, i: (0, 0, 0)),
            cur(cq), prev(ck), cur(ck), prev(cv), cur(cv),
        ],
        out_specs=pl.BlockSpec((1, BAND_TQ, w), lambda bi, i: (bi, i, 0)),
        scratch_shapes=[pltpu.VMEM((2 * BAND_TQ, w), BF16), pltpu.VMEM((2 * BAND_TQ, w), BF16)],
        compiler_params=_cparams(("parallel", "arbitrary")),
        name="band_prompt",
    )(bias.reshape(H_B // 2, 2 * QPAIR, WIN), q3, kv3, kv3, kv3, kv3)


def _band_sample_kernel(bc_ref, bn_ref, q_ref, kc_ref, vc_ref, kn_ref, vn_ref, o_ref, *, past):
    t = q_ref.shape[1]
    nband = kc_ref.shape[1]
    lane = lax.broadcasted_iota(jnp.int32, (t, LANES), 1)
    qpos_c = past + lax.broadcasted_iota(jnp.int32, (t, nband), 0)
    kpos_c = past - nband + lax.broadcasted_iota(jnp.int32, (t, nband), 1)
    qpos_n = past + lax.broadcasted_iota(jnp.int32, (t, t), 0)
    kpos_n = past + lax.broadcasted_iota(jnp.int32, (t, t), 1)

    def allowed(qpos, kpos):
        qc, kc = qpos // CHUNK, kpos // CHUNK
        return (kpos >= 0) & (kc <= qc) & (kc >= qc - BAND_CHUNKS)

    ok_c = allowed(qpos_c, kpos_c)
    ok_n = allowed(qpos_n, kpos_n)
    for hp in range(H_B // 2):
        cols = slice(hp * LANES, (hp + 1) * LANES)
        halves = []
        for sub, qm in enumerate(_split_halves(q_ref[0, :, cols])):
            hd = 2 * hp + sub
            pieces = [(kc_ref[0, :, cols].astype(BF16), vc_ref[0, :, cols].astype(BF16),
                       jnp.where(ok_c, bc_ref[hd], NEG)),
                      (kn_ref[0, :, cols].astype(BF16), vn_ref[0, :, cols].astype(BF16),
                       jnp.where(ok_n, bn_ref[hd], NEG))]
            halves.append(_pieces_attention(qm, pieces))
        o_ref[0, :, cols] = jnp.where(lane < DH_B, halves[0], halves[1]).astype(o_ref.dtype)


def _band_sample(q3, kv3, kc, vc, bias, past):
    b, t, _ = q3.shape
    nband = kc.shape[1]
    assert nband == BAND and t <= CHUNK
    w = H_B * DH_B
    bias_c = bias[:, :t, :nband]
    bias_n = bias[:, :t, nband:nband + t]
    blk = lambda col: pl.BlockSpec((1, t, w), lambda bi: (bi, 0, col))
    cache = pl.BlockSpec((1, nband, w), lambda bi: (bi, 0, 0))
    return pl.pallas_call(
        functools.partial(_band_sample_kernel, past=past),
        out_shape=jax.ShapeDtypeStruct((b, t, w), BF16),
        grid=(b,),
        in_specs=[
            pl.BlockSpec((H_B, t, nband), lambda bi: (0, 0, 0)),
            pl.BlockSpec((H_B, t, t), lambda bi: (0, 0, 0)),
            blk(1), cache, cache, blk(0), blk(1),
        ],
        out_specs=pl.BlockSpec((1, t, w), lambda bi: (bi, 0, 0)),
        compiler_params=_cparams(("parallel",)),
        name="band_sample",
    )(bias_c, bias_n, q3, kc, vc, kv3, kv3)


CONV_PAD = 8
MLSTM_L = 256


def _mlstm_kernel(cqk_ref, cv_ref, co_ref, cif_ref, conv0_ref, cw_ref, cb_ref, gb_ref, hg_ref,
                  c0_ref, n0_ref, m0_ref, hs_ref, c_ref, n_ref, m_ref, xext, *, L):
    @pl.when(pl.program_id(1) == 0)
    def _():
        xext[0:CONV_PAD, :] = conv0_ref[0]
        c_ref[...] = c0_ref[...]
        n_ref[...] = n0_ref[...]
        m_ref[...] = m0_ref[...]

    xext[CONV_PAD:CONV_PAD + L, :] = cqk_ref[0]
    base = CONV_PAD - (CONV_W - 1)
    u = 0.0
    for j in range(CONV_W):
        u = u + xext[base + j:base + j + L, :] * cw_ref[j:j + 1, :]
    u = cb_ref[...] + u
    u = u * jax.nn.sigmoid(u)
    xext[0:CONV_PAD, :] = xext[L:L + CONV_PAD, :]

    z = cif_ref[0] + gb_ref[...]
    lane = lax.broadcasted_iota(jnp.int32, (L, LANES), 1)
    lf = jnp.minimum(z, 0.0) - jnp.log1p(jnp.exp(-jnp.abs(z)))
    tr = lax.broadcasted_iota(jnp.int32, (L, L), 0)
    tc = lax.broadcasted_iota(jnp.int32, (L, L), 1)
    causal = tr >= tc
    b_all = jnp.dot(causal.astype(F32), lf, preferred_element_type=F32,
                    precision=lax.Precision.HIGHEST)
    sr = lax.broadcasted_iota(jnp.int32, (8, LANES), 0)
    sc = lax.broadcasted_iota(jnp.int32, (8, LANES), 1)
    sel = jnp.where(sr < H_C, jnp.where(sc == sr, 1.0, jnp.where(sc == sr + H_C, -1.0, 0.0)), 0.0)
    rt = lax.dot_general(sel, jnp.where(lane < H_C, z, b_all), NT_DIMS,
                         preferred_element_type=F32, precision=lax.Precision.HIGHEST)
    er = lax.broadcasted_iota(jnp.int32, (DH_C, DH_C), 0)
    ec = lax.broadcasted_iota(jnp.int32, (DH_C, DH_C), 1)
    eye = jnp.where(er == ec, 1.0, 0.0).astype(BF16)

    for h in range(H_C):
        cols = slice(h * DH_C, (h + 1) * DH_C)
        q = u[:, h * DH_C:(h + 1) * DH_C]
        k = u[:, D_C + h * DH_C:D_C + (h + 1) * DH_C] * (DH_C ** -0.5)
        v = cv_ref[0, :, cols]
        qb, kb_, vb_ = q.astype(BF16), k.astype(BF16), v.astype(BF16)
        b_col = b_all[:, H_C + h:H_C + h + 1]
        li_col = z[:, h:h + 1]
        m_prev = m_ref[0, h:h + 1, 0:1]
        cs = c_ref[0, h]
        ns = n_ref[0, h:h + 1, :]

        dmat = jnp.where(causal, b_col + rt[h:h + 1, :], NEG)
        inter = b_col + m_prev
        m_t = jnp.maximum(inter, jnp.max(dmat, axis=-1, keepdims=True))
        w_intra = jnp.exp(dmat - m_t)
        w_inter = jnp.exp(inter - m_t)
        a = w_intra * _nt(qb, kb_)
        num = (jnp.dot(a.astype(BF16), vb_, preferred_element_type=F32)
               + w_inter * _nt(qb, cs.astype(BF16)))
        den = jnp.sum(a, axis=-1, keepdims=True) + w_inter * jnp.sum(q * ns, axis=-1, keepdims=True)
        hh = num / jnp.maximum(jnp.abs(den), jnp.exp(-m_t))

        b_last = b_col[L - 1:L, :]
        g = b_last - b_col + li_col
        m_new = jnp.maximum(b_last + m_prev, jnp.max(g, axis=0, keepdims=True))
        ws = jnp.exp(g - m_new)
        decay = jnp.exp(b_last + m_prev - m_new)
        vwt = _nt(eye, (ws * v).astype(BF16)).astype(BF16)
        c_ref[0, h] = decay * cs + jnp.dot(vwt, kb_, preferred_element_type=F32)
        n_ref[0, h:h + 1, :] = decay * ns + jnp.sum(ws * k, axis=0, keepdims=True)
        m_ref[0, h:h + 1, :] = jnp.broadcast_to(m_new, (1, LANES))

        gate = jax.nn.sigmoid(co_ref[0, :, cols])
        hs_ref[0, :, cols] = (_rms(hh, hg_ref[...]) * gate).astype(hs_ref.dtype)


def _mlstm(cqk3, cvo3, cif3, conv0, conv_w, conv_b, gate_b, head_g, c0, n0, m0, L):
    b, t, _ = cqk3.shape
    nc = t // L
    blk = lambda col, w: pl.BlockSpec((1, L, w), lambda bi, c: (bi, c, col))
    const = lambda shape: pl.BlockSpec(shape, lambda bi, c: (0,) * len(shape))
    per_b = lambda shape: pl.BlockSpec((1,) + shape, lambda bi, c: (bi,) + (0,) * len(shape))
    return pl.pallas_call(
        functools.partial(_mlstm_kernel, L=L),
        out_shape=(jax.ShapeDtypeStruct((b, t, D_C), BF16),
                   jax.ShapeDtypeStruct((b, H_C, DH_C, DH_C), F32),
                   jax.ShapeDtypeStruct((b, H_C, DH_C), F32),
                   jax.ShapeDtypeStruct((b, H_C, LANES), F32)),
        grid=(b, nc),
        in_specs=[
            blk(0, 2 * D_C), blk(0, D_C), blk(1, D_C), blk(0, LANES),
            per_b((CONV_PAD, 2 * D_C)), const((CONV_W, 2 * D_C)), const((1, 2 * D_C)),
            const((1, LANES)), const((1, DH_C)),
            per_b((H_C, DH_C, DH_C)), per_b((H_C, DH_C)), per_b((H_C, LANES)),
        ],
        out_specs=(pl.BlockSpec((1, L, D_C), lambda bi, c: (bi, c, 0)),
                   per_b((H_C, DH_C, DH_C)), per_b((H_C, DH_C)), per_b((H_C, LANES))),
        scratch_shapes=[pltpu.VMEM((CONV_PAD + L, 2 * D_C), F32)],
        compiler_params=_cparams(("parallel", "arbitrary")),
        name="mlstm",
    )(cqk3, cvo3, cvo3, cif3, conv0, conv_w, conv_b, gate_b, head_g, c0, n0, m0)


def _mix_kernel(oa_ref, ob_ref, oc_ref, ga_ref, gb_ref, gc_ref, x_ref, wa_ref, wb_ref, wc_ref, wo_ref, o_ref):
    mixed = (ga_ref[...] * jnp.dot(oa_ref[...], wa_ref[...], preferred_element_type=F32)
             + gb_ref[...] * jnp.dot(ob_ref[...], wb_ref[...], preferred_element_type=F32)
             + gc_ref[...] * jnp.dot(oc_ref[...], wc_ref[...], preferred_element_type=F32))
    o_ref[...] = x_ref[...] + jnp.dot(mixed.astype(BF16), wo_ref[...], preferred_element_type=F32)


def _mix(oa, ob, oc, gates, x, l, wa, wb, wc, wo):
    m, d = x.shape
    tm = min(m, 512)
    row = lambda w, col=0: pl.BlockSpec((tm, w), lambda i: (i, col))
    full = lambda a: pl.BlockSpec((None,) + a.shape[1:], lambda i: (l, 0, 0))
    g0 = 0
    return pl.pallas_call(
        _mix_kernel,
        out_shape=jax.ShapeDtypeStruct((m, d), F32),
        grid=(m // tm,),
        in_specs=[row(oa.shape[1]), row(ob.shape[1]), row(oc.shape[1]),
                  row(d, g0), row(d, g0 + 1), row(d, g0 + 2), row(d),
                  full(wa), full(wb), full(wc), full(wo)],
        out_specs=row(d),
        compiler_params=_cparams(("parallel",)),
        name="mix",
    )(oa, ob, oc, gates, gates, gates, x, wa, wb, wc, wo)


def _cross_kernel(x_ref, g_ref, wq_ref, mk_ref, mv_ref, wo_ref, o_ref):
    x = x_ref[0]
    h = _rms(x, g_ref[...]).astype(BF16)
    q = (jnp.dot(h, wq_ref[...], preferred_element_type=F32) * (DH_M ** -0.5 * LOG2E)).astype(BF16)
    outs = []
    for hd in range(H_M):
        cols = slice(hd * DH_M, (hd + 1) * DH_M)
        s = _nt(q[:, cols], mk_ref[0, :, cols].astype(BF16))
        m = jnp.max(s, axis=-1, keepdims=True)
        e = jnp.exp2(s - m)
        l = jnp.sum(e, axis=-1, keepdims=True)
        o = jnp.dot(e.astype(BF16), mv_ref[0, :, cols].astype(BF16), preferred_element_type=F32) / l
        outs.append(o.astype(BF16))
    o = jnp.concatenate(outs, axis=1)
    o_ref[0] = x + jnp.dot(o, wo_ref[...], preferred_element_type=F32)


def _cross(x3, g, l, wq, mk, mv, wo):
    b, t, d = x3.shape
    tm = min(t, 512)
    nm = mk.shape[1]
    full = lambda a: pl.BlockSpec((None,) + a.shape[1:], lambda bi, i: (l, 0, 0))
    return pl.pallas_call(
        _cross_kernel,
        out_shape=jax.ShapeDtypeStruct((b, t, d), F32),
        grid=(b, t // tm),
        in_specs=[pl.BlockSpec((1, tm, d), lambda bi, i: (bi, i, 0)),
                  pl.BlockSpec(g.shape, lambda bi, i: (0, 0)), full(wq),
                  pl.BlockSpec((1, nm, d), lambda bi, i: (bi, 0, 0)),
                  pl.BlockSpec((1, nm, d), lambda bi, i: (bi, 0, 0)), full(wo)],
        out_specs=pl.BlockSpec((1, tm, d), lambda bi, i: (bi, i, 0)),
        compiler_params=_cparams(("parallel", "parallel")),
        name="cross",
    )(x3, g, wq, mk, mv, wo)


FFN_TF = 1408
FFN_TM = 512


def _ffn_kernel(x_ref, g_ref, wg_ref, wu_ref, wd_ref, gf_ref, o_ref, *, final_norm):
    x = x_ref[...]
    h = _rms(x, g_ref[...]).astype(BF16)
    chunks = [slice(c, c + FFN_TF) for c in range(0, D_FF, FFN_TF)]
    pre = [(jnp.dot(h, wg_ref[:, c], preferred_element_type=F32),
            jnp.dot(h, wu_ref[:, c], preferred_element_type=F32)) for c in chunks]
    y = x
    for c, (gate, up) in zip(chunks, pre):
        a = (gate * jax.nn.sigmoid(gate) * up).astype(BF16)
        y = y + jnp.dot(a, wd_ref[c, :], preferred_element_type=F32)
    if final_norm:
        y = _rms(y, gf_ref[...])
    o_ref[...] = y


def _ffn(x, g, l, wg, wu, wd, g_final, final_norm):
    m, d = x.shape
    tm = min(m, FFN_TM)
    resident = lambda a: pl.BlockSpec((None,) + a.shape[1:], lambda i: (l, 0, 0),
                                      pipeline_mode=pl.Buffered(1))
    return pl.pallas_call(
        functools.partial(_ffn_kernel, final_norm=final_norm),
        out_shape=jax.ShapeDtypeStruct((m, d), F32),
        grid=(m // tm,),
        in_specs=[
            pl.BlockSpec((tm, d), lambda i: (i, 0)),
            pl.BlockSpec((1, d), lambda i: (0, 0)),
            resident(wg), resident(wu), resident(wd),
            pl.BlockSpec((1, d), lambda i: (0, 0)),
        ],
        out_specs=pl.BlockSpec((tm, d), lambda i: (i, 0)),
        compiler_params=_cparams(("parallel",)),
        name="ffn",
    )(x, g, wg, wu, wd, g_final)


def _layer(x3, l, w, mem_k, mem_v, cache, final_norm, akv_all):
    b, t, d = x3.shape
    x2 = x3.reshape(b * t, d)
    row = lambda a: a.reshape(1, -1)
    lam_init = 0.8 - 0.6 * math.exp(-0.3 * l)

    gates, cqk, cif, q, ak_all, av_all, akv, bkv, cvo = _in_proj(
        x2, row(w['g_mix']), l, w['w_in'], w['w_if'], *akv_all, (b, t))
    q3 = q.reshape(b, t, SEG)
    bkv3 = bkv.reshape(b, t, SEG)
    cqk3 = cqk.reshape(b, t, 2 * D_C)
    keep = t if cache is not None else min(BAND, t)
    b_k = bkv3[:, t - keep:, :HALF].reshape(b, keep, H_B, DH_B)
    b_v = bkv3[:, t - keep:, HALF:].reshape(b, keep, H_B, DH_B)

    gate_b = jnp.concatenate([w['b_i'], w['b_f'], jnp.zeros((LANES - 2 * H_C,), F32)]).reshape(1, LANES)
    if cache is None:
        oa = _diff_prompt(q3, akv.reshape(b, t, SEG), w['lqk'], row(w['a_head_g']), lam_init)
        ob = _band_prompt(q3, bkv3, w['relbias'])
        conv0 = jnp.zeros((b, CONV_PAD, 2 * D_C), F32)
        c0 = jnp.zeros((b, H_C, DH_C, DH_C), F32)
        n0 = jnp.zeros((b, H_C, DH_C), F32)
        m0 = jnp.zeros((b, H_C, LANES), F32)
        L = min(t, MLSTM_L)
    else:
        past = cache['a_k'].shape[2]
        ak3 = ak_all.reshape(DEPTH * b, t, HALF)
        av3 = av_all.reshape(DEPTH * b, t, HALF)
        oa = _diff_sample(q3, ak3, av3, l, cache['a_k'], cache['a_v'],
                          w['lqk'], row(w['a_head_g']), lam_init)
        nband = cache['b_k'].shape[1]
        ob = _band_sample(q3, bkv3, cache['b_k'].reshape(b, nband, -1), cache['b_v'].reshape(b, nband, -1),
                          w['relbias'], past)
        conv0 = jnp.pad(cache['conv'], ((0, 0), (CONV_PAD - (CONV_W - 1), 0), (0, 0)))
        c0, n0 = cache['C'], cache['n']
        m0 = jnp.broadcast_to(cache['m'][:, :, None], (b, H_C, LANES))
        L = t
    oc, c_new, n_new, m_new = _mlstm(cqk3, cvo.reshape(b, t, SEG), cif.reshape(b, t, LANES), conv0,
                                     w['conv_w'], row(w['conv_b']), gate_b, row(w['c_head_g']), c0, n0, m0, L)
    assert t >= CONV_W - 1
    conv_new = cqk3[:, t - (CONV_W - 1):]

    x2 = _mix(oa.reshape(b * t, -1), ob.reshape(b * t, -1), oc.reshape(b * t, -1), gates, x2, l,
              w['w_up_a'], w['w_up_b'], w['w_up_c'], w['w_o'])
    x3 = _cross(x2.reshape(b, t, d), row(w['g_cross']), l, w['w_mq'], mem_k, mem_v, w['w_mo'])
    x2 = _ffn(x3.reshape(b * t, d), row(w['g_ffn']), l, w['w_ff_g'], w['w_ff_u'], w['w_ff_d'],
              row(w['g_final']), final_norm)
    return x2.reshape(b, t, d), (ak_all, av_all), (b_k, b_v, c_new, n_new, m_new[:, :, 0], conv_new)


def kernel(x_prompt, x_sample, cache_a_k, cache_a_v, cache_b_k, cache_b_v, state_c_C, state_c_n, state_c_m, state_c_conv, cache_mem_k, cache_mem_v, mem_prompt, g_mix, w_in, a_lq1, a_lk1, a_lq2, a_lk2, a_head_g, b_rel, c_conv_w, c_conv_b, c_b_i, c_b_f, c_head_g, w_up_a, w_up_b, w_up_c, w_o, g_cross, w_mq, w_mk, w_mv, w_mo, g_ffn, w_ff_g, w_ff_u, w_ff_d, g_final):
    xp, xs = x_prompt, x_sample
    bp = x_prompt.shape[0]
    n_mem = mem_prompt.shape[1]
    mem2 = mem_prompt.reshape(bp * n_mem, D_MODEL)
    new_p = [[] for _ in range(8)]
    new_s = [[] for _ in range(6)]
    akv_p = akv_s = (None, None)
    bf = lambda a: a.astype(BF16)
    wp, wif = _prep_w_in(w_in)
    wts = dict(w_up_a=bf(w_up_a), w_up_b=bf(w_up_b), w_up_c=bf(w_up_c), w_o=bf(w_o), w_mq=bf(w_mq),
               w_mo=bf(w_mo), w_ff_g=bf(w_ff_g), w_ff_u=bf(w_ff_u), w_ff_d=bf(w_ff_d))
    w_mk_b, w_mv_b = bf(w_mk), bf(w_mv)
    for l in range(DEPTH):
        w = dict(g_mix=g_mix[l], w_in=wp, w_if=wif,
                 lqk=jnp.stack([a_lq1[l], a_lk1[l], a_lq2[l], a_lk2[l]]),
                 a_head_g=a_head_g[l], relbias=_relbias(b_rel[l]), conv_w=c_conv_w[l], conv_b=c_conv_b[l],
                 b_i=c_b_i[l], b_f=c_b_f[l], c_head_g=c_head_g[l],
                 g_cross=g_cross[l], g_ffn=g_ffn[l], g_final=g_final, **wts)
        mk2 = _mem_proj(mem2, l, w_mk_b).reshape(bp, n_mem, D_MODEL)
        mv2 = _mem_proj(mem2, l, w_mv_b).reshape(bp, n_mem, D_MODEL)
        last = l == DEPTH - 1
        xp, akv_p, st_p = _layer(xp, l, w, mk2, mv2, None, last, akv_p)
        cache = dict(a_k=cache_a_k, a_v=cache_a_v, b_k=cache_b_k[l], b_v=cache_b_v[l],
                     C=state_c_C[l], n=state_c_n[l], m=state_c_m[l], conv=state_c_conv[l])
        bs = xs.shape[0]
        xs, akv_s, st_s = _layer(xs, l, w, cache_mem_k[l].reshape(bs, n_mem, D_MODEL),
                                 cache_mem_v[l].reshape(bs, n_mem, D_MODEL), cache, last, akv_s)
        mk4 = mk2.reshape(bp, n_mem, H_M, DH_M)
        mv4 = mv2.reshape(bp, n_mem, H_M, DH_M)
        for lst, a in zip(new_p, st_p + (mk4, mv4)):
            lst.append(a)
        for lst, a in zip(new_s, st_s):
            lst.append(a)
    heads = lambda a, b, t: a.reshape(DEPTH, b, t, H_A, DV_A)
    outs_p = [heads(a, bp, x_prompt.shape[1]) for a in akv_p] + [jnp.stack(a, 0) for a in new_p]
    outs_s = [heads(a, xs.shape[0], xs.shape[1]) for a in akv_s] + [jnp.stack(a, 0) for a in new_s]
    return (xp, xs) + tuple(outs_p) + tuple(outs_s)
```

```python
import functools
import math

import jax
import jax.numpy as jnp
from jax import lax
from jax.experimental import pallas as pl
from jax.experimental.pallas import tpu as pltpu

F32 = jnp.float32
BF16 = jnp.bfloat16

D_MODEL = 1024
DEPTH = 2
CHUNK = 64
EPS = 1e-6
NEG = -1e30
LOG2E = math.log2(math.e)
H_A = 4
DH_A = 64
DV_A = 2 * DH_A
H_B = 8
DH_B = 64
BAND_CHUNKS = 8
BAND = BAND_CHUNKS * CHUNK
REL_CLIP = 128
H_C = 4
DH_C = 128
D_C = H_C * DH_C
CONV_W = 4
H_M = 4
DH_M = D_MODEL // H_M
D_FF = -(-8 * D_MODEL // (3 * 256)) * 256

LANES = 128
VMEM_LIMIT = 48 * 1024 * 1024

SEG = 1024
HALF = SEG // 2
GATE_TILES = 3
N_TILES = 8
IN_SIZES = (H_A * DH_A,) * 4 + (H_A * DV_A,) + (H_B * DH_B,) * 3 + (2 * D_C, D_C, D_C, 2 * H_C, 3 * D_MODEL)

ALIBI_SLOPES = tuple(2.0 ** (-8.0 * (i + 1) / H_A) for i in range(H_A))

NT_DIMS = (((1,), (1,)), ((), ()))


def _cparams(sem):
    return pltpu.CompilerParams(dimension_semantics=sem, vmem_limit_bytes=VMEM_LIMIT)


def _rms(x, g):
    ms = jnp.mean(x * x, axis=-1, keepdims=True)
    return x * lax.rsqrt(ms + EPS) * g


def _nt(a, b):
    return lax.dot_general(a, b, NT_DIMS, preferred_element_type=F32)


def _prep_w_in_kernel(w_ref, o_ref, oif_ref):
    offs = [0]
    for size in IN_SIZES:
        offs.append(offs[-1] + size)
    seg = lambda i: w_ref[:, offs[i]:offs[i + 1]]
    a_q1, a_q2, a_k1, a_k2, a_v, b_q, b_k, b_v, c_qk, c_v, c_o = (seg(i) for i in range(11))
    g = seg(12)

    def put(col, val):
        o_ref[:, col:col + val.shape[1]] = val.astype(BF16)

    put(0, g)
    put(GATE_TILES * SEG, c_qk)
    for h in range(H_A):
        hs = slice(h * DH_A, (h + 1) * DH_A)
        base = h * 2 * DH_A
        put(4 * SEG + base, a_q1[:, hs] * (DH_A ** -0.5))
        put(4 * SEG + base + DH_A, a_q2[:, hs] * (DH_A ** -0.5))
        put(5 * SEG + base, a_k1[:, hs])
        put(5 * SEG + base + DH_A, a_k2[:, hs])
    put(4 * SEG + HALF, b_q * (DH_B ** -0.5))
    put(5 * SEG + HALF, a_v)
    put(6 * SEG, b_k)
    put(6 * SEG + HALF, b_v)
    put(7 * SEG, c_v)
    put(7 * SEG + HALF, c_o)
    tile = w_ref[:, offs[11]:offs[11] + LANES]
    lane = lax.broadcasted_iota(jnp.int32, tile.shape, 1)
    oif_ref[...] = jnp.where(lane < 2 * H_C, tile, 0.0).astype(BF16)


def _prep_w_in(w):
    depth, k, n = w.shape
    tk = 256
    n_out = N_TILES * SEG
    return pl.pallas_call(
        _prep_w_in_kernel,
        out_shape=(jax.ShapeDtypeStruct((depth, k, n_out), BF16),
                   jax.ShapeDtypeStruct((depth, k, LANES), BF16)),
        grid=(depth, k // tk),
        in_specs=[pl.BlockSpec((None, tk, n), lambda l, i: (l, i, 0))],
        out_specs=(pl.BlockSpec((None, tk, n_out), lambda l, i: (l, i, 0)),
                   pl.BlockSpec((None, tk, LANES), lambda l, i: (l, i, 0))),
        compiler_params=_cparams(("parallel", "parallel")),
        name="prep_w_in",
    )(w)


IN_PROJ_TM = 256


def _in_proj_kernel(*refs, aliased, heads5d):
    x_ref, g_ref, w_ref, wif_ref = refs[:4]
    og_ref, ocqk_ref, ocif_ref, oq_ref, oak_ref, oav_ref, oakv_ref, obkv_ref, ocvo_ref = refs[6 if aliased else 4:]
    h = _rms(x_ref[...], g_ref[...]).astype(BF16)
    cols = lambda j, lo=0, hi=SEG: jnp.dot(h, w_ref[:, j * SEG + lo:j * SEG + hi], preferred_element_type=F32)

    ocif_ref[...] = jnp.dot(h, wif_ref[...], preferred_element_type=F32)
    for j in range(GATE_TILES):
        og_ref[:, j * SEG:(j + 1) * SEG] = jax.nn.sigmoid(cols(j)).astype(og_ref.dtype)
    ocqk_ref[...] = cols(GATE_TILES)
    oq_ref[...] = (cols(4) * LOG2E).astype(oq_ref.dtype)
    for out_ref, off in ((oak_ref, 0), (oav_ref, HALF)):
        a = cols(5, off, off + HALF)
        oakv_ref[:, off:off + HALF] = a.astype(oakv_ref.dtype)
        if heads5d:
            out_ref[0, 0] = a.reshape(a.shape[0], H_A, DV_A)
        else:
            out_ref[0] = a
    obkv_ref[...] = cols(6)
    ocvo_ref[...] = cols(7)


def _in_proj(x, g, l, wp, wif, ak_all, av_all, bt):
    m, k = x.shape
    tm = min(m, IN_PROJ_TM)
    aliased = ak_all is not None
    b, t = bt
    heads5d = t % tm == 0
    if heads5d:
        stacked = jax.ShapeDtypeStruct((DEPTH, b, t, H_A, DV_A), F32)
        layer_row = pl.BlockSpec((1, 1, tm, H_A, DV_A), lambda i: (l, i // (t // tm), i % (t // tm), 0, 0))
    else:
        stacked = jax.ShapeDtypeStruct((DEPTH, m, HALF), F32)
        layer_row = pl.BlockSpec((1, tm, HALF), lambda i: (l, i, 0))
    resident = lambda a: pl.BlockSpec((None,) + a.shape[1:], lambda i: (l, 0, 0), pipeline_mode=pl.Buffered(1))
    in_specs = [pl.BlockSpec((tm, k), lambda i: (i, 0)), pl.BlockSpec((1, k), lambda i: (0, 0)),
                resident(wp), resident(wif)]
    args = [x, g, wp, wif]
    if aliased:
        in_specs += [pl.BlockSpec(memory_space=pl.ANY)] * 2
        args += [ak_all, av_all]
    row = lambda w: pl.BlockSpec((tm, w), lambda i: (i, 0))
    sds = jax.ShapeDtypeStruct
    return pl.pallas_call(
        functools.partial(_in_proj_kernel, aliased=aliased, heads5d=heads5d),
        out_shape=(sds((m, GATE_TILES * SEG), BF16), sds((m, SEG), F32), sds((m, LANES), F32),
                   sds((m, SEG), BF16), stacked, stacked, sds((m, SEG), BF16),
                   sds((m, SEG), F32), sds((m, SEG), F32)),
        grid=(m // tm,),
        in_specs=in_specs,
        out_specs=(row(GATE_TILES * SEG), row(SEG), row(LANES), row(SEG), layer_row, layer_row, row(SEG),
                   row(SEG), row(SEG)),
        input_output_aliases={4: 4, 5: 5} if aliased else {},
        compiler_params=_cparams(("parallel",)),
        name="in_proj",
    )(*args)


def _mem_proj_kernel(x_ref, w_ref, o_ref):
    y = jnp.dot(x_ref[0].astype(BF16), w_ref[...], preferred_element_type=F32)
    o_ref[...] = y.reshape(o_ref.shape)


def _mem_proj(mem, w):
    b, nm, k = mem.shape
    depth, _, n = w.shape
    return pl.pallas_call(
        _mem_proj_kernel,
        out_shape=jax.ShapeDtypeStruct((depth, b, nm, H_M, DH_M), F32),
        grid=(depth, b),
        in_specs=[pl.BlockSpec((1, nm, k), lambda l, bi: (bi, 0, 0)),
                  pl.BlockSpec((None, k, n), lambda l, bi: (l, 0, 0))],
        out_specs=pl.BlockSpec((None, None, nm, H_M, DH_M), lambda l, bi: (l, bi, 0, 0, 0)),
        compiler_params=_cparams(("parallel", "parallel")),
        name="mem_proj",
    )(mem, w)


def _online_updates(jobs):
    staged = []
    for s, _, (acc, m, l) in jobs:
        slabs = [s[:, j * LANES:(j + 1) * LANES] for j in range(s.shape[1] // LANES)]
        m_old = m[...]
        m_new = jnp.maximum(m_old, jnp.max(functools.reduce(jnp.maximum, slabs), axis=-1, keepdims=True))
        alpha = jnp.exp2(m_old - m_new)
        ps = [jnp.exp2(c - m_new) for c in slabs]
        l[...] = alpha * l[...] + jnp.sum(functools.reduce(jnp.add, ps), axis=-1, keepdims=True)
        m[...] = m_new
        staged.append((alpha, jnp.concatenate(ps, axis=1).astype(BF16)))
    for (alpha, p), (_, v, (acc, m, l)) in zip(staged, jobs):
        acc[...] = alpha * acc[...] + jnp.dot(p, v, preferred_element_type=F32)


def _split_halves(q):
    lane = lax.broadcasted_iota(jnp.int32, q.shape, 1)
    lo = jnp.where(lane < DH_A, q, 0.0).astype(BF16)
    hi = jnp.where(lane >= DH_A, q, 0.0).astype(BF16)
    return lo, hi


def _lambda(lqk, lam_init):
    e1 = jnp.exp(jnp.sum(lqk[0:1, :] * lqk[1:2, :], axis=-1, keepdims=True))
    e2 = jnp.exp(jnp.sum(lqk[2:3, :] * lqk[3:4, :], axis=-1, keepdims=True))
    return e1 - e2 + lam_init


def _diff_prompt_kernel(slopes_ref, lqk_ref, hg_ref, q_ref, k_ref, v_ref, o_ref,
                        diag_bias, acc1, acc2, m1, l1, m2, l2, *, tq, lam_init):
    h = pl.program_id(1)
    qi = pl.program_id(2)
    slope = slopes_ref[h] * LOG2E
    kb, vb = k_ref.at[0], v_ref.at[0]
    q0 = pl.multiple_of(qi * tq, tq)

    @pl.when(qi == 0)
    def _():
        r = lax.broadcasted_iota(jnp.int32, (tq, tq), 0)
        c = lax.broadcasted_iota(jnp.int32, (tq, tq), 1)
        bias = slope * (r - jnp.abs(r - c)).astype(F32)
        diag_bias[...] = jnp.where((c // CHUNK) <= (r // CHUNK), bias, NEG)

    qa, qb = _split_halves(q_ref[0])
    states = ((acc1, m1, l1), (acc2, m2, l2))
    for acc, m, l in states:
        m[...] = jnp.full(m.shape, NEG, F32)
        l[...] = jnp.zeros(l.shape, F32)
        acc[...] = jnp.zeros(acc.shape, F32)

    def full_tiles(k_starts):
        jobs = []
        for k0 in k_starts:
            k = kb[pl.ds(k0, tq), :]
            v = vb[pl.ds(k0, tq), :]
            kpos = (k0 - q0) + lax.broadcasted_iota(jnp.int32, (1, tq), 1)
            bias = slope * kpos.astype(F32)
            jobs += [(_nt(qa, k) + bias, v, states[0]), (_nt(qb, k) + bias, v, states[1])]
        _online_updates(jobs)

    def tile_pair(it, carry):
        k0 = pl.multiple_of(it * (2 * tq), 2 * tq)
        full_tiles([k0, pl.multiple_of(k0 + tq, tq)])
        return carry

    lax.fori_loop(0, qi // 2, tile_pair, 0)

    @pl.when(qi % 2 == 1)
    def _():
        full_tiles([pl.multiple_of((qi - 1) * tq, tq)])

    half = tq // 2
    jobs = []
    for rows, nk in ((slice(0, half), half), (slice(half, tq), tq)):
        k = kb[pl.ds(q0, nk), :]
        v = vb[pl.ds(q0, nk), :]
        bias = diag_bias[rows, 0:nk]
        jobs += [(_nt(qm[rows], k) + bias, v, tuple(ref.at[rows] for ref in st))
                 for qm, st in ((qa, states[0]), (qb, states[1]))]
    _online_updates(jobs)

    lam = _lambda(lqk_ref[...], lam_init)
    o = acc1[...] / l1[...] - lam * (acc2[...] / l2[...])
    o_ref[0] = (_rms(o, hg_ref[...]) * (1.0 - lam_init)).astype(o_ref.dtype)


def _diff_prompt(q3, kv3, lqk, head_g, lam_init):
    b, t, _ = q3.shape
    tq = min(t, 512)
    slopes = jnp.asarray(ALIBI_SLOPES, F32)
    return pl.pallas_call(
        functools.partial(_diff_prompt_kernel, tq=tq, lam_init=lam_init),
        out_shape=jax.ShapeDtypeStruct((b, t, H_A * DV_A), BF16),
        grid=(b, H_A, t // tq),
        in_specs=[
            pl.BlockSpec(memory_space=pltpu.SMEM),
            pl.BlockSpec((4, DH_A), lambda bi, h, qi: (0, 0)),
            pl.BlockSpec((1, DV_A), lambda bi, h, qi: (0, 0)),
            pl.BlockSpec((1, tq, LANES), lambda bi, h, qi: (bi, qi, h)),
            pl.BlockSpec((1, t, LANES), lambda bi, h, qi: (bi, 0, h)),
            pl.BlockSpec((1, t, LANES), lambda bi, h, qi: (bi, 0, H_A + h)),
        ],
        out_specs=pl.BlockSpec((1, tq, LANES), lambda bi, h, qi: (bi, qi, h)),
        scratch_shapes=[
            pltpu.VMEM((tq, tq), F32),
            pltpu.VMEM((tq, LANES), F32), pltpu.VMEM((tq, LANES), F32),
            pltpu.VMEM((tq, LANES), F32), pltpu.VMEM((tq, LANES), F32),
            pltpu.VMEM((tq, LANES), F32), pltpu.VMEM((tq, LANES), F32),
        ],
        compiler_params=_cparams(("parallel", "parallel", "arbitrary")),
        name="diff_prompt",
    )(slopes, lqk, head_g, q3, kv3, kv3)


def _pieces_attention(qm, pieces):
    ss = [_nt(qm, k) + bias for k, _, bias in pieces]
    m = functools.reduce(jnp.maximum, [jnp.max(s, axis=-1, keepdims=True) for s in ss])
    es = [jnp.exp2(s - m) for s in ss]
    l = functools.reduce(jnp.add, [jnp.sum(e, axis=-1, keepdims=True) for e in es])
    o = functools.reduce(jnp.add, [jnp.dot(e.astype(BF16), v, preferred_element_type=F32)
                                   for e, (_, v, _) in zip(es, pieces)])
    return o / l


def _diff_sample_kernel(lqk_ref, hg_ref, q_ref, kc_ref, vc_ref, kn_ref, vn_ref, o_ref, *, lam_init):
    t = q_ref.shape[1]
    past = kc_ref.shape[1]
    lam = _lambda(lqk_ref[...], lam_init)
    rc = lax.broadcasted_iota(jnp.int32, (t, past), 0)
    cc = lax.broadcasted_iota(jnp.int32, (t, past), 1)
    rn = lax.broadcasted_iota(jnp.int32, (t, t), 0)
    cn = lax.broadcasted_iota(jnp.int32, (t, t), 1)
    dist_c = jnp.abs(past + rc - cc).astype(F32)
    dist_n = jnp.abs(rn - cn).astype(F32)
    ok_c = (cc // CHUNK) <= ((past + rc) // CHUNK)
    ok_n = ((past + cn) // CHUNK) <= ((past + rn) // CHUNK)
    kc = kc_ref[0].reshape(past, H_A * DV_A).astype(BF16)
    vc = vc_ref[0].reshape(past, H_A * DV_A).astype(BF16)
    for h in range(H_A):
        cols = slice(h * LANES, (h + 1) * LANES)
        bias_c = jnp.where(ok_c, (-ALIBI_SLOPES[h] * LOG2E) * dist_c, NEG)
        bias_n = jnp.where(ok_n, (-ALIBI_SLOPES[h] * LOG2E) * dist_n, NEG)
        pieces = [(kc[:, cols], vc[:, cols], bias_c),
                  (kn_ref[0, :, cols].astype(BF16), vn_ref[0, :, cols].astype(BF16), bias_n)]
        qa, qb = _split_halves(q_ref[0, :, cols])
        o = _pieces_attention(qa, pieces) - lam * _pieces_attention(qb, pieces)
        o_ref[0, :, cols] = (_rms(o, hg_ref[...]) * (1.0 - lam_init)).astype(o_ref.dtype)


def _diff_sample(q3, kn3, vn3, l, kc, vc, lqk, head_g, lam_init):
    b, t, _ = q3.shape
    past = kc.shape[2]
    w = H_A * DV_A
    new = pl.BlockSpec((1, t, w), lambda bi: (l * b + bi, 0, 0))
    cache = pl.BlockSpec((None, 1, past, H_A, DV_A), lambda bi: (l, bi, 0, 0, 0))
    return pl.pallas_call(
        functools.partial(_diff_sample_kernel, lam_init=lam_init),
        out_shape=jax.ShapeDtypeStruct((b, t, w), BF16),
        grid=(b,),
        in_specs=[
            pl.BlockSpec((4, DH_A), lambda bi: (0, 0)),
            pl.BlockSpec((1, DV_A), lambda bi: (0, 0)),
            pl.BlockSpec((1, t, w), lambda bi: (bi, 0, 0)), cache, cache, new, new,
        ],
        out_specs=pl.BlockSpec((1, t, w), lambda bi: (bi, 0, 0)),
        compiler_params=_cparams(("parallel",)),
        name="diff_sample",
    )(lqk, head_g, q3, kc, vc, kn3, vn3)


QPAIR = 2 * CHUNK
WIN = BAND + QPAIR


def _relbias_kernel(tab_ref, o_ref):
    h = pl.program_id(0)
    r = lax.broadcasted_iota(jnp.int32, (QPAIR, WIN), 0)
    j = lax.broadcasted_iota(jnp.int32, (QPAIR, WIN), 1)
    rel = jnp.clip(r - j + BAND, -REL_CLIP, REL_CLIP) + REL_CLIP
    qc = r // CHUNK
    kc = j // CHUNK - BAND_CHUNKS
    allowed = (kc <= qc) & (kc >= qc - BAND_CHUNKS)

    far = BAND - REL_CLIP
    near_rel = rel[:, far:]

    near = jnp.zeros((QPAIR, WIN - far), F32)
    for t in range(2 * REL_CLIP + 1):
        near = jnp.where(near_rel == t, tab_ref[h, t], near)
    bias = jnp.concatenate([jnp.full((QPAIR, far), tab_ref[h, 2 * REL_CLIP], F32), near], axis=1)
    o_ref[0] = jnp.where(allowed, bias * LOG2E, NEG)


def _relbias(table):
    return pl.pallas_call(
        _relbias_kernel,
        out_shape=jax.ShapeDtypeStruct((H_B, QPAIR, WIN), F32),
        grid=(H_B,),
        in_specs=[pl.BlockSpec(memory_space=pltpu.SMEM)],
        out_specs=pl.BlockSpec((1, QPAIR, WIN), lambda h: (h, 0, 0)),
        compiler_params=_cparams(("arbitrary",)),
        name="relbias",
    )(table)


BAND_TQ = 512
BAND_UNROLL = 1


def _band_prompt_kernel(bias_ref, q_ref, kp_ref, kc_ref, vp_ref, vc_ref, o_ref, kcat, vcat):
    i = pl.program_id(1)
    kcat[0:BAND_TQ, :] = kp_ref[0].astype(BF16)
    kcat[BAND_TQ:, :] = kc_ref[0].astype(BF16)
    vcat[0:BAND_TQ, :] = vp_ref[0].astype(BF16)
    vcat[BAND_TQ:, :] = vc_ref[0].astype(BF16)
    lane = lax.broadcasted_iota(jnp.int32, (QPAIR, LANES), 1)

    def pair_blocks(it, carry, *, first_block):
        starts = [pl.multiple_of((it * BAND_UNROLL + u) * QPAIR, QPAIR) for u in range(BAND_UNROLL)]
        scores = []
        for r0 in starts:
            q = q_ref[0, pl.ds(r0, QPAIR), :]
            if first_block:
                kpos = r0 - BAND_TQ + lax.broadcasted_iota(jnp.int32, (1, WIN), 1)
                before_start = jnp.where(kpos >= 0, 0.0, NEG)
            for hp in range(H_B // 2):
                cols = slice(hp * LANES, (hp + 1) * LANES)
                qs = jnp.concatenate(_split_halves(q[:, cols]), axis=0)
                s = _nt(qs, kcat[pl.ds(r0, WIN), cols]) + bias_ref[hp]
                scores.append(s + before_start if first_block else s)
        probs = []
        for s in scores:
            e = jnp.exp2(s - jnp.max(s, axis=-1, keepdims=True))
            probs.append((e.astype(BF16), jnp.sum(e, axis=-1, keepdims=True)))
        for u, r0 in enumerate(starts):
            outs = []
            for hp in range(H_B // 2):
                e, l = probs[u * (H_B // 2) + hp]
                cols = slice(hp * LANES, (hp + 1) * LANES)
                o = jnp.dot(e, vcat[pl.ds(r0, WIN), cols], preferred_element_type=F32) / l
                outs.append(jnp.where(lane < DH_B, o[:QPAIR], o[QPAIR:]))
            o_ref[0, pl.ds(r0, QPAIR), :] = jnp.concatenate(outs, axis=1).astype(o_ref.dtype)
        return carry

    trips = BAND_TQ // QPAIR // BAND_UNROLL

    @pl.when(i == 0)
    def _():
        lax.fori_loop(0, trips, functools.partial(pair_blocks, first_block=True), 0)

    @pl.when(i > 0)
    def _():
        lax.fori_loop(0, trips, functools.partial(pair_blocks, first_block=False), 0)


def _band_prompt(q3, kv3, bias):
    b, t, _ = q3.shape
    assert t % BAND_TQ == 0
    w = H_B * DH_B
    cq, ck, cv = 1, 0, 1
    prev = lambda col: pl.BlockSpec((1, BAND_TQ, w), lambda bi, i: (bi, jnp.maximum(i - 1, 0), col))
    cur = lambda col: pl.BlockSpec((1, BAND_TQ, w), lambda bi, i: (bi, i, col))
    return pl.pallas_call(
        _band_prompt_kernel,
        out_shape=jax.ShapeDtypeStruct((b, t, w), BF16),
        grid=(b, t // BAND_TQ),
        in_specs=[
            pl.BlockSpec((H_B // 2, 2 * QPAIR, WIN), lambda bi, i: (0, 0, 0)),
            cur(cq), prev(ck), cur(ck), prev(cv), cur(cv),
        ],
        out_specs=pl.BlockSpec((1, BAND_TQ, w), lambda bi, i: (bi, i, 0)),
        scratch_shapes=[pltpu.VMEM((2 * BAND_TQ, w), BF16), pltpu.VMEM((2 * BAND_TQ, w), BF16)],
        compiler_params=_cparams(("parallel", "arbitrary")),
        name="band_prompt",
    )(bias.reshape(H_B // 2, 2 * QPAIR, WIN), q3, kv3, kv3, kv3, kv3)


def _band_sample_kernel(bc_ref, bn_ref, q_ref, kc_ref, vc_ref, kn_ref, vn_ref, o_ref, *, past):
    t = q_ref.shape[1]
    nband = kc_ref.shape[1]
    lane = lax.broadcasted_iota(jnp.int32, (t, LANES), 1)
    qpos_c = past + lax.broadcasted_iota(jnp.int32, (t, nband), 0)
    kpos_c = past - nband + lax.broadcasted_iota(jnp.int32, (t, nband), 1)
    qpos_n = past + lax.broadcasted_iota(jnp.int32, (t, t), 0)
    kpos_n = past + lax.broadcasted_iota(jnp.int32, (t, t), 1)

    def allowed(qpos, kpos):
        qc, kc = qpos // CHUNK, kpos // CHUNK
        return (kpos >= 0) & (kc <= qc) & (kc >= qc - BAND_CHUNKS)

    ok_c = allowed(qpos_c, kpos_c)
    ok_n = allowed(qpos_n, kpos_n)
    for hp in range(H_B // 2):
        cols = slice(hp * LANES, (hp + 1) * LANES)
        halves = []
        for sub, qm in enumerate(_split_halves(q_ref[0, :, cols])):
            hd = 2 * hp + sub
            pieces = [(kc_ref[0, :, cols].astype(BF16), vc_ref[0, :, cols].astype(BF16),
                       jnp.where(ok_c, bc_ref[hd], NEG)),
                      (kn_ref[0, :, cols].astype(BF16), vn_ref[0, :, cols].astype(BF16),
                       jnp.where(ok_n, bn_ref[hd], NEG))]
            halves.append(_pieces_attention(qm, pieces))
        o_ref[0, :, cols] = jnp.where(lane < DH_B, halves[0], halves[1]).astype(o_ref.dtype)


def _band_sample(q3, kv3, kc, vc, bias, past):
    b, t, _ = q3.shape
    nband = kc.shape[1]
    assert nband == BAND and t <= CHUNK
    w = H_B * DH_B
    bias_c = bias[:, :t, :nband]
    bias_n = bias[:, :t, nband:nband + t]
    blk = lambda col: pl.BlockSpec((1, t, w), lambda bi: (bi, 0, col))
    cache = pl.BlockSpec((1, nband, w), lambda bi: (bi, 0, 0))
    return pl.pallas_call(
        functools.partial(_band_sample_kernel, past=past),
        out_shape=jax.ShapeDtypeStruct((b, t, w), BF16),
        grid=(b,),
        in_specs=[
            pl.BlockSpec((H_B, t, nband), lambda bi: (0, 0, 0)),
            pl.BlockSpec((H_B, t, t), lambda bi: (0, 0, 0)),
            blk(1), cache, cache, blk(0), blk(1),
        ],
        out_specs=pl.BlockSpec((1, t, w), lambda bi: (bi, 0, 0)),
        compiler_params=_cparams(("parallel",)),
        name="band_sample",
    )(bias_c, bias_n, q3, kc, vc, kv3, kv3)


CONV_PAD = 8
MLSTM_L = 256


def _mlstm_kernel(cqk_ref, cv_ref, co_ref, cif_ref, conv0_ref, cw_ref, cb_ref, gb_ref, hg_ref,
                  c0_ref, n0_ref, m0_ref, hs_ref, c_ref, n_ref, m_ref, xext, *, L):
    @pl.when(pl.program_id(1) == 0)
    def _():
        xext[0:CONV_PAD, :] = conv0_ref[0]
        c_ref[...] = c0_ref[...]
        n_ref[...] = n0_ref[...]
        m_ref[...] = m0_ref[...]

    xext[CONV_PAD:CONV_PAD + L, :] = cqk_ref[0]
    base = CONV_PAD - (CONV_W - 1)
    u = 0.0
    for j in range(CONV_W):
        u = u + xext[base + j:base + j + L, :] * cw_ref[j:j + 1, :]
    u = cb_ref[...] + u
    u = u * jax.nn.sigmoid(u)
    xext[0:CONV_PAD, :] = xext[L:L + CONV_PAD, :]

    z = cif_ref[0] + gb_ref[...]
    lane = lax.broadcasted_iota(jnp.int32, (L, LANES), 1)
    lf = jnp.minimum(z, 0.0) - jnp.log1p(jnp.exp(-jnp.abs(z)))
    tr = lax.broadcasted_iota(jnp.int32, (L, L), 0)
    tc = lax.broadcasted_iota(jnp.int32, (L, L), 1)
    causal = tr >= tc
    b_all = jnp.dot(causal.astype(F32), lf, preferred_element_type=F32,
                    precision=lax.Precision.HIGHEST)
    sr = lax.broadcasted_iota(jnp.int32, (8, LANES), 0)
    sc = lax.broadcasted_iota(jnp.int32, (8, LANES), 1)
    sel = jnp.where(sr < H_C, jnp.where(sc == sr, 1.0, jnp.where(sc == sr + H_C, -1.0, 0.0)), 0.0)
    rt = lax.dot_general(sel, jnp.where(lane < H_C, z, b_all), NT_DIMS,
                         preferred_element_type=F32, precision=lax.Precision.HIGHEST)
    er = lax.broadcasted_iota(jnp.int32, (DH_C, DH_C), 0)
    ec = lax.broadcasted_iota(jnp.int32, (DH_C, DH_C), 1)
    eye = jnp.where(er == ec, 1.0, 0.0).astype(BF16)

    for h in range(H_C):
        cols = slice(h * DH_C, (h + 1) * DH_C)
        q = u[:, h * DH_C:(h + 1) * DH_C]
        k = u[:, D_C + h * DH_C:D_C + (h + 1) * DH_C] * (DH_C ** -0.5)
        v = cv_ref[0, :, cols]
        qb, kb_, vb_ = q.astype(BF16), k.astype(BF16), v.astype(BF16)
        b_col = b_all[:, H_C + h:H_C + h + 1]
        li_col = z[:, h:h + 1]
        m_prev = m_ref[0, h:h + 1, 0:1]
        cs = c_ref[0, h]
        ns = n_ref[0, h:h + 1, :]

        dmat = jnp.where(causal, b_col + rt[h:h + 1, :], NEG)
        inter = b_col + m_prev
        m_t = jnp.maximum(inter, jnp.max(dmat, axis=-1, keepdims=True))
        w_intra = jnp.exp(dmat - m_t)
        w_inter = jnp.exp(inter - m_t)
        a = w_intra * _nt(qb, kb_)
        num = (jnp.dot(a.astype(BF16), vb_, preferred_element_type=F32)
               + w_inter * _nt(qb, cs.astype(BF16)))
        den = jnp.sum(a, axis=-1, keepdims=True) + w_inter * jnp.sum(q * ns, axis=-1, keepdims=True)
        hh = num / jnp.maximum(jnp.abs(den), jnp.exp(-m_t))

        b_last = b_col[L - 1:L, :]
        g = b_last - b_col + li_col
        m_new = jnp.maximum(b_last + m_prev, jnp.max(g, axis=0, keepdims=True))
        ws = jnp.exp(g - m_new)
        decay = jnp.exp(b_last + m_prev - m_new)
        vwt = _nt(eye, (ws * v).astype(BF16)).astype(BF16)
        c_ref[0, h] = decay * cs + jnp.dot(vwt, kb_, preferred_element_type=F32)
        n_ref[0, h:h + 1, :] = decay * ns + jnp.sum(ws * k, axis=0, keepdims=True)
        m_ref[0, h:h + 1, :] = jnp.broadcast_to(m_new, (1, LANES))

        gate = jax.nn.sigmoid(co_ref[0, :, cols])
        hs_ref[0, :, cols] = (_rms(hh, hg_ref[...]) * gate).astype(hs_ref.dtype)


def _mlstm(cqk3, cvo3, cif3, conv0, conv_w, conv_b, gate_b, head_g, c0, n0, m0, L):
    b, t, _ = cqk3.shape
    nc = t // L
    blk = lambda col, w: pl.BlockSpec((1, L, w), lambda bi, c: (bi, c, col))
    const = lambda shape: pl.BlockSpec(shape, lambda bi, c: (0,) * len(shape))
    per_b = lambda shape: pl.BlockSpec((1,) + shape, lambda bi, c: (bi,) + (0,) * len(shape))
    return pl.pallas_call(
        functools.partial(_mlstm_kernel, L=L),
        out_shape=(jax.ShapeDtypeStruct((b, t, D_C), BF16),
                   jax.ShapeDtypeStruct((b, H_C, DH_C, DH_C), F32),
                   jax.ShapeDtypeStruct((b, H_C, DH_C), F32),
                   jax.ShapeDtypeStruct((b, H_C, LANES), F32)),
        grid=(b, nc),
        in_specs=[
            blk(0, 2 * D_C), blk(0, D_C), blk(1, D_C), blk(0, LANES),
            per_b((CONV_PAD, 2 * D_C)), const((CONV_W, 2 * D_C)), const((1, 2 * D_C)),
            const((1, LANES)), const((1, DH_C)),
            per_b((H_C, DH_C, DH_C)), per_b((H_C, DH_C)), per_b((H_C, LANES)),
        ],
        out_specs=(pl.BlockSpec((1, L, D_C), lambda bi, c: (bi, c, 0)),
                   per_b((H_C, DH_C, DH_C)), per_b((H_C, DH_C)), per_b((H_C, LANES))),
        scratch_shapes=[pltpu.VMEM((CONV_PAD + L, 2 * D_C), F32)],
        compiler_params=_cparams(("parallel", "arbitrary")),
        name="mlstm",
    )(cqk3, cvo3, cvo3, cif3, conv0, conv_w, conv_b, gate_b, head_g, c0, n0, m0)


def _mix_kernel(oa_ref, ob_ref, oc_ref, ga_ref, gb_ref, gc_ref, x_ref, wa_ref, wb_ref, wc_ref, wo_ref, o_ref):
    mixed = (ga_ref[...] * jnp.dot(oa_ref[...], wa_ref[...], preferred_element_type=F32)
             + gb_ref[...] * jnp.dot(ob_ref[...], wb_ref[...], preferred_element_type=F32)
             + gc_ref[...] * jnp.dot(oc_ref[...], wc_ref[...], preferred_element_type=F32))
    o_ref[...] = x_ref[...] + jnp.dot(mixed.astype(BF16), wo_ref[...], preferred_element_type=F32)


def _mix(oa, ob, oc, gates, x, l, wa, wb, wc, wo):
    m, d = x.shape
    tm = min(m, 512)
    row = lambda w, col=0: pl.BlockSpec((tm, w), lambda i: (i, col))
    full = lambda a: pl.BlockSpec((None,) + a.shape[1:], lambda i: (l, 0, 0))
    g0 = 0
    return pl.pallas_call(
        _mix_kernel,
        out_shape=jax.ShapeDtypeStruct((m, d), F32),
        grid=(m // tm,),
        in_specs=[row(oa.shape[1]), row(ob.shape[1]), row(oc.shape[1]),
                  row(d, g0), row(d, g0 + 1), row(d, g0 + 2), row(d),
                  full(wa), full(wb), full(wc), full(wo)],
        out_specs=row(d),
        compiler_params=_cparams(("parallel",)),
        name="mix",
    )(oa, ob, oc, gates, gates, gates, x, wa, wb, wc, wo)


def _cross_kernel(x_ref, g_ref, wq_ref, mk_ref, mv_ref, wo_ref, o_ref):
    x = x_ref[0]
    h = _rms(x, g_ref[...]).astype(BF16)
    q = (jnp.dot(h, wq_ref[...], preferred_element_type=F32) * (DH_M ** -0.5 * LOG2E)).astype(BF16)
    nm = mk_ref.shape[1]
    mk = mk_ref[0].reshape(nm, H_M * DH_M).astype(BF16)
    mv = mv_ref[0].reshape(nm, H_M * DH_M).astype(BF16)
    outs = []
    for hd in range(H_M):
        cols = slice(hd * DH_M, (hd + 1) * DH_M)
        s = _nt(q[:, cols], mk[:, cols])
        m = jnp.max(s, axis=-1, keepdims=True)
        e = jnp.exp2(s - m)
        l = jnp.sum(e, axis=-1, keepdims=True)
        o = jnp.dot(e.astype(BF16), mv[:, cols], preferred_element_type=F32) / l
        outs.append(o.astype(BF16))
    o = jnp.concatenate(outs, axis=1)
    o_ref[0] = x + jnp.dot(o, wo_ref[...], preferred_element_type=F32)


def _cross(x3, g, l, wq, mk, mv, wo):
    b, t, d = x3.shape
    tm = min(t, 512)
    nm = mk.shape[2]
    full = lambda a: pl.BlockSpec((None,) + a.shape[1:], lambda bi, i: (l, 0, 0))
    mem = pl.BlockSpec((None, 1, nm, H_M, DH_M), lambda bi, i: (l, bi, 0, 0, 0))
    return pl.pallas_call(
        _cross_kernel,
        out_shape=jax.ShapeDtypeStruct((b, t, d), F32),
        grid=(b, t // tm),
        in_specs=[pl.BlockSpec((1, tm, d), lambda bi, i: (bi, i, 0)),
                  pl.BlockSpec(g.shape, lambda bi, i: (0, 0)), full(wq), mem, mem, full(wo)],
        out_specs=pl.BlockSpec((1, tm, d), lambda bi, i: (bi, i, 0)),
        compiler_params=_cparams(("parallel", "parallel")),
        name="cross",
    )(x3, g, wq, mk, mv, wo)


FFN_TF = 1408
FFN_TM = 512


def _ffn_kernel(x_ref, g_ref, wg_ref, wu_ref, wd_ref, gf_ref, o_ref, *, final_norm):
    x = x_ref[...]
    h = _rms(x, g_ref[...]).astype(BF16)
    chunks = [slice(c, c + FFN_TF) for c in range(0, D_FF, FFN_TF)]
    pre = [(jnp.dot(h, wg_ref[:, c], preferred_element_type=F32),
            jnp.dot(h, wu_ref[:, c], preferred_element_type=F32)) for c in chunks]
    y = x
    for c, (gate, up) in zip(chunks, pre):
        a = (gate * jax.nn.sigmoid(gate) * up).astype(BF16)
        y = y + jnp.dot(a, wd_ref[c, :], preferred_element_type=F32)
    if final_norm:
        y = _rms(y, gf_ref[...])
    o_ref[...] = y


def _ffn(x, g, l, wg, wu, wd, g_final, final_norm):
    m, d = x.shape
    tm = min(m, FFN_TM)
    resident = lambda a: pl.BlockSpec((None,) + a.shape[1:], lambda i: (l, 0, 0),
                                      pipeline_mode=pl.Buffered(1))
    return pl.pallas_call(
        functools.partial(_ffn_kernel, final_norm=final_norm),
        out_shape=jax.ShapeDtypeStruct((m, d), F32),
        grid=(m // tm,),
        in_specs=[
            pl.BlockSpec((tm, d), lambda i: (i, 0)),
            pl.BlockSpec((1, d), lambda i: (0, 0)),
            resident(wg), resident(wu), resident(wd),
            pl.BlockSpec((1, d), lambda i: (0, 0)),
        ],
        out_specs=pl.BlockSpec((tm, d), lambda i: (i, 0)),
        compiler_params=_cparams(("parallel",)),
        name="ffn",
    )(x, g, wg, wu, wd, g_final)


def _layer(x3, l, w, mem_k, mem_v, cache, final_norm, akv_all):
    b, t, d = x3.shape
    x2 = x3.reshape(b * t, d)
    row = lambda a: a.reshape(1, -1)
    lam_init = 0.8 - 0.6 * math.exp(-0.3 * l)

    gates, cqk, cif, q, ak_all, av_all, akv, bkv, cvo = _in_proj(
        x2, row(w['g_mix']), l, w['w_in'], w['w_if'], *akv_all, (b, t))
    q3 = q.reshape(b, t, SEG)
    bkv3 = bkv.reshape(b, t, SEG)
    cqk3 = cqk.reshape(b, t, 2 * D_C)
    keep = t if cache is not None else min(BAND, t)
    b_k = bkv3[:, t - keep:, :HALF].reshape(b, keep, H_B, DH_B)
    b_v = bkv3[:, t - keep:, HALF:].reshape(b, keep, H_B, DH_B)

    gate_b = jnp.concatenate([w['b_i'], w['b_f'], jnp.zeros((LANES - 2 * H_C,), F32)]).reshape(1, LANES)
    if cache is None:
        oa = _diff_prompt(q3, akv.reshape(b, t, SEG), w['lqk'], row(w['a_head_g']), lam_init)
        ob = _band_prompt(q3, bkv3, w['relbias'])
        conv0 = jnp.zeros((b, CONV_PAD, 2 * D_C), F32)
        c0 = jnp.zeros((b, H_C, DH_C, DH_C), F32)
        n0 = jnp.zeros((b, H_C, DH_C), F32)
        m0 = jnp.zeros((b, H_C, LANES), F32)
        L = min(t, MLSTM_L)
    else:
        past = cache['a_k'].shape[2]
        ak3 = ak_all.reshape(DEPTH * b, t, HALF)
        av3 = av_all.reshape(DEPTH * b, t, HALF)
        oa = _diff_sample(q3, ak3, av3, l, cache['a_k'], cache['a_v'],
                          w['lqk'], row(w['a_head_g']), lam_init)
        nband = cache['b_k'].shape[1]
        ob = _band_sample(q3, bkv3, cache['b_k'].reshape(b, nband, -1), cache['b_v'].reshape(b, nband, -1),
                          w['relbias'], past)
        conv0 = jnp.pad(cache['conv'], ((0, 0), (CONV_PAD - (CONV_W - 1), 0), (0, 0)))
        c0, n0 = cache['C'], cache['n']
        m0 = jnp.broadcast_to(cache['m'][:, :, None], (b, H_C, LANES))
        L = t
    oc, c_new, n_new, m_new = _mlstm(cqk3, cvo.reshape(b, t, SEG), cif.reshape(b, t, LANES), conv0,
                                     w['conv_w'], row(w['conv_b']), gate_b, row(w['c_head_g']), c0, n0, m0, L)
    assert t >= CONV_W - 1
    conv_new = cqk3[:, t - (CONV_W - 1):]

    x2 = _mix(oa.reshape(b * t, -1), ob.reshape(b * t, -1), oc.reshape(b * t, -1), gates, x2, l,
              w['w_up_a'], w['w_up_b'], w['w_up_c'], w['w_o'])
    x3 = _cross(x2.reshape(b, t, d), row(w['g_cross']), l, w['w_mq'], mem_k, mem_v, w['w_mo'])
    x2 = _ffn(x3.reshape(b * t, d), row(w['g_ffn']), l, w['w_ff_g'], w['w_ff_u'], w['w_ff_d'],
              row(w['g_final']), final_norm)
    return x2.reshape(b, t, d), (ak_all, av_all), (b_k, b_v, c_new, n_new, m_new[:, :, 0], conv_new)


def kernel(x_prompt, x_sample, cache_a_k, cache_a_v, cache_b_k, cache_b_v, state_c_C, state_c_n, state_c_m, state_c_conv, cache_mem_k, cache_mem_v, mem_prompt, g_mix, w_in, a_lq1, a_lk1, a_lq2, a_lk2, a_head_g, b_rel, c_conv_w, c_conv_b, c_b_i, c_b_f, c_head_g, w_up_a, w_up_b, w_up_c, w_o, g_cross, w_mq, w_mk, w_mv, w_mo, g_ffn, w_ff_g, w_ff_u, w_ff_d, g_final):
    xp, xs = x_prompt, x_sample
    bp = x_prompt.shape[0]
    new_p = [[] for _ in range(6)]
    new_s = [[] for _ in range(6)]
    akv_p = akv_s = (None, None)
    bf = lambda a: a.astype(BF16)
    wp, wif = _prep_w_in(w_in)
    wts = dict(w_up_a=bf(w_up_a), w_up_b=bf(w_up_b), w_up_c=bf(w_up_c), w_o=bf(w_o), w_mq=bf(w_mq),
               w_mo=bf(w_mo), w_ff_g=bf(w_ff_g), w_ff_u=bf(w_ff_u), w_ff_d=bf(w_ff_d))
    mem_k = _mem_proj(mem_prompt, bf(w_mk))
    mem_v = _mem_proj(mem_prompt, bf(w_mv))
    for l in range(DEPTH):
        w = dict(g_mix=g_mix[l], w_in=wp, w_if=wif,
                 lqk=jnp.stack([a_lq1[l], a_lk1[l], a_lq2[l], a_lk2[l]]),
                 a_head_g=a_head_g[l], relbias=_relbias(b_rel[l]), conv_w=c_conv_w[l], conv_b=c_conv_b[l],
                 b_i=c_b_i[l], b_f=c_b_f[l], c_head_g=c_head_g[l],
                 g_cross=g_cross[l], g_ffn=g_ffn[l], g_final=g_final, **wts)
        last = l == DEPTH - 1
        xp, akv_p, st_p = _layer(xp, l, w, mem_k, mem_v, None, last, akv_p)
        cache = dict(a_k=cache_a_k, a_v=cache_a_v, b_k=cache_b_k[l], b_v=cache_b_v[l],
                     C=state_c_C[l], n=state_c_n[l], m=state_c_m[l], conv=state_c_conv[l])
        xs, akv_s, st_s = _layer(xs, l, w, cache_mem_k, cache_mem_v, cache, last, akv_s)
        for lst, a in zip(new_p, st_p):
            lst.append(a)
        for lst, a in zip(new_s, st_s):
            lst.append(a)
    heads = lambda a, b, t: a.reshape(DEPTH, b, t, H_A, DV_A)
    outs_p = ([heads(a, bp, x_prompt.shape[1]) for a in akv_p] + [jnp.stack(a, 0) for a in new_p]
              + [mem_k, mem_v])
    outs_s = [heads(a, xs.shape[0], xs.shape[1]) for a in akv_s] + [jnp.stack(a, 0) for a in new_s]
    return (xp, xs) + tuple(outs_p) + tuple(outs_s)
```

```python
import functools
import math

import jax
import jax.numpy as jnp
from jax import lax
from jax.experimental import pallas as pl
from jax.experimental.pallas import tpu as pltpu

F32 = jnp.float32
BF16 = jnp.bfloat16

D_MODEL = 1024
DEPTH = 2
CHUNK = 64
EPS = 1e-6
NEG = -1e30
LOG2E = math.log2(math.e)
H_A = 4
DH_A = 64
DV_A = 2 * DH_A
H_B = 8
DH_B = 64
BAND_CHUNKS = 8
BAND = BAND_CHUNKS * CHUNK
REL_CLIP = 128
H_C = 4
DH_C = 128
D_C = H_C * DH_C
CONV_W = 4
H_M = 4
DH_M = D_MODEL // H_M
D_FF = -(-8 * D_MODEL // (3 * 256)) * 256

LANES = 128
VMEM_LIMIT = 48 * 1024 * 1024

SEG = 1024
HALF = SEG // 2
GATE_TILES = 3
N_TILES = 8
IN_SIZES = (H_A * DH_A,) * 4 + (H_A * DV_A,) + (H_B * DH_B,) * 3 + (2 * D_C, D_C, D_C, 2 * H_C, 3 * D_MODEL)

ALIBI_SLOPES = tuple(2.0 ** (-8.0 * (i + 1) / H_A) for i in range(H_A))

NT_DIMS = (((1,), (1,)), ((), ()))


def _cparams(sem):
    return pltpu.CompilerParams(dimension_semantics=sem, vmem_limit_bytes=VMEM_LIMIT)


def _rms(x, g):
    ms = jnp.mean(x * x, axis=-1, keepdims=True)
    return x * lax.rsqrt(ms + EPS) * g


def _nt(a, b):
    return lax.dot_general(a, b, NT_DIMS, preferred_element_type=F32)


def _prep_w_in_kernel(w_ref, o_ref, oif_ref):
    offs = [0]
    for size in IN_SIZES:
        offs.append(offs[-1] + size)
    seg = lambda i: w_ref[:, offs[i]:offs[i + 1]]
    a_q1, a_q2, a_k1, a_k2, a_v, b_q, b_k, b_v, c_qk, c_v, c_o = (seg(i) for i in range(11))
    g = seg(12)

    def put(col, val):
        o_ref[:, col:col + val.shape[1]] = val.astype(BF16)

    put(0, g)
    put(GATE_TILES * SEG, c_qk)
    for h in range(H_A):
        hs = slice(h * DH_A, (h + 1) * DH_A)
        base = h * 2 * DH_A
        put(4 * SEG + base, a_q1[:, hs] * (DH_A ** -0.5))
        put(4 * SEG + base + DH_A, a_q2[:, hs] * (DH_A ** -0.5))
        put(5 * SEG + base, a_k1[:, hs])
        put(5 * SEG + base + DH_A, a_k2[:, hs])
    put(4 * SEG + HALF, b_q * (DH_B ** -0.5))
    put(5 * SEG + HALF, a_v)
    put(6 * SEG, b_k)
    put(6 * SEG + HALF, b_v)
    put(7 * SEG, c_v)
    put(7 * SEG + HALF, c_o)
    tile = w_ref[:, offs[11]:offs[11] + LANES]
    lane = lax.broadcasted_iota(jnp.int32, tile.shape, 1)
    oif_ref[...] = jnp.where(lane < 2 * H_C, tile, 0.0).astype(BF16)


def _prep_w_in(w):
    depth, k, n = w.shape
    tk = 256
    n_out = N_TILES * SEG
    return pl.pallas_call(
        _prep_w_in_kernel,
        out_shape=(jax.ShapeDtypeStruct((depth, k, n_out), BF16),
                   jax.ShapeDtypeStruct((depth, k, LANES), BF16)),
        grid=(depth, k // tk),
        in_specs=[pl.BlockSpec((None, tk, n), lambda l, i: (l, i, 0))],
        out_specs=(pl.BlockSpec((None, tk, n_out), lambda l, i: (l, i, 0)),
                   pl.BlockSpec((None, tk, LANES), lambda l, i: (l, i, 0))),
        compiler_params=_cparams(("parallel", "parallel")),
        name="prep_w_in",
    )(w)


IN_PROJ_TM = 256


def _in_proj_kernel(*refs, aliased, heads5d):
    x_ref, g_ref, w_ref, wif_ref = refs[:4]
    og_ref, ocqk_ref, ocif_ref, oq_ref, oak_ref, oav_ref, oakv_ref, obkv_ref, ocvo_ref = refs[6 if aliased else 4:]
    h = _rms(x_ref[...], g_ref[...]).astype(BF16)
    cols = lambda j, lo=0, hi=SEG: jnp.dot(h, w_ref[:, j * SEG + lo:j * SEG + hi], preferred_element_type=F32)

    ocif_ref[...] = jnp.dot(h, wif_ref[...], preferred_element_type=F32)
    for j in range(GATE_TILES):
        og_ref[:, j * SEG:(j + 1) * SEG] = jax.nn.sigmoid(cols(j)).astype(og_ref.dtype)
    ocqk_ref[...] = cols(GATE_TILES)
    oq_ref[...] = (cols(4) * LOG2E).astype(oq_ref.dtype)
    for out_ref, off in ((oak_ref, 0), (oav_ref, HALF)):
        a = cols(5, off, off + HALF)
        oakv_ref[:, off:off + HALF] = a.astype(oakv_ref.dtype)
        if heads5d:
            out_ref[0, 0] = a.reshape(a.shape[0], H_A, DV_A)
        else:
            out_ref[0] = a
    obkv_ref[...] = cols(6)
    ocvo_ref[...] = cols(7)


def _in_proj(x, g, l, wp, wif, ak_all, av_all, bt):
    m, k = x.shape
    tm = min(m, IN_PROJ_TM)
    aliased = ak_all is not None
    b, t = bt
    heads5d = t % tm == 0
    if heads5d:
        stacked = jax.ShapeDtypeStruct((DEPTH, b, t, H_A, DV_A), F32)
        layer_row = pl.BlockSpec((1, 1, tm, H_A, DV_A), lambda i: (l, i // (t // tm), i % (t // tm), 0, 0))
    else:
        stacked = jax.ShapeDtypeStruct((DEPTH, m, HALF), F32)
        layer_row = pl.BlockSpec((1, tm, HALF), lambda i: (l, i, 0))
    resident = lambda a: pl.BlockSpec((None,) + a.shape[1:], lambda i: (l, 0, 0), pipeline_mode=pl.Buffered(1))
    in_specs = [pl.BlockSpec((tm, k), lambda i: (i, 0)), pl.BlockSpec((1, k), lambda i: (0, 0)),
                resident(wp), resident(wif)]
    args = [x, g, wp, wif]
    if aliased:
        in_specs += [pl.BlockSpec(memory_space=pl.ANY)] * 2
        args += [ak_all, av_all]
    row = lambda w: pl.BlockSpec((tm, w), lambda i: (i, 0))
    sds = jax.ShapeDtypeStruct
    return pl.pallas_call(
        functools.partial(_in_proj_kernel, aliased=aliased, heads5d=heads5d),
        out_shape=(sds((m, GATE_TILES * SEG), BF16), sds((m, SEG), F32), sds((m, LANES), F32),
                   sds((m, SEG), BF16), stacked, stacked, sds((m, SEG), BF16),
                   sds((m, SEG), F32), sds((m, SEG), F32)),
        grid=(m // tm,),
        in_specs=in_specs,
        out_specs=(row(GATE_TILES * SEG), row(SEG), row(LANES), row(SEG), layer_row, layer_row, row(SEG),
                   row(SEG), row(SEG)),
        input_output_aliases={4: 4, 5: 5} if aliased else {},
        compiler_params=_cparams(("parallel",)),
        name="in_proj",
    )(*args)


def _mem_proj_kernel(x_ref, w_ref, o_ref):
    y = jnp.dot(x_ref[0].astype(BF16), w_ref[...], preferred_element_type=F32)
    o_ref[...] = y.reshape(o_ref.shape)


def _mem_proj(mem, w):
    b, nm, k = mem.shape
    depth, _, n = w.shape
    return pl.pallas_call(
        _mem_proj_kernel,
        out_shape=jax.ShapeDtypeStruct((depth, b, nm, H_M, DH_M), F32),
        grid=(depth, b),
        in_specs=[pl.BlockSpec((1, nm, k), lambda l, bi: (bi, 0, 0)),
                  pl.BlockSpec((None, k, n), lambda l, bi: (l, 0, 0))],
        out_specs=pl.BlockSpec((None, None, nm, H_M, DH_M), lambda l, bi: (l, bi, 0, 0, 0)),
        compiler_params=_cparams(("parallel", "parallel")),
        name="mem_proj",
    )(mem, w)


def _online_updates(jobs):
    scores, staged = {}, {}

    def softmax(i):
        s, (acc, m, l) = scores.pop(i), jobs[i][2]
        slabs = [s[:, j * LANES:(j + 1) * LANES] for j in range(s.shape[1] // LANES)]
        m_old = m[...]
        m_new = jnp.maximum(m_old, jnp.max(functools.reduce(jnp.maximum, slabs), axis=-1, keepdims=True))
        alpha = jnp.exp2(m_old - m_new)
        ps = [jnp.exp2(c - m_new) for c in slabs]
        l[...] = alpha * l[...] + jnp.sum(functools.reduce(jnp.add, ps), axis=-1, keepdims=True)
        m[...] = m_new
        staged[i] = (alpha, jnp.concatenate(ps, axis=1).astype(BF16))

    def pv(i):
        (alpha, p), (_, v, (acc, m, l)) = staged.pop(i), jobs[i]
        acc[...] = alpha * acc[...] + jnp.dot(p, v, preferred_element_type=F32)

    n = len(jobs)
    for step in range(n + 2):
        if step < n:
            scores[step] = jobs[step][0]()
        if 0 <= step - 1 < n:
            softmax(step - 1)
        if 0 <= step - 2 < n:
            pv(step - 2)


def _split_halves(q):
    lane = lax.broadcasted_iota(jnp.int32, q.shape, 1)
    lo = jnp.where(lane < DH_A, q, 0.0).astype(BF16)
    hi = jnp.where(lane >= DH_A, q, 0.0).astype(BF16)
    return lo, hi


def _lambda(lqk, lam_init):
    e1 = jnp.exp(jnp.sum(lqk[0:1, :] * lqk[1:2, :], axis=-1, keepdims=True))
    e2 = jnp.exp(jnp.sum(lqk[2:3, :] * lqk[3:4, :], axis=-1, keepdims=True))
    return e1 - e2 + lam_init


def _diff_prompt_kernel(slopes_ref, lqk_ref, hg_ref, q_ref, k_ref, v_ref, o_ref,
                        diag_bias, acc1, acc2, m1, l1, m2, l2, *, tq, lam_init):
    h = pl.program_id(1)
    qi = pl.program_id(2)
    slope = slopes_ref[h] * LOG2E
    kb, vb = k_ref.at[0], v_ref.at[0]
    q0 = pl.multiple_of(qi * tq, tq)

    @pl.when(qi == 0)
    def _():
        r = lax.broadcasted_iota(jnp.int32, (tq, tq), 0)
        c = lax.broadcasted_iota(jnp.int32, (tq, tq), 1)
        bias = slope * (r - jnp.abs(r - c)).astype(F32)
        diag_bias[...] = jnp.where((c // CHUNK) <= (r // CHUNK), bias, NEG)

    qa, qb = _split_halves(q_ref[0])
    states = ((acc1, m1, l1), (acc2, m2, l2))
    for acc, m, l in states:
        m[...] = jnp.full(m.shape, NEG, F32)
        l[...] = jnp.zeros(l.shape, F32)
        acc[...] = jnp.zeros(acc.shape, F32)

    def full_tiles(k_starts):
        jobs = []
        for k0 in k_starts:
            k = kb[pl.ds(k0, tq), :]
            v = vb[pl.ds(k0, tq), :]
            kpos = (k0 - q0) + lax.broadcasted_iota(jnp.int32, (1, tq), 1)
            bias = slope * kpos.astype(F32)
            jobs += [(functools.partial(lambda qm, k, bias: _nt(qm, k) + bias, qm, k, bias), v, st)
                     for qm, st in ((qa, states[0]), (qb, states[1]))]
        _online_updates(jobs)

    def tile_pair(it, carry):
        k0 = pl.multiple_of(it * (2 * tq), 2 * tq)
        full_tiles([k0, pl.multiple_of(k0 + tq, tq)])
        return carry

    lax.fori_loop(0, qi // 2, tile_pair, 0)

    @pl.when(qi % 2 == 1)
    def _():
        full_tiles([pl.multiple_of((qi - 1) * tq, tq)])

    half = tq // 2
    jobs = []
    for rows, nk in ((slice(0, half), half), (slice(half, tq), tq)):
        k = kb[pl.ds(q0, nk), :]
        v = vb[pl.ds(q0, nk), :]
        bias = diag_bias[rows, 0:nk]
        jobs += [(functools.partial(lambda qm, k, bias: _nt(qm, k) + bias, qm[rows], k, bias), v,
                  tuple(ref.at[rows] for ref in st))
                 for qm, st in ((qa, states[0]), (qb, states[1]))]
    _online_updates(jobs)

    lam = _lambda(lqk_ref[...], lam_init)
    o = acc1[...] / l1[...] - lam * (acc2[...] / l2[...])
    o_ref[0] = (_rms(o, hg_ref[...]) * (1.0 - lam_init)).astype(o_ref.dtype)


def _diff_prompt(q3, kv3, lqk, head_g, lam_init):
    b, t, _ = q3.shape
    tq = min(t, 512)
    slopes = jnp.asarray(ALIBI_SLOPES, F32)
    return pl.pallas_call(
        functools.partial(_diff_prompt_kernel, tq=tq, lam_init=lam_init),
        out_shape=jax.ShapeDtypeStruct((b, t, H_A * DV_A), BF16),
        grid=(b, H_A, t // tq),
        in_specs=[
            pl.BlockSpec(memory_space=pltpu.SMEM),
            pl.BlockSpec((4, DH_A), lambda bi, h, qi: (0, 0)),
            pl.BlockSpec((1, DV_A), lambda bi, h, qi: (0, 0)),
            pl.BlockSpec((1, tq, LANES), lambda bi, h, qi: (bi, qi, h)),
            pl.BlockSpec((1, t, LANES), lambda bi, h, qi: (bi, 0, h)),
            pl.BlockSpec((1, t, LANES), lambda bi, h, qi: (bi, 0, H_A + h)),
        ],
        out_specs=pl.BlockSpec((1, tq, LANES), lambda bi, h, qi: (bi, qi, h)),
        scratch_shapes=[
            pltpu.VMEM((tq, tq), F32),
            pltpu.VMEM((tq, LANES), F32), pltpu.VMEM((tq, LANES), F32),
            pltpu.VMEM((tq, LANES), F32), pltpu.VMEM((tq, LANES), F32),
            pltpu.VMEM((tq, LANES), F32), pltpu.VMEM((tq, LANES), F32),
        ],
        compiler_params=_cparams(("parallel", "parallel", "arbitrary")),
        name="diff_prompt",
    )(slopes, lqk, head_g, q3, kv3, kv3)


def _pieces_attention(qm, pieces):
    ss = [_nt(qm, k) + bias for k, _, bias in pieces]
    m = functools.reduce(jnp.maximum, [jnp.max(s, axis=-1, keepdims=True) for s in ss])
    es = [jnp.exp2(s - m) for s in ss]
    l = functools.reduce(jnp.add, [jnp.sum(e, axis=-1, keepdims=True) for e in es])
    o = functools.reduce(jnp.add, [jnp.dot(e.astype(BF16), v, preferred_element_type=F32)
                                   for e, (_, v, _) in zip(es, pieces)])
    return o / l


def _diff_sample_kernel(lqk_ref, hg_ref, q_ref, kc_ref, vc_ref, kn_ref, vn_ref, o_ref, *, lam_init):
    t = q_ref.shape[1]
    past = kc_ref.shape[1]
    lam = _lambda(lqk_ref[...], lam_init)
    rc = lax.broadcasted_iota(jnp.int32, (t, past), 0)
    cc = lax.broadcasted_iota(jnp.int32, (t, past), 1)
    rn = lax.broadcasted_iota(jnp.int32, (t, t), 0)
    cn = lax.broadcasted_iota(jnp.int32, (t, t), 1)
    dist_c = jnp.abs(past + rc - cc).astype(F32)
    dist_n = jnp.abs(rn - cn).astype(F32)
    ok_c = (cc // CHUNK) <= ((past + rc) // CHUNK)
    ok_n = ((past + cn) // CHUNK) <= ((past + rn) // CHUNK)
    kc = kc_ref[0].reshape(past, H_A * DV_A).astype(BF16)
    vc = vc_ref[0].reshape(past, H_A * DV_A).astype(BF16)
    for h in range(H_A):
        cols = slice(h * LANES, (h + 1) * LANES)
        bias_c = jnp.where(ok_c, (-ALIBI_SLOPES[h] * LOG2E) * dist_c, NEG)
        bias_n = jnp.where(ok_n, (-ALIBI_SLOPES[h] * LOG2E) * dist_n, NEG)
        pieces = [(kc[:, cols], vc[:, cols], bias_c),
                  (kn_ref[0, :, cols].astype(BF16), vn_ref[0, :, cols].astype(BF16), bias_n)]
        qa, qb = _split_halves(q_ref[0, :, cols])
        o = _pieces_attention(qa, pieces) - lam * _pieces_attention(qb, pieces)
        o_ref[0, :, cols] = (_rms(o, hg_ref[...]) * (1.0 - lam_init)).astype(o_ref.dtype)


def _diff_sample(q3, kn3, vn3, l, kc, vc, lqk, head_g, lam_init):
    b, t, _ = q3.shape
    past = kc.shape[2]
    w = H_A * DV_A
    new = pl.BlockSpec((1, t, w), lambda bi: (l * b + bi, 0, 0))
    cache = pl.BlockSpec((None, 1, past, H_A, DV_A), lambda bi: (l, bi, 0, 0, 0))
    return pl.pallas_call(
        functools.partial(_diff_sample_kernel, lam_init=lam_init),
        out_shape=jax.ShapeDtypeStruct((b, t, w), BF16),
        grid=(b,),
        in_specs=[
            pl.BlockSpec((4, DH_A), lambda bi: (0, 0)),
            pl.BlockSpec((1, DV_A), lambda bi: (0, 0)),
            pl.BlockSpec((1, t, w), lambda bi: (bi, 0, 0)), cache, cache, new, new,
        ],
        out_specs=pl.BlockSpec((1, t, w), lambda bi: (bi, 0, 0)),
        compiler_params=_cparams(("parallel",)),
        name="diff_sample",
    )(lqk, head_g, q3, kc, vc, kn3, vn3)


QPAIR = 2 * CHUNK
WIN = BAND + QPAIR


def _relbias_kernel(tab_ref, o_ref):
    h = pl.program_id(0)
    r = lax.broadcasted_iota(jnp.int32, (QPAIR, WIN), 0)
    j = lax.broadcasted_iota(jnp.int32, (QPAIR, WIN), 1)
    rel = jnp.clip(r - j + BAND, -REL_CLIP, REL_CLIP) + REL_CLIP
    qc = r // CHUNK
    kc = j // CHUNK - BAND_CHUNKS
    allowed = (kc <= qc) & (kc >= qc - BAND_CHUNKS)

    far = BAND - REL_CLIP
    near_rel = rel[:, far:]

    near = jnp.zeros((QPAIR, WIN - far), F32)
    for t in range(2 * REL_CLIP + 1):
        near = jnp.where(near_rel == t, tab_ref[h, t], near)
    bias = jnp.concatenate([jnp.full((QPAIR, far), tab_ref[h, 2 * REL_CLIP], F32), near], axis=1)
    o_ref[0] = jnp.where(allowed, bias * LOG2E, NEG)


def _relbias(table):
    return pl.pallas_call(
        _relbias_kernel,
        out_shape=jax.ShapeDtypeStruct((H_B, QPAIR, WIN), F32),
        grid=(H_B,),
        in_specs=[pl.BlockSpec(memory_space=pltpu.SMEM)],
        out_specs=pl.BlockSpec((1, QPAIR, WIN), lambda h: (h, 0, 0)),
        compiler_params=_cparams(("arbitrary",)),
        name="relbias",
    )(table)


BAND_TQ = 512
BAND_UNROLL = 2


def _band_prompt_kernel(bias_ref, q_ref, kp_ref, kc_ref, vp_ref, vc_ref, o_ref, kcat, vcat):
    i = pl.program_id(1)
    kcat[0:BAND_TQ, :] = kp_ref[0].astype(BF16)
    kcat[BAND_TQ:, :] = kc_ref[0].astype(BF16)
    vcat[0:BAND_TQ, :] = vp_ref[0].astype(BF16)
    vcat[BAND_TQ:, :] = vc_ref[0].astype(BF16)
    lane = lax.broadcasted_iota(jnp.int32, (QPAIR, LANES), 1)

    def pair_blocks(it, carry, *, first_block):
        starts = [pl.multiple_of((it * BAND_UNROLL + u) * QPAIR, QPAIR) for u in range(BAND_UNROLL)]
        units = [(r0, hp) for r0 in starts for hp in range(H_B // 2)]
        queries = {u: q_ref[0, pl.ds(r0, QPAIR), :] for u, r0 in enumerate(starts)}
        scores, probs, outs = {}, {}, {}

        def qk(i):
            r0, hp = units[i]
            cols = slice(hp * LANES, (hp + 1) * LANES)
            qs = jnp.concatenate(_split_halves(queries[i // (H_B // 2)][:, cols]), axis=0)
            s = _nt(qs, kcat[pl.ds(r0, WIN), cols]) + bias_ref[hp]
            if first_block:
                kpos = r0 - BAND_TQ + lax.broadcasted_iota(jnp.int32, (1, WIN), 1)
                s = s + jnp.where(kpos >= 0, 0.0, NEG)
            scores[i] = s

        def softmax(i):
            s = scores.pop(i)
            e = jnp.exp2(s - jnp.max(s, axis=-1, keepdims=True))
            probs[i] = (e.astype(BF16), jnp.sum(e, axis=-1, keepdims=True))

        def pv(i):
            r0, hp = units[i]
            e, l = probs.pop(i)
            o = jnp.dot(e, vcat[pl.ds(r0, WIN), hp * LANES:(hp + 1) * LANES], preferred_element_type=F32) / l
            outs[i] = jnp.where(lane < DH_B, o[:QPAIR], o[QPAIR:])

        n = len(units)
        for step in range(n + 2):
            if step < n:
                qk(step)
            if 0 <= step - 1 < n:
                softmax(step - 1)
            if 0 <= step - 2 < n:
                pv(step - 2)
        for u, r0 in enumerate(starts):
            row = [outs[u * (H_B // 2) + hp] for hp in range(H_B // 2)]
            o_ref[0, pl.ds(r0, QPAIR), :] = jnp.concatenate(row, axis=1).astype(o_ref.dtype)
        return carry

    trips = BAND_TQ // QPAIR // BAND_UNROLL

    @pl.when(i == 0)
    def _():
        lax.fori_loop(0, trips, functools.partial(pair_blocks, first_block=True), 0)

    @pl.when(i > 0)
    def _():
        lax.fori_loop(0, trips, functools.partial(pair_blocks, first_block=False), 0)


def _band_prompt(q3, kv3, bias):
    b, t, _ = q3.shape
    assert t % BAND_TQ == 0
    w = H_B * DH_B
    cq, ck, cv = 1, 0, 1
    prev = lambda col: pl.BlockSpec((1, BAND_TQ, w), lambda bi, i: (bi, jnp.maximum(i - 1, 0), col))
    cur = lambda col: pl.BlockSpec((1, BAND_TQ, w), lambda bi, i: (bi, i, col))
    return pl.pallas_call(
        _band_prompt_kernel,
        out_shape=jax.ShapeDtypeStruct((b, t, w), BF16),
        grid=(b, t // BAND_TQ),
        in_specs=[
            pl.BlockSpec((H_B // 2, 2 * QPAIR, WIN), lambda bi, i: (0, 0, 0)),
            cur(cq), prev(ck), cur(ck), prev(cv), cur(cv),
        ],
        out_specs=pl.BlockSpec((1, BAND_TQ, w), lambda bi, i: (bi, i, 0)),
        scratch_shapes=[pltpu.VMEM((2 * BAND_TQ, w), BF16), pltpu.VMEM((2 * BAND_TQ, w), BF16)],
        compiler_params=_cparams(("parallel", "arbitrary")),
        name="band_prompt",
    )(bias.reshape(H_B // 2, 2 * QPAIR, WIN), q3, kv3, kv3, kv3, kv3)


def _band_sample_kernel(bc_ref, bn_ref, q_ref, kc_ref, vc_ref, kn_ref, vn_ref, o_ref, *, past):
    t = q_ref.shape[1]
    nband = kc_ref.shape[1]
    lane = lax.broadcasted_iota(jnp.int32, (t, LANES), 1)
    qpos_c = past + lax.broadcasted_iota(jnp.int32, (t, nband), 0)
    kpos_c = past - nband + lax.broadcasted_iota(jnp.int32, (t, nband), 1)
    qpos_n = past + lax.broadcasted_iota(jnp.int32, (t, t), 0)
    kpos_n = past + lax.broadcasted_iota(jnp.int32, (t, t), 1)

    def allowed(qpos, kpos):
        qc, kc = qpos // CHUNK, kpos // CHUNK
        return (kpos >= 0) & (kc <= qc) & (kc >= qc - BAND_CHUNKS)

    ok_c = allowed(qpos_c, kpos_c)
    ok_n = allowed(qpos_n, kpos_n)
    for hp in range(H_B // 2):
        cols = slice(hp * LANES, (hp + 1) * LANES)
        halves = []
        for sub, qm in enumerate(_split_halves(q_ref[0, :, cols])):
            hd = 2 * hp + sub
            pieces = [(kc_ref[0, :, cols].astype(BF16), vc_ref[0, :, cols].astype(BF16),
                       jnp.where(ok_c, bc_ref[hd], NEG)),
                      (kn_ref[0, :, cols].astype(BF16), vn_ref[0, :, cols].astype(BF16),
                       jnp.where(ok_n, bn_ref[hd], NEG))]
            halves.append(_pieces_attention(qm, pieces))
        o_ref[0, :, cols] = jnp.where(lane < DH_B, halves[0], halves[1]).astype(o_ref.dtype)


def _band_sample(q3, kv3, kc, vc, bias, past):
    b, t, _ = q3.shape
    nband = kc.shape[1]
    assert nband == BAND and t <= CHUNK
    w = H_B * DH_B
    bias_c = bias[:, :t, :nband]
    bias_n = bias[:, :t, nband:nband + t]
    blk = lambda col: pl.BlockSpec((1, t, w), lambda bi: (bi, 0, col))
    cache = pl.BlockSpec((1, nband, w), lambda bi: (bi, 0, 0))
    return pl.pallas_call(
        functools.partial(_band_sample_kernel, past=past),
        out_shape=jax.ShapeDtypeStruct((b, t, w), BF16),
        grid=(b,),
        in_specs=[
            pl.BlockSpec((H_B, t, nband), lambda bi: (0, 0, 0)),
            pl.BlockSpec((H_B, t, t), lambda bi: (0, 0, 0)),
            blk(1), cache, cache, blk(0), blk(1),
        ],
        out_specs=pl.BlockSpec((1, t, w), lambda bi: (bi, 0, 0)),
        compiler_params=_cparams(("parallel",)),
        name="band_sample",
    )(bias_c, bias_n, q3, kc, vc, kv3, kv3)


CONV_PAD = 8
MLSTM_L = 256


def _mlstm_kernel(cqk_ref, cv_ref, co_ref, cif_ref, conv0_ref, cw_ref, cb_ref, gb_ref, hg_ref,
                  c0_ref, n0_ref, m0_ref, hs_ref, c_ref, n_ref, m_ref, xext, *, L):
    @pl.when(pl.program_id(1) == 0)
    def _():
        xext[0:CONV_PAD, :] = conv0_ref[0]
        c_ref[...] = c0_ref[...]
        n_ref[...] = n0_ref[...]
        m_ref[...] = m0_ref[...]

    xext[CONV_PAD:CONV_PAD + L, :] = cqk_ref[0]
    base = CONV_PAD - (CONV_W - 1)
    u = 0.0
    for j in range(CONV_W):
        u = u + xext[base + j:base + j + L, :] * cw_ref[j:j + 1, :]
    u = cb_ref[...] + u
    u = u * jax.nn.sigmoid(u)
    xext[0:CONV_PAD, :] = xext[L:L + CONV_PAD, :]

    z = cif_ref[0] + gb_ref[...]
    lane = lax.broadcasted_iota(jnp.int32, (L, LANES), 1)
    lf = jnp.minimum(z, 0.0) - jnp.log1p(jnp.exp(-jnp.abs(z)))
    tr = lax.broadcasted_iota(jnp.int32, (L, L), 0)
    tc = lax.broadcasted_iota(jnp.int32, (L, L), 1)
    causal = tr >= tc
    b_all = jnp.dot(causal.astype(F32), lf, preferred_element_type=F32,
                    precision=lax.Precision.HIGHEST)
    sr = lax.broadcasted_iota(jnp.int32, (8, LANES), 0)
    sc = lax.broadcasted_iota(jnp.int32, (8, LANES), 1)
    sel = jnp.where(sr < H_C, jnp.where(sc == sr, 1.0, jnp.where(sc == sr + H_C, -1.0, 0.0)), 0.0)
    rt = lax.dot_general(sel, jnp.where(lane < H_C, z, b_all), NT_DIMS,
                         preferred_element_type=F32, precision=lax.Precision.HIGHEST)
    er = lax.broadcasted_iota(jnp.int32, (DH_C, DH_C), 0)
    ec = lax.broadcasted_iota(jnp.int32, (DH_C, DH_C), 1)
    eye = jnp.where(er == ec, 1.0, 0.0).astype(BF16)

    for h in range(H_C):
        cols = slice(h * DH_C, (h + 1) * DH_C)
        q = u[:, h * DH_C:(h + 1) * DH_C]
        k = u[:, D_C + h * DH_C:D_C + (h + 1) * DH_C] * (DH_C ** -0.5)
        v = cv_ref[0, :, cols]
        qb, kb_, vb_ = q.astype(BF16), k.astype(BF16), v.astype(BF16)
        b_col = b_all[:, H_C + h:H_C + h + 1]
        li_col = z[:, h:h + 1]
        m_prev = m_ref[0, h:h + 1, 0:1]
        cs = c_ref[0, h]
        ns = n_ref[0, h:h + 1, :]

        dmat = jnp.where(causal, b_col + rt[h:h + 1, :], NEG)
        inter = b_col + m_prev
        m_t = jnp.maximum(inter, jnp.max(dmat, axis=-1, keepdims=True))
        w_intra = jnp.exp(dmat - m_t)
        w_inter = jnp.exp(inter - m_t)
        a = w_intra * _nt(qb, kb_)
        num = (jnp.dot(a.astype(BF16), vb_, preferred_element_type=F32)
               + w_inter * _nt(qb, cs.astype(BF16)))
        den = jnp.sum(a, axis=-1, keepdims=True) + w_inter * jnp.sum(q * ns, axis=-1, keepdims=True)
        hh = num / jnp.maximum(jnp.abs(den), jnp.exp(-m_t))

        b_last = b_col[L - 1:L, :]
        g = b_last - b_col + li_col
        m_new = jnp.maximum(b_last + m_prev, jnp.max(g, axis=0, keepdims=True))
        ws = jnp.exp(g - m_new)
        decay = jnp.exp(b_last + m_prev - m_new)
        vwt = _nt(eye, (ws * v).astype(BF16)).astype(BF16)
        c_ref[0, h] = decay * cs + jnp.dot(vwt, kb_, preferred_element_type=F32)
        n_ref[0, h:h + 1, :] = decay * ns + jnp.sum(ws * k, axis=0, keepdims=True)
        m_ref[0, h:h + 1, :] = jnp.broadcast_to(m_new, (1, LANES))

        gate = jax.nn.sigmoid(co_ref[0, :, cols])
        hs_ref[0, :, cols] = (_rms(hh, hg_ref[...]) * gate).astype(hs_ref.dtype)


def _mlstm(cqk3, cvo3, cif3, conv0, conv_w, conv_b, gate_b, head_g, c0, n0, m0, L):
    b, t, _ = cqk3.shape
    nc = t // L
    blk = lambda col, w: pl.BlockSpec((1, L, w), lambda bi, c: (bi, c, col))
    const = lambda shape: pl.BlockSpec(shape, lambda bi, c: (0,) * len(shape))
    per_b = lambda shape: pl.BlockSpec((1,) + shape, lambda bi, c: (bi,) + (0,) * len(shape))
    return pl.pallas_call(
        functools.partial(_mlstm_kernel, L=L),
        out_shape=(jax.ShapeDtypeStruct((b, t, D_C), BF16),
                   jax.ShapeDtypeStruct((b, H_C, DH_C, DH_C), F32),
                   jax.ShapeDtypeStruct((b, H_C, DH_C), F32),
                   jax.ShapeDtypeStruct((b, H_C, LANES), F32)),
        grid=(b, nc),
        in_specs=[
            blk(0, 2 * D_C), blk(0, D_C), blk(1, D_C), blk(0, LANES),
            per_b((CONV_PAD, 2 * D_C)), const((CONV_W, 2 * D_C)), const((1, 2 * D_C)),
            const((1, LANES)), const((1, DH_C)),
            per_b((H_C, DH_C, DH_C)), per_b((H_C, DH_C)), per_b((H_C, LANES)),
        ],
        out_specs=(pl.BlockSpec((1, L, D_C), lambda bi, c: (bi, c, 0)),
                   per_b((H_C, DH_C, DH_C)), per_b((H_C, DH_C)), per_b((H_C, LANES))),
        scratch_shapes=[pltpu.VMEM((CONV_PAD + L, 2 * D_C), F32)],
        compiler_params=_cparams(("parallel", "arbitrary")),
        name="mlstm",
    )(cqk3, cvo3, cvo3, cif3, conv0, conv_w, conv_b, gate_b, head_g, c0, n0, m0)


def _mix_kernel(oa_ref, ob_ref, oc_ref, ga_ref, gb_ref, gc_ref, x_ref, wa_ref, wb_ref, wc_ref, wo_ref, o_ref):
    mixed = (ga_ref[...] * jnp.dot(oa_ref[...], wa_ref[...], preferred_element_type=F32)
             + gb_ref[...] * jnp.dot(ob_ref[...], wb_ref[...], preferred_element_type=F32)
             + gc_ref[...] * jnp.dot(oc_ref[...], wc_ref[...], preferred_element_type=F32))
    o_ref[...] = x_ref[...] + jnp.dot(mixed.astype(BF16), wo_ref[...], preferred_element_type=F32)


def _mix(oa, ob, oc, gates, x, l, wa, wb, wc, wo):
    m, d = x.shape
    tm = min(m, 512)
    row = lambda w, col=0: pl.BlockSpec((tm, w), lambda i: (i, col))
    full = lambda a: pl.BlockSpec((None,) + a.shape[1:], lambda i: (l, 0, 0))
    g0 = 0
    return pl.pallas_call(
        _mix_kernel,
        out_shape=jax.ShapeDtypeStruct((m, d), F32),
        grid=(m // tm,),
        in_specs=[row(oa.shape[1]), row(ob.shape[1]), row(oc.shape[1]),
                  row(d, g0), row(d, g0 + 1), row(d, g0 + 2), row(d),
                  full(wa), full(wb), full(wc), full(wo)],
        out_specs=row(d),
        compiler_params=_cparams(("parallel",)),
        name="mix",
    )(oa, ob, oc, gates, gates, gates, x, wa, wb, wc, wo)


def _cross_kernel(x_ref, g_ref, wq_ref, mk_ref, mv_ref, wo_ref, o_ref):
    x = x_ref[0]
    h = _rms(x, g_ref[...]).astype(BF16)
    q = (jnp.dot(h, wq_ref[...], preferred_element_type=F32) * (DH_M ** -0.5 * LOG2E)).astype(BF16)
    nm = mk_ref.shape[1]
    mk = mk_ref[0].reshape(nm, H_M * DH_M).astype(BF16)
    mv = mv_ref[0].reshape(nm, H_M * DH_M).astype(BF16)
    outs = []
    for hd in range(H_M):
        cols = slice(hd * DH_M, (hd + 1) * DH_M)
        s = _nt(q[:, cols], mk[:, cols])
        m = jnp.max(s, axis=-1, keepdims=True)
        e = jnp.exp2(s - m)
        l = jnp.sum(e, axis=-1, keepdims=True)
        o = jnp.dot(e.astype(BF16), mv[:, cols], preferred_element_type=F32) / l
        outs.append(o.astype(BF16))
    o = jnp.concatenate(outs, axis=1)
    o_ref[0] = x + jnp.dot(o, wo_ref[...], preferred_element_type=F32)


def _cross(x3, g, l, wq, mk, mv, wo):
    b, t, d = x3.shape
    tm = min(t, 512)
    nm = mk.shape[2]
    full = lambda a: pl.BlockSpec((None,) + a.shape[1:], lambda bi, i: (l, 0, 0))
    mem = pl.BlockSpec((None, 1, nm, H_M, DH_M), lambda bi, i: (l, bi, 0, 0, 0))
    return pl.pallas_call(
        _cross_kernel,
        out_shape=jax.ShapeDtypeStruct((b, t, d), F32),
        grid=(b, t // tm),
        in_specs=[pl.BlockSpec((1, tm, d), lambda bi, i: (bi, i, 0)),
                  pl.BlockSpec(g.shape, lambda bi, i: (0, 0)), full(wq), mem, mem, full(wo)],
        out_specs=pl.BlockSpec((1, tm, d), lambda bi, i: (bi, i, 0)),
        compiler_params=_cparams(("parallel", "parallel")),
        name="cross",
    )(x3, g, wq, mk, mv, wo)


FFN_TF = 1408
FFN_TM = 512


def _ffn_kernel(x_ref, g_ref, wg_ref, wu_ref, wd_ref, gf_ref, o_ref, *, final_norm):
    x = x_ref[...]
    h = _rms(x, g_ref[...]).astype(BF16)
    chunks = [slice(c, c + FFN_TF) for c in range(0, D_FF, FFN_TF)]
    pre = [(jnp.dot(h, wg_ref[:, c], preferred_element_type=F32),
            jnp.dot(h, wu_ref[:, c], preferred_element_type=F32)) for c in chunks]
    y = x
    for c, (gate, up) in zip(chunks, pre):
        a = (gate * jax.nn.sigmoid(gate) * up).astype(BF16)
        y = y + jnp.dot(a, wd_ref[c, :], preferred_element_type=F32)
    if final_norm:
        y = _rms(y, gf_ref[...])
    o_ref[...] = y


def _ffn(x, g, l, wg, wu, wd, g_final, final_norm):
    m, d = x.shape
    tm = min(m, FFN_TM)
    resident = lambda a: pl.BlockSpec((None,) + a.shape[1:], lambda i: (l, 0, 0),
                                      pipeline_mode=pl.Buffered(1))
    return pl.pallas_call(
        functools.partial(_ffn_kernel, final_norm=final_norm),
        out_shape=jax.ShapeDtypeStruct((m, d), F32),
        grid=(m // tm,),
        in_specs=[
            pl.BlockSpec((tm, d), lambda i: (i, 0)),
            pl.BlockSpec((1, d), lambda i: (0, 0)),
            resident(wg), resident(wu), resident(wd),
            pl.BlockSpec((1, d), lambda i: (0, 0)),
        ],
        out_specs=pl.BlockSpec((tm, d), lambda i: (i, 0)),
        compiler_params=_cparams(("parallel",)),
        name="ffn",
    )(x, g, wg, wu, wd, g_final)


def _layer(x3, l, w, mem_k, mem_v, cache, final_norm, akv_all):
    b, t, d = x3.shape
    x2 = x3.reshape(b * t, d)
    row = lambda a: a.reshape(1, -1)
    lam_init = 0.8 - 0.6 * math.exp(-0.3 * l)

    gates, cqk, cif, q, ak_all, av_all, akv, bkv, cvo = _in_proj(
        x2, row(w['g_mix']), l, w['w_in'], w['w_if'], *akv_all, (b, t))
    q3 = q.reshape(b, t, SEG)
    bkv3 = bkv.reshape(b, t, SEG)
    cqk3 = cqk.reshape(b, t, 2 * D_C)
    keep = t if cache is not None else min(BAND, t)
    b_k = bkv3[:, t - keep:, :HALF].reshape(b, keep, H_B, DH_B)
    b_v = bkv3[:, t - keep:, HALF:].reshape(b, keep, H_B, DH_B)

    gate_b = jnp.concatenate([w['b_i'], w['b_f'], jnp.zeros((LANES - 2 * H_C,), F32)]).reshape(1, LANES)
    if cache is None:
        oa = _diff_prompt(q3, akv.reshape(b, t, SEG), w['lqk'], row(w['a_head_g']), lam_init)
        ob = _band_prompt(q3, bkv3, w['relbias'])
        conv0 = jnp.zeros((b, CONV_PAD, 2 * D_C), F32)
        c0 = jnp.zeros((b, H_C, DH_C, DH_C), F32)
        n0 = jnp.zeros((b, H_C, DH_C), F32)
        m0 = jnp.zeros((b, H_C, LANES), F32)
        L = min(t, MLSTM_L)
    else:
        past = cache['a_k'].shape[2]
        ak3 = ak_all.reshape(DEPTH * b, t, HALF)
        av3 = av_all.reshape(DEPTH * b, t, HALF)
        oa = _diff_sample(q3, ak3, av3, l, cache['a_k'], cache['a_v'],
                          w['lqk'], row(w['a_head_g']), lam_init)
        nband = cache['b_k'].shape[1]
        ob = _band_sample(q3, bkv3, cache['b_k'].reshape(b, nband, -1), cache['b_v'].reshape(b, nband, -1),
                          w['relbias'], past)
        conv0 = jnp.pad(cache['conv'], ((0, 0), (CONV_PAD - (CONV_W - 1), 0), (0, 0)))
        c0, n0 = cache['C'], cache['n']
        m0 = jnp.broadcast_to(cache['m'][:, :, None], (b, H_C, LANES))
        L = t
    oc, c_new, n_new, m_new = _mlstm(cqk3, cvo.reshape(b, t, SEG), cif.reshape(b, t, LANES), conv0,
                                     w['conv_w'], row(w['conv_b']), gate_b, row(w['c_head_g']), c0, n0, m0, L)
    assert t >= CONV_W - 1
    conv_new = cqk3[:, t - (CONV_W - 1):]

    x2 = _mix(oa.reshape(b * t, -1), ob.reshape(b * t, -1), oc.reshape(b * t, -1), gates, x2, l,
              w['w_up_a'], w['w_up_b'], w['w_up_c'], w['w_o'])
    x3 = _cross(x2.reshape(b, t, d), row(w['g_cross']), l, w['w_mq'], mem_k, mem_v, w['w_mo'])
    x2 = _ffn(x3.reshape(b * t, d), row(w['g_ffn']), l, w['w_ff_g'], w['w_ff_u'], w['w_ff_d'],
              row(w['g_final']), final_norm)
    return x2.reshape(b, t, d), (ak_all, av_all), (b_k, b_v, c_new, n_new, m_new[:, :, 0], conv_new)


def kernel(x_prompt, x_sample, cache_a_k, cache_a_v, cache_b_k, cache_b_v, state_c_C, state_c_n, state_c_m, state_c_conv, cache_mem_k, cache_mem_v, mem_prompt, g_mix, w_in, a_lq1, a_lk1, a_lq2, a_lk2, a_head_g, b_rel, c_conv_w, c_conv_b, c_b_i, c_b_f, c_head_g, w_up_a, w_up_b, w_up_c, w_o, g_cross, w_mq, w_mk, w_mv, w_mo, g_ffn, w_ff_g, w_ff_u, w_ff_d, g_final):
    xp, xs = x_prompt, x_sample
    bp = x_prompt.shape[0]
    new_p = [[] for _ in range(6)]
    new_s = [[] for _ in range(6)]
    akv_p = akv_s = (None, None)
    bf = lambda a: a.astype(BF16)
    wp, wif = _prep_w_in(w_in)
    wts = dict(w_up_a=bf(w_up_a), w_up_b=bf(w_up_b), w_up_c=bf(w_up_c), w_o=bf(w_o), w_mq=bf(w_mq),
               w_mo=bf(w_mo), w_ff_g=bf(w_ff_g), w_ff_u=bf(w_ff_u), w_ff_d=bf(w_ff_d))
    mem_k = _mem_proj(mem_prompt, bf(w_mk))
    mem_v = _mem_proj(mem_prompt, bf(w_mv))
    for l in range(DEPTH):
        w = dict(g_mix=g_mix[l], w_in=wp, w_if=wif,
                 lqk=jnp.stack([a_lq1[l], a_lk1[l], a_lq2[l], a_lk2[l]]),
                 a_head_g=a_head_g[l], relbias=_relbias(b_rel[l]), conv_w=c_conv_w[l], conv_b=c_conv_b[l],
                 b_i=c_b_i[l], b_f=c_b_f[l], c_head_g=c_head_g[l],
                 g_cross=g_cross[l], g_ffn=g_ffn[l], g_final=g_final, **wts)
        last = l == DEPTH - 1
        xp, akv_p, st_p = _layer(xp, l, w, mem_k, mem_v, None, last, akv_p)
        cache = dict(a_k=cache_a_k, a_v=cache_a_v, b_k=cache_b_k[l], b_v=cache_b_v[l],
                     C=state_c_C[l], n=state_c_n[l], m=state_c_m[l], conv=state_c_conv[l])
        xs, akv_s, st_s = _layer(xs, l, w, cache_mem_k, cache_mem_v, cache, last, akv_s)
        for lst, a in zip(new_p, st_p):
            lst.append(a)
        for lst, a in zip(new_s, st_s):
            lst.append(a)
    heads = lambda a, b, t: a.reshape(DEPTH, b, t, H_A, DV_A)
    outs_p = ([heads(a, bp, x_prompt.shape[1]) for a in akv_p] + [jnp.stack(a, 0) for a in new_p]
              + [mem_k, mem_v])
    outs_s = [heads(a, xs.shape[0], xs.shape[1]) for a in akv_s] + [jnp.stack(a, 0) for a in new_s]
    return (xp, xs) + tuple(outs_p) + tuple(outs_s)
```

```python
import functools
import math

import jax
import jax.numpy as jnp
from jax import lax
from jax.experimental import pallas as pl
from jax.experimental.pallas import tpu as pltpu

F32 = jnp.float32
BF16 = jnp.bfloat16

D_MODEL = 1024
DEPTH = 2
CHUNK = 64
EPS = 1e-6
NEG = -1e30
LOG2E = math.log2(math.e)
H_A = 4
DH_A = 64
DV_A = 2 * DH_A
H_B = 8
DH_B = 64
BAND_CHUNKS = 8
BAND = BAND_CHUNKS * CHUNK
REL_CLIP = 128
H_C = 4
DH_C = 128
D_C = H_C * DH_C
CONV_W = 4
H_M = 4
DH_M = D_MODEL // H_M
D_FF = -(-8 * D_MODEL // (3 * 256)) * 256

LANES = 128
VMEM_LIMIT = 48 * 1024 * 1024

SEG = 1024
HALF = SEG // 2
GATE_TILES = 3
N_TILES = 8
IN_SIZES = (H_A * DH_A,) * 4 + (H_A * DV_A,) + (H_B * DH_B,) * 3 + (2 * D_C, D_C, D_C, 2 * H_C, 3 * D_MODEL)

ALIBI_SLOPES = tuple(2.0 ** (-8.0 * (i + 1) / H_A) for i in range(H_A))

NT_DIMS = (((1,), (1,)), ((), ()))


def _cparams(sem):
    return pltpu.CompilerParams(dimension_semantics=sem, vmem_limit_bytes=VMEM_LIMIT)


def _rms(x, g):
    ms = jnp.mean(x * x, axis=-1, keepdims=True)
    return x * lax.rsqrt(ms + EPS) * g


def _nt(a, b):
    return lax.dot_general(a, b, NT_DIMS, preferred_element_type=F32)


def _prep_w_in_kernel(w_ref, o_ref, oif_ref):
    offs = [0]
    for size in IN_SIZES:
        offs.append(offs[-1] + size)
    seg = lambda i: w_ref[:, offs[i]:offs[i + 1]]
    a_q1, a_q2, a_k1, a_k2, a_v, b_q, b_k, b_v, c_qk, c_v, c_o = (seg(i) for i in range(11))
    g = seg(12)

    def put(col, val):
        o_ref[:, col:col + val.shape[1]] = val.astype(BF16)

    put(0, g)
    put(GATE_TILES * SEG, c_qk)
    for h in range(H_A):
        hs = slice(h * DH_A, (h + 1) * DH_A)
        base = h * 2 * DH_A
        put(4 * SEG + base, a_q1[:, hs] * (DH_A ** -0.5))
        put(4 * SEG + base + DH_A, a_q2[:, hs] * (DH_A ** -0.5))
        put(5 * SEG + base, a_k1[:, hs])
        put(5 * SEG + base + DH_A, a_k2[:, hs])
    put(4 * SEG + HALF, b_q * (DH_B ** -0.5))
    put(5 * SEG + HALF, a_v)
    put(6 * SEG, b_k)
    put(6 * SEG + HALF, b_v)
    put(7 * SEG, c_v)
    put(7 * SEG + HALF, c_o)
    tile = w_ref[:, offs[11]:offs[11] + LANES]
    lane = lax.broadcasted_iota(jnp.int32, tile.shape, 1)
    oif_ref[...] = jnp.where(lane < 2 * H_C, tile, 0.0).astype(BF16)


def _prep_w_in(w):
    depth, k, n = w.shape
    tk = 256
    n_out = N_TILES * SEG
    return pl.pallas_call(
        _prep_w_in_kernel,
        out_shape=(jax.ShapeDtypeStruct((depth, k, n_out), BF16),
                   jax.ShapeDtypeStruct((depth, k, LANES), BF16)),
        grid=(depth, k // tk),
        in_specs=[pl.BlockSpec((None, tk, n), lambda l, i: (l, i, 0))],
        out_specs=(pl.BlockSpec((None, tk, n_out), lambda l, i: (l, i, 0)),
                   pl.BlockSpec((None, tk, LANES), lambda l, i: (l, i, 0))),
        compiler_params=_cparams(("parallel", "parallel")),
        name="prep_w_in",
    )(w)


IN_PROJ_TM = 256


def _in_proj_kernel(*refs, aliased, heads5d):
    x_ref, g_ref, w_ref, wif_ref = refs[:4]
    og_ref, ocqk_ref, ocif_ref, oq_ref, oak_ref, oav_ref, oakv_ref, obkv_ref, ocvo_ref = refs[6 if aliased else 4:]
    h = _rms(x_ref[...], g_ref[...]).astype(BF16)
    cols = lambda j, lo=0, hi=SEG: jnp.dot(h, w_ref[:, j * SEG + lo:j * SEG + hi], preferred_element_type=F32)

    ocif_ref[...] = jnp.dot(h, wif_ref[...], preferred_element_type=F32)
    for j in range(GATE_TILES):
        og_ref[:, j * SEG:(j + 1) * SEG] = jax.nn.sigmoid(cols(j)).astype(og_ref.dtype)
    ocqk_ref[...] = cols(GATE_TILES)
    oq_ref[...] = (cols(4) * LOG2E).astype(oq_ref.dtype)
    for out_ref, off in ((oak_ref, 0), (oav_ref, HALF)):
        a = cols(5, off, off + HALF)
        oakv_ref[:, off:off + HALF] = a.astype(oakv_ref.dtype)
        if heads5d:
            out_ref[0, 0] = a.reshape(a.shape[0], H_A, DV_A)
        else:
            out_ref[0] = a
    obkv_ref[...] = cols(6)
    ocvo_ref[...] = cols(7)


def _in_proj(x, g, l, wp, wif, ak_all, av_all, bt):
    m, k = x.shape
    tm = min(m, IN_PROJ_TM)
    aliased = ak_all is not None
    b, t = bt
    heads5d = t % tm == 0
    if heads5d:
        stacked = jax.ShapeDtypeStruct((DEPTH, b, t, H_A, DV_A), F32)
        layer_row = pl.BlockSpec((1, 1, tm, H_A, DV_A), lambda i: (l, i // (t // tm), i % (t // tm), 0, 0))
    else:
        stacked = jax.ShapeDtypeStruct((DEPTH, m, HALF), F32)
        layer_row = pl.BlockSpec((1, tm, HALF), lambda i: (l, i, 0))
    resident = lambda a: pl.BlockSpec((None,) + a.shape[1:], lambda i: (l, 0, 0), pipeline_mode=pl.Buffered(1))
    in_specs = [pl.BlockSpec((tm, k), lambda i: (i, 0)), pl.BlockSpec((1, k), lambda i: (0, 0)),
                resident(wp), resident(wif)]
    args = [x, g, wp, wif]
    if aliased:
        in_specs += [pl.BlockSpec(memory_space=pl.ANY)] * 2
        args += [ak_all, av_all]
    row = lambda w: pl.BlockSpec((tm, w), lambda i: (i, 0))
    sds = jax.ShapeDtypeStruct
    return pl.pallas_call(
        functools.partial(_in_proj_kernel, aliased=aliased, heads5d=heads5d),
        out_shape=(sds((m, GATE_TILES * SEG), BF16), sds((m, SEG), F32), sds((m, LANES), F32),
                   sds((m, SEG), BF16), stacked, stacked, sds((m, SEG), BF16),
                   sds((m, SEG), F32), sds((m, SEG), F32)),
        grid=(m // tm,),
        in_specs=in_specs,
        out_specs=(row(GATE_TILES * SEG), row(SEG), row(LANES), row(SEG), layer_row, layer_row, row(SEG),
                   row(SEG), row(SEG)),
        input_output_aliases={4: 4, 5: 5} if aliased else {},
        compiler_params=_cparams(("parallel",)),
        name="in_proj",
    )(*args)


def _mem_proj_kernel(x_ref, w_ref, o_ref):
    y = jnp.dot(x_ref[0].astype(BF16), w_ref[...], preferred_element_type=F32)
    o_ref[...] = y.reshape(o_ref.shape)


def _mem_proj(mem, w):
    b, nm, k = mem.shape
    depth, _, n = w.shape
    return pl.pallas_call(
        _mem_proj_kernel,
        out_shape=jax.ShapeDtypeStruct((depth, b, nm, H_M, DH_M), F32),
        grid=(depth, b),
        in_specs=[pl.BlockSpec((1, nm, k), lambda l, bi: (bi, 0, 0)),
                  pl.BlockSpec((None, k, n), lambda l, bi: (l, 0, 0))],
        out_specs=pl.BlockSpec((None, None, nm, H_M, DH_M), lambda l, bi: (l, bi, 0, 0, 0)),
        compiler_params=_cparams(("parallel", "parallel")),
        name="mem_proj",
    )(mem, w)


def _online_updates(jobs):
    scores, staged = {}, {}

    def softmax(i):
        s, (acc, m, l) = scores.pop(i), jobs[i][2]
        slabs = [s[:, j * LANES:(j + 1) * LANES] for j in range(s.shape[1] // LANES)]
        m_old = m[...]
        m_new = jnp.maximum(m_old, jnp.max(functools.reduce(jnp.maximum, slabs), axis=-1, keepdims=True))
        alpha = jnp.exp2(m_old - m_new)
        ps = [jnp.exp2(c - m_new) for c in slabs]
        l[...] = alpha * l[...] + jnp.sum(functools.reduce(jnp.add, ps), axis=-1, keepdims=True)
        m[...] = m_new
        staged[i] = (alpha, jnp.concatenate(ps, axis=1).astype(BF16))

    def pv(i):
        (alpha, p), (_, v, (acc, m, l)) = staged.pop(i), jobs[i]
        acc[...] = alpha * acc[...] + jnp.dot(p, v, preferred_element_type=F32)

    n = len(jobs)
    for step in range(n + 2):
        if step < n:
            scores[step] = jobs[step][0]()
        if 0 <= step - 1 < n:
            softmax(step - 1)
        if 0 <= step - 2 < n:
            pv(step - 2)


def _split_halves(q):
    lane = lax.broadcasted_iota(jnp.int32, q.shape, 1)
    lo = jnp.where(lane < DH_A, q, 0.0).astype(BF16)
    hi = jnp.where(lane >= DH_A, q, 0.0).astype(BF16)
    return lo, hi


def _lambda(lqk, lam_init):
    e1 = jnp.exp(jnp.sum(lqk[0:1, :] * lqk[1:2, :], axis=-1, keepdims=True))
    e2 = jnp.exp(jnp.sum(lqk[2:3, :] * lqk[3:4, :], axis=-1, keepdims=True))
    return e1 - e2 + lam_init


DIFF_HEADS = 2


def _diff_prompt_kernel(slopes_ref, lqk_ref, hg_ref, q_ref, k_ref, v_ref, o_ref,
                        diag_bias, acc, m, l, *, tq, nh, lam_init):
    hg = pl.program_id(1)
    qi = pl.program_id(2)
    slopes = [slopes_ref[hg * nh + i] * LOG2E for i in range(nh)]
    q0 = pl.multiple_of(qi * tq, tq)
    head_cols = [slice(i * LANES, (i + 1) * LANES) for i in range(nh)]

    @pl.when(qi == 0)
    def _():
        r = lax.broadcasted_iota(jnp.int32, (tq, tq), 0)
        c = lax.broadcasted_iota(jnp.int32, (tq, tq), 1)
        rel = (r - jnp.abs(r - c)).astype(F32)
        allowed = (c // CHUNK) <= (r // CHUNK)
        for i in range(nh):
            diag_bias[i] = jnp.where(allowed, slopes[i] * rel, NEG)

    m[...] = jnp.full(m.shape, NEG, F32)
    l[...] = jnp.zeros(l.shape, F32)
    acc[...] = jnp.zeros(acc.shape, F32)
    chains = []
    for i in range(nh):
        for j, qm in enumerate(_split_halves(q_ref[0, :, head_cols[i]])):
            c = 2 * i + j
            chains.append((qm, i, (acc.at[c], m.at[c], l.at[c])))

    score = lambda qm, k, bias: _nt(qm, k) + bias

    def full_tiles(k_starts):
        jobs = []
        for k0 in k_starts:
            kpos = (k0 - q0) + lax.broadcasted_iota(jnp.int32, (1, tq), 1)
            kposf = kpos.astype(F32)
            for qm, i, st in chains:
                k = k_ref[0, pl.ds(k0, tq), head_cols[i]]
                v = v_ref[0, pl.ds(k0, tq), head_cols[i]]
                jobs.append((functools.partial(score, qm, k, slopes[i] * kposf), v, st))
        _online_updates(jobs)

    def tile_pair(it, carry):
        k0 = pl.multiple_of(it * (2 * tq), 2 * tq)
        full_tiles([k0, pl.multiple_of(k0 + tq, tq)])
        return carry

    lax.fori_loop(0, qi // 2, tile_pair, 0)

    @pl.when(qi % 2 == 1)
    def _():
        full_tiles([pl.multiple_of((qi - 1) * tq, tq)])

    half = tq // 2
    jobs = []
    for rows, nk in ((slice(0, half), half), (slice(half, tq), tq)):
        for qm, i, st in chains:
            k = k_ref[0, pl.ds(q0, nk), head_cols[i]]
            v = v_ref[0, pl.ds(q0, nk), head_cols[i]]
            jobs.append((functools.partial(score, qm[rows], k, diag_bias[i, rows, 0:nk]), v,
                         tuple(ref.at[rows] for ref in st)))
    _online_updates(jobs)

    lam = _lambda(lqk_ref[...], lam_init)
    for i in range(nh):
        o = acc[2 * i] / l[2 * i] - lam * (acc[2 * i + 1] / l[2 * i + 1])
        o_ref[0, :, head_cols[i]] = (_rms(o, hg_ref[...]) * (1.0 - lam_init)).astype(o_ref.dtype)


def _diff_prompt(q3, kv3, lqk, head_g, lam_init):
    b, t, _ = q3.shape
    tq = min(t, 512)
    nh = DIFF_HEADS
    w = nh * LANES
    slopes = jnp.asarray(ALIBI_SLOPES, F32)
    return pl.pallas_call(
        functools.partial(_diff_prompt_kernel, tq=tq, nh=nh, lam_init=lam_init),
        out_shape=jax.ShapeDtypeStruct((b, t, H_A * DV_A), BF16),
        grid=(b, H_A // nh, t // tq),
        in_specs=[
            pl.BlockSpec(memory_space=pltpu.SMEM),
            pl.BlockSpec((4, DH_A), lambda bi, h, qi: (0, 0)),
            pl.BlockSpec((1, DV_A), lambda bi, h, qi: (0, 0)),
            pl.BlockSpec((1, tq, w), lambda bi, h, qi: (bi, qi, h)),
            pl.BlockSpec((1, t, w), lambda bi, h, qi: (bi, 0, h)),
            pl.BlockSpec((1, t, w), lambda bi, h, qi: (bi, 0, H_A // nh + h)),
        ],
        out_specs=pl.BlockSpec((1, tq, w), lambda bi, h, qi: (bi, qi, h)),
        scratch_shapes=[
            pltpu.VMEM((nh, tq, tq), F32),
            pltpu.VMEM((2 * nh, tq, LANES), F32), pltpu.VMEM((2 * nh, tq, LANES), F32),
            pltpu.VMEM((2 * nh, tq, LANES), F32),
        ],
        compiler_params=_cparams(("parallel", "parallel", "arbitrary")),
        name="diff_prompt",
    )(slopes, lqk, head_g, q3, kv3, kv3)


def _pieces_attention(qm, pieces):
    ss = [_nt(qm, k) + bias for k, _, bias in pieces]
    m = functools.reduce(jnp.maximum, [jnp.max(s, axis=-1, keepdims=True) for s in ss])
    es = [jnp.exp2(s - m) for s in ss]
    l = functools.reduce(jnp.add, [jnp.sum(e, axis=-1, keepdims=True) for e in es])
    o = functools.reduce(jnp.add, [jnp.dot(e.astype(BF16), v, preferred_element_type=F32)
                                   for e, (_, v, _) in zip(es, pieces)])
    return o / l


def _diff_sample_kernel(lqk_ref, hg_ref, q_ref, kc_ref, vc_ref, kn_ref, vn_ref, o_ref, *, lam_init):
    t = q_ref.shape[1]
    past = kc_ref.shape[1]
    lam = _lambda(lqk_ref[...], lam_init)
    rc = lax.broadcasted_iota(jnp.int32, (t, past), 0)
    cc = lax.broadcasted_iota(jnp.int32, (t, past), 1)
    rn = lax.broadcasted_iota(jnp.int32, (t, t), 0)
    cn = lax.broadcasted_iota(jnp.int32, (t, t), 1)
    dist_c = jnp.abs(past + rc - cc).astype(F32)
    dist_n = jnp.abs(rn - cn).astype(F32)
    ok_c = (cc // CHUNK) <= ((past + rc) // CHUNK)
    ok_n = ((past + cn) // CHUNK) <= ((past + rn) // CHUNK)
    kc = kc_ref[0].reshape(past, H_A * DV_A).astype(BF16)
    vc = vc_ref[0].reshape(past, H_A * DV_A).astype(BF16)
    for h in range(H_A):
        cols = slice(h * LANES, (h + 1) * LANES)
        bias_c = jnp.where(ok_c, (-ALIBI_SLOPES[h] * LOG2E) * dist_c, NEG)
        bias_n = jnp.where(ok_n, (-ALIBI_SLOPES[h] * LOG2E) * dist_n, NEG)
        pieces = [(kc[:, cols], vc[:, cols], bias_c),
                  (kn_ref[0, :, cols].astype(BF16), vn_ref[0, :, cols].astype(BF16), bias_n)]
        qa, qb = _split_halves(q_ref[0, :, cols])
        o = _pieces_attention(qa, pieces) - lam * _pieces_attention(qb, pieces)
        o_ref[0, :, cols] = (_rms(o, hg_ref[...]) * (1.0 - lam_init)).astype(o_ref.dtype)


def _diff_sample(q3, kn3, vn3, l, kc, vc, lqk, head_g, lam_init):
    b, t, _ = q3.shape
    past = kc.shape[2]
    w = H_A * DV_A
    new = pl.BlockSpec((1, t, w), lambda bi: (l * b + bi, 0, 0))
    cache = pl.BlockSpec((None, 1, past, H_A, DV_A), lambda bi: (l, bi, 0, 0, 0))
    return pl.pallas_call(
        functools.partial(_diff_sample_kernel, lam_init=lam_init),
        out_shape=jax.ShapeDtypeStruct((b, t, w), BF16),
        grid=(b,),
        in_specs=[
            pl.BlockSpec((4, DH_A), lambda bi: (0, 0)),
            pl.BlockSpec((1, DV_A), lambda bi: (0, 0)),
            pl.BlockSpec((1, t, w), lambda bi: (bi, 0, 0)), cache, cache, new, new,
        ],
        out_specs=pl.BlockSpec((1, t, w), lambda bi: (bi, 0, 0)),
        compiler_params=_cparams(("parallel",)),
        name="diff_sample",
    )(lqk, head_g, q3, kc, vc, kn3, vn3)


QPAIR = 2 * CHUNK
WIN = BAND + QPAIR


def _relbias_kernel(tab_ref, o_ref):
    h = pl.program_id(0)
    r = lax.broadcasted_iota(jnp.int32, (QPAIR, WIN), 0)
    j = lax.broadcasted_iota(jnp.int32, (QPAIR, WIN), 1)
    rel = jnp.clip(r - j + BAND, -REL_CLIP, REL_CLIP) + REL_CLIP
    qc = r // CHUNK
    kc = j // CHUNK - BAND_CHUNKS
    allowed = (kc <= qc) & (kc >= qc - BAND_CHUNKS)

    far = BAND - REL_CLIP
    near_rel = rel[:, far:]

    near = jnp.zeros((QPAIR, WIN - far), F32)
    for t in range(2 * REL_CLIP + 1):
        near = jnp.where(near_rel == t, tab_ref[h, t], near)
    bias = jnp.concatenate([jnp.full((QPAIR, far), tab_ref[h, 2 * REL_CLIP], F32), near], axis=1)
    o_ref[0] = jnp.where(allowed, bias * LOG2E, NEG)


def _relbias(table):
    return pl.pallas_call(
        _relbias_kernel,
        out_shape=jax.ShapeDtypeStruct((H_B, QPAIR, WIN), F32),
        grid=(H_B,),
        in_specs=[pl.BlockSpec(memory_space=pltpu.SMEM)],
        out_specs=pl.BlockSpec((1, QPAIR, WIN), lambda h: (h, 0, 0)),
        compiler_params=_cparams(("arbitrary",)),
        name="relbias",
    )(table)


BAND_TQ = 512
BAND_UNROLL = 2


def _band_prompt_kernel(bias_ref, q_ref, kp_ref, kc_ref, vp_ref, vc_ref, o_ref, kcat, vcat):
    i = pl.program_id(1)
    kcat[0:BAND_TQ, :] = kp_ref[0].astype(BF16)
    kcat[BAND_TQ:, :] = kc_ref[0].astype(BF16)
    vcat[0:BAND_TQ, :] = vp_ref[0].astype(BF16)
    vcat[BAND_TQ:, :] = vc_ref[0].astype(BF16)
    lane = lax.broadcasted_iota(jnp.int32, (QPAIR, LANES), 1)

    def pair_blocks(it, carry, *, first_block):
        starts = [pl.multiple_of((it * BAND_UNROLL + u) * QPAIR, QPAIR) for u in range(BAND_UNROLL)]
        units = [(r0, hp) for r0 in starts for hp in range(H_B // 2)]
        queries = {u: q_ref[0, pl.ds(r0, QPAIR), :] for u, r0 in enumerate(starts)}
        scores, probs, outs = {}, {}, {}

        def qk(i):
            r0, hp = units[i]
            cols = slice(hp * LANES, (hp + 1) * LANES)
            qs = jnp.concatenate(_split_halves(queries[i // (H_B // 2)][:, cols]), axis=0)
            s = _nt(qs, kcat[pl.ds(r0, WIN), cols]) + bias_ref[hp]
            if first_block:
                kpos = r0 - BAND_TQ + lax.broadcasted_iota(jnp.int32, (1, WIN), 1)
                s = s + jnp.where(kpos >= 0, 0.0, NEG)
            scores[i] = s

        def softmax(i):
            s = scores.pop(i)
            e = jnp.exp2(s - jnp.max(s, axis=-1, keepdims=True))
            probs[i] = (e.astype(BF16), jnp.sum(e, axis=-1, keepdims=True))

        def pv(i):
            r0, hp = units[i]
            e, l = probs.pop(i)
            o = jnp.dot(e, vcat[pl.ds(r0, WIN), hp * LANES:(hp + 1) * LANES], preferred_element_type=F32) / l
            outs[i] = jnp.where(lane < DH_B, o[:QPAIR], o[QPAIR:])

        n = len(units)
        for step in range(n + 2):
            if step < n:
                qk(step)
            if 0 <= step - 1 < n:
                softmax(step - 1)
            if 0 <= step - 2 < n:
                pv(step - 2)
        for u, r0 in enumerate(starts):
            row = [outs[u * (H_B // 2) + hp] for hp in range(H_B // 2)]
            o_ref[0, pl.ds(r0, QPAIR), :] = jnp.concatenate(row, axis=1).astype(o_ref.dtype)
        return carry

    trips = BAND_TQ // QPAIR // BAND_UNROLL

    @pl.when(i == 0)
    def _():
        lax.fori_loop(0, trips, functools.partial(pair_blocks, first_block=True), 0)

    @pl.when(i > 0)
    def _():
        lax.fori_loop(0, trips, functools.partial(pair_blocks, first_block=False), 0)


def _band_prompt(q3, kv3, bias):
    b, t, _ = q3.shape
    assert t % BAND_TQ == 0
    w = H_B * DH_B
    cq, ck, cv = 1, 0, 1
    prev = lambda col: pl.BlockSpec((1, BAND_TQ, w), lambda bi, i: (bi, jnp.maximum(i - 1, 0), col))
    cur = lambda col: pl.BlockSpec((1, BAND_TQ, w), lambda bi, i: (bi, i, col))
    return pl.pallas_call(
        _band_prompt_kernel,
        out_shape=jax.ShapeDtypeStruct((b, t, w), BF16),
        grid=(b, t // BAND_TQ),
        in_specs=[
            pl.BlockSpec((H_B // 2, 2 * QPAIR, WIN), lambda bi, i: (0, 0, 0)),
            cur(cq), prev(ck), cur(ck), prev(cv), cur(cv),
        ],
        out_specs=pl.BlockSpec((1, BAND_TQ, w), lambda bi, i: (bi, i, 0)),
        scratch_shapes=[pltpu.VMEM((2 * BAND_TQ, w), BF16), pltpu.VMEM((2 * BAND_TQ, w), BF16)],
        compiler_params=_cparams(("parallel", "arbitrary")),
        name="band_prompt",
    )(bias.reshape(H_B // 2, 2 * QPAIR, WIN), q3, kv3, kv3, kv3, kv3)


def _band_sample_kernel(bc_ref, bn_ref, q_ref, kc_ref, vc_ref, kn_ref, vn_ref, o_ref, *, past):
    t = q_ref.shape[1]
    nband = kc_ref.shape[1]
    lane = lax.broadcasted_iota(jnp.int32, (t, LANES), 1)
    qpos_c = past + lax.broadcasted_iota(jnp.int32, (t, nband), 0)
    kpos_c = past - nband + lax.broadcasted_iota(jnp.int32, (t, nband), 1)
    qpos_n = past + lax.broadcasted_iota(jnp.int32, (t, t), 0)
    kpos_n = past + lax.broadcasted_iota(jnp.int32, (t, t), 1)

    def allowed(qpos, kpos):
        qc, kc = qpos // CHUNK, kpos // CHUNK
        return (kpos >= 0) & (kc <= qc) & (kc >= qc - BAND_CHUNKS)

    ok_c = allowed(qpos_c, kpos_c)
    ok_n = allowed(qpos_n, kpos_n)
    for hp in range(H_B // 2):
        cols = slice(hp * LANES, (hp + 1) * LANES)
        halves = []
        for sub, qm in enumerate(_split_halves(q_ref[0, :, cols])):
            hd = 2 * hp + sub
            pieces = [(kc_ref[0, :, cols].astype(BF16), vc_ref[0, :, cols].astype(BF16),
                       jnp.where(ok_c, bc_ref[hd], NEG)),
                      (kn_ref[0, :, cols].astype(BF16), vn_ref[0, :, cols].astype(BF16),
                       jnp.where(ok_n, bn_ref[hd], NEG))]
            halves.append(_pieces_attention(qm, pieces))
        o_ref[0, :, cols] = jnp.where(lane < DH_B, halves[0], halves[1]).astype(o_ref.dtype)


def _band_sample(q3, kv3, kc, vc, bias, past):
    b, t, _ = q3.shape
    nband = kc.shape[1]
    assert nband == BAND and t <= CHUNK
    w = H_B * DH_B
    bias_c = bias[:, :t, :nband]
    bias_n = bias[:, :t, nband:nband + t]
    blk = lambda col: pl.BlockSpec((1, t, w), lambda bi: (bi, 0, col))
    cache = pl.BlockSpec((1, nband, w), lambda bi: (bi, 0, 0))
    return pl.pallas_call(
        functools.partial(_band_sample_kernel, past=past),
        out_shape=jax.ShapeDtypeStruct((b, t, w), BF16),
        grid=(b,),
        in_specs=[
            pl.BlockSpec((H_B, t, nband), lambda bi: (0, 0, 0)),
            pl.BlockSpec((H_B, t, t), lambda bi: (0, 0, 0)),
            blk(1), cache, cache, blk(0), blk(1),
        ],
        out_specs=pl.BlockSpec((1, t, w), lambda bi: (bi, 0, 0)),
        compiler_params=_cparams(("parallel",)),
        name="band_sample",
    )(bias_c, bias_n, q3, kc, vc, kv3, kv3)


CONV_PAD = 8
MLSTM_L = 256


def _mlstm_kernel(cqk_ref, cv_ref, co_ref, cif_ref, conv0_ref, cw_ref, cb_ref, gb_ref, hg_ref,
                  c0_ref, n0_ref, m0_ref, hs_ref, c_ref, n_ref, m_ref, xext, *, L):
    @pl.when(pl.program_id(1) == 0)
    def _():
        xext[0:CONV_PAD, :] = conv0_ref[0]
        c_ref[...] = c0_ref[...]
        n_ref[...] = n0_ref[...]
        m_ref[...] = m0_ref[...]

    xext[CONV_PAD:CONV_PAD + L, :] = cqk_ref[0]
    base = CONV_PAD - (CONV_W - 1)
    u = 0.0
    for j in range(CONV_W):
        u = u + xext[base + j:base + j + L, :] * cw_ref[j:j + 1, :]
    u = cb_ref[...] + u
    u = u * jax.nn.sigmoid(u)
    xext[0:CONV_PAD, :] = xext[L:L + CONV_PAD, :]

    z = cif_ref[0] + gb_ref[...]
    lane = lax.broadcasted_iota(jnp.int32, (L, LANES), 1)
    lf = jnp.minimum(z, 0.0) - jnp.log1p(jnp.exp(-jnp.abs(z)))
    tr = lax.broadcasted_iota(jnp.int32, (L, L), 0)
    tc = lax.broadcasted_iota(jnp.int32, (L, L), 1)
    causal = tr >= tc
    b_all = jnp.dot(causal.astype(F32), lf, preferred_element_type=F32,
                    precision=lax.Precision.HIGHEST)
    sr = lax.broadcasted_iota(jnp.int32, (8, LANES), 0)
    sc = lax.broadcasted_iota(jnp.int32, (8, LANES), 1)
    sel = jnp.where(sr < H_C, jnp.where(sc == sr, 1.0, jnp.where(sc == sr + H_C, -1.0, 0.0)), 0.0)
    rt = lax.dot_general(sel, jnp.where(lane < H_C, z, b_all), NT_DIMS,
                         preferred_element_type=F32, precision=lax.Precision.HIGHEST)
    er = lax.broadcasted_iota(jnp.int32, (DH_C, DH_C), 0)
    ec = lax.broadcasted_iota(jnp.int32, (DH_C, DH_C), 1)
    eye = jnp.where(er == ec, 1.0, 0.0).astype(BF16)

    for h in range(H_C):
        cols = slice(h * DH_C, (h + 1) * DH_C)
        q = u[:, h * DH_C:(h + 1) * DH_C]
        k = u[:, D_C + h * DH_C:D_C + (h + 1) * DH_C] * (DH_C ** -0.5)
        v = cv_ref[0, :, cols]
        qb, kb_, vb_ = q.astype(BF16), k.astype(BF16), v.astype(BF16)
        b_col = b_all[:, H_C + h:H_C + h + 1]
        li_col = z[:, h:h + 1]
        m_prev = m_ref[0, h:h + 1, 0:1]
        cs = c_ref[0, h]
        ns = n_ref[0, h:h + 1, :]

        dmat = jnp.where(causal, b_col + rt[h:h + 1, :], NEG)
        inter = b_col + m_prev
        m_t = jnp.maximum(inter, jnp.max(dmat, axis=-1, keepdims=True))
        w_intra = jnp.exp(dmat - m_t)
        w_inter = jnp.exp(inter - m_t)
        a = w_intra * _nt(qb, kb_)
        num = (jnp.dot(a.astype(BF16), vb_, preferred_element_type=F32)
               + w_inter * _nt(qb, cs.astype(BF16)))
        den = jnp.sum(a, axis=-1, keepdims=True) + w_inter * jnp.sum(q * ns, axis=-1, keepdims=True)
        hh = num / jnp.maximum(jnp.abs(den), jnp.exp(-m_t))

        b_last = b_col[L - 1:L, :]
        g = b_last - b_col + li_col
        m_new = jnp.maximum(b_last + m_prev, jnp.max(g, axis=0, keepdims=True))
        ws = jnp.exp(g - m_new)
        decay = jnp.exp(b_last + m_prev - m_new)
        vwt = _nt(eye, (ws * v).astype(BF16)).astype(BF16)
        c_ref[0, h] = decay * cs + jnp.dot(vwt, kb_, preferred_element_type=F32)
        n_ref[0, h:h + 1, :] = decay * ns + jnp.sum(ws * k, axis=0, keepdims=True)
        m_ref[0, h:h + 1, :] = jnp.broadcast_to(m_new, (1, LANES))

        gate = jax.nn.sigmoid(co_ref[0, :, cols])
        hs_ref[0, :, cols] = (_rms(hh, hg_ref[...]) * gate).astype(hs_ref.dtype)


def _mlstm(cqk3, cvo3, cif3, conv0, conv_w, conv_b, gate_b, head_g, c0, n0, m0, L):
    b, t, _ = cqk3.shape
    nc = t // L
    blk = lambda col, w: pl.BlockSpec((1, L, w), lambda bi, c: (bi, c, col))
    const = lambda shape: pl.BlockSpec(shape, lambda bi, c: (0,) * len(shape))
    per_b = lambda shape: pl.BlockSpec((1,) + shape, lambda bi, c: (bi,) + (0,) * len(shape))
    return pl.pallas_call(
        functools.partial(_mlstm_kernel, L=L),
        out_shape=(jax.ShapeDtypeStruct((b, t, D_C), BF16),
                   jax.ShapeDtypeStruct((b, H_C, DH_C, DH_C), F32),
                   jax.ShapeDtypeStruct((b, H_C, DH_C), F32),
                   jax.ShapeDtypeStruct((b, H_C, LANES), F32)),
        grid=(b, nc),
        in_specs=[
            blk(0, 2 * D_C), blk(0, D_C), blk(1, D_C), blk(0, LANES),
            per_b((CONV_PAD, 2 * D_C)), const((CONV_W, 2 * D_C)), const((1, 2 * D_C)),
            const((1, LANES)), const((1, DH_C)),
            per_b((H_C, DH_C, DH_C)), per_b((H_C, DH_C)), per_b((H_C, LANES)),
        ],
        out_specs=(pl.BlockSpec((1, L, D_C), lambda bi, c: (bi, c, 0)),
                   per_b((H_C, DH_C, DH_C)), per_b((H_C, DH_C)), per_b((H_C, LANES))),
        scratch_shapes=[pltpu.VMEM((CONV_PAD + L, 2 * D_C), F32)],
        compiler_params=_cparams(("parallel", "arbitrary")),
        name="mlstm",
    )(cqk3, cvo3, cvo3, cif3, conv0, conv_w, conv_b, gate_b, head_g, c0, n0, m0)


def _mix_kernel(oa_ref, ob_ref, oc_ref, ga_ref, gb_ref, gc_ref, x_ref, wa_ref, wb_ref, wc_ref, wo_ref, o_ref):
    mixed = (ga_ref[...] * jnp.dot(oa_ref[...], wa_ref[...], preferred_element_type=F32)
             + gb_ref[...] * jnp.dot(ob_ref[...], wb_ref[...], preferred_element_type=F32)
             + gc_ref[...] * jnp.dot(oc_ref[...], wc_ref[...], preferred_element_type=F32))
    o_ref[...] = x_ref[...] + jnp.dot(mixed.astype(BF16), wo_ref[...], preferred_element_type=F32)


def _mix(oa, ob, oc, gates, x, l, wa, wb, wc, wo):
    m, d = x.shape
    tm = min(m, 512)
    row = lambda w, col=0: pl.BlockSpec((tm, w), lambda i: (i, col))
    full = lambda a: pl.BlockSpec((None,) + a.shape[1:], lambda i: (l, 0, 0))
    g0 = 0
    return pl.pallas_call(
        _mix_kernel,
        out_shape=jax.ShapeDtypeStruct((m, d), F32),
        grid=(m // tm,),
        in_specs=[row(oa.shape[1]), row(ob.shape[1]), row(oc.shape[1]),
                  row(d, g0), row(d, g0 + 1), row(d, g0 + 2), row(d),
                  full(wa), full(wb), full(wc), full(wo)],
        out_specs=row(d),
        compiler_params=_cparams(("parallel",)),
        name="mix",
    )(oa, ob, oc, gates, gates, gates, x, wa, wb, wc, wo)


def _cross_kernel(x_ref, g_ref, wq_ref, mk_ref, mv_ref, wo_ref, o_ref):
    x = x_ref[0]
    h = _rms(x, g_ref[...]).astype(BF16)
    q = (jnp.dot(h, wq_ref[...], preferred_element_type=F32) * (DH_M ** -0.5 * LOG2E)).astype(BF16)
    nm = mk_ref.shape[1]
    mk = mk_ref[0].reshape(nm, H_M * DH_M).astype(BF16)
    mv = mv_ref[0].reshape(nm, H_M * DH_M).astype(BF16)
    head_cols = [slice(hd * DH_M, (hd + 1) * DH_M) for hd in range(H_M)]
    scores, probs, outs = {}, {}, {}
    for step in range(H_M + 2):
        if step < H_M:
            scores[step] = _nt(q[:, head_cols[step]], mk[:, head_cols[step]])
        if 0 <= step - 1 < H_M:
            s = scores.pop(step - 1)
            e = jnp.exp2(s - jnp.max(s, axis=-1, keepdims=True))
            probs[step - 1] = (e.astype(BF16), jnp.sum(e, axis=-1, keepdims=True))
        if 0 <= step - 2 < H_M:
            e, l = probs.pop(step - 2)
            o = jnp.dot(e, mv[:, head_cols[step - 2]], preferred_element_type=F32) / l
            outs[step - 2] = o.astype(BF16)
    o = jnp.concatenate([outs[hd] for hd in range(H_M)], axis=1)
    o_ref[0] = x + jnp.dot(o, wo_ref[...], preferred_element_type=F32)


def _cross(x3, g, l, wq, mk, mv, wo):
    b, t, d = x3.shape
    tm = min(t, 512)
    nm = mk.shape[2]
    full = lambda a: pl.BlockSpec((None,) + a.shape[1:], lambda bi, i: (l, 0, 0))
    mem = pl.BlockSpec((None, 1, nm, H_M, DH_M), lambda bi, i: (l, bi, 0, 0, 0))
    return pl.pallas_call(
        _cross_kernel,
        out_shape=jax.ShapeDtypeStruct((b, t, d), F32),
        grid=(b, t // tm),
        in_specs=[pl.BlockSpec((1, tm, d), lambda bi, i: (bi, i, 0)),
                  pl.BlockSpec(g.shape, lambda bi, i: (0, 0)), full(wq), mem, mem, full(wo)],
        out_specs=pl.BlockSpec((1, tm, d), lambda bi, i: (bi, i, 0)),
        compiler_params=_cparams(("parallel", "parallel")),
        name="cross",
    )(x3, g, wq, mk, mv, wo)


FFN_TF = 1408
FFN_TM = 512


def _ffn_kernel(x_ref, g_ref, wg_ref, wu_ref, wd_ref, gf_ref, o_ref, *, final_norm):
    x = x_ref[...]
    h = _rms(x, g_ref[...]).astype(BF16)
    chunks = [slice(c, c + FFN_TF) for c in range(0, D_FF, FFN_TF)]
    pre = [(jnp.dot(h, wg_ref[:, c], preferred_element_type=F32),
            jnp.dot(h, wu_ref[:, c], preferred_element_type=F32)) for c in chunks]
    y = x
    for c, (gate, up) in zip(chunks, pre):
        a = (gate * jax.nn.sigmoid(gate) * up).astype(BF16)
        y = y + jnp.dot(a, wd_ref[c, :], preferred_element_type=F32)
    if final_norm:
        y = _rms(y, gf_ref[...])
    o_ref[...] = y


def _ffn(x, g, l, wg, wu, wd, g_final, final_norm):
    m, d = x.shape
    tm = min(m, FFN_TM)
    resident = lambda a: pl.BlockSpec((None,) + a.shape[1:], lambda i: (l, 0, 0),
                                      pipeline_mode=pl.Buffered(1))
    return pl.pallas_call(
        functools.partial(_ffn_kernel, final_norm=final_norm),
        out_shape=jax.ShapeDtypeStruct((m, d), F32),
        grid=(m // tm,),
        in_specs=[
            pl.BlockSpec((tm, d), lambda i: (i, 0)),
            pl.BlockSpec((1, d), lambda i: (0, 0)),
            resident(wg), resident(wu), resident(wd),
            pl.BlockSpec((1, d), lambda i: (0, 0)),
        ],
        out_specs=pl.BlockSpec((tm, d), lambda i: (i, 0)),
        compiler_params=_cparams(("parallel",)),
        name="ffn",
    )(x, g, wg, wu, wd, g_final)


def _layer(x3, l, w, mem_k, mem_v, cache, final_norm, akv_all):
    b, t, d = x3.shape
    x2 = x3.reshape(b * t, d)
    row = lambda a: a.reshape(1, -1)
    lam_init = 0.8 - 0.6 * math.exp(-0.3 * l)

    gates, cqk, cif, q, ak_all, av_all, akv, bkv, cvo = _in_proj(
        x2, row(w['g_mix']), l, w['w_in'], w['w_if'], *akv_all, (b, t))
    q3 = q.reshape(b, t, SEG)
    bkv3 = bkv.reshape(b, t, SEG)
    cqk3 = cqk.reshape(b, t, 2 * D_C)
    keep = t if cache is not None else min(BAND, t)
    b_k = bkv3[:, t - keep:, :HALF].reshape(b, keep, H_B, DH_B)
    b_v = bkv3[:, t - keep:, HALF:].reshape(b, keep, H_B, DH_B)

    gate_b = jnp.concatenate([w['b_i'], w['b_f'], jnp.zeros((LANES - 2 * H_C,), F32)]).reshape(1, LANES)
    if cache is None:
        oa = _diff_prompt(q3, akv.reshape(b, t, SEG), w['lqk'], row(w['a_head_g']), lam_init)
        ob = _band_prompt(q3, bkv3, w['relbias'])
        conv0 = jnp.zeros((b, CONV_PAD, 2 * D_C), F32)
        c0 = jnp.zeros((b, H_C, DH_C, DH_C), F32)
        n0 = jnp.zeros((b, H_C, DH_C), F32)
        m0 = jnp.zeros((b, H_C, LANES), F32)
        L = min(t, MLSTM_L)
    else:
        past = cache['a_k'].shape[2]
        ak3 = ak_all.reshape(DEPTH * b, t, HALF)
        av3 = av_all.reshape(DEPTH * b, t, HALF)
        oa = _diff_sample(q3, ak3, av3, l, cache['a_k'], cache['a_v'],
                          w['lqk'], row(w['a_head_g']), lam_init)
        nband = cache['b_k'].shape[1]
        ob = _band_sample(q3, bkv3, cache['b_k'].reshape(b, nband, -1), cache['b_v'].reshape(b, nband, -1),
                          w['relbias'], past)
        conv0 = jnp.pad(cache['conv'], ((0, 0), (CONV_PAD - (CONV_W - 1), 0), (0, 0)))
        c0, n0 = cache['C'], cache['n']
        m0 = jnp.broadcast_to(cache['m'][:, :, None], (b, H_C, LANES))
        L = t
    oc, c_new, n_new, m_new = _mlstm(cqk3, cvo.reshape(b, t, SEG), cif.reshape(b, t, LANES), conv0,
                                     w['conv_w'], row(w['conv_b']), gate_b, row(w['c_head_g']), c0, n0, m0, L)
    assert t >= CONV_W - 1
    conv_new = cqk3[:, t - (CONV_W - 1):]

    x2 = _mix(oa.reshape(b * t, -1), ob.reshape(b * t, -1), oc.reshape(b * t, -1), gates, x2, l,
              w['w_up_a'], w['w_up_b'], w['w_up_c'], w['w_o'])
    x3 = _cross(x2.reshape(b, t, d), row(w['g_cross']), l, w['w_mq'], mem_k, mem_v, w['w_mo'])
    x2 = _ffn(x3.reshape(b * t, d), row(w['g_ffn']), l, w['w_ff_g'], w['w_ff_u'], w['w_ff_d'],
              row(w['g_final']), final_norm)
    return x2.reshape(b, t, d), (ak_all, av_all), (b_k, b_v, c_new, n_new, m_new[:, :, 0], conv_new)


def kernel(x_prompt, x_sample, cache_a_k, cache_a_v, cache_b_k, cache_b_v, state_c_C, state_c_n, state_c_m, state_c_conv, cache_mem_k, cache_mem_v, mem_prompt, g_mix, w_in, a_lq1, a_lk1, a_lq2, a_lk2, a_head_g, b_rel, c_conv_w, c_conv_b, c_b_i, c_b_f, c_head_g, w_up_a, w_up_b, w_up_c, w_o, g_cross, w_mq, w_mk, w_mv, w_mo, g_ffn, w_ff_g, w_ff_u, w_ff_d, g_final):
    xp, xs = x_prompt, x_sample
    bp = x_prompt.shape[0]
    new_p = [[] for _ in range(6)]
    new_s = [[] for _ in range(6)]
    akv_p = akv_s = (None, None)
    bf = lambda a: a.astype(BF16)
    wp, wif = _prep_w_in(w_in)
    wts = dict(w_up_a=bf(w_up_a), w_up_b=bf(w_up_b), w_up_c=bf(w_up_c), w_o=bf(w_o), w_mq=bf(w_mq),
               w_mo=bf(w_mo), w_ff_g=bf(w_ff_g), w_ff_u=bf(w_ff_u), w_ff_d=bf(w_ff_d))
    mem_k = _mem_proj(mem_prompt, bf(w_mk))
    mem_v = _mem_proj(mem_prompt, bf(w_mv))
    for l in range(DEPTH):
        w = dict(g_mix=g_mix[l], w_in=wp, w_if=wif,
                 lqk=jnp.stack([a_lq1[l], a_lk1[l], a_lq2[l], a_lk2[l]]),
                 a_head_g=a_head_g[l], relbias=_relbias(b_rel[l]), conv_w=c_conv_w[l], conv_b=c_conv_b[l],
                 b_i=c_b_i[l], b_f=c_b_f[l], c_head_g=c_head_g[l],
                 g_cross=g_cross[l], g_ffn=g_ffn[l], g_final=g_final, **wts)
        last = l == DEPTH - 1
        xp, akv_p, st_p = _layer(xp, l, w, mem_k, mem_v, None, last, akv_p)
        cache = dict(a_k=cache_a_k, a_v=cache_a_v, b_k=cache_b_k[l], b_v=cache_b_v[l],
                     C=state_c_C[l], n=state_c_n[l], m=state_c_m[l], conv=state_c_conv[l])
        xs, akv_s, st_s = _layer(xs, l, w, cache_mem_k, cache_mem_v, cache, last, akv_s)
        for lst, a in zip(new_p, st_p):
            lst.append(a)
        for lst, a in zip(new_s, st_s):
            lst.append(a)
    heads = lambda a, b, t: a.reshape(DEPTH, b, t, H_A, DV_A)
    outs_p = ([heads(a, bp, x_prompt.shape[1]) for a in akv_p] + [jnp.stack(a, 0) for a in new_p]
              + [mem_k, mem_v])
    outs_s = [heads(a, xs.shape[0], xs.shape[1]) for a in akv_s] + [jnp.stack(a, 0) for a in new_s]
    return (xp, xs) + tuple(outs_p) + tuple(outs_s)
```

```python
import functools
import math

import jax
import jax.numpy as jnp
from jax import lax
from jax.experimental import pallas as pl
from jax.experimental.pallas import tpu as pltpu

F32 = jnp.float32
BF16 = jnp.bfloat16

D_MODEL = 1024
DEPTH = 2
CHUNK = 64
EPS = 1e-6
NEG = -1e30
LOG2E = math.log2(math.e)
H_A = 4
DH_A = 64
DV_A = 2 * DH_A
H_B = 8
DH_B = 64
BAND_CHUNKS = 8
BAND = BAND_CHUNKS * CHUNK
REL_CLIP = 128
H_C = 4
DH_C = 128
D_C = H_C * DH_C
CONV_W = 4
H_M = 4
DH_M = D_MODEL // H_M
D_FF = -(-8 * D_MODEL // (3 * 256)) * 256

LANES = 128
VMEM_LIMIT = 48 * 1024 * 1024

SEG = 1024
HALF = SEG // 2
GATE_TILES = 3
N_TILES = 8
IN_SIZES = (H_A * DH_A,) * 4 + (H_A * DV_A,) + (H_B * DH_B,) * 3 + (2 * D_C, D_C, D_C, 2 * H_C, 3 * D_MODEL)

ALIBI_SLOPES = tuple(2.0 ** (-8.0 * (i + 1) / H_A) for i in range(H_A))

NT_DIMS = (((1,), (1,)), ((), ()))


def _cparams(sem):
    return pltpu.CompilerParams(dimension_semantics=sem, vmem_limit_bytes=VMEM_LIMIT)


def _rms(x, g):
    ms = jnp.mean(x * x, axis=-1, keepdims=True)
    return x * lax.rsqrt(ms + EPS) * g


def _nt(a, b):
    return lax.dot_general(a, b, NT_DIMS, preferred_element_type=F32)


def _prep_w_in_kernel(w_ref, o_ref, oif_ref):
    offs = [0]
    for size in IN_SIZES:
        offs.append(offs[-1] + size)
    seg = lambda i: w_ref[:, offs[i]:offs[i + 1]]
    a_q1, a_q2, a_k1, a_k2, a_v, b_q, b_k, b_v, c_qk, c_v, c_o = (seg(i) for i in range(11))
    g = seg(12)

    def put(col, val):
        o_ref[:, col:col + val.shape[1]] = val.astype(BF16)

    put(0, g)
    put(GATE_TILES * SEG, c_qk)
    for h in range(H_A):
        hs = slice(h * DH_A, (h + 1) * DH_A)
        base = h * 2 * DH_A
        put(4 * SEG + base, a_q1[:, hs] * (DH_A ** -0.5))
        put(4 * SEG + base + DH_A, a_q2[:, hs] * (DH_A ** -0.5))
        put(5 * SEG + base, a_k1[:, hs])
        put(5 * SEG + base + DH_A, a_k2[:, hs])
    put(4 * SEG + HALF, b_q * (DH_B ** -0.5))
    put(5 * SEG + HALF, a_v)
    put(6 * SEG, b_k)
    put(6 * SEG + HALF, b_v)
    put(7 * SEG, c_v)
    put(7 * SEG + HALF, c_o)
    tile = w_ref[:, offs[11]:offs[11] + LANES]
    lane = lax.broadcasted_iota(jnp.int32, tile.shape, 1)
    oif_ref[...] = jnp.where(lane < 2 * H_C, tile, 0.0).astype(BF16)


def _prep_w_in(w):
    depth, k, n = w.shape
    tk = 256
    n_out = N_TILES * SEG
    return pl.pallas_call(
        _prep_w_in_kernel,
        out_shape=(jax.ShapeDtypeStruct((depth, k, n_out), BF16),
                   jax.ShapeDtypeStruct((depth, k, LANES), BF16)),
        grid=(depth, k // tk),
        in_specs=[pl.BlockSpec((None, tk, n), lambda l, i: (l, i, 0))],
        out_specs=(pl.BlockSpec((None, tk, n_out), lambda l, i: (l, i, 0)),
                   pl.BlockSpec((None, tk, LANES), lambda l, i: (l, i, 0))),
        compiler_params=_cparams(("parallel", "parallel")),
        name="prep_w_in",
    )(w)


IN_PROJ_TM = 256


def _in_proj_kernel(*refs, aliased, heads5d):
    x_ref, g_ref, w_ref, wif_ref = refs[:4]
    og_ref, ocqk_ref, ocif_ref, oq_ref, oak_ref, oav_ref, oakv_ref, obkv_ref, ocvo_ref = refs[6 if aliased else 4:]
    h = _rms(x_ref[...], g_ref[...]).astype(BF16)
    cols = lambda j, lo=0, hi=SEG: jnp.dot(h, w_ref[:, j * SEG + lo:j * SEG + hi], preferred_element_type=F32)

    ocif_ref[...] = jnp.dot(h, wif_ref[...], preferred_element_type=F32)
    for j in range(GATE_TILES):
        og_ref[:, j * SEG:(j + 1) * SEG] = jax.nn.sigmoid(cols(j)).astype(og_ref.dtype)
    ocqk_ref[...] = cols(GATE_TILES)
    oq_ref[...] = (cols(4) * LOG2E).astype(oq_ref.dtype)
    for out_ref, off in ((oak_ref, 0), (oav_ref, HALF)):
        a = cols(5, off, off + HALF)
        oakv_ref[:, off:off + HALF] = a.astype(oakv_ref.dtype)
        if heads5d:
            out_ref[0, 0] = a.reshape(a.shape[0], H_A, DV_A)
        else:
            out_ref[0] = a
    obkv_ref[...] = cols(6)
    ocvo_ref[...] = cols(7)


def _in_proj(x, g, l, wp, wif, ak_all, av_all, bt):
    m, k = x.shape
    tm = min(m, IN_PROJ_TM)
    aliased = ak_all is not None
    b, t = bt
    heads5d = t % tm == 0
    if heads5d:
        stacked = jax.ShapeDtypeStruct((DEPTH, b, t, H_A, DV_A), F32)
        layer_row = pl.BlockSpec((1, 1, tm, H_A, DV_A), lambda i: (l, i // (t // tm), i % (t // tm), 0, 0))
    else:
        stacked = jax.ShapeDtypeStruct((DEPTH, m, HALF), F32)
        layer_row = pl.BlockSpec((1, tm, HALF), lambda i: (l, i, 0))
    resident = lambda a: pl.BlockSpec((None,) + a.shape[1:], lambda i: (l, 0, 0), pipeline_mode=pl.Buffered(1))
    in_specs = [pl.BlockSpec((tm, k), lambda i: (i, 0)), pl.BlockSpec((1, k), lambda i: (0, 0)),
                resident(wp), resident(wif)]
    args = [x, g, wp, wif]
    if aliased:
        in_specs += [pl.BlockSpec(memory_space=pl.ANY)] * 2
        args += [ak_all, av_all]
    row = lambda w: pl.BlockSpec((tm, w), lambda i: (i, 0))
    sds = jax.ShapeDtypeStruct
    return pl.pallas_call(
        functools.partial(_in_proj_kernel, aliased=aliased, heads5d=heads5d),
        out_shape=(sds((m, GATE_TILES * SEG), BF16), sds((m, SEG), F32), sds((m, LANES), F32),
                   sds((m, SEG), BF16), stacked, stacked, sds((m, SEG), BF16),
                   sds((m, SEG), F32), sds((m, SEG), F32)),
        grid=(m // tm,),
        in_specs=in_specs,
        out_specs=(row(GATE_TILES * SEG), row(SEG), row(LANES), row(SEG), layer_row, layer_row, row(SEG),
                   row(SEG), row(SEG)),
        input_output_aliases={4: 4, 5: 5} if aliased else {},
        compiler_params=_cparams(("parallel",)),
        name="in_proj",
    )(*args)


def _mem_proj_kernel(x_ref, w_ref, o_ref):
    y = jnp.dot(x_ref[0].astype(BF16), w_ref[...], preferred_element_type=F32)
    o_ref[...] = y.reshape(o_ref.shape)


def _mem_proj(mem, w):
    b, nm, k = mem.shape
    depth, _, n = w.shape
    return pl.pallas_call(
        _mem_proj_kernel,
        out_shape=jax.ShapeDtypeStruct((depth, b, nm, H_M, DH_M), F32),
        grid=(depth, b),
        in_specs=[pl.BlockSpec((1, nm, k), lambda l, bi: (bi, 0, 0)),
                  pl.BlockSpec((None, k, n), lambda l, bi: (l, 0, 0))],
        out_specs=pl.BlockSpec((None, None, nm, H_M, DH_M), lambda l, bi: (l, bi, 0, 0, 0)),
        compiler_params=_cparams(("parallel", "parallel")),
        name="mem_proj",
    )(mem, w)


def _online_updates(jobs):
    scores, staged = {}, {}

    def softmax(i):
        s, (acc, m, l) = scores.pop(i), jobs[i][2]
        slabs = [s[:, j * LANES:(j + 1) * LANES] for j in range(s.shape[1] // LANES)]
        m_old = m[...]
        m_new = jnp.maximum(m_old, jnp.max(functools.reduce(jnp.maximum, slabs), axis=-1, keepdims=True))
        alpha = jnp.exp2(m_old - m_new)
        ps = [jnp.exp2(c - m_new) for c in slabs]
        l[...] = alpha * l[...] + jnp.sum(functools.reduce(jnp.add, ps), axis=-1, keepdims=True)
        m[...] = m_new
        staged[i] = (alpha, jnp.concatenate(ps, axis=1).astype(BF16))

    def pv(i):
        (alpha, p), (_, v, (acc, m, l)) = staged.pop(i), jobs[i]
        acc[...] = alpha * acc[...] + jnp.dot(p, v, preferred_element_type=F32)

    n = len(jobs)
    for step in range(n + 2):
        if step < n:
            scores[step] = jobs[step][0]()
        if 0 <= step - 1 < n:
            softmax(step - 1)
        if 0 <= step - 2 < n:
            pv(step - 2)


def _split_halves(q):
    lane = lax.broadcasted_iota(jnp.int32, q.shape, 1)
    lo = jnp.where(lane < DH_A, q, 0.0).astype(BF16)
    hi = jnp.where(lane >= DH_A, q, 0.0).astype(BF16)
    return lo, hi


def _lambda(lqk, lam_init):
    e1 = jnp.exp(jnp.sum(lqk[0:1, :] * lqk[1:2, :], axis=-1, keepdims=True))
    e2 = jnp.exp(jnp.sum(lqk[2:3, :] * lqk[3:4, :], axis=-1, keepdims=True))
    return e1 - e2 + lam_init


DIFF_HEADS = 4


def _diff_prompt_kernel(slopes_ref, lqk_ref, hg_ref, q_ref, k_ref, v_ref, o_ref,
                        diag_bias, acc, m, l, *, tq, nh, lam_init):
    hg = pl.program_id(1)
    qi = pl.program_id(2)
    slopes = [slopes_ref[hg * nh + i] * LOG2E for i in range(nh)]
    q0 = pl.multiple_of(qi * tq, tq)
    head_cols = [slice(i * LANES, (i + 1) * LANES) for i in range(nh)]

    @pl.when(qi == 0)
    def _():
        r = lax.broadcasted_iota(jnp.int32, (tq, tq), 0)
        c = lax.broadcasted_iota(jnp.int32, (tq, tq), 1)
        rel = (r - jnp.abs(r - c)).astype(F32)
        allowed = (c // CHUNK) <= (r // CHUNK)
        for i in range(nh):
            diag_bias[i] = jnp.where(allowed, slopes[i] * rel, NEG)

    m[...] = jnp.full(m.shape, NEG, F32)
    l[...] = jnp.zeros(l.shape, F32)
    acc[...] = jnp.zeros(acc.shape, F32)
    chains = []
    for i in range(nh):
        for j, qm in enumerate(_split_halves(q_ref[0, :, head_cols[i]])):
            c = 2 * i + j
            chains.append((qm, i, (acc.at[c], m.at[c], l.at[c])))

    score = lambda qm, k, bias: _nt(qm, k) + bias

    def full_tiles(k_starts):
        jobs = []
        for k0 in k_starts:
            kpos = (k0 - q0) + lax.broadcasted_iota(jnp.int32, (1, tq), 1)
            kposf = kpos.astype(F32)
            for qm, i, st in chains:
                k = k_ref[0, pl.ds(k0, tq), head_cols[i]]
                v = v_ref[0, pl.ds(k0, tq), head_cols[i]]
                jobs.append((functools.partial(score, qm, k, slopes[i] * kposf), v, st))
        _online_updates(jobs)

    def tile_pair(it, carry):
        k0 = pl.multiple_of(it * (2 * tq), 2 * tq)
        full_tiles([k0, pl.multiple_of(k0 + tq, tq)])
        return carry

    lax.fori_loop(0, qi // 2, tile_pair, 0)

    @pl.when(qi % 2 == 1)
    def _():
        full_tiles([pl.multiple_of((qi - 1) * tq, tq)])

    half = tq // 2
    jobs = []
    for rows, nk in ((slice(0, half), half), (slice(half, tq), tq)):
        for qm, i, st in chains:
            k = k_ref[0, pl.ds(q0, nk), head_cols[i]]
            v = v_ref[0, pl.ds(q0, nk), head_cols[i]]
            jobs.append((functools.partial(score, qm[rows], k, diag_bias[i, rows, 0:nk]), v,
                         tuple(ref.at[rows] for ref in st)))
    _online_updates(jobs)

    lam = _lambda(lqk_ref[...], lam_init)
    for i in range(nh):
        o = acc[2 * i] / l[2 * i] - lam * (acc[2 * i + 1] / l[2 * i + 1])
        o_ref[0, :, head_cols[i]] = (_rms(o, hg_ref[...]) * (1.0 - lam_init)).astype(o_ref.dtype)


def _diff_prompt(q3, kv3, lqk, head_g, lam_init):
    b, t, _ = q3.shape
    tq = min(t, 512)
    nh = DIFF_HEADS
    w = nh * LANES
    slopes = jnp.asarray(ALIBI_SLOPES, F32)
    return pl.pallas_call(
        functools.partial(_diff_prompt_kernel, tq=tq, nh=nh, lam_init=lam_init),
        out_shape=jax.ShapeDtypeStruct((b, t, H_A * DV_A), BF16),
        grid=(b, H_A // nh, t // tq),
        in_specs=[
            pl.BlockSpec(memory_space=pltpu.SMEM),
            pl.BlockSpec((4, DH_A), lambda bi, h, qi: (0, 0)),
            pl.BlockSpec((1, DV_A), lambda bi, h, qi: (0, 0)),
            pl.BlockSpec((1, tq, w), lambda bi, h, qi: (bi, qi, h)),
            pl.BlockSpec((1, t, w), lambda bi, h, qi: (bi, 0, h)),
            pl.BlockSpec((1, t, w), lambda bi, h, qi: (bi, 0, H_A // nh + h)),
        ],
        out_specs=pl.BlockSpec((1, tq, w), lambda bi, h, qi: (bi, qi, h)),
        scratch_shapes=[
            pltpu.VMEM((nh, tq, tq), F32),
            pltpu.VMEM((2 * nh, tq, LANES), F32), pltpu.VMEM((2 * nh, tq, LANES), F32),
            pltpu.VMEM((2 * nh, tq, LANES), F32),
        ],
        compiler_params=_cparams(("parallel", "parallel", "arbitrary")),
        name="diff_prompt",
    )(slopes, lqk, head_g, q3, kv3, kv3)


def _pieces_attention(qm, pieces):
    ss = [_nt(qm, k) + bias for k, _, bias in pieces]
    m = functools.reduce(jnp.maximum, [jnp.max(s, axis=-1, keepdims=True) for s in ss])
    es = [jnp.exp2(s - m) for s in ss]
    l = functools.reduce(jnp.add, [jnp.sum(e, axis=-1, keepdims=True) for e in es])
    o = functools.reduce(jnp.add, [jnp.dot(e.astype(BF16), v, preferred_element_type=F32)
                                   for e, (_, v, _) in zip(es, pieces)])
    return o / l


def _diff_sample_kernel(lqk_ref, hg_ref, q_ref, kc_ref, vc_ref, kn_ref, vn_ref, o_ref, *, lam_init):
    t = q_ref.shape[1]
    past = kc_ref.shape[1]
    lam = _lambda(lqk_ref[...], lam_init)
    rc = lax.broadcasted_iota(jnp.int32, (t, past), 0)
    cc = lax.broadcasted_iota(jnp.int32, (t, past), 1)
    rn = lax.broadcasted_iota(jnp.int32, (t, t), 0)
    cn = lax.broadcasted_iota(jnp.int32, (t, t), 1)
    dist_c = jnp.abs(past + rc - cc).astype(F32)
    dist_n = jnp.abs(rn - cn).astype(F32)
    ok_c = (cc // CHUNK) <= ((past + rc) // CHUNK)
    ok_n = ((past + cn) // CHUNK) <= ((past + rn) // CHUNK)
    kc = kc_ref[0].reshape(past, H_A * DV_A).astype(BF16)
    vc = vc_ref[0].reshape(past, H_A * DV_A).astype(BF16)
    for h in range(H_A):
        cols = slice(h * LANES, (h + 1) * LANES)
        bias_c = jnp.where(ok_c, (-ALIBI_SLOPES[h] * LOG2E) * dist_c, NEG)
        bias_n = jnp.where(ok_n, (-ALIBI_SLOPES[h] * LOG2E) * dist_n, NEG)
        pieces = [(kc[:, cols], vc[:, cols], bias_c),
                  (kn_ref[0, :, cols].astype(BF16), vn_ref[0, :, cols].astype(BF16), bias_n)]
        qa, qb = _split_halves(q_ref[0, :, cols])
        o = _pieces_attention(qa, pieces) - lam * _pieces_attention(qb, pieces)
        o_ref[0, :, cols] = (_rms(o, hg_ref[...]) * (1.0 - lam_init)).astype(o_ref.dtype)


def _diff_sample(q3, kn3, vn3, l, kc, vc, lqk, head_g, lam_init):
    b, t, _ = q3.shape
    past = kc.shape[2]
    w = H_A * DV_A
    new = pl.BlockSpec((1, t, w), lambda bi: (l * b + bi, 0, 0))
    cache = pl.BlockSpec((None, 1, past, H_A, DV_A), lambda bi: (l, bi, 0, 0, 0))
    return pl.pallas_call(
        functools.partial(_diff_sample_kernel, lam_init=lam_init),
        out_shape=jax.ShapeDtypeStruct((b, t, w), BF16),
        grid=(b,),
        in_specs=[
            pl.BlockSpec((4, DH_A), lambda bi: (0, 0)),
            pl.BlockSpec((1, DV_A), lambda bi: (0, 0)),
            pl.BlockSpec((1, t, w), lambda bi: (bi, 0, 0)), cache, cache, new, new,
        ],
        out_specs=pl.BlockSpec((1, t, w), lambda bi: (bi, 0, 0)),
        compiler_params=_cparams(("parallel",)),
        name="diff_sample",
    )(lqk, head_g, q3, kc, vc, kn3, vn3)


QPAIR = 2 * CHUNK
WIN = BAND + QPAIR


def _relbias_kernel(tab_ref, o_ref):
    h = pl.program_id(0)
    r = lax.broadcasted_iota(jnp.int32, (QPAIR, WIN), 0)
    j = lax.broadcasted_iota(jnp.int32, (QPAIR, WIN), 1)
    rel = jnp.clip(r - j + BAND, -REL_CLIP, REL_CLIP) + REL_CLIP
    qc = r // CHUNK
    kc = j // CHUNK - BAND_CHUNKS
    allowed = (kc <= qc) & (kc >= qc - BAND_CHUNKS)

    far = BAND - REL_CLIP
    near_rel = rel[:, far:]

    near = jnp.zeros((QPAIR, WIN - far), F32)
    for t in range(2 * REL_CLIP + 1):
        near = jnp.where(near_rel == t, tab_ref[h, t], near)
    bias = jnp.concatenate([jnp.full((QPAIR, far), tab_ref[h, 2 * REL_CLIP], F32), near], axis=1)
    o_ref[0] = jnp.where(allowed, bias * LOG2E, NEG)


def _relbias(table):
    return pl.pallas_call(
        _relbias_kernel,
        out_shape=jax.ShapeDtypeStruct((H_B, QPAIR, WIN), F32),
        grid=(H_B,),
        in_specs=[pl.BlockSpec(memory_space=pltpu.SMEM)],
        out_specs=pl.BlockSpec((1, QPAIR, WIN), lambda h: (h, 0, 0)),
        compiler_params=_cparams(("arbitrary",)),
        name="relbias",
    )(table)


BAND_TQ = 512
BAND_UNROLL = 2


def _band_prompt_kernel(bias_ref, q_ref, kp_ref, kc_ref, vp_ref, vc_ref, o_ref, kcat, vcat):
    i = pl.program_id(1)
    kcat[0:BAND_TQ, :] = kp_ref[0].astype(BF16)
    kcat[BAND_TQ:, :] = kc_ref[0].astype(BF16)
    vcat[0:BAND_TQ, :] = vp_ref[0].astype(BF16)
    vcat[BAND_TQ:, :] = vc_ref[0].astype(BF16)
    lane = lax.broadcasted_iota(jnp.int32, (QPAIR, LANES), 1)

    def pair_blocks(it, carry, *, first_block):
        starts = [pl.multiple_of((it * BAND_UNROLL + u) * QPAIR, QPAIR) for u in range(BAND_UNROLL)]
        units = [(r0, hp) for r0 in starts for hp in range(H_B // 2)]
        queries = {u: q_ref[0, pl.ds(r0, QPAIR), :] for u, r0 in enumerate(starts)}
        scores, probs, outs = {}, {}, {}

        def qk(i):
            r0, hp = units[i]
            cols = slice(hp * LANES, (hp + 1) * LANES)
            qs = jnp.concatenate(_split_halves(queries[i // (H_B // 2)][:, cols]), axis=0)
            s = _nt(qs, kcat[pl.ds(r0, WIN), cols]) + bias_ref[hp]
            if first_block:
                kpos = r0 - BAND_TQ + lax.broadcasted_iota(jnp.int32, (1, WIN), 1)
                s = s + jnp.where(kpos >= 0, 0.0, NEG)
            scores[i] = s

        def softmax(i):
            s = scores.pop(i)
            e = jnp.exp2(s - jnp.max(s, axis=-1, keepdims=True))
            probs[i] = (e.astype(BF16), jnp.sum(e, axis=-1, keepdims=True))

        def pv(i):
            r0, hp = units[i]
            e, l = probs.pop(i)
            o = jnp.dot(e, vcat[pl.ds(r0, WIN), hp * LANES:(hp + 1) * LANES], preferred_element_type=F32) / l
            outs[i] = jnp.where(lane < DH_B, o[:QPAIR], o[QPAIR:])

        n = len(units)
        for step in range(n + 2):
            if step < n:
                qk(step)
            if 0 <= step - 1 < n:
                softmax(step - 1)
            if 0 <= step - 2 < n:
                pv(step - 2)
        for u, r0 in enumerate(starts):
            row = [outs[u * (H_B // 2) + hp] for hp in range(H_B // 2)]
            o_ref[0, pl.ds(r0, QPAIR), :] = jnp.concatenate(row, axis=1).astype(o_ref.dtype)
        return carry

    trips = BAND_TQ // QPAIR // BAND_UNROLL

    @pl.when(i == 0)
    def _():
        lax.fori_loop(0, trips, functools.partial(pair_blocks, first_block=True), 0)

    @pl.when(i > 0)
    def _():
        lax.fori_loop(0, trips, functools.partial(pair_blocks, first_block=False), 0)


def _band_prompt(q3, kv3, bias):
    b, t, _ = q3.shape
    assert t % BAND_TQ == 0
    w = H_B * DH_B
    cq, ck, cv = 1, 0, 1
    prev = lambda col: pl.BlockSpec((1, BAND_TQ, w), lambda bi, i: (bi, jnp.maximum(i - 1, 0), col))
    cur = lambda col: pl.BlockSpec((1, BAND_TQ, w), lambda bi, i: (bi, i, col))
    return pl.pallas_call(
        _band_prompt_kernel,
        out_shape=jax.ShapeDtypeStruct((b, t, w), BF16),
        grid=(b, t // BAND_TQ),
        in_specs=[
            pl.BlockSpec((H_B // 2, 2 * QPAIR, WIN), lambda bi, i: (0, 0, 0)),
            cur(cq), prev(ck), cur(ck), prev(cv), cur(cv),
        ],
        out_specs=pl.BlockSpec((1, BAND_TQ, w), lambda bi, i: (bi, i, 0)),
        scratch_shapes=[pltpu.VMEM((2 * BAND_TQ, w), BF16), pltpu.VMEM((2 * BAND_TQ, w), BF16)],
        compiler_params=_cparams(("parallel", "arbitrary")),
        name="band_prompt",
    )(bias.reshape(H_B // 2, 2 * QPAIR, WIN), q3, kv3, kv3, kv3, kv3)


def _band_sample_kernel(bc_ref, bn_ref, q_ref, kc_ref, vc_ref, kn_ref, vn_ref, o_ref, *, past):
    t = q_ref.shape[1]
    nband = kc_ref.shape[1]
    lane = lax.broadcasted_iota(jnp.int32, (t, LANES), 1)
    qpos_c = past + lax.broadcasted_iota(jnp.int32, (t, nband), 0)
    kpos_c = past - nband + lax.broadcasted_iota(jnp.int32, (t, nband), 1)
    qpos_n = past + lax.broadcasted_iota(jnp.int32, (t, t), 0)
    kpos_n = past + lax.broadcasted_iota(jnp.int32, (t, t), 1)

    def allowed(qpos, kpos):
        qc, kc = qpos // CHUNK, kpos // CHUNK
        return (kpos >= 0) & (kc <= qc) & (kc >= qc - BAND_CHUNKS)

    ok_c = allowed(qpos_c, kpos_c)
    ok_n = allowed(qpos_n, kpos_n)
    for hp in range(H_B // 2):
        cols = slice(hp * LANES, (hp + 1) * LANES)
        halves = []
        for sub, qm in enumerate(_split_halves(q_ref[0, :, cols])):
            hd = 2 * hp + sub
            pieces = [(kc_ref[0, :, cols].astype(BF16), vc_ref[0, :, cols].astype(BF16),
                       jnp.where(ok_c, bc_ref[hd], NEG)),
                      (kn_ref[0, :, cols].astype(BF16), vn_ref[0, :, cols].astype(BF16),
                       jnp.where(ok_n, bn_ref[hd], NEG))]
            halves.append(_pieces_attention(qm, pieces))
        o_ref[0, :, cols] = jnp.where(lane < DH_B, halves[0], halves[1]).astype(o_ref.dtype)


def _band_sample(q3, kv3, kc, vc, bias, past):
    b, t, _ = q3.shape
    nband = kc.shape[1]
    assert nband == BAND and t <= CHUNK
    w = H_B * DH_B
    bias_c = bias[:, :t, :nband]
    bias_n = bias[:, :t, nband:nband + t]
    blk = lambda col: pl.BlockSpec((1, t, w), lambda bi: (bi, 0, col))
    cache = pl.BlockSpec((1, nband, w), lambda bi: (bi, 0, 0))
    return pl.pallas_call(
        functools.partial(_band_sample_kernel, past=past),
        out_shape=jax.ShapeDtypeStruct((b, t, w), BF16),
        grid=(b,),
        in_specs=[
            pl.BlockSpec((H_B, t, nband), lambda bi: (0, 0, 0)),
            pl.BlockSpec((H_B, t, t), lambda bi: (0, 0, 0)),
            blk(1), cache, cache, blk(0), blk(1),
        ],
        out_specs=pl.BlockSpec((1, t, w), lambda bi: (bi, 0, 0)),
        compiler_params=_cparams(("parallel",)),
        name="band_sample",
    )(bias_c, bias_n, q3, kc, vc, kv3, kv3)


CONV_PAD = 8
MLSTM_L = 256


def _mlstm_kernel(cqk_ref, cv_ref, co_ref, cif_ref, conv0_ref, cw_ref, cb_ref, gb_ref, hg_ref,
                  c0_ref, n0_ref, m0_ref, hs_ref, c_ref, n_ref, m_ref, xext, *, L):
    @pl.when(pl.program_id(1) == 0)
    def _():
        xext[0:CONV_PAD, :] = conv0_ref[0]
        c_ref[...] = c0_ref[...]
        n_ref[...] = n0_ref[...]
        m_ref[...] = m0_ref[...]

    xext[CONV_PAD:CONV_PAD + L, :] = cqk_ref[0]
    base = CONV_PAD - (CONV_W - 1)
    u = 0.0
    for j in range(CONV_W):
        u = u + xext[base + j:base + j + L, :] * cw_ref[j:j + 1, :]
    u = cb_ref[...] + u
    u = u * jax.nn.sigmoid(u)
    xext[0:CONV_PAD, :] = xext[L:L + CONV_PAD, :]

    z = cif_ref[0] + gb_ref[...]
    lane = lax.broadcasted_iota(jnp.int32, (L, LANES), 1)
    lf = jnp.minimum(z, 0.0) - jnp.log1p(jnp.exp(-jnp.abs(z)))
    tr = lax.broadcasted_iota(jnp.int32, (L, L), 0)
    tc = lax.broadcasted_iota(jnp.int32, (L, L), 1)
    causal = tr >= tc
    b_all = jnp.dot(causal.astype(F32), lf, preferred_element_type=F32,
                    precision=lax.Precision.HIGHEST)
    sr = lax.broadcasted_iota(jnp.int32, (8, LANES), 0)
    sc = lax.broadcasted_iota(jnp.int32, (8, LANES), 1)
    sel = jnp.where(sr < H_C, jnp.where(sc == sr, 1.0, jnp.where(sc == sr + H_C, -1.0, 0.0)), 0.0)
    rt = lax.dot_general(sel, jnp.where(lane < H_C, z, b_all), NT_DIMS,
                         preferred_element_type=F32, precision=lax.Precision.HIGHEST)
    er = lax.broadcasted_iota(jnp.int32, (DH_C, DH_C), 0)
    ec = lax.broadcasted_iota(jnp.int32, (DH_C, DH_C), 1)
    eye = jnp.where(er == ec, 1.0, 0.0).astype(BF16)

    for h in range(H_C):
        cols = slice(h * DH_C, (h + 1) * DH_C)
        q = u[:, h * DH_C:(h + 1) * DH_C]
        k = u[:, D_C + h * DH_C:D_C + (h + 1) * DH_C] * (DH_C ** -0.5)
        v = cv_ref[0, :, cols]
        qb, kb_, vb_ = q.astype(BF16), k.astype(BF16), v.astype(BF16)
        b_col = b_all[:, H_C + h:H_C + h + 1]
        li_col = z[:, h:h + 1]
        m_prev = m_ref[0, h:h + 1, 0:1]
        cs = c_ref[0, h]
        ns = n_ref[0, h:h + 1, :]

        dmat = jnp.where(causal, b_col + rt[h:h + 1, :], NEG)
        inter = b_col + m_prev
        m_t = jnp.maximum(inter, jnp.max(dmat, axis=-1, keepdims=True))
        w_intra = jnp.exp(dmat - m_t)
        w_inter = jnp.exp(inter - m_t)
        a = w_intra * _nt(qb, kb_)
        num = (jnp.dot(a.astype(BF16), vb_, preferred_element_type=F32)
               + w_inter * _nt(qb, cs.astype(BF16)))
        den = jnp.sum(a, axis=-1, keepdims=True) + w_inter * jnp.sum(q * ns, axis=-1, keepdims=True)
        hh = num / jnp.maximum(jnp.abs(den), jnp.exp(-m_t))

        b_last = b_col[L - 1:L, :]
        g = b_last - b_col + li_col
        m_new = jnp.maximum(b_last + m_prev, jnp.max(g, axis=0, keepdims=True))
        ws = jnp.exp(g - m_new)
        decay = jnp.exp(b_last + m_prev - m_new)
        vwt = _nt(eye, (ws * v).astype(BF16)).astype(BF16)
        c_ref[0, h] = decay * cs + jnp.dot(vwt, kb_, preferred_element_type=F32)
        n_ref[0, h:h + 1, :] = decay * ns + jnp.sum(ws * k, axis=0, keepdims=True)
        m_ref[0, h:h + 1, :] = jnp.broadcast_to(m_new, (1, LANES))

        gate = jax.nn.sigmoid(co_ref[0, :, cols])
        hs_ref[0, :, cols] = (_rms(hh, hg_ref[...]) * gate).astype(hs_ref.dtype)


def _mlstm(cqk3, cvo3, cif3, conv0, conv_w, conv_b, gate_b, head_g, c0, n0, m0, L):
    b, t, _ = cqk3.shape
    nc = t // L
    blk = lambda col, w: pl.BlockSpec((1, L, w), lambda bi, c: (bi, c, col))
    const = lambda shape: pl.BlockSpec(shape, lambda bi, c: (0,) * len(shape))
    per_b = lambda shape: pl.BlockSpec((1,) + shape, lambda bi, c: (bi,) + (0,) * len(shape))
    return pl.pallas_call(
        functools.partial(_mlstm_kernel, L=L),
        out_shape=(jax.ShapeDtypeStruct((b, t, D_C), BF16),
                   jax.ShapeDtypeStruct((b, H_C, DH_C, DH_C), F32),
                   jax.ShapeDtypeStruct((b, H_C, DH_C), F32),
                   jax.ShapeDtypeStruct((b, H_C, LANES), F32)),
        grid=(b, nc),
        in_specs=[
            blk(0, 2 * D_C), blk(0, D_C), blk(1, D_C), blk(0, LANES),
            per_b((CONV_PAD, 2 * D_C)), const((CONV_W, 2 * D_C)), const((1, 2 * D_C)),
            const((1, LANES)), const((1, DH_C)),
            per_b((H_C, DH_C, DH_C)), per_b((H_C, DH_C)), per_b((H_C, LANES)),
        ],
        out_specs=(pl.BlockSpec((1, L, D_C), lambda bi, c: (bi, c, 0)),
                   per_b((H_C, DH_C, DH_C)), per_b((H_C, DH_C)), per_b((H_C, LANES))),
        scratch_shapes=[pltpu.VMEM((CONV_PAD + L, 2 * D_C), F32)],
        compiler_params=_cparams(("parallel", "arbitrary")),
        name="mlstm",
    )(cqk3, cvo3, cvo3, cif3, conv0, conv_w, conv_b, gate_b, head_g, c0, n0, m0)


def _mix_kernel(oa_ref, ob_ref, oc_ref, ga_ref, gb_ref, gc_ref, x_ref, wa_ref, wb_ref, wc_ref, wo_ref, o_ref):
    mixed = (ga_ref[...] * jnp.dot(oa_ref[...], wa_ref[...], preferred_element_type=F32)
             + gb_ref[...] * jnp.dot(ob_ref[...], wb_ref[...], preferred_element_type=F32)
             + gc_ref[...] * jnp.dot(oc_ref[...], wc_ref[...], preferred_element_type=F32))
    o_ref[...] = x_ref[...] + jnp.dot(mixed.astype(BF16), wo_ref[...], preferred_element_type=F32)


def _mix(oa, ob, oc, gates, x, l, wa, wb, wc, wo):
    m, d = x.shape
    tm = min(m, 512)
    row = lambda w, col=0: pl.BlockSpec((tm, w), lambda i: (i, col))
    full = lambda a: pl.BlockSpec((None,) + a.shape[1:], lambda i: (l, 0, 0))
    g0 = 0
    return pl.pallas_call(
        _mix_kernel,
        out_shape=jax.ShapeDtypeStruct((m, d), F32),
        grid=(m // tm,),
        in_specs=[row(oa.shape[1]), row(ob.shape[1]), row(oc.shape[1]),
                  row(d, g0), row(d, g0 + 1), row(d, g0 + 2), row(d),
                  full(wa), full(wb), full(wc), full(wo)],
        out_specs=row(d),
        compiler_params=_cparams(("parallel",)),
        name="mix",
    )(oa, ob, oc, gates, gates, gates, x, wa, wb, wc, wo)


def _cross_kernel(x_ref, g_ref, wq_ref, mk_ref, mv_ref, wo_ref, o_ref):
    x = x_ref[0]
    h = _rms(x, g_ref[...]).astype(BF16)
    q = (jnp.dot(h, wq_ref[...], preferred_element_type=F32) * (DH_M ** -0.5 * LOG2E)).astype(BF16)
    nm = mk_ref.shape[1]
    mk = mk_ref[0].reshape(nm, H_M * DH_M).astype(BF16)
    mv = mv_ref[0].reshape(nm, H_M * DH_M).astype(BF16)
    head_cols = [slice(hd * DH_M, (hd + 1) * DH_M) for hd in range(H_M)]
    scores, probs, outs = {}, {}, {}
    for step in range(H_M + 2):
        if step < H_M:
            scores[step] = _nt(q[:, head_cols[step]], mk[:, head_cols[step]])
        if 0 <= step - 1 < H_M:
            s = scores.pop(step - 1)
            e = jnp.exp2(s - jnp.max(s, axis=-1, keepdims=True))
            probs[step - 1] = (e.astype(BF16), jnp.sum(e, axis=-1, keepdims=True))
        if 0 <= step - 2 < H_M:
            e, l = probs.pop(step - 2)
            o = jnp.dot(e, mv[:, head_cols[step - 2]], preferred_element_type=F32) / l
            outs[step - 2] = o.astype(BF16)
    o = jnp.concatenate([outs[hd] for hd in range(H_M)], axis=1)
    o_ref[0] = x + jnp.dot(o, wo_ref[...], preferred_element_type=F32)


def _cross(x3, g, l, wq, mk, mv, wo):
    b, t, d = x3.shape
    tm = min(t, 512)
    nm = mk.shape[2]
    full = lambda a: pl.BlockSpec((None,) + a.shape[1:], lambda bi, i: (l, 0, 0))
    mem = pl.BlockSpec((None, 1, nm, H_M, DH_M), lambda bi, i: (l, bi, 0, 0, 0))
    return pl.pallas_call(
        _cross_kernel,
        out_shape=jax.ShapeDtypeStruct((b, t, d), F32),
        grid=(b, t // tm),
        in_specs=[pl.BlockSpec((1, tm, d), lambda bi, i: (bi, i, 0)),
                  pl.BlockSpec(g.shape, lambda bi, i: (0, 0)), full(wq), mem, mem, full(wo)],
        out_specs=pl.BlockSpec((1, tm, d), lambda bi, i: (bi, i, 0)),
        compiler_params=_cparams(("parallel", "parallel")),
        name="cross",
    )(x3, g, wq, mk, mv, wo)


FFN_TF = 1408
FFN_TM = 512


def _ffn_kernel(x_ref, g_ref, wg_ref, wu_ref, wd_ref, gf_ref, o_ref, *, final_norm):
    x = x_ref[...]
    h = _rms(x, g_ref[...]).astype(BF16)
    chunks = [slice(c, c + FFN_TF) for c in range(0, D_FF, FFN_TF)]
    pre = [(jnp.dot(h, wg_ref[:, c], preferred_element_type=F32),
            jnp.dot(h, wu_ref[:, c], preferred_element_type=F32)) for c in chunks]
    y = x
    for c, (gate, up) in zip(chunks, pre):
        a = (gate * jax.nn.sigmoid(gate) * up).astype(BF16)
        y = y + jnp.dot(a, wd_ref[c, :], preferred_element_type=F32)
    if final_norm:
        y = _rms(y, gf_ref[...])
    o_ref[...] = y


def _ffn(x, g, l, wg, wu, wd, g_final, final_norm):
    m, d = x.shape
    tm = min(m, FFN_TM)
    resident = lambda a: pl.BlockSpec((None,) + a.shape[1:], lambda i: (l, 0, 0),
                                      pipeline_mode=pl.Buffered(1))
    return pl.pallas_call(
        functools.partial(_ffn_kernel, final_norm=final_norm),
        out_shape=jax.ShapeDtypeStruct((m, d), F32),
        grid=(m // tm,),
        in_specs=[
            pl.BlockSpec((tm, d), lambda i: (i, 0)),
            pl.BlockSpec((1, d), lambda i: (0, 0)),
            resident(wg), resident(wu), resident(wd),
            pl.BlockSpec((1, d), lambda i: (0, 0)),
        ],
        out_specs=pl.BlockSpec((tm, d), lambda i: (i, 0)),
        compiler_params=_cparams(("parallel",)),
        name="ffn",
    )(x, g, wg, wu, wd, g_final)


def _layer(x3, l, w, mem_k, mem_v, cache, final_norm, akv_all):
    b, t, d = x3.shape
    x2 = x3.reshape(b * t, d)
    row = lambda a: a.reshape(1, -1)
    lam_init = 0.8 - 0.6 * math.exp(-0.3 * l)

    gates, cqk, cif, q, ak_all, av_all, akv, bkv, cvo = _in_proj(
        x2, row(w['g_mix']), l, w['w_in'], w['w_if'], *akv_all, (b, t))
    q3 = q.reshape(b, t, SEG)
    bkv3 = bkv.reshape(b, t, SEG)
    cqk3 = cqk.reshape(b, t, 2 * D_C)
    keep = t if cache is not None else min(BAND, t)
    b_k = bkv3[:, t - keep:, :HALF].reshape(b, keep, H_B, DH_B)
    b_v = bkv3[:, t - keep:, HALF:].reshape(b, keep, H_B, DH_B)

    gate_b = jnp.concatenate([w['b_i'], w['b_f'], jnp.zeros((LANES - 2 * H_C,), F32)]).reshape(1, LANES)
    if cache is None:
        oa = _diff_prompt(q3, akv.reshape(b, t, SEG), w['lqk'], row(w['a_head_g']), lam_init)
        ob = _band_prompt(q3, bkv3, w['relbias'])
        conv0 = jnp.zeros((b, CONV_PAD, 2 * D_C), F32)
        c0 = jnp.zeros((b, H_C, DH_C, DH_C), F32)
        n0 = jnp.zeros((b, H_C, DH_C), F32)
        m0 = jnp.zeros((b, H_C, LANES), F32)
        L = min(t, MLSTM_L)
    else:
        past = cache['a_k'].shape[2]
        ak3 = ak_all.reshape(DEPTH * b, t, HALF)
        av3 = av_all.reshape(DEPTH * b, t, HALF)
        oa = _diff_sample(q3, ak3, av3, l, cache['a_k'], cache['a_v'],
                          w['lqk'], row(w['a_head_g']), lam_init)
        nband = cache['b_k'].shape[1]
        ob = _band_sample(q3, bkv3, cache['b_k'].reshape(b, nband, -1), cache['b_v'].reshape(b, nband, -1),
                          w['relbias'], past)
        conv0 = jnp.pad(cache['conv'], ((0, 0), (CONV_PAD - (CONV_W - 1), 0), (0, 0)))
        c0, n0 = cache['C'], cache['n']
        m0 = jnp.broadcast_to(cache['m'][:, :, None], (b, H_C, LANES))
        L = t
    oc, c_new, n_new, m_new = _mlstm(cqk3, cvo.reshape(b, t, SEG), cif.reshape(b, t, LANES), conv0,
                                     w['conv_w'], row(w['conv_b']), gate_b, row(w['c_head_g']), c0, n0, m0, L)
    assert t >= CONV_W - 1
    conv_new = cqk3[:, t - (CONV_W - 1):]

    x2 = _mix(oa.reshape(b * t, -1), ob.reshape(b * t, -1), oc.reshape(b * t, -1), gates, x2, l,
              w['w_up_a'], w['w_up_b'], w['w_up_c'], w['w_o'])
    x3 = _cross(x2.reshape(b, t, d), row(w['g_cross']), l, w['w_mq'], mem_k, mem_v, w['w_mo'])
    x2 = _ffn(x3.reshape(b * t, d), row(w['g_ffn']), l, w['w_ff_g'], w['w_ff_u'], w['w_ff_d'],
              row(w['g_final']), final_norm)
    return x2.reshape(b, t, d), (ak_all, av_all), (b_k, b_v, c_new, n_new, m_new[:, :, 0], conv_new)


def kernel(x_prompt, x_sample, cache_a_k, cache_a_v, cache_b_k, cache_b_v, state_c_C, state_c_n, state_c_m, state_c_conv, cache_mem_k, cache_mem_v, mem_prompt, g_mix, w_in, a_lq1, a_lk1, a_lq2, a_lk2, a_head_g, b_rel, c_conv_w, c_conv_b, c_b_i, c_b_f, c_head_g, w_up_a, w_up_b, w_up_c, w_o, g_cross, w_mq, w_mk, w_mv, w_mo, g_ffn, w_ff_g, w_ff_u, w_ff_d, g_final):
    xp, xs = x_prompt, x_sample
    bp = x_prompt.shape[0]
    new_p = [[] for _ in range(6)]
    new_s = [[] for _ in range(6)]
    akv_p = akv_s = (None, None)
    bf = lambda a: a.astype(BF16)
    wp, wif = _prep_w_in(w_in)
    wts = dict(w_up_a=bf(w_up_a), w_up_b=bf(w_up_b), w_up_c=bf(w_up_c), w_o=bf(w_o), w_mq=bf(w_mq),
               w_mo=bf(w_mo), w_ff_g=bf(w_ff_g), w_ff_u=bf(w_ff_u), w_ff_d=bf(w_ff_d))
    mem_k = _mem_proj(mem_prompt, bf(w_mk))
    mem_v = _mem_proj(mem_prompt, bf(w_mv))
    for l in range(DEPTH):
        w = dict(g_mix=g_mix[l], w_in=wp, w_if=wif,
                 lqk=jnp.stack([a_lq1[l], a_lk1[l], a_lq2[l], a_lk2[l]]),
                 a_head_g=a_head_g[l], relbias=_relbias(b_rel[l]), conv_w=c_conv_w[l], conv_b=c_conv_b[l],
                 b_i=c_b_i[l], b_f=c_b_f[l], c_head_g=c_head_g[l],
                 g_cross=g_cross[l], g_ffn=g_ffn[l], g_final=g_final, **wts)
        last = l == DEPTH - 1
        xp, akv_p, st_p = _layer(xp, l, w, mem_k, mem_v, None, last, akv_p)
        cache = dict(a_k=cache_a_k, a_v=cache_a_v, b_k=cache_b_k[l], b_v=cache_b_v[l],
                     C=state_c_C[l], n=state_c_n[l], m=state_c_m[l], conv=state_c_conv[l])
        xs, akv_s, st_s = _layer(xs, l, w, cache_mem_k, cache_mem_v, cache, last, akv_s)
        for lst, a in zip(new_p, st_p):
            lst.append(a)
        for lst, a in zip(new_s, st_s):
            lst.append(a)
    heads = lambda a, b, t: a.reshape(DEPTH, b, t, H_A, DV_A)
    outs_p = ([heads(a, bp, x_prompt.shape[1]) for a in akv_p] + [jnp.stack(a, 0) for a in new_p]
              + [mem_k, mem_v])
    outs_s = [heads(a, xs.shape[0], xs.shape[1]) for a in akv_s] + [jnp.stack(a, 0) for a in new_s]
    return (xp, xs) + tuple(outs_p) + tuple(outs_s)
```

```python
import functools
import math

import jax
import jax.numpy as jnp
from jax import lax
from jax.experimental import pallas as pl
from jax.experimental.pallas import tpu as pltpu

F32 = jnp.float32
BF16 = jnp.bfloat16

D_MODEL = 1024
DEPTH = 2
CHUNK = 64
EPS = 1e-6
NEG = -1e30
LOG2E = math.log2(math.e)
H_A = 4
DH_A = 64
DV_A = 2 * DH_A
H_B = 8
DH_B = 64
BAND_CHUNKS = 8
BAND = BAND_CHUNKS * CHUNK
REL_CLIP = 128
H_C = 4
DH_C = 128
D_C = H_C * DH_C
CONV_W = 4
H_M = 4
DH_M = D_MODEL // H_M
D_FF = -(-8 * D_MODEL // (3 * 256)) * 256

LANES = 128
VMEM_LIMIT = 48 * 1024 * 1024

SEG = 1024
HALF = SEG // 2
GATE_TILES = 3
N_TILES = 8
IN_SIZES = (H_A * DH_A,) * 4 + (H_A * DV_A,) + (H_B * DH_B,) * 3 + (2 * D_C, D_C, D_C, 2 * H_C, 3 * D_MODEL)

ALIBI_SLOPES = tuple(2.0 ** (-8.0 * (i + 1) / H_A) for i in range(H_A))

NT_DIMS = (((1,), (1,)), ((), ()))


def _cparams(sem):
    return pltpu.CompilerParams(dimension_semantics=sem, vmem_limit_bytes=VMEM_LIMIT)


def _rms(x, g):
    ms = jnp.mean(x * x, axis=-1, keepdims=True)
    return x * lax.rsqrt(ms + EPS) * g


def _nt(a, b):
    return lax.dot_general(a, b, NT_DIMS, preferred_element_type=F32)


def _prep_w_in_kernel(w_ref, o_ref, oif_ref):
    offs = [0]
    for size in IN_SIZES:
        offs.append(offs[-1] + size)
    seg = lambda i: w_ref[:, offs[i]:offs[i + 1]]
    a_q1, a_q2, a_k1, a_k2, a_v, b_q, b_k, b_v, c_qk, c_v, c_o = (seg(i) for i in range(11))
    g = seg(12)

    def put(col, val):
        o_ref[:, col:col + val.shape[1]] = val.astype(BF16)

    put(0, g)
    put(GATE_TILES * SEG, c_qk)
    for h in range(H_A):
        hs = slice(h * DH_A, (h + 1) * DH_A)
        base = h * 2 * DH_A
        put(4 * SEG + base, a_q1[:, hs] * (DH_A ** -0.5))
        put(4 * SEG + base + DH_A, a_q2[:, hs] * (DH_A ** -0.5))
        put(5 * SEG + base, a_k1[:, hs])
        put(5 * SEG + base + DH_A, a_k2[:, hs])
    put(4 * SEG + HALF, b_q * (DH_B ** -0.5))
    put(5 * SEG + HALF, a_v)
    put(6 * SEG, b_k)
    put(6 * SEG + HALF, b_v)
    put(7 * SEG, c_v)
    put(7 * SEG + HALF, c_o)
    tile = w_ref[:, offs[11]:offs[11] + LANES]
    lane = lax.broadcasted_iota(jnp.int32, tile.shape, 1)
    oif_ref[...] = jnp.where(lane < 2 * H_C, tile, 0.0).astype(BF16)


def _prep_w_in(w):
    depth, k, n = w.shape
    tk = 256
    n_out = N_TILES * SEG
    return pl.pallas_call(
        _prep_w_in_kernel,
        out_shape=(jax.ShapeDtypeStruct((depth, k, n_out), BF16),
                   jax.ShapeDtypeStruct((depth, k, LANES), BF16)),
        grid=(depth, k // tk),
        in_specs=[pl.BlockSpec((None, tk, n), lambda l, i: (l, i, 0))],
        out_specs=(pl.BlockSpec((None, tk, n_out), lambda l, i: (l, i, 0)),
                   pl.BlockSpec((None, tk, LANES), lambda l, i: (l, i, 0))),
        compiler_params=_cparams(("parallel", "parallel")),
        name="prep_w_in",
    )(w)


IN_PROJ_TM = 256


def _in_proj_kernel(*refs, aliased, heads5d):
    x_ref, g_ref, w_ref, wif_ref = refs[:4]
    og_ref, ocqk_ref, ocif_ref, oq_ref, oak_ref, oav_ref, oakv_ref, obkv_ref, ocvo_ref = refs[6 if aliased else 4:]
    h = _rms(x_ref[...], g_ref[...]).astype(BF16)
    cols = lambda j, lo=0, hi=SEG: jnp.dot(h, w_ref[:, j * SEG + lo:j * SEG + hi], preferred_element_type=F32)

    ocif_ref[...] = jnp.dot(h, wif_ref[...], preferred_element_type=F32)
    for j in range(GATE_TILES):
        og_ref[:, j * SEG:(j + 1) * SEG] = jax.nn.sigmoid(cols(j)).astype(og_ref.dtype)
    ocqk_ref[...] = cols(GATE_TILES)
    oq_ref[...] = (cols(4) * LOG2E).astype(oq_ref.dtype)
    for out_ref, off in ((oak_ref, 0), (oav_ref, HALF)):
        a = cols(5, off, off + HALF)
        oakv_ref[:, off:off + HALF] = a.astype(oakv_ref.dtype)
        if heads5d:
            out_ref[0, 0] = a.reshape(a.shape[0], H_A, DV_A)
        else:
            out_ref[0] = a
    obkv_ref[...] = cols(6)
    ocvo_ref[...] = cols(7)


def _in_proj(x, g, l, wp, wif, ak_all, av_all, bt):
    m, k = x.shape
    tm = min(m, IN_PROJ_TM)
    aliased = ak_all is not None
    b, t = bt
    heads5d = t % tm == 0
    if heads5d:
        stacked = jax.ShapeDtypeStruct((DEPTH, b, t, H_A, DV_A), F32)
        layer_row = pl.BlockSpec((1, 1, tm, H_A, DV_A), lambda i: (l, i // (t // tm), i % (t // tm), 0, 0))
    else:
        stacked = jax.ShapeDtypeStruct((DEPTH, m, HALF), F32)
        layer_row = pl.BlockSpec((1, tm, HALF), lambda i: (l, i, 0))
    resident = lambda a: pl.BlockSpec((None,) + a.shape[1:], lambda i: (l, 0, 0), pipeline_mode=pl.Buffered(1))
    in_specs = [pl.BlockSpec((tm, k), lambda i: (i, 0)), pl.BlockSpec((1, k), lambda i: (0, 0)),
                resident(wp), resident(wif)]
    args = [x, g, wp, wif]
    if aliased:
        in_specs += [pl.BlockSpec(memory_space=pl.ANY)] * 2
        args += [ak_all, av_all]
    row = lambda w: pl.BlockSpec((tm, w), lambda i: (i, 0))
    sds = jax.ShapeDtypeStruct
    return pl.pallas_call(
        functools.partial(_in_proj_kernel, aliased=aliased, heads5d=heads5d),
        out_shape=(sds((m, GATE_TILES * SEG), BF16), sds((m, SEG), F32), sds((m, LANES), F32),
                   sds((m, SEG), BF16), stacked, stacked, sds((m, SEG), BF16),
                   sds((m, SEG), F32), sds((m, SEG), F32)),
        grid=(m // tm,),
        in_specs=in_specs,
        out_specs=(row(GATE_TILES * SEG), row(SEG), row(LANES), row(SEG), layer_row, layer_row, row(SEG),
                   row(SEG), row(SEG)),
        input_output_aliases={4: 4, 5: 5} if aliased else {},
        compiler_params=_cparams(("parallel",)),
        name="in_proj",
    )(*args)


def _mem_proj_kernel(x_ref, w_ref, o_ref):
    y = jnp.dot(x_ref[0].astype(BF16), w_ref[...], preferred_element_type=F32)
    o_ref[...] = y.reshape(o_ref.shape)


def _mem_proj(mem, w):
    b, nm, k = mem.shape
    depth, _, n = w.shape
    return pl.pallas_call(
        _mem_proj_kernel,
        out_shape=jax.ShapeDtypeStruct((depth, b, nm, H_M, DH_M), F32),
        grid=(depth, b),
        in_specs=[pl.BlockSpec((1, nm, k), lambda l, bi: (bi, 0, 0)),
                  pl.BlockSpec((None, k, n), lambda l, bi: (l, 0, 0))],
        out_specs=pl.BlockSpec((None, None, nm, H_M, DH_M), lambda l, bi: (l, bi, 0, 0, 0)),
        compiler_params=_cparams(("parallel", "parallel")),
        name="mem_proj",
    )(mem, w)


def _online_updates(jobs):
    scores, staged = {}, {}

    def softmax(i):
        s, (acc, m, l) = scores.pop(i), jobs[i][2]
        slabs = [s[:, j * LANES:(j + 1) * LANES] for j in range(s.shape[1] // LANES)]
        m_old = m[...]
        m_new = jnp.maximum(m_old, jnp.max(functools.reduce(jnp.maximum, slabs), axis=-1, keepdims=True))
        alpha = jnp.exp2(m_old - m_new)
        ps = [jnp.exp2(c - m_new) for c in slabs]
        l[...] = alpha * l[...] + jnp.sum(functools.reduce(jnp.add, ps), axis=-1, keepdims=True)
        m[...] = m_new
        staged[i] = (alpha, jnp.concatenate(ps, axis=1).astype(BF16))

    def pv(i):
        (alpha, p), (_, v, (acc, m, l)) = staged.pop(i), jobs[i]
        acc[...] = alpha * acc[...] + jnp.dot(p, v, preferred_element_type=F32)

    n = len(jobs)
    for step in range(n + 2):
        if step < n:
            scores[step] = jobs[step][0]()
        if 0 <= step - 1 < n:
            softmax(step - 1)
        if 0 <= step - 2 < n:
            pv(step - 2)


def _split_halves(q):
    lane = lax.broadcasted_iota(jnp.int32, q.shape, 1)
    lo = jnp.where(lane < DH_A, q, 0.0).astype(BF16)
    hi = jnp.where(lane >= DH_A, q, 0.0).astype(BF16)
    return lo, hi


def _lambda(lqk, lam_init):
    e1 = jnp.exp(jnp.sum(lqk[0:1, :] * lqk[1:2, :], axis=-1, keepdims=True))
    e2 = jnp.exp(jnp.sum(lqk[2:3, :] * lqk[3:4, :], axis=-1, keepdims=True))
    return e1 - e2 + lam_init


DIFF_HEADS = 4


def _diff_prompt_kernel(slopes_ref, lqk_ref, hg_ref, q_ref, k_ref, v_ref, o_ref,
                        diag_bias, acc, m, l, *, tq, nh, lam_init):
    hg = pl.program_id(1)
    qi = pl.program_id(2)
    slopes = [slopes_ref[hg * nh + i] * LOG2E for i in range(nh)]
    q0 = pl.multiple_of(qi * tq, tq)
    head_cols = [slice(i * LANES, (i + 1) * LANES) for i in range(nh)]

    @pl.when(qi == 0)
    def _():
        r = lax.broadcasted_iota(jnp.int32, (tq, tq), 0)
        c = lax.broadcasted_iota(jnp.int32, (tq, tq), 1)
        rel = (r - jnp.abs(r - c)).astype(F32)
        allowed = (c // CHUNK) <= (r // CHUNK)
        for i in range(nh):
            diag_bias[i] = jnp.where(allowed, slopes[i] * rel, NEG)

    m[...] = jnp.full(m.shape, NEG, F32)
    l[...] = jnp.zeros(l.shape, F32)
    acc[...] = jnp.zeros(acc.shape, F32)
    chains = []
    for i in range(nh):
        for j, qm in enumerate(_split_halves(q_ref[0, :, head_cols[i]])):
            c = 2 * i + j
            chains.append((qm, i, (acc.at[c], m.at[c], l.at[c])))

    score = lambda qm, k, bias: _nt(qm, k) + bias

    def full_tiles(k_starts):
        jobs = []
        for k0 in k_starts:
            kpos = (k0 - q0) + lax.broadcasted_iota(jnp.int32, (1, tq), 1)
            kposf = kpos.astype(F32)
            for qm, i, st in chains:
                k = k_ref[0, pl.ds(k0, tq), head_cols[i]]
                v = v_ref[0, pl.ds(k0, tq), head_cols[i]]
                jobs.append((functools.partial(score, qm, k, slopes[i] * kposf), v, st))
        _online_updates(jobs)

    def tile_pair(it, carry):
        k0 = pl.multiple_of(it * (2 * tq), 2 * tq)
        full_tiles([k0, pl.multiple_of(k0 + tq, tq)])
        return carry

    lax.fori_loop(0, qi // 2, tile_pair, 0)

    @pl.when(qi % 2 == 1)
    def _():
        full_tiles([pl.multiple_of((qi - 1) * tq, tq)])

    half = tq // 2
    jobs = []
    for rows, nk in ((slice(0, half), half), (slice(half, tq), tq)):
        for qm, i, st in chains:
            k = k_ref[0, pl.ds(q0, nk), head_cols[i]]
            v = v_ref[0, pl.ds(q0, nk), head_cols[i]]
            jobs.append((functools.partial(score, qm[rows], k, diag_bias[i, rows, 0:nk]), v,
                         tuple(ref.at[rows] for ref in st)))
    _online_updates(jobs)

    lam = _lambda(lqk_ref[...], lam_init)
    for i in range(nh):
        o = acc[2 * i] / l[2 * i] - lam * (acc[2 * i + 1] / l[2 * i + 1])
        o_ref[0, :, head_cols[i]] = (_rms(o, hg_ref[...]) * (1.0 - lam_init)).astype(o_ref.dtype)


def _diff_prompt(q3, kv3, lqk, head_g, lam_init):
    b, t, _ = q3.shape
    tq = min(t, 512)
    nh = DIFF_HEADS
    w = nh * LANES
    slopes = jnp.asarray(ALIBI_SLOPES, F32)
    return pl.pallas_call(
        functools.partial(_diff_prompt_kernel, tq=tq, nh=nh, lam_init=lam_init),
        out_shape=jax.ShapeDtypeStruct((b, t, H_A * DV_A), BF16),
        grid=(b, H_A // nh, t // tq),
        in_specs=[
            pl.BlockSpec(memory_space=pltpu.SMEM),
            pl.BlockSpec((4, DH_A), lambda bi, h, qi: (0, 0)),
            pl.BlockSpec((1, DV_A), lambda bi, h, qi: (0, 0)),
            pl.BlockSpec((1, tq, w), lambda bi, h, qi: (bi, qi, h)),
            pl.BlockSpec((1, t, w), lambda bi, h, qi: (bi, 0, h)),
            pl.BlockSpec((1, t, w), lambda bi, h, qi: (bi, 0, H_A // nh + h)),
        ],
        out_specs=pl.BlockSpec((1, tq, w), lambda bi, h, qi: (bi, qi, h)),
        scratch_shapes=[
            pltpu.VMEM((nh, tq, tq), F32),
            pltpu.VMEM((2 * nh, tq, LANES), F32), pltpu.VMEM((2 * nh, tq, LANES), F32),
            pltpu.VMEM((2 * nh, tq, LANES), F32),
        ],
        compiler_params=_cparams(("parallel", "parallel", "arbitrary")),
        name="diff_prompt",
    )(slopes, lqk, head_g, q3, kv3, kv3)


def _pieces_attention(qm, pieces):
    ss = [_nt(qm, k) + bias for k, _, bias in pieces]
    m = functools.reduce(jnp.maximum, [jnp.max(s, axis=-1, keepdims=True) for s in ss])
    es = [jnp.exp2(s - m) for s in ss]
    l = functools.reduce(jnp.add, [jnp.sum(e, axis=-1, keepdims=True) for e in es])
    o = functools.reduce(jnp.add, [jnp.dot(e.astype(BF16), v, preferred_element_type=F32)
                                   for e, (_, v, _) in zip(es, pieces)])
    return o / l


def _diff_sample_kernel(lqk_ref, hg_ref, q_ref, kc_ref, vc_ref, kn_ref, vn_ref, o_ref, *, lam_init):
    t = q_ref.shape[1]
    past = kc_ref.shape[1]
    lam = _lambda(lqk_ref[...], lam_init)
    rc = lax.broadcasted_iota(jnp.int32, (t, past), 0)
    cc = lax.broadcasted_iota(jnp.int32, (t, past), 1)
    rn = lax.broadcasted_iota(jnp.int32, (t, t), 0)
    cn = lax.broadcasted_iota(jnp.int32, (t, t), 1)
    dist_c = jnp.abs(past + rc - cc).astype(F32)
    dist_n = jnp.abs(rn - cn).astype(F32)
    ok_c = (cc // CHUNK) <= ((past + rc) // CHUNK)
    ok_n = ((past + cn) // CHUNK) <= ((past + rn) // CHUNK)
    kc = kc_ref[0].reshape(past, H_A * DV_A).astype(BF16)
    vc = vc_ref[0].reshape(past, H_A * DV_A).astype(BF16)
    for h in range(H_A):
        cols = slice(h * LANES, (h + 1) * LANES)
        bias_c = jnp.where(ok_c, (-ALIBI_SLOPES[h] * LOG2E) * dist_c, NEG)
        bias_n = jnp.where(ok_n, (-ALIBI_SLOPES[h] * LOG2E) * dist_n, NEG)
        pieces = [(kc[:, cols], vc[:, cols], bias_c),
                  (kn_ref[0, :, cols].astype(BF16), vn_ref[0, :, cols].astype(BF16), bias_n)]
        qa, qb = _split_halves(q_ref[0, :, cols])
        o = _pieces_attention(qa, pieces) - lam * _pieces_attention(qb, pieces)
        o_ref[0, :, cols] = (_rms(o, hg_ref[...]) * (1.0 - lam_init)).astype(o_ref.dtype)


def _diff_sample(q3, kn3, vn3, l, kc, vc, lqk, head_g, lam_init):
    b, t, _ = q3.shape
    past = kc.shape[2]
    w = H_A * DV_A
    new = pl.BlockSpec((1, t, w), lambda bi: (l * b + bi, 0, 0))
    cache = pl.BlockSpec((None, 1, past, H_A, DV_A), lambda bi: (l, bi, 0, 0, 0))
    return pl.pallas_call(
        functools.partial(_diff_sample_kernel, lam_init=lam_init),
        out_shape=jax.ShapeDtypeStruct((b, t, w), BF16),
        grid=(b,),
        in_specs=[
            pl.BlockSpec((4, DH_A), lambda bi: (0, 0)),
            pl.BlockSpec((1, DV_A), lambda bi: (0, 0)),
            pl.BlockSpec((1, t, w), lambda bi: (bi, 0, 0)), cache, cache, new, new,
        ],
        out_specs=pl.BlockSpec((1, t, w), lambda bi: (bi, 0, 0)),
        compiler_params=_cparams(("parallel",)),
        name="diff_sample",
    )(lqk, head_g, q3, kc, vc, kn3, vn3)


QPAIR = 2 * CHUNK
WIN = BAND + QPAIR


def _relbias_kernel(tab_ref, o_ref):
    h = pl.program_id(0)
    r = lax.broadcasted_iota(jnp.int32, (QPAIR, WIN), 0)
    j = lax.broadcasted_iota(jnp.int32, (QPAIR, WIN), 1)
    rel = jnp.clip(r - j + BAND, -REL_CLIP, REL_CLIP) + REL_CLIP
    qc = r // CHUNK
    kc = j // CHUNK - BAND_CHUNKS
    allowed = (kc <= qc) & (kc >= qc - BAND_CHUNKS)

    far = BAND - REL_CLIP
    near_rel = rel[:, far:]

    near = jnp.zeros((QPAIR, WIN - far), F32)
    for t in range(2 * REL_CLIP + 1):
        near = jnp.where(near_rel == t, tab_ref[h, t], near)
    bias = jnp.concatenate([jnp.full((QPAIR, far), tab_ref[h, 2 * REL_CLIP], F32), near], axis=1)
    o_ref[0] = jnp.where(allowed, bias * LOG2E, NEG)


def _relbias(table):
    return pl.pallas_call(
        _relbias_kernel,
        out_shape=jax.ShapeDtypeStruct((H_B, QPAIR, WIN), F32),
        grid=(H_B,),
        in_specs=[pl.BlockSpec(memory_space=pltpu.SMEM)],
        out_specs=pl.BlockSpec((1, QPAIR, WIN), lambda h: (h, 0, 0)),
        compiler_params=_cparams(("arbitrary",)),
        name="relbias",
    )(table)


BAND_TQ = 512
BAND_UNROLL = 2


def _band_prompt_kernel(bias_ref, q_ref, kp_ref, kc_ref, vp_ref, vc_ref, o_ref, kcat, vcat):
    i = pl.program_id(1)
    kcat[0:BAND_TQ, :] = kp_ref[0].astype(BF16)
    kcat[BAND_TQ:, :] = kc_ref[0].astype(BF16)
    vcat[0:BAND_TQ, :] = vp_ref[0].astype(BF16)
    vcat[BAND_TQ:, :] = vc_ref[0].astype(BF16)
    lane = lax.broadcasted_iota(jnp.int32, (QPAIR, LANES), 1)

    def pair_blocks(it, carry, *, first_block):
        starts = [pl.multiple_of((it * BAND_UNROLL + u) * QPAIR, QPAIR) for u in range(BAND_UNROLL)]
        units = [(r0, hp) for r0 in starts for hp in range(H_B // 2)]
        queries = {u: q_ref[0, pl.ds(r0, QPAIR), :] for u, r0 in enumerate(starts)}
        scores, probs, outs = {}, {}, {}

        def qk(i):
            r0, hp = units[i]
            cols = slice(hp * LANES, (hp + 1) * LANES)
            qs = jnp.concatenate(_split_halves(queries[i // (H_B // 2)][:, cols]), axis=0)
            s = _nt(qs, kcat[pl.ds(r0, WIN), cols]) + bias_ref[hp]
            if first_block:
                kpos = r0 - BAND_TQ + lax.broadcasted_iota(jnp.int32, (1, WIN), 1)
                s = s + jnp.where(kpos >= 0, 0.0, NEG)
            scores[i] = s

        def softmax(i):
            s = scores.pop(i)
            e = jnp.exp2(s - jnp.max(s, axis=-1, keepdims=True))
            probs[i] = (e.astype(BF16), jnp.sum(e, axis=-1, keepdims=True))

        def pv(i):
            r0, hp = units[i]
            e, l = probs.pop(i)
            o = jnp.dot(e, vcat[pl.ds(r0, WIN), hp * LANES:(hp + 1) * LANES], preferred_element_type=F32) / l
            outs[i] = jnp.where(lane < DH_B, o[:QPAIR], o[QPAIR:])

        n = len(units)
        for step in range(n + 2):
            if step < n:
                qk(step)
            if 0 <= step - 1 < n:
                softmax(step - 1)
            if 0 <= step - 2 < n:
                pv(step - 2)
        for u, r0 in enumerate(starts):
            row = [outs[u * (H_B // 2) + hp] for hp in range(H_B // 2)]
            o_ref[0, pl.ds(r0, QPAIR), :] = jnp.concatenate(row, axis=1).astype(o_ref.dtype)
        return carry

    trips = BAND_TQ // QPAIR // BAND_UNROLL

    @pl.when(i == 0)
    def _():
        lax.fori_loop(0, trips, functools.partial(pair_blocks, first_block=True), 0)

    @pl.when(i > 0)
    def _():
        lax.fori_loop(0, trips, functools.partial(pair_blocks, first_block=False), 0)


def _band_prompt(q3, kv3, bias):
    b, t, _ = q3.shape
    assert t % BAND_TQ == 0
    w = H_B * DH_B
    cq, ck, cv = 1, 0, 1
    prev = lambda col: pl.BlockSpec((1, BAND_TQ, w), lambda bi, i: (bi, jnp.maximum(i - 1, 0), col))
    cur = lambda col: pl.BlockSpec((1, BAND_TQ, w), lambda bi, i: (bi, i, col))
    return pl.pallas_call(
        _band_prompt_kernel,
        out_shape=jax.ShapeDtypeStruct((b, t, w), BF16),
        grid=(b, t // BAND_TQ),
        in_specs=[
            pl.BlockSpec((H_B // 2, 2 * QPAIR, WIN), lambda bi, i: (0, 0, 0)),
            cur(cq), prev(ck), cur(ck), prev(cv), cur(cv),
        ],
        out_specs=pl.BlockSpec((1, BAND_TQ, w), lambda bi, i: (bi, i, 0)),
        scratch_shapes=[pltpu.VMEM((2 * BAND_TQ, w), BF16), pltpu.VMEM((2 * BAND_TQ, w), BF16)],
        compiler_params=_cparams(("parallel", "arbitrary")),
        name="band_prompt",
    )(bias.reshape(H_B // 2, 2 * QPAIR, WIN), q3, kv3, kv3, kv3, kv3)


def _band_sample_kernel(bc_ref, bn_ref, q_ref, kc_ref, vc_ref, kn_ref, vn_ref, o_ref, *, past):
    t = q_ref.shape[1]
    nband = kc_ref.shape[1]
    lane = lax.broadcasted_iota(jnp.int32, (t, LANES), 1)
    qpos_c = past + lax.broadcasted_iota(jnp.int32, (t, nband), 0)
    kpos_c = past - nband + lax.broadcasted_iota(jnp.int32, (t, nband), 1)
    qpos_n = past + lax.broadcasted_iota(jnp.int32, (t, t), 0)
    kpos_n = past + lax.broadcasted_iota(jnp.int32, (t, t), 1)

    def allowed(qpos, kpos):
        qc, kc = qpos // CHUNK, kpos // CHUNK
        return (kpos >= 0) & (kc <= qc) & (kc >= qc - BAND_CHUNKS)

    ok_c = allowed(qpos_c, kpos_c)
    ok_n = allowed(qpos_n, kpos_n)
    for hp in range(H_B // 2):
        cols = slice(hp * LANES, (hp + 1) * LANES)
        halves = []
        for sub, qm in enumerate(_split_halves(q_ref[0, :, cols])):
            hd = 2 * hp + sub
            pieces = [(kc_ref[0, :, cols].astype(BF16), vc_ref[0, :, cols].astype(BF16),
                       jnp.where(ok_c, bc_ref[hd], NEG)),
                      (kn_ref[0, :, cols].astype(BF16), vn_ref[0, :, cols].astype(BF16),
                       jnp.where(ok_n, bn_ref[hd], NEG))]
            halves.append(_pieces_attention(qm, pieces))
        o_ref[0, :, cols] = jnp.where(lane < DH_B, halves[0], halves[1]).astype(o_ref.dtype)


def _band_sample(q3, kv3, kc, vc, bias, past):
    b, t, _ = q3.shape
    nband = kc.shape[1]
    assert nband == BAND and t <= CHUNK
    w = H_B * DH_B
    bias_c = bias[:, :t, :nband]
    bias_n = bias[:, :t, nband:nband + t]
    blk = lambda col: pl.BlockSpec((1, t, w), lambda bi: (bi, 0, col))
    cache = pl.BlockSpec((1, nband, w), lambda bi: (bi, 0, 0))
    return pl.pallas_call(
        functools.partial(_band_sample_kernel, past=past),
        out_shape=jax.ShapeDtypeStruct((b, t, w), BF16),
        grid=(b,),
        in_specs=[
            pl.BlockSpec((H_B, t, nband), lambda bi: (0, 0, 0)),
            pl.BlockSpec((H_B, t, t), lambda bi: (0, 0, 0)),
            blk(1), cache, cache, blk(0), blk(1),
        ],
        out_specs=pl.BlockSpec((1, t, w), lambda bi: (bi, 0, 0)),
        compiler_params=_cparams(("parallel",)),
        name="band_sample",
    )(bias_c, bias_n, q3, kc, vc, kv3, kv3)


CONV_PAD = 8
MLSTM_L = 256
MLSTM_NB = 2


def _mlstm_kernel(cqk_ref, cv_ref, co_ref, cif_ref, conv0_ref, cw_ref, cb_ref, gb_ref, hg_ref,
                  c0_ref, n0_ref, m0_ref, hs_ref, c_ref, n_ref, m_ref, xext, *, L):
    @pl.when(pl.program_id(1) == 0)
    def _():
        xext[:, 0:CONV_PAD, :] = conv0_ref[...]
        c_ref[...] = c0_ref[...]
        n_ref[...] = n0_ref[...]
        m_ref[...] = m0_ref[...]

    lane = lax.broadcasted_iota(jnp.int32, (L, LANES), 1)
    tr = lax.broadcasted_iota(jnp.int32, (L, L), 0)
    tc = lax.broadcasted_iota(jnp.int32, (L, L), 1)
    causal = tr >= tc
    sr = lax.broadcasted_iota(jnp.int32, (8, LANES), 0)
    sc = lax.broadcasted_iota(jnp.int32, (8, LANES), 1)
    sel = jnp.where(sr < H_C, jnp.where(sc == sr, 1.0, jnp.where(sc == sr + H_C, -1.0, 0.0)), 0.0)
    er = lax.broadcasted_iota(jnp.int32, (DH_C, DH_C), 0)
    ec = lax.broadcasted_iota(jnp.int32, (DH_C, DH_C), 1)
    eye = jnp.where(er == ec, 1.0, 0.0).astype(BF16)
    for bb in range(cqk_ref.shape[0]):
        _mlstm_chunk(bb, cqk_ref, cv_ref, co_ref, cif_ref, cw_ref, cb_ref, gb_ref, hg_ref,
                     hs_ref, c_ref, n_ref, m_ref, xext, L, lane, causal, sel, eye)


def _mlstm_chunk(bb, cqk_ref, cv_ref, co_ref, cif_ref, cw_ref, cb_ref, gb_ref, hg_ref,
                 hs_ref, c_ref, n_ref, m_ref, xext, L, lane, causal, sel, eye):
    xext[bb, CONV_PAD:CONV_PAD + L, :] = cqk_ref[bb]
    base = CONV_PAD - (CONV_W - 1)
    u = 0.0
    for j in range(CONV_W):
        u = u + xext[bb, base + j:base + j + L, :] * cw_ref[j:j + 1, :]
    u = cb_ref[...] + u
    u = u * jax.nn.sigmoid(u)
    xext[bb, 0:CONV_PAD, :] = xext[bb, L:L + CONV_PAD, :]

    z = cif_ref[bb] + gb_ref[...]
    lf = jnp.minimum(z, 0.0) - jnp.log1p(jnp.exp(-jnp.abs(z)))
    b_all = jnp.dot(causal.astype(F32), lf, preferred_element_type=F32,
                    precision=lax.Precision.HIGHEST)
    rt = lax.dot_general(sel, jnp.where(lane < H_C, z, b_all), NT_DIMS,
                         preferred_element_type=F32, precision=lax.Precision.HIGHEST)

    for h in range(H_C):
        cols = slice(h * DH_C, (h + 1) * DH_C)
        q = u[:, h * DH_C:(h + 1) * DH_C]
        k = u[:, D_C + h * DH_C:D_C + (h + 1) * DH_C] * (DH_C ** -0.5)
        v = cv_ref[bb, :, cols]
        qb, kb_, vb_ = q.astype(BF16), k.astype(BF16), v.astype(BF16)
        b_col = b_all[:, H_C + h:H_C + h + 1]
        li_col = z[:, h:h + 1]
        m_prev = m_ref[bb, h:h + 1, 0:1]
        cs = c_ref[bb, h]
        ns = n_ref[bb, h:h + 1, :]

        dmat = jnp.where(causal, b_col + rt[h:h + 1, :], NEG)
        inter = b_col + m_prev
        m_t = jnp.maximum(inter, jnp.max(dmat, axis=-1, keepdims=True))
        w_intra = jnp.exp(dmat - m_t)
        w_inter = jnp.exp(inter - m_t)
        a = w_intra * _nt(qb, kb_)
        num = (jnp.dot(a.astype(BF16), vb_, preferred_element_type=F32)
               + w_inter * _nt(qb, cs.astype(BF16)))
        den = jnp.sum(a, axis=-1, keepdims=True) + w_inter * jnp.sum(q * ns, axis=-1, keepdims=True)
        hh = num / jnp.maximum(jnp.abs(den), jnp.exp(-m_t))

        b_last = b_col[L - 1:L, :]
        g = b_last - b_col + li_col
        m_new = jnp.maximum(b_last + m_prev, jnp.max(g, axis=0, keepdims=True))
        ws = jnp.exp(g - m_new)
        decay = jnp.exp(b_last + m_prev - m_new)
        vwt = _nt(eye, (ws * v).astype(BF16)).astype(BF16)
        c_ref[bb, h] = decay * cs + jnp.dot(vwt, kb_, preferred_element_type=F32)
        n_ref[bb, h:h + 1, :] = decay * ns + jnp.sum(ws * k, axis=0, keepdims=True)
        m_ref[bb, h:h + 1, :] = jnp.broadcast_to(m_new, (1, LANES))

        gate = jax.nn.sigmoid(co_ref[bb, :, cols])
        hs_ref[bb, :, cols] = (_rms(hh, hg_ref[...]) * gate).astype(hs_ref.dtype)


def _mlstm(cqk3, cvo3, cif3, conv0, conv_w, conv_b, gate_b, head_g, c0, n0, m0, L):
    b, t, _ = cqk3.shape
    nc = t // L
    nb = MLSTM_NB
    assert b % nb == 0
    blk = lambda col, w: pl.BlockSpec((nb, L, w), lambda bi, c: (bi, c, col))
    const = lambda shape: pl.BlockSpec(shape, lambda bi, c: (0,) * len(shape))
    per_b = lambda shape: pl.BlockSpec((nb,) + shape, lambda bi, c: (bi,) + (0,) * len(shape))
    return pl.pallas_call(
        functools.partial(_mlstm_kernel, L=L),
        out_shape=(jax.ShapeDtypeStruct((b, t, D_C), BF16),
                   jax.ShapeDtypeStruct((b, H_C, DH_C, DH_C), F32),
                   jax.ShapeDtypeStruct((b, H_C, DH_C), F32),
                   jax.ShapeDtypeStruct((b, H_C, LANES), F32)),
        grid=(b // nb, nc),
        in_specs=[
            blk(0, 2 * D_C), blk(0, D_C), blk(1, D_C), blk(0, LANES),
            per_b((CONV_PAD, 2 * D_C)), const((CONV_W, 2 * D_C)), const((1, 2 * D_C)),
            const((1, LANES)), const((1, DH_C)),
            per_b((H_C, DH_C, DH_C)), per_b((H_C, DH_C)), per_b((H_C, LANES)),
        ],
        out_specs=(pl.BlockSpec((nb, L, D_C), lambda bi, c: (bi, c, 0)),
                   per_b((H_C, DH_C, DH_C)), per_b((H_C, DH_C)), per_b((H_C, LANES))),
        scratch_shapes=[pltpu.VMEM((nb, CONV_PAD + L, 2 * D_C), F32)],
        compiler_params=_cparams(("parallel", "arbitrary")),
        name="mlstm",
    )(cqk3, cvo3, cvo3, cif3, conv0, conv_w, conv_b, gate_b, head_g, c0, n0, m0)


def _mix_kernel(oa_ref, ob_ref, oc_ref, ga_ref, gb_ref, gc_ref, x_ref, wa_ref, wb_ref, wc_ref, wo_ref, o_ref):
    mixed = (ga_ref[...] * jnp.dot(oa_ref[...], wa_ref[...], preferred_element_type=F32)
             + gb_ref[...] * jnp.dot(ob_ref[...], wb_ref[...], preferred_element_type=F32)
             + gc_ref[...] * jnp.dot(oc_ref[...], wc_ref[...], preferred_element_type=F32))
    o_ref[...] = x_ref[...] + jnp.dot(mixed.astype(BF16), wo_ref[...], preferred_element_type=F32)


def _mix(oa, ob, oc, gates, x, l, wa, wb, wc, wo):
    m, d = x.shape
    tm = min(m, 512)
    row = lambda w, col=0: pl.BlockSpec((tm, w), lambda i: (i, col))
    full = lambda a: pl.BlockSpec((None,) + a.shape[1:], lambda i: (l, 0, 0))
    g0 = 0
    return pl.pallas_call(
        _mix_kernel,
        out_shape=jax.ShapeDtypeStruct((m, d), F32),
        grid=(m // tm,),
        in_specs=[row(oa.shape[1]), row(ob.shape[1]), row(oc.shape[1]),
                  row(d, g0), row(d, g0 + 1), row(d, g0 + 2), row(d),
                  full(wa), full(wb), full(wc), full(wo)],
        out_specs=row(d),
        compiler_params=_cparams(("parallel",)),
        name="mix",
    )(oa, ob, oc, gates, gates, gates, x, wa, wb, wc, wo)


def _cross_kernel(x_ref, g_ref, wq_ref, mk_ref, mv_ref, wo_ref, o_ref):
    x = x_ref[0]
    h = _rms(x, g_ref[...]).astype(BF16)
    q = (jnp.dot(h, wq_ref[...], preferred_element_type=F32) * (DH_M ** -0.5 * LOG2E)).astype(BF16)
    nm = mk_ref.shape[1]
    mk = mk_ref[0].reshape(nm, H_M * DH_M).astype(BF16)
    mv = mv_ref[0].reshape(nm, H_M * DH_M).astype(BF16)
    head_cols = [slice(hd * DH_M, (hd + 1) * DH_M) for hd in range(H_M)]
    scores, probs, outs = {}, {}, {}
    for step in range(H_M + 2):
        if step < H_M:
            scores[step] = _nt(q[:, head_cols[step]], mk[:, head_cols[step]])
        if 0 <= step - 1 < H_M:
            s = scores.pop(step - 1)
            e = jnp.exp2(s - jnp.max(s, axis=-1, keepdims=True))
            probs[step - 1] = (e.astype(BF16), jnp.sum(e, axis=-1, keepdims=True))
        if 0 <= step - 2 < H_M:
            e, l = probs.pop(step - 2)
            o = jnp.dot(e, mv[:, head_cols[step - 2]], preferred_element_type=F32) / l
            outs[step - 2] = o.astype(BF16)
    o = jnp.concatenate([outs[hd] for hd in range(H_M)], axis=1)
    o_ref[0] = x + jnp.dot(o, wo_ref[...], preferred_element_type=F32)


def _cross(x3, g, l, wq, mk, mv, wo):
    b, t, d = x3.shape
    tm = min(t, 512)
    nm = mk.shape[2]
    full = lambda a: pl.BlockSpec((None,) + a.shape[1:], lambda bi, i: (l, 0, 0))
    mem = pl.BlockSpec((None, 1, nm, H_M, DH_M), lambda bi, i: (l, bi, 0, 0, 0))
    return pl.pallas_call(
        _cross_kernel,
        out_shape=jax.ShapeDtypeStruct((b, t, d), F32),
        grid=(b, t // tm),
        in_specs=[pl.BlockSpec((1, tm, d), lambda bi, i: (bi, i, 0)),
                  pl.BlockSpec(g.shape, lambda bi, i: (0, 0)), full(wq), mem, mem, full(wo)],
        out_specs=pl.BlockSpec((1, tm, d), lambda bi, i: (bi, i, 0)),
        compiler_params=_cparams(("parallel", "parallel")),
        name="cross",
    )(x3, g, wq, mk, mv, wo)


FFN_TF = 1408
FFN_TM = 512


def _ffn_kernel(x_ref, g_ref, wg_ref, wu_ref, wd_ref, gf_ref, o_ref, *, final_norm):
    x = x_ref[...]
    h = _rms(x, g_ref[...]).astype(BF16)
    chunks = [slice(c, c + FFN_TF) for c in range(0, D_FF, FFN_TF)]
    pre = [(jnp.dot(h, wg_ref[:, c], preferred_element_type=F32),
            jnp.dot(h, wu_ref[:, c], preferred_element_type=F32)) for c in chunks]
    y = x
    for c, (gate, up) in zip(chunks, pre):
        a = (gate * jax.nn.sigmoid(gate) * up).astype(BF16)
        y = y + jnp.dot(a, wd_ref[c, :], preferred_element_type=F32)
    if final_norm:
        y = _rms(y, gf_ref[...])
    o_ref[...] = y


def _ffn(x, g, l, wg, wu, wd, g_final, final_norm):
    m, d = x.shape
    tm = min(m, FFN_TM)
    resident = lambda a: pl.BlockSpec((None,) + a.shape[1:], lambda i: (l, 0, 0),
                                      pipeline_mode=pl.Buffered(1))
    return pl.pallas_call(
        functools.partial(_ffn_kernel, final_norm=final_norm),
        out_shape=jax.ShapeDtypeStruct((m, d), F32),
        grid=(m // tm,),
        in_specs=[
            pl.BlockSpec((tm, d), lambda i: (i, 0)),
            pl.BlockSpec((1, d), lambda i: (0, 0)),
            resident(wg), resident(wu), resident(wd),
            pl.BlockSpec((1, d), lambda i: (0, 0)),
        ],
        out_specs=pl.BlockSpec((tm, d), lambda i: (i, 0)),
        compiler_params=_cparams(("parallel",)),
        name="ffn",
    )(x, g, wg, wu, wd, g_final)


def _layer(x3, l, w, mem_k, mem_v, cache, final_norm, akv_all):
    b, t, d = x3.shape
    x2 = x3.reshape(b * t, d)
    row = lambda a: a.reshape(1, -1)
    lam_init = 0.8 - 0.6 * math.exp(-0.3 * l)

    gates, cqk, cif, q, ak_all, av_all, akv, bkv, cvo = _in_proj(
        x2, row(w['g_mix']), l, w['w_in'], w['w_if'], *akv_all, (b, t))
    q3 = q.reshape(b, t, SEG)
    bkv3 = bkv.reshape(b, t, SEG)
    cqk3 = cqk.reshape(b, t, 2 * D_C)
    keep = t if cache is not None else min(BAND, t)
    b_k = bkv3[:, t - keep:, :HALF].reshape(b, keep, H_B, DH_B)
    b_v = bkv3[:, t - keep:, HALF:].reshape(b, keep, H_B, DH_B)

    gate_b = jnp.concatenate([w['b_i'], w['b_f'], jnp.zeros((LANES - 2 * H_C,), F32)]).reshape(1, LANES)
    if cache is None:
        oa = _diff_prompt(q3, akv.reshape(b, t, SEG), w['lqk'], row(w['a_head_g']), lam_init)
        ob = _band_prompt(q3, bkv3, w['relbias'])
        conv0 = jnp.zeros((b, CONV_PAD, 2 * D_C), F32)
        c0 = jnp.zeros((b, H_C, DH_C, DH_C), F32)
        n0 = jnp.zeros((b, H_C, DH_C), F32)
        m0 = jnp.zeros((b, H_C, LANES), F32)
        L = min(t, MLSTM_L)
    else:
        past = cache['a_k'].shape[2]
        ak3 = ak_all.reshape(DEPTH * b, t, HALF)
        av3 = av_all.reshape(DEPTH * b, t, HALF)
        oa = _diff_sample(q3, ak3, av3, l, cache['a_k'], cache['a_v'],
                          w['lqk'], row(w['a_head_g']), lam_init)
        nband = cache['b_k'].shape[1]
        ob = _band_sample(q3, bkv3, cache['b_k'].reshape(b, nband, -1), cache['b_v'].reshape(b, nband, -1),
                          w['relbias'], past)
        conv0 = jnp.pad(cache['conv'], ((0, 0), (CONV_PAD - (CONV_W - 1), 0), (0, 0)))
        c0, n0 = cache['C'], cache['n']
        m0 = jnp.broadcast_to(cache['m'][:, :, None], (b, H_C, LANES))
        L = t
    oc, c_new, n_new, m_new = _mlstm(cqk3, cvo.reshape(b, t, SEG), cif.reshape(b, t, LANES), conv0,
                                     w['conv_w'], row(w['conv_b']), gate_b, row(w['c_head_g']), c0, n0, m0, L)
    assert t >= CONV_W - 1
    conv_new = cqk3[:, t - (CONV_W - 1):]

    x2 = _mix(oa.reshape(b * t, -1), ob.reshape(b * t, -1), oc.reshape(b * t, -1), gates, x2, l,
              w['w_up_a'], w['w_up_b'], w['w_up_c'], w['w_o'])
    x3 = _cross(x2.reshape(b, t, d), row(w['g_cross']), l, w['w_mq'], mem_k, mem_v, w['w_mo'])
    x2 = _ffn(x3.reshape(b * t, d), row(w['g_ffn']), l, w['w_ff_g'], w['w_ff_u'], w['w_ff_d'],
              row(w['g_final']), final_norm)
    return x2.reshape(b, t, d), (ak_all, av_all), (b_k, b_v, c_new, n_new, m_new[:, :, 0], conv_new)


def kernel(x_prompt, x_sample, cache_a_k, cache_a_v, cache_b_k, cache_b_v, state_c_C, state_c_n, state_c_m, state_c_conv, cache_mem_k, cache_mem_v, mem_prompt, g_mix, w_in, a_lq1, a_lk1, a_lq2, a_lk2, a_head_g, b_rel, c_conv_w, c_conv_b, c_b_i, c_b_f, c_head_g, w_up_a, w_up_b, w_up_c, w_o, g_cross, w_mq, w_mk, w_mv, w_mo, g_ffn, w_ff_g, w_ff_u, w_ff_d, g_final):
    xp, xs = x_prompt, x_sample
    bp = x_prompt.shape[0]
    new_p = [[] for _ in range(6)]
    new_s = [[] for _ in range(6)]
    akv_p = akv_s = (None, None)
    bf = lambda a: a.astype(BF16)
    wp, wif = _prep_w_in(w_in)
    wts = dict(w_up_a=bf(w_up_a), w_up_b=bf(w_up_b), w_up_c=bf(w_up_c), w_o=bf(w_o), w_mq=bf(w_mq),
               w_mo=bf(w_mo), w_ff_g=bf(w_ff_g), w_ff_u=bf(w_ff_u), w_ff_d=bf(w_ff_d))
    mem_k = _mem_proj(mem_prompt, bf(w_mk))
    mem_v = _mem_proj(mem_prompt, bf(w_mv))
    for l in range(DEPTH):
        w = dict(g_mix=g_mix[l], w_in=wp, w_if=wif,
                 lqk=jnp.stack([a_lq1[l], a_lk1[l], a_lq2[l], a_lk2[l]]),
                 a_head_g=a_head_g[l], relbias=_relbias(b_rel[l]), conv_w=c_conv_w[l], conv_b=c_conv_b[l],
                 b_i=c_b_i[l], b_f=c_b_f[l], c_head_g=c_head_g[l],
                 g_cross=g_cross[l], g_ffn=g_ffn[l], g_final=g_final, **wts)
        last = l == DEPTH - 1
        xp, akv_p, st_p = _layer(xp, l, w, mem_k, mem_v, None, last, akv_p)
        cache = dict(a_k=cache_a_k, a_v=cache_a_v, b_k=cache_b_k[l], b_v=cache_b_v[l],
                     C=state_c_C[l], n=state_c_n[l], m=state_c_m[l], conv=state_c_conv[l])
        xs, akv_s, st_s = _layer(xs, l, w, cache_mem_k, cache_mem_v, cache, last, akv_s)
        for lst, a in zip(new_p, st_p):
            lst.append(a)
        for lst, a in zip(new_s, st_s):
            lst.append(a)
    heads = lambda a, b, t: a.reshape(DEPTH, b, t, H_A, DV_A)
    outs_p = ([heads(a, bp, x_prompt.shape[1]) for a in akv_p] + [jnp.stack(a, 0) for a in new_p]
              + [mem_k, mem_v])
    outs_s = [heads(a, xs.shape[0], xs.shape[1]) for a in akv_s] + [jnp.stack(a, 0) for a in new_s]
    return (xp, xs) + tuple(outs_p) + tuple(outs_s)
```

```python
import functools
import math

import jax
import jax.numpy as jnp
from jax import lax
from jax.experimental import pallas as pl
from jax.experimental.pallas import tpu as pltpu

F32 = jnp.float32
BF16 = jnp.bfloat16

D_MODEL = 1024
DEPTH = 2
CHUNK = 64
EPS = 1e-6
NEG = -1e30
LOG2E = math.log2(math.e)
H_A = 4
DH_A = 64
DV_A = 2 * DH_A
H_B = 8
DH_B = 64
BAND_CHUNKS = 8
BAND = BAND_CHUNKS * CHUNK
REL_CLIP = 128
H_C = 4
DH_C = 128
D_C = H_C * DH_C
CONV_W = 4
H_M = 4
DH_M = D_MODEL // H_M
D_FF = -(-8 * D_MODEL // (3 * 256)) * 256

LANES = 128
VMEM_LIMIT = 48 * 1024 * 1024

SEG = 1024
HALF = SEG // 2
GATE_TILES = 3
N_TILES = 8
IN_SIZES = (H_A * DH_A,) * 4 + (H_A * DV_A,) + (H_B * DH_B,) * 3 + (2 * D_C, D_C, D_C, 2 * H_C, 3 * D_MODEL)

ALIBI_SLOPES = tuple(2.0 ** (-8.0 * (i + 1) / H_A) for i in range(H_A))

NT_DIMS = (((1,), (1,)), ((), ()))


def _cparams(sem):
    return pltpu.CompilerParams(dimension_semantics=sem, vmem_limit_bytes=VMEM_LIMIT)


def _rms(x, g):
    ms = jnp.mean(x * x, axis=-1, keepdims=True)
    return x * lax.rsqrt(ms + EPS) * g


def _nt(a, b):
    return lax.dot_general(a, b, NT_DIMS, preferred_element_type=F32)


def _prep_w_in_kernel(w_ref, o_ref, oif_ref):
    offs = [0]
    for size in IN_SIZES:
        offs.append(offs[-1] + size)
    seg = lambda i: w_ref[:, offs[i]:offs[i + 1]]
    a_q1, a_q2, a_k1, a_k2, a_v, b_q, b_k, b_v, c_qk, c_v, c_o = (seg(i) for i in range(11))
    g = seg(12)

    def put(col, val):
        o_ref[:, col:col + val.shape[1]] = val.astype(BF16)

    put(0, g)
    put(GATE_TILES * SEG, c_qk)
    for h in range(H_A):
        hs = slice(h * DH_A, (h + 1) * DH_A)
        base = h * 2 * DH_A
        put(4 * SEG + base, a_q1[:, hs] * (DH_A ** -0.5))
        put(4 * SEG + base + DH_A, a_q2[:, hs] * (DH_A ** -0.5))
        put(5 * SEG + base, a_k1[:, hs])
        put(5 * SEG + base + DH_A, a_k2[:, hs])
    put(4 * SEG + HALF, b_q * (DH_B ** -0.5))
    put(5 * SEG + HALF, a_v)
    put(6 * SEG, b_k)
    put(6 * SEG + HALF, b_v)
    put(7 * SEG, c_v)
    put(7 * SEG + HALF, c_o)
    tile = w_ref[:, offs[11]:offs[11] + LANES]
    lane = lax.broadcasted_iota(jnp.int32, tile.shape, 1)
    oif_ref[...] = jnp.where(lane < 2 * H_C, tile, 0.0).astype(BF16)


def _prep_w_in(w):
    depth, k, n = w.shape
    tk = 256
    n_out = N_TILES * SEG
    return pl.pallas_call(
        _prep_w_in_kernel,
        out_shape=(jax.ShapeDtypeStruct((depth, k, n_out), BF16),
                   jax.ShapeDtypeStruct((depth, k, LANES), BF16)),
        grid=(depth, k // tk),
        in_specs=[pl.BlockSpec((None, tk, n), lambda l, i: (l, i, 0))],
        out_specs=(pl.BlockSpec((None, tk, n_out), lambda l, i: (l, i, 0)),
                   pl.BlockSpec((None, tk, LANES), lambda l, i: (l, i, 0))),
        compiler_params=_cparams(("parallel", "parallel")),
        name="prep_w_in",
    )(w)


IN_PROJ_TM = 256


def _in_proj_kernel(*refs, aliased, heads5d):
    x_ref, g_ref, w_ref, wif_ref = refs[:4]
    og_ref, ocqk_ref, ocif_ref, oq_ref, oak_ref, oav_ref, oakv_ref, obkv_ref, ocvo_ref = refs[6 if aliased else 4:]
    h = _rms(x_ref[...], g_ref[...]).astype(BF16)
    cols = lambda j, lo=0, hi=SEG: jnp.dot(h, w_ref[:, j * SEG + lo:j * SEG + hi], preferred_element_type=F32)

    ocif_ref[...] = jnp.dot(h, wif_ref[...], preferred_element_type=F32)
    for j in range(GATE_TILES):
        og_ref[:, j * SEG:(j + 1) * SEG] = jax.nn.sigmoid(cols(j)).astype(og_ref.dtype)
    ocqk_ref[...] = cols(GATE_TILES)
    oq_ref[...] = (cols(4) * LOG2E).astype(oq_ref.dtype)
    for out_ref, off in ((oak_ref, 0), (oav_ref, HALF)):
        a = cols(5, off, off + HALF)
        oakv_ref[:, off:off + HALF] = a.astype(oakv_ref.dtype)
        if heads5d:
            out_ref[0, 0] = a.reshape(a.shape[0], H_A, DV_A)
        else:
            out_ref[0] = a
    obkv_ref[...] = cols(6)
    ocvo_ref[...] = cols(7)


def _in_proj(x, g, l, wp, wif, ak_all, av_all, bt):
    m, k = x.shape
    tm = min(m, IN_PROJ_TM)
    aliased = ak_all is not None
    b, t = bt
    heads5d = t % tm == 0
    if heads5d:
        stacked = jax.ShapeDtypeStruct((DEPTH, b, t, H_A, DV_A), F32)
        layer_row = pl.BlockSpec((1, 1, tm, H_A, DV_A), lambda i: (l, i // (t // tm), i % (t // tm), 0, 0))
    else:
        stacked = jax.ShapeDtypeStruct((DEPTH, m, HALF), F32)
        layer_row = pl.BlockSpec((1, tm, HALF), lambda i: (l, i, 0))
    resident = lambda a: pl.BlockSpec((None,) + a.shape[1:], lambda i: (l, 0, 0), pipeline_mode=pl.Buffered(1))
    in_specs = [pl.BlockSpec((tm, k), lambda i: (i, 0)), pl.BlockSpec((1, k), lambda i: (0, 0)),
                resident(wp), resident(wif)]
    args = [x, g, wp, wif]
    if aliased:
        in_specs += [pl.BlockSpec(memory_space=pl.ANY)] * 2
        args += [ak_all, av_all]
    row = lambda w: pl.BlockSpec((tm, w), lambda i: (i, 0))
    sds = jax.ShapeDtypeStruct
    return pl.pallas_call(
        functools.partial(_in_proj_kernel, aliased=aliased, heads5d=heads5d),
        out_shape=(sds((m, GATE_TILES * SEG), BF16), sds((m, SEG), F32), sds((m, LANES), F32),
                   sds((m, SEG), BF16), stacked, stacked, sds((m, SEG), BF16),
                   sds((m, SEG), F32), sds((m, SEG), F32)),
        grid=(m // tm,),
        in_specs=in_specs,
        out_specs=(row(GATE_TILES * SEG), row(SEG), row(LANES), row(SEG), layer_row, layer_row, row(SEG),
                   row(SEG), row(SEG)),
        input_output_aliases={4: 4, 5: 5} if aliased else {},
        compiler_params=_cparams(("parallel",)),
        name="in_proj",
    )(*args)


def _mem_proj_kernel(x_ref, w_ref, o_ref):
    y = jnp.dot(x_ref[0].astype(BF16), w_ref[...], preferred_element_type=F32)
    o_ref[...] = y.reshape(o_ref.shape)


def _mem_proj(mem, w):
    b, nm, k = mem.shape
    depth, _, n = w.shape
    return pl.pallas_call(
        _mem_proj_kernel,
        out_shape=jax.ShapeDtypeStruct((depth, b, nm, H_M, DH_M), F32),
        grid=(depth, b),
        in_specs=[pl.BlockSpec((1, nm, k), lambda l, bi: (bi, 0, 0)),
                  pl.BlockSpec((None, k, n), lambda l, bi: (l, 0, 0))],
        out_specs=pl.BlockSpec((None, None, nm, H_M, DH_M), lambda l, bi: (l, bi, 0, 0, 0)),
        compiler_params=_cparams(("parallel", "parallel")),
        name="mem_proj",
    )(mem, w)


def _online_updates(jobs):
    scores, staged = {}, {}

    def softmax(i):
        s, (acc, m, l) = scores.pop(i), jobs[i][2]
        slabs = [s[:, j * LANES:(j + 1) * LANES] for j in range(s.shape[1] // LANES)]
        m_old = m[...]
        m_new = jnp.maximum(m_old, jnp.max(functools.reduce(jnp.maximum, slabs), axis=-1, keepdims=True))
        alpha = jnp.exp2(m_old - m_new)
        ps = [jnp.exp2(c - m_new) for c in slabs]
        l[...] = alpha * l[...] + jnp.sum(functools.reduce(jnp.add, ps), axis=-1, keepdims=True)
        m[...] = m_new
        staged[i] = (alpha, jnp.concatenate(ps, axis=1).astype(BF16))

    def pv(i):
        (alpha, p), (_, v, (acc, m, l)) = staged.pop(i), jobs[i]
        acc[...] = alpha * acc[...] + jnp.dot(p, v, preferred_element_type=F32)

    n = len(jobs)
    for step in range(n + 2):
        if step < n:
            scores[step] = jobs[step][0]()
        if 0 <= step - 1 < n:
            softmax(step - 1)
        if 0 <= step - 2 < n:
            pv(step - 2)


def _split_halves(q):
    lane = lax.broadcasted_iota(jnp.int32, q.shape, 1)
    lo = jnp.where(lane < DH_A, q, 0.0).astype(BF16)
    hi = jnp.where(lane >= DH_A, q, 0.0).astype(BF16)
    return lo, hi


def _lambda(lqk, lam_init):
    e1 = jnp.exp(jnp.sum(lqk[0:1, :] * lqk[1:2, :], axis=-1, keepdims=True))
    e2 = jnp.exp(jnp.sum(lqk[2:3, :] * lqk[3:4, :], axis=-1, keepdims=True))
    return e1 - e2 + lam_init


DIFF_HEADS = 4


def _diff_prompt_kernel(slopes_ref, lqk_ref, hg_ref, q_ref, k_ref, v_ref, o_ref,
                        diag_bias, acc, m, l, *, tq, nh, lam_init):
    hg = pl.program_id(1)
    qi = pl.program_id(2)
    slopes = [slopes_ref[hg * nh + i] * LOG2E for i in range(nh)]
    q0 = pl.multiple_of(qi * tq, tq)
    head_cols = [slice(i * LANES, (i + 1) * LANES) for i in range(nh)]

    @pl.when(qi == 0)
    def _():
        r = lax.broadcasted_iota(jnp.int32, (tq, tq), 0)
        c = lax.broadcasted_iota(jnp.int32, (tq, tq), 1)
        rel = (r - jnp.abs(r - c)).astype(F32)
        allowed = (c // CHUNK) <= (r // CHUNK)
        for i in range(nh):
            diag_bias[i] = jnp.where(allowed, slopes[i] * rel, NEG)

    m[...] = jnp.full(m.shape, NEG, F32)
    l[...] = jnp.zeros(l.shape, F32)
    acc[...] = jnp.zeros(acc.shape, F32)
    chains = []
    for i in range(nh):
        for j, qm in enumerate(_split_halves(q_ref[0, :, head_cols[i]])):
            c = 2 * i + j
            chains.append((qm, i, (acc.at[c], m.at[c], l.at[c])))

    score = lambda qm, k, bias: _nt(qm, k) + bias

    def full_tiles(k_starts):
        jobs = []
        for k0 in k_starts:
            kpos = (k0 - q0) + lax.broadcasted_iota(jnp.int32, (1, tq), 1)
            kposf = kpos.astype(F32)
            for qm, i, st in chains:
                k = k_ref[0, pl.ds(k0, tq), head_cols[i]]
                v = v_ref[0, pl.ds(k0, tq), head_cols[i]]
                jobs.append((functools.partial(score, qm, k, slopes[i] * kposf), v, st))
        _online_updates(jobs)

    def tile_pair(it, carry):
        k0 = pl.multiple_of(it * (2 * tq), 2 * tq)
        full_tiles([k0, pl.multiple_of(k0 + tq, tq)])
        return carry

    lax.fori_loop(0, qi // 2, tile_pair, 0)

    @pl.when(qi % 2 == 1)
    def _():
        full_tiles([pl.multiple_of((qi - 1) * tq, tq)])

    half = tq // 2
    jobs = []
    for rows, nk in ((slice(0, half), half), (slice(half, tq), tq)):
        for qm, i, st in chains:
            k = k_ref[0, pl.ds(q0, nk), head_cols[i]]
            v = v_ref[0, pl.ds(q0, nk), head_cols[i]]
            jobs.append((functools.partial(score, qm[rows], k, diag_bias[i, rows, 0:nk]), v,
                         tuple(ref.at[rows] for ref in st)))
    _online_updates(jobs)

    lam = _lambda(lqk_ref[...], lam_init)
    for i in range(nh):
        o = acc[2 * i] / l[2 * i] - lam * (acc[2 * i + 1] / l[2 * i + 1])
        o_ref[0, :, head_cols[i]] = (_rms(o, hg_ref[...]) * (1.0 - lam_init)).astype(o_ref.dtype)


def _diff_prompt(q3, kv3, lqk, head_g, lam_init):
    b, t, _ = q3.shape
    tq = min(t, 512)
    nh = DIFF_HEADS
    w = nh * LANES
    slopes = jnp.asarray(ALIBI_SLOPES, F32)
    return pl.pallas_call(
        functools.partial(_diff_prompt_kernel, tq=tq, nh=nh, lam_init=lam_init),
        out_shape=jax.ShapeDtypeStruct((b, t, H_A * DV_A), BF16),
        grid=(b, H_A // nh, t // tq),
        in_specs=[
            pl.BlockSpec(memory_space=pltpu.SMEM),
            pl.BlockSpec((4, DH_A), lambda bi, h, qi: (0, 0)),
            pl.BlockSpec((1, DV_A), lambda bi, h, qi: (0, 0)),
            pl.BlockSpec((1, tq, w), lambda bi, h, qi: (bi, qi, h)),
            pl.BlockSpec((1, t, w), lambda bi, h, qi: (bi, 0, h)),
            pl.BlockSpec((1, t, w), lambda bi, h, qi: (bi, 0, H_A // nh + h)),
        ],
        out_specs=pl.BlockSpec((1, tq, w), lambda bi, h, qi: (bi, qi, h)),
        scratch_shapes=[
            pltpu.VMEM((nh, tq, tq), F32),
            pltpu.VMEM((2 * nh, tq, LANES), F32), pltpu.VMEM((2 * nh, tq, LANES), F32),
            pltpu.VMEM((2 * nh, tq, LANES), F32),
        ],
        compiler_params=_cparams(("parallel", "parallel", "arbitrary")),
        name="diff_prompt",
    )(slopes, lqk, head_g, q3, kv3, kv3)


def _pieces_attention(qm, pieces):
    ss = [_nt(qm, k) + bias for k, _, bias in pieces]
    m = functools.reduce(jnp.maximum, [jnp.max(s, axis=-1, keepdims=True) for s in ss])
    es = [jnp.exp2(s - m) for s in ss]
    l = functools.reduce(jnp.add, [jnp.sum(e, axis=-1, keepdims=True) for e in es])
    o = functools.reduce(jnp.add, [jnp.dot(e.astype(BF16), v, preferred_element_type=F32)
                                   for e, (_, v, _) in zip(es, pieces)])
    return o / l


def _diff_sample_kernel(lqk_ref, hg_ref, q_ref, kc_ref, vc_ref, kn_ref, vn_ref, o_ref, *, lam_init):
    t = q_ref.shape[1]
    past = kc_ref.shape[1]
    lam = _lambda(lqk_ref[...], lam_init)
    rc = lax.broadcasted_iota(jnp.int32, (t, past), 0)
    cc = lax.broadcasted_iota(jnp.int32, (t, past), 1)
    rn = lax.broadcasted_iota(jnp.int32, (t, t), 0)
    cn = lax.broadcasted_iota(jnp.int32, (t, t), 1)
    dist_c = jnp.abs(past + rc - cc).astype(F32)
    dist_n = jnp.abs(rn - cn).astype(F32)
    ok_c = (cc // CHUNK) <= ((past + rc) // CHUNK)
    ok_n = ((past + cn) // CHUNK) <= ((past + rn) // CHUNK)
    kc = kc_ref[0].reshape(past, H_A * DV_A).astype(BF16)
    vc = vc_ref[0].reshape(past, H_A * DV_A).astype(BF16)
    for h in range(H_A):
        cols = slice(h * LANES, (h + 1) * LANES)
        bias_c = jnp.where(ok_c, (-ALIBI_SLOPES[h] * LOG2E) * dist_c, NEG)
        bias_n = jnp.where(ok_n, (-ALIBI_SLOPES[h] * LOG2E) * dist_n, NEG)
        pieces = [(kc[:, cols], vc[:, cols], bias_c),
                  (kn_ref[0, :, cols].astype(BF16), vn_ref[0, :, cols].astype(BF16), bias_n)]
        qa, qb = _split_halves(q_ref[0, :, cols])
        o = _pieces_attention(qa, pieces) - lam * _pieces_attention(qb, pieces)
        o_ref[0, :, cols] = (_rms(o, hg_ref[...]) * (1.0 - lam_init)).astype(o_ref.dtype)


def _diff_sample(q3, kn3, vn3, l, kc, vc, lqk, head_g, lam_init):
    b, t, _ = q3.shape
    past = kc.shape[2]
    w = H_A * DV_A
    new = pl.BlockSpec((1, t, w), lambda bi: (l * b + bi, 0, 0))
    cache = pl.BlockSpec((None, 1, past, H_A, DV_A), lambda bi: (l, bi, 0, 0, 0))
    return pl.pallas_call(
        functools.partial(_diff_sample_kernel, lam_init=lam_init),
        out_shape=jax.ShapeDtypeStruct((b, t, w), BF16),
        grid=(b,),
        in_specs=[
            pl.BlockSpec((4, DH_A), lambda bi: (0, 0)),
            pl.BlockSpec((1, DV_A), lambda bi: (0, 0)),
            pl.BlockSpec((1, t, w), lambda bi: (bi, 0, 0)), cache, cache, new, new,
        ],
        out_specs=pl.BlockSpec((1, t, w), lambda bi: (bi, 0, 0)),
        compiler_params=_cparams(("parallel",)),
        name="diff_sample",
    )(lqk, head_g, q3, kc, vc, kn3, vn3)


QPAIR = 2 * CHUNK
WIN = BAND + QPAIR


def _relbias_kernel(tab_ref, o_ref):
    h = pl.program_id(0)
    r = lax.broadcasted_iota(jnp.int32, (QPAIR, WIN), 0)
    j = lax.broadcasted_iota(jnp.int32, (QPAIR, WIN), 1)
    rel = jnp.clip(r - j + BAND, -REL_CLIP, REL_CLIP) + REL_CLIP
    qc = r // CHUNK
    kc = j // CHUNK - BAND_CHUNKS
    allowed = (kc <= qc) & (kc >= qc - BAND_CHUNKS)

    far = BAND - REL_CLIP
    near_rel = rel[:, far:]

    near = jnp.zeros((QPAIR, WIN - far), F32)
    for t in range(2 * REL_CLIP + 1):
        near = jnp.where(near_rel == t, tab_ref[h, t], near)
    bias = jnp.concatenate([jnp.full((QPAIR, far), tab_ref[h, 2 * REL_CLIP], F32), near], axis=1)
    o_ref[0] = jnp.where(allowed, bias * LOG2E, NEG)


def _relbias(table):
    return pl.pallas_call(
        _relbias_kernel,
        out_shape=jax.ShapeDtypeStruct((H_B, QPAIR, WIN), F32),
        grid=(H_B,),
        in_specs=[pl.BlockSpec(memory_space=pltpu.SMEM)],
        out_specs=pl.BlockSpec((1, QPAIR, WIN), lambda h: (h, 0, 0)),
        compiler_params=_cparams(("arbitrary",)),
        name="relbias",
    )(table)


BAND_TQ = 512
BAND_UNROLL = 2


def _band_prompt_kernel(bias_ref, q_ref, kp_ref, kc_ref, vp_ref, vc_ref, o_ref, kcat, vcat):
    i = pl.program_id(1)
    kcat[0:BAND_TQ, :] = kp_ref[0].astype(BF16)
    kcat[BAND_TQ:, :] = kc_ref[0].astype(BF16)
    vcat[0:BAND_TQ, :] = vp_ref[0].astype(BF16)
    vcat[BAND_TQ:, :] = vc_ref[0].astype(BF16)
    lane = lax.broadcasted_iota(jnp.int32, (QPAIR, LANES), 1)

    def pair_blocks(it, carry, *, first_block):
        starts = [pl.multiple_of((it * BAND_UNROLL + u) * QPAIR, QPAIR) for u in range(BAND_UNROLL)]
        units = [(r0, hp) for r0 in starts for hp in range(H_B // 2)]
        queries = {u: q_ref[0, pl.ds(r0, QPAIR), :] for u, r0 in enumerate(starts)}
        scores, probs, outs = {}, {}, {}

        def qk(i):
            r0, hp = units[i]
            cols = slice(hp * LANES, (hp + 1) * LANES)
            qs = jnp.concatenate(_split_halves(queries[i // (H_B // 2)][:, cols]), axis=0)
            s = _nt(qs, kcat[pl.ds(r0, WIN), cols]) + bias_ref[hp]
            if first_block:
                kpos = r0 - BAND_TQ + lax.broadcasted_iota(jnp.int32, (1, WIN), 1)
                s = s + jnp.where(kpos >= 0, 0.0, NEG)
            scores[i] = s

        def softmax(i):
            s = scores.pop(i)
            e = jnp.exp2(s - jnp.max(s, axis=-1, keepdims=True))
            probs[i] = (e.astype(BF16), jnp.sum(e, axis=-1, keepdims=True))

        def pv(i):
            r0, hp = units[i]
            e, l = probs.pop(i)
            o = jnp.dot(e, vcat[pl.ds(r0, WIN), hp * LANES:(hp + 1) * LANES], preferred_element_type=F32) / l
            outs[i] = jnp.where(lane < DH_B, o[:QPAIR], o[QPAIR:])

        n = len(units)
        for step in range(n + 2):
            if step < n:
                qk(step)
            if 0 <= step - 1 < n:
                softmax(step - 1)
            if 0 <= step - 2 < n:
                pv(step - 2)
        for u, r0 in enumerate(starts):
            row = [outs[u * (H_B // 2) + hp] for hp in range(H_B // 2)]
            o_ref[0, pl.ds(r0, QPAIR), :] = jnp.concatenate(row, axis=1).astype(o_ref.dtype)
        return carry

    trips = BAND_TQ // QPAIR // BAND_UNROLL

    @pl.when(i == 0)
    def _():
        lax.fori_loop(0, trips, functools.partial(pair_blocks, first_block=True), 0)

    @pl.when(i > 0)
    def _():
        lax.fori_loop(0, trips, functools.partial(pair_blocks, first_block=False), 0)


def _band_prompt(q3, kv3, bias):
    b, t, _ = q3.shape
    assert t % BAND_TQ == 0
    w = H_B * DH_B
    cq, ck, cv = 1, 0, 1
    prev = lambda col: pl.BlockSpec((1, BAND_TQ, w), lambda bi, i: (bi, jnp.maximum(i - 1, 0), col))
    cur = lambda col: pl.BlockSpec((1, BAND_TQ, w), lambda bi, i: (bi, i, col))
    return pl.pallas_call(
        _band_prompt_kernel,
        out_shape=jax.ShapeDtypeStruct((b, t, w), BF16),
        grid=(b, t // BAND_TQ),
        in_specs=[
            pl.BlockSpec((H_B // 2, 2 * QPAIR, WIN), lambda bi, i: (0, 0, 0)),
            cur(cq), prev(ck), cur(ck), prev(cv), cur(cv),
        ],
        out_specs=pl.BlockSpec((1, BAND_TQ, w), lambda bi, i: (bi, i, 0)),
        scratch_shapes=[pltpu.VMEM((2 * BAND_TQ, w), BF16), pltpu.VMEM((2 * BAND_TQ, w), BF16)],
        compiler_params=_cparams(("parallel", "arbitrary")),
        name="band_prompt",
    )(bias.reshape(H_B // 2, 2 * QPAIR, WIN), q3, kv3, kv3, kv3, kv3)


def _band_sample_kernel(bc_ref, bn_ref, q_ref, kc_ref, vc_ref, kn_ref, vn_ref, o_ref, *, past):
    t = q_ref.shape[1]
    nband = kc_ref.shape[1]
    lane = lax.broadcasted_iota(jnp.int32, (t, LANES), 1)
    qpos_c = past + lax.broadcasted_iota(jnp.int32, (t, nband), 0)
    kpos_c = past - nband + lax.broadcasted_iota(jnp.int32, (t, nband), 1)
    qpos_n = past + lax.broadcasted_iota(jnp.int32, (t, t), 0)
    kpos_n = past + lax.broadcasted_iota(jnp.int32, (t, t), 1)

    def allowed(qpos, kpos):
        qc, kc = qpos // CHUNK, kpos // CHUNK
        return (kpos >= 0) & (kc <= qc) & (kc >= qc - BAND_CHUNKS)

    ok_c = allowed(qpos_c, kpos_c)
    ok_n = allowed(qpos_n, kpos_n)
    for hp in range(H_B // 2):
        cols = slice(hp * LANES, (hp + 1) * LANES)
        halves = []
        for sub, qm in enumerate(_split_halves(q_ref[0, :, cols])):
            hd = 2 * hp + sub
            pieces = [(kc_ref[0, :, cols].astype(BF16), vc_ref[0, :, cols].astype(BF16),
                       jnp.where(ok_c, bc_ref[hd], NEG)),
                      (kn_ref[0, :, cols].astype(BF16), vn_ref[0, :, cols].astype(BF16),
                       jnp.where(ok_n, bn_ref[hd], NEG))]
            halves.append(_pieces_attention(qm, pieces))
        o_ref[0, :, cols] = jnp.where(lane < DH_B, halves[0], halves[1]).astype(o_ref.dtype)


def _band_sample(q3, kv3, kc, vc, bias, past):
    b, t, _ = q3.shape
    nband = kc.shape[1]
    assert nband == BAND and t <= CHUNK
    w = H_B * DH_B
    bias_c = bias[:, :t, :nband]
    bias_n = bias[:, :t, nband:nband + t]
    blk = lambda col: pl.BlockSpec((1, t, w), lambda bi: (bi, 0, col))
    cache = pl.BlockSpec((1, nband, w), lambda bi: (bi, 0, 0))
    return pl.pallas_call(
        functools.partial(_band_sample_kernel, past=past),
        out_shape=jax.ShapeDtypeStruct((b, t, w), BF16),
        grid=(b,),
        in_specs=[
            pl.BlockSpec((H_B, t, nband), lambda bi: (0, 0, 0)),
            pl.BlockSpec((H_B, t, t), lambda bi: (0, 0, 0)),
            blk(1), cache, cache, blk(0), blk(1),
        ],
        out_specs=pl.BlockSpec((1, t, w), lambda bi: (bi, 0, 0)),
        compiler_params=_cparams(("parallel",)),
        name="band_sample",
    )(bias_c, bias_n, q3, kc, vc, kv3, kv3)


CONV_PAD = 8
MLSTM_L = 256
MLSTM_NB = 2


def _mlstm_kernel(cqk_ref, cv_ref, co_ref, cif_ref, conv0_ref, cw_ref, cb_ref, gb_ref, hg_ref,
                  c0_ref, n0_ref, m0_ref, hs_ref, c_ref, n_ref, m_ref, xext, *, L):
    @pl.when(pl.program_id(1) == 0)
    def _():
        xext[:, 0:CONV_PAD, :] = conv0_ref[...]
        c_ref[...] = c0_ref[...]
        n_ref[...] = n0_ref[...]
        m_ref[...] = m0_ref[...]

    lane = lax.broadcasted_iota(jnp.int32, (L, LANES), 1)
    tr = lax.broadcasted_iota(jnp.int32, (L, L), 0)
    tc = lax.broadcasted_iota(jnp.int32, (L, L), 1)
    causal = tr >= tc
    sr = lax.broadcasted_iota(jnp.int32, (8, LANES), 0)
    sc = lax.broadcasted_iota(jnp.int32, (8, LANES), 1)
    sel = jnp.where(sr < H_C, jnp.where(sc == sr, 1.0, jnp.where(sc == sr + H_C, -1.0, 0.0)), 0.0)
    er = lax.broadcasted_iota(jnp.int32, (DH_C, DH_C), 0)
    ec = lax.broadcasted_iota(jnp.int32, (DH_C, DH_C), 1)
    eye = jnp.where(er == ec, 1.0, 0.0).astype(BF16)
    for bb in range(cqk_ref.shape[0]):
        _mlstm_chunk(bb, cqk_ref, cv_ref, co_ref, cif_ref, cw_ref, cb_ref, gb_ref, hg_ref,
                     hs_ref, c_ref, n_ref, m_ref, xext, L, lane, causal, sel, eye)


def _mlstm_chunk(bb, cqk_ref, cv_ref, co_ref, cif_ref, cw_ref, cb_ref, gb_ref, hg_ref,
                 hs_ref, c_ref, n_ref, m_ref, xext, L, lane, causal, sel, eye):
    xext[bb, CONV_PAD:CONV_PAD + L, :] = cqk_ref[bb]
    base = CONV_PAD - (CONV_W - 1)
    u = 0.0
    for j in range(CONV_W):
        u = u + xext[bb, base + j:base + j + L, :] * cw_ref[j:j + 1, :]
    u = cb_ref[...] + u
    u = u * jax.nn.sigmoid(u)
    xext[bb, 0:CONV_PAD, :] = xext[bb, L:L + CONV_PAD, :]

    z = cif_ref[bb] + gb_ref[...]
    lf = jnp.minimum(z, 0.0) - jnp.log1p(jnp.exp(-jnp.abs(z)))
    b_all = jnp.dot(causal.astype(F32), lf, preferred_element_type=F32,
                    precision=lax.Precision.HIGHEST)
    rt = lax.dot_general(sel, jnp.where(lane < H_C, z, b_all), NT_DIMS,
                         preferred_element_type=F32, precision=lax.Precision.HIGHEST)

    for h in range(H_C):
        cols = slice(h * DH_C, (h + 1) * DH_C)
        q = u[:, h * DH_C:(h + 1) * DH_C]
        k = u[:, D_C + h * DH_C:D_C + (h + 1) * DH_C] * (DH_C ** -0.5)
        v = cv_ref[bb, :, cols]
        qb, kb_, vb_ = q.astype(BF16), k.astype(BF16), v.astype(BF16)
        b_col = b_all[:, H_C + h:H_C + h + 1]
        li_col = z[:, h:h + 1]
        m_prev = m_ref[bb, h:h + 1, 0:1]
        cs = c_ref[bb, h]
        ns = n_ref[bb, h:h + 1, :]

        dmat = jnp.where(causal, b_col + rt[h:h + 1, :], NEG)
        inter = b_col + m_prev
        m_t = jnp.maximum(inter, jnp.max(dmat, axis=-1, keepdims=True))
        w_intra = jnp.exp(dmat - m_t)
        w_inter = jnp.exp(inter - m_t)
        a = w_intra * _nt(qb, kb_)
        num = (jnp.dot(a.astype(BF16), vb_, preferred_element_type=F32)
               + w_inter * _nt(qb, cs.astype(BF16)))
        den = jnp.sum(a, axis=-1, keepdims=True) + w_inter * jnp.sum(q * ns, axis=-1, keepdims=True)
        hh = num / jnp.maximum(jnp.abs(den), jnp.exp(-m_t))

        b_last = b_col[L - 1:L, :]
        g = b_last - b_col + li_col
        m_new = jnp.maximum(b_last + m_prev, jnp.max(g, axis=0, keepdims=True))
        ws = jnp.exp(g - m_new)
        decay = jnp.exp(b_last + m_prev - m_new)
        vwt = _nt(eye, (ws * v).astype(BF16)).astype(BF16)
        c_ref[bb, h] = decay * cs + jnp.dot(vwt, kb_, preferred_element_type=F32)
        n_ref[bb, h:h + 1, :] = decay * ns + jnp.sum(ws * k, axis=0, keepdims=True)
        m_ref[bb, h:h + 1, :] = jnp.broadcast_to(m_new, (1, LANES))

        gate = jax.nn.sigmoid(co_ref[bb, :, cols])
        hs_ref[bb, :, cols] = (_rms(hh, hg_ref[...]) * gate).astype(hs_ref.dtype)


def _mlstm(cqk3, cvo3, cif3, conv0, conv_w, conv_b, gate_b, head_g, c0, n0, m0, L):
    b, t, _ = cqk3.shape
    nc = t // L
    nb = MLSTM_NB
    assert b % nb == 0
    blk = lambda col, w: pl.BlockSpec((nb, L, w), lambda bi, c: (bi, c, col))
    const = lambda shape: pl.BlockSpec(shape, lambda bi, c: (0,) * len(shape))
    per_b = lambda shape: pl.BlockSpec((nb,) + shape, lambda bi, c: (bi,) + (0,) * len(shape))
    return pl.pallas_call(
        functools.partial(_mlstm_kernel, L=L),
        out_shape=(jax.ShapeDtypeStruct((b, t, D_C), BF16),
                   jax.ShapeDtypeStruct((b, H_C, DH_C, DH_C), F32),
                   jax.ShapeDtypeStruct((b, H_C, DH_C), F32),
                   jax.ShapeDtypeStruct((b, H_C, LANES), F32)),
        grid=(b // nb, nc),
        in_specs=[
            blk(0, 2 * D_C), blk(0, D_C), blk(1, D_C), blk(0, LANES),
            per_b((CONV_PAD, 2 * D_C)), const((CONV_W, 2 * D_C)), const((1, 2 * D_C)),
            const((1, LANES)), const((1, DH_C)),
            per_b((H_C, DH_C, DH_C)), per_b((H_C, DH_C)), per_b((H_C, LANES)),
        ],
        out_specs=(pl.BlockSpec((nb, L, D_C), lambda bi, c: (bi, c, 0)),
                   per_b((H_C, DH_C, DH_C)), per_b((H_C, DH_C)), per_b((H_C, LANES))),
        scratch_shapes=[pltpu.VMEM((nb, CONV_PAD + L, 2 * D_C), F32)],
        compiler_params=_cparams(("parallel", "arbitrary")),
        name="mlstm",
    )(cqk3, cvo3, cvo3, cif3, conv0, conv_w, conv_b, gate_b, head_g, c0, n0, m0)


def _mix_cross_kernel(oa_ref, ob_ref, oc_ref, ga_ref, gb_ref, gc_ref, x_ref, wa_ref, wb_ref, wc_ref, wp_ref,
                      g_ref, wq_ref, mk_ref, mv_ref, wo_ref, o_ref):
    mixed = (ga_ref[0] * jnp.dot(oa_ref[0], wa_ref[...], preferred_element_type=F32)
             + gb_ref[0] * jnp.dot(ob_ref[0], wb_ref[...], preferred_element_type=F32)
             + gc_ref[0] * jnp.dot(oc_ref[0], wc_ref[...], preferred_element_type=F32))
    x = x_ref[0] + jnp.dot(mixed.astype(BF16), wp_ref[...], preferred_element_type=F32)
    h = _rms(x, g_ref[...]).astype(BF16)
    q = (jnp.dot(h, wq_ref[...], preferred_element_type=F32) * (DH_M ** -0.5 * LOG2E)).astype(BF16)
    nm = mk_ref.shape[1]
    mk = mk_ref[0].reshape(nm, H_M * DH_M).astype(BF16)
    mv = mv_ref[0].reshape(nm, H_M * DH_M).astype(BF16)
    head_cols = [slice(hd * DH_M, (hd + 1) * DH_M) for hd in range(H_M)]
    scores, probs, outs = {}, {}, {}
    for step in range(H_M + 2):
        if step < H_M:
            scores[step] = _nt(q[:, head_cols[step]], mk[:, head_cols[step]])
        if 0 <= step - 1 < H_M:
            s = scores.pop(step - 1)
            e = jnp.exp2(s - jnp.max(s, axis=-1, keepdims=True))
            probs[step - 1] = (e.astype(BF16), jnp.sum(e, axis=-1, keepdims=True))
        if 0 <= step - 2 < H_M:
            e, l = probs.pop(step - 2)
            o = jnp.dot(e, mv[:, head_cols[step - 2]], preferred_element_type=F32) / l
            outs[step - 2] = o.astype(BF16)
    o = jnp.concatenate([outs[hd] for hd in range(H_M)], axis=1)
    o_ref[0] = x + jnp.dot(o, wo_ref[...], preferred_element_type=F32)


def _mix_cross(oa, ob, oc, gates, x3, l, wa, wb, wc, wp, g, wq, mk, mv, wo):
    b, t, d = x3.shape
    tm = min(t, 512)
    nm = mk.shape[2]
    row = lambda w, col=0: pl.BlockSpec((1, tm, w), lambda bi, i: (bi, i, col))
    full = lambda a: pl.BlockSpec((None,) + a.shape[1:], lambda bi, i: (l, 0, 0))
    mem = pl.BlockSpec((None, 1, nm, H_M, DH_M), lambda bi, i: (l, bi, 0, 0, 0))
    return pl.pallas_call(
        _mix_cross_kernel,
        out_shape=jax.ShapeDtypeStruct((b, t, d), F32),
        grid=(b, t // tm),
        in_specs=[row(oa.shape[2]), row(ob.shape[2]), row(oc.shape[2]),
                  row(d, 0), row(d, 1), row(d, 2), row(d),
                  full(wa), full(wb), full(wc), full(wp),
                  pl.BlockSpec(g.shape, lambda bi, i: (0, 0)), full(wq), mem, mem, full(wo)],
        out_specs=row(d),
        compiler_params=_cparams(("parallel", "parallel")),
        name="mix_cross",
    )(oa, ob, oc, gates, gates, gates, x3, wa, wb, wc, wp, g, wq, mk, mv, wo)


FFN_TF = 1408
FFN_TM = 512


def _ffn_kernel(x_ref, g_ref, wg_ref, wu_ref, wd_ref, gf_ref, o_ref, *, final_norm):
    x = x_ref[...]
    h = _rms(x, g_ref[...]).astype(BF16)
    chunks = [slice(c, c + FFN_TF) for c in range(0, D_FF, FFN_TF)]
    pre = [(jnp.dot(h, wg_ref[:, c], preferred_element_type=F32),
            jnp.dot(h, wu_ref[:, c], preferred_element_type=F32)) for c in chunks]
    y = x
    for c, (gate, up) in zip(chunks, pre):
        a = (gate * jax.nn.sigmoid(gate) * up).astype(BF16)
        y = y + jnp.dot(a, wd_ref[c, :], preferred_element_type=F32)
    if final_norm:
        y = _rms(y, gf_ref[...])
    o_ref[...] = y


def _ffn(x, g, l, wg, wu, wd, g_final, final_norm):
    m, d = x.shape
    tm = min(m, FFN_TM)
    resident = lambda a: pl.BlockSpec((None,) + a.shape[1:], lambda i: (l, 0, 0),
                                      pipeline_mode=pl.Buffered(1))
    return pl.pallas_call(
        functools.partial(_ffn_kernel, final_norm=final_norm),
        out_shape=jax.ShapeDtypeStruct((m, d), F32),
        grid=(m // tm,),
        in_specs=[
            pl.BlockSpec((tm, d), lambda i: (i, 0)),
            pl.BlockSpec((1, d), lambda i: (0, 0)),
            resident(wg), resident(wu), resident(wd),
            pl.BlockSpec((1, d), lambda i: (0, 0)),
        ],
        out_specs=pl.BlockSpec((tm, d), lambda i: (i, 0)),
        compiler_params=_cparams(("parallel",)),
        name="ffn",
    )(x, g, wg, wu, wd, g_final)


def _layer(x3, l, w, mem_k, mem_v, cache, final_norm, akv_all):
    b, t, d = x3.shape
    x2 = x3.reshape(b * t, d)
    row = lambda a: a.reshape(1, -1)
    lam_init = 0.8 - 0.6 * math.exp(-0.3 * l)

    gates, cqk, cif, q, ak_all, av_all, akv, bkv, cvo = _in_proj(
        x2, row(w['g_mix']), l, w['w_in'], w['w_if'], *akv_all, (b, t))
    q3 = q.reshape(b, t, SEG)
    bkv3 = bkv.reshape(b, t, SEG)
    cqk3 = cqk.reshape(b, t, 2 * D_C)
    keep = t if cache is not None else min(BAND, t)
    b_k = bkv3[:, t - keep:, :HALF].reshape(b, keep, H_B, DH_B)
    b_v = bkv3[:, t - keep:, HALF:].reshape(b, keep, H_B, DH_B)

    gate_b = jnp.concatenate([w['b_i'], w['b_f'], jnp.zeros((LANES - 2 * H_C,), F32)]).reshape(1, LANES)
    if cache is None:
        oa = _diff_prompt(q3, akv.reshape(b, t, SEG), w['lqk'], row(w['a_head_g']), lam_init)
        ob = _band_prompt(q3, bkv3, w['relbias'])
        conv0 = jnp.zeros((b, CONV_PAD, 2 * D_C), F32)
        c0 = jnp.zeros((b, H_C, DH_C, DH_C), F32)
        n0 = jnp.zeros((b, H_C, DH_C), F32)
        m0 = jnp.zeros((b, H_C, LANES), F32)
        L = min(t, MLSTM_L)
    else:
        past = cache['a_k'].shape[2]
        ak3 = ak_all.reshape(DEPTH * b, t, HALF)
        av3 = av_all.reshape(DEPTH * b, t, HALF)
        oa = _diff_sample(q3, ak3, av3, l, cache['a_k'], cache['a_v'],
                          w['lqk'], row(w['a_head_g']), lam_init)
        nband = cache['b_k'].shape[1]
        ob = _band_sample(q3, bkv3, cache['b_k'].reshape(b, nband, -1), cache['b_v'].reshape(b, nband, -1),
                          w['relbias'], past)
        conv0 = jnp.pad(cache['conv'], ((0, 0), (CONV_PAD - (CONV_W - 1), 0), (0, 0)))
        c0, n0 = cache['C'], cache['n']
        m0 = jnp.broadcast_to(cache['m'][:, :, None], (b, H_C, LANES))
        L = t
    oc, c_new, n_new, m_new = _mlstm(cqk3, cvo.reshape(b, t, SEG), cif.reshape(b, t, LANES), conv0,
                                     w['conv_w'], row(w['conv_b']), gate_b, row(w['c_head_g']), c0, n0, m0, L)
    assert t >= CONV_W - 1
    conv_new = cqk3[:, t - (CONV_W - 1):]

    x3 = _mix_cross(oa, ob, oc, gates.reshape(b, t, 3 * d), x3, l,
                    w['w_up_a'], w['w_up_b'], w['w_up_c'], w['w_o'],
                    row(w['g_cross']), w['w_mq'], mem_k, mem_v, w['w_mo'])
    x2 = _ffn(x3.reshape(b * t, d), row(w['g_ffn']), l, w['w_ff_g'], w['w_ff_u'], w['w_ff_d'],
              row(w['g_final']), final_norm)
    return x2.reshape(b, t, d), (ak_all, av_all), (b_k, b_v, c_new, n_new, m_new[:, :, 0], conv_new)


def kernel(x_prompt, x_sample, cache_a_k, cache_a_v, cache_b_k, cache_b_v, state_c_C, state_c_n, state_c_m, state_c_conv, cache_mem_k, cache_mem_v, mem_prompt, g_mix, w_in, a_lq1, a_lk1, a_lq2, a_lk2, a_head_g, b_rel, c_conv_w, c_conv_b, c_b_i, c_b_f, c_head_g, w_up_a, w_up_b, w_up_c, w_o, g_cross, w_mq, w_mk, w_mv, w_mo, g_ffn, w_ff_g, w_ff_u, w_ff_d, g_final):
    xp, xs = x_prompt, x_sample
    bp = x_prompt.shape[0]
    new_p = [[] for _ in range(6)]
    new_s = [[] for _ in range(6)]
    akv_p = akv_s = (None, None)
    bf = lambda a: a.astype(BF16)
    wp, wif = _prep_w_in(w_in)
    wts = dict(w_up_a=bf(w_up_a), w_up_b=bf(w_up_b), w_up_c=bf(w_up_c), w_o=bf(w_o), w_mq=bf(w_mq),
               w_mo=bf(w_mo), w_ff_g=bf(w_ff_g), w_ff_u=bf(w_ff_u), w_ff_d=bf(w_ff_d))
    mem_k = _mem_proj(mem_prompt, bf(w_mk))
    mem_v = _mem_proj(mem_prompt, bf(w_mv))
    for l in range(DEPTH):
        w = dict(g_mix=g_mix[l], w_in=wp, w_if=wif,
                 lqk=jnp.stack([a_lq1[l], a_lk1[l], a_lq2[l], a_lk2[l]]),
                 a_head_g=a_head_g[l], relbias=_relbias(b_rel[l]), conv_w=c_conv_w[l], conv_b=c_conv_b[l],
                 b_i=c_b_i[l], b_f=c_b_f[l], c_head_g=c_head_g[l],
                 g_cross=g_cross[l], g_ffn=g_ffn[l], g_final=g_final, **wts)
        last = l == DEPTH - 1
        xp, akv_p, st_p = _layer(xp, l, w, mem_k, mem_v, None, last, akv_p)
        cache = dict(a_k=cache_a_k, a_v=cache_a_v, b_k=cache_b_k[l], b_v=cache_b_v[l],
                     C=state_c_C[l], n=state_c_n[l], m=state_c_m[l], conv=state_c_conv[l])
        xs, akv_s, st_s = _layer(xs, l, w, cache_mem_k, cache_mem_v, cache, last, akv_s)
        for lst, a in zip(new_p, st_p):
            lst.append(a)
        for lst, a in zip(new_s, st_s):
            lst.append(a)
    heads = lambda a, b, t: a.reshape(DEPTH, b, t, H_A, DV_A)
    outs_p = ([heads(a, bp, x_prompt.shape[1]) for a in akv_p] + [jnp.stack(a, 0) for a in new_p]
              + [mem_k, mem_v])
    outs_s = [heads(a, xs.shape[0], xs.shape[1]) for a in akv_s] + [jnp.stack(a, 0) for a in new_s]
    return (xp, xs) + tuple(outs_p) + tuple(outs_s)
```

```python
import functools
import math

import jax
import jax.numpy as jnp
from jax import lax
from jax.experimental import pallas as pl
from jax.experimental.pallas import tpu as pltpu

F32 = jnp.float32
BF16 = jnp.bfloat16

D_MODEL = 1024
DEPTH = 2
CHUNK = 64
EPS = 1e-6
NEG = -1e30
LOG2E = math.log2(math.e)
H_A = 4
DH_A = 64
DV_A = 2 * DH_A
H_B = 8
DH_B = 64
BAND_CHUNKS = 8
BAND = BAND_CHUNKS * CHUNK
REL_CLIP = 128
H_C = 4
DH_C = 128
D_C = H_C * DH_C
CONV_W = 4
H_M = 4
DH_M = D_MODEL // H_M
D_FF = -(-8 * D_MODEL // (3 * 256)) * 256

LANES = 128
VMEM_LIMIT = 48 * 1024 * 1024

SEG = 1024
HALF = SEG // 2
GATE_TILES = 3
N_TILES = 8
IN_SIZES = (H_A * DH_A,) * 4 + (H_A * DV_A,) + (H_B * DH_B,) * 3 + (2 * D_C, D_C, D_C, 2 * H_C, 3 * D_MODEL)

ALIBI_SLOPES = tuple(2.0 ** (-8.0 * (i + 1) / H_A) for i in range(H_A))

NT_DIMS = (((1,), (1,)), ((), ()))


def _cparams(sem):
    return pltpu.CompilerParams(dimension_semantics=sem, vmem_limit_bytes=VMEM_LIMIT)


def _rms(x, g):
    ms = jnp.mean(x * x, axis=-1, keepdims=True)
    return x * lax.rsqrt(ms + EPS) * g


def _nt(a, b):
    return lax.dot_general(a, b, NT_DIMS, preferred_element_type=F32)


def _prep_w_in_kernel(w_ref, o_ref, oif_ref):
    offs = [0]
    for size in IN_SIZES:
        offs.append(offs[-1] + size)
    seg = lambda i: w_ref[:, offs[i]:offs[i + 1]]
    a_q1, a_q2, a_k1, a_k2, a_v, b_q, b_k, b_v, c_qk, c_v, c_o = (seg(i) for i in range(11))
    g = seg(12)

    def put(col, val):
        o_ref[:, col:col + val.shape[1]] = val.astype(BF16)

    put(0, g)
    put(GATE_TILES * SEG, c_qk)
    for h in range(H_A):
        hs = slice(h * DH_A, (h + 1) * DH_A)
        base = h * 2 * DH_A
        put(4 * SEG + base, a_q1[:, hs] * (DH_A ** -0.5))
        put(4 * SEG + base + DH_A, a_q2[:, hs] * (DH_A ** -0.5))
        put(5 * SEG + base, a_k1[:, hs])
        put(5 * SEG + base + DH_A, a_k2[:, hs])
    put(4 * SEG + HALF, b_q * (DH_B ** -0.5))
    put(5 * SEG + HALF, a_v)
    put(6 * SEG, b_k)
    put(6 * SEG + HALF, b_v)
    put(7 * SEG, c_v)
    put(7 * SEG + HALF, c_o)
    tile = w_ref[:, offs[11]:offs[11] + LANES]
    lane = lax.broadcasted_iota(jnp.int32, tile.shape, 1)
    oif_ref[...] = jnp.where(lane < 2 * H_C, tile, 0.0).astype(BF16)


def _prep_w_in(w):
    depth, k, n = w.shape
    tk = 256
    n_out = N_TILES * SEG
    return pl.pallas_call(
        _prep_w_in_kernel,
        out_shape=(jax.ShapeDtypeStruct((depth, k, n_out), BF16),
                   jax.ShapeDtypeStruct((depth, k, LANES), BF16)),
        grid=(depth, k // tk),
        in_specs=[pl.BlockSpec((None, tk, n), lambda l, i: (l, i, 0))],
        out_specs=(pl.BlockSpec((None, tk, n_out), lambda l, i: (l, i, 0)),
                   pl.BlockSpec((None, tk, LANES), lambda l, i: (l, i, 0))),
        compiler_params=_cparams(("parallel", "parallel")),
        name="prep_w_in",
    )(w)


IN_PROJ_TM = 256


def _in_proj_kernel(*refs, aliased, heads5d):
    x_ref, g_ref, w_ref, wif_ref = refs[:4]
    og_ref, ocqk_ref, ocif_ref, oq_ref, oak_ref, oav_ref, oakv_ref, obkv_ref, ocvo_ref = refs[6 if aliased else 4:]
    h = _rms(x_ref[...], g_ref[...]).astype(BF16)
    cols = lambda j, lo=0, hi=SEG: jnp.dot(h, w_ref[:, j * SEG + lo:j * SEG + hi], preferred_element_type=F32)

    ocif_ref[...] = jnp.dot(h, wif_ref[...], preferred_element_type=F32)
    for j in range(GATE_TILES):
        og_ref[:, j * SEG:(j + 1) * SEG] = jax.nn.sigmoid(cols(j)).astype(og_ref.dtype)
    ocqk_ref[...] = cols(GATE_TILES)
    oq_ref[...] = (cols(4) * LOG2E).astype(oq_ref.dtype)
    for out_ref, off in ((oak_ref, 0), (oav_ref, HALF)):
        a = cols(5, off, off + HALF)
        oakv_ref[:, off:off + HALF] = a.astype(oakv_ref.dtype)
        if heads5d:
            out_ref[0, 0] = a.reshape(a.shape[0], H_A, DV_A)
        else:
            out_ref[0] = a
    obkv_ref[...] = cols(6)
    ocvo_ref[...] = cols(7)


def _in_proj(x, g, l, wp, wif, ak_all, av_all, bt):
    m, k = x.shape
    tm = min(m, IN_PROJ_TM)
    aliased = ak_all is not None
    b, t = bt
    heads5d = t % tm == 0
    if heads5d:
        stacked = jax.ShapeDtypeStruct((DEPTH, b, t, H_A, DV_A), F32)
        layer_row = pl.BlockSpec((1, 1, tm, H_A, DV_A), lambda i: (l, i // (t // tm), i % (t // tm), 0, 0))
    else:
        stacked = jax.ShapeDtypeStruct((DEPTH, m, HALF), F32)
        layer_row = pl.BlockSpec((1, tm, HALF), lambda i: (l, i, 0))
    resident = lambda a: pl.BlockSpec((None,) + a.shape[1:], lambda i: (l, 0, 0), pipeline_mode=pl.Buffered(1))
    in_specs = [pl.BlockSpec((tm, k), lambda i: (i, 0)), pl.BlockSpec((1, k), lambda i: (0, 0)),
                resident(wp), resident(wif)]
    args = [x, g, wp, wif]
    if aliased:
        in_specs += [pl.BlockSpec(memory_space=pl.ANY)] * 2
        args += [ak_all, av_all]
    row = lambda w: pl.BlockSpec((tm, w), lambda i: (i, 0))
    sds = jax.ShapeDtypeStruct
    return pl.pallas_call(
        functools.partial(_in_proj_kernel, aliased=aliased, heads5d=heads5d),
        out_shape=(sds((m, GATE_TILES * SEG), BF16), sds((m, SEG), F32), sds((m, LANES), F32),
                   sds((m, SEG), BF16), stacked, stacked, sds((m, SEG), BF16),
                   sds((m, SEG), F32), sds((m, SEG), F32)),
        grid=(m // tm,),
        in_specs=in_specs,
        out_specs=(row(GATE_TILES * SEG), row(SEG), row(LANES), row(SEG), layer_row, layer_row, row(SEG),
                   row(SEG), row(SEG)),
        input_output_aliases={4: 4, 5: 5} if aliased else {},
        compiler_params=_cparams(("parallel",)),
        name="in_proj",
    )(*args)


def _mem_proj_kernel(x_ref, w_ref, o_ref):
    y = jnp.dot(x_ref[0].astype(BF16), w_ref[...], preferred_element_type=F32)
    o_ref[...] = y.reshape(o_ref.shape)


def _mem_proj(mem, w):
    b, nm, k = mem.shape
    depth, _, n = w.shape
    return pl.pallas_call(
        _mem_proj_kernel,
        out_shape=jax.ShapeDtypeStruct((depth, b, nm, H_M, DH_M), F32),
        grid=(depth, b),
        in_specs=[pl.BlockSpec((1, nm, k), lambda l, bi: (bi, 0, 0)),
                  pl.BlockSpec((None, k, n), lambda l, bi: (l, 0, 0))],
        out_specs=pl.BlockSpec((None, None, nm, H_M, DH_M), lambda l, bi: (l, bi, 0, 0, 0)),
        compiler_params=_cparams(("parallel", "parallel")),
        name="mem_proj",
    )(mem, w)


def _online_updates(jobs):
    scores, staged = {}, {}

    def softmax(i):
        s, (acc, m, l) = scores.pop(i), jobs[i][2]
        slabs = [s[:, j * LANES:(j + 1) * LANES] for j in range(s.shape[1] // LANES)]
        m_old = m[...]
        m_new = jnp.maximum(m_old, jnp.max(functools.reduce(jnp.maximum, slabs), axis=-1, keepdims=True))
        alpha = jnp.exp2(m_old - m_new)
        ps = [jnp.exp2(c - m_new) for c in slabs]
        l[...] = alpha * l[...] + jnp.sum(functools.reduce(jnp.add, ps), axis=-1, keepdims=True)
        m[...] = m_new
        staged[i] = (alpha, jnp.concatenate(ps, axis=1).astype(BF16))

    def pv(i):
        (alpha, p), (_, v, (acc, m, l)) = staged.pop(i), jobs[i]
        acc[...] = alpha * acc[...] + jnp.dot(p, v, preferred_element_type=F32)

    n = len(jobs)
    for step in range(n + 2):
        if step < n:
            scores[step] = jobs[step][0]()
        if 0 <= step - 1 < n:
            softmax(step - 1)
        if 0 <= step - 2 < n:
            pv(step - 2)


def _split_halves(q):
    lane = lax.broadcasted_iota(jnp.int32, q.shape, 1)
    lo = jnp.where(lane < DH_A, q, 0.0).astype(BF16)
    hi = jnp.where(lane >= DH_A, q, 0.0).astype(BF16)
    return lo, hi


def _lambda(lqk, lam_init):
    e1 = jnp.exp(jnp.sum(lqk[0:1, :] * lqk[1:2, :], axis=-1, keepdims=True))
    e2 = jnp.exp(jnp.sum(lqk[2:3, :] * lqk[3:4, :], axis=-1, keepdims=True))
    return e1 - e2 + lam_init


DIFF_HEADS = 4


def _diff_prompt_kernel(slopes_ref, lqk_ref, hg_ref, q_ref, k_ref, v_ref, o_ref,
                        diag_bias, acc, m, l, *, tq, nh, lam_init):
    hg = pl.program_id(1)
    qi = pl.program_id(2)
    slopes = [slopes_ref[hg * nh + i] * LOG2E for i in range(nh)]
    q0 = pl.multiple_of(qi * tq, tq)
    head_cols = [slice(i * LANES, (i + 1) * LANES) for i in range(nh)]

    @pl.when(qi == 0)
    def _():
        r = lax.broadcasted_iota(jnp.int32, (tq, tq), 0)
        c = lax.broadcasted_iota(jnp.int32, (tq, tq), 1)
        rel = (r - jnp.abs(r - c)).astype(F32)
        allowed = (c // CHUNK) <= (r // CHUNK)
        for i in range(nh):
            diag_bias[i] = jnp.where(allowed, slopes[i] * rel, NEG)

    m[...] = jnp.full(m.shape, NEG, F32)
    l[...] = jnp.zeros(l.shape, F32)
    acc[...] = jnp.zeros(acc.shape, F32)
    chains = []
    for i in range(nh):
        for j, qm in enumerate(_split_halves(q_ref[0, :, head_cols[i]])):
            c = 2 * i + j
            chains.append((qm, i, (acc.at[c], m.at[c], l.at[c])))

    score = lambda qm, k, bias: _nt(qm, k) + bias

    def full_tiles(k_starts):
        jobs = []
        for k0 in k_starts:
            kpos = (k0 - q0) + lax.broadcasted_iota(jnp.int32, (1, tq), 1)
            kposf = kpos.astype(F32)
            for qm, i, st in chains:
                k = k_ref[0, pl.ds(k0, tq), head_cols[i]]
                v = v_ref[0, pl.ds(k0, tq), head_cols[i]]
                jobs.append((functools.partial(score, qm, k, slopes[i] * kposf), v, st))
        _online_updates(jobs)

    def tile_pair(it, carry):
        k0 = pl.multiple_of(it * (2 * tq), 2 * tq)
        full_tiles([k0, pl.multiple_of(k0 + tq, tq)])
        return carry

    lax.fori_loop(0, qi // 2, tile_pair, 0)

    @pl.when(qi % 2 == 1)
    def _():
        full_tiles([pl.multiple_of((qi - 1) * tq, tq)])

    half = tq // 2
    jobs = []
    for rows, nk in ((slice(0, half), half), (slice(half, tq), tq)):
        for qm, i, st in chains:
            k = k_ref[0, pl.ds(q0, nk), head_cols[i]]
            v = v_ref[0, pl.ds(q0, nk), head_cols[i]]
            jobs.append((functools.partial(score, qm[rows], k, diag_bias[i, rows, 0:nk]), v,
                         tuple(ref.at[rows] for ref in st)))
    _online_updates(jobs)

    lam = _lambda(lqk_ref[...], lam_init)
    for i in range(nh):
        o = acc[2 * i] / l[2 * i] - lam * (acc[2 * i + 1] / l[2 * i + 1])
        o_ref[0, :, head_cols[i]] = (_rms(o, hg_ref[...]) * (1.0 - lam_init)).astype(o_ref.dtype)


def _diff_prompt(q3, kv3, lqk, head_g, lam_init):
    b, t, _ = q3.shape
    tq = min(t, 512)
    nh = DIFF_HEADS
    w = nh * LANES
    slopes = jnp.asarray(ALIBI_SLOPES, F32)
    return pl.pallas_call(
        functools.partial(_diff_prompt_kernel, tq=tq, nh=nh, lam_init=lam_init),
        out_shape=jax.ShapeDtypeStruct((b, t, H_A * DV_A), BF16),
        grid=(b, H_A // nh, t // tq),
        in_specs=[
            pl.BlockSpec(memory_space=pltpu.SMEM),
            pl.BlockSpec((4, DH_A), lambda bi, h, qi: (0, 0)),
            pl.BlockSpec((1, DV_A), lambda bi, h, qi: (0, 0)),
            pl.BlockSpec((1, tq, w), lambda bi, h, qi: (bi, qi, h)),
            pl.BlockSpec((1, t, w), lambda bi, h, qi: (bi, 0, h)),
            pl.BlockSpec((1, t, w), lambda bi, h, qi: (bi, 0, H_A // nh + h)),
        ],
        out_specs=pl.BlockSpec((1, tq, w), lambda bi, h, qi: (bi, qi, h)),
        scratch_shapes=[
            pltpu.VMEM((nh, tq, tq), F32),
            pltpu.VMEM((2 * nh, tq, LANES), F32), pltpu.VMEM((2 * nh, tq, LANES), F32),
            pltpu.VMEM((2 * nh, tq, LANES), F32),
        ],
        compiler_params=_cparams(("parallel", "parallel", "arbitrary")),
        name="diff_prompt",
    )(slopes, lqk, head_g, q3, kv3, kv3)


def _pieces_attention(qm, pieces):
    ss = [_nt(qm, k) + bias for k, _, bias in pieces]
    m = functools.reduce(jnp.maximum, [jnp.max(s, axis=-1, keepdims=True) for s in ss])
    es = [jnp.exp2(s - m) for s in ss]
    l = functools.reduce(jnp.add, [jnp.sum(e, axis=-1, keepdims=True) for e in es])
    o = functools.reduce(jnp.add, [jnp.dot(e.astype(BF16), v, preferred_element_type=F32)
                                   for e, (_, v, _) in zip(es, pieces)])
    return o / l


def _diff_sample_kernel(lqk_ref, hg_ref, q_ref, kc_ref, vc_ref, kn_ref, vn_ref, o_ref, *, lam_init):
    t = q_ref.shape[1]
    past = kc_ref.shape[1]
    lam = _lambda(lqk_ref[...], lam_init)
    rc = lax.broadcasted_iota(jnp.int32, (t, past), 0)
    cc = lax.broadcasted_iota(jnp.int32, (t, past), 1)
    rn = lax.broadcasted_iota(jnp.int32, (t, t), 0)
    cn = lax.broadcasted_iota(jnp.int32, (t, t), 1)
    dist_c = jnp.abs(past + rc - cc).astype(F32)
    dist_n = jnp.abs(rn - cn).astype(F32)
    ok_c = (cc // CHUNK) <= ((past + rc) // CHUNK)
    ok_n = ((past + cn) // CHUNK) <= ((past + rn) // CHUNK)
    kc = kc_ref[0].reshape(past, H_A * DV_A).astype(BF16)
    vc = vc_ref[0].reshape(past, H_A * DV_A).astype(BF16)
    for h in range(H_A):
        cols = slice(h * LANES, (h + 1) * LANES)
        bias_c = jnp.where(ok_c, (-ALIBI_SLOPES[h] * LOG2E) * dist_c, NEG)
        bias_n = jnp.where(ok_n, (-ALIBI_SLOPES[h] * LOG2E) * dist_n, NEG)
        pieces = [(kc[:, cols], vc[:, cols], bias_c),
                  (kn_ref[0, :, cols].astype(BF16), vn_ref[0, :, cols].astype(BF16), bias_n)]
        qa, qb = _split_halves(q_ref[0, :, cols])
        o = _pieces_attention(qa, pieces) - lam * _pieces_attention(qb, pieces)
        o_ref[0, :, cols] = (_rms(o, hg_ref[...]) * (1.0 - lam_init)).astype(o_ref.dtype)


def _diff_sample(q3, kn3, vn3, l, kc, vc, lqk, head_g, lam_init):
    b, t, _ = q3.shape
    past = kc.shape[2]
    w = H_A * DV_A
    new = pl.BlockSpec((1, t, w), lambda bi: (l * b + bi, 0, 0))
    cache = pl.BlockSpec((None, 1, past, H_A, DV_A), lambda bi: (l, bi, 0, 0, 0))
    return pl.pallas_call(
        functools.partial(_diff_sample_kernel, lam_init=lam_init),
        out_shape=jax.ShapeDtypeStruct((b, t, w), BF16),
        grid=(b,),
        in_specs=[
            pl.BlockSpec((4, DH_A), lambda bi: (0, 0)),
            pl.BlockSpec((1, DV_A), lambda bi: (0, 0)),
            pl.BlockSpec((1, t, w), lambda bi: (bi, 0, 0)), cache, cache, new, new,
        ],
        out_specs=pl.BlockSpec((1, t, w), lambda bi: (bi, 0, 0)),
        compiler_params=_cparams(("parallel",)),
        name="diff_sample",
    )(lqk, head_g, q3, kc, vc, kn3, vn3)


QPAIR = 2 * CHUNK
WIN = BAND + QPAIR


def _relbias_kernel(tab_ref, o_ref):
    h = pl.program_id(0)
    r = lax.broadcasted_iota(jnp.int32, (QPAIR, WIN), 0)
    j = lax.broadcasted_iota(jnp.int32, (QPAIR, WIN), 1)
    rel = jnp.clip(r - j + BAND, -REL_CLIP, REL_CLIP) + REL_CLIP
    qc = r // CHUNK
    kc = j // CHUNK - BAND_CHUNKS
    allowed = (kc <= qc) & (kc >= qc - BAND_CHUNKS)

    far = BAND - REL_CLIP
    near_rel = rel[:, far:]

    near = jnp.zeros((QPAIR, WIN - far), F32)
    for t in range(2 * REL_CLIP + 1):
        near = jnp.where(near_rel == t, tab_ref[h, t], near)
    bias = jnp.concatenate([jnp.full((QPAIR, far), tab_ref[h, 2 * REL_CLIP], F32), near], axis=1)
    o_ref[0] = jnp.where(allowed, bias * LOG2E, NEG)


def _relbias(table):
    return pl.pallas_call(
        _relbias_kernel,
        out_shape=jax.ShapeDtypeStruct((H_B, QPAIR, WIN), F32),
        grid=(H_B,),
        in_specs=[pl.BlockSpec(memory_space=pltpu.SMEM)],
        out_specs=pl.BlockSpec((1, QPAIR, WIN), lambda h: (h, 0, 0)),
        compiler_params=_cparams(("arbitrary",)),
        name="relbias",
    )(table)


BAND_TQ = 512
BAND_UNROLL = 4


def _band_prompt_kernel(bias_ref, q_ref, kp_ref, kc_ref, vp_ref, vc_ref, o_ref, kcat, vcat):
    i = pl.program_id(1)
    kcat[0:BAND_TQ, :] = kp_ref[0].astype(BF16)
    kcat[BAND_TQ:, :] = kc_ref[0].astype(BF16)
    vcat[0:BAND_TQ, :] = vp_ref[0].astype(BF16)
    vcat[BAND_TQ:, :] = vc_ref[0].astype(BF16)
    lane = lax.broadcasted_iota(jnp.int32, (QPAIR, LANES), 1)

    def pair_blocks(it, carry, *, first_block):
        starts = [pl.multiple_of((it * BAND_UNROLL + u) * QPAIR, QPAIR) for u in range(BAND_UNROLL)]
        units = [(r0, hp) for r0 in starts for hp in range(H_B // 2)]
        queries = {u: q_ref[0, pl.ds(r0, QPAIR), :] for u, r0 in enumerate(starts)}
        scores, probs, outs = {}, {}, {}

        def qk(i):
            r0, hp = units[i]
            cols = slice(hp * LANES, (hp + 1) * LANES)
            qs = jnp.concatenate(_split_halves(queries[i // (H_B // 2)][:, cols]), axis=0)
            s = _nt(qs, kcat[pl.ds(r0, WIN), cols]) + bias_ref[hp]
            if first_block:
                kpos = r0 - BAND_TQ + lax.broadcasted_iota(jnp.int32, (1, WIN), 1)
                s = s + jnp.where(kpos >= 0, 0.0, NEG)
            scores[i] = s

        def softmax(i):
            s = scores.pop(i)
            e = jnp.exp2(s - jnp.max(s, axis=-1, keepdims=True))
            probs[i] = (e.astype(BF16), jnp.sum(e, axis=-1, keepdims=True))

        def pv(i):
            r0, hp = units[i]
            e, l = probs.pop(i)
            o = jnp.dot(e, vcat[pl.ds(r0, WIN), hp * LANES:(hp + 1) * LANES], preferred_element_type=F32) / l
            outs[i] = jnp.where(lane < DH_B, o[:QPAIR], o[QPAIR:])

        n = len(units)
        for step in range(n + 2):
            if step < n:
                qk(step)
            if 0 <= step - 1 < n:
                softmax(step - 1)
            if 0 <= step - 2 < n:
                pv(step - 2)
        for u, r0 in enumerate(starts):
            row = [outs[u * (H_B // 2) + hp] for hp in range(H_B // 2)]
            o_ref[0, pl.ds(r0, QPAIR), :] = jnp.concatenate(row, axis=1).astype(o_ref.dtype)
        return carry

    trips = BAND_TQ // QPAIR // BAND_UNROLL

    @pl.when(i == 0)
    def _():
        lax.fori_loop(0, trips, functools.partial(pair_blocks, first_block=True), 0)

    @pl.when(i > 0)
    def _():
        lax.fori_loop(0, trips, functools.partial(pair_blocks, first_block=False), 0)


def _band_prompt(q3, kv3, bias):
    b, t, _ = q3.shape
    assert t % BAND_TQ == 0
    w = H_B * DH_B
    cq, ck, cv = 1, 0, 1
    prev = lambda col: pl.BlockSpec((1, BAND_TQ, w), lambda bi, i: (bi, jnp.maximum(i - 1, 0), col))
    cur = lambda col: pl.BlockSpec((1, BAND_TQ, w), lambda bi, i: (bi, i, col))
    return pl.pallas_call(
        _band_prompt_kernel,
        out_shape=jax.ShapeDtypeStruct((b, t, w), BF16),
        grid=(b, t // BAND_TQ),
        in_specs=[
            pl.BlockSpec((H_B // 2, 2 * QPAIR, WIN), lambda bi, i: (0, 0, 0)),
            cur(cq), prev(ck), cur(ck), prev(cv), cur(cv),
        ],
        out_specs=pl.BlockSpec((1, BAND_TQ, w), lambda bi, i: (bi, i, 0)),
        scratch_shapes=[pltpu.VMEM((2 * BAND_TQ, w), BF16), pltpu.VMEM((2 * BAND_TQ, w), BF16)],
        compiler_params=_cparams(("parallel", "arbitrary")),
        name="band_prompt",
    )(bias.reshape(H_B // 2, 2 * QPAIR, WIN), q3, kv3, kv3, kv3, kv3)


def _band_sample_kernel(bc_ref, bn_ref, q_ref, kc_ref, vc_ref, kn_ref, vn_ref, o_ref, *, past):
    t = q_ref.shape[1]
    nband = kc_ref.shape[1]
    lane = lax.broadcasted_iota(jnp.int32, (t, LANES), 1)
    qpos_c = past + lax.broadcasted_iota(jnp.int32, (t, nband), 0)
    kpos_c = past - nband + lax.broadcasted_iota(jnp.int32, (t, nband), 1)
    qpos_n = past + lax.broadcasted_iota(jnp.int32, (t, t), 0)
    kpos_n = past + lax.broadcasted_iota(jnp.int32, (t, t), 1)

    def allowed(qpos, kpos):
        qc, kc = qpos // CHUNK, kpos // CHUNK
        return (kpos >= 0) & (kc <= qc) & (kc >= qc - BAND_CHUNKS)

    ok_c = allowed(qpos_c, kpos_c)
    ok_n = allowed(qpos_n, kpos_n)
    for hp in range(H_B // 2):
        cols = slice(hp * LANES, (hp + 1) * LANES)
        halves = []
        for sub, qm in enumerate(_split_halves(q_ref[0, :, cols])):
            hd = 2 * hp + sub
            pieces = [(kc_ref[0, :, cols].astype(BF16), vc_ref[0, :, cols].astype(BF16),
                       jnp.where(ok_c, bc_ref[hd], NEG)),
                      (kn_ref[0, :, cols].astype(BF16), vn_ref[0, :, cols].astype(BF16),
                       jnp.where(ok_n, bn_ref[hd], NEG))]
            halves.append(_pieces_attention(qm, pieces))
        o_ref[0, :, cols] = jnp.where(lane < DH_B, halves[0], halves[1]).astype(o_ref.dtype)


def _band_sample(q3, kv3, kc, vc, bias, past):
    b, t, _ = q3.shape
    nband = kc.shape[1]
    assert nband == BAND and t <= CHUNK
    w = H_B * DH_B
    bias_c = bias[:, :t, :nband]
    bias_n = bias[:, :t, nband:nband + t]
    blk = lambda col: pl.BlockSpec((1, t, w), lambda bi: (bi, 0, col))
    cache = pl.BlockSpec((1, nband, w), lambda bi: (bi, 0, 0))
    return pl.pallas_call(
        functools.partial(_band_sample_kernel, past=past),
        out_shape=jax.ShapeDtypeStruct((b, t, w), BF16),
        grid=(b,),
        in_specs=[
            pl.BlockSpec((H_B, t, nband), lambda bi: (0, 0, 0)),
            pl.BlockSpec((H_B, t, t), lambda bi: (0, 0, 0)),
            blk(1), cache, cache, blk(0), blk(1),
        ],
        out_specs=pl.BlockSpec((1, t, w), lambda bi: (bi, 0, 0)),
        compiler_params=_cparams(("parallel",)),
        name="band_sample",
    )(bias_c, bias_n, q3, kc, vc, kv3, kv3)


CONV_PAD = 8
MLSTM_L = 256
MLSTM_NB = 4


def _mlstm_kernel(cqk_ref, cv_ref, co_ref, cif_ref, conv0_ref, cw_ref, cb_ref, gb_ref, hg_ref,
                  c0_ref, n0_ref, m0_ref, hs_ref, c_ref, n_ref, m_ref, xext, *, L):
    @pl.when(pl.program_id(1) == 0)
    def _():
        xext[:, 0:CONV_PAD, :] = conv0_ref[...]
        c_ref[...] = c0_ref[...]
        n_ref[...] = n0_ref[...]
        m_ref[...] = m0_ref[...]

    lane = lax.broadcasted_iota(jnp.int32, (L, LANES), 1)
    tr = lax.broadcasted_iota(jnp.int32, (L, L), 0)
    tc = lax.broadcasted_iota(jnp.int32, (L, L), 1)
    causal = tr >= tc
    sr = lax.broadcasted_iota(jnp.int32, (8, LANES), 0)
    sc = lax.broadcasted_iota(jnp.int32, (8, LANES), 1)
    sel = jnp.where(sr < H_C, jnp.where(sc == sr, 1.0, jnp.where(sc == sr + H_C, -1.0, 0.0)), 0.0)
    er = lax.broadcasted_iota(jnp.int32, (DH_C, DH_C), 0)
    ec = lax.broadcasted_iota(jnp.int32, (DH_C, DH_C), 1)
    eye = jnp.where(er == ec, 1.0, 0.0).astype(BF16)
    for bb in range(cqk_ref.shape[0]):
        _mlstm_chunk(bb, cqk_ref, cv_ref, co_ref, cif_ref, cw_ref, cb_ref, gb_ref, hg_ref,
                     hs_ref, c_ref, n_ref, m_ref, xext, L, lane, causal, sel, eye)


def _mlstm_chunk(bb, cqk_ref, cv_ref, co_ref, cif_ref, cw_ref, cb_ref, gb_ref, hg_ref,
                 hs_ref, c_ref, n_ref, m_ref, xext, L, lane, causal, sel, eye):
    xext[bb, CONV_PAD:CONV_PAD + L, :] = cqk_ref[bb]
    base = CONV_PAD - (CONV_W - 1)
    u = 0.0
    for j in range(CONV_W):
        u = u + xext[bb, base + j:base + j + L, :] * cw_ref[j:j + 1, :]
    u = cb_ref[...] + u
    u = u * jax.nn.sigmoid(u)
    xext[bb, 0:CONV_PAD, :] = xext[bb, L:L + CONV_PAD, :]

    z = cif_ref[bb] + gb_ref[...]
    lf = jnp.minimum(z, 0.0) - jnp.log1p(jnp.exp(-jnp.abs(z)))
    b_all = jnp.dot(causal.astype(F32), lf, preferred_element_type=F32,
                    precision=lax.Precision.HIGHEST)
    rt = lax.dot_general(sel, jnp.where(lane < H_C, z, b_all), NT_DIMS,
                         preferred_element_type=F32, precision=lax.Precision.HIGHEST)

    for h in range(H_C):
        cols = slice(h * DH_C, (h + 1) * DH_C)
        q = u[:, h * DH_C:(h + 1) * DH_C]
        k = u[:, D_C + h * DH_C:D_C + (h + 1) * DH_C] * (DH_C ** -0.5)
        v = cv_ref[bb, :, cols]
        qb, kb_, vb_ = q.astype(BF16), k.astype(BF16), v.astype(BF16)
        b_col = b_all[:, H_C + h:H_C + h + 1]
        li_col = z[:, h:h + 1]
        m_prev = m_ref[bb, h:h + 1, 0:1]
        cs = c_ref[bb, h]
        ns = n_ref[bb, h:h + 1, :]

        dmat = jnp.where(causal, b_col + rt[h:h + 1, :], NEG)
        inter = b_col + m_prev
        m_t = jnp.maximum(inter, jnp.max(dmat, axis=-1, keepdims=True))
        w_intra = jnp.exp(dmat - m_t)
        w_inter = jnp.exp(inter - m_t)
        a = w_intra * _nt(qb, kb_)
        num = (jnp.dot(a.astype(BF16), vb_, preferred_element_type=F32)
               + w_inter * _nt(qb, cs.astype(BF16)))
        den = jnp.sum(a, axis=-1, keepdims=True) + w_inter * jnp.sum(q * ns, axis=-1, keepdims=True)
        hh = num / jnp.maximum(jnp.abs(den), jnp.exp(-m_t))

        b_last = b_col[L - 1:L, :]
        g = b_last - b_col + li_col
        m_new = jnp.maximum(b_last + m_prev, jnp.max(g, axis=0, keepdims=True))
        ws = jnp.exp(g - m_new)
        decay = jnp.exp(b_last + m_prev - m_new)
        vwt = _nt(eye, (ws * v).astype(BF16)).astype(BF16)
        c_ref[bb, h] = decay * cs + jnp.dot(vwt, kb_, preferred_element_type=F32)
        n_ref[bb, h:h + 1, :] = decay * ns + jnp.sum(ws * k, axis=0, keepdims=True)
        m_ref[bb, h:h + 1, :] = jnp.broadcast_to(m_new, (1, LANES))

        gate = jax.nn.sigmoid(co_ref[bb, :, cols])
        hs_ref[bb, :, cols] = (_rms(hh, hg_ref[...]) * gate).astype(hs_ref.dtype)


def _mlstm(cqk3, cvo3, cif3, conv0, conv_w, conv_b, gate_b, head_g, c0, n0, m0, L):
    b, t, _ = cqk3.shape
    nc = t // L
    nb = MLSTM_NB
    assert b % nb == 0
    blk = lambda col, w: pl.BlockSpec((nb, L, w), lambda bi, c: (bi, c, col))
    const = lambda shape: pl.BlockSpec(shape, lambda bi, c: (0,) * len(shape))
    per_b = lambda shape: pl.BlockSpec((nb,) + shape, lambda bi, c: (bi,) + (0,) * len(shape))
    return pl.pallas_call(
        functools.partial(_mlstm_kernel, L=L),
        out_shape=(jax.ShapeDtypeStruct((b, t, D_C), BF16),
                   jax.ShapeDtypeStruct((b, H_C, DH_C, DH_C), F32),
                   jax.ShapeDtypeStruct((b, H_C, DH_C), F32),
                   jax.ShapeDtypeStruct((b, H_C, LANES), F32)),
        grid=(b // nb, nc),
        in_specs=[
            blk(0, 2 * D_C), blk(0, D_C), blk(1, D_C), blk(0, LANES),
            per_b((CONV_PAD, 2 * D_C)), const((CONV_W, 2 * D_C)), const((1, 2 * D_C)),
            const((1, LANES)), const((1, DH_C)),
            per_b((H_C, DH_C, DH_C)), per_b((H_C, DH_C)), per_b((H_C, LANES)),
        ],
        out_specs=(pl.BlockSpec((nb, L, D_C), lambda bi, c: (bi, c, 0)),
                   per_b((H_C, DH_C, DH_C)), per_b((H_C, DH_C)), per_b((H_C, LANES))),
        scratch_shapes=[pltpu.VMEM((nb, CONV_PAD + L, 2 * D_C), F32)],
        compiler_params=_cparams(("parallel", "arbitrary")),
        name="mlstm",
    )(cqk3, cvo3, cvo3, cif3, conv0, conv_w, conv_b, gate_b, head_g, c0, n0, m0)


def _mix_cross_kernel(oa_ref, ob_ref, oc_ref, ga_ref, gb_ref, gc_ref, x_ref, wa_ref, wb_ref, wc_ref, wp_ref,
                      g_ref, wq_ref, mk_ref, mv_ref, wo_ref, o_ref):
    mixed = (ga_ref[0] * jnp.dot(oa_ref[0], wa_ref[...], preferred_element_type=F32)
             + gb_ref[0] * jnp.dot(ob_ref[0], wb_ref[...], preferred_element_type=F32)
             + gc_ref[0] * jnp.dot(oc_ref[0], wc_ref[...], preferred_element_type=F32))
    x = x_ref[0] + jnp.dot(mixed.astype(BF16), wp_ref[...], preferred_element_type=F32)
    h = _rms(x, g_ref[...]).astype(BF16)
    q = (jnp.dot(h, wq_ref[...], preferred_element_type=F32) * (DH_M ** -0.5 * LOG2E)).astype(BF16)
    nm = mk_ref.shape[1]
    mk = mk_ref[0].reshape(nm, H_M * DH_M).astype(BF16)
    mv = mv_ref[0].reshape(nm, H_M * DH_M).astype(BF16)
    head_cols = [slice(hd * DH_M, (hd + 1) * DH_M) for hd in range(H_M)]
    scores, probs, outs = {}, {}, {}
    for step in range(H_M + 2):
        if step < H_M:
            scores[step] = _nt(q[:, head_cols[step]], mk[:, head_cols[step]])
        if 0 <= step - 1 < H_M:
            s = scores.pop(step - 1)
            e = jnp.exp2(s - jnp.max(s, axis=-1, keepdims=True))
            probs[step - 1] = (e.astype(BF16), jnp.sum(e, axis=-1, keepdims=True))
        if 0 <= step - 2 < H_M:
            e, l = probs.pop(step - 2)
            o = jnp.dot(e, mv[:, head_cols[step - 2]], preferred_element_type=F32) / l
            outs[step - 2] = o.astype(BF16)
    o = jnp.concatenate([outs[hd] for hd in range(H_M)], axis=1)
    o_ref[0] = x + jnp.dot(o, wo_ref[...], preferred_element_type=F32)


def _mix_cross(oa, ob, oc, gates, x3, l, wa, wb, wc, wp, g, wq, mk, mv, wo):
    b, t, d = x3.shape
    tm = min(t, 512)
    nm = mk.shape[2]
    row = lambda w, col=0: pl.BlockSpec((1, tm, w), lambda bi, i: (bi, i, col))
    full = lambda a: pl.BlockSpec((None,) + a.shape[1:], lambda bi, i: (l, 0, 0))
    mem = pl.BlockSpec((None, 1, nm, H_M, DH_M), lambda bi, i: (l, bi, 0, 0, 0))
    return pl.pallas_call(
        _mix_cross_kernel,
        out_shape=jax.ShapeDtypeStruct((b, t, d), F32),
        grid=(b, t // tm),
        in_specs=[row(oa.shape[2]), row(ob.shape[2]), row(oc.shape[2]),
                  row(d, 0), row(d, 1), row(d, 2), row(d),
                  full(wa), full(wb), full(wc), full(wp),
                  pl.BlockSpec(g.shape, lambda bi, i: (0, 0)), full(wq), mem, mem, full(wo)],
        out_specs=row(d),
        compiler_params=_cparams(("parallel", "parallel")),
        name="mix_cross",
    )(oa, ob, oc, gates, gates, gates, x3, wa, wb, wc, wp, g, wq, mk, mv, wo)


FFN_TF = 1408
FFN_TM = 512


def _ffn_kernel(x_ref, g_ref, wg_ref, wu_ref, wd_ref, gf_ref, o_ref, *, final_norm):
    x = x_ref[...]
    h = _rms(x, g_ref[...]).astype(BF16)
    chunks = [slice(c, c + FFN_TF) for c in range(0, D_FF, FFN_TF)]
    pre = [(jnp.dot(h, wg_ref[:, c], preferred_element_type=F32),
            jnp.dot(h, wu_ref[:, c], preferred_element_type=F32)) for c in chunks]
    y = x
    for c, (gate, up) in zip(chunks, pre):
        a = (gate * jax.nn.sigmoid(gate) * up).astype(BF16)
        y = y + jnp.dot(a, wd_ref[c, :], preferred_element_type=F32)
    if final_norm:
        y = _rms(y, gf_ref[...])
    o_ref[...] = y


def _ffn(x, g, l, wg, wu, wd, g_final, final_norm):
    m, d = x.shape
    tm = min(m, FFN_TM)
    resident = lambda a: pl.BlockSpec((None,) + a.shape[1:], lambda i: (l, 0, 0),
                                      pipeline_mode=pl.Buffered(1))
    return pl.pallas_call(
        functools.partial(_ffn_kernel, final_norm=final_norm),
        out_shape=jax.ShapeDtypeStruct((m, d), F32),
        grid=(m // tm,),
        in_specs=[
            pl.BlockSpec((tm, d), lambda i: (i, 0)),
            pl.BlockSpec((1, d), lambda i: (0, 0)),
            resident(wg), resident(wu), resident(wd),
            pl.BlockSpec((1, d), lambda i: (0, 0)),
        ],
        out_specs=pl.BlockSpec((tm, d), lambda i: (i, 0)),
        compiler_params=_cparams(("parallel",)),
        name="ffn",
    )(x, g, wg, wu, wd, g_final)


def _layer(x3, l, w, mem_k, mem_v, cache, final_norm, akv_all):
    b, t, d = x3.shape
    x2 = x3.reshape(b * t, d)
    row = lambda a: a.reshape(1, -1)
    lam_init = 0.8 - 0.6 * math.exp(-0.3 * l)

    gates, cqk, cif, q, ak_all, av_all, akv, bkv, cvo = _in_proj(
        x2, row(w['g_mix']), l, w['w_in'], w['w_if'], *akv_all, (b, t))
    q3 = q.reshape(b, t, SEG)
    bkv3 = bkv.reshape(b, t, SEG)
    cqk3 = cqk.reshape(b, t, 2 * D_C)
    keep = t if cache is not None else min(BAND, t)
    b_k = bkv3[:, t - keep:, :HALF].reshape(b, keep, H_B, DH_B)
    b_v = bkv3[:, t - keep:, HALF:].reshape(b, keep, H_B, DH_B)

    gate_b = jnp.concatenate([w['b_i'], w['b_f'], jnp.zeros((LANES - 2 * H_C,), F32)]).reshape(1, LANES)
    if cache is None:
        oa = _diff_prompt(q3, akv.reshape(b, t, SEG), w['lqk'], row(w['a_head_g']), lam_init)
        ob = _band_prompt(q3, bkv3, w['relbias'])
        conv0 = jnp.zeros((b, CONV_PAD, 2 * D_C), F32)
        c0 = jnp.zeros((b, H_C, DH_C, DH_C), F32)
        n0 = jnp.zeros((b, H_C, DH_C), F32)
        m0 = jnp.zeros((b, H_C, LANES), F32)
        L = min(t, MLSTM_L)
    else:
        past = cache['a_k'].shape[2]
        ak3 = ak_all.reshape(DEPTH * b, t, HALF)
        av3 = av_all.reshape(DEPTH * b, t, HALF)
        oa = _diff_sample(q3, ak3, av3, l, cache['a_k'], cache['a_v'],
                          w['lqk'], row(w['a_head_g']), lam_init)
        nband = cache['b_k'].shape[1]
        ob = _band_sample(q3, bkv3, cache['b_k'].reshape(b, nband, -1), cache['b_v'].reshape(b, nband, -1),
                          w['relbias'], past)
        conv0 = jnp.pad(cache['conv'], ((0, 0), (CONV_PAD - (CONV_W - 1), 0), (0, 0)))
        c0, n0 = cache['C'], cache['n']
        m0 = jnp.broadcast_to(cache['m'][:, :, None], (b, H_C, LANES))
        L = t
    oc, c_new, n_new, m_new = _mlstm(cqk3, cvo.reshape(b, t, SEG), cif.reshape(b, t, LANES), conv0,
                                     w['conv_w'], row(w['conv_b']), gate_b, row(w['c_head_g']), c0, n0, m0, L)
    assert t >= CONV_W - 1
    conv_new = cqk3[:, t - (CONV_W - 1):]

    x3 = _mix_cross(oa, ob, oc, gates.reshape(b, t, 3 * d), x3, l,
                    w['w_up_a'], w['w_up_b'], w['w_up_c'], w['w_o'],
                    row(w['g_cross']), w['w_mq'], mem_k, mem_v, w['w_mo'])
    x2 = _ffn(x3.reshape(b * t, d), row(w['g_ffn']), l, w['w_ff_g'], w['w_ff_u'], w['w_ff_d'],
              row(w['g_final']), final_norm)
    return x2.reshape(b, t, d), (ak_all, av_all), (b_k, b_v, c_new, n_new, m_new[:, :, 0], conv_new)


def kernel(x_prompt, x_sample, cache_a_k, cache_a_v, cache_b_k, cache_b_v, state_c_C, state_c_n, state_c_m, state_c_conv, cache_mem_k, cache_mem_v, mem_prompt, g_mix, w_in, a_lq1, a_lk1, a_lq2, a_lk2, a_head_g, b_rel, c_conv_w, c_conv_b, c_b_i, c_b_f, c_head_g, w_up_a, w_up_b, w_up_c, w_o, g_cross, w_mq, w_mk, w_mv, w_mo, g_ffn, w_ff_g, w_ff_u, w_ff_d, g_final):
    xp, xs = x_prompt, x_sample
    bp = x_prompt.shape[0]
    new_p = [[] for _ in range(6)]
    new_s = [[] for _ in range(6)]
    akv_p = akv_s = (None, None)
    bf = lambda a: a.astype(BF16)
    wp, wif = _prep_w_in(w_in)
    wts = dict(w_up_a=bf(w_up_a), w_up_b=bf(w_up_b), w_up_c=bf(w_up_c), w_o=bf(w_o), w_mq=bf(w_mq),
               w_mo=bf(w_mo), w_ff_g=bf(w_ff_g), w_ff_u=bf(w_ff_u), w_ff_d=bf(w_ff_d))
    mem_k = _mem_proj(mem_prompt, bf(w_mk))
    mem_v = _mem_proj(mem_prompt, bf(w_mv))
    for l in range(DEPTH):
        w = dict(g_mix=g_mix[l], w_in=wp, w_if=wif,
                 lqk=jnp.stack([a_lq1[l], a_lk1[l], a_lq2[l], a_lk2[l]]),
                 a_head_g=a_head_g[l], relbias=_relbias(b_rel[l]), conv_w=c_conv_w[l], conv_b=c_conv_b[l],
                 b_i=c_b_i[l], b_f=c_b_f[l], c_head_g=c_head_g[l],
                 g_cross=g_cross[l], g_ffn=g_ffn[l], g_final=g_final, **wts)
        last = l == DEPTH - 1
        xp, akv_p, st_p = _layer(xp, l, w, mem_k, mem_v, None, last, akv_p)
        cache = dict(a_k=cache_a_k, a_v=cache_a_v, b_k=cache_b_k[l], b_v=cache_b_v[l],
                     C=state_c_C[l], n=state_c_n[l], m=state_c_m[l], conv=state_c_conv[l])
        xs, akv_s, st_s = _layer(xs, l, w, cache_mem_k, cache_mem_v, cache, last, akv_s)
        for lst, a in zip(new_p, st_p):
            lst.append(a)
        for lst, a in zip(new_s, st_s):
            lst.append(a)
    heads = lambda a, b, t: a.reshape(DEPTH, b, t, H_A, DV_A)
    outs_p = ([heads(a, bp, x_prompt.shape[1]) for a in akv_p] + [jnp.stack(a, 0) for a in new_p]
              + [mem_k, mem_v])
    outs_s = [heads(a, xs.shape[0], xs.shape[1]) for a in akv_s] + [jnp.stack(a, 0) for a in new_s]
    return (xp, xs) + tuple(outs_p) + tuple(outs_s)
```

```python
import functools
import math

import jax
import jax.numpy as jnp
from jax import lax
from jax.experimental import pallas as pl
from jax.experimental.pallas import tpu as pltpu

F32 = jnp.float32
BF16 = jnp.bfloat16

D_MODEL = 1024
DEPTH = 2
CHUNK = 64
EPS = 1e-6
NEG = -1e30
LOG2E = math.log2(math.e)
H_A = 4
DH_A = 64
DV_A = 2 * DH_A
H_B = 8
DH_B = 64
BAND_CHUNKS = 8
BAND = BAND_CHUNKS * CHUNK
REL_CLIP = 128
H_C = 4
DH_C = 128
D_C = H_C * DH_C
CONV_W = 4
H_M = 4
DH_M = D_MODEL // H_M
D_FF = -(-8 * D_MODEL // (3 * 256)) * 256

LANES = 128
VMEM_LIMIT = 48 * 1024 * 1024

SEG = 1024
HALF = SEG // 2
GATE_TILES = 3
N_TILES = 8
IN_SIZES = (H_A * DH_A,) * 4 + (H_A * DV_A,) + (H_B * DH_B,) * 3 + (2 * D_C, D_C, D_C, 2 * H_C, 3 * D_MODEL)

ALIBI_SLOPES = tuple(2.0 ** (-8.0 * (i + 1) / H_A) for i in range(H_A))

NT_DIMS = (((1,), (1,)), ((), ()))


def _cparams(sem):
    return pltpu.CompilerParams(dimension_semantics=sem, vmem_limit_bytes=VMEM_LIMIT)


def _rms(x, g):
    ms = jnp.mean(x * x, axis=-1, keepdims=True)
    return x * lax.rsqrt(ms + EPS) * g


def _nt(a, b):
    return lax.dot_general(a, b, NT_DIMS, preferred_element_type=F32)


def _prep_w_in_kernel(w_ref, o_ref, oif_ref):
    offs = [0]
    for size in IN_SIZES:
        offs.append(offs[-1] + size)
    seg = lambda i: w_ref[:, offs[i]:offs[i + 1]]
    a_q1, a_q2, a_k1, a_k2, a_v, b_q, b_k, b_v, c_qk, c_v, c_o = (seg(i) for i in range(11))
    g = seg(12)

    def put(col, val):
        o_ref[:, col:col + val.shape[1]] = val.astype(BF16)

    put(0, g)
    put(GATE_TILES * SEG, c_qk)
    for h in range(H_A):
        hs = slice(h * DH_A, (h + 1) * DH_A)
        base = h * 2 * DH_A
        put(4 * SEG + base, a_q1[:, hs] * (DH_A ** -0.5))
        put(4 * SEG + base + DH_A, a_q2[:, hs] * (DH_A ** -0.5))
        put(5 * SEG + base, a_k1[:, hs])
        put(5 * SEG + base + DH_A, a_k2[:, hs])
    put(4 * SEG + HALF, b_q * (DH_B ** -0.5))
    put(5 * SEG + HALF, a_v)
    put(6 * SEG, b_k)
    put(6 * SEG + HALF, b_v)
    put(7 * SEG, c_v)
    put(7 * SEG + HALF, c_o)
    tile = w_ref[:, offs[11]:offs[11] + LANES]
    lane = lax.broadcasted_iota(jnp.int32, tile.shape, 1)
    oif_ref[...] = jnp.where(lane < 2 * H_C, tile, 0.0).astype(BF16)


def _prep_w_in(w):
    depth, k, n = w.shape
    tk = 256
    n_out = N_TILES * SEG
    return pl.pallas_call(
        _prep_w_in_kernel,
        out_shape=(jax.ShapeDtypeStruct((depth, k, n_out), BF16),
                   jax.ShapeDtypeStruct((depth, k, LANES), BF16)),
        grid=(depth, k // tk),
        in_specs=[pl.BlockSpec((None, tk, n), lambda l, i: (l, i, 0))],
        out_specs=(pl.BlockSpec((None, tk, n_out), lambda l, i: (l, i, 0)),
                   pl.BlockSpec((None, tk, LANES), lambda l, i: (l, i, 0))),
        compiler_params=_cparams(("parallel", "parallel")),
        name="prep_w_in",
    )(w)


IN_PROJ_TM = 256


def _in_proj_kernel(*refs, aliased, heads5d):
    x_ref, g_ref, w_ref, wif_ref = refs[:4]
    og_ref, ocqk_ref, ocif_ref, oq_ref, oak_ref, oav_ref, oakv_ref, obkv_ref, ocvo_ref = refs[6 if aliased else 4:]
    h = _rms(x_ref[...], g_ref[...]).astype(BF16)
    cols = lambda j, lo=0, hi=SEG: jnp.dot(h, w_ref[:, j * SEG + lo:j * SEG + hi], preferred_element_type=F32)

    ocif_ref[...] = jnp.dot(h, wif_ref[...], preferred_element_type=F32)
    for j in range(GATE_TILES):
        og_ref[:, j * SEG:(j + 1) * SEG] = jax.nn.sigmoid(cols(j)).astype(og_ref.dtype)
    ocqk_ref[...] = cols(GATE_TILES)
    oq_ref[...] = (cols(4) * LOG2E).astype(oq_ref.dtype)
    for out_ref, off in ((oak_ref, 0), (oav_ref, HALF)):
        a = cols(5, off, off + HALF)
        oakv_ref[:, off:off + HALF] = a.astype(oakv_ref.dtype)
        if heads5d:
            out_ref[0, 0] = a.reshape(a.shape[0], H_A, DV_A)
        else:
            out_ref[0] = a
    obkv_ref[...] = cols(6)
    ocvo_ref[...] = cols(7)


def _in_proj(x, g, l, wp, wif, ak_all, av_all, bt):
    m, k = x.shape
    tm = min(m, IN_PROJ_TM)
    aliased = ak_all is not None
    b, t = bt
    heads5d = t % tm == 0
    if heads5d:
        stacked = jax.ShapeDtypeStruct((DEPTH, b, t, H_A, DV_A), F32)
        layer_row = pl.BlockSpec((1, 1, tm, H_A, DV_A), lambda i: (l, i // (t // tm), i % (t // tm), 0, 0))
    else:
        stacked = jax.ShapeDtypeStruct((DEPTH, m, HALF), F32)
        layer_row = pl.BlockSpec((1, tm, HALF), lambda i: (l, i, 0))
    resident = lambda a: pl.BlockSpec((None,) + a.shape[1:], lambda i: (l, 0, 0), pipeline_mode=pl.Buffered(1))
    in_specs = [pl.BlockSpec((tm, k), lambda i: (i, 0)), pl.BlockSpec((1, k), lambda i: (0, 0)),
                resident(wp), resident(wif)]
    args = [x, g, wp, wif]
    if aliased:
        in_specs += [pl.BlockSpec(memory_space=pl.ANY)] * 2
        args += [ak_all, av_all]
    row = lambda w: pl.BlockSpec((tm, w), lambda i: (i, 0))
    sds = jax.ShapeDtypeStruct
    return pl.pallas_call(
        functools.partial(_in_proj_kernel, aliased=aliased, heads5d=heads5d),
        out_shape=(sds((m, GATE_TILES * SEG), BF16), sds((m, SEG), F32), sds((m, LANES), F32),
                   sds((m, SEG), BF16), stacked, stacked, sds((m, SEG), BF16),
                   sds((m, SEG), F32), sds((m, SEG), F32)),
        grid=(m // tm,),
        in_specs=in_specs,
        out_specs=(row(GATE_TILES * SEG), row(SEG), row(LANES), row(SEG), layer_row, layer_row, row(SEG),
                   row(SEG), row(SEG)),
        input_output_aliases={4: 4, 5: 5} if aliased else {},
        compiler_params=_cparams(("parallel",)),
        name="in_proj",
    )(*args)


def _mem_proj_kernel(x_ref, w_ref, o_ref):
    y = jnp.dot(x_ref[0].astype(BF16), w_ref[...], preferred_element_type=F32)
    o_ref[...] = y.reshape(o_ref.shape)


def _mem_proj(mem, w):
    b, nm, k = mem.shape
    depth, _, n = w.shape
    return pl.pallas_call(
        _mem_proj_kernel,
        out_shape=jax.ShapeDtypeStruct((depth, b, nm, H_M, DH_M), F32),
        grid=(depth, b),
        in_specs=[pl.BlockSpec((1, nm, k), lambda l, bi: (bi, 0, 0)),
                  pl.BlockSpec((None, k, n), lambda l, bi: (l, 0, 0))],
        out_specs=pl.BlockSpec((None, None, nm, H_M, DH_M), lambda l, bi: (l, bi, 0, 0, 0)),
        compiler_params=_cparams(("parallel", "parallel")),
        name="mem_proj",
    )(mem, w)


def _online_updates(jobs):
    scores, staged = {}, {}

    def softmax(i):
        s, (acc, m, l) = scores.pop(i), jobs[i][2]
        slabs = [s[:, j * LANES:(j + 1) * LANES] for j in range(s.shape[1] // LANES)]
        m_old = m[...]
        m_new = jnp.maximum(m_old, jnp.max(functools.reduce(jnp.maximum, slabs), axis=-1, keepdims=True))
        alpha = jnp.exp2(m_old - m_new)
        ps = [jnp.exp2(c - m_new) for c in slabs]
        l[...] = alpha * l[...] + jnp.sum(functools.reduce(jnp.add, ps), axis=-1, keepdims=True)
        m[...] = m_new
        staged[i] = (alpha, jnp.concatenate(ps, axis=1).astype(BF16))

    def pv(i):
        (alpha, p), (_, v, (acc, m, l)) = staged.pop(i), jobs[i]
        acc[...] = alpha * acc[...] + jnp.dot(p, v, preferred_element_type=F32)

    n = len(jobs)
    for step in range(n + 2):
        if step < n:
            scores[step] = jobs[step][0]()
        if 0 <= step - 1 < n:
            softmax(step - 1)
        if 0 <= step - 2 < n:
            pv(step - 2)


def _split_halves(q):
    lane = lax.broadcasted_iota(jnp.int32, q.shape, 1)
    lo = jnp.where(lane < DH_A, q, 0.0).astype(BF16)
    hi = jnp.where(lane >= DH_A, q, 0.0).astype(BF16)
    return lo, hi


def _lambda(lqk, lam_init):
    e1 = jnp.exp(jnp.sum(lqk[0:1, :] * lqk[1:2, :], axis=-1, keepdims=True))
    e2 = jnp.exp(jnp.sum(lqk[2:3, :] * lqk[3:4, :], axis=-1, keepdims=True))
    return e1 - e2 + lam_init


DIFF_HEADS = 4


def _diff_prompt_kernel(slopes_ref, lqk_ref, hg_ref, q_ref, k_ref, v_ref, o_ref,
                        diag_bias, acc, m, l, *, tq, nh, lam_init):
    hg = pl.program_id(1)
    qi = pl.program_id(2)
    slopes = [slopes_ref[hg * nh + i] * LOG2E for i in range(nh)]
    q0 = pl.multiple_of(qi * tq, tq)
    head_cols = [slice(i * LANES, (i + 1) * LANES) for i in range(nh)]

    @pl.when(qi == 0)
    def _():
        r = lax.broadcasted_iota(jnp.int32, (tq, tq), 0)
        c = lax.broadcasted_iota(jnp.int32, (tq, tq), 1)
        rel = (r - jnp.abs(r - c)).astype(F32)
        allowed = (c // CHUNK) <= (r // CHUNK)
        for i in range(nh):
            diag_bias[i] = jnp.where(allowed, slopes[i] * rel, NEG)

    m[...] = jnp.full(m.shape, NEG, F32)
    l[...] = jnp.zeros(l.shape, F32)
    acc[...] = jnp.zeros(acc.shape, F32)
    chains = []
    for i in range(nh):
        for j, qm in enumerate(_split_halves(q_ref[0, :, head_cols[i]])):
            c = 2 * i + j
            chains.append((qm, i, (acc.at[c], m.at[c], l.at[c])))

    score = lambda qm, k, bias: _nt(qm, k) + bias

    def full_tiles(k_starts):
        jobs = []
        for k0 in k_starts:
            kpos = (k0 - q0) + lax.broadcasted_iota(jnp.int32, (1, tq), 1)
            kposf = kpos.astype(F32)
            for qm, i, st in chains:
                k = k_ref[0, pl.ds(k0, tq), head_cols[i]]
                v = v_ref[0, pl.ds(k0, tq), head_cols[i]]
                jobs.append((functools.partial(score, qm, k, slopes[i] * kposf), v, st))
        _online_updates(jobs)

    def tile_pair(it, carry):
        k0 = pl.multiple_of(it * (2 * tq), 2 * tq)
        full_tiles([k0, pl.multiple_of(k0 + tq, tq)])
        return carry

    lax.fori_loop(0, qi // 2, tile_pair, 0)

    @pl.when(qi % 2 == 1)
    def _():
        full_tiles([pl.multiple_of((qi - 1) * tq, tq)])

    half = tq // 2
    jobs = []
    for rows, nk in ((slice(0, half), half), (slice(half, tq), tq)):
        for qm, i, st in chains:
            k = k_ref[0, pl.ds(q0, nk), head_cols[i]]
            v = v_ref[0, pl.ds(q0, nk), head_cols[i]]
            jobs.append((functools.partial(score, qm[rows], k, diag_bias[i, rows, 0:nk]), v,
                         tuple(ref.at[rows] for ref in st)))
    _online_updates(jobs)

    lam = _lambda(lqk_ref[...], lam_init)
    for i in range(nh):
        o = acc[2 * i] / l[2 * i] - lam * (acc[2 * i + 1] / l[2 * i + 1])
        o_ref[0, :, head_cols[i]] = (_rms(o, hg_ref[...]) * (1.0 - lam_init)).astype(o_ref.dtype)


def _diff_prompt(q3, kv3, lqk, head_g, lam_init):
    b, t, _ = q3.shape
    tq = min(t, 512)
    nh = DIFF_HEADS
    w = nh * LANES
    slopes = jnp.asarray(ALIBI_SLOPES, F32)
    return pl.pallas_call(
        functools.partial(_diff_prompt_kernel, tq=tq, nh=nh, lam_init=lam_init),
        out_shape=jax.ShapeDtypeStruct((b, t, H_A * DV_A), BF16),
        grid=(b, H_A // nh, t // tq),
        in_specs=[
            pl.BlockSpec(memory_space=pltpu.SMEM),
            pl.BlockSpec((4, DH_A), lambda bi, h, qi: (0, 0)),
            pl.BlockSpec((1, DV_A), lambda bi, h, qi: (0, 0)),
            pl.BlockSpec((1, tq, w), lambda bi, h, qi: (bi, qi, h)),
            pl.BlockSpec((1, t, w), lambda bi, h, qi: (bi, 0, h)),
            pl.BlockSpec((1, t, w), lambda bi, h, qi: (bi, 0, H_A // nh + h)),
        ],
        out_specs=pl.BlockSpec((1, tq, w), lambda bi, h, qi: (bi, qi, h)),
        scratch_shapes=[
            pltpu.VMEM((nh, tq, tq), F32),
            pltpu.VMEM((2 * nh, tq, LANES), F32), pltpu.VMEM((2 * nh, tq, LANES), F32),
            pltpu.VMEM((2 * nh, tq, LANES), F32),
        ],
        compiler_params=_cparams(("parallel", "parallel", "arbitrary")),
        name="diff_prompt",
    )(slopes, lqk, head_g, q3, kv3, kv3)


def _pieces_attention(qm, pieces):
    ss = [_nt(qm, k) + bias for k, _, bias in pieces]
    m = functools.reduce(jnp.maximum, [jnp.max(s, axis=-1, keepdims=True) for s in ss])
    es = [jnp.exp2(s - m) for s in ss]
    l = functools.reduce(jnp.add, [jnp.sum(e, axis=-1, keepdims=True) for e in es])
    o = functools.reduce(jnp.add, [jnp.dot(e.astype(BF16), v, preferred_element_type=F32)
                                   for e, (_, v, _) in zip(es, pieces)])
    return o / l


def _diff_sample_kernel(lqk_ref, hg_ref, q_ref, kc_ref, vc_ref, kn_ref, vn_ref, o_ref, *, lam_init):
    t = q_ref.shape[1]
    past = kc_ref.shape[1]
    lam = _lambda(lqk_ref[...], lam_init)
    rc = lax.broadcasted_iota(jnp.int32, (t, past), 0)
    cc = lax.broadcasted_iota(jnp.int32, (t, past), 1)
    rn = lax.broadcasted_iota(jnp.int32, (t, t), 0)
    cn = lax.broadcasted_iota(jnp.int32, (t, t), 1)
    dist_c = jnp.abs(past + rc - cc).astype(F32)
    dist_n = jnp.abs(rn - cn).astype(F32)
    ok_c = (cc // CHUNK) <= ((past + rc) // CHUNK)
    ok_n = ((past + cn) // CHUNK) <= ((past + rn) // CHUNK)
    kc = kc_ref[0].reshape(past, H_A * DV_A).astype(BF16)
    vc = vc_ref[0].reshape(past, H_A * DV_A).astype(BF16)
    for h in range(H_A):
        cols = slice(h * LANES, (h + 1) * LANES)
        bias_c = jnp.where(ok_c, (-ALIBI_SLOPES[h] * LOG2E) * dist_c, NEG)
        bias_n = jnp.where(ok_n, (-ALIBI_SLOPES[h] * LOG2E) * dist_n, NEG)
        pieces = [(kc[:, cols], vc[:, cols], bias_c),
                  (kn_ref[0, :, cols].astype(BF16), vn_ref[0, :, cols].astype(BF16), bias_n)]
        qa, qb = _split_halves(q_ref[0, :, cols])
        o = _pieces_attention(qa, pieces) - lam * _pieces_attention(qb, pieces)
        o_ref[0, :, cols] = (_rms(o, hg_ref[...]) * (1.0 - lam_init)).astype(o_ref.dtype)


def _diff_sample(q3, kn3, vn3, l, kc, vc, lqk, head_g, lam_init):
    b, t, _ = q3.shape
    past = kc.shape[2]
    w = H_A * DV_A
    new = pl.BlockSpec((1, t, w), lambda bi: (l * b + bi, 0, 0))
    cache = pl.BlockSpec((None, 1, past, H_A, DV_A), lambda bi: (l, bi, 0, 0, 0))
    return pl.pallas_call(
        functools.partial(_diff_sample_kernel, lam_init=lam_init),
        out_shape=jax.ShapeDtypeStruct((b, t, w), BF16),
        grid=(b,),
        in_specs=[
            pl.BlockSpec((4, DH_A), lambda bi: (0, 0)),
            pl.BlockSpec((1, DV_A), lambda bi: (0, 0)),
            pl.BlockSpec((1, t, w), lambda bi: (bi, 0, 0)), cache, cache, new, new,
        ],
        out_specs=pl.BlockSpec((1, t, w), lambda bi: (bi, 0, 0)),
        compiler_params=_cparams(("parallel",)),
        name="diff_sample",
    )(lqk, head_g, q3, kc, vc, kn3, vn3)


QPAIR = 2 * CHUNK
WIN = BAND + QPAIR


def _relbias_kernel(tab_ref, o_ref):
    h = pl.program_id(0)
    r = lax.broadcasted_iota(jnp.int32, (QPAIR, WIN), 0)
    j = lax.broadcasted_iota(jnp.int32, (QPAIR, WIN), 1)
    rel = jnp.clip(r - j + BAND, -REL_CLIP, REL_CLIP) + REL_CLIP
    qc = r // CHUNK
    kc = j // CHUNK - BAND_CHUNKS
    allowed = (kc <= qc) & (kc >= qc - BAND_CHUNKS)

    far = BAND - REL_CLIP
    near_rel = rel[:, far:]

    near = jnp.zeros((QPAIR, WIN - far), F32)
    for t in range(2 * REL_CLIP + 1):
        near = jnp.where(near_rel == t, tab_ref[h, t], near)
    bias = jnp.concatenate([jnp.full((QPAIR, far), tab_ref[h, 2 * REL_CLIP], F32), near], axis=1)
    o_ref[0] = jnp.where(allowed, bias * LOG2E, NEG)


def _relbias(table):
    return pl.pallas_call(
        _relbias_kernel,
        out_shape=jax.ShapeDtypeStruct((H_B, QPAIR, WIN), F32),
        grid=(H_B,),
        in_specs=[pl.BlockSpec(memory_space=pltpu.SMEM)],
        out_specs=pl.BlockSpec((1, QPAIR, WIN), lambda h: (h, 0, 0)),
        compiler_params=_cparams(("arbitrary",)),
        name="relbias",
    )(table)


BAND_TQ = 512
BAND_UNROLL = 4


def _band_prompt_kernel(bias_ref, q_ref, kp_ref, kc_ref, vp_ref, vc_ref, o_ref, kcat, vcat):
    i = pl.program_id(1)
    kcat[0:BAND_TQ, :] = kp_ref[0].astype(BF16)
    kcat[BAND_TQ:, :] = kc_ref[0].astype(BF16)
    vcat[0:BAND_TQ, :] = vp_ref[0].astype(BF16)
    vcat[BAND_TQ:, :] = vc_ref[0].astype(BF16)
    lane = lax.broadcasted_iota(jnp.int32, (QPAIR, LANES), 1)

    def pair_blocks(it, carry, *, first_block):
        starts = [pl.multiple_of((it * BAND_UNROLL + u) * QPAIR, QPAIR) for u in range(BAND_UNROLL)]
        units = [(r0, hp) for r0 in starts for hp in range(H_B // 2)]
        queries = {u: q_ref[0, pl.ds(r0, QPAIR), :] for u, r0 in enumerate(starts)}
        scores, probs, outs = {}, {}, {}

        def qk(i):
            r0, hp = units[i]
            cols = slice(hp * LANES, (hp + 1) * LANES)
            qs = jnp.concatenate(_split_halves(queries[i // (H_B // 2)][:, cols]), axis=0)
            s = _nt(qs, kcat[pl.ds(r0, WIN), cols]) + bias_ref[hp]
            if first_block:
                kpos = r0 - BAND_TQ + lax.broadcasted_iota(jnp.int32, (1, WIN), 1)
                s = s + jnp.where(kpos >= 0, 0.0, NEG)
            scores[i] = s

        def softmax(i):
            s = scores.pop(i)
            e = jnp.exp2(s - jnp.max(s, axis=-1, keepdims=True))
            probs[i] = (e.astype(BF16), jnp.sum(e, axis=-1, keepdims=True))

        def pv(i):
            r0, hp = units[i]
            e, l = probs.pop(i)
            o = jnp.dot(e, vcat[pl.ds(r0, WIN), hp * LANES:(hp + 1) * LANES], preferred_element_type=F32) / l
            outs[i] = jnp.where(lane < DH_B, o[:QPAIR], o[QPAIR:])

        n = len(units)
        for step in range(n + 2):
            if step < n:
                qk(step)
            if 0 <= step - 1 < n:
                softmax(step - 1)
            if 0 <= step - 2 < n:
                pv(step - 2)
        for u, r0 in enumerate(starts):
            row = [outs[u * (H_B // 2) + hp] for hp in range(H_B // 2)]
            o_ref[0, pl.ds(r0, QPAIR), :] = jnp.concatenate(row, axis=1).astype(o_ref.dtype)
        return carry

    trips = BAND_TQ // QPAIR // BAND_UNROLL

    @pl.when(i == 0)
    def _():
        lax.fori_loop(0, trips, functools.partial(pair_blocks, first_block=True), 0)

    @pl.when(i > 0)
    def _():
        lax.fori_loop(0, trips, functools.partial(pair_blocks, first_block=False), 0)


def _band_prompt(q3, kv3, bias):
    b, t, _ = q3.shape
    assert t % BAND_TQ == 0
    w = H_B * DH_B
    cq, ck, cv = 1, 0, 1
    prev = lambda col: pl.BlockSpec((1, BAND_TQ, w), lambda bi, i: (bi, jnp.maximum(i - 1, 0), col))
    cur = lambda col: pl.BlockSpec((1, BAND_TQ, w), lambda bi, i: (bi, i, col))
    return pl.pallas_call(
        _band_prompt_kernel,
        out_shape=jax.ShapeDtypeStruct((b, t, w), BF16),
        grid=(b, t // BAND_TQ),
        in_specs=[
            pl.BlockSpec((H_B // 2, 2 * QPAIR, WIN), lambda bi, i: (0, 0, 0)),
            cur(cq), prev(ck), cur(ck), prev(cv), cur(cv),
        ],
        out_specs=pl.BlockSpec((1, BAND_TQ, w), lambda bi, i: (bi, i, 0)),
        scratch_shapes=[pltpu.VMEM((2 * BAND_TQ, w), BF16), pltpu.VMEM((2 * BAND_TQ, w), BF16)],
        compiler_params=_cparams(("parallel", "arbitrary")),
        name="band_prompt",
    )(bias.reshape(H_B // 2, 2 * QPAIR, WIN), q3, kv3, kv3, kv3, kv3)


def _band_sample_kernel(bc_ref, bn_ref, q_ref, kc_ref, vc_ref, kn_ref, vn_ref, o_ref, *, past):
    t = q_ref.shape[1]
    nband = kc_ref.shape[1]
    lane = lax.broadcasted_iota(jnp.int32, (t, LANES), 1)
    qpos_c = past + lax.broadcasted_iota(jnp.int32, (t, nband), 0)
    kpos_c = past - nband + lax.broadcasted_iota(jnp.int32, (t, nband), 1)
    qpos_n = past + lax.broadcasted_iota(jnp.int32, (t, t), 0)
    kpos_n = past + lax.broadcasted_iota(jnp.int32, (t, t), 1)

    def allowed(qpos, kpos):
        qc, kc = qpos // CHUNK, kpos // CHUNK
        return (kpos >= 0) & (kc <= qc) & (kc >= qc - BAND_CHUNKS)

    ok_c = allowed(qpos_c, kpos_c)
    ok_n = allowed(qpos_n, kpos_n)
    for hp in range(H_B // 2):
        cols = slice(hp * LANES, (hp + 1) * LANES)
        halves = []
        for sub, qm in enumerate(_split_halves(q_ref[0, :, cols])):
            hd = 2 * hp + sub
            pieces = [(kc_ref[0, :, cols].astype(BF16), vc_ref[0, :, cols].astype(BF16),
                       jnp.where(ok_c, bc_ref[hd], NEG)),
                      (kn_ref[0, :, cols].astype(BF16), vn_ref[0, :, cols].astype(BF16),
                       jnp.where(ok_n, bn_ref[hd], NEG))]
            halves.append(_pieces_attention(qm, pieces))
        o_ref[0, :, cols] = jnp.where(lane < DH_B, halves[0], halves[1]).astype(o_ref.dtype)


def _band_sample(q3, kv3, kc, vc, bias, past):
    b, t, _ = q3.shape
    nband = kc.shape[1]
    assert nband == BAND and t <= CHUNK
    w = H_B * DH_B
    bias_c = bias[:, :t, :nband]
    bias_n = bias[:, :t, nband:nband + t]
    blk = lambda col: pl.BlockSpec((1, t, w), lambda bi: (bi, 0, col))
    cache = pl.BlockSpec((1, nband, w), lambda bi: (bi, 0, 0))
    return pl.pallas_call(
        functools.partial(_band_sample_kernel, past=past),
        out_shape=jax.ShapeDtypeStruct((b, t, w), BF16),
        grid=(b,),
        in_specs=[
            pl.BlockSpec((H_B, t, nband), lambda bi: (0, 0, 0)),
            pl.BlockSpec((H_B, t, t), lambda bi: (0, 0, 0)),
            blk(1), cache, cache, blk(0), blk(1),
        ],
        out_specs=pl.BlockSpec((1, t, w), lambda bi: (bi, 0, 0)),
        compiler_params=_cparams(("parallel",)),
        name="band_sample",
    )(bias_c, bias_n, q3, kc, vc, kv3, kv3)


CONV_PAD = 8
MLSTM_L = 256
MLSTM_NB = 4


def _mlstm_kernel(cqk_ref, cv_ref, co_ref, cif_ref, conv0_ref, cw_ref, cb_ref, gb_ref, hg_ref,
                  c0_ref, n0_ref, m0_ref, hs_ref, c_ref, n_ref, m_ref, xext, *, L):
    @pl.when(pl.program_id(1) == 0)
    def _():
        xext[:, 0:CONV_PAD, :] = conv0_ref[...]
        c_ref[...] = c0_ref[...]
        n_ref[...] = n0_ref[...]
        m_ref[...] = m0_ref[...]

    lane = lax.broadcasted_iota(jnp.int32, (L, LANES), 1)
    tr = lax.broadcasted_iota(jnp.int32, (L, L), 0)
    tc = lax.broadcasted_iota(jnp.int32, (L, L), 1)
    causal = tr >= tc
    sr = lax.broadcasted_iota(jnp.int32, (8, LANES), 0)
    sc = lax.broadcasted_iota(jnp.int32, (8, LANES), 1)
    sel = jnp.where(sr < H_C, jnp.where(sc == sr, 1.0, jnp.where(sc == sr + H_C, -1.0, 0.0)), 0.0)
    er = lax.broadcasted_iota(jnp.int32, (DH_C, DH_C), 0)
    ec = lax.broadcasted_iota(jnp.int32, (DH_C, DH_C), 1)
    eye = jnp.where(er == ec, 1.0, 0.0).astype(BF16)
    for bb in range(cqk_ref.shape[0]):
        _mlstm_chunk(bb, cqk_ref, cv_ref, co_ref, cif_ref, cw_ref, cb_ref, gb_ref, hg_ref,
                     hs_ref, c_ref, n_ref, m_ref, xext, L, lane, causal, sel, eye)


def _mlstm_chunk(bb, cqk_ref, cv_ref, co_ref, cif_ref, cw_ref, cb_ref, gb_ref, hg_ref,
                 hs_ref, c_ref, n_ref, m_ref, xext, L, lane, causal, sel, eye):
    xext[bb, CONV_PAD:CONV_PAD + L, :] = cqk_ref[bb]
    base = CONV_PAD - (CONV_W - 1)
    u = 0.0
    for j in range(CONV_W):
        u = u + xext[bb, base + j:base + j + L, :] * cw_ref[j:j + 1, :]
    u = cb_ref[...] + u
    u = u * jax.nn.sigmoid(u)
    xext[bb, 0:CONV_PAD, :] = xext[bb, L:L + CONV_PAD, :]

    z = cif_ref[bb] + gb_ref[...]
    lf = jnp.minimum(z, 0.0) - jnp.log1p(jnp.exp(-jnp.abs(z)))
    b_all = jnp.dot(causal.astype(F32), lf, preferred_element_type=F32,
                    precision=lax.Precision.HIGHEST)
    rt = lax.dot_general(sel, jnp.where(lane < H_C, z, b_all), NT_DIMS,
                         preferred_element_type=F32, precision=lax.Precision.HIGHEST)

    for h in range(H_C):
        cols = slice(h * DH_C, (h + 1) * DH_C)
        q = u[:, h * DH_C:(h + 1) * DH_C]
        k = u[:, D_C + h * DH_C:D_C + (h + 1) * DH_C] * (DH_C ** -0.5)
        v = cv_ref[bb, :, cols]
        qb, kb_, vb_ = q.astype(BF16), k.astype(BF16), v.astype(BF16)
        b_col = b_all[:, H_C + h:H_C + h + 1]
        li_col = z[:, h:h + 1]
        m_prev = m_ref[bb, h:h + 1, 0:1]
        cs = c_ref[bb, h]
        ns = n_ref[bb, h:h + 1, :]

        dmat = jnp.where(causal, b_col + rt[h:h + 1, :], NEG)
        inter = b_col + m_prev
        m_t = jnp.maximum(inter, jnp.max(dmat, axis=-1, keepdims=True))
        w_intra = jnp.exp(dmat - m_t)
        w_inter = jnp.exp(inter - m_t)
        a = w_intra * _nt(qb, kb_)
        num = (jnp.dot(a.astype(BF16), vb_, preferred_element_type=F32)
               + w_inter * _nt(qb, cs.astype(BF16)))
        den = jnp.sum(a, axis=-1, keepdims=True) + w_inter * jnp.sum(q * ns, axis=-1, keepdims=True)
        hh = num / jnp.maximum(jnp.abs(den), jnp.exp(-m_t))

        b_last = b_col[L - 1:L, :]
        g = b_last - b_col + li_col
        m_new = jnp.maximum(b_last + m_prev, jnp.max(g, axis=0, keepdims=True))
        ws = jnp.exp(g - m_new)
        decay = jnp.exp(b_last + m_prev - m_new)
        vwt = _nt(eye, (ws * v).astype(BF16)).astype(BF16)
        c_ref[bb, h] = decay * cs + jnp.dot(vwt, kb_, preferred_element_type=F32)
        n_ref[bb, h:h + 1, :] = decay * ns + jnp.sum(ws * k, axis=0, keepdims=True)
        m_ref[bb, h:h + 1, :] = jnp.broadcast_to(m_new, (1, LANES))

        gate = jax.nn.sigmoid(co_ref[bb, :, cols])
        hs_ref[bb, :, cols] = (_rms(hh, hg_ref[...]) * gate).astype(hs_ref.dtype)


def _mlstm(cqk3, cvo3, cif3, conv0, conv_w, conv_b, gate_b, head_g, c0, n0, m0, L):
    b, t, _ = cqk3.shape
    nc = t // L
    nb = MLSTM_NB
    assert b % nb == 0
    blk = lambda col, w: pl.BlockSpec((nb, L, w), lambda bi, c: (bi, c, col))
    const = lambda shape: pl.BlockSpec(shape, lambda bi, c: (0,) * len(shape))
    per_b = lambda shape: pl.BlockSpec((nb,) + shape, lambda bi, c: (bi,) + (0,) * len(shape))
    return pl.pallas_call(
        functools.partial(_mlstm_kernel, L=L),
        out_shape=(jax.ShapeDtypeStruct((b, t, D_C), BF16),
                   jax.ShapeDtypeStruct((b, H_C, DH_C, DH_C), F32),
                   jax.ShapeDtypeStruct((b, H_C, DH_C), F32),
                   jax.ShapeDtypeStruct((b, H_C, LANES), F32)),
        grid=(b // nb, nc),
        in_specs=[
            blk(0, 2 * D_C), blk(0, D_C), blk(1, D_C), blk(0, LANES),
            per_b((CONV_PAD, 2 * D_C)), const((CONV_W, 2 * D_C)), const((1, 2 * D_C)),
            const((1, LANES)), const((1, DH_C)),
            per_b((H_C, DH_C, DH_C)), per_b((H_C, DH_C)), per_b((H_C, LANES)),
        ],
        out_specs=(pl.BlockSpec((nb, L, D_C), lambda bi, c: (bi, c, 0)),
                   per_b((H_C, DH_C, DH_C)), per_b((H_C, DH_C)), per_b((H_C, LANES))),
        scratch_shapes=[pltpu.VMEM((nb, CONV_PAD + L, 2 * D_C), F32)],
        compiler_params=_cparams(("parallel", "arbitrary")),
        name="mlstm",
    )(cqk3, cvo3, cvo3, cif3, conv0, conv_w, conv_b, gate_b, head_g, c0, n0, m0)


def _mix_cross_kernel(oa_ref, ob_ref, oc_ref, ga_ref, gb_ref, gc_ref, x_ref, wa_ref, wb_ref, wc_ref, wp_ref,
                      g_ref, wq_ref, mk_ref, mv_ref, wo_ref, o_ref):
    nm = mk_ref.shape[1]
    mk = mk_ref[0].reshape(nm, H_M * DH_M).astype(BF16)
    mv = mv_ref[0].reshape(nm, H_M * DH_M).astype(BF16)
    head_cols = [slice(hd * DH_M, (hd + 1) * DH_M) for hd in range(H_M)]
    tm = x_ref.shape[1]
    halves = [slice(0, tm // 2), slice(tm // 2, tm)]
    ups = [(jnp.dot(oa_ref[0, r], wa_ref[...], preferred_element_type=F32),
            jnp.dot(ob_ref[0, r], wb_ref[...], preferred_element_type=F32),
            jnp.dot(oc_ref[0, r], wc_ref[...], preferred_element_type=F32)) for r in halves]
    mixed = [(ga_ref[0, r] * ua + gb_ref[0, r] * ub + gc_ref[0, r] * uc).astype(BF16)
             for r, (ua, ub, uc) in zip(halves, ups)]
    xs = [x_ref[0, r] + jnp.dot(mx, wp_ref[...], preferred_element_type=F32) for r, mx in zip(halves, mixed)]
    hs = [_rms(x, g_ref[...]).astype(BF16) for x in xs]
    qs = [(jnp.dot(h, wq_ref[...], preferred_element_type=F32) * (DH_M ** -0.5 * LOG2E)).astype(BF16) for h in hs]
    units = [(i, hd) for hd in range(H_M) for i in range(len(halves))]
    scores, probs, outs = {}, {}, {}
    for step in range(len(units) + 2):
        if step < len(units):
            i, hd = units[step]
            scores[step] = _nt(qs[i][:, head_cols[hd]], mk[:, head_cols[hd]])
        if 0 <= step - 1 < len(units):
            s = scores.pop(step - 1)
            e = jnp.exp2(s - jnp.max(s, axis=-1, keepdims=True))
            probs[step - 1] = (e.astype(BF16), jnp.sum(e, axis=-1, keepdims=True))
        if 0 <= step - 2 < len(units):
            i, hd = units[step - 2]
            e, l = probs.pop(step - 2)
            o = jnp.dot(e, mv[:, head_cols[hd]], preferred_element_type=F32) / l
            outs[(i, hd)] = o.astype(BF16)
    for i, r in enumerate(halves):
        o = jnp.concatenate([outs[(i, hd)] for hd in range(H_M)], axis=1)
        o_ref[0, r] = xs[i] + jnp.dot(o, wo_ref[...], preferred_element_type=F32)


def _mix_cross(oa, ob, oc, gates, x3, l, wa, wb, wc, wp, g, wq, mk, mv, wo):
    b, t, d = x3.shape
    tm = min(t, 512)
    nm = mk.shape[2]
    row = lambda w, col=0: pl.BlockSpec((1, tm, w), lambda bi, i: (bi, i, col))
    full = lambda a: pl.BlockSpec((None,) + a.shape[1:], lambda bi, i: (l, 0, 0))
    mem = pl.BlockSpec((None, 1, nm, H_M, DH_M), lambda bi, i: (l, bi, 0, 0, 0))
    return pl.pallas_call(
        _mix_cross_kernel,
        out_shape=jax.ShapeDtypeStruct((b, t, d), F32),
        grid=(b, t // tm),
        in_specs=[row(oa.shape[2]), row(ob.shape[2]), row(oc.shape[2]),
                  row(d, 0), row(d, 1), row(d, 2), row(d),
                  full(wa), full(wb), full(wc), full(wp),
                  pl.BlockSpec(g.shape, lambda bi, i: (0, 0)), full(wq), mem, mem, full(wo)],
        out_specs=row(d),
        compiler_params=_cparams(("parallel", "parallel")),
        name="mix_cross",
    )(oa, ob, oc, gates, gates, gates, x3, wa, wb, wc, wp, g, wq, mk, mv, wo)


FFN_TF = 1408
FFN_TM = 512


def _ffn_kernel(x_ref, g_ref, wg_ref, wu_ref, wd_ref, gf_ref, o_ref, *, final_norm):
    x = x_ref[...]
    h = _rms(x, g_ref[...]).astype(BF16)
    chunks = [slice(c, c + FFN_TF) for c in range(0, D_FF, FFN_TF)]
    pre = [(jnp.dot(h, wg_ref[:, c], preferred_element_type=F32),
            jnp.dot(h, wu_ref[:, c], preferred_element_type=F32)) for c in chunks]
    y = x
    for c, (gate, up) in zip(chunks, pre):
        a = (gate * jax.nn.sigmoid(gate) * up).astype(BF16)
        y = y + jnp.dot(a, wd_ref[c, :], preferred_element_type=F32)
    if final_norm:
        y = _rms(y, gf_ref[...])
    o_ref[...] = y


def _ffn(x, g, l, wg, wu, wd, g_final, final_norm):
    m, d = x.shape
    tm = min(m, FFN_TM)
    resident = lambda a: pl.BlockSpec((None,) + a.shape[1:], lambda i: (l, 0, 0),
                                      pipeline_mode=pl.Buffered(1))
    return pl.pallas_call(
        functools.partial(_ffn_kernel, final_norm=final_norm),
        out_shape=jax.ShapeDtypeStruct((m, d), F32),
        grid=(m // tm,),
        in_specs=[
            pl.BlockSpec((tm, d), lambda i: (i, 0)),
            pl.BlockSpec((1, d), lambda i: (0, 0)),
            resident(wg), resident(wu), resident(wd),
            pl.BlockSpec((1, d), lambda i: (0, 0)),
        ],
        out_specs=pl.BlockSpec((tm, d), lambda i: (i, 0)),
        compiler_params=_cparams(("parallel",)),
        name="ffn",
    )(x, g, wg, wu, wd, g_final)


def _layer(x3, l, w, mem_k, mem_v, cache, final_norm, akv_all):
    b, t, d = x3.shape
    x2 = x3.reshape(b * t, d)
    row = lambda a: a.reshape(1, -1)
    lam_init = 0.8 - 0.6 * math.exp(-0.3 * l)

    gates, cqk, cif, q, ak_all, av_all, akv, bkv, cvo = _in_proj(
        x2, row(w['g_mix']), l, w['w_in'], w['w_if'], *akv_all, (b, t))
    q3 = q.reshape(b, t, SEG)
    bkv3 = bkv.reshape(b, t, SEG)
    cqk3 = cqk.reshape(b, t, 2 * D_C)
    keep = t if cache is not None else min(BAND, t)
    b_k = bkv3[:, t - keep:, :HALF].reshape(b, keep, H_B, DH_B)
    b_v = bkv3[:, t - keep:, HALF:].reshape(b, keep, H_B, DH_B)

    gate_b = jnp.concatenate([w['b_i'], w['b_f'], jnp.zeros((LANES - 2 * H_C,), F32)]).reshape(1, LANES)
    if cache is None:
        oa = _diff_prompt(q3, akv.reshape(b, t, SEG), w['lqk'], row(w['a_head_g']), lam_init)
        ob = _band_prompt(q3, bkv3, w['relbias'])
        conv0 = jnp.zeros((b, CONV_PAD, 2 * D_C), F32)
        c0 = jnp.zeros((b, H_C, DH_C, DH_C), F32)
        n0 = jnp.zeros((b, H_C, DH_C), F32)
        m0 = jnp.zeros((b, H_C, LANES), F32)
        L = min(t, MLSTM_L)
    else:
        past = cache['a_k'].shape[2]
        ak3 = ak_all.reshape(DEPTH * b, t, HALF)
        av3 = av_all.reshape(DEPTH * b, t, HALF)
        oa = _diff_sample(q3, ak3, av3, l, cache['a_k'], cache['a_v'],
                          w['lqk'], row(w['a_head_g']), lam_init)
        nband = cache['b_k'].shape[1]
        ob = _band_sample(q3, bkv3, cache['b_k'].reshape(b, nband, -1), cache['b_v'].reshape(b, nband, -1),
                          w['relbias'], past)
        conv0 = jnp.pad(cache['conv'], ((0, 0), (CONV_PAD - (CONV_W - 1), 0), (0, 0)))
        c0, n0 = cache['C'], cache['n']
        m0 = jnp.broadcast_to(cache['m'][:, :, None], (b, H_C, LANES))
        L = t
    oc, c_new, n_new, m_new = _mlstm(cqk3, cvo.reshape(b, t, SEG), cif.reshape(b, t, LANES), conv0,
                                     w['conv_w'], row(w['conv_b']), gate_b, row(w['c_head_g']), c0, n0, m0, L)
    assert t >= CONV_W - 1
    conv_new = cqk3[:, t - (CONV_W - 1):]

    x3 = _mix_cross(oa, ob, oc, gates.reshape(b, t, 3 * d), x3, l,
                    w['w_up_a'], w['w_up_b'], w['w_up_c'], w['w_o'],
                    row(w['g_cross']), w['w_mq'], mem_k, mem_v, w['w_mo'])
    x2 = _ffn(x3.reshape(b * t, d), row(w['g_ffn']), l, w['w_ff_g'], w['w_ff_u'], w['w_ff_d'],
              row(w['g_final']), final_norm)
    return x2.reshape(b, t, d), (ak_all, av_all), (b_k, b_v, c_new, n_new, m_new[:, :, 0], conv_new)


def kernel(x_prompt, x_sample, cache_a_k, cache_a_v, cache_b_k, cache_b_v, state_c_C, state_c_n, state_c_m, state_c_conv, cache_mem_k, cache_mem_v, mem_prompt, g_mix, w_in, a_lq1, a_lk1, a_lq2, a_lk2, a_head_g, b_rel, c_conv_w, c_conv_b, c_b_i, c_b_f, c_head_g, w_up_a, w_up_b, w_up_c, w_o, g_cross, w_mq, w_mk, w_mv, w_mo, g_ffn, w_ff_g, w_ff_u, w_ff_d, g_final):
    xp, xs = x_prompt, x_sample
    bp = x_prompt.shape[0]
    new_p = [[] for _ in range(6)]
    new_s = [[] for _ in range(6)]
    akv_p = akv_s = (None, None)
    bf = lambda a: a.astype(BF16)
    wp, wif = _prep_w_in(w_in)
    wts = dict(w_up_a=bf(w_up_a), w_up_b=bf(w_up_b), w_up_c=bf(w_up_c), w_o=bf(w_o), w_mq=bf(w_mq),
               w_mo=bf(w_mo), w_ff_g=bf(w_ff_g), w_ff_u=bf(w_ff_u), w_ff_d=bf(w_ff_d))
    mem_k = _mem_proj(mem_prompt, bf(w_mk))
    mem_v = _mem_proj(mem_prompt, bf(w_mv))
    for l in range(DEPTH):
        w = dict(g_mix=g_mix[l], w_in=wp, w_if=wif,
                 lqk=jnp.stack([a_lq1[l], a_lk1[l], a_lq2[l], a_lk2[l]]),
                 a_head_g=a_head_g[l], relbias=_relbias(b_rel[l]), conv_w=c_conv_w[l], conv_b=c_conv_b[l],
                 b_i=c_b_i[l], b_f=c_b_f[l], c_head_g=c_head_g[l],
                 g_cross=g_cross[l], g_ffn=g_ffn[l], g_final=g_final, **wts)
        last = l == DEPTH - 1
        xp, akv_p, st_p = _layer(xp, l, w, mem_k, mem_v, None, last, akv_p)
        cache = dict(a_k=cache_a_k, a_v=cache_a_v, b_k=cache_b_k[l], b_v=cache_b_v[l],
                     C=state_c_C[l], n=state_c_n[l], m=state_c_m[l], conv=state_c_conv[l])
        xs, akv_s, st_s = _layer(xs, l, w, cache_mem_k, cache_mem_v, cache, last, akv_s)
        for lst, a in zip(new_p, st_p):
            lst.append(a)
        for lst, a in zip(new_s, st_s):
            lst.append(a)
    heads = lambda a, b, t: a.reshape(DEPTH, b, t, H_A, DV_A)
    outs_p = ([heads(a, bp, x_prompt.shape[1]) for a in akv_p] + [jnp.stack(a, 0) for a in new_p]
              + [mem_k, mem_v])
    outs_s = [heads(a, xs.shape[0], xs.shape[1]) for a in akv_s] + [jnp.stack(a, 0) for a in new_s]
    return (xp, xs) + tuple(outs_p) + tuple(outs_s)
```
